```python
import math
import jax, jax.numpy as jnp
from jax import lax
import numpy as np

D_MODEL = 1024
BATCH = 8
SEQ = 2048
DEPTH = 1
DEC_BATCH = 128
DEC_SEQ = 1
PAST_LEN = 16384
PAGE_SIZE = 128

D_MIX = D_MODEL
D_SSM = D_MIX // 2
D_GMLP = D_MIX - D_SSM
SSM_GROUP = 16
N_SSM_GROUPS = D_SSM // SSM_GROUP
SSM_STATE = 64
CHUNK = 128
N_GMLP_HEADS = 4
GMLP_HEAD = D_GMLP // N_GMLP_HEADS
N_EXPERT_GROUPS = 4
EXPERTS_PER_GROUP = 8
N_EXPERTS = N_EXPERT_GROUPS * EXPERTS_PER_GROUP
TOP_K_INNER = 2
D_EXPERT = D_MODEL // 2
D_IN = 2 * D_SSM + 2 * D_GMLP
EPS = 1e-6
DT_MIN = 1e-3
DT_MAX = 1e-1

kernel_name = 'hymba_s5_gmlp_hmoe_step'


def _rmsnorm(x, g):
    xf = x.astype(jnp.float32)
    y = xf * lax.rsqrt(jnp.mean(xf * xf, axis=-1, keepdims=True) + EPS) * g.astype(jnp.float32)
    return y.astype(x.dtype)


def _cplx_affine_combine(e1, e2):
    a1r, a1i, b1r, b1i = e1
    a2r, a2i, b2r, b2i = e2
    ar = a2r * a1r - a2i * a1i
    ai = a2r * a1i + a2i * a1r
    br = a2r * b1r - a2i * b1i + b2r
    bi = a2r * b1i + a2i * b1r + b2i
    return (ar, ai, br, bi)


def _ssm(xa, h0_re, h0_im, lam_re, lam_im, log_dt, b_re, b_im, c_re, c_im, d_skip):
    f32 = jnp.float32
    lam_re = lam_re.astype(f32)
    lam_im = lam_im.astype(f32)
    dt = jnp.exp(log_dt.astype(f32))[:, None]
    mag = jnp.exp(lam_re * dt)
    ang = lam_im * dt
    lb_re = mag * jnp.cos(ang)
    lb_im = mag * jnp.sin(ang)
    den = lam_re * lam_re + lam_im * lam_im
    nr = lb_re - 1.0
    ni = lb_im
    k_re = (nr * lam_re + ni * lam_im) / den
    k_im = (ni * lam_re - nr * lam_im) / den
    b_re = b_re.astype(f32)
    b_im = b_im.astype(f32)
    bb_re = k_re[:, :, None] * b_re - k_im[:, :, None] * b_im
    bb_im = k_re[:, :, None] * b_im + k_im[:, :, None] * b_re
    bu_re = jnp.einsum('nlgh,gph->nlgp', xa, bb_re)
    bu_im = jnp.einsum('nlgh,gph->nlgp', xa, bb_im)
    h0_re = h0_re.astype(f32)
    h0_im = h0_im.astype(f32)
    bu_re = bu_re.at[:, 0].add(lb_re * h0_re - lb_im * h0_im)
    bu_im = bu_im.at[:, 0].add(lb_re * h0_im + lb_im * h0_re)
    a_re = jnp.broadcast_to(lb_re, bu_re.shape)
    a_im = jnp.broadcast_to(lb_im, bu_im.shape)
    _, _, h_re, h_im = lax.associative_scan(_cplx_affine_combine, (a_re, a_im, bu_re, bu_im), axis=1)
    y = (jnp.einsum('nlgp,ghp->nlgh', h_re, c_re.astype(f32))
         - jnp.einsum('nlgp,ghp->nlgh', h_im, c_im.astype(f32))
         + d_skip.astype(f32) * xa)
    return y, h_re[:, -1], h_im[:, -1]


def _spatial_gate(v, w_s, b_s):
    n, l = v.shape[0], v.shape[1]
    n_chunks = -(-l // CHUNK)
    pad = n_chunks * CHUNK - l
    vp = jnp.pad(v, ((0, 0), (0, pad), (0, 0), (0, 0)))
    vp = vp.reshape(n, n_chunks, CHUNK, N_GMLP_HEADS, GMLP_HEAD)
    mask = jnp.tril(jnp.ones((CHUNK, CHUNK), dtype=bool))
    w = jnp.where(mask[None], w_s, jnp.zeros_like(w_s)).astype(v.dtype)
    s = jnp.einsum('hij,ncjhd->ncihd', w, vp) + jnp.transpose(b_s).astype(v.dtype)[None, None, :, :, None]
    return s.reshape(n, n_chunks * CHUNK, N_GMLP_HEADS, GMLP_HEAD)[:, :l]


def _moe(h, p):
    f32 = jnp.float32
    n, l, d = h.shape
    t = h.reshape(n * l, d)
    gl = (t @ p['w_router_group']).astype(f32) + p['b_router_group'].astype(f32)
    pg = jax.nn.softmax(gl, axis=-1)
    gi = jnp.argmax(gl, axis=-1)
    p_top = jnp.take_along_axis(pg, gi[:, None], axis=1)
    el = ((t @ p['w_router_expert']).astype(f32) + p['b_router_expert'].astype(f32))
    el = el.reshape(-1, N_EXPERT_GROUPS, EXPERTS_PER_GROUP)
    el_sel = jnp.take_along_axis(el, gi[:, None, None], axis=1)[:, 0]
    pe = jax.nn.softmax(el_sel, axis=-1)
    wv, wi = lax.top_k(pe, TOP_K_INNER)
    wv = wv / jnp.sum(wv, axis=-1, keepdims=True) * p_top
    eidx = gi[:, None] * EXPERTS_PER_GROUP + wi
    combine = jnp.einsum('tk,tke->te', wv, jax.nn.one_hot(eidx, N_EXPERTS, dtype=f32)).astype(t.dtype)
    out = jnp.zeros_like(t)
    for e in range(N_EXPERTS):
        a = t @ p['w_gate'][e]
        u = t @ p['w_up'][e]
        out = out + combine[:, e:e + 1] * ((jax.nn.silu(a) * u) @ p['w_down'][e])
    return out.reshape(n, l, d)


def _layer(x, h0_re, h0_im, p):
    n, l = x.shape[0], x.shape[1]
    dt = x.dtype
    hn = _rmsnorm(x, p['norm1_g'])
    z = hn @ p['w_in']
    xa, ga, ub, vb = jnp.split(z, [D_SSM, 2 * D_SSM, 2 * D_SSM + D_GMLP], axis=-1)
    xa_f = xa.astype(jnp.float32).reshape(n, l, N_SSM_GROUPS, SSM_GROUP)
    y_a, h_re, h_im = _ssm(xa_f, h0_re, h0_im, p['lam_re'], p['lam_im'], p['log_dt'],
                           p['ssm_b_re'], p['ssm_b_im'], p['ssm_c_re'], p['ssm_c_im'], p['ssm_d'])
    y_a = (jax.nn.gelu(y_a.reshape(n, l, D_SSM), approximate=False)
           * jax.nn.sigmoid(ga.astype(jnp.float32))).astype(dt)
    ub = jax.nn.gelu(ub, approximate=False)
    vb = jax.nn.gelu(vb, approximate=False).reshape(n, l, N_GMLP_HEADS, GMLP_HEAD)
    vb = _rmsnorm(vb, p['gmlp_norm_g'])
    s = _spatial_gate(vb, p['gmlp_w_s'], p['gmlp_b_s'])
    y_b = ub * s.reshape(n, l, D_GMLP)
    mix = jnp.concatenate([_rmsnorm(y_a, p['out_norm_ssm_g']), _rmsnorm(y_b, p['out_norm_gmlp_g'])], axis=-1)
    x = x + mix @ p['w_out']
    x = x + _moe(_rmsnorm(x, p['norm2_g']), p)
    return x, h_re, h_im, vb.reshape(n, l, D_GMLP)


def setup_inputs(seed: int = 0) -> dict:
    key = jax.random.key(seed)
    ks = jax.random.split(key, 32)
    f32 = jnp.float32
    nrm = lambda k, shape, scale: jax.random.normal(k, shape, f32) * scale
    n_idx = jnp.arange(SSM_STATE, dtype=f32)
    inp = {}
    inp['x_prompt'] = nrm(ks[0], (BATCH, SEQ, D_MODEL), 1.0)
    inp['x_sample'] = nrm(ks[1], (DEC_BATCH, DEC_SEQ, D_MODEL), 1.0)
    inp['state_ssm_re'] = nrm(ks[2], (DEPTH, DEC_BATCH, N_SSM_GROUPS, SSM_STATE), 0.5)
    inp['state_ssm_im'] = nrm(ks[3], (DEPTH, DEC_BATCH, N_SSM_GROUPS, SSM_STATE), 0.5)
    inp['norm1_g'] = 1.0 + nrm(ks[4], (DEPTH, D_MODEL), 0.01)
    inp['w_in'] = nrm(ks[5], (DEPTH, D_MODEL, D_IN), D_MODEL ** -0.5)
    inp['lam_re'] = -0.5 + nrm(ks[6], (DEPTH, N_SSM_GROUPS, SSM_STATE), 0.01)
    inp['lam_im'] = math.pi * n_idx + nrm(ks[7], (DEPTH, N_SSM_GROUPS, SSM_STATE), 0.01)
    inp['log_dt'] = jax.random.uniform(ks[8], (DEPTH, N_SSM_GROUPS), f32, math.log(DT_MIN), math.log(DT_MAX))
    inp['ssm_b_re'] = nrm(ks[9], (DEPTH, N_SSM_GROUPS, SSM_STATE, SSM_GROUP), (2 * SSM_GROUP) ** -0.5)
    inp['ssm_b_im'] = nrm(ks[10], (DEPTH, N_SSM_GROUPS, SSM_STATE, SSM_GROUP), (2 * SSM_GROUP) ** -0.5)
    inp['ssm_c_re'] = nrm(ks[11], (DEPTH, N_SSM_GROUPS, SSM_GROUP, SSM_STATE), (2 * SSM_STATE) ** -0.5)
    inp['ssm_c_im'] = nrm(ks[12], (DEPTH, N_SSM_GROUPS, SSM_GROUP, SSM_STATE), (2 * SSM_STATE) ** -0.5)
    inp['ssm_d'] = nrm(ks[13], (DEPTH, N_SSM_GROUPS, SSM_GROUP), 1.0)
    inp['gmlp_norm_g'] = 1.0 + nrm(ks[14], (DEPTH, N_GMLP_HEADS, GMLP_HEAD), 0.01)
    inp['gmlp_w_s'] = nrm(ks[15], (DEPTH, N_GMLP_HEADS, CHUNK, CHUNK), CHUNK ** -0.5)
    inp['gmlp_b_s'] = 1.0 + nrm(ks[16], (DEPTH, N_GMLP_HEADS, CHUNK), 0.1)
    inp['out_norm_ssm_g'] = 1.0 + nrm(ks[17], (DEPTH, D_SSM), 0.01)
    inp['out_norm_gmlp_g'] = 1.0 + nrm(ks[18], (DEPTH, D_GMLP), 0.01)
    inp['w_out'] = nrm(ks[19], (DEPTH, D_MIX, D_MODEL), D_MIX ** -0.5)
    inp['norm2_g'] = 1.0 + nrm(ks[20], (DEPTH, D_MODEL), 0.01)
    inp['w_router_group'] = nrm(ks[21], (DEPTH, D_MODEL, N_EXPERT_GROUPS), D_MODEL ** -0.5)
    inp['b_router_group'] = nrm(ks[22], (DEPTH, N_EXPERT_GROUPS), 0.01)
    inp['w_router_expert'] = nrm(ks[23], (DEPTH, D_MODEL, N_EXPERTS), D_MODEL ** -0.5)
    inp['b_router_expert'] = nrm(ks[24], (DEPTH, N_EXPERTS), 0.01)
    inp['w_gate'] = nrm(ks[25], (DEPTH, N_EXPERTS, D_MODEL, D_EXPERT), D_MODEL ** -0.5)
    inp['w_up'] = nrm(ks[26], (DEPTH, N_EXPERTS, D_MODEL, D_EXPERT), D_MODEL ** -0.5)
    inp['w_down'] = nrm(ks[27], (DEPTH, N_EXPERTS, D_EXPERT, D_MODEL), D_EXPERT ** -0.5)
    inp['final_norm_g'] = 1.0 + nrm(ks[28], (D_MODEL,), 0.01)
    return inp


def reference(x_prompt, x_sample, state_ssm_re, state_ssm_im, norm1_g, w_in, lam_re, lam_im, log_dt,
              ssm_b_re, ssm_b_im, ssm_c_re, ssm_c_im, ssm_d, gmlp_norm_g, gmlp_w_s, gmlp_b_s,
              out_norm_ssm_g, out_norm_gmlp_g, w_out, norm2_g, w_router_group, b_router_group,
              w_router_expert, b_router_expert, w_gate, w_up, w_down, final_norm_g):
    sdt = state_ssm_re.dtype
    xp = x_prompt
    xs = x_sample
    re_p, im_p, re_s, im_s, v_s = [], [], [], [], []
    zeros_state = jnp.zeros((x_prompt.shape[0], N_SSM_GROUPS, SSM_STATE), jnp.float32)
    for i in range(DEPTH):
        p = dict(norm1_g=norm1_g[i], w_in=w_in[i], lam_re=lam_re[i], lam_im=lam_im[i], log_dt=log_dt[i],
                 ssm_b_re=ssm_b_re[i], ssm_b_im=ssm_b_im[i], ssm_c_re=ssm_c_re[i], ssm_c_im=ssm_c_im[i],
                 ssm_d=ssm_d[i], gmlp_norm_g=gmlp_norm_g[i], gmlp_w_s=gmlp_w_s[i], gmlp_b_s=gmlp_b_s[i],
                 out_norm_ssm_g=out_norm_ssm_g[i], out_norm_gmlp_g=out_norm_gmlp_g[i], w_out=w_out[i],
                 norm2_g=norm2_g[i], w_router_group=w_router_group[i], b_router_group=b_router_group[i],
                 w_router_expert=w_router_expert[i], b_router_expert=b_router_expert[i],
                 w_gate=w_gate[i], w_up=w_up[i], w_down=w_down[i])
        xp, hr_p, hi_p, _ = _layer(xp, zeros_state, zeros_state, p)
        xs, hr_s, hi_s, vrows = _layer(xs, state_ssm_re[i], state_ssm_im[i], p)
        re_p.append(hr_p.astype(sdt))
        im_p.append(hi_p.astype(sdt))
        re_s.append(hr_s.astype(sdt))
        im_s.append(hi_s.astype(sdt))
        v_s.append(vrows)
    y_prompt = _rmsnorm(xp, final_norm_g)
    y_sample = _rmsnorm(xs, final_norm_g)
    return (y_prompt, y_sample, jnp.stack(re_p), jnp.stack(im_p), jnp.stack(re_s), jnp.stack(im_s), jnp.stack(v_s))
```

```python
import functools
import math

import jax
import jax.numpy as jnp
from jax import lax
from jax.experimental import pallas as pl
from jax.experimental.pallas import tpu as pltpu

D_MODEL = 1024
D_SSM = 512
D_GMLP = 512
SSM_GROUP = 16
N_SSM_GROUPS = 32
SSM_STATE = 64
CHUNK = 128
N_GMLP_HEADS = 4
GMLP_HEAD = 128
N_EXPERT_GROUPS = 4
EXPERTS_PER_GROUP = 8
N_EXPERTS = 32
D_EXPERT = 512
D_IN = 2048
EPS = 1e-6

LANES = 128
SUBLANES = 8
N_LANE_TILES = D_SSM // LANES
STATE_COLS = N_SSM_GROUPS * SSM_STATE
TILE_STATE = STATE_COLS // N_LANE_TILES
VMEM_LIMIT = 56 * 1024 * 1024

FRONT_TL = 512
SSM_LC = 64
TOK_TM = 256
EXP_TM = 256

_INV_SQRT2 = 1.0 / math.sqrt(2.0)
_BF16 = jnp.bfloat16
_F32 = jnp.float32


def _gelu(x):
    return 0.5 * x * (1.0 + lax.erf(x * _INV_SQRT2))


def _rms(x, g):
    return x * lax.rsqrt(jnp.mean(x * x, axis=-1, keepdims=True) + EPS) * g


def _dot(a, b):
    return jnp.dot(a, b, preferred_element_type=_F32)


def _head_norm_gelu(vb, gn):
    v = _gelu(vb)
    parts = []
    for h in range(N_GMLP_HEADS):
        vh = v[:, h * GMLP_HEAD:(h + 1) * GMLP_HEAD]
        parts.append(vh * lax.rsqrt(jnp.mean(vh * vh, axis=-1, keepdims=True) + EPS))
    return jnp.concatenate(parts, axis=-1) * gn


def _front_prompt_kernel(x_ref, g1_ref, win_ref, gn_ref, ws_ref, bs_ref, gog_ref,
                         xa_ref, sg_ref, mixb_ref):
    x = x_ref[0]
    hn = _rms(x, g1_ref[...]).astype(_BF16)
    z = _dot(hn, win_ref[...])
    xa_ref[...] = z[:, :D_SSM]
    sg_ref[...] = jax.nn.sigmoid(z[:, D_SSM:2 * D_SSM])
    ub = _gelu(z[:, 2 * D_SSM:2 * D_SSM + D_GMLP])
    vbn = _head_norm_gelu(z[:, 2 * D_SSM + D_GMLP:], gn_ref[...]).astype(_BF16)
    tl = x.shape[0]
    rows = []
    for c in range(tl // CHUNK):
        heads = []
        for h in range(N_GMLP_HEADS):
            vh = vbn[c * CHUNK:(c + 1) * CHUNK, h * GMLP_HEAD:(h + 1) * GMLP_HEAD]
            heads.append(_dot(ws_ref[h], vh) + bs_ref[:, h:h + 1])
        rows.append(jnp.concatenate(heads, axis=-1))
    s = jnp.concatenate(rows, axis=0)
    mixb_ref[0] = _rms(ub * s, gog_ref[...])


def _front_prompt(x, g1, win_bf, gn, ws_tril_bf, bs_t, gog):
    n, l, d = x.shape
    tl = FRONT_TL
    grid = (n, l // tl)
    const = lambda *shape: pl.BlockSpec(shape, lambda b, i: (0,) * len(shape))
    tb_spec = pl.BlockSpec((tl, D_SSM), lambda b, i: (i, b))
    return pl.pallas_call(
        _front_prompt_kernel,
        grid=grid,
        in_specs=[
            pl.BlockSpec((1, tl, d), lambda b, i: (b, i, 0)),
            const(1, d), const(d, D_IN), const(1, D_GMLP),
            const(N_GMLP_HEADS, CHUNK, CHUNK), const(CHUNK, N_GMLP_HEADS), const(1, D_GMLP),
        ],
        out_specs=[tb_spec, tb_spec, pl.BlockSpec((1, tl, D_GMLP), lambda b, i: (b, i, 0))],
        out_shape=[
            jax.ShapeDtypeStruct((l, n * D_SSM), _F32),
            jax.ShapeDtypeStruct((l, n * D_SSM), _F32),
            jax.ShapeDtypeStruct((n, l, D_GMLP), _F32),
        ],
        compiler_params=pltpu.CompilerParams(
            dimension_semantics=("parallel", "parallel"), vmem_limit_bytes=VMEM_LIMIT),
        name="front_prompt",
    )(x, g1, win_bf, gn, ws_tril_bf, bs_t, gog)


def _ssm_prompt_kernel(xa_ref, sg_ref, wb_ref, wc_ref, lbr_ref, lbi_ref, dsk_ref, gos_ref,
                       mixa_ref, hfin_ref, bu_ref, st_ref):
    lc = xa_ref.shape[0]
    rows = lc * SUBLANES

    @pl.when(pl.program_id(0) == 0)
    def _():
        st_ref[...] = jnp.zeros_like(st_ref)

    xa = xa_ref[...].reshape(rows, D_SSM)
    xa_bf = xa.astype(_BF16)
    for k in range(N_LANE_TILES):
        bu_ref[:, 2 * TILE_STATE * k:2 * TILE_STATE * (k + 1)] = _dot(
            xa_bf[:, k * LANES:(k + 1) * LANES], wb_ref[k])

    for k in range(N_LANE_TILES):
        c_re = 2 * TILE_STATE * k
        c_im = c_re + TILE_STATE
        lr = jnp.broadcast_to(lbr_ref[:, k * TILE_STATE:(k + 1) * TILE_STATE], (SUBLANES, TILE_STATE))
        li = jnp.broadcast_to(lbi_ref[:, k * TILE_STATE:(k + 1) * TILE_STATE], (SUBLANES, TILE_STATE))

        def body(t, carry, c_re=c_re, c_im=c_im, lr=lr, li=li):
            hr, hi = carry
            r0 = pl.multiple_of(t * SUBLANES, SUBLANES)
            nr = lr * hr - li * hi + bu_ref[pl.ds(r0, SUBLANES), c_re:c_re + TILE_STATE]
            ni = lr * hi + li * hr + bu_ref[pl.ds(r0, SUBLANES), c_im:c_im + TILE_STATE]
            bu_ref[pl.ds(r0, SUBLANES), c_re:c_re + TILE_STATE] = nr
            bu_ref[pl.ds(r0, SUBLANES), c_im:c_im + TILE_STATE] = ni
            return nr, ni

        hr, hi = lax.fori_loop(
            0, lc, body, (st_ref[:, c_re:c_re + TILE_STATE], st_ref[:, c_im:c_im + TILE_STATE]), unroll=2)
        st_ref[:, c_re:c_re + TILE_STATE] = hr
        st_ref[:, c_im:c_im + TILE_STATE] = hi

    ys = []
    for k in range(N_LANE_TILES):
        hk = bu_ref[:, 2 * TILE_STATE * k:2 * TILE_STATE * (k + 1)].astype(_BF16)
        ys.append(_dot(hk, wc_ref[k]))
    y = jnp.concatenate(ys, axis=-1) + dsk_ref[...] * xa
    ya = _gelu(y) * sg_ref[...].reshape(rows, D_SSM)
    mixa_ref[...] = _rms(ya, gos_ref[...]).reshape(lc, SUBLANES, D_SSM)
    hfin_ref[...] = st_ref[...]


def _ssm_prompt(xa_tb, sg_tb, wb, wc, lbr, lbi, dsk, gos):
    l, n, _ = xa_tb.shape
    lc = SSM_LC
    const = lambda *shape: pl.BlockSpec(shape, lambda i: (0,) * len(shape))
    tb_spec = pl.BlockSpec((lc, n, D_SSM), lambda i: (i, 0, 0))
    return pl.pallas_call(
        _ssm_prompt_kernel,
        grid=(l // lc,),
        in_specs=[tb_spec, tb_spec,
                  const(N_LANE_TILES, LANES, 2 * TILE_STATE), const(N_LANE_TILES, 2 * TILE_STATE, LANES),
                  const(1, STATE_COLS), const(1, STATE_COLS), const(1, D_SSM), const(1, D_SSM)],
        out_specs=[tb_spec, const(n, 2 * STATE_COLS)],
        out_shape=[jax.ShapeDtypeStruct((l, n, D_SSM), _F32),
                   jax.ShapeDtypeStruct((n, 2 * STATE_COLS), _F32)],
        scratch_shapes=[pltpu.VMEM((lc * n, 2 * STATE_COLS), _F32),
                        pltpu.VMEM((n, 2 * STATE_COLS), _F32)],
        compiler_params=pltpu.CompilerParams(
            dimension_semantics=("arbitrary",), vmem_limit_bytes=VMEM_LIMIT),
        name="ssm_prompt",
    )(xa_tb, sg_tb, wb, wc, lbr, lbi, dsk, gos)


def _front_sample_kernel(x_ref, g1_ref, win_ref, gn_ref, w00_ref, b0_ref, gog_ref,
                         wb_ref, wc_ref, lbr_ref, lbi_ref, dsk_ref, gos_ref, h0r_ref, h0i_ref,
                         mix_ref, hr_ref, hi_ref, vrow_ref):
    x = x_ref[...]
    hn = _rms(x, g1_ref[...]).astype(_BF16)
    z = _dot(hn, win_ref[...])
    xa = z[:, :D_SSM]
    xa_bf = xa.astype(_BF16)
    ys = []
    for k in range(N_LANE_TILES):
        bu = _dot(xa_bf[:, k * LANES:(k + 1) * LANES], wb_ref[k])
        sl = slice(k * TILE_STATE, (k + 1) * TILE_STATE)
        lr, li = lbr_ref[:, sl], lbi_ref[:, sl]
        h0r, h0i = h0r_ref[:, sl], h0i_ref[:, sl]
        nr = lr * h0r - li * h0i + bu[:, :TILE_STATE]
        ni = lr * h0i + li * h0r + bu[:, TILE_STATE:]
        hr_ref[:, sl] = nr
        hi_ref[:, sl] = ni
        ys.append(_dot(jnp.concatenate([nr, ni], axis=-1).astype(_BF16), wc_ref[k]))
    y = jnp.concatenate(ys, axis=-1) + dsk_ref[...] * xa
    ya = _gelu(y) * jax.nn.sigmoid(z[:, D_SSM:2 * D_SSM])
    mix_ref[:, :D_SSM] = _rms(ya, gos_ref[...])
    ub = _gelu(z[:, 2 * D_SSM:2 * D_SSM + D_GMLP])
    vbn = _head_norm_gelu(z[:, 2 * D_SSM + D_GMLP:], gn_ref[...])
    vrow_ref[...] = vbn
    s = w00_ref[...] * vbn + b0_ref[...]
    mix_ref[:, D_SSM:] = _rms(ub * s, gog_ref[...])


def _front_sample(x, g1, win_bf, gn, w00, b0, gog, wb, wc, lbr, lbi, dsk, gos, h0r, h0i):
    n = x.shape[0]
    vmem = pl.BlockSpec(memory_space=pltpu.VMEM)
    return pl.pallas_call(
        _front_sample_kernel,
        in_specs=[vmem] * 15,
        out_specs=[vmem] * 4,
        out_shape=[jax.ShapeDtypeStruct((n, D_MODEL), _F32),
                   jax.ShapeDtypeStruct((n, STATE_COLS), _F32),
                   jax.ShapeDtypeStruct((n, STATE_COLS), _F32),
                   jax.ShapeDtypeStruct((n, D_GMLP), _F32)],
        compiler_params=pltpu.CompilerParams(vmem_limit_bytes=VMEM_LIMIT),
        name="front_sample",
    )(x, g1, win_bf, gn, w00, b0, gog, wb, wc, lbr, lbi, dsk, gos, h0r, h0i)


def _route(logits):
    lane = lax.broadcasted_iota(jnp.int32, logits.shape, 1).astype(_F32)
    neg = jnp.float32(-jnp.inf)
    big = jnp.float32(LANES)
    is_g = (lane >= N_EXPERTS) & (lane < N_EXPERTS + N_EXPERT_GROUPS)
    gl = jnp.where(is_g, logits, neg)
    gmax = jnp.max(gl, axis=-1, keepdims=True)
    gi = jnp.min(jnp.where(is_g & (logits == gmax), lane, big), axis=-1, keepdims=True) - N_EXPERTS
    p_top = 1.0 / jnp.sum(jnp.where(is_g, jnp.exp(gl - gmax), 0.0), axis=-1, keepdims=True)
    lo = gi * EXPERTS_PER_GROUP
    in_grp = (lane >= lo) & (lane < lo + EXPERTS_PER_GROUP)
    m1 = jnp.max(jnp.where(in_grp, logits, neg), axis=-1, keepdims=True)
    i1 = jnp.min(jnp.where(in_grp & (logits == m1), lane, big), axis=-1, keepdims=True)
    rest = in_grp & (lane != i1)
    m2 = jnp.max(jnp.where(rest, logits, neg), axis=-1, keepdims=True)
    i2 = jnp.min(jnp.where(rest & (logits == m2), lane, big), axis=-1, keepdims=True)
    e2 = jnp.exp(m2 - m1)
    w1 = p_top / (1.0 + e2)
    w2 = p_top * e2 / (1.0 + e2)
    out = jnp.where(lane == 0, i1, 0.0)
    out = jnp.where(lane == 1, i2, out)
    out = jnp.where(lane == 2, w1, out)
    out = jnp.where(lane == 3, w2, out)
    return out


def _mixer_out_body(x, mixa, mixb, wo_ref, g2_ref, wr_ref, br_ref):
    x1 = x + _dot(mixa.astype(_BF16), wo_ref[:D_SSM, :]) + _dot(mixb.astype(_BF16), wo_ref[D_SSM:, :])
    xn = _rms(x1, g2_ref[...]).astype(_BF16)
    logits = _dot(xn, wr_ref[...]) + br_ref[...]
    return x1, xn, _route(logits)


def _mixer_out_prompt_kernel(x_ref, mixa_ref, mixb_ref, wo_ref, g2_ref, wr_ref, br_ref,
                             x1_ref, xn_ref, route_ref):
    x1, xn, route = _mixer_out_body(x_ref[0], mixa_ref[...], mixb_ref[0], wo_ref, g2_ref, wr_ref, br_ref)
    x1_ref[...] = x1
    xn_ref[...] = xn
    route_ref[...] = route


def _mixer_out_sample_kernel(x_ref, mix_ref, wo_ref, g2_ref, wr_ref, br_ref, x1_in, xn_in, route_in,
                             x1_ref, xn_ref, route_ref):
    del x1_in, xn_in, route_in
    x1, xn, route = _mixer_out_body(x_ref[...], mix_ref[:, :D_SSM], mix_ref[:, D_SSM:],
                                    wo_ref, g2_ref, wr_ref, br_ref)
    x1_ref[...] = x1
    xn_ref[...] = xn
    route_ref[...] = route


def _mixer_out(x_p, mixa_tb, mixb, x_s, mix_s, wo_bf, g2, wr_bf, br):
    n, l, d = x_p.shape
    ns = x_s.shape[0]
    t_all = n * l + ns
    tm = TOK_TM
    per_seq = l // tm
    const = lambda *shape: pl.BlockSpec(shape, lambda b, i: (0,) * len(shape))
    tok = lambda w: pl.BlockSpec((tm, w), lambda b, i: (b * per_seq + i, 0))
    out_shape = [jax.ShapeDtypeStruct((t_all, d), _F32),
                 jax.ShapeDtypeStruct((t_all, d), _BF16),
                 jax.ShapeDtypeStruct((t_all, LANES), _F32)]
    x1, xn, route = pl.pallas_call(
        _mixer_out_prompt_kernel,
        grid=(n, per_seq),
        in_specs=[pl.BlockSpec((1, tm, d), lambda b, i: (b, i, 0)),
                  pl.BlockSpec((tm, D_SSM), lambda b, i: (i, b)),
                  pl.BlockSpec((1, tm, D_GMLP), lambda b, i: (b, i, 0)),
                  const(d, d), const(1, d), const(d, LANES), const(1, LANES)],
        out_specs=[tok(d), tok(d), tok(LANES)],
        out_shape=out_shape,
        compiler_params=pltpu.CompilerParams(
            dimension_semantics=("parallel", "parallel"), vmem_limit_bytes=VMEM_LIMIT),
        name="mixer_out_prompt",
    )(x_p, mixa_tb, mixb, wo_bf, g2, wr_bf, br)
    tail = (n * l) // ns
    c1 = lambda *shape: pl.BlockSpec(shape, lambda i: (0,) * len(shape))
    anyspec = pl.BlockSpec(memory_space=pl.ANY)
    tail_spec = lambda w: pl.BlockSpec((ns, w), lambda i: (tail, 0))
    return pl.pallas_call(
        _mixer_out_sample_kernel,
        grid=(1,),
        in_specs=[c1(ns, d), c1(ns, d), c1(d, d), c1(1, d), c1(d, LANES), c1(1, LANES),
                  anyspec, anyspec, anyspec],
        out_specs=[tail_spec(d), tail_spec(d), tail_spec(LANES)],
        out_shape=out_shape,
        input_output_aliases={6: 0, 7: 1, 8: 2},
        compiler_params=pltpu.CompilerParams(
            dimension_semantics=("arbitrary",), vmem_limit_bytes=VMEM_LIMIT),
        name="mixer_out_sample",
    )(x_s, mix_s, wo_bf, g2, wr_bf, br, x1, xn, route)


def _experts_kernel(tile_e_ref, tile_first_ref, n_tiles_ref, xs_ref, wg_ref, wu_ref, wd_ref,
                    y_ref, wg_bf, wu_bf, wd_bf):
    i = pl.program_id(0)

    @pl.when(tile_first_ref[i] == 1)
    def _():
        wg_bf[...] = wg_ref[0].astype(_BF16)
        wu_bf[...] = wu_ref[0].astype(_BF16)
        wd_bf[...] = wd_ref[0].astype(_BF16)

    @pl.when(i < n_tiles_ref[0])
    def _():
        x = xs_ref[...]
        a = _dot(x, wg_bf[...])
        u = _dot(x, wu_bf[...])
        h = (a * jax.nn.sigmoid(a) * u).astype(_BF16)
        y_ref[...] = _dot(h, wd_bf[...])

    @pl.when(i >= n_tiles_ref[0])
    def _():
        y_ref[...] = jnp.zeros_like(y_ref)


def _experts(tile_e, tile_first, n_tiles, xs, w_gate, w_up, w_down):
    p, d = xs.shape
    tm = EXP_TM
    grid_spec = pltpu.PrefetchScalarGridSpec(
        num_scalar_prefetch=3,
        grid=(p // tm,),
        in_specs=[pl.BlockSpec((tm, d), lambda i, te, tf, nt: (i, 0)),
                  pl.BlockSpec((1, d, D_EXPERT), lambda i, te, tf, nt: (te[i], 0, 0)),
                  pl.BlockSpec((1, d, D_EXPERT), lambda i, te, tf, nt: (te[i], 0, 0)),
                  pl.BlockSpec((1, D_EXPERT, d), lambda i, te, tf, nt: (te[i], 0, 0))],
        out_specs=pl.BlockSpec((tm, d), lambda i, te, tf, nt: (i, 0)),
        scratch_shapes=[pltpu.VMEM((d, D_EXPERT), _BF16), pltpu.VMEM((d, D_EXPERT), _BF16),
                        pltpu.VMEM((D_EXPERT, d), _BF16)],
    )
    return pl.pallas_call(
        _experts_kernel,
        grid_spec=grid_spec,
        out_shape=jax.ShapeDtypeStruct((p, d), _F32),
        compiler_params=pltpu.CompilerParams(
            dimension_semantics=("arbitrary",), vmem_limit_bytes=VMEM_LIMIT),
        name="experts",
    )(tile_e, tile_first, n_tiles, xs, w_gate, w_up, w_down)


def _final_kernel(x1_ref, ya_ref, yb_ref, route_ref, gf_ref, y_ref):
    route = route_ref[...]
    x2 = x1_ref[...] + route[:, 2:3] * ya_ref[...] + route[:, 3:4] * yb_ref[...]
    y_ref[...] = _rms(x2, gf_ref[...])


def _final(x1, ya, yb, route, gf):
    t_all, d = x1.shape
    tm = 128
    tok = lambda w: pl.BlockSpec((tm, w), lambda i: (i, 0))
    return pl.pallas_call(
        _final_kernel,
        grid=(t_all // tm,),
        in_specs=[tok(d), tok(d), tok(d), tok(LANES), pl.BlockSpec((1, d), lambda i: (0, 0))],
        out_specs=tok(d),
        out_shape=jax.ShapeDtypeStruct((t_all, d), _F32),
        compiler_params=pltpu.CompilerParams(
            dimension_semantics=("parallel",), vmem_limit_bytes=VMEM_LIMIT),
        name="final",
    )(x1, ya, yb, route, gf)


def _ssm_params(lam_re, lam_im, log_dt, b_re, b_im, c_re, c_im, d_skip):
    dt = jnp.exp(log_dt)[:, None]
    mag = jnp.exp(lam_re * dt)
    ang = lam_im * dt
    lb_re = mag * jnp.cos(ang)
    lb_im = mag * jnp.sin(ang)
    den = lam_re * lam_re + lam_im * lam_im
    nr = lb_re - 1.0
    ni = lb_im
    k_re = (nr * lam_re + ni * lam_im) / den
    k_im = (ni * lam_re - nr * lam_im) / den
    bb_re = k_re[:, :, None] * b_re - k_im[:, :, None] * b_im
    bb_im = k_re[:, :, None] * b_im + k_im[:, :, None] * b_re
    eye = jnp.eye(SUBLANES, dtype=_F32)

    def b_blocks(bb):
        t = jnp.transpose(bb, (0, 2, 1)).reshape(N_LANE_TILES, 8, SSM_GROUP, SSM_STATE)
        return jnp.einsum('kahp,ab->kahbp', t, eye).reshape(N_LANE_TILES, LANES, TILE_STATE)

    def c_blocks(c):
        t = c.reshape(N_LANE_TILES, 8, SSM_GROUP, SSM_STATE)
        return jnp.einsum('kahp,ab->kapbh', t, eye).reshape(N_LANE_TILES, TILE_STATE, LANES)

    wb = jnp.concatenate([b_blocks(bb_re), b_blocks(bb_im)], axis=-1).astype(_BF16)
    wc = jnp.concatenate([c_blocks(c_re), -c_blocks(c_im)], axis=1).astype(_BF16)
    return (wb, wc, lb_re.reshape(1, STATE_COLS), lb_im.reshape(1, STATE_COLS),
            d_skip.reshape(1, D_SSM))


def _dispatch(route, tm):
    t_all = route.shape[0]
    e_flat = route[:, :2].astype(jnp.int32).reshape(-1)
    onehot = (e_flat[:, None] == jnp.arange(N_EXPERTS, dtype=jnp.int32)[None, :]).astype(jnp.int32)
    csum = jnp.cumsum(onehot, axis=0)
    rank = jnp.take_along_axis(csum, e_flat[:, None], axis=1)[:, 0] - 1
    counts = csum[-1]
    tiles_per = (counts + tm - 1) // tm
    tile_end = jnp.cumsum(tiles_per)
    tile_start = tile_end - tiles_per
    dest = tile_start[e_flat] * tm + rank
    n_tiles_max = (2 * t_all + N_EXPERTS * (tm - 1)) // tm
    p = n_tiles_max * tm
    src_tok = jnp.zeros((p,), jnp.int32).at[dest].set(jnp.arange(2 * t_all, dtype=jnp.int32) // 2)
    tile_ids = jnp.arange(n_tiles_max, dtype=jnp.int32)
    tile_e = jnp.minimum(jnp.searchsorted(tile_end, tile_ids, side='right'), N_EXPERTS - 1).astype(jnp.int32)
    tile_first = jnp.concatenate([jnp.ones((1,), jnp.int32), (tile_e[1:] != tile_e[:-1]).astype(jnp.int32)])
    n_tiles = tile_end[-1:].astype(jnp.int32)
    return src_tok, dest.reshape(t_all, 2), tile_e, tile_first, n_tiles


def kernel(x_prompt, x_sample, state_ssm_re, state_ssm_im, norm1_g, w_in, lam_re, lam_im, log_dt, ssm_b_re, ssm_b_im, ssm_c_re, ssm_c_im, ssm_d, gmlp_norm_g, gmlp_w_s, gmlp_b_s, out_norm_ssm_g, out_norm_gmlp_g, w_out, norm2_g, w_router_group, b_router_group, w_router_expert, b_router_expert, w_gate, w_up, w_down, final_norm_g):
    n, l, d = x_prompt.shape
    ns = x_sample.shape[0]
    li = 0
    g1 = norm1_g[li].reshape(1, d)
    win_bf = w_in[li].astype(_BF16)
    gn = gmlp_norm_g[li].reshape(1, D_GMLP)
    tril = jnp.tril(jnp.ones((CHUNK, CHUNK), dtype=bool))
    ws_tril = jnp.where(tril[None], gmlp_w_s[li], 0.0)
    bs = gmlp_b_s[li]
    gog = out_norm_gmlp_g[li].reshape(1, D_GMLP)
    gos = out_norm_ssm_g[li].reshape(1, D_SSM)
    wb, wc, lbr, lbi, dsk = _ssm_params(lam_re[li], lam_im[li], log_dt[li], ssm_b_re[li], ssm_b_im[li],
                                        ssm_c_re[li], ssm_c_im[li], ssm_d[li])
    wo_bf = w_out[li].astype(_BF16)
    g2 = norm2_g[li].reshape(1, d)
    pad = LANES - N_EXPERTS - N_EXPERT_GROUPS
    wr_bf = jnp.concatenate([w_router_expert[li], w_router_group[li], jnp.zeros((d, pad), _F32)],
                            axis=1).astype(_BF16)
    br = jnp.concatenate([b_router_expert[li], b_router_group[li], jnp.zeros((pad,), _F32)]).reshape(1, LANES)

    xa_tb, sg_tb, mixb = _front_prompt(x_prompt, g1, win_bf, gn, ws_tril.astype(_BF16), bs.T, gog)
    mixa_tb, hfin = _ssm_prompt(xa_tb.reshape(l, n, D_SSM), sg_tb.reshape(l, n, D_SSM),
                                wb, wc, lbr, lbi, dsk, gos)
    w00 = jnp.repeat(ws_tril[:, 0, 0], GMLP_HEAD).reshape(1, D_GMLP)
    b0 = jnp.repeat(bs[:, 0], GMLP_HEAD).reshape(1, D_GMLP)
    mix_s, hr_s, hi_s, vrow = _front_sample(
        x_sample.reshape(ns, d), g1, win_bf, gn, w00, b0, gog, wb, wc, lbr, lbi, dsk, gos,
        state_ssm_re[li].reshape(ns, STATE_COLS), state_ssm_im[li].reshape(ns, STATE_COLS))

    x1, xn, route = _mixer_out(x_prompt, mixa_tb.reshape(l, n * D_SSM), mixb, x_sample.reshape(ns, d),
                               mix_s, wo_bf, g2, wr_bf, br)
    src_tok, dest, tile_e, tile_first, n_tiles = _dispatch(route, EXP_TM)
    xs = jnp.take(xn, src_tok, axis=0)
    ys = _experts(tile_e, tile_first, n_tiles, xs, w_gate[li], w_up[li], w_down[li])
    ya = jnp.take(ys, dest[:, 0], axis=0)
    yb = jnp.take(ys, dest[:, 1], axis=0)
    y = _final(x1, ya, yb, route, final_norm_g.reshape(1, d))

    y_prompt = y[:n * l].reshape(n, l, d)
    y_sample = y[n * l:].reshape(ns, 1, d)
    hf = hfin.reshape(n, N_LANE_TILES, 2, 8, SSM_STATE)
    re_p = hf[:, :, 0].reshape(1, n, N_SSM_GROUPS, SSM_STATE)
    im_p = hf[:, :, 1].reshape(1, n, N_SSM_GROUPS, SSM_STATE)
    re_s = hr_s.reshape(1, ns, N_SSM_GROUPS, SSM_STATE)
    im_s = hi_s.reshape(1, ns, N_SSM_GROUPS, SSM_STATE)
    return (y_prompt, y_sample, re_p, im_p, re_s, im_s, vrow.reshape(1, ns, 1, D_GMLP))
```

```python
import math

import jax
import jax.numpy as jnp
from jax import lax
from jax.experimental import pallas as pl
from jax.experimental.pallas import tpu as pltpu

D_MODEL = 1024
D_SSM = 512
D_GMLP = 512
SSM_GROUP = 16
N_SSM_GROUPS = 32
SSM_STATE = 64
CHUNK = 128
N_GMLP_HEADS = 4
GMLP_HEAD = 128
N_EXPERT_GROUPS = 4
EXPERTS_PER_GROUP = 8
N_EXPERTS = 32
D_EXPERT = 512
D_IN = 2048
EPS = 1e-6

LANES = 128
SUBLANES = 8
N_LANE_TILES = D_SSM // LANES
STATE_COLS = N_SSM_GROUPS * SSM_STATE
TILE_STATE = STATE_COLS // N_LANE_TILES
VMEM_LIMIT = 56 * 1024 * 1024

FRONT_TL = 512
SSM_LC = 64
TOK_TM = 256
FINAL_TM = 512
EXP_TM = 256

R_E1, R_E2, R_W1, R_W2, R_RANK1, R_RANK2 = 0, 1, 2, 3, 4, 5

_INV_SQRT2 = 1.0 / math.sqrt(2.0)
_BF16 = jnp.bfloat16
_F32 = jnp.float32


def _gelu(x):
    return 0.5 * x * (1.0 + lax.erf(x * _INV_SQRT2))


def _rms(x, g):
    return x * lax.rsqrt(jnp.mean(x * x, axis=-1, keepdims=True) + EPS) * g


def _dot(a, b):
    return jnp.dot(a, b, preferred_element_type=_F32)


def _dot_f32(a, b):
    return jnp.dot(a, b, preferred_element_type=_F32, precision=lax.Precision.HIGHEST)


def _head_norm_gelu(vb, gn):
    v = _gelu(vb)
    parts = []
    for h in range(N_GMLP_HEADS):
        vh = v[:, h * GMLP_HEAD:(h + 1) * GMLP_HEAD]
        parts.append(vh * lax.rsqrt(jnp.mean(vh * vh, axis=-1, keepdims=True) + EPS))
    return jnp.concatenate(parts, axis=-1) * gn


def _front_prompt_kernel(x_ref, g1_ref, win_ref, gn_ref, ws_ref, bs_ref, gog_ref,
                         xa_ref, sg_ref, mixb_ref):
    x = x_ref[0]
    hn = _rms(x, g1_ref[...]).astype(_BF16)
    z = _dot(hn, win_ref[...])
    xa_ref[...] = z[:, :D_SSM]
    sg_ref[...] = jax.nn.sigmoid(z[:, D_SSM:2 * D_SSM])
    ub = _gelu(z[:, 2 * D_SSM:2 * D_SSM + D_GMLP])
    vbn = _head_norm_gelu(z[:, 2 * D_SSM + D_GMLP:], gn_ref[...]).astype(_BF16)
    tl = x.shape[0]
    rows = []
    for c in range(tl // CHUNK):
        heads = []
        for h in range(N_GMLP_HEADS):
            vh = vbn[c * CHUNK:(c + 1) * CHUNK, h * GMLP_HEAD:(h + 1) * GMLP_HEAD]
            heads.append(_dot(ws_ref[h], vh) + bs_ref[:, h:h + 1])
        rows.append(jnp.concatenate(heads, axis=-1))
    s = jnp.concatenate(rows, axis=0)
    mixb_ref[0] = _rms(ub * s, gog_ref[...])


def _front_prompt(x, g1, win_bf, gn, ws_tril_bf, bs_t, gog):
    n, l, d = x.shape
    tl = FRONT_TL
    grid = (n, l // tl)
    const = lambda *shape: pl.BlockSpec(shape, lambda b, i: (0,) * len(shape))
    tb_spec = pl.BlockSpec((tl, D_SSM), lambda b, i: (i, b))
    return pl.pallas_call(
        _front_prompt_kernel,
        grid=grid,
        in_specs=[
            pl.BlockSpec((1, tl, d), lambda b, i: (b, i, 0)),
            const(1, d), const(d, D_IN), const(1, D_GMLP),
            const(N_GMLP_HEADS, CHUNK, CHUNK), const(CHUNK, N_GMLP_HEADS), const(1, D_GMLP),
        ],
        out_specs=[tb_spec, tb_spec, pl.BlockSpec((1, tl, D_GMLP), lambda b, i: (b, i, 0))],
        out_shape=[
            jax.ShapeDtypeStruct((l, n * D_SSM), _F32),
            jax.ShapeDtypeStruct((l, n * D_SSM), _F32),
            jax.ShapeDtypeStruct((n, l, D_GMLP), _F32),
        ],
        compiler_params=pltpu.CompilerParams(
            dimension_semantics=("parallel", "parallel"), vmem_limit_bytes=VMEM_LIMIT),
        name="front_prompt",
    )(x, g1, win_bf, gn, ws_tril_bf, bs_t, gog)


def _ssm_prompt_kernel(xa_ref, sg_ref, wb_ref, wc_ref, lbr_ref, lbi_ref, dsk_ref, gos_ref,
                       mixa_ref, hfin_ref, bu_ref, st_ref):
    lc = xa_ref.shape[0]
    rows = lc * SUBLANES

    @pl.when(pl.program_id(0) == 0)
    def _():
        st_ref[...] = jnp.zeros_like(st_ref)

    xa = xa_ref[...].reshape(rows, D_SSM)
    xa_bf = xa.astype(_BF16)
    for k in range(N_LANE_TILES):
        bu_ref[:, 2 * TILE_STATE * k:2 * TILE_STATE * (k + 1)] = _dot(
            xa_bf[:, k * LANES:(k + 1) * LANES], wb_ref[k])

    for k in range(N_LANE_TILES):
        c_re = 2 * TILE_STATE * k
        c_im = c_re + TILE_STATE
        lr = jnp.broadcast_to(lbr_ref[:, k * TILE_STATE:(k + 1) * TILE_STATE], (SUBLANES, TILE_STATE))
        li = jnp.broadcast_to(lbi_ref[:, k * TILE_STATE:(k + 1) * TILE_STATE], (SUBLANES, TILE_STATE))

        def body(t, carry, c_re=c_re, c_im=c_im, lr=lr, li=li):
            hr, hi = carry
            r0 = pl.multiple_of(t * SUBLANES, SUBLANES)
            nr = lr * hr - li * hi + bu_ref[pl.ds(r0, SUBLANES), c_re:c_re + TILE_STATE]
            ni = lr * hi + li * hr + bu_ref[pl.ds(r0, SUBLANES), c_im:c_im + TILE_STATE]
            bu_ref[pl.ds(r0, SUBLANES), c_re:c_re + TILE_STATE] = nr
            bu_ref[pl.ds(r0, SUBLANES), c_im:c_im + TILE_STATE] = ni
            return nr, ni

        hr, hi = lax.fori_loop(
            0, lc, body, (st_ref[:, c_re:c_re + TILE_STATE], st_ref[:, c_im:c_im + TILE_STATE]), unroll=2)
        st_ref[:, c_re:c_re + TILE_STATE] = hr
        st_ref[:, c_im:c_im + TILE_STATE] = hi

    ys = []
    for k in range(N_LANE_TILES):
        hk = bu_ref[:, 2 * TILE_STATE * k:2 * TILE_STATE * (k + 1)].astype(_BF16)
        ys.append(_dot(hk, wc_ref[k]))
    y = jnp.concatenate(ys, axis=-1) + dsk_ref[...] * xa
    ya = _gelu(y) * sg_ref[...].reshape(rows, D_SSM)
    mixa_ref[...] = _rms(ya, gos_ref[...]).reshape(lc, SUBLANES, D_SSM)
    hfin_ref[...] = st_ref[...]


def _ssm_prompt(xa_tb, sg_tb, wb, wc, lbr, lbi, dsk, gos):
    l, n, _ = xa_tb.shape
    lc = SSM_LC
    const = lambda *shape: pl.BlockSpec(shape, lambda i: (0,) * len(shape))
    tb_spec = pl.BlockSpec((lc, n, D_SSM), lambda i: (i, 0, 0))
    return pl.pallas_call(
        _ssm_prompt_kernel,
        grid=(l // lc,),
        in_specs=[tb_spec, tb_spec,
                  const(N_LANE_TILES, LANES, 2 * TILE_STATE), const(N_LANE_TILES, 2 * TILE_STATE, LANES),
                  const(1, STATE_COLS), const(1, STATE_COLS), const(1, D_SSM), const(1, D_SSM)],
        out_specs=[tb_spec, const(n, 2 * STATE_COLS)],
        out_shape=[jax.ShapeDtypeStruct((l, n, D_SSM), _F32),
                   jax.ShapeDtypeStruct((n, 2 * STATE_COLS), _F32)],
        scratch_shapes=[pltpu.VMEM((lc * n, 2 * STATE_COLS), _F32),
                        pltpu.VMEM((n, 2 * STATE_COLS), _F32)],
        compiler_params=pltpu.CompilerParams(
            dimension_semantics=("arbitrary",), vmem_limit_bytes=VMEM_LIMIT),
        name="ssm_prompt",
    )(xa_tb, sg_tb, wb, wc, lbr, lbi, dsk, gos)


def _front_sample_kernel(x_ref, g1_ref, win_ref, gn_ref, w00_ref, b0_ref, gog_ref,
                         wb_ref, wc_ref, lbr_ref, lbi_ref, dsk_ref, gos_ref, h0r_ref, h0i_ref,
                         mix_ref, hr_ref, hi_ref, vrow_ref):
    x = x_ref[...]
    hn = _rms(x, g1_ref[...])
    z = _dot_f32(hn, win_ref[...])
    xa = z[:, :D_SSM]
    ys = []
    for k in range(N_LANE_TILES):
        bu = _dot_f32(xa[:, k * LANES:(k + 1) * LANES], wb_ref[k])
        sl = slice(k * TILE_STATE, (k + 1) * TILE_STATE)
        lr, li = lbr_ref[:, sl], lbi_ref[:, sl]
        h0r, h0i = h0r_ref[:, sl], h0i_ref[:, sl]
        nr = lr * h0r - li * h0i + bu[:, :TILE_STATE]
        ni = lr * h0i + li * h0r + bu[:, TILE_STATE:]
        hr_ref[:, sl] = nr
        hi_ref[:, sl] = ni
        ys.append(_dot_f32(jnp.concatenate([nr, ni], axis=-1), wc_ref[k]))
    y = jnp.concatenate(ys, axis=-1) + dsk_ref[...] * xa
    ya = _gelu(y) * jax.nn.sigmoid(z[:, D_SSM:2 * D_SSM])
    mix_ref[:, :D_SSM] = _rms(ya, gos_ref[...])
    ub = _gelu(z[:, 2 * D_SSM:2 * D_SSM + D_GMLP])
    vbn = _head_norm_gelu(z[:, 2 * D_SSM + D_GMLP:], gn_ref[...])
    vrow_ref[...] = vbn
    s = w00_ref[...] * vbn + b0_ref[...]
    mix_ref[:, D_SSM:] = _rms(ub * s, gog_ref[...])


def _front_sample(x, g1, win, gn, w00, b0, gog, wb, wc, lbr, lbi, dsk, gos, h0r, h0i):
    n = x.shape[0]
    vmem = pl.BlockSpec(memory_space=pltpu.VMEM)
    return pl.pallas_call(
        _front_sample_kernel,
        in_specs=[vmem] * 15,
        out_specs=[vmem] * 4,
        out_shape=[jax.ShapeDtypeStruct((n, D_MODEL), _F32),
                   jax.ShapeDtypeStruct((n, STATE_COLS), _F32),
                   jax.ShapeDtypeStruct((n, STATE_COLS), _F32),
                   jax.ShapeDtypeStruct((n, D_GMLP), _F32)],
        compiler_params=pltpu.CompilerParams(vmem_limit_bytes=VMEM_LIMIT),
        name="front_sample",
    )(x, g1, win, gn, w00, b0, gog, wb, wc, lbr, lbi, dsk, gos, h0r, h0i)


def _route(logits, base):
    tm = logits.shape[0]
    lane = lax.broadcasted_iota(jnp.int32, logits.shape, 1).astype(_F32)
    neg = jnp.float32(-jnp.inf)
    big = jnp.float32(LANES)
    is_g = (lane >= N_EXPERTS) & (lane < N_EXPERTS + N_EXPERT_GROUPS)
    gl = jnp.where(is_g, logits, neg)
    gmax = jnp.max(gl, axis=-1, keepdims=True)
    gi = jnp.min(jnp.where(is_g & (logits == gmax), lane, big), axis=-1, keepdims=True) - N_EXPERTS
    p_top = 1.0 / jnp.sum(jnp.where(is_g, jnp.exp(gl - gmax), 0.0), axis=-1, keepdims=True)
    lo = gi * EXPERTS_PER_GROUP
    in_grp = (lane >= lo) & (lane < lo + EXPERTS_PER_GROUP)
    m1 = jnp.max(jnp.where(in_grp, logits, neg), axis=-1, keepdims=True)
    i1 = jnp.min(jnp.where(in_grp & (logits == m1), lane, big), axis=-1, keepdims=True)
    rest = in_grp & (lane != i1)
    m2 = jnp.max(jnp.where(rest, logits, neg), axis=-1, keepdims=True)
    i2 = jnp.min(jnp.where(rest & (logits == m2), lane, big), axis=-1, keepdims=True)
    e2 = jnp.exp(m2 - m1)
    w1 = p_top / (1.0 + e2)
    w2 = p_top * e2 / (1.0 + e2)
    sel1 = lane == i1
    sel2 = lane == i2
    hits = jnp.where(sel1 | sel2, 1.0, 0.0)
    r_id = lax.broadcasted_iota(jnp.int32, (tm, tm), 0)
    c_id = lax.broadcasted_iota(jnp.int32, (tm, tm), 1)
    ltri = jnp.where(c_id < r_id, 1.0, 0.0).astype(_BF16)
    before = _dot(ltri, hits.astype(_BF16)) + base
    rank1 = jnp.sum(jnp.where(sel1, before, 0.0), axis=-1, keepdims=True)
    rank2 = jnp.sum(jnp.where(sel2, before, 0.0), axis=-1, keepdims=True)
    out = jnp.where(lane == R_E1, i1, 0.0)
    out = jnp.where(lane == R_E2, i2, out)
    out = jnp.where(lane == R_W1, w1, out)
    out = jnp.where(lane == R_W2, w2, out)
    out = jnp.where(lane == R_RANK1, rank1, out)
    out = jnp.where(lane == R_RANK2, rank2, out)
    return out, base + jnp.sum(hits, axis=0, keepdims=True)


def _mixer_out_prompt_kernel(x_ref, mixa_ref, mixb_ref, wo_ref, g2_ref, wr_ref, br_ref,
                             x1_ref, xn_ref, route_ref, cnt_ref, base_ref):
    @pl.when((pl.program_id(0) == 0) & (pl.program_id(1) == 0))
    def _():
        base_ref[...] = jnp.zeros_like(base_ref)

    x1 = (x_ref[0] + _dot(mixa_ref[...].astype(_BF16), wo_ref[:D_SSM, :])
          + _dot(mixb_ref[0].astype(_BF16), wo_ref[D_SSM:, :]))
    xn = _rms(x1, g2_ref[...])
    logits = _dot(xn.astype(_BF16), wr_ref[...]) + br_ref[...]
    route, base = _route(logits, base_ref[...])
    x1_ref[...] = x1
    xn_ref[...] = xn
    route_ref[...] = route
    base_ref[...] = base
    cnt_ref[...] = base


def _mixer_out_sample_kernel(x_ref, mix_ref, wo_ref, g2_ref, wr_ref, br_ref, cnt_in_ref,
                             x1_in, xn_in, route_in, x1_ref, xn_ref, route_ref, cnt_ref):
    del x1_in, xn_in, route_in
    x1 = (x_ref[...] + _dot_f32(mix_ref[:, :D_SSM], wo_ref[:D_SSM, :])
          + _dot_f32(mix_ref[:, D_SSM:], wo_ref[D_SSM:, :]))
    xn = _rms(x1, g2_ref[...])
    logits = _dot_f32(xn, wr_ref[...]) + br_ref[...]
    route, base = _route(logits, cnt_in_ref[...])
    x1_ref[...] = x1
    xn_ref[...] = xn
    route_ref[...] = route
    cnt_ref[...] = base


def _mixer_out(x_p, mixa_tb, mixb, x_s, mix_s, wo, g2, wr, br):
    n, l, d = x_p.shape
    ns = x_s.shape[0]
    t_all = n * l + ns
    tm = TOK_TM
    per_seq = l // tm
    const = lambda *shape: pl.BlockSpec(shape, lambda b, i: (0,) * len(shape))
    tok = lambda w: pl.BlockSpec((tm, w), lambda b, i: (b * per_seq + i, 0))
    tok_shapes = [jax.ShapeDtypeStruct((t_all, d), _F32),
                  jax.ShapeDtypeStruct((t_all, d), _F32),
                  jax.ShapeDtypeStruct((t_all, LANES), _F32)]
    cnt_shape = jax.ShapeDtypeStruct((1, LANES), _F32)
    x1, xn, route, cnt = pl.pallas_call(
        _mixer_out_prompt_kernel,
        grid=(n, per_seq),
        in_specs=[pl.BlockSpec((1, tm, d), lambda b, i: (b, i, 0)),
                  pl.BlockSpec((tm, D_SSM), lambda b, i: (i, b)),
                  pl.BlockSpec((1, tm, D_GMLP), lambda b, i: (b, i, 0)),
                  const(d, d), const(1, d), const(d, LANES), const(1, LANES)],
        out_specs=[tok(d), tok(d), tok(LANES), const(1, LANES)],
        out_shape=tok_shapes + [cnt_shape],
        scratch_shapes=[pltpu.VMEM((1, LANES), _F32)],
        compiler_params=pltpu.CompilerParams(
            dimension_semantics=("arbitrary", "arbitrary"), vmem_limit_bytes=VMEM_LIMIT),
        name="mixer_out_prompt",
    )(x_p, mixa_tb, mixb, wo.astype(_BF16), g2, wr.astype(_BF16), br)
    tail = (n * l) // ns
    c1 = lambda *shape: pl.BlockSpec(shape, lambda i: (0,) * len(shape))
    anyspec = pl.BlockSpec(memory_space=pl.ANY)
    tail_spec = lambda w: pl.BlockSpec((ns, w), lambda i: (tail, 0))
    return pl.pallas_call(
        _mixer_out_sample_kernel,
        grid=(1,),
        in_specs=[c1(ns, d), c1(ns, d), c1(d, d), c1(1, d), c1(d, LANES), c1(1, LANES), c1(1, LANES),
                  anyspec, anyspec, anyspec],
        out_specs=[tail_spec(d), tail_spec(d), tail_spec(LANES), c1(1, LANES)],
        out_shape=tok_shapes + [cnt_shape],
        input_output_aliases={7: 0, 8: 1, 9: 2},
        compiler_params=pltpu.CompilerParams(
            dimension_semantics=("arbitrary",), vmem_limit_bytes=VMEM_LIMIT),
        name="mixer_out_sample",
    )(x_s, mix_s, wo, g2, wr, br, cnt, x1, xn, route)


def _experts_kernel(tile_e_ref, tile_first_ref, n_tiles_ref, xs_ref, wg_ref, wu_ref, wd_ref,
                    y_ref, wg_bf, wu_bf, wd_bf):
    i = pl.program_id(0)

    @pl.when(tile_first_ref[i] == 1)
    def _():
        wg_bf[...] = wg_ref[0].astype(_BF16)
        wu_bf[...] = wu_ref[0].astype(_BF16)
        wd_bf[...] = wd_ref[0].astype(_BF16)

    @pl.when(i < n_tiles_ref[0])
    def _():
        x = xs_ref[...].astype(_BF16)
        a = _dot(x, wg_bf[...])
        u = _dot(x, wu_bf[...])
        h = (a * jax.nn.sigmoid(a) * u).astype(_BF16)
        y_ref[...] = _dot(h, wd_bf[...])

    @pl.when(i >= n_tiles_ref[0])
    def _():
        y_ref[...] = jnp.zeros_like(y_ref)


def _experts(tile_e, tile_first, n_tiles, xs, w_gate, w_up, w_down):
    p, d = xs.shape
    tm = EXP_TM
    grid_spec = pltpu.PrefetchScalarGridSpec(
        num_scalar_prefetch=3,
        grid=(p // tm,),
        in_specs=[pl.BlockSpec((tm, d), lambda i, te, tf, nt: (i, 0)),
                  pl.BlockSpec((1, d, D_EXPERT), lambda i, te, tf, nt: (te[i], 0, 0)),
                  pl.BlockSpec((1, d, D_EXPERT), lambda i, te, tf, nt: (te[i], 0, 0)),
                  pl.BlockSpec((1, D_EXPERT, d), lambda i, te, tf, nt: (te[i], 0, 0))],
        out_specs=pl.BlockSpec((tm, d), lambda i, te, tf, nt: (i, 0)),
        scratch_shapes=[pltpu.VMEM((d, D_EXPERT), _BF16), pltpu.VMEM((d, D_EXPERT), _BF16),
                        pltpu.VMEM((D_EXPERT, d), _BF16)],
    )
    return pl.pallas_call(
        _experts_kernel,
        grid_spec=grid_spec,
        out_shape=jax.ShapeDtypeStruct((p, d), _F32),
        compiler_params=pltpu.CompilerParams(
            dimension_semantics=("arbitrary",), vmem_limit_bytes=VMEM_LIMIT),
        name="experts",
    )(tile_e, tile_first, n_tiles, xs, w_gate, w_up, w_down)


def _final_kernel(x1_ref, yab_ref, route_ref, gf_ref, y_ref):
    route = route_ref[...]
    d = x1_ref.shape[1]
    x2 = (x1_ref[...] + route[:, R_W1:R_W1 + 1] * yab_ref[:, :d]
          + route[:, R_W2:R_W2 + 1] * yab_ref[:, d:])
    y_ref[...] = _rms(x2, gf_ref[...])


def _final(x1, yab, route, gf, n_prompt, n_sample):
    d = x1.shape[1]

    def call(tm, first_block, n_rows, name):
        tok = lambda w: pl.BlockSpec((tm, w), lambda i: (first_block + i, 0))
        return pl.pallas_call(
            _final_kernel,
            grid=(n_rows // tm,),
            in_specs=[tok(d), tok(2 * d), tok(LANES), pl.BlockSpec((1, d), lambda i: (0, 0))],
            out_specs=pl.BlockSpec((tm, d), lambda i: (i, 0)),
            out_shape=jax.ShapeDtypeStruct((n_rows, d), _F32),
            compiler_params=pltpu.CompilerParams(
                dimension_semantics=("parallel",), vmem_limit_bytes=VMEM_LIMIT),
            name=name,
        )(x1, yab, route, gf)

    return (call(FINAL_TM, 0, n_prompt, "final_prompt"),
            call(n_sample, n_prompt // n_sample, n_sample, "final_sample"))


def _ssm_params(lam_re, lam_im, log_dt, b_re, b_im, c_re, c_im, d_skip):
    dt = jnp.exp(log_dt)[:, None]
    mag = jnp.exp(lam_re * dt)
    ang = lam_im * dt
    lb_re = mag * jnp.cos(ang)
    lb_im = mag * jnp.sin(ang)
    den = lam_re * lam_re + lam_im * lam_im
    nr = lb_re - 1.0
    ni = lb_im
    k_re = (nr * lam_re + ni * lam_im) / den
    k_im = (ni * lam_re - nr * lam_im) / den
    bb_re = k_re[:, :, None] * b_re - k_im[:, :, None] * b_im
    bb_im = k_re[:, :, None] * b_im + k_im[:, :, None] * b_re
    eye = jnp.eye(SUBLANES, dtype=_F32)

    def b_blocks(bb):
        t = jnp.transpose(bb, (0, 2, 1)).reshape(N_LANE_TILES, 8, SSM_GROUP, SSM_STATE)
        return jnp.einsum('kahp,ab->kahbp', t, eye).reshape(N_LANE_TILES, LANES, TILE_STATE)

    def c_blocks(c):
        t = c.reshape(N_LANE_TILES, 8, SSM_GROUP, SSM_STATE)
        return jnp.einsum('kahp,ab->kapbh', t, eye).reshape(N_LANE_TILES, TILE_STATE, LANES)

    wb = jnp.concatenate([b_blocks(bb_re), b_blocks(bb_im)], axis=-1)
    wc = jnp.concatenate([c_blocks(c_re), -c_blocks(c_im)], axis=1)
    return (wb, wc, lb_re.reshape(1, STATE_COLS), lb_im.reshape(1, STATE_COLS),
            d_skip.reshape(1, D_SSM))


def _dispatch(route, cnt, tm):
    t_all = route.shape[0]
    e = route[:, R_E1:R_E2 + 1].astype(jnp.int32)
    rank = route[:, R_RANK1:R_RANK2 + 1].astype(jnp.int32)
    counts = cnt[0, :N_EXPERTS].astype(jnp.int32)
    tiles_per = (counts + tm - 1) // tm
    tile_end = jnp.cumsum(tiles_per)
    start_row = (tile_end - tiles_per) * tm
    ids = jnp.arange(N_EXPERTS, dtype=jnp.int32)
    dest = jnp.sum(jnp.where(e[:, :, None] == ids, start_row, 0), axis=-1) + rank
    n_tiles_max = (2 * t_all + N_EXPERTS * (tm - 1)) // tm
    n_tiles_max += n_tiles_max % 2
    p = n_tiles_max * tm
    tok = jnp.broadcast_to(jnp.arange(t_all, dtype=jnp.int32)[:, None], (t_all, 2))
    src_tok = jnp.zeros((p,), jnp.int32).at[dest.reshape(-1)].set(tok.reshape(-1), unique_indices=True)
    tile_ids = jnp.arange(n_tiles_max, dtype=jnp.int32)
    tile_e = jnp.minimum(jnp.sum((tile_ids[:, None] >= tile_end[None, :]).astype(jnp.int32), axis=1),
                         N_EXPERTS - 1)
    tile_first = jnp.concatenate([jnp.ones((1,), jnp.int32), (tile_e[1:] != tile_e[:-1]).astype(jnp.int32)])
    n_tiles = tile_end[-1:].astype(jnp.int32)
    return src_tok, dest.reshape(-1), tile_e, tile_first, n_tiles


def kernel(x_prompt, x_sample, state_ssm_re, state_ssm_im, norm1_g, w_in, lam_re, lam_im, log_dt, ssm_b_re, ssm_b_im, ssm_c_re, ssm_c_im, ssm_d, gmlp_norm_g, gmlp_w_s, gmlp_b_s, out_norm_ssm_g, out_norm_gmlp_g, w_out, norm2_g, w_router_group, b_router_group, w_router_expert, b_router_expert, w_gate, w_up, w_down, final_norm_g):
    n, l, d = x_prompt.shape
    ns = x_sample.shape[0]
    li = 0
    g1 = norm1_g[li].reshape(1, d)
    gn = gmlp_norm_g[li].reshape(1, D_GMLP)
    tril = jnp.tril(jnp.ones((CHUNK, CHUNK), dtype=bool))
    ws_tril = jnp.where(tril[None], gmlp_w_s[li], 0.0)
    bs = gmlp_b_s[li]
    gog = out_norm_gmlp_g[li].reshape(1, D_GMLP)
    gos = out_norm_ssm_g[li].reshape(1, D_SSM)
    wb, wc, lbr, lbi, dsk = _ssm_params(lam_re[li], lam_im[li], log_dt[li], ssm_b_re[li], ssm_b_im[li],
                                        ssm_c_re[li], ssm_c_im[li], ssm_d[li])
    g2 = norm2_g[li].reshape(1, d)
    pad = LANES - N_EXPERTS - N_EXPERT_GROUPS
    wr = jnp.concatenate([w_router_expert[li], w_router_group[li], jnp.zeros((d, pad), _F32)], axis=1)
    br = jnp.concatenate([b_router_expert[li], b_router_group[li], jnp.zeros((pad,), _F32)]).reshape(1, LANES)

    xa_tb, sg_tb, mixb = _front_prompt(x_prompt, g1, w_in[li].astype(_BF16), gn, ws_tril.astype(_BF16),
                                       bs.T, gog)
    mixa_tb, hfin = _ssm_prompt(xa_tb.reshape(l, n, D_SSM), sg_tb.reshape(l, n, D_SSM),
                                wb.astype(_BF16), wc.astype(_BF16), lbr, lbi, dsk, gos)
    w00 = jnp.repeat(ws_tril[:, 0, 0], GMLP_HEAD).reshape(1, D_GMLP)
    b0 = jnp.repeat(bs[:, 0], GMLP_HEAD).reshape(1, D_GMLP)
    mix_s, hr_s, hi_s, vrow = _front_sample(
        x_sample.reshape(ns, d), g1, w_in[li], gn, w00, b0, gog, wb, wc, lbr, lbi, dsk, gos,
        state_ssm_re[li].reshape(ns, STATE_COLS), state_ssm_im[li].reshape(ns, STATE_COLS))

    x1, xn, route, cnt = _mixer_out(x_prompt, mixa_tb.reshape(l, n * D_SSM), mixb, x_sample.reshape(ns, d),
                                    mix_s, w_out[li], g2, wr, br)
    src_tok, dest, tile_e, tile_first, n_tiles = _dispatch(route, cnt, EXP_TM)
    xs = jnp.take(xn, src_tok, axis=0, mode='clip')
    ys = _experts(tile_e, tile_first, n_tiles, xs, w_gate[li], w_up[li], w_down[li])
    yab = jnp.take(ys, dest, axis=0, mode='clip').reshape(n * l + ns, 2 * d)
    y_p, y_s = _final(x1, yab, route, final_norm_g.reshape(1, d), n * l, ns)

    hf = hfin.reshape(n, N_LANE_TILES, 2, 8, SSM_STATE)
    re_p = hf[:, :, 0].reshape(1, n, N_SSM_GROUPS, SSM_STATE)
    im_p = hf[:, :, 1].reshape(1, n, N_SSM_GROUPS, SSM_STATE)
    re_s = hr_s.reshape(1, ns, N_SSM_GROUPS, SSM_STATE)
    im_s = hi_s.reshape(1, ns, N_SSM_GROUPS, SSM_STATE)
    return (y_p.reshape(n, l, d), y_s.reshape(ns, 1, d), re_p, im_p, re_s, im_s,
            vrow.reshape(1, ns, 1, D_GMLP))
```

```python
import math

import jax
import jax.numpy as jnp
from jax import lax
from jax.experimental import pallas as pl
from jax.experimental.pallas import tpu as pltpu
from jax.experimental.pallas import tpu_sc as plsc

D_MODEL = 1024
D_SSM = 512
D_GMLP = 512
SSM_GROUP = 16
N_SSM_GROUPS = 32
SSM_STATE = 64
CHUNK = 128
N_GMLP_HEADS = 4
GMLP_HEAD = 128
N_EXPERT_GROUPS = 4
EXPERTS_PER_GROUP = 8
N_EXPERTS = 32
D_EXPERT = 512
D_IN = 2048
EPS = 1e-6

LANES = 128
SUBLANES = 8
N_LANE_TILES = D_SSM // LANES
STATE_COLS = N_SSM_GROUPS * SSM_STATE
TILE_STATE = STATE_COLS // N_LANE_TILES
VMEM_LIMIT = 56 * 1024 * 1024

SC_CORES = 2
SC_SUBCORES = 16
SC_LANES = 16
SC_WORKERS = SC_CORES * SC_SUBCORES

FRONT_TL = 512
SSM_LC = 64
TOK_TM = 256
FINAL_TM = 512
EXP_TM = 256
DISPATCH_CHUNK = 64
COMBINE_CHUNK = 24

R_E1, R_E2, R_W1, R_W2, R_RANK1, R_RANK2 = 0, 1, 2, 3, 4, 5

_INV_SQRT2 = 1.0 / math.sqrt(2.0)
_BF16 = jnp.bfloat16
_F32 = jnp.float32
_U32 = jnp.uint32


def _gelu(x):
    return 0.5 * x * (1.0 + lax.erf(x * _INV_SQRT2))


def _rms(x, g):
    return x * lax.rsqrt(jnp.mean(x * x, axis=-1, keepdims=True) + EPS) * g


def _dot(a, b):
    return jnp.dot(a, b, preferred_element_type=_F32)


def _dot_f32(a, b):
    return jnp.dot(a, b, preferred_element_type=_F32, precision=lax.Precision.HIGHEST)


def _pack_bf16_pair(x):
    w = x.shape[1] // 2
    hi = lax.bitcast_convert_type(x[:, :w].astype(_BF16).astype(_F32), _U32)
    lo = lax.bitcast_convert_type(x[:, w:].astype(_BF16).astype(_F32), _U32)
    return hi | (lo >> 16)


def _unpack_bf16_pair(p):
    hi = lax.bitcast_convert_type(p & jnp.uint32(0xFFFF0000), _F32)
    lo = lax.bitcast_convert_type(p << 16, _F32)
    return jnp.concatenate([hi, lo], axis=-1)


def _head_norm_gelu(vb, gn):
    v = _gelu(vb)
    parts = []
    for h in range(N_GMLP_HEADS):
        vh = v[:, h * GMLP_HEAD:(h + 1) * GMLP_HEAD]
        parts.append(vh * lax.rsqrt(jnp.mean(vh * vh, axis=-1, keepdims=True) + EPS))
    return jnp.concatenate(parts, axis=-1) * gn


def _front_prompt_kernel(x_ref, g1_ref, win_ref, gn_ref, ws_ref, bs_ref, gog_ref,
                         xa_ref, sg_ref, mixb_ref):
    x = x_ref[0]
    hn = _rms(x, g1_ref[...]).astype(_BF16)
    z = _dot(hn, win_ref[...])
    xa_ref[0] = z[:, :D_SSM]
    sg_ref[0] = jax.nn.sigmoid(z[:, D_SSM:2 * D_SSM])
    ub = _gelu(z[:, 2 * D_SSM:2 * D_SSM + D_GMLP])
    vbn = _head_norm_gelu(z[:, 2 * D_SSM + D_GMLP:], gn_ref[...]).astype(_BF16)
    tl = x.shape[0]
    rows = []
    for c in range(tl // CHUNK):
        heads = []
        for h in range(N_GMLP_HEADS):
            vh = vbn[c * CHUNK:(c + 1) * CHUNK, h * GMLP_HEAD:(h + 1) * GMLP_HEAD]
            heads.append(_dot(ws_ref[h], vh) + bs_ref[:, h:h + 1])
        rows.append(jnp.concatenate(heads, axis=-1))
    s = jnp.concatenate(rows, axis=0)
    mixb_ref[0] = _rms(ub * s, gog_ref[...]).astype(_BF16)


def _front_prompt(x, g1, win_bf, gn, ws_tril_bf, bs_t, gog):
    n, l, d = x.shape
    tl = FRONT_TL
    grid = (n, l // tl)
    const = lambda *shape: pl.BlockSpec(shape, lambda b, i: (0,) * len(shape))
    seq = lambda w: pl.BlockSpec((1, tl, w), lambda b, i: (b, i, 0))
    return pl.pallas_call(
        _front_prompt_kernel,
        grid=grid,
        in_specs=[seq(d), const(1, d), const(d, D_IN), const(1, D_GMLP),
                  const(N_GMLP_HEADS, CHUNK, CHUNK), const(CHUNK, N_GMLP_HEADS), const(1, D_GMLP)],
        out_specs=[seq(D_SSM), seq(D_SSM), seq(D_GMLP)],
        out_shape=[jax.ShapeDtypeStruct((n, l, D_SSM), _F32),
                   jax.ShapeDtypeStruct((n, l, D_SSM), _F32),
                   jax.ShapeDtypeStruct((n, l, D_GMLP), _BF16)],
        compiler_params=pltpu.CompilerParams(
            dimension_semantics=("parallel", "parallel"), vmem_limit_bytes=VMEM_LIMIT),
        name="front_prompt",
    )(x, g1, win_bf, gn, ws_tril_bf, bs_t, gog)


def _ssm_prompt_kernel(xa_ref, sg_ref, wb_ref, wc_ref, lbr_ref, lbi_ref, dsk_ref, gos_ref,
                       mixa_ref, hfin_ref, bu_ref, st_ref):
    lc = xa_ref.shape[1]
    rows = lc * SUBLANES

    @pl.when(pl.program_id(0) == 0)
    def _():
        st_ref[...] = jnp.zeros_like(st_ref)

    xa = pltpu.einshape("btc->tbc", xa_ref[...]).reshape(rows, D_SSM)
    xa_bf = xa.astype(_BF16)
    for k in range(N_LANE_TILES):
        bu_ref[:, 2 * TILE_STATE * k:2 * TILE_STATE * (k + 1)] = _dot(
            xa_bf[:, k * LANES:(k + 1) * LANES], wb_ref[k])

    for k in range(N_LANE_TILES):
        c_re = 2 * TILE_STATE * k
        c_im = c_re + TILE_STATE
        lr = jnp.broadcast_to(lbr_ref[:, k * TILE_STATE:(k + 1) * TILE_STATE], (SUBLANES, TILE_STATE))
        li = jnp.broadcast_to(lbi_ref[:, k * TILE_STATE:(k + 1) * TILE_STATE], (SUBLANES, TILE_STATE))

        def body(t, carry, c_re=c_re, c_im=c_im, lr=lr, li=li):
            hr, hi = carry
            r0 = pl.multiple_of(t * SUBLANES, SUBLANES)
            nr = lr * hr - li * hi + bu_ref[pl.ds(r0, SUBLANES), c_re:c_re + TILE_STATE]
            ni = lr * hi + li * hr + bu_ref[pl.ds(r0, SUBLANES), c_im:c_im + TILE_STATE]
            bu_ref[pl.ds(r0, SUBLANES), c_re:c_re + TILE_STATE] = nr
            bu_ref[pl.ds(r0, SUBLANES), c_im:c_im + TILE_STATE] = ni
            return nr, ni

        hr, hi = lax.fori_loop(
            0, lc, body, (st_ref[:, c_re:c_re + TILE_STATE], st_ref[:, c_im:c_im + TILE_STATE]), unroll=2)
        st_ref[:, c_re:c_re + TILE_STATE] = hr
        st_ref[:, c_im:c_im + TILE_STATE] = hi

    ys = []
    for k in range(N_LANE_TILES):
        hk = bu_ref[:, 2 * TILE_STATE * k:2 * TILE_STATE * (k + 1)].astype(_BF16)
        ys.append(_dot(hk, wc_ref[k]))
    y = jnp.concatenate(ys, axis=-1) + dsk_ref[...] * xa
    sg = pltpu.einshape("btc->tbc", sg_ref[...]).reshape(rows, D_SSM)
    mixa = _rms(_gelu(y) * sg, gos_ref[...]).reshape(lc, SUBLANES, D_SSM)
    mixa_ref[...] = pltpu.einshape("tbc->btc", mixa).astype(_BF16)
    hfin_ref[...] = st_ref[...]


def _ssm_prompt(xa, sg, wb, wc, lbr, lbi, dsk, gos):
    n, l, _ = xa.shape
    lc = SSM_LC
    const = lambda *shape: pl.BlockSpec(shape, lambda i: (0,) * len(shape))
    seq_spec = pl.BlockSpec((n, lc, D_SSM), lambda i: (0, i, 0))
    return pl.pallas_call(
        _ssm_prompt_kernel,
        grid=(l // lc,),
        in_specs=[seq_spec, seq_spec,
                  const(N_LANE_TILES, LANES, 2 * TILE_STATE), const(N_LANE_TILES, 2 * TILE_STATE, LANES),
                  const(1, STATE_COLS), const(1, STATE_COLS), const(1, D_SSM), const(1, D_SSM)],
        out_specs=[seq_spec, const(n, 2 * STATE_COLS)],
        out_shape=[jax.ShapeDtypeStruct((n, l, D_SSM), _BF16),
                   jax.ShapeDtypeStruct((n, 2 * STATE_COLS), _F32)],
        scratch_shapes=[pltpu.VMEM((lc * n, 2 * STATE_COLS), _F32),
                        pltpu.VMEM((n, 2 * STATE_COLS), _F32)],
        compiler_params=pltpu.CompilerParams(
            dimension_semantics=("arbitrary",), vmem_limit_bytes=VMEM_LIMIT),
        name="ssm_prompt",
    )(xa, sg, wb, wc, lbr, lbi, dsk, gos)


def _front_sample_kernel(x_ref, g1_ref, win_ref, gn_ref, w00_ref, b0_ref, gog_ref,
                         wb_ref, wc_ref, lbr_ref, lbi_ref, dsk_ref, gos_ref, h0r_ref, h0i_ref,
                         mix_ref, hr_ref, hi_ref, vrow_ref):
    x = x_ref[...]
    hn = _rms(x, g1_ref[...])
    z = _dot_f32(hn, win_ref[...])
    xa = z[:, :D_SSM]
    ys = []
    for k in range(N_LANE_TILES):
        bu = _dot_f32(xa[:, k * LANES:(k + 1) * LANES], wb_ref[k])
        sl = slice(k * TILE_STATE, (k + 1) * TILE_STATE)
        lr, li = lbr_ref[:, sl], lbi_ref[:, sl]
        h0r, h0i = h0r_ref[:, sl], h0i_ref[:, sl]
        nr = lr * h0r - li * h0i + bu[:, :TILE_STATE]
        ni = lr * h0i + li * h0r + bu[:, TILE_STATE:]
        hr_ref[:, sl] = nr
        hi_ref[:, sl] = ni
        ys.append(_dot_f32(jnp.concatenate([nr, ni], axis=-1), wc_ref[k]))
    y = jnp.concatenate(ys, axis=-1) + dsk_ref[...] * xa
    ya = _gelu(y) * jax.nn.sigmoid(z[:, D_SSM:2 * D_SSM])
    mix_ref[:, :D_SSM] = _rms(ya, gos_ref[...])
    ub = _gelu(z[:, 2 * D_SSM:2 * D_SSM + D_GMLP])
    vbn = _head_norm_gelu(z[:, 2 * D_SSM + D_GMLP:], gn_ref[...])
    vrow_ref[...] = vbn
    s = w00_ref[...] * vbn + b0_ref[...]
    mix_ref[:, D_SSM:] = _rms(ub * s, gog_ref[...])


def _front_sample(x, g1, win, gn, w00, b0, gog, wb, wc, lbr, lbi, dsk, gos, h0r, h0i):
    n = x.shape[0]
    vmem = pl.BlockSpec(memory_space=pltpu.VMEM)
    return pl.pallas_call(
        _front_sample_kernel,
        in_specs=[vmem] * 15,
        out_specs=[vmem] * 4,
        out_shape=[jax.ShapeDtypeStruct((n, D_MODEL), _F32),
                   jax.ShapeDtypeStruct((n, STATE_COLS), _F32),
                   jax.ShapeDtypeStruct((n, STATE_COLS), _F32),
                   jax.ShapeDtypeStruct((n, D_GMLP), _F32)],
        compiler_params=pltpu.CompilerParams(vmem_limit_bytes=VMEM_LIMIT),
        name="front_sample",
    )(x, g1, win, gn, w00, b0, gog, wb, wc, lbr, lbi, dsk, gos, h0r, h0i)


def _route(logits, base):
    tm = logits.shape[0]
    lane = lax.broadcasted_iota(jnp.int32, logits.shape, 1).astype(_F32)
    neg = jnp.float32(-jnp.inf)
    big = jnp.float32(LANES)
    is_g = (lane >= N_EXPERTS) & (lane < N_EXPERTS + N_EXPERT_GROUPS)
    gl = jnp.where(is_g, logits, neg)
    gmax = jnp.max(gl, axis=-1, keepdims=True)
    gi = jnp.min(jnp.where(is_g & (logits == gmax), lane, big), axis=-1, keepdims=True) - N_EXPERTS
    p_top = 1.0 / jnp.sum(jnp.where(is_g, jnp.exp(gl - gmax), 0.0), axis=-1, keepdims=True)
    lo = gi * EXPERTS_PER_GROUP
    in_grp = (lane >= lo) & (lane < lo + EXPERTS_PER_GROUP)
    m1 = jnp.max(jnp.where(in_grp, logits, neg), axis=-1, keepdims=True)
    i1 = jnp.min(jnp.where(in_grp & (logits == m1), lane, big), axis=-1, keepdims=True)
    rest = in_grp & (lane != i1)
    m2 = jnp.max(jnp.where(rest, logits, neg), axis=-1, keepdims=True)
    i2 = jnp.min(jnp.where(rest & (logits == m2), lane, big), axis=-1, keepdims=True)
    e2 = jnp.exp(m2 - m1)
    w1 = p_top / (1.0 + e2)
    w2 = p_top * e2 / (1.0 + e2)
    sel1 = lane == i1
    sel2 = lane == i2
    hits = jnp.where(sel1 | sel2, 1.0, 0.0)
    r_id = lax.broadcasted_iota(jnp.int32, (tm, tm), 0)
    c_id = lax.broadcasted_iota(jnp.int32, (tm, tm), 1)
    ltri = jnp.where(c_id < r_id, 1.0, 0.0).astype(_BF16)
    before = _dot(ltri, hits.astype(_BF16)) + base
    rank1 = jnp.sum(jnp.where(sel1, before, 0.0), axis=-1, keepdims=True)
    rank2 = jnp.sum(jnp.where(sel2, before, 0.0), axis=-1, keepdims=True)
    out = jnp.where(lane == R_E1, i1, 0.0)
    out = jnp.where(lane == R_E2, i2, out)
    out = jnp.where(lane == R_W1, w1, out)
    out = jnp.where(lane == R_W2, w2, out)
    out = jnp.where(lane == R_RANK1, rank1, out)
    out = jnp.where(lane == R_RANK2, rank2, out)
    return out, base + jnp.sum(hits, axis=0, keepdims=True)


def _mixer_out_prompt_kernel(x_ref, mixa_ref, mixb_ref, wo_ref, g2_ref, wr_ref, br_ref,
                             x1_ref, xn_ref, route_ref, cnt_ref, base_ref):
    @pl.when((pl.program_id(0) == 0) & (pl.program_id(1) == 0))
    def _():
        base_ref[...] = jnp.zeros_like(base_ref)

    x1 = x_ref[0] + _dot(mixa_ref[0], wo_ref[:D_SSM, :]) + _dot(mixb_ref[0], wo_ref[D_SSM:, :])
    xn = _rms(x1, g2_ref[...])
    logits = _dot(xn.astype(_BF16), wr_ref[...]) + br_ref[...]
    route, base = _route(logits, base_ref[...])
    x1_ref[...] = x1
    xn_ref[...] = _pack_bf16_pair(xn)
    route_ref[...] = route
    base_ref[...] = base
    cnt_ref[...] = base


def _mixer_out_sample_kernel(x_ref, mix_ref, wo_ref, g2_ref, wr_ref, br_ref, cnt_in_ref,
                             x1_in, xn_in, route_in, x1_ref, xn_ref, route_ref, cnt_ref):
    del x1_in, xn_in, route_in
    x1 = (x_ref[...] + _dot_f32(mix_ref[:, :D_SSM], wo_ref[:D_SSM, :])
          + _dot_f32(mix_ref[:, D_SSM:], wo_ref[D_SSM:, :]))
    xn = _rms(x1, g2_ref[...])
    logits = _dot_f32(xn, wr_ref[...]) + br_ref[...]
    route, base = _route(logits, cnt_in_ref[...])
    x1_ref[...] = x1
    xn_ref[...] = _pack_bf16_pair(xn)
    route_ref[...] = route
    cnt_ref[...] = base


def _mixer_out(x_p, mixa, mixb, x_s, mix_s, wo, g2, wr, br):
    n, l, d = x_p.shape
    ns = x_s.shape[0]
    t_all = n * l + ns
    tm = TOK_TM
    per_seq = l // tm
    const = lambda *shape: pl.BlockSpec(shape, lambda b, i: (0,) * len(shape))
    seq = lambda w: pl.BlockSpec((1, tm, w), lambda b, i: (b, i, 0))
    tok = lambda w: pl.BlockSpec((tm, w), lambda b, i: (b * per_seq + i, 0))
    tok_shapes = [jax.ShapeDtypeStruct((t_all, d), _F32),
                  jax.ShapeDtypeStruct((t_all, d // 2), _U32),
                  jax.ShapeDtypeStruct((t_all, LANES), _F32)]
    cnt_shape = jax.ShapeDtypeStruct((1, LANES), _F32)
    x1, xn, route, cnt = pl.pallas_call(
        _mixer_out_prompt_kernel,
        grid=(n, per_seq),
        in_specs=[seq(d), seq(D_SSM), seq(D_GMLP),
                  const(d, d), const(1, d), const(d, LANES), const(1, LANES)],
        out_specs=[tok(d), tok(d // 2), tok(LANES), const(1, LANES)],
        out_shape=tok_shapes + [cnt_shape],
        scratch_shapes=[pltpu.VMEM((1, LANES), _F32)],
        compiler_params=pltpu.CompilerParams(
            dimension_semantics=("arbitrary", "arbitrary"), vmem_limit_bytes=VMEM_LIMIT),
        name="mixer_out_prompt",
    )(x_p, mixa, mixb, wo.astype(_BF16), g2, wr.astype(_BF16), br)
    tail = (n * l) // ns
    c1 = lambda *shape: pl.BlockSpec(shape, lambda i: (0,) * len(shape))
    anyspec = pl.BlockSpec(memory_space=pl.ANY)
    tail_spec = lambda w: pl.BlockSpec((ns, w), lambda i: (tail, 0))
    return pl.pallas_call(
        _mixer_out_sample_kernel,
        grid=(1,),
        in_specs=[c1(ns, d), c1(ns, d), c1(d, d), c1(1, d), c1(d, LANES), c1(1, LANES), c1(1, LANES),
                  anyspec, anyspec, anyspec],
        out_specs=[tail_spec(d), tail_spec(d // 2), tail_spec(LANES), c1(1, LANES)],
        out_shape=tok_shapes + [cnt_shape],
        input_output_aliases={7: 0, 8: 1, 9: 2},
        compiler_params=pltpu.CompilerParams(
            dimension_semantics=("arbitrary",), vmem_limit_bytes=VMEM_LIMIT),
        name="mixer_out_sample",
    )(x_s, mix_s, wo, g2, wr, br, cnt, x1, xn, route)


def _sc_stream_rows(table_hbm, idx_v, out_hbm, base, n_chunks, chunk, bufs, gsems, wsems):
    def gather(j):
        return pltpu.make_async_copy(table_hbm.at[idx_v.at[pl.ds(j * chunk, chunk)]], bufs[j % 2], gsems[j % 2])

    def write(j):
        return pltpu.make_async_copy(bufs[j % 2], out_hbm.at[pl.ds(base + j * chunk, chunk)], wsems[j % 2])

    gather(0).start()
    for j in range(n_chunks):
        if j + 1 < n_chunks:
            if j >= 1:
                write(j - 1).wait()
            gather(j + 1).start()
        gather(j).wait()
        write(j).start()
    if n_chunks >= 2:
        write(n_chunks - 2).wait()
    write(n_chunks - 1).wait()


def _sc_mesh():
    return plsc.VectorSubcoreMesh(core_axis_name="c", subcore_axis_name="s",
                                  num_cores=SC_CORES, num_subcores=SC_SUBCORES)


def _sc_scratch(n_idx, chunk, w, dtype):
    return ([pltpu.VMEM((n_idx,), jnp.int32), pltpu.VMEM((chunk, w), dtype), pltpu.VMEM((chunk, w), dtype)]
            + [pltpu.SemaphoreType.DMA] * 4)


def _sc_combine(table, idx, chunk):
    n_out = idx.shape[0]
    w = table.shape[1]
    rows_w = n_out // SC_WORKERS
    n_chunks = rows_w // chunk
    assert rows_w * SC_WORKERS == n_out and n_chunks * chunk == rows_w and rows_w % SUBLANES == 0

    def body(table_hbm, idx_hbm, out_hbm, idx_v, buf0, buf1, g0, g1, w0, w1):
        wid = lax.axis_index("s") * SC_CORES + lax.axis_index("c")
        base = pl.multiple_of(wid * rows_w, SUBLANES)
        pltpu.sync_copy(idx_hbm.at[pl.ds(base, rows_w)], idx_v)
        _sc_stream_rows(table_hbm, idx_v, out_hbm, base, n_chunks, chunk, (buf0, buf1), (g0, g1), (w0, w1))

    return pl.kernel(
        body,
        out_type=jax.ShapeDtypeStruct((n_out, w), table.dtype),
        mesh=_sc_mesh(),
        scratch_types=_sc_scratch(rows_w, chunk, w, table.dtype),
        compiler_params=pltpu.CompilerParams(use_tc_tiling_on_sc=True),
        name="sc_combine",
    )(table, idx)


def _sc_dispatch(table, dest, n_out, chunk):
    t_all, w = table.shape
    n_ent = dest.shape[0]
    rows_w = n_out // SC_WORKERS
    n_chunks = rows_w // chunk
    assert rows_w * SC_WORKERS == n_out and n_chunks * chunk == rows_w
    assert rows_w % SC_LANES == 0 and n_ent % SC_LANES == 0 and n_ent == 2 * t_all and n_out < 3 * t_all

    def body(table_hbm, dest_hbm, out_hbm, inv_v, buf0, buf1, g0, g1, w0, w1, dest_v):
        wid = lax.axis_index("s") * SC_CORES + lax.axis_index("c")
        base = pl.multiple_of(wid * rows_w, SUBLANES)
        pltpu.sync_copy(dest_hbm, dest_v)
        lane = lax.iota(jnp.int32, SC_LANES)

        @pl.loop(0, rows_w // SC_LANES)
        def _(i):
            r = base + i * SC_LANES + lane
            r = jnp.where(r >= t_all, r - t_all, r)
            r = jnp.where(r >= t_all, r - t_all, r)
            inv_v[pl.ds(pl.multiple_of(i * SC_LANES, SC_LANES), SC_LANES)] = r

        @pl.loop(0, n_ent // SC_LANES)
        def _(i):
            ent = i * SC_LANES + lane
            local = dest_v[pl.ds(pl.multiple_of(i * SC_LANES, SC_LANES), SC_LANES)] - base
            mine = (local >= 0) & (local < rows_w)
            tok = jnp.where(ent >= t_all, ent - t_all, ent)
            plsc.store_scatter(inv_v, [jnp.where(mine, local, 0)], tok, mask=mine)

        _sc_stream_rows(table_hbm, inv_v, out_hbm, base, n_chunks, chunk, (buf0, buf1), (g0, g1), (w0, w1))

    return pl.kernel(
        body,
        out_type=jax.ShapeDtypeStruct((n_out, w), table.dtype),
        mesh=_sc_mesh(),
        scratch_types=_sc_scratch(rows_w, chunk, w, table.dtype) + [pltpu.VMEM((n_ent,), jnp.int32)],
        compiler_params=pltpu.CompilerParams(use_tc_tiling_on_sc=True, needs_layout_passes=False),
        name="sc_dispatch",
    )(table, dest)


def _experts_kernel(tile_e_ref, tile_first_ref, n_tiles_ref, xs_ref, wg_ref, wu_ref, wd_ref,
                    y_ref, wg_bf, wu_bf, wd_bf):
    i = pl.program_id(0)

    @pl.when(tile_first_ref[i] == 1)
    def _():
        wg_bf[...] = wg_ref[0].astype(_BF16)
        wu_bf[...] = wu_ref[0].astype(_BF16)
        wd_bf[...] = wd_ref[0].astype(_BF16)

    @pl.when(i < n_tiles_ref[0])
    def _():
        x = _unpack_bf16_pair(xs_ref[...]).astype(_BF16)
        a = _dot(x, wg_bf[...])
        u = _dot(x, wu_bf[...])
        h = (a * jax.nn.sigmoid(a) * u).astype(_BF16)
        y_ref[...] = _pack_bf16_pair(_dot(h, wd_bf[...]))


def _experts(tile_e, tile_first, n_tiles, xs, w_gate, w_up, w_down):
    p, dh = xs.shape
    d = 2 * dh
    tm = EXP_TM
    row = lambda i, te, tf, nt: (jnp.minimum(i, nt[0] - 1), 0)
    wsel = lambda i, te, tf, nt: (te[i], 0, 0)
    grid_spec = pltpu.PrefetchScalarGridSpec(
        num_scalar_prefetch=3,
        grid=(p // tm,),
        in_specs=[pl.BlockSpec((tm, dh), row),
                  pl.BlockSpec((1, d, D_EXPERT), wsel),
                  pl.BlockSpec((1, d, D_EXPERT), wsel),
                  pl.BlockSpec((1, D_EXPERT, d), wsel)],
        out_specs=pl.BlockSpec((tm, dh), row),
        scratch_shapes=[pltpu.VMEM((d, D_EXPERT), _BF16), pltpu.VMEM((d, D_EXPERT), _BF16),
                        pltpu.VMEM((D_EXPERT, d), _BF16)],
    )
    return pl.pallas_call(
        _experts_kernel,
        grid_spec=grid_spec,
        out_shape=jax.ShapeDtypeStruct((p, dh), _U32),
        compiler_params=pltpu.CompilerParams(
            dimension_semantics=("arbitrary",), vmem_limit_bytes=VMEM_LIMIT),
        name="experts",
    )(tile_e, tile_first, n_tiles, xs, w_gate, w_up, w_down)


def _final_kernel(x1_ref, ya_ref, yb_ref, route_ref, gf_ref, y_ref):
    route = route_ref[...]
    x2 = (x1_ref[...] + route[:, R_W1:R_W1 + 1] * _unpack_bf16_pair(ya_ref[...])
          + route[:, R_W2:R_W2 + 1] * _unpack_bf16_pair(yb_ref[...]))
    y_ref[...] = _rms(x2, gf_ref[...])


def _final(x1, yab, route, gf, n_prompt, n_sample):
    d = x1.shape[1]

    def call(tm, first_block, n_rows, name):
        tok = lambda w: pl.BlockSpec((tm, w), lambda i: (first_block + i, 0))
        sel = lambda k: pl.BlockSpec((None, tm, d // 2), lambda i: (k, first_block + i, 0))
        return pl.pallas_call(
            _final_kernel,
            grid=(n_rows // tm,),
            in_specs=[tok(d), sel(0), sel(1), tok(LANES), pl.BlockSpec((1, d), lambda i: (0, 0))],
            out_specs=pl.BlockSpec((tm, d), lambda i: (i, 0)),
            out_shape=jax.ShapeDtypeStruct((n_rows, d), _F32),
            compiler_params=pltpu.CompilerParams(
                dimension_semantics=("parallel",), vmem_limit_bytes=VMEM_LIMIT),
            name=name,
        )(x1, yab, yab, route, gf)

    return (call(FINAL_TM, 0, n_prompt, "final_prompt"),
            call(n_sample, n_prompt // n_sample, n_sample, "final_sample"))


def _ssm_params(lam_re, lam_im, log_dt, b_re, b_im, c_re, c_im, d_skip):
    dt = jnp.exp(log_dt)[:, None]
    mag = jnp.exp(lam_re * dt)
    ang = lam_im * dt
    lb_re = mag * jnp.cos(ang)
    lb_im = mag * jnp.sin(ang)
    den = lam_re * lam_re + lam_im * lam_im
    nr = lb_re - 1.0
    ni = lb_im
    k_re = (nr * lam_re + ni * lam_im) / den
    k_im = (ni * lam_re - nr * lam_im) / den
    bb_re = k_re[:, :, None] * b_re - k_im[:, :, None] * b_im
    bb_im = k_re[:, :, None] * b_im + k_im[:, :, None] * b_re
    eye = jnp.eye(SUBLANES, dtype=_F32)

    def b_blocks(bb):
        t = jnp.transpose(bb, (0, 2, 1)).reshape(N_LANE_TILES, 8, SSM_GROUP, SSM_STATE)
        return jnp.einsum('kahp,ab->kahbp', t, eye).reshape(N_LANE_TILES, LANES, TILE_STATE)

    def c_blocks(c):
        t = c.reshape(N_LANE_TILES, 8, SSM_GROUP, SSM_STATE)
        return jnp.einsum('kahp,ab->kapbh', t, eye).reshape(N_LANE_TILES, TILE_STATE, LANES)

    wb = jnp.concatenate([b_blocks(bb_re), b_blocks(bb_im)], axis=-1)
    wc = jnp.concatenate([c_blocks(c_re), -c_blocks(c_im)], axis=1)
    return (wb, wc, lb_re.reshape(1, STATE_COLS), lb_im.reshape(1, STATE_COLS),
            d_skip.reshape(1, D_SSM))


def _dispatch_plan(route, cnt, tm):
    t_all = route.shape[0]
    e = route[:, R_E1:R_E2 + 1].astype(jnp.int32)
    rank = route[:, R_RANK1:R_RANK2 + 1].astype(jnp.int32)
    counts = cnt[0, :N_EXPERTS].astype(jnp.int32)
    tiles_per = (counts + tm - 1) // tm
    tile_end = jnp.cumsum(tiles_per)
    start_row = (tile_end - tiles_per) * tm
    ids = jnp.arange(N_EXPERTS, dtype=jnp.int32)
    dest = jnp.sum(jnp.where(e[:, :, None] == ids, start_row, 0), axis=-1) + rank
    dest = jnp.concatenate([dest[:, 0], dest[:, 1]])
    n_tiles_max = (2 * t_all + N_EXPERTS * (tm - 1)) // tm
    n_tiles_max += n_tiles_max % 2
    tile_ids = jnp.arange(n_tiles_max, dtype=jnp.int32)
    tile_e = jnp.minimum(jnp.sum((tile_ids[:, None] >= tile_end[None, :]).astype(jnp.int32), axis=1),
                         N_EXPERTS - 1)
    tile_first = jnp.concatenate([jnp.ones((1,), jnp.int32), (tile_e[1:] != tile_e[:-1]).astype(jnp.int32)])
    n_tiles = tile_end[-1:].astype(jnp.int32)
    return dest, n_tiles_max * tm, tile_e, tile_first, n_tiles


def kernel(x_prompt, x_sample, state_ssm_re, state_ssm_im, norm1_g, w_in, lam_re, lam_im, log_dt, ssm_b_re, ssm_b_im, ssm_c_re, ssm_c_im, ssm_d, gmlp_norm_g, gmlp_w_s, gmlp_b_s, out_norm_ssm_g, out_norm_gmlp_g, w_out, norm2_g, w_router_group, b_router_group, w_router_expert, b_router_expert, w_gate, w_up, w_down, final_norm_g):
    n, l, d = x_prompt.shape
    ns = x_sample.shape[0]
    t_all = n * l + ns
    li = 0
    g1 = norm1_g[li].reshape(1, d)
    gn = gmlp_norm_g[li].reshape(1, D_GMLP)
    tril = jnp.tril(jnp.ones((CHUNK, CHUNK), dtype=bool))
    ws_tril = jnp.where(tril[None], gmlp_w_s[li], 0.0)
    bs = gmlp_b_s[li]
    gog = out_norm_gmlp_g[li].reshape(1, D_GMLP)
    gos = out_norm_ssm_g[li].reshape(1, D_SSM)
    wb, wc, lbr, lbi, dsk = _ssm_params(lam_re[li], lam_im[li], log_dt[li], ssm_b_re[li], ssm_b_im[li],
                                        ssm_c_re[li], ssm_c_im[li], ssm_d[li])
    g2 = norm2_g[li].reshape(1, d)
    pad = LANES - N_EXPERTS - N_EXPERT_GROUPS
    wr = jnp.concatenate([w_router_expert[li], w_router_group[li], jnp.zeros((d, pad), _F32)], axis=1)
    br = jnp.concatenate([b_router_expert[li], b_router_group[li], jnp.zeros((pad,), _F32)]).reshape(1, LANES)

    xa, sg, mixb = _front_prompt(x_prompt, g1, w_in[li].astype(_BF16), gn, ws_tril.astype(_BF16), bs.T, gog)
    mixa, hfin = _ssm_prompt(xa, sg, wb.astype(_BF16), wc.astype(_BF16), lbr, lbi, dsk, gos)
    w00 = jnp.repeat(ws_tril[:, 0, 0], GMLP_HEAD).reshape(1, D_GMLP)
    b0 = jnp.repeat(bs[:, 0], GMLP_HEAD).reshape(1, D_GMLP)
    mix_s, hr_s, hi_s, vrow = _front_sample(
        x_sample.reshape(ns, d), g1, w_in[li], gn, w00, b0, gog, wb, wc, lbr, lbi, dsk, gos,
        state_ssm_re[li].reshape(ns, STATE_COLS), state_ssm_im[li].reshape(ns, STATE_COLS))

    x1, xn, route, cnt = _mixer_out(x_prompt, mixa, mixb, x_sample.reshape(ns, d), mix_s,
                                    w_out[li], g2, wr, br)
    dest, n_rows, tile_e, tile_first, n_tiles = _dispatch_plan(route, cnt, EXP_TM)
    xs = _sc_dispatch(xn, dest, n_rows, DISPATCH_CHUNK)
    ys = _experts(tile_e, tile_first, n_tiles, xs, w_gate[li], w_up[li], w_down[li])
    yab = _sc_combine(ys, dest, COMBINE_CHUNK).reshape(2, t_all, d // 2)
    y_p, y_s = _final(x1, yab, route, final_norm_g.reshape(1, d), n * l, ns)

    hf = hfin.reshape(n, N_LANE_TILES, 2, 8, SSM_STATE)
    re_p = hf[:, :, 0].reshape(1, n, N_SSM_GROUPS, SSM_STATE)
    im_p = hf[:, :, 1].reshape(1, n, N_SSM_GROUPS, SSM_STATE)
    re_s = hr_s.reshape(1, ns, N_SSM_GROUPS, SSM_STATE)
    im_s = hi_s.reshape(1, ns, N_SSM_GROUPS, SSM_STATE)
    return (y_p.reshape(n, l, d), y_s.reshape(ns, 1, d), re_p, im_p, re_s, im_s,
            vrow.reshape(1, ns, 1, D_GMLP))
```

```python
import math

import jax
import jax.numpy as jnp
from jax import lax
from jax.experimental import pallas as pl
from jax.experimental.pallas import tpu as pltpu
from jax.experimental.pallas import tpu_sc as plsc

D_MODEL = 1024
D_SSM = 512
D_GMLP = 512
SSM_GROUP = 16
N_SSM_GROUPS = 32
SSM_STATE = 64
CHUNK = 128
N_GMLP_HEADS = 4
GMLP_HEAD = 128
N_EXPERT_GROUPS = 4
EXPERTS_PER_GROUP = 8
N_EXPERTS = 32
D_EXPERT = 512
D_IN = 2048
EPS = 1e-6

LANES = 128
SUBLANES = 8
N_LANE_TILES = D_SSM // LANES
STATE_COLS = N_SSM_GROUPS * SSM_STATE
TILE_STATE = STATE_COLS // N_LANE_TILES
VMEM_LIMIT = 56 * 1024 * 1024

SC_CORES = 2
SC_SUBCORES = 16
SC_LANES = 16
SC_WORKERS = SC_CORES * SC_SUBCORES

FRONT_TL = 512
SSM_LC = 128
TOK_TM = 512
FINAL_TM = 512
EXP_TM = 512
DISPATCH_CHUNK = 64
COMBINE_CHUNK = 24

R_E1, R_E2, R_W1, R_W2, R_RANK1, R_RANK2 = 0, 1, 2, 3, 4, 5

_INV_SQRT2 = 1.0 / math.sqrt(2.0)
_BF16 = jnp.bfloat16
_F32 = jnp.float32
_U32 = jnp.uint32


def _gelu(x):
    return 0.5 * x * (1.0 + lax.erf(x * _INV_SQRT2))


def _rms(x, g):
    return x * lax.rsqrt(jnp.mean(x * x, axis=-1, keepdims=True) + EPS) * g


def _dot(a, b):
    return jnp.dot(a, b, preferred_element_type=_F32)


def _dot_f32(a, b):
    return jnp.dot(a, b, preferred_element_type=_F32, precision=lax.Precision.HIGHEST)


def _pack_bf16_pair(x):
    w = x.shape[1] // 2
    hi = lax.bitcast_convert_type(x[:, :w].astype(_BF16).astype(_F32), _U32)
    lo = lax.bitcast_convert_type(x[:, w:].astype(_BF16).astype(_F32), _U32)
    return hi | (lo >> 16)


def _unpack_bf16_pair(p):
    hi = lax.bitcast_convert_type(p & jnp.uint32(0xFFFF0000), _F32)
    lo = lax.bitcast_convert_type(p << 16, _F32)
    return jnp.concatenate([hi, lo], axis=-1)


def _head_norm_gelu(vb, gn):
    v = _gelu(vb)
    parts = []
    for h in range(N_GMLP_HEADS):
        vh = v[:, h * GMLP_HEAD:(h + 1) * GMLP_HEAD]
        parts.append(vh * lax.rsqrt(jnp.mean(vh * vh, axis=-1, keepdims=True) + EPS))
    return jnp.concatenate(parts, axis=-1) * gn


def _front_prompt_kernel(x_ref, g1_ref, win_ref, gn_ref, ws_ref, bs_ref, gog_ref,
                         xa_ref, sg_ref, mixb_ref):
    x = x_ref[0]
    hn = _rms(x, g1_ref[...]).astype(_BF16)
    z = _dot(hn, win_ref[...])
    xa_ref[0] = z[:, :D_SSM]
    sg_ref[0] = jax.nn.sigmoid(z[:, D_SSM:2 * D_SSM])
    ub = _gelu(z[:, 2 * D_SSM:2 * D_SSM + D_GMLP])
    vbn = _head_norm_gelu(z[:, 2 * D_SSM + D_GMLP:], gn_ref[...]).astype(_BF16)
    tl = x.shape[0]
    rows = []
    for c in range(tl // CHUNK):
        heads = []
        for h in range(N_GMLP_HEADS):
            vh = vbn[c * CHUNK:(c + 1) * CHUNK, h * GMLP_HEAD:(h + 1) * GMLP_HEAD]
            heads.append(_dot(ws_ref[h], vh) + bs_ref[:, h:h + 1])
        rows.append(jnp.concatenate(heads, axis=-1))
    s = jnp.concatenate(rows, axis=0)
    mixb_ref[0] = _rms(ub * s, gog_ref[...]).astype(_BF16)


def _front_prompt(x, g1, win_bf, gn, ws_tril_bf, bs_t, gog):
    n, l, d = x.shape
    tl = FRONT_TL
    grid = (n, l // tl)
    const = lambda *shape: pl.BlockSpec(shape, lambda b, i: (0,) * len(shape))
    seq = lambda w: pl.BlockSpec((1, tl, w), lambda b, i: (b, i, 0))
    return pl.pallas_call(
        _front_prompt_kernel,
        grid=grid,
        in_specs=[seq(d), const(1, d), const(d, D_IN), const(1, D_GMLP),
                  const(N_GMLP_HEADS, CHUNK, CHUNK), const(CHUNK, N_GMLP_HEADS), const(1, D_GMLP)],
        out_specs=[seq(D_SSM), seq(D_SSM), seq(D_GMLP)],
        out_shape=[jax.ShapeDtypeStruct((n, l, D_SSM), _F32),
                   jax.ShapeDtypeStruct((n, l, D_SSM), _F32),
                   jax.ShapeDtypeStruct((n, l, D_GMLP), _BF16)],
        compiler_params=pltpu.CompilerParams(
            dimension_semantics=("parallel", "parallel"), vmem_limit_bytes=VMEM_LIMIT),
        name="front_prompt",
    )(x, g1, win_bf, gn, ws_tril_bf, bs_t, gog)


def _ssm_prompt_kernel(xa_ref, sg_ref, wb_ref, wc_ref, lbr_ref, lbi_ref, dsk_ref, gos_ref,
                       mixa_ref, hfin_ref, bu_ref, st_ref):
    lc = xa_ref.shape[1]
    rows = lc * SUBLANES
    pair = 2 * SUBLANES

    @pl.when(pl.program_id(0) == 0)
    def _():
        st_ref[...] = jnp.zeros_like(st_ref)

    xa = pltpu.einshape("btc->tbc", xa_ref[...]).reshape(rows, D_SSM)
    xa_bf = xa.astype(_BF16)
    for k in range(N_LANE_TILES):
        bu_ref[:, 2 * TILE_STATE * k:2 * TILE_STATE * (k + 1)] = _dot(
            xa_bf[:, k * LANES:(k + 1) * LANES], wb_ref[k])

    for kk in range(0, N_LANE_TILES, 2):
        tiles = (kk, kk + 1)
        cols = [(2 * TILE_STATE * k, 2 * TILE_STATE * k + TILE_STATE) for k in tiles]
        lbs = [(jnp.broadcast_to(lbr_ref[:, k * TILE_STATE:(k + 1) * TILE_STATE], (SUBLANES, TILE_STATE)),
                jnp.broadcast_to(lbi_ref[:, k * TILE_STATE:(k + 1) * TILE_STATE], (SUBLANES, TILE_STATE)))
               for k in tiles]

        def body(j, carry, cols=cols, lbs=lbs):
            r0 = pl.multiple_of(j * pair, pair)
            r1 = pl.multiple_of(r0 + SUBLANES, SUBLANES)
            out = []
            for q, ((c_re, c_im), (lr, li)) in enumerate(zip(cols, lbs)):
                hr, hi = carry[2 * q], carry[2 * q + 1]
                ar = lr * hr - li * hi + bu_ref[pl.ds(r0, SUBLANES), c_re:c_re + TILE_STATE]
                ai = lr * hi + li * hr + bu_ref[pl.ds(r0, SUBLANES), c_im:c_im + TILE_STATE]
                br = lr * ar - li * ai + bu_ref[pl.ds(r1, SUBLANES), c_re:c_re + TILE_STATE]
                bi = lr * ai + li * ar + bu_ref[pl.ds(r1, SUBLANES), c_im:c_im + TILE_STATE]
                bu_ref[pl.ds(r0, SUBLANES), c_re:c_re + TILE_STATE] = ar
                bu_ref[pl.ds(r0, SUBLANES), c_im:c_im + TILE_STATE] = ai
                bu_ref[pl.ds(r1, SUBLANES), c_re:c_re + TILE_STATE] = br
                bu_ref[pl.ds(r1, SUBLANES), c_im:c_im + TILE_STATE] = bi
                out += [br, bi]
            return tuple(out)

        init = tuple(st_ref[:, c:c + TILE_STATE] for c_pair in cols for c in c_pair)
        fin = lax.fori_loop(0, lc // 2, body, init, unroll=2)
        for q, (c_re, c_im) in enumerate(cols):
            st_ref[:, c_re:c_re + TILE_STATE] = fin[2 * q]
            st_ref[:, c_im:c_im + TILE_STATE] = fin[2 * q + 1]

    ys = []
    for k in range(N_LANE_TILES):
        hk = bu_ref[:, 2 * TILE_STATE * k:2 * TILE_STATE * (k + 1)].astype(_BF16)
        ys.append(_dot(hk, wc_ref[k]))
    y = jnp.concatenate(ys, axis=-1) + dsk_ref[...] * xa
    sg = pltpu.einshape("btc->tbc", sg_ref[...]).reshape(rows, D_SSM)
    mixa = _rms(_gelu(y) * sg, gos_ref[...]).reshape(lc, SUBLANES, D_SSM)
    mixa_ref[...] = pltpu.einshape("tbc->btc", mixa).astype(_BF16)
    hfin_ref[...] = st_ref[...]


def _ssm_prompt(xa, sg, wb, wc, lbr, lbi, dsk, gos):
    n, l, _ = xa.shape
    lc = SSM_LC
    const = lambda *shape: pl.BlockSpec(shape, lambda i: (0,) * len(shape))
    seq_spec = pl.BlockSpec((n, lc, D_SSM), lambda i: (0, i, 0))
    return pl.pallas_call(
        _ssm_prompt_kernel,
        grid=(l // lc,),
        in_specs=[seq_spec, seq_spec,
                  const(N_LANE_TILES, LANES, 2 * TILE_STATE), const(N_LANE_TILES, 2 * TILE_STATE, LANES),
                  const(1, STATE_COLS), const(1, STATE_COLS), const(1, D_SSM), const(1, D_SSM)],
        out_specs=[seq_spec, const(n, 2 * STATE_COLS)],
        out_shape=[jax.ShapeDtypeStruct((n, l, D_SSM), _BF16),
                   jax.ShapeDtypeStruct((n, 2 * STATE_COLS), _F32)],
        scratch_shapes=[pltpu.VMEM((lc * n, 2 * STATE_COLS), _F32),
                        pltpu.VMEM((n, 2 * STATE_COLS), _F32)],
        compiler_params=pltpu.CompilerParams(
            dimension_semantics=("arbitrary",), vmem_limit_bytes=VMEM_LIMIT),
        name="ssm_prompt",
    )(xa, sg, wb, wc, lbr, lbi, dsk, gos)


def _front_sample_kernel(x_ref, g1_ref, win_ref, gn_ref, w00_ref, b0_ref, gog_ref,
                         wb_ref, wc_ref, lbr_ref, lbi_ref, dsk_ref, gos_ref, h0r_ref, h0i_ref,
                         mix_ref, hr_ref, hi_ref, vrow_ref):
    x = x_ref[...]
    hn = _rms(x, g1_ref[...])
    z = _dot_f32(hn, win_ref[...])
    xa = z[:, :D_SSM]
    ys = []
    for k in range(N_LANE_TILES):
        bu = _dot_f32(xa[:, k * LANES:(k + 1) * LANES], wb_ref[k])
        sl = slice(k * TILE_STATE, (k + 1) * TILE_STATE)
        lr, li = lbr_ref[:, sl], lbi_ref[:, sl]
        h0r, h0i = h0r_ref[:, sl], h0i_ref[:, sl]
        nr = lr * h0r - li * h0i + bu[:, :TILE_STATE]
        ni = lr * h0i + li * h0r + bu[:, TILE_STATE:]
        hr_ref[:, sl] = nr
        hi_ref[:, sl] = ni
        ys.append(_dot_f32(jnp.concatenate([nr, ni], axis=-1), wc_ref[k]))
    y = jnp.concatenate(ys, axis=-1) + dsk_ref[...] * xa
    ya = _gelu(y) * jax.nn.sigmoid(z[:, D_SSM:2 * D_SSM])
    mix_ref[:, :D_SSM] = _rms(ya, gos_ref[...])
    ub = _gelu(z[:, 2 * D_SSM:2 * D_SSM + D_GMLP])
    vbn = _head_norm_gelu(z[:, 2 * D_SSM + D_GMLP:], gn_ref[...])
    vrow_ref[...] = vbn
    s = w00_ref[...] * vbn + b0_ref[...]
    mix_ref[:, D_SSM:] = _rms(ub * s, gog_ref[...])


def _front_sample(x, g1, win, gn, w00, b0, gog, wb, wc, lbr, lbi, dsk, gos, h0r, h0i):
    n = x.shape[0]
    vmem = pl.BlockSpec(memory_space=pltpu.VMEM)
    return pl.pallas_call(
        _front_sample_kernel,
        in_specs=[vmem] * 15,
        out_specs=[vmem] * 4,
        out_shape=[jax.ShapeDtypeStruct((n, D_MODEL), _F32),
                   jax.ShapeDtypeStruct((n, STATE_COLS), _F32),
                   jax.ShapeDtypeStruct((n, STATE_COLS), _F32),
                   jax.ShapeDtypeStruct((n, D_GMLP), _F32)],
        compiler_params=pltpu.CompilerParams(vmem_limit_bytes=VMEM_LIMIT),
        name="front_sample",
    )(x, g1, win, gn, w00, b0, gog, wb, wc, lbr, lbi, dsk, gos, h0r, h0i)


def _route(logits, base):
    tm = logits.shape[0]
    lane = lax.broadcasted_iota(jnp.int32, logits.shape, 1).astype(_F32)
    neg = jnp.float32(-jnp.inf)
    big = jnp.float32(LANES)
    is_g = (lane >= N_EXPERTS) & (lane < N_EXPERTS + N_EXPERT_GROUPS)
    gl = jnp.where(is_g, logits, neg)
    gmax = jnp.max(gl, axis=-1, keepdims=True)
    gi = jnp.min(jnp.where(is_g & (logits == gmax), lane, big), axis=-1, keepdims=True) - N_EXPERTS
    p_top = 1.0 / jnp.sum(jnp.where(is_g, jnp.exp(gl - gmax), 0.0), axis=-1, keepdims=True)
    lo = gi * EXPERTS_PER_GROUP
    in_grp = (lane >= lo) & (lane < lo + EXPERTS_PER_GROUP)
    m1 = jnp.max(jnp.where(in_grp, logits, neg), axis=-1, keepdims=True)
    i1 = jnp.min(jnp.where(in_grp & (logits == m1), lane, big), axis=-1, keepdims=True)
    rest = in_grp & (lane != i1)
    m2 = jnp.max(jnp.where(rest, logits, neg), axis=-1, keepdims=True)
    i2 = jnp.min(jnp.where(rest & (logits == m2), lane, big), axis=-1, keepdims=True)
    e2 = jnp.exp(m2 - m1)
    w1 = p_top / (1.0 + e2)
    w2 = p_top * e2 / (1.0 + e2)
    sel1 = lane == i1
    sel2 = lane == i2
    hits = jnp.where(sel1 | sel2, 1.0, 0.0)
    r_id = lax.broadcasted_iota(jnp.int32, (tm, tm), 0)
    c_id = lax.broadcasted_iota(jnp.int32, (tm, tm), 1)
    ltri = jnp.where(c_id < r_id, 1.0, 0.0).astype(_BF16)
    before = _dot(ltri, hits.astype(_BF16)) + base
    rank1 = jnp.sum(jnp.where(sel1, before, 0.0), axis=-1, keepdims=True)
    rank2 = jnp.sum(jnp.where(sel2, before, 0.0), axis=-1, keepdims=True)
    out = jnp.where(lane == R_E1, i1, 0.0)
    out = jnp.where(lane == R_E2, i2, out)
    out = jnp.where(lane == R_W1, w1, out)
    out = jnp.where(lane == R_W2, w2, out)
    out = jnp.where(lane == R_RANK1, rank1, out)
    out = jnp.where(lane == R_RANK2, rank2, out)
    return out, base + jnp.sum(hits, axis=0, keepdims=True)


def _mixer_out_prompt_kernel(x_ref, mixa_ref, mixb_ref, wo_ref, g2_ref, wr_ref, br_ref,
                             x1_ref, xn_ref, route_ref, cnt_ref, base_ref):
    @pl.when((pl.program_id(0) == 0) & (pl.program_id(1) == 0))
    def _():
        base_ref[...] = jnp.zeros_like(base_ref)

    x1 = x_ref[0] + _dot(mixa_ref[0], wo_ref[:D_SSM, :]) + _dot(mixb_ref[0], wo_ref[D_SSM:, :])
    xn = _rms(x1, g2_ref[...])
    logits = _dot(xn.astype(_BF16), wr_ref[...]) + br_ref[...]
    route, base = _route(logits, base_ref[...])
    x1_ref[...] = x1
    xn_ref[...] = _pack_bf16_pair(xn)
    route_ref[...] = route
    base_ref[...] = base
    cnt_ref[...] = base


def _mixer_out_sample_kernel(x_ref, mix_ref, wo_ref, g2_ref, wr_ref, br_ref, cnt_in_ref,
                             x1_in, xn_in, route_in, x1_ref, xn_ref, route_ref, cnt_ref):
    del x1_in, xn_in, route_in
    x1 = (x_ref[...] + _dot_f32(mix_ref[:, :D_SSM], wo_ref[:D_SSM, :])
          + _dot_f32(mix_ref[:, D_SSM:], wo_ref[D_SSM:, :]))
    xn = _rms(x1, g2_ref[...])
    logits = _dot_f32(xn, wr_ref[...]) + br_ref[...]
    route, base = _route(logits, cnt_in_ref[...])
    x1_ref[...] = x1
    xn_ref[...] = _pack_bf16_pair(xn)
    route_ref[...] = route
    cnt_ref[...] = base


def _mixer_out(x_p, mixa, mixb, x_s, mix_s, wo, g2, wr, br):
    n, l, d = x_p.shape
    ns = x_s.shape[0]
    t_all = n * l + ns
    tm = TOK_TM
    per_seq = l // tm
    const = lambda *shape: pl.BlockSpec(shape, lambda b, i: (0,) * len(shape))
    seq = lambda w: pl.BlockSpec((1, tm, w), lambda b, i: (b, i, 0))
    tok = lambda w: pl.BlockSpec((tm, w), lambda b, i: (b * per_seq + i, 0))
    tok_shapes = [jax.ShapeDtypeStruct((t_all, d), _F32),
                  jax.ShapeDtypeStruct((t_all, d // 2), _U32),
                  jax.ShapeDtypeStruct((t_all, LANES), _F32)]
    cnt_shape = jax.ShapeDtypeStruct((1, LANES), _F32)
    x1, xn, route, cnt = pl.pallas_call(
        _mixer_out_prompt_kernel,
        grid=(n, per_seq),
        in_specs=[seq(d), seq(D_SSM), seq(D_GMLP),
                  const(d, d), const(1, d), const(d, LANES), const(1, LANES)],
        out_specs=[tok(d), tok(d // 2), tok(LANES), const(1, LANES)],
        out_shape=tok_shapes + [cnt_shape],
        scratch_shapes=[pltpu.VMEM((1, LANES), _F32)],
        compiler_params=pltpu.CompilerParams(
            dimension_semantics=("arbitrary", "arbitrary"), vmem_limit_bytes=VMEM_LIMIT),
        name="mixer_out_prompt",
    )(x_p, mixa, mixb, wo.astype(_BF16), g2, wr.astype(_BF16), br)
    tail = (n * l) // ns
    c1 = lambda *shape: pl.BlockSpec(shape, lambda i: (0,) * len(shape))
    anyspec = pl.BlockSpec(memory_space=pl.ANY)
    tail_spec = lambda w: pl.BlockSpec((ns, w), lambda i: (tail, 0))
    return pl.pallas_call(
        _mixer_out_sample_kernel,
        grid=(1,),
        in_specs=[c1(ns, d), c1(ns, d), c1(d, d), c1(1, d), c1(d, LANES), c1(1, LANES), c1(1, LANES),
                  anyspec, anyspec, anyspec],
        out_specs=[tail_spec(d), tail_spec(d // 2), tail_spec(LANES), c1(1, LANES)],
        out_shape=tok_shapes + [cnt_shape],
        input_output_aliases={7: 0, 8: 1, 9: 2},
        compiler_params=pltpu.CompilerParams(
            dimension_semantics=("arbitrary",), vmem_limit_bytes=VMEM_LIMIT),
        name="mixer_out_sample",
    )(x_s, mix_s, wo, g2, wr, br, cnt, x1, xn, route)


def _sc_stream_rows(table_hbm, idx_v, out_hbm, base, n_chunks, chunk, bufs, gsems, wsems):
    def gather(j):
        return pltpu.make_async_copy(table_hbm.at[idx_v.at[pl.ds(j * chunk, chunk)]], bufs[j % 2], gsems[j % 2])

    def write(j):
        return pltpu.make_async_copy(bufs[j % 2], out_hbm.at[pl.ds(base + j * chunk, chunk)], wsems[j % 2])

    gather(0).start()
    for j in range(n_chunks):
        if j + 1 < n_chunks:
            if j >= 1:
                write(j - 1).wait()
            gather(j + 1).start()
        gather(j).wait()
        write(j).start()
    if n_chunks >= 2:
        write(n_chunks - 2).wait()
    write(n_chunks - 1).wait()


def _sc_mesh():
    return plsc.VectorSubcoreMesh(core_axis_name="c", subcore_axis_name="s",
                                  num_cores=SC_CORES, num_subcores=SC_SUBCORES)


def _sc_scratch(n_idx, chunk, w, dtype):
    return ([pltpu.VMEM((n_idx,), jnp.int32), pltpu.VMEM((chunk, w), dtype), pltpu.VMEM((chunk, w), dtype)]
            + [pltpu.SemaphoreType.DMA] * 4)


def _sc_combine(table, idx, chunk):
    n_out = idx.shape[0]
    w = table.shape[1]
    rows_w = n_out // SC_WORKERS
    n_chunks = rows_w // chunk
    assert rows_w * SC_WORKERS == n_out and n_chunks * chunk == rows_w and rows_w % SUBLANES == 0

    def body(table_hbm, idx_hbm, out_hbm, idx_v, buf0, buf1, g0, g1, w0, w1):
        wid = lax.axis_index("s") * SC_CORES + lax.axis_index("c")
        base = pl.multiple_of(wid * rows_w, SUBLANES)
        pltpu.sync_copy(idx_hbm.at[pl.ds(base, rows_w)], idx_v)
        _sc_stream_rows(table_hbm, idx_v, out_hbm, base, n_chunks, chunk, (buf0, buf1), (g0, g1), (w0, w1))

    return pl.kernel(
        body,
        out_type=jax.ShapeDtypeStruct((n_out, w), table.dtype),
        mesh=_sc_mesh(),
        scratch_types=_sc_scratch(rows_w, chunk, w, table.dtype),
        compiler_params=pltpu.CompilerParams(use_tc_tiling_on_sc=True),
        name="sc_combine",
    )(table, idx)


def _sc_dispatch(table, dest, n_out, chunk):
    t_all, w = table.shape
    n_ent = dest.shape[0]
    rows_w = n_out // SC_WORKERS
    n_chunks = rows_w // chunk
    assert rows_w * SC_WORKERS == n_out and n_chunks * chunk == rows_w
    assert rows_w % SC_LANES == 0 and n_ent % SC_LANES == 0 and n_ent == 2 * t_all and n_out < 3 * t_all

    def body(table_hbm, dest_hbm, out_hbm, inv_v, buf0, buf1, g0, g1, w0, w1, dest_v):
        wid = lax.axis_index("s") * SC_CORES + lax.axis_index("c")
        base = pl.multiple_of(wid * rows_w, SUBLANES)
        pltpu.sync_copy(dest_hbm, dest_v)
        lane = lax.iota(jnp.int32, SC_LANES)

        @pl.loop(0, rows_w // SC_LANES)
        def _(i):
            r = base + i * SC_LANES + lane
            r = jnp.where(r >= t_all, r - t_all, r)
            r = jnp.where(r >= t_all, r - t_all, r)
            inv_v[pl.ds(pl.multiple_of(i * SC_LANES, SC_LANES), SC_LANES)] = r

        @pl.loop(0, n_ent // SC_LANES)
        def _(i):
            ent = i * SC_LANES + lane
            local = dest_v[pl.ds(pl.multiple_of(i * SC_LANES, SC_LANES), SC_LANES)] - base
            mine = (local >= 0) & (local < rows_w)
            tok = jnp.where(ent >= t_all, ent - t_all, ent)
            plsc.store_scatter(inv_v, [jnp.where(mine, local, 0)], tok, mask=mine)

        _sc_stream_rows(table_hbm, inv_v, out_hbm, base, n_chunks, chunk, (buf0, buf1), (g0, g1), (w0, w1))

    return pl.kernel(
        body,
        out_type=jax.ShapeDtypeStruct((n_out, w), table.dtype),
        mesh=_sc_mesh(),
        scratch_types=_sc_scratch(rows_w, chunk, w, table.dtype) + [pltpu.VMEM((n_ent,), jnp.int32)],
        compiler_params=pltpu.CompilerParams(use_tc_tiling_on_sc=True, needs_layout_passes=False),
        name="sc_dispatch",
    )(table, dest)


def _experts_kernel(tile_e_ref, tile_first_ref, n_tiles_ref, xs_ref, wg_ref, wu_ref, wd_ref,
                    y_ref, wg_bf, wu_bf, wd_bf):
    i = pl.program_id(0)

    @pl.when(tile_first_ref[i] == 1)
    def _():
        wg_bf[...] = wg_ref[0].astype(_BF16)
        wu_bf[...] = wu_ref[0].astype(_BF16)
        wd_bf[...] = wd_ref[0].astype(_BF16)

    @pl.when(i < n_tiles_ref[0])
    def _():
        x = _unpack_bf16_pair(xs_ref[...]).astype(_BF16)
        a = _dot(x, wg_bf[...])
        u = _dot(x, wu_bf[...])
        h = (a * jax.nn.sigmoid(a) * u).astype(_BF16)
        y_ref[...] = _pack_bf16_pair(_dot(h, wd_bf[...]))


def _experts(tile_e, tile_first, n_tiles, xs, w_gate, w_up, w_down):
    p, dh = xs.shape
    d = 2 * dh
    tm = EXP_TM
    row = lambda i, te, tf, nt: (jnp.minimum(i, nt[0] - 1), 0)
    wsel = lambda i, te, tf, nt: (te[i], 0, 0)
    grid_spec = pltpu.PrefetchScalarGridSpec(
        num_scalar_prefetch=3,
        grid=(p // tm,),
        in_specs=[pl.BlockSpec((tm, dh), row),
                  pl.BlockSpec((1, d, D_EXPERT), wsel),
                  pl.BlockSpec((1, d, D_EXPERT), wsel),
                  pl.BlockSpec((1, D_EXPERT, d), wsel)],
        out_specs=pl.BlockSpec((tm, dh), row),
        scratch_shapes=[pltpu.VMEM((d, D_EXPERT), _BF16), pltpu.VMEM((d, D_EXPERT), _BF16),
                        pltpu.VMEM((D_EXPERT, d), _BF16)],
    )
    return pl.pallas_call(
        _experts_kernel,
        grid_spec=grid_spec,
        out_shape=jax.ShapeDtypeStruct((p, dh), _U32),
        compiler_params=pltpu.CompilerParams(
            dimension_semantics=("arbitrary",), vmem_limit_bytes=VMEM_LIMIT),
        name="experts",
    )(tile_e, tile_first, n_tiles, xs, w_gate, w_up, w_down)


def _final_kernel(x1_ref, ya_ref, yb_ref, route_ref, gf_ref, y_ref):
    route = route_ref[...]
    x2 = (x1_ref[...] + route[:, R_W1:R_W1 + 1] * _unpack_bf16_pair(ya_ref[...])
          + route[:, R_W2:R_W2 + 1] * _unpack_bf16_pair(yb_ref[...]))
    y_ref[...] = _rms(x2, gf_ref[...])


def _final(x1, yab, route, gf, n_prompt, n_sample):
    d = x1.shape[1]

    def call(tm, first_block, n_rows, name):
        tok = lambda w: pl.BlockSpec((tm, w), lambda i: (first_block + i, 0))
        sel = lambda k: pl.BlockSpec((None, tm, d // 2), lambda i: (k, first_block + i, 0))
        return pl.pallas_call(
            _final_kernel,
            grid=(n_rows // tm,),
            in_specs=[tok(d), sel(0), sel(1), tok(LANES), pl.BlockSpec((1, d), lambda i: (0, 0))],
            out_specs=pl.BlockSpec((tm, d), lambda i: (i, 0)),
            out_shape=jax.ShapeDtypeStruct((n_rows, d), _F32),
            compiler_params=pltpu.CompilerParams(
                dimension_semantics=("parallel",), vmem_limit_bytes=VMEM_LIMIT),
            name=name,
        )(x1, yab, yab, route, gf)

    return (call(FINAL_TM, 0, n_prompt, "final_prompt"),
            call(n_sample, n_prompt // n_sample, n_sample, "final_sample"))


def _ssm_params(lam_re, lam_im, log_dt, b_re, b_im, c_re, c_im, d_skip):
    dt = jnp.exp(log_dt)[:, None]
    mag = jnp.exp(lam_re * dt)
    ang = lam_im * dt
    lb_re = mag * jnp.cos(ang)
    lb_im = mag * jnp.sin(ang)
    den = lam_re * lam_re + lam_im * lam_im
    nr = lb_re - 1.0
    ni = lb_im
    k_re = (nr * lam_re + ni * lam_im) / den
    k_im = (ni * lam_re - nr * lam_im) / den
    bb_re = k_re[:, :, None] * b_re - k_im[:, :, None] * b_im
    bb_im = k_re[:, :, None] * b_im + k_im[:, :, None] * b_re
    eye = jnp.eye(SUBLANES, dtype=_F32)

    def b_blocks(bb):
        t = jnp.transpose(bb, (0, 2, 1)).reshape(N_LANE_TILES, 8, SSM_GROUP, SSM_STATE)
        return jnp.einsum('kahp,ab->kahbp', t, eye).reshape(N_LANE_TILES, LANES, TILE_STATE)

    def c_blocks(c):
        t = c.reshape(N_LANE_TILES, 8, SSM_GROUP, SSM_STATE)
        return jnp.einsum('kahp,ab->kapbh', t, eye).reshape(N_LANE_TILES, TILE_STATE, LANES)

    wb = jnp.concatenate([b_blocks(bb_re), b_blocks(bb_im)], axis=-1)
    wc = jnp.concatenate([c_blocks(c_re), -c_blocks(c_im)], axis=1)
    return (wb, wc, lb_re.reshape(1, STATE_COLS), lb_im.reshape(1, STATE_COLS),
            d_skip.reshape(1, D_SSM))


def _dispatch_plan(route, cnt, tm):
    t_all = route.shape[0]
    e = route[:, R_E1:R_E2 + 1].astype(jnp.int32)
    rank = route[:, R_RANK1:R_RANK2 + 1].astype(jnp.int32)
    counts = cnt[0, :N_EXPERTS].astype(jnp.int32)
    tiles_per = (counts + tm - 1) // tm
    tile_end = jnp.cumsum(tiles_per)
    start_row = (tile_end - tiles_per) * tm
    ids = jnp.arange(N_EXPERTS, dtype=jnp.int32)
    dest = jnp.sum(jnp.where(e[:, :, None] == ids, start_row, 0), axis=-1) + rank
    dest = jnp.concatenate([dest[:, 0], dest[:, 1]])
    n_tiles_max = (2 * t_all + N_EXPERTS * (tm - 1)) // tm
    n_tiles_max += n_tiles_max % 2
    tile_ids = jnp.arange(n_tiles_max, dtype=jnp.int32)
    tile_e = jnp.minimum(jnp.sum((tile_ids[:, None] >= tile_end[None, :]).astype(jnp.int32), axis=1),
                         N_EXPERTS - 1)
    tile_first = jnp.concatenate([jnp.ones((1,), jnp.int32), (tile_e[1:] != tile_e[:-1]).astype(jnp.int32)])
    n_tiles = tile_end[-1:].astype(jnp.int32)
    return dest, n_tiles_max * tm, tile_e, tile_first, n_tiles


def kernel(x_prompt, x_sample, state_ssm_re, state_ssm_im, norm1_g, w_in, lam_re, lam_im, log_dt, ssm_b_re, ssm_b_im, ssm_c_re, ssm_c_im, ssm_d, gmlp_norm_g, gmlp_w_s, gmlp_b_s, out_norm_ssm_g, out_norm_gmlp_g, w_out, norm2_g, w_router_group, b_router_group, w_router_expert, b_router_expert, w_gate, w_up, w_down, final_norm_g):
    n, l, d = x_prompt.shape
    ns = x_sample.shape[0]
    t_all = n * l + ns
    li = 0
    g1 = norm1_g[li].reshape(1, d)
    gn = gmlp_norm_g[li].reshape(1, D_GMLP)
    tril = jnp.tril(jnp.ones((CHUNK, CHUNK), dtype=bool))
    ws_tril = jnp.where(tril[None], gmlp_w_s[li], 0.0)
    bs = gmlp_b_s[li]
    gog = out_norm_gmlp_g[li].reshape(1, D_GMLP)
    gos = out_norm_ssm_g[li].reshape(1, D_SSM)
    wb, wc, lbr, lbi, dsk = _ssm_params(lam_re[li], lam_im[li], log_dt[li], ssm_b_re[li], ssm_b_im[li],
                                        ssm_c_re[li], ssm_c_im[li], ssm_d[li])
    g2 = norm2_g[li].reshape(1, d)
    pad = LANES - N_EXPERTS - N_EXPERT_GROUPS
    wr = jnp.concatenate([w_router_expert[li], w_router_group[li], jnp.zeros((d, pad), _F32)], axis=1)
    br = jnp.concatenate([b_router_expert[li], b_router_group[li], jnp.zeros((pad,), _F32)]).reshape(1, LANES)

    xa, sg, mixb = _front_prompt(x_prompt, g1, w_in[li].astype(_BF16), gn, ws_tril.astype(_BF16), bs.T, gog)
    mixa, hfin = _ssm_prompt(xa, sg, wb.astype(_BF16), wc.astype(_BF16), lbr, lbi, dsk, gos)
    w00 = jnp.repeat(ws_tril[:, 0, 0], GMLP_HEAD).reshape(1, D_GMLP)
    b0 = jnp.repeat(bs[:, 0], GMLP_HEAD).reshape(1, D_GMLP)
    mix_s, hr_s, hi_s, vrow = _front_sample(
        x_sample.reshape(ns, d), g1, w_in[li], gn, w00, b0, gog, wb, wc, lbr, lbi, dsk, gos,
        state_ssm_re[li].reshape(ns, STATE_COLS), state_ssm_im[li].reshape(ns, STATE_COLS))

    x1, xn, route, cnt = _mixer_out(x_prompt, mixa, mixb, x_sample.reshape(ns, d), mix_s,
                                    w_out[li], g2, wr, br)
    dest, n_rows, tile_e, tile_first, n_tiles = _dispatch_plan(route, cnt, EXP_TM)
    xs = _sc_dispatch(xn, dest, n_rows, DISPATCH_CHUNK)
    ys = _experts(tile_e, tile_first, n_tiles, xs, w_gate[li], w_up[li], w_down[li])
    yab = _sc_combine(ys, dest, COMBINE_CHUNK).reshape(2, t_all, d // 2)
    y_p, y_s = _final(x1, yab, route, final_norm_g.reshape(1, d), n * l, ns)

    hf = hfin.reshape(n, N_LANE_TILES, 2, 8, SSM_STATE)
    re_p = hf[:, :, 0].reshape(1, n, N_SSM_GROUPS, SSM_STATE)
    im_p = hf[:, :, 1].reshape(1, n, N_SSM_GROUPS, SSM_STATE)
    re_s = hr_s.reshape(1, ns, N_SSM_GROUPS, SSM_STATE)
    im_s = hi_s.reshape(1, ns, N_SSM_GROUPS, SSM_STATE)
    return (y_p.reshape(n, l, d), y_s.reshape(ns, 1, d), re_p, im_p, re_s, im_s,
            vrow.reshape(1, ns, 1, D_GMLP))
```

```python
import math

import jax
import jax.numpy as jnp
from jax import lax
from jax.experimental import pallas as pl
from jax.experimental.pallas import tpu as pltpu
from jax.experimental.pallas import tpu_sc as plsc

D_MODEL = 1024
D_SSM = 512
D_GMLP = 512
SSM_GROUP = 16
N_SSM_GROUPS = 32
SSM_STATE = 64
CHUNK = 128
N_GMLP_HEADS = 4
GMLP_HEAD = 128
N_EXPERT_GROUPS = 4
EXPERTS_PER_GROUP = 8
N_EXPERTS = 32
D_EXPERT = 512
D_IN = 2048
EPS = 1e-6

LANES = 128
SUBLANES = 8
N_LANE_TILES = D_SSM // LANES
STATE_COLS = N_SSM_GROUPS * SSM_STATE
TILE_STATE = STATE_COLS // N_LANE_TILES
VMEM_LIMIT = 56 * 1024 * 1024

SC_CORES = 2
SC_SUBCORES = 16
SC_LANES = 16
SC_WORKERS = SC_CORES * SC_SUBCORES

FRONT_TL = 512
SSM_LC = 128
TOK_TM = 512
FINAL_TM = 512
EXP_TM = 512
DISPATCH_CHUNK = 64
COMBINE_CHUNK = 24

R_E1, R_E2, R_W1, R_W2, R_RANK1, R_RANK2, R_CODE1, R_CODE2 = 0, 1, 2, 3, 4, 5, 6, 7
CODE_BITS = 16
CODE_SHIFT = float(1 << CODE_BITS)

_INV_SQRT2 = 1.0 / math.sqrt(2.0)
_BF16 = jnp.bfloat16
_F32 = jnp.float32
_U32 = jnp.uint32


def _gelu(x):
    return 0.5 * x * (1.0 + lax.erf(x * _INV_SQRT2))


def _rms(x, g):
    return x * lax.rsqrt(jnp.mean(x * x, axis=-1, keepdims=True) + EPS) * g


def _dot(a, b):
    return jnp.dot(a, b, preferred_element_type=_F32)


def _dot_f32(a, b):
    return jnp.dot(a, b, preferred_element_type=_F32, precision=lax.Precision.HIGHEST)


def _pack_bf16_pair(x):
    w = x.shape[1] // 2
    hi = lax.bitcast_convert_type(x[:, :w].astype(_BF16).astype(_F32), _U32)
    lo = lax.bitcast_convert_type(x[:, w:].astype(_BF16).astype(_F32), _U32)
    return hi | (lo >> 16)


def _unpack_bf16_pair(p):
    hi = lax.bitcast_convert_type(p & jnp.uint32(0xFFFF0000), _F32)
    lo = lax.bitcast_convert_type(p << 16, _F32)
    return jnp.concatenate([hi, lo], axis=-1)


def _head_norm_gelu(vb, gn):
    v = _gelu(vb)
    parts = []
    for h in range(N_GMLP_HEADS):
        vh = v[:, h * GMLP_HEAD:(h + 1) * GMLP_HEAD]
        parts.append(vh * lax.rsqrt(jnp.mean(vh * vh, axis=-1, keepdims=True) + EPS))
    return jnp.concatenate(parts, axis=-1) * gn


def _front_prompt_kernel(x_ref, g1_ref, win_ref, gn_ref, ws_ref, bs_ref, gog_ref,
                         xa_ref, sg_ref, mixb_ref):
    x = x_ref[0]
    hn = _rms(x, g1_ref[...]).astype(_BF16)
    z = _dot(hn, win_ref[...])
    xa_ref[0] = z[:, :D_SSM]
    sg_ref[0] = jax.nn.sigmoid(z[:, D_SSM:2 * D_SSM])
    ub = _gelu(z[:, 2 * D_SSM:2 * D_SSM + D_GMLP])
    vbn = _head_norm_gelu(z[:, 2 * D_SSM + D_GMLP:], gn_ref[...]).astype(_BF16)
    tl = x.shape[0]
    rows = []
    for c in range(tl // CHUNK):
        heads = []
        for h in range(N_GMLP_HEADS):
            vh = vbn[c * CHUNK:(c + 1) * CHUNK, h * GMLP_HEAD:(h + 1) * GMLP_HEAD]
            heads.append(_dot(ws_ref[h], vh) + bs_ref[:, h:h + 1])
        rows.append(jnp.concatenate(heads, axis=-1))
    s = jnp.concatenate(rows, axis=0)
    mixb_ref[0] = _rms(ub * s, gog_ref[...]).astype(_BF16)


def _front_prompt(x, g1, win_bf, gn, ws_tril_bf, bs_t, gog):
    n, l, d = x.shape
    tl = FRONT_TL
    grid = (n, l // tl)
    const = lambda *shape: pl.BlockSpec(shape, lambda b, i: (0,) * len(shape))
    seq = lambda w: pl.BlockSpec((1, tl, w), lambda b, i: (b, i, 0))
    return pl.pallas_call(
        _front_prompt_kernel,
        grid=grid,
        in_specs=[seq(d), const(1, d), const(d, D_IN), const(1, D_GMLP),
                  const(N_GMLP_HEADS, CHUNK, CHUNK), const(CHUNK, N_GMLP_HEADS), const(1, D_GMLP)],
        out_specs=[seq(D_SSM), seq(D_SSM), seq(D_GMLP)],
        out_shape=[jax.ShapeDtypeStruct((n, l, D_SSM), _F32),
                   jax.ShapeDtypeStruct((n, l, D_SSM), _F32),
                   jax.ShapeDtypeStruct((n, l, D_GMLP), _BF16)],
        compiler_params=pltpu.CompilerParams(
            dimension_semantics=("parallel", "parallel"), vmem_limit_bytes=VMEM_LIMIT),
        name="front_prompt",
    )(x, g1, win_bf, gn, ws_tril_bf, bs_t, gog)


def _ssm_prompt_kernel(xa_ref, sg_ref, wb_ref, wc_ref, lbr_ref, lbi_ref, dsk_ref, gos_ref,
                       mixa_ref, hfin_ref, bu_ref, st_ref):
    lc = xa_ref.shape[1]
    rows = lc * SUBLANES
    pair = 2 * SUBLANES

    @pl.when(pl.program_id(0) == 0)
    def _():
        st_ref[...] = jnp.zeros_like(st_ref)

    xa = pltpu.einshape("btc->tbc", xa_ref[...]).reshape(rows, D_SSM)
    xa_bf = xa.astype(_BF16)
    for k in range(N_LANE_TILES):
        bu_ref[:, 2 * TILE_STATE * k:2 * TILE_STATE * (k + 1)] = _dot(
            xa_bf[:, k * LANES:(k + 1) * LANES], wb_ref[k])

    for kk in range(0, N_LANE_TILES, 2):
        tiles = (kk, kk + 1)
        cols = [(2 * TILE_STATE * k, 2 * TILE_STATE * k + TILE_STATE) for k in tiles]
        lbs = [(jnp.broadcast_to(lbr_ref[:, k * TILE_STATE:(k + 1) * TILE_STATE], (SUBLANES, TILE_STATE)),
                jnp.broadcast_to(lbi_ref[:, k * TILE_STATE:(k + 1) * TILE_STATE], (SUBLANES, TILE_STATE)))
               for k in tiles]

        def body(j, carry, cols=cols, lbs=lbs):
            r0 = pl.multiple_of(j * pair, pair)
            r1 = pl.multiple_of(r0 + SUBLANES, SUBLANES)
            out = []
            for q, ((c_re, c_im), (lr, li)) in enumerate(zip(cols, lbs)):
                hr, hi = carry[2 * q], carry[2 * q + 1]
                ar = lr * hr - li * hi + bu_ref[pl.ds(r0, SUBLANES), c_re:c_re + TILE_STATE]
                ai = lr * hi + li * hr + bu_ref[pl.ds(r0, SUBLANES), c_im:c_im + TILE_STATE]
                br = lr * ar - li * ai + bu_ref[pl.ds(r1, SUBLANES), c_re:c_re + TILE_STATE]
                bi = lr * ai + li * ar + bu_ref[pl.ds(r1, SUBLANES), c_im:c_im + TILE_STATE]
                bu_ref[pl.ds(r0, SUBLANES), c_re:c_re + TILE_STATE] = ar
                bu_ref[pl.ds(r0, SUBLANES), c_im:c_im + TILE_STATE] = ai
                bu_ref[pl.ds(r1, SUBLANES), c_re:c_re + TILE_STATE] = br
                bu_ref[pl.ds(r1, SUBLANES), c_im:c_im + TILE_STATE] = bi
                out += [br, bi]
            return tuple(out)

        init = tuple(st_ref[:, c:c + TILE_STATE] for c_pair in cols for c in c_pair)
        fin = lax.fori_loop(0, lc // 2, body, init, unroll=2)
        for q, (c_re, c_im) in enumerate(cols):
            st_ref[:, c_re:c_re + TILE_STATE] = fin[2 * q]
            st_ref[:, c_im:c_im + TILE_STATE] = fin[2 * q + 1]

    ys = []
    for k in range(N_LANE_TILES):
        hk = bu_ref[:, 2 * TILE_STATE * k:2 * TILE_STATE * (k + 1)].astype(_BF16)
        ys.append(_dot(hk, wc_ref[k]))
    y = jnp.concatenate(ys, axis=-1) + dsk_ref[...] * xa
    sg = pltpu.einshape("btc->tbc", sg_ref[...]).reshape(rows, D_SSM)
    mixa = _rms(_gelu(y) * sg, gos_ref[...]).reshape(lc, SUBLANES, D_SSM)
    mixa_ref[...] = pltpu.einshape("tbc->btc", mixa).astype(_BF16)
    hfin_ref[...] = st_ref[...]


def _ssm_prompt(xa, sg, wb, wc, lbr, lbi, dsk, gos):
    n, l, _ = xa.shape
    lc = SSM_LC
    const = lambda *shape: pl.BlockSpec(shape, lambda i: (0,) * len(shape))
    seq_spec = pl.BlockSpec((n, lc, D_SSM), lambda i: (0, i, 0))
    return pl.pallas_call(
        _ssm_prompt_kernel,
        grid=(l // lc,),
        in_specs=[seq_spec, seq_spec,
                  const(N_LANE_TILES, LANES, 2 * TILE_STATE), const(N_LANE_TILES, 2 * TILE_STATE, LANES),
                  const(1, STATE_COLS), const(1, STATE_COLS), const(1, D_SSM), const(1, D_SSM)],
        out_specs=[seq_spec, const(n, 2 * STATE_COLS)],
        out_shape=[jax.ShapeDtypeStruct((n, l, D_SSM), _BF16),
                   jax.ShapeDtypeStruct((n, 2 * STATE_COLS), _F32)],
        scratch_shapes=[pltpu.VMEM((lc * n, 2 * STATE_COLS), _F32),
                        pltpu.VMEM((n, 2 * STATE_COLS), _F32)],
        compiler_params=pltpu.CompilerParams(
            dimension_semantics=("arbitrary",), vmem_limit_bytes=VMEM_LIMIT),
        name="ssm_prompt",
    )(xa, sg, wb, wc, lbr, lbi, dsk, gos)


def _front_sample_kernel(x_ref, g1_ref, win_ref, gn_ref, w00_ref, b0_ref, gog_ref,
                         wb_ref, wc_ref, lbr_ref, lbi_ref, dsk_ref, gos_ref, h0r_ref, h0i_ref,
                         mix_ref, hr_ref, hi_ref, vrow_ref):
    x = x_ref[...]
    hn = _rms(x, g1_ref[...])
    z = _dot_f32(hn, win_ref[...])
    xa = z[:, :D_SSM]
    ys = []
    for k in range(N_LANE_TILES):
        bu = _dot_f32(xa[:, k * LANES:(k + 1) * LANES], wb_ref[k])
        sl = slice(k * TILE_STATE, (k + 1) * TILE_STATE)
        lr, li = lbr_ref[:, sl], lbi_ref[:, sl]
        h0r, h0i = h0r_ref[:, sl], h0i_ref[:, sl]
        nr = lr * h0r - li * h0i + bu[:, :TILE_STATE]
        ni = lr * h0i + li * h0r + bu[:, TILE_STATE:]
        hr_ref[:, sl] = nr
        hi_ref[:, sl] = ni
        ys.append(_dot_f32(jnp.concatenate([nr, ni], axis=-1), wc_ref[k]))
    y = jnp.concatenate(ys, axis=-1) + dsk_ref[...] * xa
    ya = _gelu(y) * jax.nn.sigmoid(z[:, D_SSM:2 * D_SSM])
    mix_ref[:, :D_SSM] = _rms(ya, gos_ref[...])
    ub = _gelu(z[:, 2 * D_SSM:2 * D_SSM + D_GMLP])
    vbn = _head_norm_gelu(z[:, 2 * D_SSM + D_GMLP:], gn_ref[...])
    vrow_ref[...] = vbn
    s = w00_ref[...] * vbn + b0_ref[...]
    mix_ref[:, D_SSM:] = _rms(ub * s, gog_ref[...])


def _front_sample(x, g1, win, gn, w00, b0, gog, wb, wc, lbr, lbi, dsk, gos, h0r, h0i):
    n = x.shape[0]
    vmem = pl.BlockSpec(memory_space=pltpu.VMEM)
    return pl.pallas_call(
        _front_sample_kernel,
        in_specs=[vmem] * 15,
        out_specs=[vmem] * 4,
        out_shape=[jax.ShapeDtypeStruct((n, D_MODEL), _F32),
                   jax.ShapeDtypeStruct((n, STATE_COLS), _F32),
                   jax.ShapeDtypeStruct((n, STATE_COLS), _F32),
                   jax.ShapeDtypeStruct((n, D_GMLP), _F32)],
        compiler_params=pltpu.CompilerParams(vmem_limit_bytes=VMEM_LIMIT),
        name="front_sample",
    )(x, g1, win, gn, w00, b0, gog, wb, wc, lbr, lbi, dsk, gos, h0r, h0i)


def _route(logits, base):
    tm = logits.shape[0]
    lane = lax.broadcasted_iota(jnp.int32, logits.shape, 1).astype(_F32)
    neg = jnp.float32(-jnp.inf)
    big = jnp.float32(LANES)
    is_g = (lane >= N_EXPERTS) & (lane < N_EXPERTS + N_EXPERT_GROUPS)
    gl = jnp.where(is_g, logits, neg)
    gmax = jnp.max(gl, axis=-1, keepdims=True)
    gi = jnp.min(jnp.where(is_g & (logits == gmax), lane, big), axis=-1, keepdims=True) - N_EXPERTS
    p_top = 1.0 / jnp.sum(jnp.where(is_g, jnp.exp(gl - gmax), 0.0), axis=-1, keepdims=True)
    lo = gi * EXPERTS_PER_GROUP
    in_grp = (lane >= lo) & (lane < lo + EXPERTS_PER_GROUP)
    m1 = jnp.max(jnp.where(in_grp, logits, neg), axis=-1, keepdims=True)
    i1 = jnp.min(jnp.where(in_grp & (logits == m1), lane, big), axis=-1, keepdims=True)
    rest = in_grp & (lane != i1)
    m2 = jnp.max(jnp.where(rest, logits, neg), axis=-1, keepdims=True)
    i2 = jnp.min(jnp.where(rest & (logits == m2), lane, big), axis=-1, keepdims=True)
    e2 = jnp.exp(m2 - m1)
    w1 = p_top / (1.0 + e2)
    w2 = p_top * e2 / (1.0 + e2)
    sel1 = lane == i1
    sel2 = lane == i2
    hits = jnp.where(sel1 | sel2, 1.0, 0.0)
    r_id = lax.broadcasted_iota(jnp.int32, (tm, tm), 0)
    c_id = lax.broadcasted_iota(jnp.int32, (tm, tm), 1)
    ltri = jnp.where(c_id < r_id, 1.0, 0.0).astype(_BF16)
    before = _dot(ltri, hits.astype(_BF16)) + base
    rank1 = jnp.sum(jnp.where(sel1, before, 0.0), axis=-1, keepdims=True)
    rank2 = jnp.sum(jnp.where(sel2, before, 0.0), axis=-1, keepdims=True)
    out = jnp.where(lane == R_E1, i1, 0.0)
    out = jnp.where(lane == R_E2, i2, out)
    out = jnp.where(lane == R_W1, w1, out)
    out = jnp.where(lane == R_W2, w2, out)
    out = jnp.where(lane == R_RANK1, rank1, out)
    out = jnp.where(lane == R_RANK2, rank2, out)
    out = jnp.where(lane == R_CODE1, i1 * CODE_SHIFT + rank1, out)
    out = jnp.where(lane == R_CODE2, i2 * CODE_SHIFT + rank2, out)
    return out, base + jnp.sum(hits, axis=0, keepdims=True)


def _mixer_out_prompt_kernel(x_ref, mixa_ref, mixb_ref, wo_ref, g2_ref, wr_ref, br_ref,
                             x1_ref, xn_ref, route_ref, route_t_ref, cnt_ref, base_ref):
    @pl.when((pl.program_id(0) == 0) & (pl.program_id(1) == 0))
    def _():
        base_ref[...] = jnp.zeros_like(base_ref)

    x1 = x_ref[0] + _dot(mixa_ref[0], wo_ref[:D_SSM, :]) + _dot(mixb_ref[0], wo_ref[D_SSM:, :])
    xn = _rms(x1, g2_ref[...])
    logits = _dot(xn.astype(_BF16), wr_ref[...]) + br_ref[...]
    route, base = _route(logits, base_ref[...])
    x1_ref[...] = x1
    xn_ref[...] = _pack_bf16_pair(xn)
    route_ref[...] = route
    route_t_ref[...] = route.T[:SUBLANES, :]
    base_ref[...] = base
    cnt_ref[...] = base


def _mixer_out_sample_kernel(x_ref, mix_ref, wo_ref, g2_ref, wr_ref, br_ref, cnt_in_ref,
                             x1_in, xn_in, route_in, route_t_in,
                             x1_ref, xn_ref, route_ref, route_t_ref, cnt_ref):
    del x1_in, xn_in, route_in, route_t_in
    x1 = (x_ref[...] + _dot_f32(mix_ref[:, :D_SSM], wo_ref[:D_SSM, :])
          + _dot_f32(mix_ref[:, D_SSM:], wo_ref[D_SSM:, :]))
    xn = _rms(x1, g2_ref[...])
    logits = _dot_f32(xn, wr_ref[...]) + br_ref[...]
    route, base = _route(logits, cnt_in_ref[...])
    x1_ref[...] = x1
    xn_ref[...] = _pack_bf16_pair(xn)
    route_ref[...] = route
    route_t_ref[...] = route.T[:SUBLANES, :]
    cnt_ref[...] = base


def _mixer_out(x_p, mixa, mixb, x_s, mix_s, wo, g2, wr, br):
    n, l, d = x_p.shape
    ns = x_s.shape[0]
    t_all = n * l + ns
    tm = TOK_TM
    per_seq = l // tm
    const = lambda *shape: pl.BlockSpec(shape, lambda b, i: (0,) * len(shape))
    seq = lambda w: pl.BlockSpec((1, tm, w), lambda b, i: (b, i, 0))
    tok = lambda w: pl.BlockSpec((tm, w), lambda b, i: (b * per_seq + i, 0))
    tok_shapes = [jax.ShapeDtypeStruct((t_all, d), _F32),
                  jax.ShapeDtypeStruct((t_all, d // 2), _U32),
                  jax.ShapeDtypeStruct((t_all, LANES), _F32),
                  jax.ShapeDtypeStruct((SUBLANES, t_all), _F32)]
    cnt_shape = jax.ShapeDtypeStruct((1, LANES), _F32)
    x1, xn, route, route_t, cnt = pl.pallas_call(
        _mixer_out_prompt_kernel,
        grid=(n, per_seq),
        in_specs=[seq(d), seq(D_SSM), seq(D_GMLP),
                  const(d, d), const(1, d), const(d, LANES), const(1, LANES)],
        out_specs=[tok(d), tok(d // 2), tok(LANES),
                   pl.BlockSpec((SUBLANES, tm), lambda b, i: (0, b * per_seq + i)), const(1, LANES)],
        out_shape=tok_shapes + [cnt_shape],
        scratch_shapes=[pltpu.VMEM((1, LANES), _F32)],
        compiler_params=pltpu.CompilerParams(
            dimension_semantics=("arbitrary", "arbitrary"), vmem_limit_bytes=VMEM_LIMIT),
        name="mixer_out_prompt",
    )(x_p, mixa, mixb, wo.astype(_BF16), g2, wr.astype(_BF16), br)
    tail = (n * l) // ns
    c1 = lambda *shape: pl.BlockSpec(shape, lambda i: (0,) * len(shape))
    anyspec = pl.BlockSpec(memory_space=pl.ANY)
    tail_spec = lambda w: pl.BlockSpec((ns, w), lambda i: (tail, 0))
    return pl.pallas_call(
        _mixer_out_sample_kernel,
        grid=(1,),
        in_specs=[c1(ns, d), c1(ns, d), c1(d, d), c1(1, d), c1(d, LANES), c1(1, LANES), c1(1, LANES),
                  anyspec, anyspec, anyspec, anyspec],
        out_specs=[tail_spec(d), tail_spec(d // 2), tail_spec(LANES),
                   pl.BlockSpec((SUBLANES, ns), lambda i: (0, tail)), c1(1, LANES)],
        out_shape=tok_shapes + [cnt_shape],
        input_output_aliases={7: 0, 8: 1, 9: 2, 10: 3},
        compiler_params=pltpu.CompilerParams(
            dimension_semantics=("arbitrary",), vmem_limit_bytes=VMEM_LIMIT),
        name="mixer_out_sample",
    )(x_s, mix_s, wo, g2, wr, br, cnt, x1, xn, route, route_t)


def _sc_stream_rows(table_hbm, idx_v, out_hbm, base, n_chunks, chunk, bufs, gsems, wsems):
    def gather(j):
        return pltpu.make_async_copy(table_hbm.at[idx_v.at[pl.ds(j * chunk, chunk)]], bufs[j % 2], gsems[j % 2])

    def write(j):
        return pltpu.make_async_copy(bufs[j % 2], out_hbm.at[pl.ds(base + j * chunk, chunk)], wsems[j % 2])

    gather(0).start()
    for j in range(n_chunks):
        if j + 1 < n_chunks:
            if j >= 1:
                write(j - 1).wait()
            gather(j + 1).start()
        gather(j).wait()
        write(j).start()
    if n_chunks >= 2:
        write(n_chunks - 2).wait()
    write(n_chunks - 1).wait()


def _sc_mesh():
    return plsc.VectorSubcoreMesh(core_axis_name="c", subcore_axis_name="s",
                                  num_cores=SC_CORES, num_subcores=SC_SUBCORES)


def _sc_scratch(n_idx, chunk, w, dtype):
    return ([pltpu.VMEM((n_idx,), jnp.int32), pltpu.VMEM((chunk, w), dtype), pltpu.VMEM((chunk, w), dtype)]
            + [pltpu.SemaphoreType.DMA] * 4)


def _sc_combine(table, idx, chunk):
    n_out = idx.shape[0]
    w = table.shape[1]
    rows_w = n_out // SC_WORKERS
    n_chunks = rows_w // chunk
    assert rows_w * SC_WORKERS == n_out and n_chunks * chunk == rows_w and rows_w % SUBLANES == 0

    def body(table_hbm, idx_hbm, out_hbm, idx_v, buf0, buf1, g0, g1, w0, w1):
        wid = lax.axis_index("s") * SC_CORES + lax.axis_index("c")
        base = pl.multiple_of(wid * rows_w, SUBLANES)
        pltpu.sync_copy(idx_hbm.at[pl.ds(base, rows_w)], idx_v)
        _sc_stream_rows(table_hbm, idx_v, out_hbm, base, n_chunks, chunk, (buf0, buf1), (g0, g1), (w0, w1))

    return pl.kernel(
        body,
        out_type=jax.ShapeDtypeStruct((n_out, w), table.dtype),
        mesh=_sc_mesh(),
        scratch_types=_sc_scratch(rows_w, chunk, w, table.dtype),
        compiler_params=pltpu.CompilerParams(use_tc_tiling_on_sc=True),
        name="sc_combine",
    )(table, idx)


def _sc_dispatch(table, codes, start_row, n_out, chunk):
    t_all, w = table.shape
    n_ent = codes.shape[0]
    rows_w = n_out // SC_WORKERS
    ent_w = n_ent // SC_WORKERS
    n_chunks = rows_w // chunk
    assert rows_w * SC_WORKERS == n_out and n_chunks * chunk == rows_w and ent_w * SC_WORKERS == n_ent
    assert rows_w % SC_LANES == 0 and n_ent % SC_LANES == 0 and ent_w % SUBLANES == 0
    assert n_ent == 2 * t_all and n_out < 3 * t_all

    def body(table_hbm, code_hbm, start_hbm, out_hbm, dest_hbm,
             inv_v, buf0, buf1, g0, g1, w0, w1, dest_v, start_v):
        wid = lax.axis_index("s") * SC_CORES + lax.axis_index("c")
        base = pl.multiple_of(wid * rows_w, SUBLANES)
        pltpu.sync_copy(code_hbm, dest_v)
        pltpu.sync_copy(start_hbm, start_v)
        lane = lax.iota(jnp.int32, SC_LANES)

        @pl.loop(0, rows_w // SC_LANES)
        def _(i):
            r = base + i * SC_LANES + lane
            r = jnp.where(r >= t_all, r - t_all, r)
            r = jnp.where(r >= t_all, r - t_all, r)
            inv_v[pl.ds(pl.multiple_of(i * SC_LANES, SC_LANES), SC_LANES)] = r

        @pl.loop(0, n_ent // SC_LANES)
        def _(i):
            sl = pl.ds(pl.multiple_of(i * SC_LANES, SC_LANES), SC_LANES)
            ent = i * SC_LANES + lane
            code = dest_v[sl]
            d = plsc.load_gather(start_v, [code >> CODE_BITS]) + (code & ((1 << CODE_BITS) - 1))
            dest_v[sl] = d
            local = d - base
            mine = (local >= 0) & (local < rows_w)
            tok = jnp.where(ent >= t_all, ent - t_all, ent)
            plsc.store_scatter(inv_v, [jnp.where(mine, local, 0)], tok, mask=mine)

        ebase = pl.multiple_of(wid * ent_w, SUBLANES)
        pltpu.sync_copy(dest_v.at[pl.ds(ebase, ent_w)], dest_hbm.at[pl.ds(ebase, ent_w)])
        _sc_stream_rows(table_hbm, inv_v, out_hbm, base, n_chunks, chunk, (buf0, buf1), (g0, g1), (w0, w1))

    return pl.kernel(
        body,
        out_type=(jax.ShapeDtypeStruct((n_out, w), table.dtype), jax.ShapeDtypeStruct((n_ent,), jnp.int32)),
        mesh=_sc_mesh(),
        scratch_types=(_sc_scratch(rows_w, chunk, w, table.dtype)
                       + [pltpu.VMEM((n_ent,), jnp.int32), pltpu.VMEM((LANES,), jnp.int32)]),
        compiler_params=pltpu.CompilerParams(use_tc_tiling_on_sc=True, needs_layout_passes=False),
        name="sc_dispatch",
    )(table, codes, start_row)


def _experts_kernel(tile_e_ref, tile_first_ref, n_tiles_ref, xs_ref, wg_ref, wu_ref, wd_ref,
                    y_ref, wg_bf, wu_bf, wd_bf):
    i = pl.program_id(0)

    @pl.when(tile_first_ref[i] == 1)
    def _():
        wg_bf[...] = wg_ref[0].astype(_BF16)
        wu_bf[...] = wu_ref[0].astype(_BF16)
        wd_bf[...] = wd_ref[0].astype(_BF16)

    @pl.when(i < n_tiles_ref[0])
    def _():
        x = _unpack_bf16_pair(xs_ref[...]).astype(_BF16)
        a = _dot(x, wg_bf[...])
        u = _dot(x, wu_bf[...])
        h = (a * jax.nn.sigmoid(a) * u).astype(_BF16)
        y_ref[...] = _pack_bf16_pair(_dot(h, wd_bf[...]))


def _experts(tile_e, tile_first, n_tiles, xs, w_gate, w_up, w_down):
    p, dh = xs.shape
    d = 2 * dh
    tm = EXP_TM
    row = lambda i, te, tf, nt: (jnp.minimum(i, nt[0] - 1), 0)
    wsel = lambda i, te, tf, nt: (te[i], 0, 0)
    grid_spec = pltpu.PrefetchScalarGridSpec(
        num_scalar_prefetch=3,
        grid=(p // tm,),
        in_specs=[pl.BlockSpec((tm, dh), row),
                  pl.BlockSpec((1, d, D_EXPERT), wsel),
                  pl.BlockSpec((1, d, D_EXPERT), wsel),
                  pl.BlockSpec((1, D_EXPERT, d), wsel)],
        out_specs=pl.BlockSpec((tm, dh), row),
        scratch_shapes=[pltpu.VMEM((d, D_EXPERT), _BF16), pltpu.VMEM((d, D_EXPERT), _BF16),
                        pltpu.VMEM((D_EXPERT, d), _BF16)],
    )
    return pl.pallas_call(
        _experts_kernel,
        grid_spec=grid_spec,
        out_shape=jax.ShapeDtypeStruct((p, dh), _U32),
        compiler_params=pltpu.CompilerParams(
            dimension_semantics=("arbitrary",), vmem_limit_bytes=VMEM_LIMIT),
        name="experts",
    )(tile_e, tile_first, n_tiles, xs, w_gate, w_up, w_down)


def _final_kernel(x1_ref, ya_ref, yb_ref, route_ref, gf_ref, y_ref):
    route = route_ref[...]
    x2 = (x1_ref[...] + route[:, R_W1:R_W1 + 1] * _unpack_bf16_pair(ya_ref[...])
          + route[:, R_W2:R_W2 + 1] * _unpack_bf16_pair(yb_ref[...]))
    y_ref[...] = _rms(x2, gf_ref[...])


def _final(x1, yab, route, gf, n_prompt, n_sample):
    d = x1.shape[1]

    def call(tm, first_block, n_rows, name):
        tok = lambda w: pl.BlockSpec((tm, w), lambda i: (first_block + i, 0))
        sel = lambda k: pl.BlockSpec((None, tm, d // 2), lambda i: (k, first_block + i, 0))
        return pl.pallas_call(
            _final_kernel,
            grid=(n_rows // tm,),
            in_specs=[tok(d), sel(0), sel(1), tok(LANES), pl.BlockSpec((1, d), lambda i: (0, 0))],
            out_specs=pl.BlockSpec((tm, d), lambda i: (i, 0)),
            out_shape=jax.ShapeDtypeStruct((n_rows, d), _F32),
            compiler_params=pltpu.CompilerParams(
                dimension_semantics=("parallel",), vmem_limit_bytes=VMEM_LIMIT),
            name=name,
        )(x1, yab, yab, route, gf)

    return (call(FINAL_TM, 0, n_prompt, "final_prompt"),
            call(n_sample, n_prompt // n_sample, n_sample, "final_sample"))


def _ssm_params(lam_re, lam_im, log_dt, b_re, b_im, c_re, c_im, d_skip):
    dt = jnp.exp(log_dt)[:, None]
    mag = jnp.exp(lam_re * dt)
    ang = lam_im * dt
    lb_re = mag * jnp.cos(ang)
    lb_im = mag * jnp.sin(ang)
    den = lam_re * lam_re + lam_im * lam_im
    nr = lb_re - 1.0
    ni = lb_im
    k_re = (nr * lam_re + ni * lam_im) / den
    k_im = (ni * lam_re - nr * lam_im) / den
    bb_re = k_re[:, :, None] * b_re - k_im[:, :, None] * b_im
    bb_im = k_re[:, :, None] * b_im + k_im[:, :, None] * b_re
    eye = jnp.eye(SUBLANES, dtype=_F32)

    def b_blocks(bb):
        t = jnp.transpose(bb, (0, 2, 1)).reshape(N_LANE_TILES, 8, SSM_GROUP, SSM_STATE)
        return jnp.einsum('kahp,ab->kahbp', t, eye).reshape(N_LANE_TILES, LANES, TILE_STATE)

    def c_blocks(c):
        t = c.reshape(N_LANE_TILES, 8, SSM_GROUP, SSM_STATE)
        return jnp.einsum('kahp,ab->kapbh', t, eye).reshape(N_LANE_TILES, TILE_STATE, LANES)

    wb = jnp.concatenate([b_blocks(bb_re), b_blocks(bb_im)], axis=-1)
    wc = jnp.concatenate([c_blocks(c_re), -c_blocks(c_im)], axis=1)
    return (wb, wc, lb_re.reshape(1, STATE_COLS), lb_im.reshape(1, STATE_COLS),
            d_skip.reshape(1, D_SSM))


def _dispatch_plan(route_t, cnt, tm):
    t_all = route_t.shape[1]
    codes = route_t[R_CODE1:R_CODE2 + 1].astype(jnp.int32).reshape(-1)
    counts = cnt[0, :N_EXPERTS].astype(jnp.int32)
    tiles_per = (counts + tm - 1) // tm
    tile_end = jnp.cumsum(tiles_per)
    start_row = jnp.zeros((LANES,), jnp.int32).at[:N_EXPERTS].set((tile_end - tiles_per) * tm)
    n_tiles_max = (2 * t_all + N_EXPERTS * (tm - 1)) // tm
    n_tiles_max += n_tiles_max % 2
    tile_ids = jnp.arange(n_tiles_max, dtype=jnp.int32)
    tile_e = jnp.minimum(jnp.sum((tile_ids[:, None] >= tile_end[None, :]).astype(jnp.int32), axis=1),
                         N_EXPERTS - 1)
    tile_first = jnp.concatenate([jnp.ones((1,), jnp.int32), (tile_e[1:] != tile_e[:-1]).astype(jnp.int32)])
    n_tiles = tile_end[-1:].astype(jnp.int32)
    return codes, start_row, n_tiles_max * tm, tile_e, tile_first, n_tiles


def kernel(x_prompt, x_sample, state_ssm_re, state_ssm_im, norm1_g, w_in, lam_re, lam_im, log_dt, ssm_b_re, ssm_b_im, ssm_c_re, ssm_c_im, ssm_d, gmlp_norm_g, gmlp_w_s, gmlp_b_s, out_norm_ssm_g, out_norm_gmlp_g, w_out, norm2_g, w_router_group, b_router_group, w_router_expert, b_router_expert, w_gate, w_up, w_down, final_norm_g):
    n, l, d = x_prompt.shape
    ns = x_sample.shape[0]
    t_all = n * l + ns
    li = 0
    g1 = norm1_g[li].reshape(1, d)
    gn = gmlp_norm_g[li].reshape(1, D_GMLP)
    tril = jnp.tril(jnp.ones((CHUNK, CHUNK), dtype=bool))
    ws_tril = jnp.where(tril[None], gmlp_w_s[li], 0.0)
    bs = gmlp_b_s[li]
    gog = out_norm_gmlp_g[li].reshape(1, D_GMLP)
    gos = out_norm_ssm_g[li].reshape(1, D_SSM)
    wb, wc, lbr, lbi, dsk = _ssm_params(lam_re[li], lam_im[li], log_dt[li], ssm_b_re[li], ssm_b_im[li],
                                        ssm_c_re[li], ssm_c_im[li], ssm_d[li])
    g2 = norm2_g[li].reshape(1, d)
    pad = LANES - N_EXPERTS - N_EXPERT_GROUPS
    wr = jnp.concatenate([w_router_expert[li], w_router_group[li], jnp.zeros((d, pad), _F32)], axis=1)
    br = jnp.concatenate([b_router_expert[li], b_router_group[li], jnp.zeros((pad,), _F32)]).reshape(1, LANES)

    xa, sg, mixb = _front_prompt(x_prompt, g1, w_in[li].astype(_BF16), gn, ws_tril.astype(_BF16), bs.T, gog)
    mixa, hfin = _ssm_prompt(xa, sg, wb.astype(_BF16), wc.astype(_BF16), lbr, lbi, dsk, gos)
    w00 = jnp.repeat(ws_tril[:, 0, 0], GMLP_HEAD).reshape(1, D_GMLP)
    b0 = jnp.repeat(bs[:, 0], GMLP_HEAD).reshape(1, D_GMLP)
    mix_s, hr_s, hi_s, vrow = _front_sample(
        x_sample.reshape(ns, d), g1, w_in[li], gn, w00, b0, gog, wb, wc, lbr, lbi, dsk, gos,
        state_ssm_re[li].reshape(ns, STATE_COLS), state_ssm_im[li].reshape(ns, STATE_COLS))

    x1, xn, route, route_t, cnt = _mixer_out(x_prompt, mixa, mixb, x_sample.reshape(ns, d), mix_s,
                                             w_out[li], g2, wr, br)
    codes, start_row, n_rows, tile_e, tile_first, n_tiles = _dispatch_plan(route_t, cnt, EXP_TM)
    xs, dest = _sc_dispatch(xn, codes, start_row, n_rows, DISPATCH_CHUNK)
    ys = _experts(tile_e, tile_first, n_tiles, xs, w_gate[li], w_up[li], w_down[li])
    yab = _sc_combine(ys, dest, COMBINE_CHUNK).reshape(2, t_all, d // 2)
    y_p, y_s = _final(x1, yab, route, final_norm_g.reshape(1, d), n * l, ns)

    hf = hfin.reshape(n, N_LANE_TILES, 2, 8, SSM_STATE)
    re_p = hf[:, :, 0].reshape(1, n, N_SSM_GROUPS, SSM_STATE)
    im_p = hf[:, :, 1].reshape(1, n, N_SSM_GROUPS, SSM_STATE)
    re_s = hr_s.reshape(1, ns, N_SSM_GROUPS, SSM_STATE)
    im_s = hi_s.reshape(1, ns, N_SSM_GROUPS, SSM_STATE)
    return (y_p.reshape(n, l, d), y_s.reshape(ns, 1, d), re_p, im_p, re_s, im_s,
            vrow.reshape(1, ns, 1, D_GMLP))
```

```python
import math

import jax
import jax.numpy as jnp
from jax import lax
from jax.experimental import pallas as pl
from jax.experimental.pallas import tpu as pltpu
from jax.experimental.pallas import tpu_sc as plsc

D_MODEL = 1024
D_SSM = 512
D_GMLP = 512
SSM_GROUP = 16
N_SSM_GROUPS = 32
SSM_STATE = 64
CHUNK = 128
N_GMLP_HEADS = 4
GMLP_HEAD = 128
N_EXPERT_GROUPS = 4
EXPERTS_PER_GROUP = 8
N_EXPERTS = 32
D_EXPERT = 512
D_IN = 2048
EPS = 1e-6

LANES = 128
SUBLANES = 8
N_LANE_TILES = D_SSM // LANES
STATE_COLS = N_SSM_GROUPS * SSM_STATE
TILE_STATE = STATE_COLS // N_LANE_TILES
VMEM_LIMIT = 56 * 1024 * 1024

SC_CORES = 2
SC_SUBCORES = 16
SC_LANES = 16
SC_WORKERS = SC_CORES * SC_SUBCORES

FRONT_TL = 512
SSM_LC = 128
TOK_TM = 512
FINAL_TM = 512
EXP_TM = 512
DISPATCH_CHUNK = 80
COMBINE_CHUNK = 24

R_E1, R_E2, R_W1, R_W2, R_RANK1, R_RANK2, R_CODE1, R_CODE2 = 0, 1, 2, 3, 4, 5, 6, 7
CODE_BITS = 16
CODE_SHIFT = float(1 << CODE_BITS)

_INV_SQRT2 = 1.0 / math.sqrt(2.0)
_BF16 = jnp.bfloat16
_F32 = jnp.float32
_U32 = jnp.uint32


def _gelu(x):
    return 0.5 * x * (1.0 + lax.erf(x * _INV_SQRT2))


def _rms(x, g):
    return x * lax.rsqrt(jnp.mean(x * x, axis=-1, keepdims=True) + EPS) * g


def _dot(a, b):
    return jnp.dot(a, b, preferred_element_type=_F32)


def _dot_f32(a, b):
    return jnp.dot(a, b, preferred_element_type=_F32, precision=lax.Precision.HIGHEST)


def _pack_bf16_pair(x):
    w = x.shape[1] // 2
    hi = lax.bitcast_convert_type(x[:, :w].astype(_BF16).astype(_F32), _U32)
    lo = lax.bitcast_convert_type(x[:, w:].astype(_BF16).astype(_F32), _U32)
    return hi | (lo >> 16)


def _unpack_bf16_pair(p):
    hi = lax.bitcast_convert_type(p & jnp.uint32(0xFFFF0000), _F32)
    lo = lax.bitcast_convert_type(p << 16, _F32)
    return jnp.concatenate([hi, lo], axis=-1)


def _head_norm_gelu(vb, gn):
    v = _gelu(vb)
    parts = []
    for h in range(N_GMLP_HEADS):
        vh = v[:, h * GMLP_HEAD:(h + 1) * GMLP_HEAD]
        parts.append(vh * lax.rsqrt(jnp.mean(vh * vh, axis=-1, keepdims=True) + EPS))
    return jnp.concatenate(parts, axis=-1) * gn


def _front_prompt_kernel(x_ref, g1_ref, win_ref, gn_ref, ws_ref, bs_ref, gog_ref,
                         xa_ref, sg_ref, mixb_ref):
    x = x_ref[0]
    hn = _rms(x, g1_ref[...]).astype(_BF16)
    z = _dot(hn, win_ref[...])
    xa_ref[0] = z[:, :D_SSM]
    sg_ref[0] = jax.nn.sigmoid(z[:, D_SSM:2 * D_SSM])
    ub = _gelu(z[:, 2 * D_SSM:2 * D_SSM + D_GMLP])
    vbn = _head_norm_gelu(z[:, 2 * D_SSM + D_GMLP:], gn_ref[...]).astype(_BF16)
    tl = x.shape[0]
    rows = []
    for c in range(tl // CHUNK):
        heads = []
        for h in range(N_GMLP_HEADS):
            vh = vbn[c * CHUNK:(c + 1) * CHUNK, h * GMLP_HEAD:(h + 1) * GMLP_HEAD]
            heads.append(_dot(ws_ref[h], vh) + bs_ref[:, h:h + 1])
        rows.append(jnp.concatenate(heads, axis=-1))
    s = jnp.concatenate(rows, axis=0)
    mixb_ref[0] = _rms(ub * s, gog_ref[...]).astype(_BF16)


def _front_prompt(x, g1, win_bf, gn, ws_tril_bf, bs_t, gog):
    n, l, d = x.shape
    tl = FRONT_TL
    grid = (n, l // tl)
    const = lambda *shape: pl.BlockSpec(shape, lambda b, i: (0,) * len(shape))
    seq = lambda w: pl.BlockSpec((1, tl, w), lambda b, i: (b, i, 0))
    return pl.pallas_call(
        _front_prompt_kernel,
        grid=grid,
        in_specs=[seq(d), const(1, d), const(d, D_IN), const(1, D_GMLP),
                  const(N_GMLP_HEADS, CHUNK, CHUNK), const(CHUNK, N_GMLP_HEADS), const(1, D_GMLP)],
        out_specs=[seq(D_SSM), seq(D_SSM), seq(D_GMLP)],
        out_shape=[jax.ShapeDtypeStruct((n, l, D_SSM), _F32),
                   jax.ShapeDtypeStruct((n, l, D_SSM), _F32),
                   jax.ShapeDtypeStruct((n, l, D_GMLP), _BF16)],
        compiler_params=pltpu.CompilerParams(
            dimension_semantics=("parallel", "parallel"), vmem_limit_bytes=VMEM_LIMIT),
        name="front_prompt",
    )(x, g1, win_bf, gn, ws_tril_bf, bs_t, gog)


def _ssm_prompt_kernel(xa_ref, sg_ref, wb_ref, wc_ref, lbr_ref, lbi_ref, dsk_ref, gos_ref,
                       mixa_ref, hfin_ref, bu_ref, st_ref):
    lc = xa_ref.shape[1]
    rows = lc * SUBLANES
    pair = 2 * SUBLANES

    @pl.when(pl.program_id(0) == 0)
    def _():
        st_ref[...] = jnp.zeros_like(st_ref)

    xa = pltpu.einshape("btc->tbc", xa_ref[...]).reshape(rows, D_SSM)
    xa_bf = xa.astype(_BF16)
    for k in range(N_LANE_TILES):
        bu_ref[:, 2 * TILE_STATE * k:2 * TILE_STATE * (k + 1)] = _dot(
            xa_bf[:, k * LANES:(k + 1) * LANES], wb_ref[k])

    for kk in range(0, N_LANE_TILES, 2):
        tiles = (kk, kk + 1)
        cols = [(2 * TILE_STATE * k, 2 * TILE_STATE * k + TILE_STATE) for k in tiles]
        lbs = [(jnp.broadcast_to(lbr_ref[:, k * TILE_STATE:(k + 1) * TILE_STATE], (SUBLANES, TILE_STATE)),
                jnp.broadcast_to(lbi_ref[:, k * TILE_STATE:(k + 1) * TILE_STATE], (SUBLANES, TILE_STATE)))
               for k in tiles]

        def body(j, carry, cols=cols, lbs=lbs):
            r0 = pl.multiple_of(j * pair, pair)
            r1 = pl.multiple_of(r0 + SUBLANES, SUBLANES)
            out = []
            for q, ((c_re, c_im), (lr, li)) in enumerate(zip(cols, lbs)):
                hr, hi = carry[2 * q], carry[2 * q + 1]
                ar = lr * hr - li * hi + bu_ref[pl.ds(r0, SUBLANES), c_re:c_re + TILE_STATE]
                ai = lr * hi + li * hr + bu_ref[pl.ds(r0, SUBLANES), c_im:c_im + TILE_STATE]
                br = lr * ar - li * ai + bu_ref[pl.ds(r1, SUBLANES), c_re:c_re + TILE_STATE]
                bi = lr * ai + li * ar + bu_ref[pl.ds(r1, SUBLANES), c_im:c_im + TILE_STATE]
                bu_ref[pl.ds(r0, SUBLANES), c_re:c_re + TILE_STATE] = ar
                bu_ref[pl.ds(r0, SUBLANES), c_im:c_im + TILE_STATE] = ai
                bu_ref[pl.ds(r1, SUBLANES), c_re:c_re + TILE_STATE] = br
                bu_ref[pl.ds(r1, SUBLANES), c_im:c_im + TILE_STATE] = bi
                out += [br, bi]
            return tuple(out)

        init = tuple(st_ref[:, c:c + TILE_STATE] for c_pair in cols for c in c_pair)
        fin = lax.fori_loop(0, lc // 2, body, init, unroll=2)
        for q, (c_re, c_im) in enumerate(cols):
            st_ref[:, c_re:c_re + TILE_STATE] = fin[2 * q]
            st_ref[:, c_im:c_im + TILE_STATE] = fin[2 * q + 1]

    ys = []
    for k in range(N_LANE_TILES):
        hk = bu_ref[:, 2 * TILE_STATE * k:2 * TILE_STATE * (k + 1)].astype(_BF16)
        ys.append(_dot(hk, wc_ref[k]))
    y = jnp.concatenate(ys, axis=-1) + dsk_ref[...] * xa
    sg = pltpu.einshape("btc->tbc", sg_ref[...]).reshape(rows, D_SSM)
    mixa = _rms(_gelu(y) * sg, gos_ref[...]).reshape(lc, SUBLANES, D_SSM)
    mixa_ref[...] = pltpu.einshape("tbc->btc", mixa).astype(_BF16)
    hfin_ref[...] = st_ref[...]


def _ssm_prompt(xa, sg, wb, wc, lbr, lbi, dsk, gos):
    n, l, _ = xa.shape
    lc = SSM_LC
    const = lambda *shape: pl.BlockSpec(shape, lambda i: (0,) * len(shape))
    seq_spec = pl.BlockSpec((n, lc, D_SSM), lambda i: (0, i, 0))
    return pl.pallas_call(
        _ssm_prompt_kernel,
        grid=(l // lc,),
        in_specs=[seq_spec, seq_spec,
                  const(N_LANE_TILES, LANES, 2 * TILE_STATE), const(N_LANE_TILES, 2 * TILE_STATE, LANES),
                  const(1, STATE_COLS), const(1, STATE_COLS), const(1, D_SSM), const(1, D_SSM)],
        out_specs=[seq_spec, const(n, 2 * STATE_COLS)],
        out_shape=[jax.ShapeDtypeStruct((n, l, D_SSM), _BF16),
                   jax.ShapeDtypeStruct((n, 2 * STATE_COLS), _F32)],
        scratch_shapes=[pltpu.VMEM((lc * n, 2 * STATE_COLS), _F32),
                        pltpu.VMEM((n, 2 * STATE_COLS), _F32)],
        compiler_params=pltpu.CompilerParams(
            dimension_semantics=("arbitrary",), vmem_limit_bytes=VMEM_LIMIT),
        name="ssm_prompt",
    )(xa, sg, wb, wc, lbr, lbi, dsk, gos)


def _front_sample_kernel(x_ref, g1_ref, win_ref, gn_ref, w00_ref, b0_ref, gog_ref,
                         wb_ref, wc_ref, lbr_ref, lbi_ref, dsk_ref, gos_ref, h0r_ref, h0i_ref,
                         mix_ref, hr_ref, hi_ref, vrow_ref):
    x = x_ref[...]
    hn = _rms(x, g1_ref[...])
    z = _dot_f32(hn, win_ref[...])
    xa = z[:, :D_SSM]
    ys = []
    for k in range(N_LANE_TILES):
        bu = _dot_f32(xa[:, k * LANES:(k + 1) * LANES], wb_ref[k])
        sl = slice(k * TILE_STATE, (k + 1) * TILE_STATE)
        lr, li = lbr_ref[:, sl], lbi_ref[:, sl]
        h0r, h0i = h0r_ref[:, sl], h0i_ref[:, sl]
        nr = lr * h0r - li * h0i + bu[:, :TILE_STATE]
        ni = lr * h0i + li * h0r + bu[:, TILE_STATE:]
        hr_ref[:, sl] = nr
        hi_ref[:, sl] = ni
        ys.append(_dot_f32(jnp.concatenate([nr, ni], axis=-1), wc_ref[k]))
    y = jnp.concatenate(ys, axis=-1) + dsk_ref[...] * xa
    ya = _gelu(y) * jax.nn.sigmoid(z[:, D_SSM:2 * D_SSM])
    mix_ref[:, :D_SSM] = _rms(ya, gos_ref[...])
    ub = _gelu(z[:, 2 * D_SSM:2 * D_SSM + D_GMLP])
    vbn = _head_norm_gelu(z[:, 2 * D_SSM + D_GMLP:], gn_ref[...])
    vrow_ref[...] = vbn
    s = w00_ref[...] * vbn + b0_ref[...]
    mix_ref[:, D_SSM:] = _rms(ub * s, gog_ref[...])


def _front_sample(x, g1, win, gn, w00, b0, gog, wb, wc, lbr, lbi, dsk, gos, h0r, h0i):
    n = x.shape[0]
    vmem = pl.BlockSpec(memory_space=pltpu.VMEM)
    return pl.pallas_call(
        _front_sample_kernel,
        in_specs=[vmem] * 15,
        out_specs=[vmem] * 4,
        out_shape=[jax.ShapeDtypeStruct((n, D_MODEL), _F32),
                   jax.ShapeDtypeStruct((n, STATE_COLS), _F32),
                   jax.ShapeDtypeStruct((n, STATE_COLS), _F32),
                   jax.ShapeDtypeStruct((n, D_GMLP), _F32)],
        compiler_params=pltpu.CompilerParams(vmem_limit_bytes=VMEM_LIMIT),
        name="front_sample",
    )(x, g1, win, gn, w00, b0, gog, wb, wc, lbr, lbi, dsk, gos, h0r, h0i)


def _route(logits, base):
    tm = logits.shape[0]
    lane = lax.broadcasted_iota(jnp.int32, logits.shape, 1).astype(_F32)
    neg = jnp.float32(-jnp.inf)
    big = jnp.float32(LANES)
    is_g = (lane >= N_EXPERTS) & (lane < N_EXPERTS + N_EXPERT_GROUPS)
    gl = jnp.where(is_g, logits, neg)
    gmax = jnp.max(gl, axis=-1, keepdims=True)
    gi = jnp.min(jnp.where(is_g & (logits == gmax), lane, big), axis=-1, keepdims=True) - N_EXPERTS
    p_top = 1.0 / jnp.sum(jnp.where(is_g, jnp.exp(gl - gmax), 0.0), axis=-1, keepdims=True)
    lo = gi * EXPERTS_PER_GROUP
    in_grp = (lane >= lo) & (lane < lo + EXPERTS_PER_GROUP)
    m1 = jnp.max(jnp.where(in_grp, logits, neg), axis=-1, keepdims=True)
    i1 = jnp.min(jnp.where(in_grp & (logits == m1), lane, big), axis=-1, keepdims=True)
    rest = in_grp & (lane != i1)
    m2 = jnp.max(jnp.where(rest, logits, neg), axis=-1, keepdims=True)
    i2 = jnp.min(jnp.where(rest & (logits == m2), lane, big), axis=-1, keepdims=True)
    e2 = jnp.exp(m2 - m1)
    w1 = p_top / (1.0 + e2)
    w2 = p_top * e2 / (1.0 + e2)
    sel1 = lane == i1
    sel2 = lane == i2
    hits = jnp.where(sel1 | sel2, 1.0, 0.0)
    r_id = lax.broadcasted_iota(jnp.int32, (tm, tm), 0)
    c_id = lax.broadcasted_iota(jnp.int32, (tm, tm), 1)
    ltri = jnp.where(c_id < r_id, 1.0, 0.0).astype(_BF16)
    before = _dot(ltri, hits.astype(_BF16)) + base
    rank1 = jnp.sum(jnp.where(sel1, before, 0.0), axis=-1, keepdims=True)
    rank2 = jnp.sum(jnp.where(sel2, before, 0.0), axis=-1, keepdims=True)
    out = jnp.where(lane == R_E1, i1, 0.0)
    out = jnp.where(lane == R_E2, i2, out)
    out = jnp.where(lane == R_W1, w1, out)
    out = jnp.where(lane == R_W2, w2, out)
    out = jnp.where(lane == R_RANK1, rank1, out)
    out = jnp.where(lane == R_RANK2, rank2, out)
    out = jnp.where(lane == R_CODE1, i1 * CODE_SHIFT + rank1, out)
    out = jnp.where(lane == R_CODE2, i2 * CODE_SHIFT + rank2, out)
    return out, base + jnp.sum(hits, axis=0, keepdims=True)


def _mixer_out_prompt_kernel(x_ref, mixa_ref, mixb_ref, wo_ref, g2_ref, wr_ref, br_ref,
                             x1_ref, xn_ref, route_ref, route_t_ref, cnt_ref, base_ref):
    @pl.when((pl.program_id(0) == 0) & (pl.program_id(1) == 0))
    def _():
        base_ref[...] = jnp.zeros_like(base_ref)

    x1 = x_ref[0] + _dot(mixa_ref[0], wo_ref[:D_SSM, :]) + _dot(mixb_ref[0], wo_ref[D_SSM:, :])
    xn = _rms(x1, g2_ref[...])
    logits = _dot(xn.astype(_BF16), wr_ref[...]) + br_ref[...]
    route, base = _route(logits, base_ref[...])
    x1_ref[...] = x1
    xn_ref[...] = _pack_bf16_pair(xn)
    route_ref[...] = route
    route_t_ref[...] = route.T[:SUBLANES, :]
    base_ref[...] = base
    cnt_ref[...] = base


def _mixer_out_sample_kernel(x_ref, mix_ref, wo_ref, g2_ref, wr_ref, br_ref, cnt_in_ref,
                             x1_in, xn_in, route_in, route_t_in,
                             x1_ref, xn_ref, route_ref, route_t_ref, cnt_ref):
    del x1_in, xn_in, route_in, route_t_in
    x1 = (x_ref[...] + _dot_f32(mix_ref[:, :D_SSM], wo_ref[:D_SSM, :])
          + _dot_f32(mix_ref[:, D_SSM:], wo_ref[D_SSM:, :]))
    xn = _rms(x1, g2_ref[...])
    logits = _dot_f32(xn, wr_ref[...]) + br_ref[...]
    route, base = _route(logits, cnt_in_ref[...])
    x1_ref[...] = x1
    xn_ref[...] = _pack_bf16_pair(xn)
    route_ref[...] = route
    route_t_ref[...] = route.T[:SUBLANES, :]
    cnt_ref[...] = base


def _mixer_out(x_p, mixa, mixb, x_s, mix_s, wo, g2, wr, br):
    n, l, d = x_p.shape
    ns = x_s.shape[0]
    t_all = n * l + ns
    tm = TOK_TM
    per_seq = l // tm
    const = lambda *shape: pl.BlockSpec(shape, lambda b, i: (0,) * len(shape))
    seq = lambda w: pl.BlockSpec((1, tm, w), lambda b, i: (b, i, 0))
    tok = lambda w: pl.BlockSpec((tm, w), lambda b, i: (b * per_seq + i, 0))
    tok_shapes = [jax.ShapeDtypeStruct((t_all, d), _F32),
                  jax.ShapeDtypeStruct((t_all, d // 2), _U32),
                  jax.ShapeDtypeStruct((t_all, LANES), _F32),
                  jax.ShapeDtypeStruct((SUBLANES, t_all), _F32)]
    cnt_shape = jax.ShapeDtypeStruct((1, LANES), _F32)
    x1, xn, route, route_t, cnt = pl.pallas_call(
        _mixer_out_prompt_kernel,
        grid=(n, per_seq),
        in_specs=[seq(d), seq(D_SSM), seq(D_GMLP),
                  const(d, d), const(1, d), const(d, LANES), const(1, LANES)],
        out_specs=[tok(d), tok(d // 2), tok(LANES),
                   pl.BlockSpec((SUBLANES, tm), lambda b, i: (0, b * per_seq + i)), const(1, LANES)],
        out_shape=tok_shapes + [cnt_shape],
        scratch_shapes=[pltpu.VMEM((1, LANES), _F32)],
        compiler_params=pltpu.CompilerParams(
            dimension_semantics=("arbitrary", "arbitrary"), vmem_limit_bytes=VMEM_LIMIT),
        name="mixer_out_prompt",
    )(x_p, mixa, mixb, wo.astype(_BF16), g2, wr.astype(_BF16), br)
    tail = (n * l) // ns
    c1 = lambda *shape: pl.BlockSpec(shape, lambda i: (0,) * len(shape))
    anyspec = pl.BlockSpec(memory_space=pl.ANY)
    tail_spec = lambda w: pl.BlockSpec((ns, w), lambda i: (tail, 0))
    return pl.pallas_call(
        _mixer_out_sample_kernel,
        grid=(1,),
        in_specs=[c1(ns, d), c1(ns, d), c1(d, d), c1(1, d), c1(d, LANES), c1(1, LANES), c1(1, LANES),
                  anyspec, anyspec, anyspec, anyspec],
        out_specs=[tail_spec(d), tail_spec(d // 2), tail_spec(LANES),
                   pl.BlockSpec((SUBLANES, ns), lambda i: (0, tail)), c1(1, LANES)],
        out_shape=tok_shapes + [cnt_shape],
        input_output_aliases={7: 0, 8: 1, 9: 2, 10: 3},
        compiler_params=pltpu.CompilerParams(
            dimension_semantics=("arbitrary",), vmem_limit_bytes=VMEM_LIMIT),
        name="mixer_out_sample",
    )(x_s, mix_s, wo, g2, wr, br, cnt, x1, xn, route, route_t)


def _sc_stream(n_chunks, gather, write):
    gather(0).start()
    for j in range(n_chunks):
        if j + 1 < n_chunks:
            if j >= 1:
                write(j - 1).wait()
            gather(j + 1).start()
        gather(j).wait()
        write(j).start()
    if n_chunks >= 2:
        write(n_chunks - 2).wait()
    write(n_chunks - 1).wait()


def _sc_mesh():
    return plsc.VectorSubcoreMesh(core_axis_name="c", subcore_axis_name="s",
                                  num_cores=SC_CORES, num_subcores=SC_SUBCORES)


def _sc_buffers(chunk, w, dtype):
    return [pltpu.VMEM((chunk, w), dtype), pltpu.VMEM((chunk, w), dtype)] + [pltpu.SemaphoreType.DMA] * 4


def _sc_combine(table, idx, n_out, chunk):
    w = table.shape[1]
    rows_w = n_out // SC_WORKERS
    n_chunks = rows_w // chunk
    assert rows_w * SC_WORKERS == n_out and n_chunks * chunk == rows_w and rows_w % SUBLANES == 0

    def body(table_hbm, idx_hbm, out_hbm, idx_v, buf0, buf1, g0, g1, w0, w1):
        wid = lax.axis_index("s") * SC_CORES + lax.axis_index("c")
        base = pl.multiple_of(wid * rows_w, SUBLANES)
        pltpu.sync_copy(idx_hbm.at[pl.ds(base, rows_w)], idx_v)
        bufs, gsems, wsems = (buf0, buf1), (g0, g1), (w0, w1)

        def gather(j):
            return pltpu.make_async_copy(table_hbm.at[idx_v.at[pl.ds(j * chunk, chunk)]], bufs[j % 2], gsems[j % 2])

        def write(j):
            return pltpu.make_async_copy(bufs[j % 2], out_hbm.at[pl.ds(base + j * chunk, chunk)], wsems[j % 2])

        _sc_stream(n_chunks, gather, write)

    return pl.kernel(
        body,
        out_type=jax.ShapeDtypeStruct((n_out, w), table.dtype),
        mesh=_sc_mesh(),
        scratch_types=[pltpu.VMEM((rows_w,), jnp.int32)] + _sc_buffers(chunk, w, table.dtype),
        compiler_params=pltpu.CompilerParams(use_tc_tiling_on_sc=True),
        name="sc_combine",
    )(table, idx)


def _sc_dispatch(table, codes, start_row, n_out, chunk):
    t_all, w = table.shape
    n_pad = codes.shape[0]
    n_ent = 2 * t_all
    ent_w = n_pad // SC_WORKERS
    n_chunks = ent_w // chunk
    per_chunk = chunk // SC_LANES
    trash = n_out - (n_pad - n_ent)
    assert ent_w * SC_WORKERS == n_pad and n_chunks * chunk == ent_w
    assert per_chunk * SC_LANES == chunk and chunk <= LANES and n_pad - n_ent <= t_all

    def body(table_hbm, code_hbm, start_hbm, out_hbm, dest_hbm,
             code_v, dest_v, tok_v, dst_v, start_v, buf0, buf1, g0, g1, w0, w1):
        wid = lax.axis_index("s") * SC_CORES + lax.axis_index("c")
        ebase = pl.multiple_of(wid * ent_w, SUBLANES)
        pltpu.sync_copy(code_hbm.at[pl.ds(ebase, ent_w)], code_v)
        pltpu.sync_copy(start_hbm, start_v)
        lane = lax.iota(jnp.int32, SC_LANES)
        for j in range(n_chunks):
            for c in range(per_chunk):
                off = j * chunk + c * SC_LANES
                ent = ebase + off + lane
                code = code_v[pl.ds(off, SC_LANES)]
                d = plsc.load_gather(start_v, [code >> CODE_BITS]) + (code & ((1 << CODE_BITS) - 1))
                d = jnp.where(ent >= n_ent, trash + (ent - n_ent), d)
                tok = jnp.where(ent >= t_all, ent - t_all, ent)
                tok = jnp.where(tok >= t_all, tok - t_all, tok)
                dest_v[pl.ds(off, SC_LANES)] = d
                dst_v[j, pl.ds(c * SC_LANES, SC_LANES)] = d
                tok_v[j, pl.ds(c * SC_LANES, SC_LANES)] = tok
        pltpu.sync_copy(dest_v, dest_hbm.at[pl.ds(ebase, ent_w)])
        bufs, gsems, wsems = (buf0, buf1), (g0, g1), (w0, w1)

        def gather(j):
            return pltpu.make_async_copy(table_hbm.at[tok_v.at[j]], bufs[j % 2], gsems[j % 2])

        def scatter(j):
            return pltpu.make_async_copy(bufs[j % 2], out_hbm.at[dst_v.at[j]], wsems[j % 2])

        _sc_stream(n_chunks, gather, scatter)

    return pl.kernel(
        body,
        out_type=(jax.ShapeDtypeStruct((n_out, w), table.dtype), jax.ShapeDtypeStruct((n_pad,), jnp.int32)),
        mesh=_sc_mesh(),
        scratch_types=([pltpu.VMEM((ent_w,), jnp.int32), pltpu.VMEM((ent_w,), jnp.int32),
                        pltpu.VMEM((n_chunks, chunk), jnp.int32), pltpu.VMEM((n_chunks, chunk), jnp.int32),
                        pltpu.VMEM((LANES,), jnp.int32)] + _sc_buffers(chunk, w, table.dtype)),
        compiler_params=pltpu.CompilerParams(use_tc_tiling_on_sc=True, needs_layout_passes=False),
        name="sc_dispatch",
    )(table, codes, start_row)


def _experts_kernel(tile_e_ref, tile_first_ref, n_tiles_ref, xs_ref, wg_ref, wu_ref, wd_ref,
                    y_ref, wg_bf, wu_bf, wd_bf):
    i = pl.program_id(0)

    @pl.when(tile_first_ref[i] == 1)
    def _():
        wg_bf[...] = wg_ref[0].astype(_BF16)
        wu_bf[...] = wu_ref[0].astype(_BF16)
        wd_bf[...] = wd_ref[0].astype(_BF16)

    @pl.when(i < n_tiles_ref[0])
    def _():
        x = _unpack_bf16_pair(xs_ref[...]).astype(_BF16)
        a = _dot(x, wg_bf[...])
        u = _dot(x, wu_bf[...])
        h = (a * jax.nn.sigmoid(a) * u).astype(_BF16)
        y_ref[...] = _pack_bf16_pair(_dot(h, wd_bf[...]))


def _experts(tile_e, tile_first, n_tiles, xs, w_gate, w_up, w_down):
    dh = xs.shape[1]
    d = 2 * dh
    tm = EXP_TM
    n_tiles_max = tile_e.shape[0]
    p = n_tiles_max * tm
    row = lambda i, te, tf, nt: (jnp.minimum(i, nt[0] - 1), 0)
    wsel = lambda i, te, tf, nt: (te[i], 0, 0)
    grid_spec = pltpu.PrefetchScalarGridSpec(
        num_scalar_prefetch=3,
        grid=(n_tiles_max,),
        in_specs=[pl.BlockSpec((tm, dh), row),
                  pl.BlockSpec((1, d, D_EXPERT), wsel),
                  pl.BlockSpec((1, d, D_EXPERT), wsel),
                  pl.BlockSpec((1, D_EXPERT, d), wsel)],
        out_specs=pl.BlockSpec((tm, dh), row),
        scratch_shapes=[pltpu.VMEM((d, D_EXPERT), _BF16), pltpu.VMEM((d, D_EXPERT), _BF16),
                        pltpu.VMEM((D_EXPERT, d), _BF16)],
    )
    return pl.pallas_call(
        _experts_kernel,
        grid_spec=grid_spec,
        out_shape=jax.ShapeDtypeStruct((p, dh), _U32),
        compiler_params=pltpu.CompilerParams(
            dimension_semantics=("arbitrary",), vmem_limit_bytes=VMEM_LIMIT),
        name="experts",
    )(tile_e, tile_first, n_tiles, xs, w_gate, w_up, w_down)


def _final_kernel(x1_ref, ya_ref, yb_ref, route_ref, gf_ref, y_ref):
    route = route_ref[...]
    x2 = (x1_ref[...] + route[:, R_W1:R_W1 + 1] * _unpack_bf16_pair(ya_ref[...])
          + route[:, R_W2:R_W2 + 1] * _unpack_bf16_pair(yb_ref[...]))
    y_ref[...] = _rms(x2, gf_ref[...])


def _final(x1, yab, route, gf, n_prompt, n_sample):
    d = x1.shape[1]

    def call(tm, first_block, n_rows, name):
        tok = lambda w: pl.BlockSpec((tm, w), lambda i: (first_block + i, 0))
        sel = lambda k: pl.BlockSpec((None, tm, d // 2), lambda i: (k, first_block + i, 0))
        return pl.pallas_call(
            _final_kernel,
            grid=(n_rows // tm,),
            in_specs=[tok(d), sel(0), sel(1), tok(LANES), pl.BlockSpec((1, d), lambda i: (0, 0))],
            out_specs=pl.BlockSpec((tm, d), lambda i: (i, 0)),
            out_shape=jax.ShapeDtypeStruct((n_rows, d), _F32),
            compiler_params=pltpu.CompilerParams(
                dimension_semantics=("parallel",), vmem_limit_bytes=VMEM_LIMIT),
            name=name,
        )(x1, yab, yab, route, gf)

    return (call(FINAL_TM, 0, n_prompt, "final_prompt"),
            call(n_sample, n_prompt // n_sample, n_sample, "final_sample"))


def _ssm_params(lam_re, lam_im, log_dt, b_re, b_im, c_re, c_im, d_skip):
    dt = jnp.exp(log_dt)[:, None]
    mag = jnp.exp(lam_re * dt)
    ang = lam_im * dt
    lb_re = mag * jnp.cos(ang)
    lb_im = mag * jnp.sin(ang)
    den = lam_re * lam_re + lam_im * lam_im
    nr = lb_re - 1.0
    ni = lb_im
    k_re = (nr * lam_re + ni * lam_im) / den
    k_im = (ni * lam_re - nr * lam_im) / den
    bb_re = k_re[:, :, None] * b_re - k_im[:, :, None] * b_im
    bb_im = k_re[:, :, None] * b_im + k_im[:, :, None] * b_re
    eye = jnp.eye(SUBLANES, dtype=_F32)

    def b_blocks(bb):
        t = jnp.transpose(bb, (0, 2, 1)).reshape(N_LANE_TILES, 8, SSM_GROUP, SSM_STATE)
        return jnp.einsum('kahp,ab->kahbp', t, eye).reshape(N_LANE_TILES, LANES, TILE_STATE)

    def c_blocks(c):
        t = c.reshape(N_LANE_TILES, 8, SSM_GROUP, SSM_STATE)
        return jnp.einsum('kahp,ab->kapbh', t, eye).reshape(N_LANE_TILES, TILE_STATE, LANES)

    wb = jnp.concatenate([b_blocks(bb_re), b_blocks(bb_im)], axis=-1)
    wc = jnp.concatenate([c_blocks(c_re), -c_blocks(c_im)], axis=1)
    return (wb, wc, lb_re.reshape(1, STATE_COLS), lb_im.reshape(1, STATE_COLS),
            d_skip.reshape(1, D_SSM))


def _dispatch_plan(route_t, cnt, tm):
    t_all = route_t.shape[1]
    codes = route_t[R_CODE1:R_CODE2 + 1].astype(jnp.int32).reshape(-1)
    per_pass = SC_WORKERS * DISPATCH_CHUNK
    codes = jnp.pad(codes, (0, -(2 * t_all) % per_pass))
    counts = cnt[0, :N_EXPERTS].astype(jnp.int32)
    tiles_per = (counts + tm - 1) // tm
    tile_end = jnp.cumsum(tiles_per)
    start_row = jnp.zeros((LANES,), jnp.int32).at[:N_EXPERTS].set((tile_end - tiles_per) * tm)
    n_tiles_max = (2 * t_all + N_EXPERTS * (tm - 1)) // tm
    tile_ids = jnp.arange(n_tiles_max, dtype=jnp.int32)
    tile_e = jnp.minimum(jnp.sum((tile_ids[:, None] >= tile_end[None, :]).astype(jnp.int32), axis=1),
                         N_EXPERTS - 1)
    tile_first = jnp.concatenate([jnp.ones((1,), jnp.int32), (tile_e[1:] != tile_e[:-1]).astype(jnp.int32)])
    n_tiles = tile_end[-1:].astype(jnp.int32)
    return codes, start_row, (n_tiles_max + 1) * tm, tile_e, tile_first, n_tiles


def kernel(x_prompt, x_sample, state_ssm_re, state_ssm_im, norm1_g, w_in, lam_re, lam_im, log_dt, ssm_b_re, ssm_b_im, ssm_c_re, ssm_c_im, ssm_d, gmlp_norm_g, gmlp_w_s, gmlp_b_s, out_norm_ssm_g, out_norm_gmlp_g, w_out, norm2_g, w_router_group, b_router_group, w_router_expert, b_router_expert, w_gate, w_up, w_down, final_norm_g):
    n, l, d = x_prompt.shape
    ns = x_sample.shape[0]
    t_all = n * l + ns
    li = 0
    g1 = norm1_g[li].reshape(1, d)
    gn = gmlp_norm_g[li].reshape(1, D_GMLP)
    tril = jnp.tril(jnp.ones((CHUNK, CHUNK), dtype=bool))
    ws_tril = jnp.where(tril[None], gmlp_w_s[li], 0.0)
    bs = gmlp_b_s[li]
    gog = out_norm_gmlp_g[li].reshape(1, D_GMLP)
    gos = out_norm_ssm_g[li].reshape(1, D_SSM)
    wb, wc, lbr, lbi, dsk = _ssm_params(lam_re[li], lam_im[li], log_dt[li], ssm_b_re[li], ssm_b_im[li],
                                        ssm_c_re[li], ssm_c_im[li], ssm_d[li])
    g2 = norm2_g[li].reshape(1, d)
    pad = LANES - N_EXPERTS - N_EXPERT_GROUPS
    wr = jnp.concatenate([w_router_expert[li], w_router_group[li], jnp.zeros((d, pad), _F32)], axis=1)
    br = jnp.concatenate([b_router_expert[li], b_router_group[li], jnp.zeros((pad,), _F32)]).reshape(1, LANES)

    xa, sg, mixb = _front_prompt(x_prompt, g1, w_in[li].astype(_BF16), gn, ws_tril.astype(_BF16), bs.T, gog)
    mixa, hfin = _ssm_prompt(xa, sg, wb.astype(_BF16), wc.astype(_BF16), lbr, lbi, dsk, gos)
    w00 = jnp.repeat(ws_tril[:, 0, 0], GMLP_HEAD).reshape(1, D_GMLP)
    b0 = jnp.repeat(bs[:, 0], GMLP_HEAD).reshape(1, D_GMLP)
    mix_s, hr_s, hi_s, vrow = _front_sample(
        x_sample.reshape(ns, d), g1, w_in[li], gn, w00, b0, gog, wb, wc, lbr, lbi, dsk, gos,
        state_ssm_re[li].reshape(ns, STATE_COLS), state_ssm_im[li].reshape(ns, STATE_COLS))

    x1, xn, route, route_t, cnt = _mixer_out(x_prompt, mixa, mixb, x_sample.reshape(ns, d), mix_s,
                                             w_out[li], g2, wr, br)
    codes, start_row, n_rows, tile_e, tile_first, n_tiles = _dispatch_plan(route_t, cnt, EXP_TM)
    xs, dest = _sc_dispatch(xn, codes, start_row, n_rows, DISPATCH_CHUNK)
    ys = _experts(tile_e, tile_first, n_tiles, xs, w_gate[li], w_up[li], w_down[li])
    yab = _sc_combine(ys, dest, 2 * t_all, COMBINE_CHUNK).reshape(2, t_all, d // 2)
    y_p, y_s = _final(x1, yab, route, final_norm_g.reshape(1, d), n * l, ns)

    hf = hfin.reshape(n, N_LANE_TILES, 2, 8, SSM_STATE)
    re_p = hf[:, :, 0].reshape(1, n, N_SSM_GROUPS, SSM_STATE)
    im_p = hf[:, :, 1].reshape(1, n, N_SSM_GROUPS, SSM_STATE)
    re_s = hr_s.reshape(1, ns, N_SSM_GROUPS, SSM_STATE)
    im_s = hi_s.reshape(1, ns, N_SSM_GROUPS, SSM_STATE)
    return (y_p.reshape(n, l, d), y_s.reshape(ns, 1, d), re_p, im_p, re_s, im_s,
            vrow.reshape(1, ns, 1, D_GMLP))
```

```python
import math

import jax
import jax.numpy as jnp
from jax import lax
from jax.experimental import pallas as pl
from jax.experimental.pallas import tpu as pltpu
from jax.experimental.pallas import tpu_sc as plsc

D_MODEL = 1024
D_SSM = 512
D_GMLP = 512
SSM_GROUP = 16
N_SSM_GROUPS = 32
SSM_STATE = 64
CHUNK = 128
N_GMLP_HEADS = 4
GMLP_HEAD = 128
N_EXPERT_GROUPS = 4
EXPERTS_PER_GROUP = 8
N_EXPERTS = 32
D_EXPERT = 512
D_IN = 2048
EPS = 1e-6

LANES = 128
SUBLANES = 8
N_LANE_TILES = D_SSM // LANES
STATE_COLS = N_SSM_GROUPS * SSM_STATE
TILE_STATE = STATE_COLS // N_LANE_TILES
VMEM_LIMIT = 56 * 1024 * 1024

SC_CORES = 2
SC_SUBCORES = 16
SC_LANES = 16
SC_WORKERS = SC_CORES * SC_SUBCORES

FRONT_TL = 512
SSM_LC = 128
TOK_TM = 512
FINAL_TM = 512
EXP_TM = 512
DISPATCH_CHUNK = 80
COMBINE_CHUNK = 24

R_E1, R_E2, R_W1, R_W2, R_RANK1, R_RANK2, R_CODE1, R_CODE2 = 0, 1, 2, 3, 4, 5, 6, 7
CODE_BITS = 16
CODE_SHIFT = float(1 << CODE_BITS)

_INV_SQRT2 = 1.0 / math.sqrt(2.0)
_BF16 = jnp.bfloat16
_F32 = jnp.float32
_U32 = jnp.uint32


def _gelu(x):
    return 0.5 * x * (1.0 + lax.erf(x * _INV_SQRT2))


def _rms(x, g):
    return x * lax.rsqrt(jnp.mean(x * x, axis=-1, keepdims=True) + EPS) * g


def _dot(a, b):
    return jnp.dot(a, b, preferred_element_type=_F32)


def _dot_f32(a, b):
    return jnp.dot(a, b, preferred_element_type=_F32, precision=lax.Precision.HIGHEST)


def _pack_bf16_pair(x):
    w = x.shape[1] // 2
    hi = lax.bitcast_convert_type(x[:, :w].astype(_BF16).astype(_F32), _U32)
    lo = lax.bitcast_convert_type(x[:, w:].astype(_BF16).astype(_F32), _U32)
    return hi | (lo >> 16)


def _unpack_bf16_pair(p):
    hi = lax.bitcast_convert_type(p & jnp.uint32(0xFFFF0000), _F32)
    lo = lax.bitcast_convert_type(p << 16, _F32)
    return jnp.concatenate([hi, lo], axis=-1)


def _head_norm_gelu(vb, gn):
    v = _gelu(vb)
    parts = []
    for h in range(N_GMLP_HEADS):
        vh = v[:, h * GMLP_HEAD:(h + 1) * GMLP_HEAD]
        parts.append(vh * lax.rsqrt(jnp.mean(vh * vh, axis=-1, keepdims=True) + EPS))
    return jnp.concatenate(parts, axis=-1) * gn


def _front_prompt_kernel(x_ref, g1_ref, win_ref, gn_ref, ws_ref, bs_ref, gog_ref,
                         xa_ref, sg_ref, mixb_ref):
    x = x_ref[0]
    hn = _rms(x, g1_ref[...]).astype(_BF16)
    z = _dot(hn, win_ref[...])
    xa_ref[0] = z[:, :D_SSM]
    sg_ref[0] = jax.nn.sigmoid(z[:, D_SSM:2 * D_SSM])
    ub = _gelu(z[:, 2 * D_SSM:2 * D_SSM + D_GMLP])
    vbn = _head_norm_gelu(z[:, 2 * D_SSM + D_GMLP:], gn_ref[...]).astype(_BF16)
    tl = x.shape[0]
    rows = []
    for c in range(tl // CHUNK):
        heads = []
        for h in range(N_GMLP_HEADS):
            vh = vbn[c * CHUNK:(c + 1) * CHUNK, h * GMLP_HEAD:(h + 1) * GMLP_HEAD]
            heads.append(_dot(ws_ref[h], vh) + bs_ref[:, h:h + 1])
        rows.append(jnp.concatenate(heads, axis=-1))
    s = jnp.concatenate(rows, axis=0)
    mixb_ref[0] = _rms(ub * s, gog_ref[...]).astype(_BF16)


def _front_prompt(x, g1, win_bf, gn, ws_tril_bf, bs_t, gog):
    n, l, d = x.shape
    tl = FRONT_TL
    grid = (n, l // tl)
    const = lambda *shape: pl.BlockSpec(shape, lambda b, i: (0,) * len(shape))
    seq = lambda w: pl.BlockSpec((1, tl, w), lambda b, i: (b, i, 0))
    return pl.pallas_call(
        _front_prompt_kernel,
        grid=grid,
        in_specs=[seq(d), const(1, d), const(d, D_IN), const(1, D_GMLP),
                  const(N_GMLP_HEADS, CHUNK, CHUNK), const(CHUNK, N_GMLP_HEADS), const(1, D_GMLP)],
        out_specs=[seq(D_SSM), seq(D_SSM), seq(D_GMLP)],
        out_shape=[jax.ShapeDtypeStruct((n, l, D_SSM), _F32),
                   jax.ShapeDtypeStruct((n, l, D_SSM), _F32),
                   jax.ShapeDtypeStruct((n, l, D_GMLP), _BF16)],
        compiler_params=pltpu.CompilerParams(
            dimension_semantics=("arbitrary", "arbitrary"), vmem_limit_bytes=VMEM_LIMIT),
        name="front_prompt",
    )(x, g1, win_bf, gn, ws_tril_bf, bs_t, gog)


def _ssm_prompt_kernel(xa_ref, sg_ref, wb_ref, wc_ref, lbr_ref, lbi_ref, dsk_ref, gos_ref,
                       mixa_ref, hfin_ref, bu_ref, st_ref):
    lc = xa_ref.shape[1]
    rows = lc * SUBLANES
    pair = 2 * SUBLANES

    @pl.when(pl.program_id(0) == 0)
    def _():
        st_ref[...] = jnp.zeros_like(st_ref)

    xa = pltpu.einshape("btc->tbc", xa_ref[...]).reshape(rows, D_SSM)
    xa_bf = xa.astype(_BF16)
    for k in range(N_LANE_TILES):
        bu_ref[:, 2 * TILE_STATE * k:2 * TILE_STATE * (k + 1)] = _dot(
            xa_bf[:, k * LANES:(k + 1) * LANES], wb_ref[k])

    for kk in range(0, N_LANE_TILES, 2):
        tiles = (kk, kk + 1)
        cols = [(2 * TILE_STATE * k, 2 * TILE_STATE * k + TILE_STATE) for k in tiles]
        lbs = [(jnp.broadcast_to(lbr_ref[:, k * TILE_STATE:(k + 1) * TILE_STATE], (SUBLANES, TILE_STATE)),
                jnp.broadcast_to(lbi_ref[:, k * TILE_STATE:(k + 1) * TILE_STATE], (SUBLANES, TILE_STATE)))
               for k in tiles]

        def body(j, carry, cols=cols, lbs=lbs):
            r0 = pl.multiple_of(j * pair, pair)
            r1 = pl.multiple_of(r0 + SUBLANES, SUBLANES)
            out = []
            for q, ((c_re, c_im), (lr, li)) in enumerate(zip(cols, lbs)):
                hr, hi = carry[2 * q], carry[2 * q + 1]
                ar = lr * hr - li * hi + bu_ref[pl.ds(r0, SUBLANES), c_re:c_re + TILE_STATE]
                ai = lr * hi + li * hr + bu_ref[pl.ds(r0, SUBLANES), c_im:c_im + TILE_STATE]
                br = lr * ar - li * ai + bu_ref[pl.ds(r1, SUBLANES), c_re:c_re + TILE_STATE]
                bi = lr * ai + li * ar + bu_ref[pl.ds(r1, SUBLANES), c_im:c_im + TILE_STATE]
                bu_ref[pl.ds(r0, SUBLANES), c_re:c_re + TILE_STATE] = ar
                bu_ref[pl.ds(r0, SUBLANES), c_im:c_im + TILE_STATE] = ai
                bu_ref[pl.ds(r1, SUBLANES), c_re:c_re + TILE_STATE] = br
                bu_ref[pl.ds(r1, SUBLANES), c_im:c_im + TILE_STATE] = bi
                out += [br, bi]
            return tuple(out)

        init = tuple(st_ref[:, c:c + TILE_STATE] for c_pair in cols for c in c_pair)
        fin = lax.fori_loop(0, lc // 2, body, init, unroll=2)
        for q, (c_re, c_im) in enumerate(cols):
            st_ref[:, c_re:c_re + TILE_STATE] = fin[2 * q]
            st_ref[:, c_im:c_im + TILE_STATE] = fin[2 * q + 1]

    ys = []
    for k in range(N_LANE_TILES):
        hk = bu_ref[:, 2 * TILE_STATE * k:2 * TILE_STATE * (k + 1)].astype(_BF16)
        ys.append(_dot(hk, wc_ref[k]))
    y = jnp.concatenate(ys, axis=-1) + dsk_ref[...] * xa
    sg = pltpu.einshape("btc->tbc", sg_ref[...]).reshape(rows, D_SSM)
    mixa = _rms(_gelu(y) * sg, gos_ref[...]).reshape(lc, SUBLANES, D_SSM)
    mixa_ref[...] = pltpu.einshape("tbc->btc", mixa).astype(_BF16)
    hfin_ref[...] = st_ref[...]


def _ssm_prompt(xa, sg, wb, wc, lbr, lbi, dsk, gos):
    n, l, _ = xa.shape
    lc = SSM_LC
    const = lambda *shape: pl.BlockSpec(shape, lambda i: (0,) * len(shape))
    seq_spec = pl.BlockSpec((n, lc, D_SSM), lambda i: (0, i, 0))
    return pl.pallas_call(
        _ssm_prompt_kernel,
        grid=(l // lc,),
        in_specs=[seq_spec, seq_spec,
                  const(N_LANE_TILES, LANES, 2 * TILE_STATE), const(N_LANE_TILES, 2 * TILE_STATE, LANES),
                  const(1, STATE_COLS), const(1, STATE_COLS), const(1, D_SSM), const(1, D_SSM)],
        out_specs=[seq_spec, const(n, 2 * STATE_COLS)],
        out_shape=[jax.ShapeDtypeStruct((n, l, D_SSM), _BF16),
                   jax.ShapeDtypeStruct((n, 2 * STATE_COLS), _F32)],
        scratch_shapes=[pltpu.VMEM((lc * n, 2 * STATE_COLS), _F32),
                        pltpu.VMEM((n, 2 * STATE_COLS), _F32)],
        compiler_params=pltpu.CompilerParams(
            dimension_semantics=("arbitrary",), vmem_limit_bytes=VMEM_LIMIT),
        name="ssm_prompt",
    )(xa, sg, wb, wc, lbr, lbi, dsk, gos)


def _front_sample_kernel(x_ref, g1_ref, win_ref, gn_ref, w00_ref, b0_ref, gog_ref,
                         wb_ref, wc_ref, lbr_ref, lbi_ref, dsk_ref, gos_ref, h0r_ref, h0i_ref,
                         mix_ref, hr_ref, hi_ref, vrow_ref):
    x = x_ref[...]
    hn = _rms(x, g1_ref[...])
    z = _dot_f32(hn, win_ref[...])
    xa = z[:, :D_SSM]
    ys = []
    for k in range(N_LANE_TILES):
        bu = _dot_f32(xa[:, k * LANES:(k + 1) * LANES], wb_ref[k])
        sl = slice(k * TILE_STATE, (k + 1) * TILE_STATE)
        lr, li = lbr_ref[:, sl], lbi_ref[:, sl]
        h0r, h0i = h0r_ref[:, sl], h0i_ref[:, sl]
        nr = lr * h0r - li * h0i + bu[:, :TILE_STATE]
        ni = lr * h0i + li * h0r + bu[:, TILE_STATE:]
        hr_ref[:, sl] = nr
        hi_ref[:, sl] = ni
        ys.append(_dot_f32(jnp.concatenate([nr, ni], axis=-1), wc_ref[k]))
    y = jnp.concatenate(ys, axis=-1) + dsk_ref[...] * xa
    ya = _gelu(y) * jax.nn.sigmoid(z[:, D_SSM:2 * D_SSM])
    mix_ref[:, :D_SSM] = _rms(ya, gos_ref[...])
    ub = _gelu(z[:, 2 * D_SSM:2 * D_SSM + D_GMLP])
    vbn = _head_norm_gelu(z[:, 2 * D_SSM + D_GMLP:], gn_ref[...])
    vrow_ref[...] = vbn
    s = w00_ref[...] * vbn + b0_ref[...]
    mix_ref[:, D_SSM:] = _rms(ub * s, gog_ref[...])


def _front_sample(x, g1, win, gn, w00, b0, gog, wb, wc, lbr, lbi, dsk, gos, h0r, h0i):
    n = x.shape[0]
    vmem = pl.BlockSpec(memory_space=pltpu.VMEM)
    return pl.pallas_call(
        _front_sample_kernel,
        in_specs=[vmem] * 15,
        out_specs=[vmem] * 4,
        out_shape=[jax.ShapeDtypeStruct((n, D_MODEL), _F32),
                   jax.ShapeDtypeStruct((n, STATE_COLS), _F32),
                   jax.ShapeDtypeStruct((n, STATE_COLS), _F32),
                   jax.ShapeDtypeStruct((n, D_GMLP), _F32)],
        compiler_params=pltpu.CompilerParams(vmem_limit_bytes=VMEM_LIMIT),
        name="front_sample",
    )(x, g1, win, gn, w00, b0, gog, wb, wc, lbr, lbi, dsk, gos, h0r, h0i)


def _route(logits, base):
    tm = logits.shape[0]
    lane = lax.broadcasted_iota(jnp.int32, logits.shape, 1).astype(_F32)
    neg = jnp.float32(-jnp.inf)
    big = jnp.float32(LANES)
    is_g = (lane >= N_EXPERTS) & (lane < N_EXPERTS + N_EXPERT_GROUPS)
    gl = jnp.where(is_g, logits, neg)
    gmax = jnp.max(gl, axis=-1, keepdims=True)
    gi = jnp.min(jnp.where(is_g & (logits == gmax), lane, big), axis=-1, keepdims=True) - N_EXPERTS
    p_top = 1.0 / jnp.sum(jnp.where(is_g, jnp.exp(gl - gmax), 0.0), axis=-1, keepdims=True)
    lo = gi * EXPERTS_PER_GROUP
    in_grp = (lane >= lo) & (lane < lo + EXPERTS_PER_GROUP)
    m1 = jnp.max(jnp.where(in_grp, logits, neg), axis=-1, keepdims=True)
    i1 = jnp.min(jnp.where(in_grp & (logits == m1), lane, big), axis=-1, keepdims=True)
    rest = in_grp & (lane != i1)
    m2 = jnp.max(jnp.where(rest, logits, neg), axis=-1, keepdims=True)
    i2 = jnp.min(jnp.where(rest & (logits == m2), lane, big), axis=-1, keepdims=True)
    e2 = jnp.exp(m2 - m1)
    w1 = p_top / (1.0 + e2)
    w2 = p_top * e2 / (1.0 + e2)
    sel1 = lane == i1
    sel2 = lane == i2
    hits = jnp.where(sel1 | sel2, 1.0, 0.0)
    r_id = lax.broadcasted_iota(jnp.int32, (tm, tm), 0)
    c_id = lax.broadcasted_iota(jnp.int32, (tm, tm), 1)
    ltri = jnp.where(c_id < r_id, 1.0, 0.0).astype(_BF16)
    before = _dot(ltri, hits.astype(_BF16)) + base
    rank1 = jnp.sum(jnp.where(sel1, before, 0.0), axis=-1, keepdims=True)
    rank2 = jnp.sum(jnp.where(sel2, before, 0.0), axis=-1, keepdims=True)
    out = jnp.where(lane == R_E1, i1, 0.0)
    out = jnp.where(lane == R_E2, i2, out)
    out = jnp.where(lane == R_W1, w1, out)
    out = jnp.where(lane == R_W2, w2, out)
    out = jnp.where(lane == R_RANK1, rank1, out)
    out = jnp.where(lane == R_RANK2, rank2, out)
    out = jnp.where(lane == R_CODE1, i1 * CODE_SHIFT + rank1, out)
    out = jnp.where(lane == R_CODE2, i2 * CODE_SHIFT + rank2, out)
    return out, base + jnp.sum(hits, axis=0, keepdims=True)


def _mixer_out_prompt_kernel(x_ref, mixa_ref, mixb_ref, wo_ref, g2_ref, wr_ref, br_ref,
                             x1_ref, xn_ref, route_ref, route_t_ref, cnt_ref, base_ref):
    @pl.when((pl.program_id(0) == 0) & (pl.program_id(1) == 0))
    def _():
        base_ref[...] = jnp.zeros_like(base_ref)

    x1 = x_ref[0] + _dot(mixa_ref[0], wo_ref[:D_SSM, :]) + _dot(mixb_ref[0], wo_ref[D_SSM:, :])
    xn = _rms(x1, g2_ref[...])
    logits = _dot(xn.astype(_BF16), wr_ref[...]) + br_ref[...]
    route, base = _route(logits, base_ref[...])
    x1_ref[...] = x1
    xn_ref[...] = _pack_bf16_pair(xn)
    route_ref[...] = route
    route_t_ref[...] = route.T[:SUBLANES, :]
    base_ref[...] = base
    cnt_ref[...] = base


def _mixer_out_sample_kernel(x_ref, mix_ref, wo_ref, g2_ref, wr_ref, br_ref, cnt_in_ref,
                             x1_in, xn_in, route_in, route_t_in,
                             x1_ref, xn_ref, route_ref, route_t_ref, cnt_ref):
    del x1_in, xn_in, route_in, route_t_in
    x1 = (x_ref[...] + _dot_f32(mix_ref[:, :D_SSM], wo_ref[:D_SSM, :])
          + _dot_f32(mix_ref[:, D_SSM:], wo_ref[D_SSM:, :]))
    xn = _rms(x1, g2_ref[...])
    logits = _dot_f32(xn, wr_ref[...]) + br_ref[...]
    route, base = _route(logits, cnt_in_ref[...])
    x1_ref[...] = x1
    xn_ref[...] = _pack_bf16_pair(xn)
    route_ref[...] = route
    route_t_ref[...] = route.T[:SUBLANES, :]
    cnt_ref[...] = base


def _mixer_out(x_p, mixa, mixb, x_s, mix_s, wo, g2, wr, br):
    n, l, d = x_p.shape
    ns = x_s.shape[0]
    t_all = n * l + ns
    tm = TOK_TM
    per_seq = l // tm
    const = lambda *shape: pl.BlockSpec(shape, lambda b, i: (0,) * len(shape))
    seq = lambda w: pl.BlockSpec((1, tm, w), lambda b, i: (b, i, 0))
    tok = lambda w: pl.BlockSpec((tm, w), lambda b, i: (b * per_seq + i, 0))
    tok_shapes = [jax.ShapeDtypeStruct((t_all, d), _F32),
                  jax.ShapeDtypeStruct((t_all, d // 2), _U32),
                  jax.ShapeDtypeStruct((t_all, LANES), _F32),
                  jax.ShapeDtypeStruct((SUBLANES, t_all), _F32)]
    cnt_shape = jax.ShapeDtypeStruct((1, LANES), _F32)
    x1, xn, route, route_t, cnt = pl.pallas_call(
        _mixer_out_prompt_kernel,
        grid=(n, per_seq),
        in_specs=[seq(d), seq(D_SSM), seq(D_GMLP),
                  const(d, d), const(1, d), const(d, LANES), const(1, LANES)],
        out_specs=[tok(d), tok(d // 2), tok(LANES),
                   pl.BlockSpec((SUBLANES, tm), lambda b, i: (0, b * per_seq + i)), const(1, LANES)],
        out_shape=tok_shapes + [cnt_shape],
        scratch_shapes=[pltpu.VMEM((1, LANES), _F32)],
        compiler_params=pltpu.CompilerParams(
            dimension_semantics=("arbitrary", "arbitrary"), vmem_limit_bytes=VMEM_LIMIT),
        name="mixer_out_prompt",
    )(x_p, mixa, mixb, wo.astype(_BF16), g2, wr.astype(_BF16), br)
    tail = (n * l) // ns
    c1 = lambda *shape: pl.BlockSpec(shape, lambda i: (0,) * len(shape))
    anyspec = pl.BlockSpec(memory_space=pl.ANY)
    tail_spec = lambda w: pl.BlockSpec((ns, w), lambda i: (tail, 0))
    return pl.pallas_call(
        _mixer_out_sample_kernel,
        grid=(1,),
        in_specs=[c1(ns, d), c1(ns, d), c1(d, d), c1(1, d), c1(d, LANES), c1(1, LANES), c1(1, LANES),
                  anyspec, anyspec, anyspec, anyspec],
        out_specs=[tail_spec(d), tail_spec(d // 2), tail_spec(LANES),
                   pl.BlockSpec((SUBLANES, ns), lambda i: (0, tail)), c1(1, LANES)],
        out_shape=tok_shapes + [cnt_shape],
        input_output_aliases={7: 0, 8: 1, 9: 2, 10: 3},
        compiler_params=pltpu.CompilerParams(
            dimension_semantics=("arbitrary",), vmem_limit_bytes=VMEM_LIMIT),
        name="mixer_out_sample",
    )(x_s, mix_s, wo, g2, wr, br, cnt, x1, xn, route, route_t)


def _sc_stream(n_chunks, gather, write):
    gather(0).start()
    for j in range(n_chunks):
        if j + 1 < n_chunks:
            if j >= 1:
                write(j - 1).wait()
            gather(j + 1).start()
        gather(j).wait()
        write(j).start()
    if n_chunks >= 2:
        write(n_chunks - 2).wait()
    write(n_chunks - 1).wait()


def _sc_mesh():
    return plsc.VectorSubcoreMesh(core_axis_name="c", subcore_axis_name="s",
                                  num_cores=SC_CORES, num_subcores=SC_SUBCORES)


def _sc_buffers(chunk, w, dtype):
    return [pltpu.VMEM((chunk, w), dtype), pltpu.VMEM((chunk, w), dtype)] + [pltpu.SemaphoreType.DMA] * 4


def _sc_combine(table, idx, n_out, chunk):
    w = table.shape[1]
    rows_w = n_out // SC_WORKERS
    n_chunks = rows_w // chunk
    assert rows_w * SC_WORKERS == n_out and n_chunks * chunk == rows_w and rows_w % SUBLANES == 0

    def body(table_hbm, idx_hbm, out_hbm, idx_v, buf0, buf1, g0, g1, w0, w1):
        wid = lax.axis_index("s") * SC_CORES + lax.axis_index("c")
        base = pl.multiple_of(wid * rows_w, SUBLANES)
        pltpu.sync_copy(idx_hbm.at[pl.ds(base, rows_w)], idx_v)
        bufs, gsems, wsems = (buf0, buf1), (g0, g1), (w0, w1)

        def gather(j):
            return pltpu.make_async_copy(table_hbm.at[idx_v.at[pl.ds(j * chunk, chunk)]], bufs[j % 2], gsems[j % 2])

        def write(j):
            return pltpu.make_async_copy(bufs[j % 2], out_hbm.at[pl.ds(base + j * chunk, chunk)], wsems[j % 2])

        _sc_stream(n_chunks, gather, write)

    return pl.kernel(
        body,
        out_type=jax.ShapeDtypeStruct((n_out, w), table.dtype),
        mesh=_sc_mesh(),
        scratch_types=[pltpu.VMEM((rows_w,), jnp.int32)] + _sc_buffers(chunk, w, table.dtype),
        compiler_params=pltpu.CompilerParams(use_tc_tiling_on_sc=True),
        name="sc_combine",
    )(table, idx)


def _sc_dispatch(table, codes, start_row, n_out, chunk):
    t_all, w = table.shape
    n_pad = codes.shape[0]
    n_ent = 2 * t_all
    ent_w = n_pad // SC_WORKERS
    n_chunks = ent_w // chunk
    per_chunk = chunk // SC_LANES
    trash = n_out - (n_pad - n_ent)
    assert ent_w * SC_WORKERS == n_pad and n_chunks * chunk == ent_w
    assert per_chunk * SC_LANES == chunk and chunk <= LANES and n_pad - n_ent <= t_all

    def body(table_hbm, code_hbm, start_hbm, out_hbm, dest_hbm,
             code_v, dest_v, tok_v, dst_v, start_v, buf0, buf1, g0, g1, w0, w1):
        wid = lax.axis_index("s") * SC_CORES + lax.axis_index("c")
        ebase = pl.multiple_of(wid * ent_w, SUBLANES)
        pltpu.sync_copy(code_hbm.at[pl.ds(ebase, ent_w)], code_v)
        pltpu.sync_copy(start_hbm, start_v)
        lane = lax.iota(jnp.int32, SC_LANES)
        for j in range(n_chunks):
            for c in range(per_chunk):
                off = j * chunk + c * SC_LANES
                ent = ebase + off + lane
                code = code_v[pl.ds(off, SC_LANES)]
                d = plsc.load_gather(start_v, [code >> CODE_BITS]) + (code & ((1 << CODE_BITS) - 1))
                d = jnp.where(ent >= n_ent, trash + (ent - n_ent), d)
                tok = jnp.where(ent >= t_all, ent - t_all, ent)
                tok = jnp.where(tok >= t_all, tok - t_all, tok)
                dest_v[pl.ds(off, SC_LANES)] = d
                dst_v[j, pl.ds(c * SC_LANES, SC_LANES)] = d
                tok_v[j, pl.ds(c * SC_LANES, SC_LANES)] = tok
        pltpu.sync_copy(dest_v, dest_hbm.at[pl.ds(ebase, ent_w)])
        bufs, gsems, wsems = (buf0, buf1), (g0, g1), (w0, w1)

        def gather(j):
            return pltpu.make_async_copy(table_hbm.at[tok_v.at[j]], bufs[j % 2], gsems[j % 2])

        def scatter(j):
            return pltpu.make_async_copy(bufs[j % 2], out_hbm.at[dst_v.at[j]], wsems[j % 2])

        _sc_stream(n_chunks, gather, scatter)

    return pl.kernel(
        body,
        out_type=(jax.ShapeDtypeStruct((n_out, w), table.dtype), jax.ShapeDtypeStruct((n_pad,), jnp.int32)),
        mesh=_sc_mesh(),
        scratch_types=([pltpu.VMEM((ent_w,), jnp.int32), pltpu.VMEM((ent_w,), jnp.int32),
                        pltpu.VMEM((n_chunks, chunk), jnp.int32), pltpu.VMEM((n_chunks, chunk), jnp.int32),
                        pltpu.VMEM((LANES,), jnp.int32)] + _sc_buffers(chunk, w, table.dtype)),
        compiler_params=pltpu.CompilerParams(use_tc_tiling_on_sc=True, needs_layout_passes=False),
        name="sc_dispatch",
    )(table, codes, start_row)


def _experts_kernel(tile_start_ref, wg_ref, wu_ref, wd_ref, xs_hbm, ys_hbm,
                    wg_bf, wu_bf, wd_bf, xbuf, ybuf, xsem, ysem):
    e = pl.program_id(0)
    tm = xbuf.shape[1]
    t0 = tile_start_ref[e]
    nt = tile_start_ref[e + 1] - t0
    n_total = tile_start_ref[N_EXPERTS]

    def x_copy(g, slot):
        return pltpu.make_async_copy(xs_hbm.at[pl.ds(pl.multiple_of(g * tm, tm), tm)], xbuf.at[slot], xsem.at[slot])

    def y_copy(g, slot):
        return pltpu.make_async_copy(ybuf.at[slot], ys_hbm.at[pl.ds(pl.multiple_of(g * tm, tm), tm)], ysem.at[slot])

    @pl.when((e == 0) & (n_total > 0))
    def _():
        x_copy(0, 0).start()

    wg_bf[...] = wg_ref[0].astype(_BF16)
    wu_bf[...] = wu_ref[0].astype(_BF16)
    wd_bf[...] = wd_ref[0].astype(_BF16)

    def tile(j, carry):
        g = t0 + j
        slot = lax.rem(g, 2)
        x_copy(g, slot).wait()

        @pl.when(g + 1 < n_total)
        def _():
            x_copy(g + 1, 1 - slot).start()

        x = _unpack_bf16_pair(xbuf[slot]).astype(_BF16)
        a = _dot(x, wg_bf[...])
        u = _dot(x, wu_bf[...])
        h = (a * jax.nn.sigmoid(a) * u).astype(_BF16)
        y = _pack_bf16_pair(_dot(h, wd_bf[...]))

        @pl.when(g >= 2)
        def _():
            y_copy(g - 2, slot).wait()

        ybuf[slot] = y
        y_copy(g, slot).start()
        return carry

    lax.fori_loop(0, nt, tile, 0)

    @pl.when(e == N_EXPERTS - 1)
    def _():
        @pl.when(n_total >= 2)
        def _():
            y_copy(n_total - 2, lax.rem(n_total, 2)).wait()

        @pl.when(n_total >= 1)
        def _():
            y_copy(n_total - 1, lax.rem(n_total - 1, 2)).wait()


def _experts(tile_start, n_rows, xs, w_gate, w_up, w_down):
    dh = xs.shape[1]
    d = 2 * dh
    tm = EXP_TM
    wsel = lambda e, ts: (e, 0, 0)
    anyspec = pl.BlockSpec(memory_space=pl.ANY)
    grid_spec = pltpu.PrefetchScalarGridSpec(
        num_scalar_prefetch=1,
        grid=(N_EXPERTS,),
        in_specs=[pl.BlockSpec((1, d, D_EXPERT), wsel),
                  pl.BlockSpec((1, d, D_EXPERT), wsel),
                  pl.BlockSpec((1, D_EXPERT, d), wsel),
                  anyspec],
        out_specs=anyspec,
        scratch_shapes=[pltpu.VMEM((d, D_EXPERT), _BF16), pltpu.VMEM((d, D_EXPERT), _BF16),
                        pltpu.VMEM((D_EXPERT, d), _BF16),
                        pltpu.VMEM((2, tm, dh), _U32), pltpu.VMEM((2, tm, dh), _U32),
                        pltpu.SemaphoreType.DMA((2,)), pltpu.SemaphoreType.DMA((2,))],
    )
    return pl.pallas_call(
        _experts_kernel,
        grid_spec=grid_spec,
        out_shape=jax.ShapeDtypeStruct((n_rows, dh), _U32),
        compiler_params=pltpu.CompilerParams(
            dimension_semantics=("arbitrary",), vmem_limit_bytes=VMEM_LIMIT),
        name="experts",
    )(tile_start, w_gate, w_up, w_down, xs)


def _final_kernel(x1_ref, ya_ref, yb_ref, route_ref, gf_ref, y_ref):
    route = route_ref[...]
    x2 = (x1_ref[...] + route[:, R_W1:R_W1 + 1] * _unpack_bf16_pair(ya_ref[...])
          + route[:, R_W2:R_W2 + 1] * _unpack_bf16_pair(yb_ref[...]))
    y_ref[...] = _rms(x2, gf_ref[...])


def _final(x1, yab, route, gf, n_prompt, n_sample):
    d = x1.shape[1]

    def call(tm, first_block, n_rows, name):
        tok = lambda w: pl.BlockSpec((tm, w), lambda i: (first_block + i, 0))
        sel = lambda k: pl.BlockSpec((None, tm, d // 2), lambda i: (k, first_block + i, 0))
        return pl.pallas_call(
            _final_kernel,
            grid=(n_rows // tm,),
            in_specs=[tok(d), sel(0), sel(1), tok(LANES), pl.BlockSpec((1, d), lambda i: (0, 0))],
            out_specs=pl.BlockSpec((tm, d), lambda i: (i, 0)),
            out_shape=jax.ShapeDtypeStruct((n_rows, d), _F32),
            compiler_params=pltpu.CompilerParams(
                dimension_semantics=("arbitrary",), vmem_limit_bytes=VMEM_LIMIT),
            name=name,
        )(x1, yab, yab, route, gf)

    return (call(FINAL_TM, 0, n_prompt, "final_prompt"),
            call(n_sample, n_prompt // n_sample, n_sample, "final_sample"))


def _ssm_params(lam_re, lam_im, log_dt, b_re, b_im, c_re, c_im, d_skip):
    dt = jnp.exp(log_dt)[:, None]
    mag = jnp.exp(lam_re * dt)
    ang = lam_im * dt
    lb_re = mag * jnp.cos(ang)
    lb_im = mag * jnp.sin(ang)
    den = lam_re * lam_re + lam_im * lam_im
    nr = lb_re - 1.0
    ni = lb_im
    k_re = (nr * lam_re + ni * lam_im) / den
    k_im = (ni * lam_re - nr * lam_im) / den
    bb_re = k_re[:, :, None] * b_re - k_im[:, :, None] * b_im
    bb_im = k_re[:, :, None] * b_im + k_im[:, :, None] * b_re
    eye = jnp.eye(SUBLANES, dtype=_F32)

    def b_blocks(bb):
        t = jnp.transpose(bb, (0, 2, 1)).reshape(N_LANE_TILES, 8, SSM_GROUP, SSM_STATE)
        return jnp.einsum('kahp,ab->kahbp', t, eye).reshape(N_LANE_TILES, LANES, TILE_STATE)

    def c_blocks(c):
        t = c.reshape(N_LANE_TILES, 8, SSM_GROUP, SSM_STATE)
        return jnp.einsum('kahp,ab->kapbh', t, eye).reshape(N_LANE_TILES, TILE_STATE, LANES)

    wb = jnp.concatenate([b_blocks(bb_re), b_blocks(bb_im)], axis=-1)
    wc = jnp.concatenate([c_blocks(c_re), -c_blocks(c_im)], axis=1)
    return (wb, wc, lb_re.reshape(1, STATE_COLS), lb_im.reshape(1, STATE_COLS),
            d_skip.reshape(1, D_SSM))


def _dispatch_plan(route_t, cnt, tm):
    t_all = route_t.shape[1]
    codes = route_t[R_CODE1:R_CODE2 + 1].astype(jnp.int32).reshape(-1)
    per_pass = SC_WORKERS * DISPATCH_CHUNK
    codes = jnp.pad(codes, (0, -(2 * t_all) % per_pass))
    counts = cnt[0, :N_EXPERTS].astype(jnp.int32)
    tiles_per = (counts + tm - 1) // tm
    tile_start = jnp.concatenate([jnp.zeros((1,), jnp.int32), jnp.cumsum(tiles_per)])
    start_row = jnp.zeros((LANES,), jnp.int32).at[:N_EXPERTS].set(tile_start[:N_EXPERTS] * tm)
    n_tiles_max = (2 * t_all + N_EXPERTS * (tm - 1)) // tm
    return codes, start_row, (n_tiles_max + 1) * tm, tile_start


def kernel(x_prompt, x_sample, state_ssm_re, state_ssm_im, norm1_g, w_in, lam_re, lam_im, log_dt, ssm_b_re, ssm_b_im, ssm_c_re, ssm_c_im, ssm_d, gmlp_norm_g, gmlp_w_s, gmlp_b_s, out_norm_ssm_g, out_norm_gmlp_g, w_out, norm2_g, w_router_group, b_router_group, w_router_expert, b_router_expert, w_gate, w_up, w_down, final_norm_g):
    n, l, d = x_prompt.shape
    ns = x_sample.shape[0]
    t_all = n * l + ns
    li = 0
    g1 = norm1_g[li].reshape(1, d)
    gn = gmlp_norm_g[li].reshape(1, D_GMLP)
    tril = jnp.tril(jnp.ones((CHUNK, CHUNK), dtype=bool))
    ws_tril = jnp.where(tril[None], gmlp_w_s[li], 0.0)
    bs = gmlp_b_s[li]
    gog = out_norm_gmlp_g[li].reshape(1, D_GMLP)
    gos = out_norm_ssm_g[li].reshape(1, D_SSM)
    wb, wc, lbr, lbi, dsk = _ssm_params(lam_re[li], lam_im[li], log_dt[li], ssm_b_re[li], ssm_b_im[li],
                                        ssm_c_re[li], ssm_c_im[li], ssm_d[li])
    g2 = norm2_g[li].reshape(1, d)
    pad = LANES - N_EXPERTS - N_EXPERT_GROUPS
    wr = jnp.concatenate([w_router_expert[li], w_router_group[li], jnp.zeros((d, pad), _F32)], axis=1)
    br = jnp.concatenate([b_router_expert[li], b_router_group[li], jnp.zeros((pad,), _F32)]).reshape(1, LANES)

    xa, sg, mixb = _front_prompt(x_prompt, g1, w_in[li].astype(_BF16), gn, ws_tril.astype(_BF16), bs.T, gog)
    mixa, hfin = _ssm_prompt(xa, sg, wb.astype(_BF16), wc.astype(_BF16), lbr, lbi, dsk, gos)
    w00 = jnp.repeat(ws_tril[:, 0, 0], GMLP_HEAD).reshape(1, D_GMLP)
    b0 = jnp.repeat(bs[:, 0], GMLP_HEAD).reshape(1, D_GMLP)
    mix_s, hr_s, hi_s, vrow = _front_sample(
        x_sample.reshape(ns, d), g1, w_in[li], gn, w00, b0, gog, wb, wc, lbr, lbi, dsk, gos,
        state_ssm_re[li].reshape(ns, STATE_COLS), state_ssm_im[li].reshape(ns, STATE_COLS))

    x1, xn, route, route_t, cnt = _mixer_out(x_prompt, mixa, mixb, x_sample.reshape(ns, d), mix_s,
                                             w_out[li], g2, wr, br)
    codes, start_row, n_rows, tile_start = _dispatch_plan(route_t, cnt, EXP_TM)
    xs, dest = _sc_dispatch(xn, codes, start_row, n_rows, DISPATCH_CHUNK)
    ys = _experts(tile_start, n_rows, xs, w_gate[li], w_up[li], w_down[li])
    yab = _sc_combine(ys, dest, 2 * t_all, COMBINE_CHUNK).reshape(2, t_all, d // 2)
    y_p, y_s = _final(x1, yab, route, final_norm_g.reshape(1, d), n * l, ns)

    hf = hfin.reshape(n, N_LANE_TILES, 2, 8, SSM_STATE)
    re_p = hf[:, :, 0].reshape(1, n, N_SSM_GROUPS, SSM_STATE)
    im_p = hf[:, :, 1].reshape(1, n, N_SSM_GROUPS, SSM_STATE)
    re_s = hr_s.reshape(1, ns, N_SSM_GROUPS, SSM_STATE)
    im_s = hi_s.reshape(1, ns, N_SSM_GROUPS, SSM_STATE)
    return (y_p.reshape(n, l, d), y_s.reshape(ns, 1, d), re_p, im_p, re_s, im_s,
            vrow.reshape(1, ns, 1, D_GMLP))
```

```python
import math

import jax
import jax.numpy as jnp
from jax import lax
from jax.experimental import pallas as pl
from jax.experimental.pallas import tpu as pltpu
from jax.experimental.pallas import tpu_sc as plsc

D_MODEL = 1024
D_SSM = 512
D_GMLP = 512
SSM_GROUP = 16
N_SSM_GROUPS = 32
SSM_STATE = 64
CHUNK = 128
N_GMLP_HEADS = 4
GMLP_HEAD = 128
N_EXPERT_GROUPS = 4
EXPERTS_PER_GROUP = 8
N_EXPERTS = 32
D_EXPERT = 512
D_IN = 2048
EPS = 1e-6

LANES = 128
SUBLANES = 8
N_LANE_TILES = D_SSM // LANES
STATE_COLS = N_SSM_GROUPS * SSM_STATE
TILE_STATE = STATE_COLS // N_LANE_TILES
VMEM_LIMIT = 56 * 1024 * 1024

SC_CORES = 2
SC_SUBCORES = 16
SC_LANES = 16
SC_WORKERS = SC_CORES * SC_SUBCORES

FRONT_TL = 512
SSM_LC = 128
TOK_TM = 512
FINAL_TM = 512
EXP_UNIT = 128
EXP_CLASSES = 4
DISPATCH_CHUNK = 80
COMBINE_CHUNK = 24

R_E1, R_E2, R_W1, R_W2, R_RANK1, R_RANK2, R_CODE1, R_CODE2 = 0, 1, 2, 3, 4, 5, 6, 7
CODE_BITS = 16
CODE_SHIFT = float(1 << CODE_BITS)

_INV_SQRT2 = 1.0 / math.sqrt(2.0)
_BF16 = jnp.bfloat16
_F32 = jnp.float32
_U32 = jnp.uint32


def _gelu(x):
    return 0.5 * x * (1.0 + lax.erf(x * _INV_SQRT2))


def _rms(x, g):
    return x * lax.rsqrt(jnp.mean(x * x, axis=-1, keepdims=True) + EPS) * g


def _dot(a, b):
    return jnp.dot(a, b, preferred_element_type=_F32)


def _dot_f32(a, b):
    return jnp.dot(a, b, preferred_element_type=_F32, precision=lax.Precision.HIGHEST)


def _pack_bf16_pair(x):
    w = x.shape[1] // 2
    hi = lax.bitcast_convert_type(x[:, :w].astype(_BF16).astype(_F32), _U32)
    lo = lax.bitcast_convert_type(x[:, w:].astype(_BF16).astype(_F32), _U32)
    return hi | (lo >> 16)


def _unpack_bf16_pair(p):
    hi = lax.bitcast_convert_type(p & jnp.uint32(0xFFFF0000), _F32)
    lo = lax.bitcast_convert_type(p << 16, _F32)
    return jnp.concatenate([hi, lo], axis=-1)


def _head_norm_gelu(vb, gn):
    v = _gelu(vb)
    parts = []
    for h in range(N_GMLP_HEADS):
        vh = v[:, h * GMLP_HEAD:(h + 1) * GMLP_HEAD]
        parts.append(vh * lax.rsqrt(jnp.mean(vh * vh, axis=-1, keepdims=True) + EPS))
    return jnp.concatenate(parts, axis=-1) * gn


def _front_prompt_kernel(x_ref, g1_ref, win_ref, gn_ref, ws_ref, bs_ref, gog_ref,
                         xa_ref, sg_ref, mixb_ref):
    x = x_ref[0]
    hn = _rms(x, g1_ref[...]).astype(_BF16)
    z = _dot(hn, win_ref[...])
    xa_ref[0] = z[:, :D_SSM]
    sg_ref[0] = jax.nn.sigmoid(z[:, D_SSM:2 * D_SSM])
    ub = _gelu(z[:, 2 * D_SSM:2 * D_SSM + D_GMLP])
    vbn = _head_norm_gelu(z[:, 2 * D_SSM + D_GMLP:], gn_ref[...]).astype(_BF16)
    tl = x.shape[0]
    rows = []
    for c in range(tl // CHUNK):
        heads = []
        for h in range(N_GMLP_HEADS):
            vh = vbn[c * CHUNK:(c + 1) * CHUNK, h * GMLP_HEAD:(h + 1) * GMLP_HEAD]
            heads.append(_dot(ws_ref[h], vh) + bs_ref[:, h:h + 1])
        rows.append(jnp.concatenate(heads, axis=-1))
    s = jnp.concatenate(rows, axis=0)
    mixb_ref[0] = _rms(ub * s, gog_ref[...]).astype(_BF16)


def _front_prompt(x, g1, win_bf, gn, ws_tril_bf, bs_t, gog):
    n, l, d = x.shape
    tl = FRONT_TL
    grid = (n, l // tl)
    const = lambda *shape: pl.BlockSpec(shape, lambda b, i: (0,) * len(shape))
    seq = lambda w: pl.BlockSpec((1, tl, w), lambda b, i: (b, i, 0))
    return pl.pallas_call(
        _front_prompt_kernel,
        grid=grid,
        in_specs=[seq(d), const(1, d), const(d, D_IN), const(1, D_GMLP),
                  const(N_GMLP_HEADS, CHUNK, CHUNK), const(CHUNK, N_GMLP_HEADS), const(1, D_GMLP)],
        out_specs=[seq(D_SSM), seq(D_SSM), seq(D_GMLP)],
        out_shape=[jax.ShapeDtypeStruct((n, l, D_SSM), _F32),
                   jax.ShapeDtypeStruct((n, l, D_SSM), _F32),
                   jax.ShapeDtypeStruct((n, l, D_GMLP), _BF16)],
        compiler_params=pltpu.CompilerParams(
            dimension_semantics=("arbitrary", "arbitrary"), vmem_limit_bytes=VMEM_LIMIT),
        name="front_prompt",
    )(x, g1, win_bf, gn, ws_tril_bf, bs_t, gog)


def _ssm_prompt_kernel(xa_ref, sg_ref, wb_ref, wc_ref, lbr_ref, lbi_ref, dsk_ref, gos_ref,
                       mixa_ref, hfin_ref, bu_ref, st_ref):
    lc = xa_ref.shape[1]
    rows = lc * SUBLANES
    pair = 2 * SUBLANES

    @pl.when(pl.program_id(0) == 0)
    def _():
        st_ref[...] = jnp.zeros_like(st_ref)

    xa = pltpu.einshape("btc->tbc", xa_ref[...]).reshape(rows, D_SSM)
    xa_bf = xa.astype(_BF16)
    for k in range(N_LANE_TILES):
        bu_ref[:, 2 * TILE_STATE * k:2 * TILE_STATE * (k + 1)] = _dot(
            xa_bf[:, k * LANES:(k + 1) * LANES], wb_ref[k])

    for kk in range(0, N_LANE_TILES, 2):
        tiles = (kk, kk + 1)
        cols = [(2 * TILE_STATE * k, 2 * TILE_STATE * k + TILE_STATE) for k in tiles]
        lbs = [(jnp.broadcast_to(lbr_ref[:, k * TILE_STATE:(k + 1) * TILE_STATE], (SUBLANES, TILE_STATE)),
                jnp.broadcast_to(lbi_ref[:, k * TILE_STATE:(k + 1) * TILE_STATE], (SUBLANES, TILE_STATE)))
               for k in tiles]

        def body(j, carry, cols=cols, lbs=lbs):
            r0 = pl.multiple_of(j * pair, pair)
            r1 = pl.multiple_of(r0 + SUBLANES, SUBLANES)
            out = []
            for q, ((c_re, c_im), (lr, li)) in enumerate(zip(cols, lbs)):
                hr, hi = carry[2 * q], carry[2 * q + 1]
                ar = lr * hr - li * hi + bu_ref[pl.ds(r0, SUBLANES), c_re:c_re + TILE_STATE]
                ai = lr * hi + li * hr + bu_ref[pl.ds(r0, SUBLANES), c_im:c_im + TILE_STATE]
                br = lr * ar - li * ai + bu_ref[pl.ds(r1, SUBLANES), c_re:c_re + TILE_STATE]
                bi = lr * ai + li * ar + bu_ref[pl.ds(r1, SUBLANES), c_im:c_im + TILE_STATE]
                bu_ref[pl.ds(r0, SUBLANES), c_re:c_re + TILE_STATE] = ar
                bu_ref[pl.ds(r0, SUBLANES), c_im:c_im + TILE_STATE] = ai
                bu_ref[pl.ds(r1, SUBLANES), c_re:c_re + TILE_STATE] = br
                bu_ref[pl.ds(r1, SUBLANES), c_im:c_im + TILE_STATE] = bi
                out += [br, bi]
            return tuple(out)

        init = tuple(st_ref[:, c:c + TILE_STATE] for c_pair in cols for c in c_pair)
        fin = lax.fori_loop(0, lc // 2, body, init, unroll=2)
        for q, (c_re, c_im) in enumerate(cols):
            st_ref[:, c_re:c_re + TILE_STATE] = fin[2 * q]
            st_ref[:, c_im:c_im + TILE_STATE] = fin[2 * q + 1]

    ys = []
    for k in range(N_LANE_TILES):
        hk = bu_ref[:, 2 * TILE_STATE * k:2 * TILE_STATE * (k + 1)].astype(_BF16)
        ys.append(_dot(hk, wc_ref[k]))
    y = jnp.concatenate(ys, axis=-1) + dsk_ref[...] * xa
    sg = pltpu.einshape("btc->tbc", sg_ref[...]).reshape(rows, D_SSM)
    mixa = _rms(_gelu(y) * sg, gos_ref[...]).reshape(lc, SUBLANES, D_SSM)
    mixa_ref[...] = pltpu.einshape("tbc->btc", mixa).astype(_BF16)
    hfin_ref[...] = st_ref[...]


def _ssm_prompt(xa, sg, wb, wc, lbr, lbi, dsk, gos):
    n, l, _ = xa.shape
    lc = SSM_LC
    const = lambda *shape: pl.BlockSpec(shape, lambda i: (0,) * len(shape))
    seq_spec = pl.BlockSpec((n, lc, D_SSM), lambda i: (0, i, 0))
    return pl.pallas_call(
        _ssm_prompt_kernel,
        grid=(l // lc,),
        in_specs=[seq_spec, seq_spec,
                  const(N_LANE_TILES, LANES, 2 * TILE_STATE), const(N_LANE_TILES, 2 * TILE_STATE, LANES),
                  const(1, STATE_COLS), const(1, STATE_COLS), const(1, D_SSM), const(1, D_SSM)],
        out_specs=[seq_spec, const(n, 2 * STATE_COLS)],
        out_shape=[jax.ShapeDtypeStruct((n, l, D_SSM), _BF16),
                   jax.ShapeDtypeStruct((n, 2 * STATE_COLS), _F32)],
        scratch_shapes=[pltpu.VMEM((lc * n, 2 * STATE_COLS), _F32),
                        pltpu.VMEM((n, 2 * STATE_COLS), _F32)],
        compiler_params=pltpu.CompilerParams(
            dimension_semantics=("arbitrary",), vmem_limit_bytes=VMEM_LIMIT),
        name="ssm_prompt",
    )(xa, sg, wb, wc, lbr, lbi, dsk, gos)


def _front_sample_kernel(x_ref, g1_ref, win_ref, gn_ref, w00_ref, b0_ref, gog_ref,
                         wb_ref, wc_ref, lbr_ref, lbi_ref, dsk_ref, gos_ref, h0r_ref, h0i_ref,
                         mix_ref, hr_ref, hi_ref, vrow_ref):
    x = x_ref[...]
    hn = _rms(x, g1_ref[...])
    z = _dot_f32(hn, win_ref[...])
    xa = z[:, :D_SSM]
    ys = []
    for k in range(N_LANE_TILES):
        bu = _dot_f32(xa[:, k * LANES:(k + 1) * LANES], wb_ref[k])
        sl = slice(k * TILE_STATE, (k + 1) * TILE_STATE)
        lr, li = lbr_ref[:, sl], lbi_ref[:, sl]
        h0r, h0i = h0r_ref[:, sl], h0i_ref[:, sl]
        nr = lr * h0r - li * h0i + bu[:, :TILE_STATE]
        ni = lr * h0i + li * h0r + bu[:, TILE_STATE:]
        hr_ref[:, sl] = nr
        hi_ref[:, sl] = ni
        ys.append(_dot_f32(jnp.concatenate([nr, ni], axis=-1), wc_ref[k]))
    y = jnp.concatenate(ys, axis=-1) + dsk_ref[...] * xa
    ya = _gelu(y) * jax.nn.sigmoid(z[:, D_SSM:2 * D_SSM])
    mix_ref[:, :D_SSM] = _rms(ya, gos_ref[...])
    ub = _gelu(z[:, 2 * D_SSM:2 * D_SSM + D_GMLP])
    vbn = _head_norm_gelu(z[:, 2 * D_SSM + D_GMLP:], gn_ref[...])
    vrow_ref[...] = vbn
    s = w00_ref[...] * vbn + b0_ref[...]
    mix_ref[:, D_SSM:] = _rms(ub * s, gog_ref[...])


def _front_sample(x, g1, win, gn, w00, b0, gog, wb, wc, lbr, lbi, dsk, gos, h0r, h0i):
    n = x.shape[0]
    vmem = pl.BlockSpec(memory_space=pltpu.VMEM)
    return pl.pallas_call(
        _front_sample_kernel,
        in_specs=[vmem] * 15,
        out_specs=[vmem] * 4,
        out_shape=[jax.ShapeDtypeStruct((n, D_MODEL), _F32),
                   jax.ShapeDtypeStruct((n, STATE_COLS), _F32),
                   jax.ShapeDtypeStruct((n, STATE_COLS), _F32),
                   jax.ShapeDtypeStruct((n, D_GMLP), _F32)],
        compiler_params=pltpu.CompilerParams(vmem_limit_bytes=VMEM_LIMIT),
        name="front_sample",
    )(x, g1, win, gn, w00, b0, gog, wb, wc, lbr, lbi, dsk, gos, h0r, h0i)


def _route(logits, base):
    tm = logits.shape[0]
    lane = lax.broadcasted_iota(jnp.int32, logits.shape, 1).astype(_F32)
    neg = jnp.float32(-jnp.inf)
    big = jnp.float32(LANES)
    is_g = (lane >= N_EXPERTS) & (lane < N_EXPERTS + N_EXPERT_GROUPS)
    gl = jnp.where(is_g, logits, neg)
    gmax = jnp.max(gl, axis=-1, keepdims=True)
    gi = jnp.min(jnp.where(is_g & (logits == gmax), lane, big), axis=-1, keepdims=True) - N_EXPERTS
    p_top = 1.0 / jnp.sum(jnp.where(is_g, jnp.exp(gl - gmax), 0.0), axis=-1, keepdims=True)
    lo = gi * EXPERTS_PER_GROUP
    in_grp = (lane >= lo) & (lane < lo + EXPERTS_PER_GROUP)
    m1 = jnp.max(jnp.where(in_grp, logits, neg), axis=-1, keepdims=True)
    i1 = jnp.min(jnp.where(in_grp & (logits == m1), lane, big), axis=-1, keepdims=True)
    rest = in_grp & (lane != i1)
    m2 = jnp.max(jnp.where(rest, logits, neg), axis=-1, keepdims=True)
    i2 = jnp.min(jnp.where(rest & (logits == m2), lane, big), axis=-1, keepdims=True)
    e2 = jnp.exp(m2 - m1)
    w1 = p_top / (1.0 + e2)
    w2 = p_top * e2 / (1.0 + e2)
    sel1 = lane == i1
    sel2 = lane == i2
    hits = jnp.where(sel1 | sel2, 1.0, 0.0)
    r_id = lax.broadcasted_iota(jnp.int32, (tm, tm), 0)
    c_id = lax.broadcasted_iota(jnp.int32, (tm, tm), 1)
    ltri = jnp.where(c_id < r_id, 1.0, 0.0).astype(_BF16)
    before = _dot(ltri, hits.astype(_BF16)) + base
    rank1 = jnp.sum(jnp.where(sel1, before, 0.0), axis=-1, keepdims=True)
    rank2 = jnp.sum(jnp.where(sel2, before, 0.0), axis=-1, keepdims=True)
    out = jnp.where(lane == R_E1, i1, 0.0)
    out = jnp.where(lane == R_E2, i2, out)
    out = jnp.where(lane == R_W1, w1, out)
    out = jnp.where(lane == R_W2, w2, out)
    out = jnp.where(lane == R_RANK1, rank1, out)
    out = jnp.where(lane == R_RANK2, rank2, out)
    out = jnp.where(lane == R_CODE1, i1 * CODE_SHIFT + rank1, out)
    out = jnp.where(lane == R_CODE2, i2 * CODE_SHIFT + rank2, out)
    return out, base + jnp.sum(hits, axis=0, keepdims=True)


def _mixer_out_prompt_kernel(x_ref, mixa_ref, mixb_ref, wo_ref, g2_ref, wr_ref, br_ref,
                             x1_ref, xn_ref, route_ref, route_t_ref, cnt_ref, base_ref):
    @pl.when((pl.program_id(0) == 0) & (pl.program_id(1) == 0))
    def _():
        base_ref[...] = jnp.zeros_like(base_ref)

    x1 = x_ref[0] + _dot(mixa_ref[0], wo_ref[:D_SSM, :]) + _dot(mixb_ref[0], wo_ref[D_SSM:, :])
    xn = _rms(x1, g2_ref[...])
    logits = _dot(xn.astype(_BF16), wr_ref[...]) + br_ref[...]
    route, base = _route(logits, base_ref[...])
    x1_ref[...] = x1
    xn_ref[...] = _pack_bf16_pair(xn)
    route_ref[...] = route
    route_t_ref[...] = route.T[:SUBLANES, :]
    base_ref[...] = base
    cnt_ref[...] = base


def _mixer_out_sample_kernel(x_ref, mix_ref, wo_ref, g2_ref, wr_ref, br_ref, cnt_in_ref,
                             x1_in, xn_in, route_in, route_t_in,
                             x1_ref, xn_ref, route_ref, route_t_ref, cnt_ref):
    del x1_in, xn_in, route_in, route_t_in
    x1 = (x_ref[...] + _dot_f32(mix_ref[:, :D_SSM], wo_ref[:D_SSM, :])
          + _dot_f32(mix_ref[:, D_SSM:], wo_ref[D_SSM:, :]))
    xn = _rms(x1, g2_ref[...])
    logits = _dot_f32(xn, wr_ref[...]) + br_ref[...]
    route, base = _route(logits, cnt_in_ref[...])
    x1_ref[...] = x1
    xn_ref[...] = _pack_bf16_pair(xn)
    route_ref[...] = route
    route_t_ref[...] = route.T[:SUBLANES, :]
    cnt_ref[...] = base


def _mixer_out(x_p, mixa, mixb, x_s, mix_s, wo, g2, wr, br):
    n, l, d = x_p.shape
    ns = x_s.shape[0]
    t_all = n * l + ns
    tm = TOK_TM
    per_seq = l // tm
    const = lambda *shape: pl.BlockSpec(shape, lambda b, i: (0,) * len(shape))
    seq = lambda w: pl.BlockSpec((1, tm, w), lambda b, i: (b, i, 0))
    tok = lambda w: pl.BlockSpec((tm, w), lambda b, i: (b * per_seq + i, 0))
    tok_shapes = [jax.ShapeDtypeStruct((t_all, d), _F32),
                  jax.ShapeDtypeStruct((t_all, d // 2), _U32),
                  jax.ShapeDtypeStruct((t_all, LANES), _F32),
                  jax.ShapeDtypeStruct((SUBLANES, t_all), _F32)]
    cnt_shape = jax.ShapeDtypeStruct((1, LANES), _F32)
    x1, xn, route, route_t, cnt = pl.pallas_call(
        _mixer_out_prompt_kernel,
        grid=(n, per_seq),
        in_specs=[seq(d), seq(D_SSM), seq(D_GMLP),
                  const(d, d), const(1, d), const(d, LANES), const(1, LANES)],
        out_specs=[tok(d), tok(d // 2), tok(LANES),
                   pl.BlockSpec((SUBLANES, tm), lambda b, i: (0, b * per_seq + i)), const(1, LANES)],
        out_shape=tok_shapes + [cnt_shape],
        scratch_shapes=[pltpu.VMEM((1, LANES), _F32)],
        compiler_params=pltpu.CompilerParams(
            dimension_semantics=("arbitrary", "arbitrary"), vmem_limit_bytes=VMEM_LIMIT),
        name="mixer_out_prompt",
    )(x_p, mixa, mixb, wo.astype(_BF16), g2, wr.astype(_BF16), br)
    tail = (n * l) // ns
    c1 = lambda *shape: pl.BlockSpec(shape, lambda i: (0,) * len(shape))
    anyspec = pl.BlockSpec(memory_space=pl.ANY)
    tail_spec = lambda w: pl.BlockSpec((ns, w), lambda i: (tail, 0))
    return pl.pallas_call(
        _mixer_out_sample_kernel,
        grid=(1,),
        in_specs=[c1(ns, d), c1(ns, d), c1(d, d), c1(1, d), c1(d, LANES), c1(1, LANES), c1(1, LANES),
                  anyspec, anyspec, anyspec, anyspec],
        out_specs=[tail_spec(d), tail_spec(d // 2), tail_spec(LANES),
                   pl.BlockSpec((SUBLANES, ns), lambda i: (0, tail)), c1(1, LANES)],
        out_shape=tok_shapes + [cnt_shape],
        input_output_aliases={7: 0, 8: 1, 9: 2, 10: 3},
        compiler_params=pltpu.CompilerParams(
            dimension_semantics=("arbitrary",), vmem_limit_bytes=VMEM_LIMIT),
        name="mixer_out_sample",
    )(x_s, mix_s, wo, g2, wr, br, cnt, x1, xn, route, route_t)


def _sc_stream(n_chunks, gather, write):
    gather(0).start()
    for j in range(n_chunks):
        if j + 1 < n_chunks:
            if j >= 1:
                write(j - 1).wait()
            gather(j + 1).start()
        gather(j).wait()
        write(j).start()
    if n_chunks >= 2:
        write(n_chunks - 2).wait()
    write(n_chunks - 1).wait()


def _sc_mesh():
    return plsc.VectorSubcoreMesh(core_axis_name="c", subcore_axis_name="s",
                                  num_cores=SC_CORES, num_subcores=SC_SUBCORES)


def _sc_buffers(chunk, w, dtype):
    return [pltpu.VMEM((chunk, w), dtype), pltpu.VMEM((chunk, w), dtype)] + [pltpu.SemaphoreType.DMA] * 4


def _sc_combine(table, idx, n_out, chunk):
    w = table.shape[1]
    rows_w = n_out // SC_WORKERS
    n_chunks = rows_w // chunk
    assert rows_w * SC_WORKERS == n_out and n_chunks * chunk == rows_w and rows_w % SUBLANES == 0

    def body(table_hbm, idx_hbm, out_hbm, idx_v, buf0, buf1, g0, g1, w0, w1):
        wid = lax.axis_index("s") * SC_CORES + lax.axis_index("c")
        base = pl.multiple_of(wid * rows_w, SUBLANES)
        pltpu.sync_copy(idx_hbm.at[pl.ds(base, rows_w)], idx_v)
        bufs, gsems, wsems = (buf0, buf1), (g0, g1), (w0, w1)

        def gather(j):
            return pltpu.make_async_copy(table_hbm.at[idx_v.at[pl.ds(j * chunk, chunk)]], bufs[j % 2], gsems[j % 2])

        def write(j):
            return pltpu.make_async_copy(bufs[j % 2], out_hbm.at[pl.ds(base + j * chunk, chunk)], wsems[j % 2])

        _sc_stream(n_chunks, gather, write)

    return pl.kernel(
        body,
        out_type=jax.ShapeDtypeStruct((n_out, w), table.dtype),
        mesh=_sc_mesh(),
        scratch_types=[pltpu.VMEM((rows_w,), jnp.int32)] + _sc_buffers(chunk, w, table.dtype),
        compiler_params=pltpu.CompilerParams(use_tc_tiling_on_sc=True),
        name="sc_combine",
    )(table, idx)


def _sc_dispatch(table, codes, start_row, n_out, chunk):
    t_all, w = table.shape
    n_pad = codes.shape[0]
    n_ent = 2 * t_all
    ent_w = n_pad // SC_WORKERS
    n_chunks = ent_w // chunk
    per_chunk = chunk // SC_LANES
    trash = n_out - (n_pad - n_ent)
    assert ent_w * SC_WORKERS == n_pad and n_chunks * chunk == ent_w
    assert per_chunk * SC_LANES == chunk and chunk <= LANES and n_pad - n_ent <= t_all

    def body(table_hbm, code_hbm, start_hbm, out_hbm, dest_hbm,
             code_v, dest_v, tok_v, dst_v, start_v, buf0, buf1, g0, g1, w0, w1):
        wid = lax.axis_index("s") * SC_CORES + lax.axis_index("c")
        ebase = pl.multiple_of(wid * ent_w, SUBLANES)
        pltpu.sync_copy(code_hbm.at[pl.ds(ebase, ent_w)], code_v)
        pltpu.sync_copy(start_hbm, start_v)
        lane = lax.iota(jnp.int32, SC_LANES)
        for j in range(n_chunks):
            for c in range(per_chunk):
                off = j * chunk + c * SC_LANES
                ent = ebase + off + lane
                code = code_v[pl.ds(off, SC_LANES)]
                d = plsc.load_gather(start_v, [code >> CODE_BITS]) + (code & ((1 << CODE_BITS) - 1))
                d = jnp.where(ent >= n_ent, trash + (ent - n_ent), d)
                tok = jnp.where(ent >= t_all, ent - t_all, ent)
                tok = jnp.where(tok >= t_all, tok - t_all, tok)
                dest_v[pl.ds(off, SC_LANES)] = d
                dst_v[j, pl.ds(c * SC_LANES, SC_LANES)] = d
                tok_v[j, pl.ds(c * SC_LANES, SC_LANES)] = tok
        pltpu.sync_copy(dest_v, dest_hbm.at[pl.ds(ebase, ent_w)])
        bufs, gsems, wsems = (buf0, buf1), (g0, g1), (w0, w1)

        def gather(j):
            return pltpu.make_async_copy(table_hbm.at[tok_v.at[j]], bufs[j % 2], gsems[j % 2])

        def scatter(j):
            return pltpu.make_async_copy(bufs[j % 2], out_hbm.at[dst_v.at[j]], wsems[j % 2])

        _sc_stream(n_chunks, gather, scatter)

    return pl.kernel(
        body,
        out_type=(jax.ShapeDtypeStruct((n_out, w), table.dtype), jax.ShapeDtypeStruct((n_pad,), jnp.int32)),
        mesh=_sc_mesh(),
        scratch_types=([pltpu.VMEM((ent_w,), jnp.int32), pltpu.VMEM((ent_w,), jnp.int32),
                        pltpu.VMEM((n_chunks, chunk), jnp.int32), pltpu.VMEM((n_chunks, chunk), jnp.int32),
                        pltpu.VMEM((LANES,), jnp.int32)] + _sc_buffers(chunk, w, table.dtype)),
        compiler_params=pltpu.CompilerParams(use_tc_tiling_on_sc=True, needs_layout_passes=False),
        name="sc_dispatch",
    )(table, codes, start_row)


def _experts_kernel(piece_start_ref, piece_row_ref, piece_cls_ref, wg_ref, wu_ref, wd_ref, xs_hbm, ys_hbm,
                    wg_bf, wu_bf, wd_bf, xbuf, ybuf, xsem, ysem):
    e = pl.program_id(0)
    g0 = piece_start_ref[e]
    n_here = piece_start_ref[e + 1] - g0
    n_total = piece_start_ref[N_EXPERTS]

    def per_class(g, fn):
        cls = piece_cls_ref[g]
        row = pl.multiple_of(piece_row_ref[g], EXP_UNIT)
        for c in range(1, EXP_CLASSES + 1):
            pl.when(cls == c)(lambda c=c: fn(c * EXP_UNIT, row))

    def x_copy(slot, rows, row):
        return pltpu.make_async_copy(xs_hbm.at[pl.ds(row, rows)], xbuf.at[slot, pl.ds(0, rows)], xsem.at[slot])

    def y_copy(slot, rows, row):
        return pltpu.make_async_copy(ybuf.at[slot, pl.ds(0, rows)], ys_hbm.at[pl.ds(row, rows)], ysem.at[slot])

    @pl.when((e == 0) & (n_total > 0))
    def _():
        per_class(0, lambda rows, row: x_copy(0, rows, row).start())

    wg_bf[...] = wg_ref[0].astype(_BF16)
    wu_bf[...] = wu_ref[0].astype(_BF16)
    wd_bf[...] = wd_ref[0].astype(_BF16)

    def piece(j, carry):
        g = g0 + j
        slot = lax.rem(g, 2)
        per_class(g, lambda rows, row: x_copy(slot, rows, row).wait())

        @pl.when(g + 1 < n_total)
        def _():
            per_class(g + 1, lambda rows, row: x_copy(1 - slot, rows, row).start())

        @pl.when(g >= 2)
        def _():
            per_class(g - 2, lambda rows, row: y_copy(slot, rows, row).wait())

        def compute(rows, row):
            x = _unpack_bf16_pair(xbuf[slot, pl.ds(0, rows)]).astype(_BF16)
            a = _dot(x, wg_bf[...])
            u = _dot(x, wu_bf[...])
            h = (a * jax.nn.sigmoid(a) * u).astype(_BF16)
            ybuf[slot, pl.ds(0, rows)] = _pack_bf16_pair(_dot(h, wd_bf[...]))
            y_copy(slot, rows, row).start()

        per_class(g, compute)
        return carry

    lax.fori_loop(0, n_here, piece, 0)

    @pl.when(e == N_EXPERTS - 1)
    def _():
        @pl.when(n_total >= 2)
        def _():
            per_class(n_total - 2, lambda rows, row: y_copy(lax.rem(n_total, 2), rows, row).wait())

        @pl.when(n_total >= 1)
        def _():
            per_class(n_total - 1, lambda rows, row: y_copy(lax.rem(n_total - 1, 2), rows, row).wait())


def _experts(piece_start, piece_row, piece_cls, n_rows, xs, w_gate, w_up, w_down):
    dh = xs.shape[1]
    d = 2 * dh
    tm = EXP_UNIT * EXP_CLASSES
    wsel = lambda e, ps, pr, pc: (e, 0, 0)
    anyspec = pl.BlockSpec(memory_space=pl.ANY)
    grid_spec = pltpu.PrefetchScalarGridSpec(
        num_scalar_prefetch=3,
        grid=(N_EXPERTS,),
        in_specs=[pl.BlockSpec((1, d, D_EXPERT), wsel),
                  pl.BlockSpec((1, d, D_EXPERT), wsel),
                  pl.BlockSpec((1, D_EXPERT, d), wsel),
                  anyspec],
        out_specs=anyspec,
        scratch_shapes=[pltpu.VMEM((d, D_EXPERT), _BF16), pltpu.VMEM((d, D_EXPERT), _BF16),
                        pltpu.VMEM((D_EXPERT, d), _BF16),
                        pltpu.VMEM((2, tm, dh), _U32), pltpu.VMEM((2, tm, dh), _U32),
                        pltpu.SemaphoreType.DMA((2,)), pltpu.SemaphoreType.DMA((2,))],
    )
    return pl.pallas_call(
        _experts_kernel,
        grid_spec=grid_spec,
        out_shape=jax.ShapeDtypeStruct((n_rows, dh), _U32),
        compiler_params=pltpu.CompilerParams(
            dimension_semantics=("arbitrary",), vmem_limit_bytes=VMEM_LIMIT),
        name="experts",
    )(piece_start, piece_row, piece_cls, w_gate, w_up, w_down, xs)


def _final_kernel(x1_ref, ya_ref, yb_ref, route_ref, gf_ref, y_ref):
    route = route_ref[...]
    x2 = (x1_ref[...] + route[:, R_W1:R_W1 + 1] * _unpack_bf16_pair(ya_ref[...])
          + route[:, R_W2:R_W2 + 1] * _unpack_bf16_pair(yb_ref[...]))
    y_ref[...] = _rms(x2, gf_ref[...])


def _final(x1, yab, route, gf, n_prompt, n_sample):
    d = x1.shape[1]

    def call(tm, first_block, n_rows, name):
        tok = lambda w: pl.BlockSpec((tm, w), lambda i: (first_block + i, 0))
        sel = lambda k: pl.BlockSpec((None, tm, d // 2), lambda i: (k, first_block + i, 0))
        return pl.pallas_call(
            _final_kernel,
            grid=(n_rows // tm,),
            in_specs=[tok(d), sel(0), sel(1), tok(LANES), pl.BlockSpec((1, d), lambda i: (0, 0))],
            out_specs=pl.BlockSpec((tm, d), lambda i: (i, 0)),
            out_shape=jax.ShapeDtypeStruct((n_rows, d), _F32),
            compiler_params=pltpu.CompilerParams(
                dimension_semantics=("arbitrary",), vmem_limit_bytes=VMEM_LIMIT),
            name=name,
        )(x1, yab, yab, route, gf)

    return (call(FINAL_TM, 0, n_prompt, "final_prompt"),
            call(n_sample, n_prompt // n_sample, n_sample, "final_sample"))


def _ssm_params(lam_re, lam_im, log_dt, b_re, b_im, c_re, c_im, d_skip):
    dt = jnp.exp(log_dt)[:, None]
    mag = jnp.exp(lam_re * dt)
    ang = lam_im * dt
    lb_re = mag * jnp.cos(ang)
    lb_im = mag * jnp.sin(ang)
    den = lam_re * lam_re + lam_im * lam_im
    nr = lb_re - 1.0
    ni = lb_im
    k_re = (nr * lam_re + ni * lam_im) / den
    k_im = (ni * lam_re - nr * lam_im) / den
    bb_re = k_re[:, :, None] * b_re - k_im[:, :, None] * b_im
    bb_im = k_re[:, :, None] * b_im + k_im[:, :, None] * b_re
    eye = jnp.eye(SUBLANES, dtype=_F32)

    def b_blocks(bb):
        t = jnp.transpose(bb, (0, 2, 1)).reshape(N_LANE_TILES, 8, SSM_GROUP, SSM_STATE)
        return jnp.einsum('kahp,ab->kahbp', t, eye).reshape(N_LANE_TILES, LANES, TILE_STATE)

    def c_blocks(c):
        t = c.reshape(N_LANE_TILES, 8, SSM_GROUP, SSM_STATE)
        return jnp.einsum('kahp,ab->kapbh', t, eye).reshape(N_LANE_TILES, TILE_STATE, LANES)

    wb = jnp.concatenate([b_blocks(bb_re), b_blocks(bb_im)], axis=-1)
    wc = jnp.concatenate([c_blocks(c_re), -c_blocks(c_im)], axis=1)
    return (wb, wc, lb_re.reshape(1, STATE_COLS), lb_im.reshape(1, STATE_COLS),
            d_skip.reshape(1, D_SSM))


def _dispatch_plan(route_t, cnt):
    t_all = route_t.shape[1]
    codes = route_t[R_CODE1:R_CODE2 + 1].astype(jnp.int32).reshape(-1)
    per_pass = SC_WORKERS * DISPATCH_CHUNK
    codes = jnp.pad(codes, (0, -(2 * t_all) % per_pass))
    counts = cnt[0, :N_EXPERTS].astype(jnp.int32)
    zero = jnp.zeros((1,), jnp.int32)
    units = (counts + EXP_UNIT - 1) // EXP_UNIT
    unit_start = jnp.concatenate([zero, jnp.cumsum(units)])
    start_row = jnp.zeros((LANES,), jnp.int32).at[:N_EXPERTS].set(unit_start[:N_EXPERTS] * EXP_UNIT)
    pieces = (units + EXP_CLASSES - 1) // EXP_CLASSES
    piece_start = jnp.concatenate([zero, jnp.cumsum(pieces)])
    tm = EXP_UNIT * EXP_CLASSES
    max_units = (2 * t_all + N_EXPERTS * (EXP_UNIT - 1)) // EXP_UNIT
    max_pieces = (max_units + N_EXPERTS * (EXP_CLASSES - 1)) // EXP_CLASSES
    g = jnp.arange(max_pieces, dtype=jnp.int32)
    owner = ((g[:, None] >= piece_start[None, :-1]) & (g[:, None] < piece_start[None, 1:])).astype(jnp.int32)
    pick = lambda table: jnp.sum(owner * table[None, :], axis=1)
    first_unit = pick(unit_start[:-1]) + (g - pick(piece_start[:-1])) * EXP_CLASSES
    piece_row = first_unit * EXP_UNIT
    piece_cls = jnp.clip(pick(unit_start[1:]) - first_unit, 1, EXP_CLASSES)
    n_rows = (max_units * EXP_UNIT + tm - 1) // tm * tm + tm
    return codes, start_row, n_rows, piece_start, piece_row, piece_cls


def kernel(x_prompt, x_sample, state_ssm_re, state_ssm_im, norm1_g, w_in, lam_re, lam_im, log_dt, ssm_b_re, ssm_b_im, ssm_c_re, ssm_c_im, ssm_d, gmlp_norm_g, gmlp_w_s, gmlp_b_s, out_norm_ssm_g, out_norm_gmlp_g, w_out, norm2_g, w_router_group, b_router_group, w_router_expert, b_router_expert, w_gate, w_up, w_down, final_norm_g):
    n, l, d = x_prompt.shape
    ns = x_sample.shape[0]
    t_all = n * l + ns
    li = 0
    g1 = norm1_g[li].reshape(1, d)
    gn = gmlp_norm_g[li].reshape(1, D_GMLP)
    tril = jnp.tril(jnp.ones((CHUNK, CHUNK), dtype=bool))
    ws_tril = jnp.where(tril[None], gmlp_w_s[li], 0.0)
    bs = gmlp_b_s[li]
    gog = out_norm_gmlp_g[li].reshape(1, D_GMLP)
    gos = out_norm_ssm_g[li].reshape(1, D_SSM)
    wb, wc, lbr, lbi, dsk = _ssm_params(lam_re[li], lam_im[li], log_dt[li], ssm_b_re[li], ssm_b_im[li],
                                        ssm_c_re[li], ssm_c_im[li], ssm_d[li])
    g2 = norm2_g[li].reshape(1, d)
    pad = LANES - N_EXPERTS - N_EXPERT_GROUPS
    wr = jnp.concatenate([w_router_expert[li], w_router_group[li], jnp.zeros((d, pad), _F32)], axis=1)
    br = jnp.concatenate([b_router_expert[li], b_router_group[li], jnp.zeros((pad,), _F32)]).reshape(1, LANES)

    xa, sg, mixb = _front_prompt(x_prompt, g1, w_in[li].astype(_BF16), gn, ws_tril.astype(_BF16), bs.T, gog)
    mixa, hfin = _ssm_prompt(xa, sg, wb.astype(_BF16), wc.astype(_BF16), lbr, lbi, dsk, gos)
    w00 = jnp.repeat(ws_tril[:, 0, 0], GMLP_HEAD).reshape(1, D_GMLP)
    b0 = jnp.repeat(bs[:, 0], GMLP_HEAD).reshape(1, D_GMLP)
    mix_s, hr_s, hi_s, vrow = _front_sample(
        x_sample.reshape(ns, d), g1, w_in[li], gn, w00, b0, gog, wb, wc, lbr, lbi, dsk, gos,
        state_ssm_re[li].reshape(ns, STATE_COLS), state_ssm_im[li].reshape(ns, STATE_COLS))

    x1, xn, route, route_t, cnt = _mixer_out(x_prompt, mixa, mixb, x_sample.reshape(ns, d), mix_s,
                                             w_out[li], g2, wr, br)
    codes, start_row, n_rows, piece_start, piece_row, piece_cls = _dispatch_plan(route_t, cnt)
    xs, dest = _sc_dispatch(xn, codes, start_row, n_rows, DISPATCH_CHUNK)
    ys = _experts(piece_start, piece_row, piece_cls, n_rows, xs, w_gate[li], w_up[li], w_down[li])
    yab = _sc_combine(ys, dest, 2 * t_all, COMBINE_CHUNK).reshape(2, t_all, d // 2)
    y_p, y_s = _final(x1, yab, route, final_norm_g.reshape(1, d), n * l, ns)

    hf = hfin.reshape(n, N_LANE_TILES, 2, 8, SSM_STATE)
    re_p = hf[:, :, 0].reshape(1, n, N_SSM_GROUPS, SSM_STATE)
    im_p = hf[:, :, 1].reshape(1, n, N_SSM_GROUPS, SSM_STATE)
    re_s = hr_s.reshape(1, ns, N_SSM_GROUPS, SSM_STATE)
    im_s = hi_s.reshape(1, ns, N_SSM_GROUPS, SSM_STATE)
    return (y_p.reshape(n, l, d), y_s.reshape(ns, 1, d), re_p, im_p, re_s, im_s,
            vrow.reshape(1, ns, 1, D_GMLP))
```

```python
import math

import jax
import jax.numpy as jnp
from jax import lax
from jax.experimental import pallas as pl
from jax.experimental.pallas import tpu as pltpu
from jax.experimental.pallas import tpu_sc as plsc

D_MODEL = 1024
D_SSM = 512
D_GMLP = 512
SSM_GROUP = 16
N_SSM_GROUPS = 32
SSM_STATE = 64
CHUNK = 128
N_GMLP_HEADS = 4
GMLP_HEAD = 128
N_EXPERT_GROUPS = 4
EXPERTS_PER_GROUP = 8
N_EXPERTS = 32
D_EXPERT = 512
D_IN = 2048
EPS = 1e-6

LANES = 128
SUBLANES = 8
N_LANE_TILES = D_SSM // LANES
STATE_COLS = N_SSM_GROUPS * SSM_STATE
TILE_STATE = STATE_COLS // N_LANE_TILES
VMEM_LIMIT = 56 * 1024 * 1024

SC_CORES = 2
SC_SUBCORES = 16
SC_LANES = 16
SC_WORKERS = SC_CORES * SC_SUBCORES

FRONT_TL = 512
SSM_LC = 128
TOK_TM = 512
FINAL_TM = 512
EXP_UNIT = 128
EXP_CLASSES = 4
W_SPLIT = 4
DISPATCH_CHUNK = 80
COMBINE_CHUNK = 24

R_E1, R_E2, R_W1, R_W2, R_RANK1, R_RANK2, R_CODE1, R_CODE2 = 0, 1, 2, 3, 4, 5, 6, 7
CODE_BITS = 16
CODE_SHIFT = float(1 << CODE_BITS)

_INV_SQRT2 = 1.0 / math.sqrt(2.0)
_BF16 = jnp.bfloat16
_F32 = jnp.float32
_U32 = jnp.uint32


def _gelu(x):
    return 0.5 * x * (1.0 + lax.erf(x * _INV_SQRT2))


def _rms(x, g):
    return x * lax.rsqrt(jnp.mean(x * x, axis=-1, keepdims=True) + EPS) * g


def _dot(a, b):
    return jnp.dot(a, b, preferred_element_type=_F32)


def _dot_f32(a, b):
    return jnp.dot(a, b, preferred_element_type=_F32, precision=lax.Precision.HIGHEST)


def _pack_bf16_pair(x):
    w = x.shape[1] // 2
    hi = lax.bitcast_convert_type(x[:, :w].astype(_BF16).astype(_F32), _U32)
    lo = lax.bitcast_convert_type(x[:, w:].astype(_BF16).astype(_F32), _U32)
    return hi | (lo >> 16)


def _unpack_bf16_pair(p):
    hi = lax.bitcast_convert_type(p & jnp.uint32(0xFFFF0000), _F32)
    lo = lax.bitcast_convert_type(p << 16, _F32)
    return jnp.concatenate([hi, lo], axis=-1)


def _head_norm_gelu(vb, gn):
    v = _gelu(vb)
    parts = []
    for h in range(N_GMLP_HEADS):
        vh = v[:, h * GMLP_HEAD:(h + 1) * GMLP_HEAD]
        parts.append(vh * lax.rsqrt(jnp.mean(vh * vh, axis=-1, keepdims=True) + EPS))
    return jnp.concatenate(parts, axis=-1) * gn


def _front_prompt_kernel(x_ref, g1_ref, win_ref, gn_ref, ws_ref, bs_ref, gog_ref,
                         xa_ref, sg_ref, mixb_ref, win_bf):
    @pl.when((pl.program_id(0) == 0) & (pl.program_id(1) == 0))
    def _():
        win_bf[...] = win_ref[...].astype(_BF16)

    x = x_ref[0]
    hn = _rms(x, g1_ref[...]).astype(_BF16)
    z = _dot(hn, win_bf[...])
    xa_ref[0] = z[:, :D_SSM]
    sg_ref[0] = jax.nn.sigmoid(z[:, D_SSM:2 * D_SSM])
    ub = _gelu(z[:, 2 * D_SSM:2 * D_SSM + D_GMLP])
    vbn = _head_norm_gelu(z[:, 2 * D_SSM + D_GMLP:], gn_ref[...]).astype(_BF16)
    tl = x.shape[0]
    rows = []
    for c in range(tl // CHUNK):
        heads = []
        for h in range(N_GMLP_HEADS):
            vh = vbn[c * CHUNK:(c + 1) * CHUNK, h * GMLP_HEAD:(h + 1) * GMLP_HEAD]
            heads.append(_dot(ws_ref[h], vh) + bs_ref[:, h:h + 1])
        rows.append(jnp.concatenate(heads, axis=-1))
    s = jnp.concatenate(rows, axis=0)
    mixb_ref[0] = _rms(ub * s, gog_ref[...]).astype(_BF16)


def _front_prompt(x, g1, win, gn, ws_tril_bf, bs_t, gog):
    n, l, d = x.shape
    tl = FRONT_TL
    grid = (n, l // tl)
    const = lambda *shape: pl.BlockSpec(shape, lambda b, i: (0,) * len(shape))
    seq = lambda w: pl.BlockSpec((1, tl, w), lambda b, i: (b, i, 0))
    return pl.pallas_call(
        _front_prompt_kernel,
        grid=grid,
        in_specs=[seq(d), const(1, d), const(d, D_IN), const(1, D_GMLP),
                  const(N_GMLP_HEADS, CHUNK, CHUNK), const(CHUNK, N_GMLP_HEADS), const(1, D_GMLP)],
        out_specs=[seq(D_SSM), seq(D_SSM), seq(D_GMLP)],
        out_shape=[jax.ShapeDtypeStruct((n, l, D_SSM), _F32),
                   jax.ShapeDtypeStruct((n, l, D_SSM), _F32),
                   jax.ShapeDtypeStruct((n, l, D_GMLP), _BF16)],
        scratch_shapes=[pltpu.VMEM((d, D_IN), _BF16)],
        compiler_params=pltpu.CompilerParams(
            dimension_semantics=("arbitrary", "arbitrary"), vmem_limit_bytes=VMEM_LIMIT),
        name="front_prompt",
    )(x, g1, win, gn, ws_tril_bf, bs_t, gog)


def _ssm_prompt_kernel(xa_ref, sg_ref, wb_ref, wc_ref, lbr_ref, lbi_ref, dsk_ref, gos_ref,
                       mixa_ref, hfin_ref, bu_ref, st_ref):
    lc = xa_ref.shape[1]
    rows = lc * SUBLANES
    pair = 2 * SUBLANES

    @pl.when(pl.program_id(0) == 0)
    def _():
        st_ref[...] = jnp.zeros_like(st_ref)

    xa = pltpu.einshape("btc->tbc", xa_ref[...]).reshape(rows, D_SSM)
    xa_bf = xa.astype(_BF16)
    for k in range(N_LANE_TILES):
        bu_ref[:, 2 * TILE_STATE * k:2 * TILE_STATE * (k + 1)] = _dot(
            xa_bf[:, k * LANES:(k + 1) * LANES], wb_ref[k])

    for kk in range(0, N_LANE_TILES, 2):
        tiles = (kk, kk + 1)
        cols = [(2 * TILE_STATE * k, 2 * TILE_STATE * k + TILE_STATE) for k in tiles]
        lbs = [(jnp.broadcast_to(lbr_ref[:, k * TILE_STATE:(k + 1) * TILE_STATE], (SUBLANES, TILE_STATE)),
                jnp.broadcast_to(lbi_ref[:, k * TILE_STATE:(k + 1) * TILE_STATE], (SUBLANES, TILE_STATE)))
               for k in tiles]

        def body(j, carry, cols=cols, lbs=lbs):
            r0 = pl.multiple_of(j * pair, pair)
            r1 = pl.multiple_of(r0 + SUBLANES, SUBLANES)
            out = []
            for q, ((c_re, c_im), (lr, li)) in enumerate(zip(cols, lbs)):
                hr, hi = carry[2 * q], carry[2 * q + 1]
                ar = lr * hr - li * hi + bu_ref[pl.ds(r0, SUBLANES), c_re:c_re + TILE_STATE]
                ai = lr * hi + li * hr + bu_ref[pl.ds(r0, SUBLANES), c_im:c_im + TILE_STATE]
                br = lr * ar - li * ai + bu_ref[pl.ds(r1, SUBLANES), c_re:c_re + TILE_STATE]
                bi = lr * ai + li * ar + bu_ref[pl.ds(r1, SUBLANES), c_im:c_im + TILE_STATE]
                bu_ref[pl.ds(r0, SUBLANES), c_re:c_re + TILE_STATE] = ar
                bu_ref[pl.ds(r0, SUBLANES), c_im:c_im + TILE_STATE] = ai
                bu_ref[pl.ds(r1, SUBLANES), c_re:c_re + TILE_STATE] = br
                bu_ref[pl.ds(r1, SUBLANES), c_im:c_im + TILE_STATE] = bi
                out += [br, bi]
            return tuple(out)

        init = tuple(st_ref[:, c:c + TILE_STATE] for c_pair in cols for c in c_pair)
        fin = lax.fori_loop(0, lc // 2, body, init, unroll=2)
        for q, (c_re, c_im) in enumerate(cols):
            st_ref[:, c_re:c_re + TILE_STATE] = fin[2 * q]
            st_ref[:, c_im:c_im + TILE_STATE] = fin[2 * q + 1]

    ys = []
    for k in range(N_LANE_TILES):
        hk = bu_ref[:, 2 * TILE_STATE * k:2 * TILE_STATE * (k + 1)].astype(_BF16)
        ys.append(_dot(hk, wc_ref[k]))
    y = jnp.concatenate(ys, axis=-1) + dsk_ref[...] * xa
    sg = pltpu.einshape("btc->tbc", sg_ref[...]).reshape(rows, D_SSM)
    mixa = _rms(_gelu(y) * sg, gos_ref[...]).reshape(lc, SUBLANES, D_SSM)
    mixa_ref[...] = pltpu.einshape("tbc->btc", mixa).astype(_BF16)
    hfin_ref[...] = st_ref[...]


def _ssm_prompt(xa, sg, wb, wc, lbr, lbi, dsk, gos):
    n, l, _ = xa.shape
    lc = SSM_LC
    const = lambda *shape: pl.BlockSpec(shape, lambda i: (0,) * len(shape))
    seq_spec = pl.BlockSpec((n, lc, D_SSM), lambda i: (0, i, 0))
    return pl.pallas_call(
        _ssm_prompt_kernel,
        grid=(l // lc,),
        in_specs=[seq_spec, seq_spec,
                  const(N_LANE_TILES, LANES, 2 * TILE_STATE), const(N_LANE_TILES, 2 * TILE_STATE, LANES),
                  const(1, STATE_COLS), const(1, STATE_COLS), const(1, D_SSM), const(1, D_SSM)],
        out_specs=[seq_spec, const(n, 2 * STATE_COLS)],
        out_shape=[jax.ShapeDtypeStruct((n, l, D_SSM), _BF16),
                   jax.ShapeDtypeStruct((n, 2 * STATE_COLS), _F32)],
        scratch_shapes=[pltpu.VMEM((lc * n, 2 * STATE_COLS), _F32),
                        pltpu.VMEM((n, 2 * STATE_COLS), _F32)],
        compiler_params=pltpu.CompilerParams(
            dimension_semantics=("arbitrary",), vmem_limit_bytes=VMEM_LIMIT),
        name="ssm_prompt",
    )(xa, sg, wb, wc, lbr, lbi, dsk, gos)


def _front_sample_kernel(x_ref, g1_ref, win_ref, gn_ref, w00_ref, b0_ref, gog_ref,
                         wb_ref, wc_ref, lbr_ref, lbi_ref, dsk_ref, gos_ref, h0r_ref, h0i_ref,
                         mix_ref, hr_ref, hi_ref, vrow_ref):
    x = x_ref[...]
    hn = _rms(x, g1_ref[...])
    z = _dot_f32(hn, win_ref[...])
    xa = z[:, :D_SSM]
    ys = []
    for k in range(N_LANE_TILES):
        bu = _dot_f32(xa[:, k * LANES:(k + 1) * LANES], wb_ref[k])
        sl = slice(k * TILE_STATE, (k + 1) * TILE_STATE)
        lr, li = lbr_ref[:, sl], lbi_ref[:, sl]
        h0r, h0i = h0r_ref[:, sl], h0i_ref[:, sl]
        nr = lr * h0r - li * h0i + bu[:, :TILE_STATE]
        ni = lr * h0i + li * h0r + bu[:, TILE_STATE:]
        hr_ref[:, sl] = nr
        hi_ref[:, sl] = ni
        ys.append(_dot_f32(jnp.concatenate([nr, ni], axis=-1), wc_ref[k]))
    y = jnp.concatenate(ys, axis=-1) + dsk_ref[...] * xa
    ya = _gelu(y) * jax.nn.sigmoid(z[:, D_SSM:2 * D_SSM])
    mix_ref[:, :D_SSM] = _rms(ya, gos_ref[...])
    ub = _gelu(z[:, 2 * D_SSM:2 * D_SSM + D_GMLP])
    vbn = _head_norm_gelu(z[:, 2 * D_SSM + D_GMLP:], gn_ref[...])
    vrow_ref[...] = vbn
    s = w00_ref[...] * vbn + b0_ref[...]
    mix_ref[:, D_SSM:] = _rms(ub * s, gog_ref[...])


def _front_sample(x, g1, win, gn, w00, b0, gog, wb, wc, lbr, lbi, dsk, gos, h0r, h0i):
    n = x.shape[0]
    vmem = pl.BlockSpec(memory_space=pltpu.VMEM)
    return pl.pallas_call(
        _front_sample_kernel,
        in_specs=[vmem] * 15,
        out_specs=[vmem] * 4,
        out_shape=[jax.ShapeDtypeStruct((n, D_MODEL), _F32),
                   jax.ShapeDtypeStruct((n, STATE_COLS), _F32),
                   jax.ShapeDtypeStruct((n, STATE_COLS), _F32),
                   jax.ShapeDtypeStruct((n, D_GMLP), _F32)],
        compiler_params=pltpu.CompilerParams(vmem_limit_bytes=VMEM_LIMIT),
        name="front_sample",
    )(x, g1, win, gn, w00, b0, gog, wb, wc, lbr, lbi, dsk, gos, h0r, h0i)


def _route(logits, base):
    tm = logits.shape[0]
    lane = lax.broadcasted_iota(jnp.int32, logits.shape, 1).astype(_F32)
    neg = jnp.float32(-jnp.inf)
    big = jnp.float32(LANES)
    is_g = (lane >= N_EXPERTS) & (lane < N_EXPERTS + N_EXPERT_GROUPS)
    gl = jnp.where(is_g, logits, neg)
    gmax = jnp.max(gl, axis=-1, keepdims=True)
    gi = jnp.min(jnp.where(is_g & (logits == gmax), lane, big), axis=-1, keepdims=True) - N_EXPERTS
    p_top = 1.0 / jnp.sum(jnp.where(is_g, jnp.exp(gl - gmax), 0.0), axis=-1, keepdims=True)
    lo = gi * EXPERTS_PER_GROUP
    in_grp = (lane >= lo) & (lane < lo + EXPERTS_PER_GROUP)
    m1 = jnp.max(jnp.where(in_grp, logits, neg), axis=-1, keepdims=True)
    i1 = jnp.min(jnp.where(in_grp & (logits == m1), lane, big), axis=-1, keepdims=True)
    rest = in_grp & (lane != i1)
    m2 = jnp.max(jnp.where(rest, logits, neg), axis=-1, keepdims=True)
    i2 = jnp.min(jnp.where(rest & (logits == m2), lane, big), axis=-1, keepdims=True)
    e2 = jnp.exp(m2 - m1)
    w1 = p_top / (1.0 + e2)
    w2 = p_top * e2 / (1.0 + e2)
    sel1 = lane == i1
    sel2 = lane == i2
    hits = jnp.where(sel1 | sel2, 1.0, 0.0)
    r_id = lax.broadcasted_iota(jnp.int32, (tm, tm), 0)
    c_id = lax.broadcasted_iota(jnp.int32, (tm, tm), 1)
    ltri = jnp.where(c_id < r_id, 1.0, 0.0).astype(_BF16)
    before = _dot(ltri, hits.astype(_BF16)) + base
    rank1 = jnp.sum(jnp.where(sel1, before, 0.0), axis=-1, keepdims=True)
    rank2 = jnp.sum(jnp.where(sel2, before, 0.0), axis=-1, keepdims=True)
    out = jnp.where(lane == R_E1, i1, 0.0)
    out = jnp.where(lane == R_E2, i2, out)
    out = jnp.where(lane == R_W1, w1, out)
    out = jnp.where(lane == R_W2, w2, out)
    out = jnp.where(lane == R_RANK1, rank1, out)
    out = jnp.where(lane == R_RANK2, rank2, out)
    out = jnp.where(lane == R_CODE1, i1 * CODE_SHIFT + rank1, out)
    out = jnp.where(lane == R_CODE2, i2 * CODE_SHIFT + rank2, out)
    return out, base + jnp.sum(hits, axis=0, keepdims=True)


def _mixer_out_prompt_kernel(x_ref, mixa_ref, mixb_ref, wo_ref, g2_ref, wr_ref, br_ref,
                             x1_ref, xn_ref, route_ref, route_t_ref, cnt_ref, base_ref):
    @pl.when((pl.program_id(0) == 0) & (pl.program_id(1) == 0))
    def _():
        base_ref[...] = jnp.zeros_like(base_ref)

    x1 = x_ref[0] + _dot(mixa_ref[0], wo_ref[:D_SSM, :]) + _dot(mixb_ref[0], wo_ref[D_SSM:, :])
    xn = _rms(x1, g2_ref[...])
    logits = _dot(xn.astype(_BF16), wr_ref[...]) + br_ref[...]
    route, base = _route(logits, base_ref[...])
    x1_ref[...] = x1
    xn_ref[...] = _pack_bf16_pair(xn)
    route_ref[...] = route
    route_t_ref[...] = route.T[:SUBLANES, :]
    base_ref[...] = base
    cnt_ref[...] = base


def _mixer_out_sample_kernel(x_ref, mix_ref, wo_ref, g2_ref, wr_ref, br_ref, cnt_in_ref,
                             x1_in, xn_in, route_in, route_t_in,
                             x1_ref, xn_ref, route_ref, route_t_ref, cnt_ref):
    del x1_in, xn_in, route_in, route_t_in
    x1 = (x_ref[...] + _dot_f32(mix_ref[:, :D_SSM], wo_ref[:D_SSM, :])
          + _dot_f32(mix_ref[:, D_SSM:], wo_ref[D_SSM:, :]))
    xn = _rms(x1, g2_ref[...])
    logits = _dot_f32(xn, wr_ref[...]) + br_ref[...]
    route, base = _route(logits, cnt_in_ref[...])
    x1_ref[...] = x1
    xn_ref[...] = _pack_bf16_pair(xn)
    route_ref[...] = route
    route_t_ref[...] = route.T[:SUBLANES, :]
    cnt_ref[...] = base


def _mixer_out(x_p, mixa, mixb, x_s, mix_s, wo, g2, wr, br):
    n, l, d = x_p.shape
    ns = x_s.shape[0]
    t_all = n * l + ns
    tm = TOK_TM
    per_seq = l // tm
    const = lambda *shape: pl.BlockSpec(shape, lambda b, i: (0,) * len(shape))
    seq = lambda w: pl.BlockSpec((1, tm, w), lambda b, i: (b, i, 0))
    tok = lambda w: pl.BlockSpec((tm, w), lambda b, i: (b * per_seq + i, 0))
    tok_shapes = [jax.ShapeDtypeStruct((t_all, d), _F32),
                  jax.ShapeDtypeStruct((t_all, d // 2), _U32),
                  jax.ShapeDtypeStruct((t_all, LANES), _F32),
                  jax.ShapeDtypeStruct((SUBLANES, t_all), _F32)]
    cnt_shape = jax.ShapeDtypeStruct((1, LANES), _F32)
    x1, xn, route, route_t, cnt = pl.pallas_call(
        _mixer_out_prompt_kernel,
        grid=(n, per_seq),
        in_specs=[seq(d), seq(D_SSM), seq(D_GMLP),
                  const(d, d), const(1, d), const(d, LANES), const(1, LANES)],
        out_specs=[tok(d), tok(d // 2), tok(LANES),
                   pl.BlockSpec((SUBLANES, tm), lambda b, i: (0, b * per_seq + i)), const(1, LANES)],
        out_shape=tok_shapes + [cnt_shape],
        scratch_shapes=[pltpu.VMEM((1, LANES), _F32)],
        compiler_params=pltpu.CompilerParams(
            dimension_semantics=("arbitrary", "arbitrary"), vmem_limit_bytes=VMEM_LIMIT),
        name="mixer_out_prompt",
    )(x_p, mixa, mixb, wo.astype(_BF16), g2, wr.astype(_BF16), br)
    tail = (n * l) // ns
    c1 = lambda *shape: pl.BlockSpec(shape, lambda i: (0,) * len(shape))
    anyspec = pl.BlockSpec(memory_space=pl.ANY)
    tail_spec = lambda w: pl.BlockSpec((ns, w), lambda i: (tail, 0))
    return pl.pallas_call(
        _mixer_out_sample_kernel,
        grid=(1,),
        in_specs=[c1(ns, d), c1(ns, d), c1(d, d), c1(1, d), c1(d, LANES), c1(1, LANES), c1(1, LANES),
                  anyspec, anyspec, anyspec, anyspec],
        out_specs=[tail_spec(d), tail_spec(d // 2), tail_spec(LANES),
                   pl.BlockSpec((SUBLANES, ns), lambda i: (0, tail)), c1(1, LANES)],
        out_shape=tok_shapes + [cnt_shape],
        input_output_aliases={7: 0, 8: 1, 9: 2, 10: 3},
        compiler_params=pltpu.CompilerParams(
            dimension_semantics=("arbitrary",), vmem_limit_bytes=VMEM_LIMIT),
        name="mixer_out_sample",
    )(x_s, mix_s, wo, g2, wr, br, cnt, x1, xn, route, route_t)


def _sc_stream(n_chunks, gather, write):
    gather(0).start()
    for j in range(n_chunks):
        if j + 1 < n_chunks:
            if j >= 1:
                write(j - 1).wait()
            gather(j + 1).start()
        gather(j).wait()
        write(j).start()
    if n_chunks >= 2:
        write(n_chunks - 2).wait()
    write(n_chunks - 1).wait()


def _sc_mesh():
    return plsc.VectorSubcoreMesh(core_axis_name="c", subcore_axis_name="s",
                                  num_cores=SC_CORES, num_subcores=SC_SUBCORES)


def _sc_buffers(chunk, w, dtype):
    return [pltpu.VMEM((chunk, w), dtype), pltpu.VMEM((chunk, w), dtype)] + [pltpu.SemaphoreType.DMA] * 4


def _sc_combine(table, idx, n_out, chunk):
    w = table.shape[1]
    rows_w = n_out // SC_WORKERS
    n_chunks = rows_w // chunk
    assert rows_w * SC_WORKERS == n_out and n_chunks * chunk == rows_w and rows_w % SUBLANES == 0

    def body(table_hbm, idx_hbm, out_hbm, idx_v, buf0, buf1, g0, g1, w0, w1):
        wid = lax.axis_index("s") * SC_CORES + lax.axis_index("c")
        base = pl.multiple_of(wid * rows_w, SUBLANES)
        pltpu.sync_copy(idx_hbm.at[pl.ds(base, rows_w)], idx_v)
        bufs, gsems, wsems = (buf0, buf1), (g0, g1), (w0, w1)

        def gather(j):
            return pltpu.make_async_copy(table_hbm.at[idx_v.at[pl.ds(j * chunk, chunk)]], bufs[j % 2], gsems[j % 2])

        def write(j):
            return pltpu.make_async_copy(bufs[j % 2], out_hbm.at[pl.ds(base + j * chunk, chunk)], wsems[j % 2])

        _sc_stream(n_chunks, gather, write)

    return pl.kernel(
        body,
        out_type=jax.ShapeDtypeStruct((n_out, w), table.dtype),
        mesh=_sc_mesh(),
        scratch_types=[pltpu.VMEM((rows_w,), jnp.int32)] + _sc_buffers(chunk, w, table.dtype),
        compiler_params=pltpu.CompilerParams(use_tc_tiling_on_sc=True),
        name="sc_combine",
    )(table, idx)


def _sc_dispatch(table, codes, start_row, n_out, chunk):
    t_all, w = table.shape
    n_pad = codes.shape[0]
    n_ent = 2 * t_all
    ent_w = n_pad // SC_WORKERS
    n_chunks = ent_w // chunk
    per_chunk = chunk // SC_LANES
    trash = n_out - (n_pad - n_ent)
    assert ent_w * SC_WORKERS == n_pad and n_chunks * chunk == ent_w
    assert per_chunk * SC_LANES == chunk and chunk <= LANES and n_pad - n_ent <= t_all

    def body(table_hbm, code_hbm, start_hbm, out_hbm, dest_hbm,
             code_v, dest_v, tok_v, dst_v, start_v, buf0, buf1, g0, g1, w0, w1):
        wid = lax.axis_index("s") * SC_CORES + lax.axis_index("c")
        ebase = pl.multiple_of(wid * ent_w, SUBLANES)
        pltpu.sync_copy(code_hbm.at[pl.ds(ebase, ent_w)], code_v)
        pltpu.sync_copy(start_hbm, start_v)
        lane = lax.iota(jnp.int32, SC_LANES)
        for j in range(n_chunks):
            for c in range(per_chunk):
                off = j * chunk + c * SC_LANES
                ent = ebase + off + lane
                code = code_v[pl.ds(off, SC_LANES)]
                d = plsc.load_gather(start_v, [code >> CODE_BITS]) + (code & ((1 << CODE_BITS) - 1))
                d = jnp.where(ent >= n_ent, trash + (ent - n_ent), d)
                tok = jnp.where(ent >= t_all, ent - t_all, ent)
                tok = jnp.where(tok >= t_all, tok - t_all, tok)
                dest_v[pl.ds(off, SC_LANES)] = d
                dst_v[j, pl.ds(c * SC_LANES, SC_LANES)] = d
                tok_v[j, pl.ds(c * SC_LANES, SC_LANES)] = tok
        pltpu.sync_copy(dest_v, dest_hbm.at[pl.ds(ebase, ent_w)])
        bufs, gsems, wsems = (buf0, buf1), (g0, g1), (w0, w1)

        def gather(j):
            return pltpu.make_async_copy(table_hbm.at[tok_v.at[j]], bufs[j % 2], gsems[j % 2])

        def scatter(j):
            return pltpu.make_async_copy(bufs[j % 2], out_hbm.at[dst_v.at[j]], wsems[j % 2])

        _sc_stream(n_chunks, gather, scatter)

    return pl.kernel(
        body,
        out_type=(jax.ShapeDtypeStruct((n_out, w), table.dtype), jax.ShapeDtypeStruct((n_pad,), jnp.int32)),
        mesh=_sc_mesh(),
        scratch_types=([pltpu.VMEM((ent_w,), jnp.int32), pltpu.VMEM((ent_w,), jnp.int32),
                        pltpu.VMEM((n_chunks, chunk), jnp.int32), pltpu.VMEM((n_chunks, chunk), jnp.int32),
                        pltpu.VMEM((LANES,), jnp.int32)] + _sc_buffers(chunk, w, table.dtype)),
        compiler_params=pltpu.CompilerParams(use_tc_tiling_on_sc=True, needs_layout_passes=False),
        name="sc_dispatch",
    )(table, codes, start_row)


def _experts_kernel(piece_start_ref, piece_row_ref, piece_cls_ref, *refs):
    n_w = 3 * W_SPLIT
    wg_refs, wu_refs, wd_refs = refs[:W_SPLIT], refs[W_SPLIT:2 * W_SPLIT], refs[2 * W_SPLIT:n_w]
    xs_hbm, ys_hbm, wg_bf, wu_bf, wd_bf, xbuf, ybuf, xsem, ysem = refs[n_w:]
    e = pl.program_id(0)
    g0 = piece_start_ref[e]
    n_here = piece_start_ref[e + 1] - g0
    n_total = piece_start_ref[N_EXPERTS]

    def per_class(g, fn):
        cls = piece_cls_ref[g]
        row = pl.multiple_of(piece_row_ref[g], EXP_UNIT)
        for c in range(1, EXP_CLASSES + 1):
            pl.when(cls == c)(lambda c=c: fn(c * EXP_UNIT, row))

    def x_copy(slot, rows, row):
        return pltpu.make_async_copy(xs_hbm.at[pl.ds(row, rows)], xbuf.at[slot, pl.ds(0, rows)], xsem.at[slot])

    def y_copy(slot, rows, row):
        return pltpu.make_async_copy(ybuf.at[slot, pl.ds(0, rows)], ys_hbm.at[pl.ds(row, rows)], ysem.at[slot])

    @pl.when((e == 0) & (n_total > 0))
    def _():
        per_class(0, lambda rows, row: x_copy(0, rows, row).start())

    for dst, chunks in ((wg_bf, wg_refs), (wu_bf, wu_refs), (wd_bf, wd_refs)):
        rows = dst.shape[0] // W_SPLIT
        for q, src in enumerate(chunks):
            dst[q * rows:(q + 1) * rows, :] = src[0, 0].astype(_BF16)

    def piece(j, carry):
        g = g0 + j
        slot = lax.rem(g, 2)
        per_class(g, lambda rows, row: x_copy(slot, rows, row).wait())

        @pl.when(g + 1 < n_total)
        def _():
            per_class(g + 1, lambda rows, row: x_copy(1 - slot, rows, row).start())

        @pl.when(g >= 2)
        def _():
            per_class(g - 2, lambda rows, row: y_copy(slot, rows, row).wait())

        def compute(rows, row):
            x = _unpack_bf16_pair(xbuf[slot, pl.ds(0, rows)]).astype(_BF16)
            a = _dot(x, wg_bf[...])
            u = _dot(x, wu_bf[...])
            h = (a * jax.nn.sigmoid(a) * u).astype(_BF16)
            ybuf[slot, pl.ds(0, rows)] = _pack_bf16_pair(_dot(h, wd_bf[...]))
            y_copy(slot, rows, row).start()

        per_class(g, compute)
        return carry

    lax.fori_loop(0, n_here, piece, 0)

    @pl.when(e == N_EXPERTS - 1)
    def _():
        @pl.when(n_total >= 2)
        def _():
            per_class(n_total - 2, lambda rows, row: y_copy(lax.rem(n_total, 2), rows, row).wait())

        @pl.when(n_total >= 1)
        def _():
            per_class(n_total - 1, lambda rows, row: y_copy(lax.rem(n_total - 1, 2), rows, row).wait())


def _experts(piece_start, piece_row, piece_cls, n_rows, xs, w_gate, w_up, w_down):
    dh = xs.shape[1]
    d = 2 * dh
    tm = EXP_UNIT * EXP_CLASSES
    anyspec = pl.BlockSpec(memory_space=pl.ANY)

    def chunk_specs(rows, cols):
        return [pl.BlockSpec((1, 1, rows // W_SPLIT, cols), lambda e, ps, pr, pc, q=q: (e, q, 0, 0))
                for q in range(W_SPLIT)]

    split = lambda w: w.reshape(w.shape[0], W_SPLIT, w.shape[1] // W_SPLIT, w.shape[2])
    grid_spec = pltpu.PrefetchScalarGridSpec(
        num_scalar_prefetch=3,
        grid=(N_EXPERTS,),
        in_specs=(chunk_specs(d, D_EXPERT) + chunk_specs(d, D_EXPERT) + chunk_specs(D_EXPERT, d) + [anyspec]),
        out_specs=anyspec,
        scratch_shapes=[pltpu.VMEM((d, D_EXPERT), _BF16), pltpu.VMEM((d, D_EXPERT), _BF16),
                        pltpu.VMEM((D_EXPERT, d), _BF16),
                        pltpu.VMEM((2, tm, dh), _U32), pltpu.VMEM((2, tm, dh), _U32),
                        pltpu.SemaphoreType.DMA((2,)), pltpu.SemaphoreType.DMA((2,))],
    )
    return pl.pallas_call(
        _experts_kernel,
        grid_spec=grid_spec,
        out_shape=jax.ShapeDtypeStruct((n_rows, dh), _U32),
        compiler_params=pltpu.CompilerParams(
            dimension_semantics=("arbitrary",), vmem_limit_bytes=VMEM_LIMIT),
        name="experts",
    )(piece_start, piece_row, piece_cls, *([split(w_gate)] * W_SPLIT), *([split(w_up)] * W_SPLIT),
      *([split(w_down)] * W_SPLIT), xs)


def _final_kernel(x1_ref, ya_ref, yb_ref, route_ref, gf_ref, y_ref):
    route = route_ref[...]
    x2 = (x1_ref[...] + route[:, R_W1:R_W1 + 1] * _unpack_bf16_pair(ya_ref[...])
          + route[:, R_W2:R_W2 + 1] * _unpack_bf16_pair(yb_ref[...]))
    y_ref[...] = _rms(x2, gf_ref[...])


def _final(x1, yab, route, gf, n_prompt, n_sample):
    d = x1.shape[1]

    def call(tm, first_block, n_rows, name):
        tok = lambda w: pl.BlockSpec((tm, w), lambda i: (first_block + i, 0))
        sel = lambda k: pl.BlockSpec((None, tm, d // 2), lambda i: (k, first_block + i, 0))
        return pl.pallas_call(
            _final_kernel,
            grid=(n_rows // tm,),
            in_specs=[tok(d), sel(0), sel(1), tok(LANES), pl.BlockSpec((1, d), lambda i: (0, 0))],
            out_specs=pl.BlockSpec((tm, d), lambda i: (i, 0)),
            out_shape=jax.ShapeDtypeStruct((n_rows, d), _F32),
            compiler_params=pltpu.CompilerParams(
                dimension_semantics=("arbitrary",), vmem_limit_bytes=VMEM_LIMIT),
            name=name,
        )(x1, yab, yab, route, gf)

    return (call(FINAL_TM, 0, n_prompt, "final_prompt"),
            call(n_sample, n_prompt // n_sample, n_sample, "final_sample"))


def _ssm_params(lam_re, lam_im, log_dt, b_re, b_im, c_re, c_im, d_skip):
    dt = jnp.exp(log_dt)[:, None]
    mag = jnp.exp(lam_re * dt)
    ang = lam_im * dt
    lb_re = mag * jnp.cos(ang)
    lb_im = mag * jnp.sin(ang)
    den = lam_re * lam_re + lam_im * lam_im
    nr = lb_re - 1.0
    ni = lb_im
    k_re = (nr * lam_re + ni * lam_im) / den
    k_im = (ni * lam_re - nr * lam_im) / den
    bb_re = k_re[:, :, None] * b_re - k_im[:, :, None] * b_im
    bb_im = k_re[:, :, None] * b_im + k_im[:, :, None] * b_re
    eye = jnp.eye(SUBLANES, dtype=_F32)

    def b_blocks(bb):
        t = jnp.transpose(bb, (0, 2, 1)).reshape(N_LANE_TILES, 8, SSM_GROUP, SSM_STATE)
        return jnp.einsum('kahp,ab->kahbp', t, eye).reshape(N_LANE_TILES, LANES, TILE_STATE)

    def c_blocks(c):
        t = c.reshape(N_LANE_TILES, 8, SSM_GROUP, SSM_STATE)
        return jnp.einsum('kahp,ab->kapbh', t, eye).reshape(N_LANE_TILES, TILE_STATE, LANES)

    wb = jnp.concatenate([b_blocks(bb_re), b_blocks(bb_im)], axis=-1)
    wc = jnp.concatenate([c_blocks(c_re), -c_blocks(c_im)], axis=1)
    return (wb, wc, lb_re.reshape(1, STATE_COLS), lb_im.reshape(1, STATE_COLS),
            d_skip.reshape(1, D_SSM))


def _dispatch_plan(route_t, cnt):
    t_all = route_t.shape[1]
    codes = route_t[R_CODE1:R_CODE2 + 1].astype(jnp.int32).reshape(-1)
    per_pass = SC_WORKERS * DISPATCH_CHUNK
    codes = jnp.pad(codes, (0, -(2 * t_all) % per_pass))
    counts = cnt[0, :N_EXPERTS].astype(jnp.int32)
    zero = jnp.zeros((1,), jnp.int32)
    units = (counts + EXP_UNIT - 1) // EXP_UNIT
    unit_start = jnp.concatenate([zero, jnp.cumsum(units)])
    start_row = jnp.zeros((LANES,), jnp.int32).at[:N_EXPERTS].set(unit_start[:N_EXPERTS] * EXP_UNIT)
    pieces = (units + EXP_CLASSES - 1) // EXP_CLASSES
    piece_start = jnp.concatenate([zero, jnp.cumsum(pieces)])
    tm = EXP_UNIT * EXP_CLASSES
    max_units = (2 * t_all + N_EXPERTS * (EXP_UNIT - 1)) // EXP_UNIT
    max_pieces = (max_units + N_EXPERTS * (EXP_CLASSES - 1)) // EXP_CLASSES
    g = jnp.arange(max_pieces, dtype=jnp.int32)
    owner = ((g[:, None] >= piece_start[None, :-1]) & (g[:, None] < piece_start[None, 1:])).astype(jnp.int32)
    pick = lambda table: jnp.sum(owner * table[None, :], axis=1)
    first_unit = pick(unit_start[:-1]) + (g - pick(piece_start[:-1])) * EXP_CLASSES
    piece_row = first_unit * EXP_UNIT
    piece_cls = jnp.clip(pick(unit_start[1:]) - first_unit, 1, EXP_CLASSES)
    n_rows = (max_units * EXP_UNIT + tm - 1) // tm * tm + tm
    return codes, start_row, n_rows, piece_start, piece_row, piece_cls


def kernel(x_prompt, x_sample, state_ssm_re, state_ssm_im, norm1_g, w_in, lam_re, lam_im, log_dt, ssm_b_re, ssm_b_im, ssm_c_re, ssm_c_im, ssm_d, gmlp_norm_g, gmlp_w_s, gmlp_b_s, out_norm_ssm_g, out_norm_gmlp_g, w_out, norm2_g, w_router_group, b_router_group, w_router_expert, b_router_expert, w_gate, w_up, w_down, final_norm_g):
    n, l, d = x_prompt.shape
    ns = x_sample.shape[0]
    t_all = n * l + ns
    li = 0
    g1 = norm1_g[li].reshape(1, d)
    gn = gmlp_norm_g[li].reshape(1, D_GMLP)
    tril = jnp.tril(jnp.ones((CHUNK, CHUNK), dtype=bool))
    ws_tril = jnp.where(tril[None], gmlp_w_s[li], 0.0)
    bs = gmlp_b_s[li]
    gog = out_norm_gmlp_g[li].reshape(1, D_GMLP)
    gos = out_norm_ssm_g[li].reshape(1, D_SSM)
    wb, wc, lbr, lbi, dsk = _ssm_params(lam_re[li], lam_im[li], log_dt[li], ssm_b_re[li], ssm_b_im[li],
                                        ssm_c_re[li], ssm_c_im[li], ssm_d[li])
    g2 = norm2_g[li].reshape(1, d)
    pad = LANES - N_EXPERTS - N_EXPERT_GROUPS
    wr = jnp.concatenate([w_router_expert[li], w_router_group[li], jnp.zeros((d, pad), _F32)], axis=1)
    br = jnp.concatenate([b_router_expert[li], b_router_group[li], jnp.zeros((pad,), _F32)]).reshape(1, LANES)

    xa, sg, mixb = _front_prompt(x_prompt, g1, w_in[li], gn, ws_tril.astype(_BF16), bs.T, gog)
    mixa, hfin = _ssm_prompt(xa, sg, wb.astype(_BF16), wc.astype(_BF16), lbr, lbi, dsk, gos)
    w00 = jnp.repeat(ws_tril[:, 0, 0], GMLP_HEAD).reshape(1, D_GMLP)
    b0 = jnp.repeat(bs[:, 0], GMLP_HEAD).reshape(1, D_GMLP)
    mix_s, hr_s, hi_s, vrow = _front_sample(
        x_sample.reshape(ns, d), g1, w_in[li], gn, w00, b0, gog, wb, wc, lbr, lbi, dsk, gos,
        state_ssm_re[li].reshape(ns, STATE_COLS), state_ssm_im[li].reshape(ns, STATE_COLS))

    x1, xn, route, route_t, cnt = _mixer_out(x_prompt, mixa, mixb, x_sample.reshape(ns, d), mix_s,
                                             w_out[li], g2, wr, br)
    codes, start_row, n_rows, piece_start, piece_row, piece_cls = _dispatch_plan(route_t, cnt)
    xs, dest = _sc_dispatch(xn, codes, start_row, n_rows, DISPATCH_CHUNK)
    ys = _experts(piece_start, piece_row, piece_cls, n_rows, xs, w_gate[li], w_up[li], w_down[li])
    yab = _sc_combine(ys, dest, 2 * t_all, COMBINE_CHUNK).reshape(2, t_all, d // 2)
    y_p, y_s = _final(x1, yab, route, final_norm_g.reshape(1, d), n * l, ns)

    hf = hfin.reshape(n, N_LANE_TILES, 2, 8, SSM_STATE)
    re_p = hf[:, :, 0].reshape(1, n, N_SSM_GROUPS, SSM_STATE)
    im_p = hf[:, :, 1].reshape(1, n, N_SSM_GROUPS, SSM_STATE)
    re_s = hr_s.reshape(1, ns, N_SSM_GROUPS, SSM_STATE)
    im_s = hi_s.reshape(1, ns, N_SSM_GROUPS, SSM_STATE)
    return (y_p.reshape(n, l, d), y_s.reshape(ns, 1, d), re_p, im_p, re_s, im_s,
            vrow.reshape(1, ns, 1, D_GMLP))
```

```python
import math

import jax
import jax.numpy as jnp
from jax import lax
from jax.experimental import pallas as pl
from jax.experimental.pallas import tpu as pltpu
from jax.experimental.pallas import tpu_sc as plsc

D_MODEL = 1024
D_SSM = 512
D_GMLP = 512
SSM_GROUP = 16
N_SSM_GROUPS = 32
SSM_STATE = 64
CHUNK = 128
N_GMLP_HEADS = 4
GMLP_HEAD = 128
N_EXPERT_GROUPS = 4
EXPERTS_PER_GROUP = 8
N_EXPERTS = 32
D_EXPERT = 512
D_IN = 2048
EPS = 1e-6

LANES = 128
SUBLANES = 8
N_LANE_TILES = D_SSM // LANES
STATE_COLS = N_SSM_GROUPS * SSM_STATE
TILE_STATE = STATE_COLS // N_LANE_TILES
VMEM_LIMIT = 56 * 1024 * 1024

SC_CORES = 2
SC_SUBCORES = 16
SC_LANES = 16
SC_WORKERS = SC_CORES * SC_SUBCORES

FRONT_TL = 512
SSM_LC = 256
SSM_BLK = 4
TOK_TM = 512
FINAL_TM = 512
EXP_UNIT = 128
EXP_CLASSES = 4
W_SPLIT = 4
DISPATCH_CHUNK = 80
COMBINE_CHUNK = 24

R_E1, R_E2, R_W1, R_W2, R_RANK1, R_RANK2, R_CODE1, R_CODE2 = 0, 1, 2, 3, 4, 5, 6, 7
CODE_BITS = 16
CODE_SHIFT = float(1 << CODE_BITS)

_INV_SQRT2 = 1.0 / math.sqrt(2.0)
_BF16 = jnp.bfloat16
_F32 = jnp.float32
_U32 = jnp.uint32


def _gelu(x):
    return 0.5 * x * (1.0 + lax.erf(x * _INV_SQRT2))


def _rms(x, g):
    return x * lax.rsqrt(jnp.mean(x * x, axis=-1, keepdims=True) + EPS) * g


def _dot(a, b):
    return jnp.dot(a, b, preferred_element_type=_F32)


def _dot_f32(a, b):
    return jnp.dot(a, b, preferred_element_type=_F32, precision=lax.Precision.HIGHEST)


def _pack_bf16_pair(x):
    w = x.shape[1] // 2
    hi = lax.bitcast_convert_type(x[:, :w].astype(_BF16).astype(_F32), _U32)
    lo = lax.bitcast_convert_type(x[:, w:].astype(_BF16).astype(_F32), _U32)
    return hi | (lo >> 16)


def _unpack_bf16_pair(p):
    hi = lax.bitcast_convert_type(p & jnp.uint32(0xFFFF0000), _F32)
    lo = lax.bitcast_convert_type(p << 16, _F32)
    return jnp.concatenate([hi, lo], axis=-1)


def _head_norm_gelu(vb, gn):
    v = _gelu(vb)
    parts = []
    for h in range(N_GMLP_HEADS):
        vh = v[:, h * GMLP_HEAD:(h + 1) * GMLP_HEAD]
        parts.append(vh * lax.rsqrt(jnp.mean(vh * vh, axis=-1, keepdims=True) + EPS))
    return jnp.concatenate(parts, axis=-1) * gn


def _front_prompt_kernel(x_ref, g1_ref, win_ref, gn_ref, ws_ref, bs_ref, gog_ref,
                         xa_ref, sg_ref, mixb_ref, win_bf):
    @pl.when((pl.program_id(0) == 0) & (pl.program_id(1) == 0))
    def _():
        win_bf[...] = win_ref[...].astype(_BF16)

    x = x_ref[0]
    hn = _rms(x, g1_ref[...]).astype(_BF16)
    z = _dot(hn, win_bf[...])
    xa_ref[0] = z[:, :D_SSM]
    sg_ref[0] = jax.nn.sigmoid(z[:, D_SSM:2 * D_SSM])
    ub = _gelu(z[:, 2 * D_SSM:2 * D_SSM + D_GMLP])
    vbn = _head_norm_gelu(z[:, 2 * D_SSM + D_GMLP:], gn_ref[...]).astype(_BF16)
    tl = x.shape[0]
    rows = []
    for c in range(tl // CHUNK):
        heads = []
        for h in range(N_GMLP_HEADS):
            vh = vbn[c * CHUNK:(c + 1) * CHUNK, h * GMLP_HEAD:(h + 1) * GMLP_HEAD]
            heads.append(_dot(ws_ref[h], vh) + bs_ref[:, h:h + 1])
        rows.append(jnp.concatenate(heads, axis=-1))
    s = jnp.concatenate(rows, axis=0)
    mixb_ref[0] = _rms(ub * s, gog_ref[...]).astype(_BF16)


def _front_prompt(x, g1, win, gn, ws_tril_bf, bs_t, gog):
    n, l, d = x.shape
    tl = FRONT_TL
    grid = (n, l // tl)
    const = lambda *shape: pl.BlockSpec(shape, lambda b, i: (0,) * len(shape))
    seq = lambda w: pl.BlockSpec((1, tl, w), lambda b, i: (b, i, 0))
    return pl.pallas_call(
        _front_prompt_kernel,
        grid=grid,
        in_specs=[seq(d), const(1, d), const(d, D_IN), const(1, D_GMLP),
                  const(N_GMLP_HEADS, CHUNK, CHUNK), const(CHUNK, N_GMLP_HEADS), const(1, D_GMLP)],
        out_specs=[seq(D_SSM), seq(D_SSM), seq(D_GMLP)],
        out_shape=[jax.ShapeDtypeStruct((n, l, D_SSM), _F32),
                   jax.ShapeDtypeStruct((n, l, D_SSM), _F32),
                   jax.ShapeDtypeStruct((n, l, D_GMLP), _BF16)],
        scratch_shapes=[pltpu.VMEM((d, D_IN), _BF16)],
        compiler_params=pltpu.CompilerParams(
            dimension_semantics=("arbitrary", "arbitrary"), vmem_limit_bytes=VMEM_LIMIT),
        name="front_prompt",
    )(x, g1, win, gn, ws_tril_bf, bs_t, gog)


def _ssm_prompt_kernel(xa_ref, sg_ref, v_ref, r_ref, l4r_ref, l4i_ref, dsk_ref, gos_ref,
                       mixa_ref, hfin_ref, s_ref, st_ref):
    lc = xa_ref.shape[1]
    nblk = lc // SSM_BLK
    rows = nblk * SUBLANES

    @pl.when(pl.program_id(0) == 0)
    def _():
        st_ref[...] = jnp.zeros_like(st_ref)

    def by_position(ref):
        t = pltpu.einshape("btc->tbc", ref[...]).reshape(nblk, SSM_BLK, SUBLANES, D_SSM)
        return [t[:, i].reshape(rows, D_SSM) for i in range(SSM_BLK)]

    xs = by_position(xa_ref)
    xs_bf = [x.astype(_BF16) for x in xs]
    xk = [jnp.concatenate([x[:, k * LANES:(k + 1) * LANES] for x in xs_bf], axis=-1)
          for k in range(N_LANE_TILES)]
    for k in range(N_LANE_TILES):
        s_ref[:, 2 * TILE_STATE * k:2 * TILE_STATE * (k + 1)] = _dot(xk[k], v_ref[k])

    for kk in range(0, N_LANE_TILES, 2):
        tiles = (kk, kk + 1)
        cols = [(2 * TILE_STATE * k, 2 * TILE_STATE * k + TILE_STATE) for k in tiles]
        lbs = [(jnp.broadcast_to(l4r_ref[:, k * TILE_STATE:(k + 1) * TILE_STATE], (SUBLANES, TILE_STATE)),
                jnp.broadcast_to(l4i_ref[:, k * TILE_STATE:(k + 1) * TILE_STATE], (SUBLANES, TILE_STATE)))
               for k in tiles]

        def body(j, carry, cols=cols, lbs=lbs):
            r0 = pl.multiple_of(j * SUBLANES, SUBLANES)
            out = []
            for q, ((c_re, c_im), (lr, li)) in enumerate(zip(cols, lbs)):
                hr, hi = carry[2 * q], carry[2 * q + 1]
                sr = s_ref[pl.ds(r0, SUBLANES), c_re:c_re + TILE_STATE]
                si = s_ref[pl.ds(r0, SUBLANES), c_im:c_im + TILE_STATE]
                s_ref[pl.ds(r0, SUBLANES), c_re:c_re + TILE_STATE] = hr
                s_ref[pl.ds(r0, SUBLANES), c_im:c_im + TILE_STATE] = hi
                out += [lr * hr - li * hi + sr, lr * hi + li * hr + si]
            return tuple(out)

        init = tuple(st_ref[:, c:c + TILE_STATE] for c_pair in cols for c in c_pair)
        fin = lax.fori_loop(0, nblk, body, init, unroll=2)
        for q, (c_re, c_im) in enumerate(cols):
            st_ref[:, c_re:c_re + TILE_STATE] = fin[2 * q]
            st_ref[:, c_im:c_im + TILE_STATE] = fin[2 * q + 1]

    yk = []
    for k in range(N_LANE_TILES):
        h_in = s_ref[:, 2 * TILE_STATE * k:2 * TILE_STATE * (k + 1)].astype(_BF16)
        yk.append(_dot(jnp.concatenate([h_in, xk[k]], axis=-1), r_ref[k]))
    sgs = by_position(sg_ref)
    outs = []
    for i in range(SSM_BLK):
        y = jnp.concatenate([y_k[:, i * LANES:(i + 1) * LANES] for y_k in yk], axis=-1) + dsk_ref[...] * xs[i]
        outs.append(_rms(_gelu(y) * sgs[i], gos_ref[...]).reshape(nblk, SUBLANES, D_SSM))
    mixa = jnp.stack(outs, axis=1).reshape(lc, SUBLANES, D_SSM)
    mixa_ref[...] = pltpu.einshape("tbc->btc", mixa).astype(_BF16)
    hfin_ref[...] = st_ref[...]


def _ssm_prompt(xa, sg, v, r, l4r, l4i, dsk, gos):
    n, l, _ = xa.shape
    lc = SSM_LC
    const = lambda *shape: pl.BlockSpec(shape, lambda i: (0,) * len(shape))
    seq_spec = pl.BlockSpec((n, lc, D_SSM), lambda i: (0, i, 0))
    return pl.pallas_call(
        _ssm_prompt_kernel,
        grid=(l // lc,),
        in_specs=[seq_spec, seq_spec, const(*v.shape), const(*r.shape),
                  const(1, STATE_COLS), const(1, STATE_COLS), const(1, D_SSM), const(1, D_SSM)],
        out_specs=[seq_spec, const(n, 2 * STATE_COLS)],
        out_shape=[jax.ShapeDtypeStruct((n, l, D_SSM), _BF16),
                   jax.ShapeDtypeStruct((n, 2 * STATE_COLS), _F32)],
        scratch_shapes=[pltpu.VMEM((lc // SSM_BLK * n, 2 * STATE_COLS), _F32),
                        pltpu.VMEM((n, 2 * STATE_COLS), _F32)],
        compiler_params=pltpu.CompilerParams(
            dimension_semantics=("arbitrary",), vmem_limit_bytes=VMEM_LIMIT),
        name="ssm_prompt",
    )(xa, sg, v, r, l4r, l4i, dsk, gos)


def _front_sample_kernel(x_ref, g1_ref, win_ref, gn_ref, w00_ref, b0_ref, gog_ref,
                         wb_ref, wc_ref, lbr_ref, lbi_ref, dsk_ref, gos_ref, h0r_ref, h0i_ref,
                         mix_ref, hr_ref, hi_ref, vrow_ref):
    x = x_ref[...]
    hn = _rms(x, g1_ref[...])
    z = _dot_f32(hn, win_ref[...])
    xa = z[:, :D_SSM]
    ys = []
    for k in range(N_LANE_TILES):
        bu = _dot_f32(xa[:, k * LANES:(k + 1) * LANES], wb_ref[k])
        sl = slice(k * TILE_STATE, (k + 1) * TILE_STATE)
        lr, li = lbr_ref[:, sl], lbi_ref[:, sl]
        h0r, h0i = h0r_ref[:, sl], h0i_ref[:, sl]
        nr = lr * h0r - li * h0i + bu[:, :TILE_STATE]
        ni = lr * h0i + li * h0r + bu[:, TILE_STATE:]
        hr_ref[:, sl] = nr
        hi_ref[:, sl] = ni
        ys.append(_dot_f32(jnp.concatenate([nr, ni], axis=-1), wc_ref[k]))
    y = jnp.concatenate(ys, axis=-1) + dsk_ref[...] * xa
    ya = _gelu(y) * jax.nn.sigmoid(z[:, D_SSM:2 * D_SSM])
    mix_ref[:, :D_SSM] = _rms(ya, gos_ref[...])
    ub = _gelu(z[:, 2 * D_SSM:2 * D_SSM + D_GMLP])
    vbn = _head_norm_gelu(z[:, 2 * D_SSM + D_GMLP:], gn_ref[...])
    vrow_ref[...] = vbn
    s = w00_ref[...] * vbn + b0_ref[...]
    mix_ref[:, D_SSM:] = _rms(ub * s, gog_ref[...])


def _front_sample(x, g1, win, gn, w00, b0, gog, wb, wc, lbr, lbi, dsk, gos, h0r, h0i):
    n = x.shape[0]
    vmem = pl.BlockSpec(memory_space=pltpu.VMEM)
    return pl.pallas_call(
        _front_sample_kernel,
        in_specs=[vmem] * 15,
        out_specs=[vmem] * 4,
        out_shape=[jax.ShapeDtypeStruct((n, D_MODEL), _F32),
                   jax.ShapeDtypeStruct((n, STATE_COLS), _F32),
                   jax.ShapeDtypeStruct((n, STATE_COLS), _F32),
                   jax.ShapeDtypeStruct((n, D_GMLP), _F32)],
        compiler_params=pltpu.CompilerParams(vmem_limit_bytes=VMEM_LIMIT),
        name="front_sample",
    )(x, g1, win, gn, w00, b0, gog, wb, wc, lbr, lbi, dsk, gos, h0r, h0i)


def _route(logits, base):
    tm = logits.shape[0]
    lane = lax.broadcasted_iota(jnp.int32, logits.shape, 1).astype(_F32)
    neg = jnp.float32(-jnp.inf)
    big = jnp.float32(LANES)
    is_g = (lane >= N_EXPERTS) & (lane < N_EXPERTS + N_EXPERT_GROUPS)
    gl = jnp.where(is_g, logits, neg)
    gmax = jnp.max(gl, axis=-1, keepdims=True)
    gi = jnp.min(jnp.where(is_g & (logits == gmax), lane, big), axis=-1, keepdims=True) - N_EXPERTS
    p_top = 1.0 / jnp.sum(jnp.where(is_g, jnp.exp(gl - gmax), 0.0), axis=-1, keepdims=True)
    lo = gi * EXPERTS_PER_GROUP
    in_grp = (lane >= lo) & (lane < lo + EXPERTS_PER_GROUP)
    m1 = jnp.max(jnp.where(in_grp, logits, neg), axis=-1, keepdims=True)
    i1 = jnp.min(jnp.where(in_grp & (logits == m1), lane, big), axis=-1, keepdims=True)
    rest = in_grp & (lane != i1)
    m2 = jnp.max(jnp.where(rest, logits, neg), axis=-1, keepdims=True)
    i2 = jnp.min(jnp.where(rest & (logits == m2), lane, big), axis=-1, keepdims=True)
    e2 = jnp.exp(m2 - m1)
    w1 = p_top / (1.0 + e2)
    w2 = p_top * e2 / (1.0 + e2)
    sel1 = lane == i1
    sel2 = lane == i2
    hits = jnp.where(sel1 | sel2, 1.0, 0.0)
    r_id = lax.broadcasted_iota(jnp.int32, (tm, tm), 0)
    c_id = lax.broadcasted_iota(jnp.int32, (tm, tm), 1)
    ltri = jnp.where(c_id < r_id, 1.0, 0.0).astype(_BF16)
    before = _dot(ltri, hits.astype(_BF16)) + base
    rank1 = jnp.sum(jnp.where(sel1, before, 0.0), axis=-1, keepdims=True)
    rank2 = jnp.sum(jnp.where(sel2, before, 0.0), axis=-1, keepdims=True)
    out = jnp.where(lane == R_E1, i1, 0.0)
    out = jnp.where(lane == R_E2, i2, out)
    out = jnp.where(lane == R_W1, w1, out)
    out = jnp.where(lane == R_W2, w2, out)
    out = jnp.where(lane == R_RANK1, rank1, out)
    out = jnp.where(lane == R_RANK2, rank2, out)
    out = jnp.where(lane == R_CODE1, i1 * CODE_SHIFT + rank1, out)
    out = jnp.where(lane == R_CODE2, i2 * CODE_SHIFT + rank2, out)
    return out, base + jnp.sum(hits, axis=0, keepdims=True)


def _mixer_out_prompt_kernel(x_ref, mixa_ref, mixb_ref, wo_ref, g2_ref, wr_ref, br_ref,
                             x1_ref, xn_ref, route_ref, route_t_ref, cnt_ref, base_ref):
    @pl.when((pl.program_id(0) == 0) & (pl.program_id(1) == 0))
    def _():
        base_ref[...] = jnp.zeros_like(base_ref)

    x1 = x_ref[0] + _dot(mixa_ref[0], wo_ref[:D_SSM, :]) + _dot(mixb_ref[0], wo_ref[D_SSM:, :])
    xn = _rms(x1, g2_ref[...])
    logits = _dot(xn.astype(_BF16), wr_ref[...]) + br_ref[...]
    route, base = _route(logits, base_ref[...])
    x1_ref[...] = x1
    xn_ref[...] = _pack_bf16_pair(xn)
    route_ref[...] = route
    route_t_ref[...] = route.T[:SUBLANES, :]
    base_ref[...] = base
    cnt_ref[...] = base


def _mixer_out_sample_kernel(x_ref, mix_ref, wo_ref, g2_ref, wr_ref, br_ref, cnt_in_ref,
                             x1_in, xn_in, route_in, route_t_in,
                             x1_ref, xn_ref, route_ref, route_t_ref, cnt_ref):
    del x1_in, xn_in, route_in, route_t_in
    x1 = (x_ref[...] + _dot_f32(mix_ref[:, :D_SSM], wo_ref[:D_SSM, :])
          + _dot_f32(mix_ref[:, D_SSM:], wo_ref[D_SSM:, :]))
    xn = _rms(x1, g2_ref[...])
    logits = _dot_f32(xn, wr_ref[...]) + br_ref[...]
    route, base = _route(logits, cnt_in_ref[...])
    x1_ref[...] = x1
    xn_ref[...] = _pack_bf16_pair(xn)
    route_ref[...] = route
    route_t_ref[...] = route.T[:SUBLANES, :]
    cnt_ref[...] = base


def _mixer_out(x_p, mixa, mixb, x_s, mix_s, wo, g2, wr, br):
    n, l, d = x_p.shape
    ns = x_s.shape[0]
    t_all = n * l + ns
    tm = TOK_TM
    per_seq = l // tm
    const = lambda *shape: pl.BlockSpec(shape, lambda b, i: (0,) * len(shape))
    seq = lambda w: pl.BlockSpec((1, tm, w), lambda b, i: (b, i, 0))
    tok = lambda w: pl.BlockSpec((tm, w), lambda b, i: (b * per_seq + i, 0))
    tok_shapes = [jax.ShapeDtypeStruct((t_all, d), _F32),
                  jax.ShapeDtypeStruct((t_all, d // 2), _U32),
                  jax.ShapeDtypeStruct((t_all, LANES), _F32),
                  jax.ShapeDtypeStruct((SUBLANES, t_all), _F32)]
    cnt_shape = jax.ShapeDtypeStruct((1, LANES), _F32)
    x1, xn, route, route_t, cnt = pl.pallas_call(
        _mixer_out_prompt_kernel,
        grid=(n, per_seq),
        in_specs=[seq(d), seq(D_SSM), seq(D_GMLP),
                  const(d, d), const(1, d), const(d, LANES), const(1, LANES)],
        out_specs=[tok(d), tok(d // 2), tok(LANES),
                   pl.BlockSpec((SUBLANES, tm), lambda b, i: (0, b * per_seq + i)), const(1, LANES)],
        out_shape=tok_shapes + [cnt_shape],
        scratch_shapes=[pltpu.VMEM((1, LANES), _F32)],
        compiler_params=pltpu.CompilerParams(
            dimension_semantics=("arbitrary", "arbitrary"), vmem_limit_bytes=VMEM_LIMIT),
        name="mixer_out_prompt",
    )(x_p, mixa, mixb, wo.astype(_BF16), g2, wr.astype(_BF16), br)
    tail = (n * l) // ns
    c1 = lambda *shape: pl.BlockSpec(shape, lambda i: (0,) * len(shape))
    anyspec = pl.BlockSpec(memory_space=pl.ANY)
    tail_spec = lambda w: pl.BlockSpec((ns, w), lambda i: (tail, 0))
    return pl.pallas_call(
        _mixer_out_sample_kernel,
        grid=(1,),
        in_specs=[c1(ns, d), c1(ns, d), c1(d, d), c1(1, d), c1(d, LANES), c1(1, LANES), c1(1, LANES),
                  anyspec, anyspec, anyspec, anyspec],
        out_specs=[tail_spec(d), tail_spec(d // 2), tail_spec(LANES),
                   pl.BlockSpec((SUBLANES, ns), lambda i: (0, tail)), c1(1, LANES)],
        out_shape=tok_shapes + [cnt_shape],
        input_output_aliases={7: 0, 8: 1, 9: 2, 10: 3},
        compiler_params=pltpu.CompilerParams(
            dimension_semantics=("arbitrary",), vmem_limit_bytes=VMEM_LIMIT),
        name="mixer_out_sample",
    )(x_s, mix_s, wo, g2, wr, br, cnt, x1, xn, route, route_t)


def _sc_stream(n_chunks, gather, write):
    gather(0).start()
    for j in range(n_chunks):
        if j + 1 < n_chunks:
            if j >= 1:
                write(j - 1).wait()
            gather(j + 1).start()
        gather(j).wait()
        write(j).start()
    if n_chunks >= 2:
        write(n_chunks - 2).wait()
    write(n_chunks - 1).wait()


def _sc_mesh():
    return plsc.VectorSubcoreMesh(core_axis_name="c", subcore_axis_name="s",
                                  num_cores=SC_CORES, num_subcores=SC_SUBCORES)


def _sc_buffers(chunk, w, dtype):
    return [pltpu.VMEM((chunk, w), dtype), pltpu.VMEM((chunk, w), dtype)] + [pltpu.SemaphoreType.DMA] * 4


def _sc_combine(table, idx, n_out, chunk):
    w = table.shape[1]
    rows_w = n_out // SC_WORKERS
    n_chunks = rows_w // chunk
    assert rows_w * SC_WORKERS == n_out and n_chunks * chunk == rows_w and rows_w % SUBLANES == 0

    def body(table_hbm, idx_hbm, out_hbm, idx_v, buf0, buf1, g0, g1, w0, w1):
        wid = lax.axis_index("s") * SC_CORES + lax.axis_index("c")
        base = pl.multiple_of(wid * rows_w, SUBLANES)
        pltpu.sync_copy(idx_hbm.at[pl.ds(base, rows_w)], idx_v)
        bufs, gsems, wsems = (buf0, buf1), (g0, g1), (w0, w1)

        def gather(j):
            return pltpu.make_async_copy(table_hbm.at[idx_v.at[pl.ds(j * chunk, chunk)]], bufs[j % 2], gsems[j % 2])

        def write(j):
            return pltpu.make_async_copy(bufs[j % 2], out_hbm.at[pl.ds(base + j * chunk, chunk)], wsems[j % 2])

        _sc_stream(n_chunks, gather, write)

    return pl.kernel(
        body,
        out_type=jax.ShapeDtypeStruct((n_out, w), table.dtype),
        mesh=_sc_mesh(),
        scratch_types=[pltpu.VMEM((rows_w,), jnp.int32)] + _sc_buffers(chunk, w, table.dtype),
        compiler_params=pltpu.CompilerParams(use_tc_tiling_on_sc=True),
        name="sc_combine",
    )(table, idx)


def _sc_dispatch(table, codes, start_row, n_out, chunk):
    t_all, w = table.shape
    n_pad = codes.shape[0]
    n_ent = 2 * t_all
    ent_w = n_pad // SC_WORKERS
    n_chunks = ent_w // chunk
    per_chunk = chunk // SC_LANES
    trash = n_out - (n_pad - n_ent)
    assert ent_w * SC_WORKERS == n_pad and n_chunks * chunk == ent_w
    assert per_chunk * SC_LANES == chunk and chunk <= LANES and n_pad - n_ent <= t_all

    def body(table_hbm, code_hbm, start_hbm, out_hbm, dest_hbm,
             code_v, dest_v, tok_v, dst_v, start_v, buf0, buf1, g0, g1, w0, w1):
        wid = lax.axis_index("s") * SC_CORES + lax.axis_index("c")
        ebase = pl.multiple_of(wid * ent_w, SUBLANES)
        pltpu.sync_copy(code_hbm.at[pl.ds(ebase, ent_w)], code_v)
        pltpu.sync_copy(start_hbm, start_v)
        lane = lax.iota(jnp.int32, SC_LANES)
        for j in range(n_chunks):
            for c in range(per_chunk):
                off = j * chunk + c * SC_LANES
                ent = ebase + off + lane
                code = code_v[pl.ds(off, SC_LANES)]
                d = plsc.load_gather(start_v, [code >> CODE_BITS]) + (code & ((1 << CODE_BITS) - 1))
                d = jnp.where(ent >= n_ent, trash + (ent - n_ent), d)
                tok = jnp.where(ent >= t_all, ent - t_all, ent)
                tok = jnp.where(tok >= t_all, tok - t_all, tok)
                dest_v[pl.ds(off, SC_LANES)] = d
                dst_v[j, pl.ds(c * SC_LANES, SC_LANES)] = d
                tok_v[j, pl.ds(c * SC_LANES, SC_LANES)] = tok
        pltpu.sync_copy(dest_v, dest_hbm.at[pl.ds(ebase, ent_w)])
        bufs, gsems, wsems = (buf0, buf1), (g0, g1), (w0, w1)

        def gather(j):
            return pltpu.make_async_copy(table_hbm.at[tok_v.at[j]], bufs[j % 2], gsems[j % 2])

        def scatter(j):
            return pltpu.make_async_copy(bufs[j % 2], out_hbm.at[dst_v.at[j]], wsems[j % 2])

        _sc_stream(n_chunks, gather, scatter)

    return pl.kernel(
        body,
        out_type=(jax.ShapeDtypeStruct((n_out, w), table.dtype), jax.ShapeDtypeStruct((n_pad,), jnp.int32)),
        mesh=_sc_mesh(),
        scratch_types=([pltpu.VMEM((ent_w,), jnp.int32), pltpu.VMEM((ent_w,), jnp.int32),
                        pltpu.VMEM((n_chunks, chunk), jnp.int32), pltpu.VMEM((n_chunks, chunk), jnp.int32),
                        pltpu.VMEM((LANES,), jnp.int32)] + _sc_buffers(chunk, w, table.dtype)),
        compiler_params=pltpu.CompilerParams(use_tc_tiling_on_sc=True, needs_layout_passes=False),
        name="sc_dispatch",
    )(table, codes, start_row)


def _experts_kernel(piece_start_ref, piece_row_ref, piece_cls_ref, *refs):
    n_w = 3 * W_SPLIT
    wg_refs, wu_refs, wd_refs = refs[:W_SPLIT], refs[W_SPLIT:2 * W_SPLIT], refs[2 * W_SPLIT:n_w]
    xs_hbm, ys_hbm, wg_bf, wu_bf, wd_bf, xbuf, ybuf, xsem, ysem = refs[n_w:]
    e = pl.program_id(0)
    g0 = piece_start_ref[e]
    n_here = piece_start_ref[e + 1] - g0
    n_total = piece_start_ref[N_EXPERTS]

    def per_class(g, fn):
        cls = piece_cls_ref[g]
        row = pl.multiple_of(piece_row_ref[g], EXP_UNIT)
        for c in range(1, EXP_CLASSES + 1):
            pl.when(cls == c)(lambda c=c: fn(c * EXP_UNIT, row))

    def x_copy(slot, rows, row):
        return pltpu.make_async_copy(xs_hbm.at[pl.ds(row, rows)], xbuf.at[slot, pl.ds(0, rows)], xsem.at[slot])

    def y_copy(slot, rows, row):
        return pltpu.make_async_copy(ybuf.at[slot, pl.ds(0, rows)], ys_hbm.at[pl.ds(row, rows)], ysem.at[slot])

    @pl.when((e == 0) & (n_total > 0))
    def _():
        per_class(0, lambda rows, row: x_copy(0, rows, row).start())

    for dst, chunks in ((wg_bf, wg_refs), (wu_bf, wu_refs), (wd_bf, wd_refs)):
        rows = dst.shape[0] // W_SPLIT
        for q, src in enumerate(chunks):
            dst[q * rows:(q + 1) * rows, :] = src[0, 0].astype(_BF16)

    def piece(j, carry):
        g = g0 + j
        slot = lax.rem(g, 2)
        per_class(g, lambda rows, row: x_copy(slot, rows, row).wait())

        @pl.when(g + 1 < n_total)
        def _():
            per_class(g + 1, lambda rows, row: x_copy(1 - slot, rows, row).start())

        @pl.when(g >= 2)
        def _():
            per_class(g - 2, lambda rows, row: y_copy(slot, rows, row).wait())

        def compute(rows, row):
            x = _unpack_bf16_pair(xbuf[slot, pl.ds(0, rows)]).astype(_BF16)
            a = _dot(x, wg_bf[...])
            u = _dot(x, wu_bf[...])
            h = (a * jax.nn.sigmoid(a) * u).astype(_BF16)
            ybuf[slot, pl.ds(0, rows)] = _pack_bf16_pair(_dot(h, wd_bf[...]))
            y_copy(slot, rows, row).start()

        per_class(g, compute)
        return carry

    lax.fori_loop(0, n_here, piece, 0)

    @pl.when(e == N_EXPERTS - 1)
    def _():
        @pl.when(n_total >= 2)
        def _():
            per_class(n_total - 2, lambda rows, row: y_copy(lax.rem(n_total, 2), rows, row).wait())

        @pl.when(n_total >= 1)
        def _():
            per_class(n_total - 1, lambda rows, row: y_copy(lax.rem(n_total - 1, 2), rows, row).wait())


def _experts(piece_start, piece_row, piece_cls, n_rows, xs, w_gate, w_up, w_down):
    dh = xs.shape[1]
    d = 2 * dh
    tm = EXP_UNIT * EXP_CLASSES
    anyspec = pl.BlockSpec(memory_space=pl.ANY)

    def chunk_specs(rows, cols):
        return [pl.BlockSpec((1, 1, rows // W_SPLIT, cols), lambda e, ps, pr, pc, q=q: (e, q, 0, 0))
                for q in range(W_SPLIT)]

    split = lambda w: w.reshape(w.shape[0], W_SPLIT, w.shape[1] // W_SPLIT, w.shape[2])
    grid_spec = pltpu.PrefetchScalarGridSpec(
        num_scalar_prefetch=3,
        grid=(N_EXPERTS,),
        in_specs=(chunk_specs(d, D_EXPERT) + chunk_specs(d, D_EXPERT) + chunk_specs(D_EXPERT, d) + [anyspec]),
        out_specs=anyspec,
        scratch_shapes=[pltpu.VMEM((d, D_EXPERT), _BF16), pltpu.VMEM((d, D_EXPERT), _BF16),
                        pltpu.VMEM((D_EXPERT, d), _BF16),
                        pltpu.VMEM((2, tm, dh), _U32), pltpu.VMEM((2, tm, dh), _U32),
                        pltpu.SemaphoreType.DMA((2,)), pltpu.SemaphoreType.DMA((2,))],
    )
    return pl.pallas_call(
        _experts_kernel,
        grid_spec=grid_spec,
        out_shape=jax.ShapeDtypeStruct((n_rows, dh), _U32),
        compiler_params=pltpu.CompilerParams(
            dimension_semantics=("arbitrary",), vmem_limit_bytes=VMEM_LIMIT),
        name="experts",
    )(piece_start, piece_row, piece_cls, *([split(w_gate)] * W_SPLIT), *([split(w_up)] * W_SPLIT),
      *([split(w_down)] * W_SPLIT), xs)


def _final_kernel(x1_ref, ya_ref, yb_ref, route_ref, gf_ref, y_ref):
    route = route_ref[...]
    x2 = (x1_ref[...] + route[:, R_W1:R_W1 + 1] * _unpack_bf16_pair(ya_ref[...])
          + route[:, R_W2:R_W2 + 1] * _unpack_bf16_pair(yb_ref[...]))
    y_ref[...] = _rms(x2, gf_ref[...])


def _final(x1, yab, route, gf, n_prompt, n_sample):
    d = x1.shape[1]

    def call(tm, first_block, n_rows, name):
        tok = lambda w: pl.BlockSpec((tm, w), lambda i: (first_block + i, 0))
        sel = lambda k: pl.BlockSpec((None, tm, d // 2), lambda i: (k, first_block + i, 0))
        return pl.pallas_call(
            _final_kernel,
            grid=(n_rows // tm,),
            in_specs=[tok(d), sel(0), sel(1), tok(LANES), pl.BlockSpec((1, d), lambda i: (0, 0))],
            out_specs=pl.BlockSpec((tm, d), lambda i: (i, 0)),
            out_shape=jax.ShapeDtypeStruct((n_rows, d), _F32),
            compiler_params=pltpu.CompilerParams(
                dimension_semantics=("arbitrary",), vmem_limit_bytes=VMEM_LIMIT),
            name=name,
        )(x1, yab, yab, route, gf)

    return (call(FINAL_TM, 0, n_prompt, "final_prompt"),
            call(n_sample, n_prompt // n_sample, n_sample, "final_sample"))


def _ssm_params(lam_re, lam_im, log_dt, b_re, b_im, c_re, c_im, d_skip):
    dt = jnp.exp(log_dt)[:, None]
    mag = jnp.exp(lam_re * dt)
    ang = lam_im * dt
    lb_re = mag * jnp.cos(ang)
    lb_im = mag * jnp.sin(ang)
    den = lam_re * lam_re + lam_im * lam_im
    nr = lb_re - 1.0
    ni = lb_im
    k_re = (nr * lam_re + ni * lam_im) / den
    k_im = (ni * lam_re - nr * lam_im) / den
    bb_re = k_re[:, :, None] * b_re - k_im[:, :, None] * b_im
    bb_im = k_re[:, :, None] * b_im + k_im[:, :, None] * b_re
    eye = jnp.eye(SUBLANES, dtype=_F32).reshape(1, SUBLANES, 1, SUBLANES, 1)

    def blockdiag(a):
        r, c = a.shape[1:]
        t = a.reshape(N_LANE_TILES, SUBLANES, r, 1, c) * eye
        return t.reshape(N_LANE_TILES, SUBLANES * r, SUBLANES * c)

    tr = lambda a: jnp.transpose(a, (0, 2, 1))
    wb = jnp.concatenate([blockdiag(tr(bb_re)), blockdiag(tr(bb_im))], axis=-1)
    wc = jnp.concatenate([blockdiag(tr(c_re)), -blockdiag(tr(c_im))], axis=1)

    m = jnp.arange(SSM_BLK + 1, dtype=_F32)[:, None, None]
    pw_re = jnp.exp(m * lam_re * dt) * jnp.cos(m * ang)
    pw_im = jnp.exp(m * lam_re * dt) * jnp.sin(m * ang)
    v_rows = []
    for s in range(SSM_BLK):
        pr, pi = pw_re[SSM_BLK - 1 - s][:, :, None], pw_im[SSM_BLK - 1 - s][:, :, None]
        v_rows.append(jnp.concatenate([blockdiag(tr(pr * bb_re - pi * bb_im)),
                                       blockdiag(tr(pr * bb_im + pi * bb_re))], axis=-1))
    v = jnp.concatenate(v_rows, axis=1)
    cl_re = [c_re * pw_re[q][:, None, :] - c_im * pw_im[q][:, None, :] for q in range(SSM_BLK + 1)]
    cl_im = [c_re * pw_im[q][:, None, :] + c_im * pw_re[q][:, None, :] for q in range(SSM_BLK + 1)]
    hi_p = lax.Precision.HIGHEST
    direct = [jnp.einsum('ghp,gpk->gkh', cl_re[q], bb_re, precision=hi_p)
              - jnp.einsum('ghp,gpk->gkh', cl_im[q], bb_im, precision=hi_p) for q in range(SSM_BLK)]
    zero = jnp.zeros((N_LANE_TILES, LANES, LANES), _F32)
    r_state_re = jnp.concatenate([blockdiag(tr(cl_re[i + 1])) for i in range(SSM_BLK)], axis=-1)
    r_state_im = jnp.concatenate([-blockdiag(tr(cl_im[i + 1])) for i in range(SSM_BLK)], axis=-1)
    r_direct = jnp.concatenate(
        [jnp.concatenate([blockdiag(direct[i - s]) if s <= i else zero for i in range(SSM_BLK)], axis=-1)
         for s in range(SSM_BLK)], axis=1)
    r = jnp.concatenate([r_state_re, r_state_im, r_direct], axis=1)
    flat = lambda a: a.reshape(1, STATE_COLS)
    return (wb, wc, flat(lb_re), flat(lb_im), v, r, flat(pw_re[SSM_BLK]), flat(pw_im[SSM_BLK]),
            d_skip.reshape(1, D_SSM))


def _dispatch_plan(route_t, cnt):
    t_all = route_t.shape[1]
    codes = route_t[R_CODE1:R_CODE2 + 1].astype(jnp.int32).reshape(-1)
    per_pass = SC_WORKERS * DISPATCH_CHUNK
    codes = jnp.pad(codes, (0, -(2 * t_all) % per_pass))
    counts = cnt[0, :N_EXPERTS].astype(jnp.int32)
    zero = jnp.zeros((1,), jnp.int32)
    units = (counts + EXP_UNIT - 1) // EXP_UNIT
    unit_start = jnp.concatenate([zero, jnp.cumsum(units)])
    start_row = jnp.zeros((LANES,), jnp.int32).at[:N_EXPERTS].set(unit_start[:N_EXPERTS] * EXP_UNIT)
    pieces = (units + EXP_CLASSES - 1) // EXP_CLASSES
    piece_start = jnp.concatenate([zero, jnp.cumsum(pieces)])
    tm = EXP_UNIT * EXP_CLASSES
    max_units = (2 * t_all + N_EXPERTS * (EXP_UNIT - 1)) // EXP_UNIT
    max_pieces = (max_units + N_EXPERTS * (EXP_CLASSES - 1)) // EXP_CLASSES
    g = jnp.arange(max_pieces, dtype=jnp.int32)
    owner = ((g[:, None] >= piece_start[None, :-1]) & (g[:, None] < piece_start[None, 1:])).astype(jnp.int32)
    pick = lambda table: jnp.sum(owner * table[None, :], axis=1)
    first_unit = pick(unit_start[:-1]) + (g - pick(piece_start[:-1])) * EXP_CLASSES
    piece_row = first_unit * EXP_UNIT
    piece_cls = jnp.clip(pick(unit_start[1:]) - first_unit, 1, EXP_CLASSES)
    n_rows = (max_units * EXP_UNIT + tm - 1) // tm * tm + tm
    return codes, start_row, n_rows, piece_start, piece_row, piece_cls


def kernel(x_prompt, x_sample, state_ssm_re, state_ssm_im, norm1_g, w_in, lam_re, lam_im, log_dt, ssm_b_re, ssm_b_im, ssm_c_re, ssm_c_im, ssm_d, gmlp_norm_g, gmlp_w_s, gmlp_b_s, out_norm_ssm_g, out_norm_gmlp_g, w_out, norm2_g, w_router_group, b_router_group, w_router_expert, b_router_expert, w_gate, w_up, w_down, final_norm_g):
    n, l, d = x_prompt.shape
    ns = x_sample.shape[0]
    t_all = n * l + ns
    li = 0
    g1 = norm1_g[li].reshape(1, d)
    gn = gmlp_norm_g[li].reshape(1, D_GMLP)
    tril = jnp.tril(jnp.ones((CHUNK, CHUNK), dtype=bool))
    ws_tril = jnp.where(tril[None], gmlp_w_s[li], 0.0)
    bs = gmlp_b_s[li]
    gog = out_norm_gmlp_g[li].reshape(1, D_GMLP)
    gos = out_norm_ssm_g[li].reshape(1, D_SSM)
    wb, wc, lbr, lbi, v_blk, r_blk, lpr, lpi, dsk = _ssm_params(
        lam_re[li], lam_im[li], log_dt[li], ssm_b_re[li], ssm_b_im[li], ssm_c_re[li], ssm_c_im[li], ssm_d[li])
    g2 = norm2_g[li].reshape(1, d)
    pad = LANES - N_EXPERTS - N_EXPERT_GROUPS
    wr = jnp.concatenate([w_router_expert[li], w_router_group[li], jnp.zeros((d, pad), _F32)], axis=1)
    br = jnp.concatenate([b_router_expert[li], b_router_group[li], jnp.zeros((pad,), _F32)]).reshape(1, LANES)

    xa, sg, mixb = _front_prompt(x_prompt, g1, w_in[li], gn, ws_tril.astype(_BF16), bs.T, gog)
    mixa, hfin = _ssm_prompt(xa, sg, v_blk.astype(_BF16), r_blk.astype(_BF16), lpr, lpi, dsk, gos)
    w00 = jnp.repeat(ws_tril[:, 0, 0], GMLP_HEAD).reshape(1, D_GMLP)
    b0 = jnp.repeat(bs[:, 0], GMLP_HEAD).reshape(1, D_GMLP)
    mix_s, hr_s, hi_s, vrow = _front_sample(
        x_sample.reshape(ns, d), g1, w_in[li], gn, w00, b0, gog, wb, wc, lbr, lbi, dsk, gos,
        state_ssm_re[li].reshape(ns, STATE_COLS), state_ssm_im[li].reshape(ns, STATE_COLS))

    x1, xn, route, route_t, cnt = _mixer_out(x_prompt, mixa, mixb, x_sample.reshape(ns, d), mix_s,
                                             w_out[li], g2, wr, br)
    codes, start_row, n_rows, piece_start, piece_row, piece_cls = _dispatch_plan(route_t, cnt)
    xs, dest = _sc_dispatch(xn, codes, start_row, n_rows, DISPATCH_CHUNK)
    ys = _experts(piece_start, piece_row, piece_cls, n_rows, xs, w_gate[li], w_up[li], w_down[li])
    yab = _sc_combine(ys, dest, 2 * t_all, COMBINE_CHUNK).reshape(2, t_all, d // 2)
    y_p, y_s = _final(x1, yab, route, final_norm_g.reshape(1, d), n * l, ns)

    hf = hfin.reshape(n, N_LANE_TILES, 2, 8, SSM_STATE)
    re_p = hf[:, :, 0].reshape(1, n, N_SSM_GROUPS, SSM_STATE)
    im_p = hf[:, :, 1].reshape(1, n, N_SSM_GROUPS, SSM_STATE)
    re_s = hr_s.reshape(1, ns, N_SSM_GROUPS, SSM_STATE)
    im_s = hi_s.reshape(1, ns, N_SSM_GROUPS, SSM_STATE)
    return (y_p.reshape(n, l, d), y_s.reshape(ns, 1, d), re_p, im_p, re_s, im_s,
            vrow.reshape(1, ns, 1, D_GMLP))
```

```python
import math

import jax
import jax.numpy as jnp
from jax import lax
from jax.experimental import pallas as pl
from jax.experimental.pallas import tpu as pltpu
from jax.experimental.pallas import tpu_sc as plsc

D_MODEL = 1024
D_SSM = 512
D_GMLP = 512
SSM_GROUP = 16
N_SSM_GROUPS = 32
SSM_STATE = 64
CHUNK = 128
N_GMLP_HEADS = 4
GMLP_HEAD = 128
N_EXPERT_GROUPS = 4
EXPERTS_PER_GROUP = 8
N_EXPERTS = 32
D_EXPERT = 512
D_IN = 2048
EPS = 1e-6

LANES = 128
SUBLANES = 8
N_LANE_TILES = D_SSM // LANES
STATE_COLS = N_SSM_GROUPS * SSM_STATE
TILE_STATE = STATE_COLS // N_LANE_TILES
VMEM_LIMIT = 56 * 1024 * 1024

SC_CORES = 2
SC_SUBCORES = 16
SC_LANES = 16
SC_WORKERS = SC_CORES * SC_SUBCORES

FRONT_TL = 512
SSM_LC = 256
SSM_BLK = 4
TOK_TM = 512
FINAL_TM = 512
EXP_UNIT = 128
EXP_CLASSES = 4
W_SPLIT = 4
DISPATCH_CHUNK = 80
COMBINE_CHUNK = 24

R_E1, R_E2, R_W1, R_W2, R_RANK1, R_RANK2, R_CODE1, R_CODE2 = 0, 1, 2, 3, 4, 5, 6, 7
CODE_BITS = 16
CODE_SHIFT = float(1 << CODE_BITS)

_INV_SQRT2 = 1.0 / math.sqrt(2.0)
_BF16 = jnp.bfloat16
_F32 = jnp.float32
_U32 = jnp.uint32


def _gelu(x):
    return 0.5 * x * (1.0 + lax.erf(x * _INV_SQRT2))


def _rms(x, g):
    return x * lax.rsqrt(jnp.mean(x * x, axis=-1, keepdims=True) + EPS) * g


def _dot(a, b):
    return jnp.dot(a, b, preferred_element_type=_F32)


def _dot_f32(a, b):
    return jnp.dot(a, b, preferred_element_type=_F32, precision=lax.Precision.HIGHEST)


def _pack_bf16_pair(x):
    w = x.shape[1] // 2
    hi = lax.bitcast_convert_type(x[:, :w].astype(_BF16).astype(_F32), _U32)
    lo = lax.bitcast_convert_type(x[:, w:].astype(_BF16).astype(_F32), _U32)
    return hi | (lo >> 16)


def _unpack_bf16_pair(p):
    hi = lax.bitcast_convert_type(p & jnp.uint32(0xFFFF0000), _F32)
    lo = lax.bitcast_convert_type(p << 16, _F32)
    return jnp.concatenate([hi, lo], axis=-1)


def _head_norm_gelu(vb, gn):
    v = _gelu(vb)
    parts = []
    for h in range(N_GMLP_HEADS):
        vh = v[:, h * GMLP_HEAD:(h + 1) * GMLP_HEAD]
        parts.append(vh * lax.rsqrt(jnp.mean(vh * vh, axis=-1, keepdims=True) + EPS))
    return jnp.concatenate(parts, axis=-1) * gn


def _front_prompt_kernel(x_ref, g1_ref, win_ref, gn_ref, ws_ref, bs_ref, gog_ref,
                         xa_ref, sg_ref, mixb_ref, win_bf):
    @pl.when((pl.program_id(0) == 0) & (pl.program_id(1) == 0))
    def _():
        win_bf[...] = win_ref[...].astype(_BF16)

    x = x_ref[0]
    hn = _rms(x, g1_ref[...]).astype(_BF16)
    z = _dot(hn, win_bf[...])
    xa_ref[0] = z[:, :D_SSM]
    sg_ref[0] = jax.nn.sigmoid(z[:, D_SSM:2 * D_SSM])
    ub = _gelu(z[:, 2 * D_SSM:2 * D_SSM + D_GMLP])
    vbn = _head_norm_gelu(z[:, 2 * D_SSM + D_GMLP:], gn_ref[...]).astype(_BF16)
    tl = x.shape[0]
    rows = []
    for c in range(tl // CHUNK):
        heads = []
        for h in range(N_GMLP_HEADS):
            vh = vbn[c * CHUNK:(c + 1) * CHUNK, h * GMLP_HEAD:(h + 1) * GMLP_HEAD]
            heads.append(_dot(ws_ref[h], vh) + bs_ref[:, h:h + 1])
        rows.append(jnp.concatenate(heads, axis=-1))
    s = jnp.concatenate(rows, axis=0)
    mixb_ref[0] = _rms(ub * s, gog_ref[...]).astype(_BF16)


def _front_prompt(x, g1, win, gn, ws_tril_bf, bs_t, gog):
    n, l, d = x.shape
    tl = FRONT_TL
    grid = (n, l // tl)
    const = lambda *shape: pl.BlockSpec(shape, lambda b, i: (0,) * len(shape))
    seq = lambda w: pl.BlockSpec((1, tl, w), lambda b, i: (b, i, 0))
    return pl.pallas_call(
        _front_prompt_kernel,
        grid=grid,
        in_specs=[seq(d), const(1, d), const(d, D_IN), const(1, D_GMLP),
                  const(N_GMLP_HEADS, CHUNK, CHUNK), const(CHUNK, N_GMLP_HEADS), const(1, D_GMLP)],
        out_specs=[seq(D_SSM), seq(D_SSM), seq(D_GMLP)],
        out_shape=[jax.ShapeDtypeStruct((n, l, D_SSM), _F32),
                   jax.ShapeDtypeStruct((n, l, D_SSM), _F32),
                   jax.ShapeDtypeStruct((n, l, D_GMLP), _BF16)],
        scratch_shapes=[pltpu.VMEM((d, D_IN), _BF16)],
        compiler_params=pltpu.CompilerParams(
            dimension_semantics=("arbitrary", "arbitrary"), vmem_limit_bytes=VMEM_LIMIT),
        name="front_prompt",
    )(x, g1, win, gn, ws_tril_bf, bs_t, gog)


def _ssm_prompt_kernel(xa_ref, sg_ref, v_ref, r_ref, l4r_ref, l4i_ref, dsk_ref, gos_ref,
                       mixa_ref, hfin_ref, s_ref, st_ref):
    lc = xa_ref.shape[1]
    nblk = lc // SSM_BLK
    rows = nblk * SUBLANES

    @pl.when(pl.program_id(0) == 0)
    def _():
        st_ref[...] = jnp.zeros_like(st_ref)

    def by_position(ref):
        t = pltpu.einshape("btc->tbc", ref[...]).reshape(nblk, SSM_BLK, SUBLANES, D_SSM)
        return [t[:, i].reshape(rows, D_SSM) for i in range(SSM_BLK)]

    xs = by_position(xa_ref)
    xs_bf = [x.astype(_BF16) for x in xs]
    xk = [jnp.concatenate([x[:, k * LANES:(k + 1) * LANES] for x in xs_bf], axis=-1)
          for k in range(N_LANE_TILES)]
    for k in range(N_LANE_TILES):
        s_ref[:, 2 * TILE_STATE * k:2 * TILE_STATE * (k + 1)] = _dot(xk[k], v_ref[k])

    for kk in range(0, N_LANE_TILES, 2):
        tiles = (kk, kk + 1)
        cols = [(2 * TILE_STATE * k, 2 * TILE_STATE * k + TILE_STATE) for k in tiles]
        lbs = [(jnp.broadcast_to(l4r_ref[:, k * TILE_STATE:(k + 1) * TILE_STATE], (SUBLANES, TILE_STATE)),
                jnp.broadcast_to(l4i_ref[:, k * TILE_STATE:(k + 1) * TILE_STATE], (SUBLANES, TILE_STATE)))
               for k in tiles]

        def body(j, carry, cols=cols, lbs=lbs):
            r0 = pl.multiple_of(j * SUBLANES, SUBLANES)
            out = []
            for q, ((c_re, c_im), (lr, li)) in enumerate(zip(cols, lbs)):
                hr, hi = carry[2 * q], carry[2 * q + 1]
                sr = s_ref[pl.ds(r0, SUBLANES), c_re:c_re + TILE_STATE]
                si = s_ref[pl.ds(r0, SUBLANES), c_im:c_im + TILE_STATE]
                s_ref[pl.ds(r0, SUBLANES), c_re:c_re + TILE_STATE] = hr
                s_ref[pl.ds(r0, SUBLANES), c_im:c_im + TILE_STATE] = hi
                out += [lr * hr - li * hi + sr, lr * hi + li * hr + si]
            return tuple(out)

        init = tuple(st_ref[:, c:c + TILE_STATE] for c_pair in cols for c in c_pair)
        fin = lax.fori_loop(0, nblk, body, init, unroll=2)
        for q, (c_re, c_im) in enumerate(cols):
            st_ref[:, c_re:c_re + TILE_STATE] = fin[2 * q]
            st_ref[:, c_im:c_im + TILE_STATE] = fin[2 * q + 1]

    yk = []
    for k in range(N_LANE_TILES):
        h_in = s_ref[:, 2 * TILE_STATE * k:2 * TILE_STATE * (k + 1)].astype(_BF16)
        yk.append(_dot(jnp.concatenate([h_in, xk[k]], axis=-1), r_ref[k]))
    sgs = by_position(sg_ref)
    outs = []
    for i in range(SSM_BLK):
        y = jnp.concatenate([y_k[:, i * LANES:(i + 1) * LANES] for y_k in yk], axis=-1) + dsk_ref[...] * xs[i]
        outs.append(_rms(_gelu(y) * sgs[i], gos_ref[...]).reshape(nblk, SUBLANES, D_SSM))
    mixa = jnp.stack(outs, axis=1).reshape(lc, SUBLANES, D_SSM)
    mixa_ref[...] = pltpu.einshape("tbc->btc", mixa).astype(_BF16)
    hfin_ref[...] = st_ref[...]


def _ssm_prompt(xa, sg, v, r, l4r, l4i, dsk, gos):
    n, l, _ = xa.shape
    lc = SSM_LC
    const = lambda *shape: pl.BlockSpec(shape, lambda i: (0,) * len(shape))
    seq_spec = pl.BlockSpec((n, lc, D_SSM), lambda i: (0, i, 0))
    return pl.pallas_call(
        _ssm_prompt_kernel,
        grid=(l // lc,),
        in_specs=[seq_spec, seq_spec, const(*v.shape), const(*r.shape),
                  const(1, STATE_COLS), const(1, STATE_COLS), const(1, D_SSM), const(1, D_SSM)],
        out_specs=[seq_spec, const(n, 2 * STATE_COLS)],
        out_shape=[jax.ShapeDtypeStruct((n, l, D_SSM), _BF16),
                   jax.ShapeDtypeStruct((n, 2 * STATE_COLS), _F32)],
        scratch_shapes=[pltpu.VMEM((lc // SSM_BLK * n, 2 * STATE_COLS), _F32),
                        pltpu.VMEM((n, 2 * STATE_COLS), _F32)],
        compiler_params=pltpu.CompilerParams(
            dimension_semantics=("arbitrary",), vmem_limit_bytes=VMEM_LIMIT),
        name="ssm_prompt",
    )(xa, sg, v, r, l4r, l4i, dsk, gos)


def _front_sample_kernel(x_ref, g1_ref, win_ref, gn_ref, w00_ref, b0_ref, gog_ref,
                         wb_ref, wc_ref, lbr_ref, lbi_ref, dsk_ref, gos_ref, h0r_ref, h0i_ref,
                         mix_ref, hr_ref, hi_ref, vrow_ref):
    x = x_ref[...]
    hn = _rms(x, g1_ref[...])
    z = _dot_f32(hn, win_ref[...])
    xa = z[:, :D_SSM]
    ys = []
    for k in range(N_LANE_TILES):
        bu = _dot_f32(xa[:, k * LANES:(k + 1) * LANES], wb_ref[k])
        sl = slice(k * TILE_STATE, (k + 1) * TILE_STATE)
        lr, li = lbr_ref[:, sl], lbi_ref[:, sl]
        h0r, h0i = h0r_ref[:, sl], h0i_ref[:, sl]
        nr = lr * h0r - li * h0i + bu[:, :TILE_STATE]
        ni = lr * h0i + li * h0r + bu[:, TILE_STATE:]
        hr_ref[:, sl] = nr
        hi_ref[:, sl] = ni
        ys.append(_dot_f32(jnp.concatenate([nr, ni], axis=-1), wc_ref[k]))
    y = jnp.concatenate(ys, axis=-1) + dsk_ref[...] * xa
    ya = _gelu(y) * jax.nn.sigmoid(z[:, D_SSM:2 * D_SSM])
    mix_ref[:, :D_SSM] = _rms(ya, gos_ref[...])
    ub = _gelu(z[:, 2 * D_SSM:2 * D_SSM + D_GMLP])
    vbn = _head_norm_gelu(z[:, 2 * D_SSM + D_GMLP:], gn_ref[...])
    vrow_ref[...] = vbn
    s = w00_ref[...] * vbn + b0_ref[...]
    mix_ref[:, D_SSM:] = _rms(ub * s, gog_ref[...])


def _front_sample(x, g1, win, gn, w00, b0, gog, wb, wc, lbr, lbi, dsk, gos, h0r, h0i):
    n = x.shape[0]
    vmem = pl.BlockSpec(memory_space=pltpu.VMEM)
    return pl.pallas_call(
        _front_sample_kernel,
        in_specs=[vmem] * 15,
        out_specs=[vmem] * 4,
        out_shape=[jax.ShapeDtypeStruct((n, D_MODEL), _F32),
                   jax.ShapeDtypeStruct((n, STATE_COLS), _F32),
                   jax.ShapeDtypeStruct((n, STATE_COLS), _F32),
                   jax.ShapeDtypeStruct((n, D_GMLP), _F32)],
        compiler_params=pltpu.CompilerParams(vmem_limit_bytes=VMEM_LIMIT),
        name="front_sample",
    )(x, g1, win, gn, w00, b0, gog, wb, wc, lbr, lbi, dsk, gos, h0r, h0i)


def _route(logits, base):
    tm = logits.shape[0]
    lane = lax.broadcasted_iota(jnp.int32, logits.shape, 1).astype(_F32)
    neg = jnp.float32(-jnp.inf)
    big = jnp.float32(LANES)
    is_g = (lane >= N_EXPERTS) & (lane < N_EXPERTS + N_EXPERT_GROUPS)
    gl = jnp.where(is_g, logits, neg)
    gmax = jnp.max(gl, axis=-1, keepdims=True)
    gi = jnp.min(jnp.where(is_g & (logits == gmax), lane, big), axis=-1, keepdims=True) - N_EXPERTS
    p_top = 1.0 / jnp.sum(jnp.where(is_g, jnp.exp(gl - gmax), 0.0), axis=-1, keepdims=True)
    lo = gi * EXPERTS_PER_GROUP
    in_grp = (lane >= lo) & (lane < lo + EXPERTS_PER_GROUP)
    m1 = jnp.max(jnp.where(in_grp, logits, neg), axis=-1, keepdims=True)
    i1 = jnp.min(jnp.where(in_grp & (logits == m1), lane, big), axis=-1, keepdims=True)
    rest = in_grp & (lane != i1)
    m2 = jnp.max(jnp.where(rest, logits, neg), axis=-1, keepdims=True)
    i2 = jnp.min(jnp.where(rest & (logits == m2), lane, big), axis=-1, keepdims=True)
    e2 = jnp.exp(m2 - m1)
    w1 = p_top / (1.0 + e2)
    w2 = p_top * e2 / (1.0 + e2)
    sel1 = lane == i1
    sel2 = lane == i2
    hits = jnp.where(sel1 | sel2, 1.0, 0.0)
    r_id = lax.broadcasted_iota(jnp.int32, (tm, tm), 0)
    c_id = lax.broadcasted_iota(jnp.int32, (tm, tm), 1)
    ltri = jnp.where(c_id < r_id, 1.0, 0.0).astype(_BF16)
    before = _dot(ltri, hits.astype(_BF16)) + base
    rank1 = jnp.sum(jnp.where(sel1, before, 0.0), axis=-1, keepdims=True)
    rank2 = jnp.sum(jnp.where(sel2, before, 0.0), axis=-1, keepdims=True)
    out = jnp.where(lane == R_E1, i1, 0.0)
    out = jnp.where(lane == R_E2, i2, out)
    out = jnp.where(lane == R_W1, w1, out)
    out = jnp.where(lane == R_W2, w2, out)
    out = jnp.where(lane == R_RANK1, rank1, out)
    out = jnp.where(lane == R_RANK2, rank2, out)
    out = jnp.where(lane == R_CODE1, i1 * CODE_SHIFT + rank1, out)
    out = jnp.where(lane == R_CODE2, i2 * CODE_SHIFT + rank2, out)
    return out, base + jnp.sum(hits, axis=0, keepdims=True)


def _mixer_out_prompt_kernel(x_ref, mixa_ref, mixb_ref, wo_ref, g2_ref, wr_ref, br_ref,
                             x1_ref, xn_ref, route_ref, route_t_ref, cnt_ref, base_ref):
    @pl.when((pl.program_id(0) == 0) & (pl.program_id(1) == 0))
    def _():
        base_ref[...] = jnp.zeros_like(base_ref)

    x1 = x_ref[0] + _dot(mixa_ref[0], wo_ref[:D_SSM, :]) + _dot(mixb_ref[0], wo_ref[D_SSM:, :])
    xn = _rms(x1, g2_ref[...])
    logits = _dot(xn.astype(_BF16), wr_ref[...]) + br_ref[...]
    route, base = _route(logits, base_ref[...])
    x1_ref[...] = x1
    xn_ref[...] = _pack_bf16_pair(xn)
    route_ref[...] = route
    route_t_ref[...] = route.T[:SUBLANES, :]
    base_ref[...] = base
    cnt_ref[...] = base


def _mixer_out_sample_kernel(x_ref, mix_ref, wo_ref, g2_ref, wr_ref, br_ref, cnt_in_ref,
                             x1_in, xn_in, route_in, route_t_in,
                             x1_ref, xn_ref, route_ref, route_t_ref, cnt_ref):
    del x1_in, xn_in, route_in, route_t_in
    x1 = (x_ref[...] + _dot_f32(mix_ref[:, :D_SSM], wo_ref[:D_SSM, :])
          + _dot_f32(mix_ref[:, D_SSM:], wo_ref[D_SSM:, :]))
    xn = _rms(x1, g2_ref[...])
    logits = _dot_f32(xn, wr_ref[...]) + br_ref[...]
    route, base = _route(logits, cnt_in_ref[...])
    x1_ref[...] = x1
    xn_ref[...] = _pack_bf16_pair(xn)
    route_ref[...] = route
    route_t_ref[...] = route.T[:SUBLANES, :]
    cnt_ref[...] = base


def _mixer_out(x_p, mixa, mixb, x_s, mix_s, wo, g2, wr, br):
    n, l, d = x_p.shape
    ns = x_s.shape[0]
    t_all = n * l + ns
    tm = TOK_TM
    per_seq = l // tm
    const = lambda *shape: pl.BlockSpec(shape, lambda b, i: (0,) * len(shape))
    seq = lambda w: pl.BlockSpec((1, tm, w), lambda b, i: (b, i, 0))
    tok = lambda w: pl.BlockSpec((tm, w), lambda b, i: (b * per_seq + i, 0))
    tok_shapes = [jax.ShapeDtypeStruct((t_all, d), _F32),
                  jax.ShapeDtypeStruct((t_all, d // 2), _U32),
                  jax.ShapeDtypeStruct((t_all, LANES), _F32),
                  jax.ShapeDtypeStruct((SUBLANES, t_all), _F32)]
    cnt_shape = jax.ShapeDtypeStruct((1, LANES), _F32)
    x1, xn, route, route_t, cnt = pl.pallas_call(
        _mixer_out_prompt_kernel,
        grid=(n, per_seq),
        in_specs=[seq(d), seq(D_SSM), seq(D_GMLP),
                  const(d, d), const(1, d), const(d, LANES), const(1, LANES)],
        out_specs=[tok(d), tok(d // 2), tok(LANES),
                   pl.BlockSpec((SUBLANES, tm), lambda b, i: (0, b * per_seq + i)), const(1, LANES)],
        out_shape=tok_shapes + [cnt_shape],
        scratch_shapes=[pltpu.VMEM((1, LANES), _F32)],
        compiler_params=pltpu.CompilerParams(
            dimension_semantics=("arbitrary", "arbitrary"), vmem_limit_bytes=VMEM_LIMIT),
        name="mixer_out_prompt",
    )(x_p, mixa, mixb, wo.astype(_BF16), g2, wr.astype(_BF16), br)
    tail = (n * l) // ns
    c1 = lambda *shape: pl.BlockSpec(shape, lambda i: (0,) * len(shape))
    anyspec = pl.BlockSpec(memory_space=pl.ANY)
    tail_spec = lambda w: pl.BlockSpec((ns, w), lambda i: (tail, 0))
    return pl.pallas_call(
        _mixer_out_sample_kernel,
        grid=(1,),
        in_specs=[c1(ns, d), c1(ns, d), c1(d, d), c1(1, d), c1(d, LANES), c1(1, LANES), c1(1, LANES),
                  anyspec, anyspec, anyspec, anyspec],
        out_specs=[tail_spec(d), tail_spec(d // 2), tail_spec(LANES),
                   pl.BlockSpec((SUBLANES, ns), lambda i: (0, tail)), c1(1, LANES)],
        out_shape=tok_shapes + [cnt_shape],
        input_output_aliases={7: 0, 8: 1, 9: 2, 10: 3},
        compiler_params=pltpu.CompilerParams(
            dimension_semantics=("arbitrary",), vmem_limit_bytes=VMEM_LIMIT),
        name="mixer_out_sample",
    )(x_s, mix_s, wo, g2, wr, br, cnt, x1, xn, route, route_t)


def _sc_stream(n_chunks, gather, write):
    gather(0).start()
    for j in range(n_chunks):
        if j + 1 < n_chunks:
            if j >= 1:
                write(j - 1).wait()
            gather(j + 1).start()
        gather(j).wait()
        write(j).start()
    if n_chunks >= 2:
        write(n_chunks - 2).wait()
    write(n_chunks - 1).wait()


def _sc_mesh():
    return plsc.VectorSubcoreMesh(core_axis_name="c", subcore_axis_name="s",
                                  num_cores=SC_CORES, num_subcores=SC_SUBCORES)


def _sc_buffers(chunk, w, dtype):
    return [pltpu.VMEM((chunk, w), dtype), pltpu.VMEM((chunk, w), dtype)] + [pltpu.SemaphoreType.DMA] * 4


def _sc_combine(table, idx, n_out, chunk):
    w = table.shape[1]
    rows_w = n_out // SC_WORKERS
    n_chunks = rows_w // chunk
    assert rows_w * SC_WORKERS == n_out and n_chunks * chunk == rows_w and rows_w % SUBLANES == 0

    def body(table_hbm, idx_hbm, out_hbm, idx_v, buf0, buf1, g0, g1, w0, w1):
        wid = lax.axis_index("s") * SC_CORES + lax.axis_index("c")
        base = pl.multiple_of(wid * rows_w, SUBLANES)
        pltpu.sync_copy(idx_hbm.at[pl.ds(base, rows_w)], idx_v)
        bufs, gsems, wsems = (buf0, buf1), (g0, g1), (w0, w1)

        def gather(j):
            return pltpu.make_async_copy(table_hbm.at[idx_v.at[pl.ds(j * chunk, chunk)]], bufs[j % 2], gsems[j % 2])

        def write(j):
            return pltpu.make_async_copy(bufs[j % 2], out_hbm.at[pl.ds(base + j * chunk, chunk)], wsems[j % 2])

        _sc_stream(n_chunks, gather, write)

    return pl.kernel(
        body,
        out_type=jax.ShapeDtypeStruct((n_out, w), table.dtype),
        mesh=_sc_mesh(),
        scratch_types=[pltpu.VMEM((rows_w,), jnp.int32)] + _sc_buffers(chunk, w, table.dtype),
        compiler_params=pltpu.CompilerParams(use_tc_tiling_on_sc=True),
        name="sc_combine",
    )(table, idx)


def _sc_dispatch(table, codes, start_row, n_out, chunk):
    t_all, w = table.shape
    n_pad = codes.shape[0]
    n_ent = 2 * t_all
    ent_w = n_pad // SC_WORKERS
    n_chunks = ent_w // chunk
    per_chunk = chunk // SC_LANES
    trash = n_out - (n_pad - n_ent)
    assert ent_w * SC_WORKERS == n_pad and n_chunks * chunk == ent_w
    assert per_chunk * SC_LANES == chunk and chunk <= LANES and n_pad - n_ent <= t_all

    def body(table_hbm, code_hbm, start_hbm, out_hbm, dest_hbm,
             code_v, dest_v, tok_v, dst_v, start_v, buf0, buf1, g0, g1, w0, w1):
        wid = lax.axis_index("s") * SC_CORES + lax.axis_index("c")
        ebase = pl.multiple_of(wid * ent_w, SUBLANES)
        pltpu.sync_copy(code_hbm.at[pl.ds(ebase, ent_w)], code_v)
        pltpu.sync_copy(start_hbm, start_v)
        lane = lax.iota(jnp.int32, SC_LANES)
        for j in range(n_chunks):
            for c in range(per_chunk):
                off = j * chunk + c * SC_LANES
                ent = ebase + off + lane
                code = code_v[pl.ds(off, SC_LANES)]
                d = plsc.load_gather(start_v, [code >> CODE_BITS]) + (code & ((1 << CODE_BITS) - 1))
                d = jnp.where(ent >= n_ent, trash + (ent - n_ent), d)
                tok = jnp.where(ent >= t_all, ent - t_all, ent)
                tok = jnp.where(tok >= t_all, tok - t_all, tok)
                dest_v[pl.ds(off, SC_LANES)] = d
                dst_v[j, pl.ds(c * SC_LANES, SC_LANES)] = d
                tok_v[j, pl.ds(c * SC_LANES, SC_LANES)] = tok
        pltpu.sync_copy(dest_v, dest_hbm.at[pl.ds(ebase, ent_w)])
        bufs, gsems, wsems = (buf0, buf1), (g0, g1), (w0, w1)

        def gather(j):
            return pltpu.make_async_copy(table_hbm.at[tok_v.at[j]], bufs[j % 2], gsems[j % 2])

        def scatter(j):
            return pltpu.make_async_copy(bufs[j % 2], out_hbm.at[dst_v.at[j]], wsems[j % 2])

        _sc_stream(n_chunks, gather, scatter)

    return pl.kernel(
        body,
        out_type=(jax.ShapeDtypeStruct((n_out, w), table.dtype), jax.ShapeDtypeStruct((n_pad,), jnp.int32)),
        mesh=_sc_mesh(),
        scratch_types=([pltpu.VMEM((ent_w,), jnp.int32), pltpu.VMEM((ent_w,), jnp.int32),
                        pltpu.VMEM((n_chunks, chunk), jnp.int32), pltpu.VMEM((n_chunks, chunk), jnp.int32),
                        pltpu.VMEM((LANES,), jnp.int32)] + _sc_buffers(chunk, w, table.dtype)),
        compiler_params=pltpu.CompilerParams(use_tc_tiling_on_sc=True, needs_layout_passes=False),
        name="sc_dispatch",
    )(table, codes, start_row)


def _experts_kernel(piece_start_ref, piece_row_ref, piece_cls_ref, *refs):
    n_w = 3 * W_SPLIT
    wg_refs, wu_refs, wd_refs = refs[:W_SPLIT], refs[W_SPLIT:2 * W_SPLIT], refs[2 * W_SPLIT:n_w]
    xs_hbm, ys_hbm, wg_bf, wu_bf, wd_bf, xbuf, ybuf, xsem, ysem = refs[n_w:]
    e = pl.program_id(0)
    g0 = piece_start_ref[e]
    n_here = piece_start_ref[e + 1] - g0
    n_total = piece_start_ref[N_EXPERTS]

    def per_class(g, fn):
        cls = piece_cls_ref[g]
        row = pl.multiple_of(piece_row_ref[g], EXP_UNIT)
        for c in range(1, EXP_CLASSES + 1):
            pl.when(cls == c)(lambda c=c: fn(c * EXP_UNIT, row))

    def x_copy(slot, rows, row):
        return pltpu.make_async_copy(xs_hbm.at[pl.ds(row, rows)], xbuf.at[slot, pl.ds(0, rows)], xsem.at[slot])

    def y_copy(slot, rows, row):
        return pltpu.make_async_copy(ybuf.at[slot, pl.ds(0, rows)], ys_hbm.at[pl.ds(row, rows)], ysem.at[slot])

    @pl.when((e == 0) & (n_total > 0))
    def _():
        per_class(0, lambda rows, row: x_copy(0, rows, row).start())

    for dst, chunks in ((wg_bf, wg_refs), (wu_bf, wu_refs), (wd_bf, wd_refs)):
        rows = dst.shape[0] // W_SPLIT
        for q, src in enumerate(chunks):
            dst[q * rows:(q + 1) * rows, :] = src[0, 0].astype(_BF16)

    def piece(j, carry):
        g = g0 + j
        slot = lax.rem(g, 2)
        per_class(g, lambda rows, row: x_copy(slot, rows, row).wait())

        @pl.when(g + 1 < n_total)
        def _():
            per_class(g + 1, lambda rows, row: x_copy(1 - slot, rows, row).start())

        @pl.when(g >= 2)
        def _():
            per_class(g - 2, lambda rows, row: y_copy(slot, rows, row).wait())

        def compute(rows, row):
            x = _unpack_bf16_pair(xbuf[slot, pl.ds(0, rows)]).astype(_BF16)
            a = _dot(x, wg_bf[...])
            u = _dot(x, wu_bf[...])
            h = (a * jax.nn.sigmoid(a) * u).astype(_BF16)
            ybuf[slot, pl.ds(0, rows)] = _pack_bf16_pair(_dot(h, wd_bf[...]))
            y_copy(slot, rows, row).start()

        per_class(g, compute)
        return carry

    lax.fori_loop(0, n_here, piece, 0)

    @pl.when(e == N_EXPERTS - 1)
    def _():
        @pl.when(n_total >= 2)
        def _():
            per_class(n_total - 2, lambda rows, row: y_copy(lax.rem(n_total, 2), rows, row).wait())

        @pl.when(n_total >= 1)
        def _():
            per_class(n_total - 1, lambda rows, row: y_copy(lax.rem(n_total - 1, 2), rows, row).wait())


def _experts(piece_start, piece_row, piece_cls, n_rows, xs, w_gate, w_up, w_down):
    dh = xs.shape[1]
    d = 2 * dh
    tm = EXP_UNIT * EXP_CLASSES
    anyspec = pl.BlockSpec(memory_space=pl.ANY)

    def chunk_specs(rows, cols):
        return [pl.BlockSpec((1, 1, rows // W_SPLIT, cols), lambda e, ps, pr, pc, q=q: (e, q, 0, 0))
                for q in range(W_SPLIT)]

    split = lambda w: w.reshape(w.shape[0], W_SPLIT, w.shape[1] // W_SPLIT, w.shape[2])
    grid_spec = pltpu.PrefetchScalarGridSpec(
        num_scalar_prefetch=3,
        grid=(N_EXPERTS,),
        in_specs=(chunk_specs(d, D_EXPERT) + chunk_specs(d, D_EXPERT) + chunk_specs(D_EXPERT, d) + [anyspec]),
        out_specs=anyspec,
        scratch_shapes=[pltpu.VMEM((d, D_EXPERT), _BF16), pltpu.VMEM((d, D_EXPERT), _BF16),
                        pltpu.VMEM((D_EXPERT, d), _BF16),
                        pltpu.VMEM((2, tm, dh), _U32), pltpu.VMEM((2, tm, dh), _U32),
                        pltpu.SemaphoreType.DMA((2,)), pltpu.SemaphoreType.DMA((2,))],
    )
    return pl.pallas_call(
        _experts_kernel,
        grid_spec=grid_spec,
        out_shape=jax.ShapeDtypeStruct((n_rows, dh), _U32),
        compiler_params=pltpu.CompilerParams(
            dimension_semantics=("arbitrary",), vmem_limit_bytes=VMEM_LIMIT),
        name="experts",
    )(piece_start, piece_row, piece_cls, *([split(w_gate)] * W_SPLIT), *([split(w_up)] * W_SPLIT),
      *([split(w_down)] * W_SPLIT), xs)


def _final_kernel(x1_ref, ya_ref, yb_ref, route_ref, gf_ref, y_ref):
    route = route_ref[...]
    x2 = (x1_ref[...] + route[:, R_W1:R_W1 + 1] * _unpack_bf16_pair(ya_ref[...])
          + route[:, R_W2:R_W2 + 1] * _unpack_bf16_pair(yb_ref[...]))
    y_ref[...] = _rms(x2, gf_ref[...])


def _final(x1, yab, route, gf, n_prompt, n_sample):
    d = x1.shape[1]

    def call(tm, first_block, n_rows, name):
        tok = lambda w: pl.BlockSpec((tm, w), lambda i: (first_block + i, 0))
        sel = lambda k: pl.BlockSpec((None, tm, d // 2), lambda i: (k, first_block + i, 0))
        return pl.pallas_call(
            _final_kernel,
            grid=(n_rows // tm,),
            in_specs=[tok(d), sel(0), sel(1), tok(LANES), pl.BlockSpec((1, d), lambda i: (0, 0))],
            out_specs=pl.BlockSpec((tm, d), lambda i: (i, 0)),
            out_shape=jax.ShapeDtypeStruct((n_rows, d), _F32),
            compiler_params=pltpu.CompilerParams(
                dimension_semantics=("arbitrary",), vmem_limit_bytes=VMEM_LIMIT),
            name=name,
        )(x1, yab, yab, route, gf)

    return (call(FINAL_TM, 0, n_prompt, "final_prompt"),
            call(n_sample, n_prompt // n_sample, n_sample, "final_sample"))


def _powers(lam_re, lam_im, dt):
    out = []
    for m in range(SSM_BLK + 1):
        mag = jnp.exp(m * lam_re * dt)
        ang = m * lam_im * dt
        out.append((mag * jnp.cos(ang), mag * jnp.sin(ang)))
    return out


def _spread(x, copies):
    w = x.shape[1]
    src = lax.broadcasted_iota(jnp.int32, (w, w * copies), 0)
    dst = lax.broadcasted_iota(jnp.int32, (w, w * copies), 1)
    return _dot_f32(x, jnp.where(dst % w == src, 1.0, 0.0))


def _ssm_prep_kernel(lamr_row, lami_row, dt_row, lamr_col, lami_col, dt_col, b_re, b_im, c_re, c_im,
                     v_ref, r_ref, wb_ref, wc_ref):
    n_p, n_h = SSM_STATE, SSM_GROUP
    lr, li, dt = lamr_col[...], lami_col[...], dt_col[...]
    pc = _powers(lr, li, dt)
    den = lr * lr + li * li
    nr, ni = pc[1][0] - 1.0, pc[1][1]
    k_re = (nr * lr + ni * li) / den
    k_im = (ni * lr - nr * li) / den
    rows_b = TILE_STATE
    on_diag_b = (lax.broadcasted_iota(jnp.int32, (rows_b, LANES), 0) // n_p
                 == lax.broadcasted_iota(jnp.int32, (rows_b, LANES), 1) // n_h)
    bb_re = jnp.where(on_diag_b, _spread(k_re * b_re[...] - k_im * b_im[...], SUBLANES), 0.0)
    bb_im = jnp.where(on_diag_b, _spread(k_re * b_im[...] + k_im * b_re[...], SUBLANES), 0.0)
    wb_ref[0] = jnp.concatenate([bb_re, bb_im], axis=0).T
    v_re, v_im = [], []
    for s in range(SSM_BLK):
        pr, pi = pc[SSM_BLK - 1 - s]
        v_re.append(pr * bb_re - pi * bb_im)
        v_im.append(pr * bb_im + pi * bb_re)
    vt = jnp.concatenate([jnp.concatenate(v_re, axis=1), jnp.concatenate(v_im, axis=1)], axis=0)
    v_ref[0] = vt.T.astype(v_ref.dtype)

    pr_ = _powers(lamr_row[...], lami_row[...], dt_row[...])
    on_diag_c = (lax.broadcasted_iota(jnp.int32, (LANES, TILE_STATE), 0) // n_h
                 == lax.broadcasted_iota(jnp.int32, (LANES, TILE_STATE), 1) // n_p)
    cl = []
    for m in range(SSM_BLK + 1):
        pr, pi = pr_[m]
        cl.append((jnp.where(on_diag_c, _spread(c_re[...] * pr - c_im[...] * pi, SUBLANES), 0.0),
                   jnp.where(on_diag_c, _spread(c_re[...] * pi + c_im[...] * pr, SUBLANES), 0.0)))
    wc_ref[0] = jnp.concatenate([cl[0][0], -cl[0][1]], axis=1).T
    direct = [_dot_f32(cl[m][0], bb_re) - _dot_f32(cl[m][1], bb_im) for m in range(SSM_BLK)]
    zero = jnp.zeros((LANES, LANES), _F32)
    rt = jnp.concatenate(
        [jnp.concatenate([cl[i + 1][0], -cl[i + 1][1]]
                         + [direct[i - s] if s <= i else zero for s in range(SSM_BLK)], axis=1)
         for i in range(SSM_BLK)], axis=0)
    r_ref[0] = rt.T.astype(r_ref.dtype)


def _ssm_params(lam_re, lam_im, log_dt, b_re, b_im, c_re, c_im, d_skip):
    n_g, n_p, n_h = N_SSM_GROUPS, SSM_STATE, SSM_GROUP
    dt = jnp.exp(log_dt)
    rep = lambda a, k: jnp.repeat(a, k, axis=0)
    lam_rows = [rep(lam_re, n_h), rep(lam_im, n_h), rep(dt, n_h)[:, None]]
    lam_cols = [lam_re.reshape(-1, 1), lam_im.reshape(-1, 1), rep(dt, n_p)[:, None]]
    b2 = [b_re.reshape(n_g * n_p, n_h), b_im.reshape(n_g * n_p, n_h)]
    c2 = [c_re.reshape(n_g * n_h, n_p), c_im.reshape(n_g * n_h, n_p)]
    tile = lambda rows, w: pl.BlockSpec((rows, w), lambda k: (k, 0))
    out3 = lambda rows, w: pl.BlockSpec((1, rows, w), lambda k: (k, 0, 0))
    k_blk = SSM_BLK * LANES
    v, r, wb, wc = pl.pallas_call(
        _ssm_prep_kernel,
        grid=(N_LANE_TILES,),
        in_specs=[tile(LANES, n_p), tile(LANES, n_p), tile(LANES, 1),
                  tile(TILE_STATE, 1), tile(TILE_STATE, 1), tile(TILE_STATE, 1),
                  tile(TILE_STATE, n_h), tile(TILE_STATE, n_h), tile(LANES, n_p), tile(LANES, n_p)],
        out_specs=[out3(k_blk, 2 * TILE_STATE), out3(2 * TILE_STATE + k_blk, k_blk),
                   out3(LANES, 2 * TILE_STATE), out3(2 * TILE_STATE, LANES)],
        out_shape=[jax.ShapeDtypeStruct((N_LANE_TILES, k_blk, 2 * TILE_STATE), _BF16),
                   jax.ShapeDtypeStruct((N_LANE_TILES, 2 * TILE_STATE + k_blk, k_blk), _BF16),
                   jax.ShapeDtypeStruct((N_LANE_TILES, LANES, 2 * TILE_STATE), _F32),
                   jax.ShapeDtypeStruct((N_LANE_TILES, 2 * TILE_STATE, LANES), _F32)],
        compiler_params=pltpu.CompilerParams(
            dimension_semantics=("arbitrary",), vmem_limit_bytes=VMEM_LIMIT),
        name="ssm_prep",
    )(*lam_rows, *lam_cols, *b2, *c2)
    pw = _powers(lam_re, lam_im, dt[:, None])
    flat = lambda a: a.reshape(1, STATE_COLS)
    return (wb, wc, flat(pw[1][0]), flat(pw[1][1]), v, r, flat(pw[SSM_BLK][0]), flat(pw[SSM_BLK][1]),
            d_skip.reshape(1, D_SSM))


def _dispatch_plan(route_t, cnt):
    t_all = route_t.shape[1]
    codes = route_t[R_CODE1:R_CODE2 + 1].astype(jnp.int32).reshape(-1)
    per_pass = SC_WORKERS * DISPATCH_CHUNK
    codes = jnp.pad(codes, (0, -(2 * t_all) % per_pass))
    counts = cnt[0, :N_EXPERTS].astype(jnp.int32)
    zero = jnp.zeros((1,), jnp.int32)
    units = (counts + EXP_UNIT - 1) // EXP_UNIT
    unit_start = jnp.concatenate([zero, jnp.cumsum(units)])
    start_row = jnp.zeros((LANES,), jnp.int32).at[:N_EXPERTS].set(unit_start[:N_EXPERTS] * EXP_UNIT)
    pieces = (units + EXP_CLASSES - 1) // EXP_CLASSES
    piece_start = jnp.concatenate([zero, jnp.cumsum(pieces)])
    tm = EXP_UNIT * EXP_CLASSES
    max_units = (2 * t_all + N_EXPERTS * (EXP_UNIT - 1)) // EXP_UNIT
    max_pieces = (max_units + N_EXPERTS * (EXP_CLASSES - 1)) // EXP_CLASSES
    g = jnp.arange(max_pieces, dtype=jnp.int32)
    owner = ((g[:, None] >= piece_start[None, :-1]) & (g[:, None] < piece_start[None, 1:])).astype(jnp.int32)
    pick = lambda table: jnp.sum(owner * table[None, :], axis=1)
    first_unit = pick(unit_start[:-1]) + (g - pick(piece_start[:-1])) * EXP_CLASSES
    piece_row = first_unit * EXP_UNIT
    piece_cls = jnp.clip(pick(unit_start[1:]) - first_unit, 1, EXP_CLASSES)
    n_rows = (max_units * EXP_UNIT + tm - 1) // tm * tm + tm
    return codes, start_row, n_rows, piece_start, piece_row, piece_cls


def kernel(x_prompt, x_sample, state_ssm_re, state_ssm_im, norm1_g, w_in, lam_re, lam_im, log_dt, ssm_b_re, ssm_b_im, ssm_c_re, ssm_c_im, ssm_d, gmlp_norm_g, gmlp_w_s, gmlp_b_s, out_norm_ssm_g, out_norm_gmlp_g, w_out, norm2_g, w_router_group, b_router_group, w_router_expert, b_router_expert, w_gate, w_up, w_down, final_norm_g):
    n, l, d = x_prompt.shape
    ns = x_sample.shape[0]
    t_all = n * l + ns
    li = 0
    g1 = norm1_g[li].reshape(1, d)
    gn = gmlp_norm_g[li].reshape(1, D_GMLP)
    tril = jnp.tril(jnp.ones((CHUNK, CHUNK), dtype=bool))
    ws_tril = jnp.where(tril[None], gmlp_w_s[li], 0.0)
    bs = gmlp_b_s[li]
    gog = out_norm_gmlp_g[li].reshape(1, D_GMLP)
    gos = out_norm_ssm_g[li].reshape(1, D_SSM)
    wb, wc, lbr, lbi, v_blk, r_blk, lpr, lpi, dsk = _ssm_params(
        lam_re[li], lam_im[li], log_dt[li], ssm_b_re[li], ssm_b_im[li], ssm_c_re[li], ssm_c_im[li], ssm_d[li])
    g2 = norm2_g[li].reshape(1, d)
    pad = LANES - N_EXPERTS - N_EXPERT_GROUPS
    wr = jnp.concatenate([w_router_expert[li], w_router_group[li], jnp.zeros((d, pad), _F32)], axis=1)
    br = jnp.concatenate([b_router_expert[li], b_router_group[li], jnp.zeros((pad,), _F32)]).reshape(1, LANES)

    xa, sg, mixb = _front_prompt(x_prompt, g1, w_in[li], gn, ws_tril.astype(_BF16), bs.T, gog)
    mixa, hfin = _ssm_prompt(xa, sg, v_blk, r_blk, lpr, lpi, dsk, gos)
    w00 = jnp.repeat(ws_tril[:, 0, 0], GMLP_HEAD).reshape(1, D_GMLP)
    b0 = jnp.repeat(bs[:, 0], GMLP_HEAD).reshape(1, D_GMLP)
    mix_s, hr_s, hi_s, vrow = _front_sample(
        x_sample.reshape(ns, d), g1, w_in[li], gn, w00, b0, gog, wb, wc, lbr, lbi, dsk, gos,
        state_ssm_re[li].reshape(ns, STATE_COLS), state_ssm_im[li].reshape(ns, STATE_COLS))

    x1, xn, route, route_t, cnt = _mixer_out(x_prompt, mixa, mixb, x_sample.reshape(ns, d), mix_s,
                                             w_out[li], g2, wr, br)
    codes, start_row, n_rows, piece_start, piece_row, piece_cls = _dispatch_plan(route_t, cnt)
    xs, dest = _sc_dispatch(xn, codes, start_row, n_rows, DISPATCH_CHUNK)
    ys = _experts(piece_start, piece_row, piece_cls, n_rows, xs, w_gate[li], w_up[li], w_down[li])
    yab = _sc_combine(ys, dest, 2 * t_all, COMBINE_CHUNK).reshape(2, t_all, d // 2)
    y_p, y_s = _final(x1, yab, route, final_norm_g.reshape(1, d), n * l, ns)

    hf = hfin.reshape(n, N_LANE_TILES, 2, 8, SSM_STATE)
    re_p = hf[:, :, 0].reshape(1, n, N_SSM_GROUPS, SSM_STATE)
    im_p = hf[:, :, 1].reshape(1, n, N_SSM_GROUPS, SSM_STATE)
    re_s = hr_s.reshape(1, ns, N_SSM_GROUPS, SSM_STATE)
    im_s = hi_s.reshape(1, ns, N_SSM_GROUPS, SSM_STATE)
    return (y_p.reshape(n, l, d), y_s.reshape(ns, 1, d), re_p, im_p, re_s, im_s,
            vrow.reshape(1, ns, 1, D_GMLP))
```

```python
import math

import jax
import jax.numpy as jnp
from jax import lax
from jax.experimental import pallas as pl
from jax.experimental.pallas import tpu as pltpu
from jax.experimental.pallas import tpu_sc as plsc

D_MODEL = 1024
D_SSM = 512
D_GMLP = 512
SSM_GROUP = 16
N_SSM_GROUPS = 32
SSM_STATE = 64
CHUNK = 128
N_GMLP_HEADS = 4
GMLP_HEAD = 128
N_EXPERT_GROUPS = 4
EXPERTS_PER_GROUP = 8
N_EXPERTS = 32
D_EXPERT = 512
D_IN = 2048
EPS = 1e-6

LANES = 128
SUBLANES = 8
N_LANE_TILES = D_SSM // LANES
STATE_COLS = N_SSM_GROUPS * SSM_STATE
TILE_STATE = STATE_COLS // N_LANE_TILES
VMEM_LIMIT = 56 * 1024 * 1024

SC_CORES = 2
SC_SUBCORES = 16
SC_LANES = 16
SC_WORKERS = SC_CORES * SC_SUBCORES

FRONT_TL = 512
SSM_LC = 256
SSM_BLK = 4
COEF_LB_RE, COEF_LB_IM, COEF_LBLK_RE, COEF_LBLK_IM = 0, 1, 2, 3
TOK_TM = 512
FINAL_TM = 512
EXP_UNIT = 128
EXP_CLASSES = 4
W_SPLIT = 4
DISPATCH_CHUNK = 80
COMBINE_CHUNK = 24

R_E1, R_E2, R_W1, R_W2, R_RANK1, R_RANK2, R_CODE1, R_CODE2 = 0, 1, 2, 3, 4, 5, 6, 7
CODE_BITS = 16
CODE_SHIFT = float(1 << CODE_BITS)

_INV_SQRT2 = 1.0 / math.sqrt(2.0)
_BF16 = jnp.bfloat16
_F32 = jnp.float32
_U32 = jnp.uint32


def _gelu(x):
    return 0.5 * x * (1.0 + lax.erf(x * _INV_SQRT2))


def _rms(x, g):
    return x * lax.rsqrt(jnp.mean(x * x, axis=-1, keepdims=True) + EPS) * g


def _dot(a, b):
    return jnp.dot(a, b, preferred_element_type=_F32)


def _dot_f32(a, b):
    return jnp.dot(a, b, preferred_element_type=_F32, precision=lax.Precision.HIGHEST)


def _pack_bf16_pair(x):
    w = x.shape[1] // 2
    hi = lax.bitcast_convert_type(x[:, :w].astype(_BF16).astype(_F32), _U32)
    lo = lax.bitcast_convert_type(x[:, w:].astype(_BF16).astype(_F32), _U32)
    return hi | (lo >> 16)


def _unpack_bf16_pair(p):
    hi = lax.bitcast_convert_type(p & jnp.uint32(0xFFFF0000), _F32)
    lo = lax.bitcast_convert_type(p << 16, _F32)
    return jnp.concatenate([hi, lo], axis=-1)


def _head_norm_gelu(vb, gn):
    v = _gelu(vb)
    parts = []
    for h in range(N_GMLP_HEADS):
        vh = v[:, h * GMLP_HEAD:(h + 1) * GMLP_HEAD]
        parts.append(vh * lax.rsqrt(jnp.mean(vh * vh, axis=-1, keepdims=True) + EPS))
    return jnp.concatenate(parts, axis=-1) * gn


def _front_prompt_kernel(x_ref, g1_ref, win_ref, gn_ref, ws_ref, bs_ref, gog_ref,
                         xa_ref, sg_ref, mixb_ref, win_bf):
    @pl.when((pl.program_id(0) == 0) & (pl.program_id(1) == 0))
    def _():
        win_bf[...] = win_ref[...].astype(_BF16)

    x = x_ref[0]
    hn = _rms(x, g1_ref[...]).astype(_BF16)
    z = _dot(hn, win_bf[...])
    xa_ref[0] = z[:, :D_SSM]
    sg_ref[0] = jax.nn.sigmoid(z[:, D_SSM:2 * D_SSM])
    ub = _gelu(z[:, 2 * D_SSM:2 * D_SSM + D_GMLP])
    vbn = _head_norm_gelu(z[:, 2 * D_SSM + D_GMLP:], gn_ref[...]).astype(_BF16)
    tl = x.shape[0]
    rows = []
    for c in range(tl // CHUNK):
        heads = []
        for h in range(N_GMLP_HEADS):
            vh = vbn[c * CHUNK:(c + 1) * CHUNK, h * GMLP_HEAD:(h + 1) * GMLP_HEAD]
            heads.append(_dot(ws_ref[h], vh) + bs_ref[:, h:h + 1])
        rows.append(jnp.concatenate(heads, axis=-1))
    s = jnp.concatenate(rows, axis=0)
    mixb_ref[0] = _rms(ub * s, gog_ref[...]).astype(_BF16)


def _front_prompt(x, g1, win, gn, ws_tril_bf, bs_t, gog):
    n, l, d = x.shape
    tl = FRONT_TL
    grid = (n, l // tl)
    const = lambda *shape: pl.BlockSpec(shape, lambda b, i: (0,) * len(shape))
    seq = lambda w: pl.BlockSpec((1, tl, w), lambda b, i: (b, i, 0))
    return pl.pallas_call(
        _front_prompt_kernel,
        grid=grid,
        in_specs=[seq(d), const(1, d), const(d, D_IN), const(1, D_GMLP),
                  const(N_GMLP_HEADS, CHUNK, CHUNK), const(CHUNK, N_GMLP_HEADS), const(1, D_GMLP)],
        out_specs=[seq(D_SSM), seq(D_SSM), seq(D_GMLP)],
        out_shape=[jax.ShapeDtypeStruct((n, l, D_SSM), _F32),
                   jax.ShapeDtypeStruct((n, l, D_SSM), _F32),
                   jax.ShapeDtypeStruct((n, l, D_GMLP), _BF16)],
        scratch_shapes=[pltpu.VMEM((d, D_IN), _BF16)],
        compiler_params=pltpu.CompilerParams(
            dimension_semantics=("arbitrary", "arbitrary"), vmem_limit_bytes=VMEM_LIMIT),
        name="front_prompt",
    )(x, g1, win, gn, ws_tril_bf, bs_t, gog)


def _ssm_prompt_kernel(xa_ref, sg_ref, v_ref, r_ref, coef_ref, dsk_ref, gos_ref,
                       mixa_ref, hfin_ref, s_ref, st_ref):
    lc = xa_ref.shape[1]
    nblk = lc // SSM_BLK
    rows = nblk * SUBLANES

    @pl.when(pl.program_id(0) == 0)
    def _():
        st_ref[...] = jnp.zeros_like(st_ref)

    def by_position(ref):
        t = pltpu.einshape("btc->tbc", ref[...]).reshape(nblk, SSM_BLK, SUBLANES, D_SSM)
        return [t[:, i].reshape(rows, D_SSM) for i in range(SSM_BLK)]

    xs = by_position(xa_ref)
    xs_bf = [x.astype(_BF16) for x in xs]
    xk = [jnp.concatenate([x[:, k * LANES:(k + 1) * LANES] for x in xs_bf], axis=-1)
          for k in range(N_LANE_TILES)]
    for k in range(N_LANE_TILES):
        s_ref[:, 2 * TILE_STATE * k:2 * TILE_STATE * (k + 1)] = _dot(xk[k], v_ref[k])

    for kk in range(0, N_LANE_TILES, 2):
        tiles = (kk, kk + 1)
        cols = [(2 * TILE_STATE * k, 2 * TILE_STATE * k + TILE_STATE) for k in tiles]
        lbs = [tuple(jnp.broadcast_to(coef_ref[row:row + 1, k * TILE_STATE:(k + 1) * TILE_STATE],
                                      (SUBLANES, TILE_STATE)) for row in (COEF_LBLK_RE, COEF_LBLK_IM))
               for k in tiles]

        def body(j, carry, cols=cols, lbs=lbs):
            r0 = pl.multiple_of(j * SUBLANES, SUBLANES)
            out = []
            for q, ((c_re, c_im), (lr, li)) in enumerate(zip(cols, lbs)):
                hr, hi = carry[2 * q], carry[2 * q + 1]
                sr = s_ref[pl.ds(r0, SUBLANES), c_re:c_re + TILE_STATE]
                si = s_ref[pl.ds(r0, SUBLANES), c_im:c_im + TILE_STATE]
                s_ref[pl.ds(r0, SUBLANES), c_re:c_re + TILE_STATE] = hr
                s_ref[pl.ds(r0, SUBLANES), c_im:c_im + TILE_STATE] = hi
                out += [lr * hr - li * hi + sr, lr * hi + li * hr + si]
            return tuple(out)

        init = tuple(st_ref[:, c:c + TILE_STATE] for c_pair in cols for c in c_pair)
        fin = lax.fori_loop(0, nblk, body, init, unroll=2)
        for q, (c_re, c_im) in enumerate(cols):
            st_ref[:, c_re:c_re + TILE_STATE] = fin[2 * q]
            st_ref[:, c_im:c_im + TILE_STATE] = fin[2 * q + 1]

    yk = []
    for k in range(N_LANE_TILES):
        h_in = s_ref[:, 2 * TILE_STATE * k:2 * TILE_STATE * (k + 1)].astype(_BF16)
        yk.append(_dot(jnp.concatenate([h_in, xk[k]], axis=-1), r_ref[k]))
    sgs = by_position(sg_ref)
    outs = []
    for i in range(SSM_BLK):
        y = jnp.concatenate([y_k[:, i * LANES:(i + 1) * LANES] for y_k in yk], axis=-1) + dsk_ref[...] * xs[i]
        outs.append(_rms(_gelu(y) * sgs[i], gos_ref[...]).reshape(nblk, SUBLANES, D_SSM))
    mixa = jnp.stack(outs, axis=1).reshape(lc, SUBLANES, D_SSM)
    mixa_ref[...] = pltpu.einshape("tbc->btc", mixa).astype(_BF16)
    hfin_ref[...] = st_ref[...]


def _ssm_prompt(xa, sg, v, r, coef, dsk, gos):
    n, l, _ = xa.shape
    lc = SSM_LC
    const = lambda *shape: pl.BlockSpec(shape, lambda i: (0,) * len(shape))
    seq_spec = pl.BlockSpec((n, lc, D_SSM), lambda i: (0, i, 0))
    return pl.pallas_call(
        _ssm_prompt_kernel,
        grid=(l // lc,),
        in_specs=[seq_spec, seq_spec, const(*v.shape), const(*r.shape),
                  const(*coef.shape), const(1, D_SSM), const(1, D_SSM)],
        out_specs=[seq_spec, const(n, 2 * STATE_COLS)],
        out_shape=[jax.ShapeDtypeStruct((n, l, D_SSM), _BF16),
                   jax.ShapeDtypeStruct((n, 2 * STATE_COLS), _F32)],
        scratch_shapes=[pltpu.VMEM((lc // SSM_BLK * n, 2 * STATE_COLS), _F32),
                        pltpu.VMEM((n, 2 * STATE_COLS), _F32)],
        compiler_params=pltpu.CompilerParams(
            dimension_semantics=("arbitrary",), vmem_limit_bytes=VMEM_LIMIT),
        name="ssm_prompt",
    )(xa, sg, v, r, coef, dsk, gos)


def _front_sample_kernel(x_ref, g1_ref, win_ref, gn_ref, w00_ref, b0_ref, gog_ref,
                         wb_ref, wc_ref, coef_ref, dsk_ref, gos_ref, h0r_ref, h0i_ref,
                         mix_ref, hr_ref, hi_ref, vrow_ref):
    x = x_ref[...]
    hn = _rms(x, g1_ref[...])
    z = _dot_f32(hn, win_ref[...])
    xa = z[:, :D_SSM]
    ys = []
    for k in range(N_LANE_TILES):
        bu = _dot_f32(xa[:, k * LANES:(k + 1) * LANES], wb_ref[k])
        sl = slice(k * TILE_STATE, (k + 1) * TILE_STATE)
        lr, li = coef_ref[COEF_LB_RE:COEF_LB_RE + 1, sl], coef_ref[COEF_LB_IM:COEF_LB_IM + 1, sl]
        h0r, h0i = h0r_ref[:, sl], h0i_ref[:, sl]
        nr = lr * h0r - li * h0i + bu[:, :TILE_STATE]
        ni = lr * h0i + li * h0r + bu[:, TILE_STATE:]
        hr_ref[:, sl] = nr
        hi_ref[:, sl] = ni
        ys.append(_dot_f32(jnp.concatenate([nr, ni], axis=-1), wc_ref[k]))
    y = jnp.concatenate(ys, axis=-1) + dsk_ref[...] * xa
    ya = _gelu(y) * jax.nn.sigmoid(z[:, D_SSM:2 * D_SSM])
    mix_ref[:, :D_SSM] = _rms(ya, gos_ref[...])
    ub = _gelu(z[:, 2 * D_SSM:2 * D_SSM + D_GMLP])
    vbn = _head_norm_gelu(z[:, 2 * D_SSM + D_GMLP:], gn_ref[...])
    vrow_ref[...] = vbn
    s = w00_ref[...] * vbn + b0_ref[...]
    mix_ref[:, D_SSM:] = _rms(ub * s, gog_ref[...])


def _front_sample(x, g1, win, gn, w00, b0, gog, wb, wc, coef, dsk, gos, h0r, h0i):
    n = x.shape[0]
    vmem = pl.BlockSpec(memory_space=pltpu.VMEM)
    return pl.pallas_call(
        _front_sample_kernel,
        in_specs=[vmem] * 14,
        out_specs=[vmem] * 4,
        out_shape=[jax.ShapeDtypeStruct((n, D_MODEL), _F32),
                   jax.ShapeDtypeStruct((n, STATE_COLS), _F32),
                   jax.ShapeDtypeStruct((n, STATE_COLS), _F32),
                   jax.ShapeDtypeStruct((n, D_GMLP), _F32)],
        compiler_params=pltpu.CompilerParams(vmem_limit_bytes=VMEM_LIMIT),
        name="front_sample",
    )(x, g1, win, gn, w00, b0, gog, wb, wc, coef, dsk, gos, h0r, h0i)


def _route(logits, base):
    tm = logits.shape[0]
    lane = lax.broadcasted_iota(jnp.int32, logits.shape, 1).astype(_F32)
    neg = jnp.float32(-jnp.inf)
    big = jnp.float32(LANES)
    is_g = (lane >= N_EXPERTS) & (lane < N_EXPERTS + N_EXPERT_GROUPS)
    gl = jnp.where(is_g, logits, neg)
    gmax = jnp.max(gl, axis=-1, keepdims=True)
    gi = jnp.min(jnp.where(is_g & (logits == gmax), lane, big), axis=-1, keepdims=True) - N_EXPERTS
    p_top = 1.0 / jnp.sum(jnp.where(is_g, jnp.exp(gl - gmax), 0.0), axis=-1, keepdims=True)
    lo = gi * EXPERTS_PER_GROUP
    in_grp = (lane >= lo) & (lane < lo + EXPERTS_PER_GROUP)
    m1 = jnp.max(jnp.where(in_grp, logits, neg), axis=-1, keepdims=True)
    i1 = jnp.min(jnp.where(in_grp & (logits == m1), lane, big), axis=-1, keepdims=True)
    rest = in_grp & (lane != i1)
    m2 = jnp.max(jnp.where(rest, logits, neg), axis=-1, keepdims=True)
    i2 = jnp.min(jnp.where(rest & (logits == m2), lane, big), axis=-1, keepdims=True)
    e2 = jnp.exp(m2 - m1)
    w1 = p_top / (1.0 + e2)
    w2 = p_top * e2 / (1.0 + e2)
    sel1 = lane == i1
    sel2 = lane == i2
    hits = jnp.where(sel1 | sel2, 1.0, 0.0)
    r_id = lax.broadcasted_iota(jnp.int32, (tm, tm), 0)
    c_id = lax.broadcasted_iota(jnp.int32, (tm, tm), 1)
    ltri = jnp.where(c_id < r_id, 1.0, 0.0).astype(_BF16)
    before = _dot(ltri, hits.astype(_BF16)) + base
    rank1 = jnp.sum(jnp.where(sel1, before, 0.0), axis=-1, keepdims=True)
    rank2 = jnp.sum(jnp.where(sel2, before, 0.0), axis=-1, keepdims=True)
    out = jnp.where(lane == R_E1, i1, 0.0)
    out = jnp.where(lane == R_E2, i2, out)
    out = jnp.where(lane == R_W1, w1, out)
    out = jnp.where(lane == R_W2, w2, out)
    out = jnp.where(lane == R_RANK1, rank1, out)
    out = jnp.where(lane == R_RANK2, rank2, out)
    out = jnp.where(lane == R_CODE1, i1 * CODE_SHIFT + rank1, out)
    out = jnp.where(lane == R_CODE2, i2 * CODE_SHIFT + rank2, out)
    return out, base + jnp.sum(hits, axis=0, keepdims=True)


def _mixer_out_prompt_kernel(x_ref, mixa_ref, mixb_ref, wo_ref, g2_ref, wr_ref, br_ref,
                             x1_ref, xn_ref, route_ref, route_t_ref, cnt_ref, base_ref):
    @pl.when((pl.program_id(0) == 0) & (pl.program_id(1) == 0))
    def _():
        base_ref[...] = jnp.zeros_like(base_ref)

    x1 = x_ref[0] + _dot(mixa_ref[0], wo_ref[:D_SSM, :]) + _dot(mixb_ref[0], wo_ref[D_SSM:, :])
    xn = _rms(x1, g2_ref[...])
    logits = _dot(xn.astype(_BF16), wr_ref[...]) + br_ref[...]
    route, base = _route(logits, base_ref[...])
    x1_ref[...] = x1
    xn_ref[...] = _pack_bf16_pair(xn)
    route_ref[...] = route
    route_t_ref[...] = route.T[:SUBLANES, :]
    base_ref[...] = base
    cnt_ref[...] = base


def _mixer_out_sample_kernel(x_ref, mix_ref, wo_ref, g2_ref, wr_ref, br_ref, cnt_in_ref,
                             x1_in, xn_in, route_in, route_t_in,
                             x1_ref, xn_ref, route_ref, route_t_ref, cnt_ref):
    del x1_in, xn_in, route_in, route_t_in
    x1 = (x_ref[...] + _dot_f32(mix_ref[:, :D_SSM], wo_ref[:D_SSM, :])
          + _dot_f32(mix_ref[:, D_SSM:], wo_ref[D_SSM:, :]))
    xn = _rms(x1, g2_ref[...])
    logits = _dot_f32(xn, wr_ref[...]) + br_ref[...]
    route, base = _route(logits, cnt_in_ref[...])
    x1_ref[...] = x1
    xn_ref[...] = _pack_bf16_pair(xn)
    route_ref[...] = route
    route_t_ref[...] = route.T[:SUBLANES, :]
    cnt_ref[...] = base


def _mixer_out(x_p, mixa, mixb, x_s, mix_s, wo, g2, wr, br):
    n, l, d = x_p.shape
    ns = x_s.shape[0]
    t_all = n * l + ns
    tm = TOK_TM
    per_seq = l // tm
    const = lambda *shape: pl.BlockSpec(shape, lambda b, i: (0,) * len(shape))
    seq = lambda w: pl.BlockSpec((1, tm, w), lambda b, i: (b, i, 0))
    tok = lambda w: pl.BlockSpec((tm, w), lambda b, i: (b * per_seq + i, 0))
    tok_shapes = [jax.ShapeDtypeStruct((t_all, d), _F32),
                  jax.ShapeDtypeStruct((t_all, d // 2), _U32),
                  jax.ShapeDtypeStruct((t_all, LANES), _F32),
                  jax.ShapeDtypeStruct((SUBLANES, t_all), _F32)]
    cnt_shape = jax.ShapeDtypeStruct((1, LANES), _F32)
    x1, xn, route, route_t, cnt = pl.pallas_call(
        _mixer_out_prompt_kernel,
        grid=(n, per_seq),
        in_specs=[seq(d), seq(D_SSM), seq(D_GMLP),
                  const(d, d), const(1, d), const(d, LANES), const(1, LANES)],
        out_specs=[tok(d), tok(d // 2), tok(LANES),
                   pl.BlockSpec((SUBLANES, tm), lambda b, i: (0, b * per_seq + i)), const(1, LANES)],
        out_shape=tok_shapes + [cnt_shape],
        scratch_shapes=[pltpu.VMEM((1, LANES), _F32)],
        compiler_params=pltpu.CompilerParams(
            dimension_semantics=("arbitrary", "arbitrary"), vmem_limit_bytes=VMEM_LIMIT),
        name="mixer_out_prompt",
    )(x_p, mixa, mixb, wo.astype(_BF16), g2, wr.astype(_BF16), br)
    tail = (n * l) // ns
    c1 = lambda *shape: pl.BlockSpec(shape, lambda i: (0,) * len(shape))
    anyspec = pl.BlockSpec(memory_space=pl.ANY)
    tail_spec = lambda w: pl.BlockSpec((ns, w), lambda i: (tail, 0))
    return pl.pallas_call(
        _mixer_out_sample_kernel,
        grid=(1,),
        in_specs=[c1(ns, d), c1(ns, d), c1(d, d), c1(1, d), c1(d, LANES), c1(1, LANES), c1(1, LANES),
                  anyspec, anyspec, anyspec, anyspec],
        out_specs=[tail_spec(d), tail_spec(d // 2), tail_spec(LANES),
                   pl.BlockSpec((SUBLANES, ns), lambda i: (0, tail)), c1(1, LANES)],
        out_shape=tok_shapes + [cnt_shape],
        input_output_aliases={7: 0, 8: 1, 9: 2, 10: 3},
        compiler_params=pltpu.CompilerParams(
            dimension_semantics=("arbitrary",), vmem_limit_bytes=VMEM_LIMIT),
        name="mixer_out_sample",
    )(x_s, mix_s, wo, g2, wr, br, cnt, x1, xn, route, route_t)


def _sc_stream(n_chunks, gather, write):
    gather(0).start()
    for j in range(n_chunks):
        if j + 1 < n_chunks:
            if j >= 1:
                write(j - 1).wait()
            gather(j + 1).start()
        gather(j).wait()
        write(j).start()
    if n_chunks >= 2:
        write(n_chunks - 2).wait()
    write(n_chunks - 1).wait()


def _sc_mesh():
    return plsc.VectorSubcoreMesh(core_axis_name="c", subcore_axis_name="s",
                                  num_cores=SC_CORES, num_subcores=SC_SUBCORES)


def _sc_buffers(chunk, w, dtype):
    return [pltpu.VMEM((chunk, w), dtype), pltpu.VMEM((chunk, w), dtype)] + [pltpu.SemaphoreType.DMA] * 4


def _sc_combine(table, idx, n_out, chunk):
    w = table.shape[1]
    rows_w = n_out // SC_WORKERS
    n_chunks = rows_w // chunk
    assert rows_w * SC_WORKERS == n_out and n_chunks * chunk == rows_w and rows_w % SUBLANES == 0

    def body(table_hbm, idx_hbm, out_hbm, idx_v, buf0, buf1, g0, g1, w0, w1):
        wid = lax.axis_index("s") * SC_CORES + lax.axis_index("c")
        base = pl.multiple_of(wid * rows_w, SUBLANES)
        pltpu.sync_copy(idx_hbm.at[pl.ds(base, rows_w)], idx_v)
        bufs, gsems, wsems = (buf0, buf1), (g0, g1), (w0, w1)

        def gather(j):
            return pltpu.make_async_copy(table_hbm.at[idx_v.at[pl.ds(j * chunk, chunk)]], bufs[j % 2], gsems[j % 2])

        def write(j):
            return pltpu.make_async_copy(bufs[j % 2], out_hbm.at[pl.ds(base + j * chunk, chunk)], wsems[j % 2])

        _sc_stream(n_chunks, gather, write)

    return pl.kernel(
        body,
        out_type=jax.ShapeDtypeStruct((n_out, w), table.dtype),
        mesh=_sc_mesh(),
        scratch_types=[pltpu.VMEM((rows_w,), jnp.int32)] + _sc_buffers(chunk, w, table.dtype),
        compiler_params=pltpu.CompilerParams(use_tc_tiling_on_sc=True),
        name="sc_combine",
    )(table, idx)


def _sc_dispatch(table, codes, start_row, n_out, chunk):
    t_all, w = table.shape
    n_pad = codes.shape[0]
    n_ent = 2 * t_all
    ent_w = n_pad // SC_WORKERS
    n_chunks = ent_w // chunk
    per_chunk = chunk // SC_LANES
    trash = n_out - (n_pad - n_ent)
    assert ent_w * SC_WORKERS == n_pad and n_chunks * chunk == ent_w
    assert per_chunk * SC_LANES == chunk and chunk <= LANES and n_pad - n_ent <= t_all

    def body(table_hbm, code_hbm, start_hbm, out_hbm, dest_hbm,
             code_v, dest_v, tok_v, dst_v, start_v, buf0, buf1, g0, g1, w0, w1):
        wid = lax.axis_index("s") * SC_CORES + lax.axis_index("c")
        ebase = pl.multiple_of(wid * ent_w, SUBLANES)
        pltpu.sync_copy(code_hbm.at[pl.ds(ebase, ent_w)], code_v)
        pltpu.sync_copy(start_hbm, start_v)
        lane = lax.iota(jnp.int32, SC_LANES)
        for j in range(n_chunks):
            for c in range(per_chunk):
                off = j * chunk + c * SC_LANES
                ent = ebase + off + lane
                code = code_v[pl.ds(off, SC_LANES)]
                d = plsc.load_gather(start_v, [code >> CODE_BITS]) + (code & ((1 << CODE_BITS) - 1))
                d = jnp.where(ent >= n_ent, trash + (ent - n_ent), d)
                tok = jnp.where(ent >= t_all, ent - t_all, ent)
                tok = jnp.where(tok >= t_all, tok - t_all, tok)
                dest_v[pl.ds(off, SC_LANES)] = d
                dst_v[j, pl.ds(c * SC_LANES, SC_LANES)] = d
                tok_v[j, pl.ds(c * SC_LANES, SC_LANES)] = tok
        pltpu.sync_copy(dest_v, dest_hbm.at[pl.ds(ebase, ent_w)])
        bufs, gsems, wsems = (buf0, buf1), (g0, g1), (w0, w1)

        def gather(j):
            return pltpu.make_async_copy(table_hbm.at[tok_v.at[j]], bufs[j % 2], gsems[j % 2])

        def scatter(j):
            return pltpu.make_async_copy(bufs[j % 2], out_hbm.at[dst_v.at[j]], wsems[j % 2])

        _sc_stream(n_chunks, gather, scatter)

    return pl.kernel(
        body,
        out_type=(jax.ShapeDtypeStruct((n_out, w), table.dtype), jax.ShapeDtypeStruct((n_pad,), jnp.int32)),
        mesh=_sc_mesh(),
        scratch_types=([pltpu.VMEM((ent_w,), jnp.int32), pltpu.VMEM((ent_w,), jnp.int32),
                        pltpu.VMEM((n_chunks, chunk), jnp.int32), pltpu.VMEM((n_chunks, chunk), jnp.int32),
                        pltpu.VMEM((LANES,), jnp.int32)] + _sc_buffers(chunk, w, table.dtype)),
        compiler_params=pltpu.CompilerParams(use_tc_tiling_on_sc=True, needs_layout_passes=False),
        name="sc_dispatch",
    )(table, codes, start_row)


def _experts_kernel(piece_start_ref, piece_row_ref, piece_cls_ref, *refs):
    n_w = 3 * W_SPLIT
    wg_refs, wu_refs, wd_refs = refs[:W_SPLIT], refs[W_SPLIT:2 * W_SPLIT], refs[2 * W_SPLIT:n_w]
    xs_hbm, ys_hbm, wg_bf, wu_bf, wd_bf, xbuf, ybuf, xsem, ysem = refs[n_w:]
    e = pl.program_id(0)
    g0 = piece_start_ref[e]
    n_here = piece_start_ref[e + 1] - g0
    n_total = piece_start_ref[N_EXPERTS]

    def per_class(g, fn):
        cls = piece_cls_ref[g]
        row = pl.multiple_of(piece_row_ref[g], EXP_UNIT)
        for c in range(1, EXP_CLASSES + 1):
            pl.when(cls == c)(lambda c=c: fn(c * EXP_UNIT, row))

    def x_copy(slot, rows, row):
        return pltpu.make_async_copy(xs_hbm.at[pl.ds(row, rows)], xbuf.at[slot, pl.ds(0, rows)], xsem.at[slot])

    def y_copy(slot, rows, row):
        return pltpu.make_async_copy(ybuf.at[slot, pl.ds(0, rows)], ys_hbm.at[pl.ds(row, rows)], ysem.at[slot])

    @pl.when((e == 0) & (n_total > 0))
    def _():
        per_class(0, lambda rows, row: x_copy(0, rows, row).start())

    for dst, chunks in ((wg_bf, wg_refs), (wu_bf, wu_refs), (wd_bf, wd_refs)):
        rows = dst.shape[0] // W_SPLIT
        for q, src in enumerate(chunks):
            dst[q * rows:(q + 1) * rows, :] = src[0, 0].astype(_BF16)

    def piece(j, carry):
        g = g0 + j
        slot = lax.rem(g, 2)
        per_class(g, lambda rows, row: x_copy(slot, rows, row).wait())

        @pl.when(g + 1 < n_total)
        def _():
            per_class(g + 1, lambda rows, row: x_copy(1 - slot, rows, row).start())

        @pl.when(g >= 2)
        def _():
            per_class(g - 2, lambda rows, row: y_copy(slot, rows, row).wait())

        def compute(rows, row):
            x = _unpack_bf16_pair(xbuf[slot, pl.ds(0, rows)]).astype(_BF16)
            a = _dot(x, wg_bf[...])
            u = _dot(x, wu_bf[...])
            h = (a * jax.nn.sigmoid(a) * u).astype(_BF16)
            ybuf[slot, pl.ds(0, rows)] = _pack_bf16_pair(_dot(h, wd_bf[...]))
            y_copy(slot, rows, row).start()

        per_class(g, compute)
        return carry

    lax.fori_loop(0, n_here, piece, 0)

    @pl.when(e == N_EXPERTS - 1)
    def _():
        @pl.when(n_total >= 2)
        def _():
            per_class(n_total - 2, lambda rows, row: y_copy(lax.rem(n_total, 2), rows, row).wait())

        @pl.when(n_total >= 1)
        def _():
            per_class(n_total - 1, lambda rows, row: y_copy(lax.rem(n_total - 1, 2), rows, row).wait())


def _experts(piece_start, piece_row, piece_cls, n_rows, xs, w_gate, w_up, w_down):
    dh = xs.shape[1]
    d = 2 * dh
    tm = EXP_UNIT * EXP_CLASSES
    anyspec = pl.BlockSpec(memory_space=pl.ANY)

    def chunk_specs(rows, cols):
        return [pl.BlockSpec((1, 1, rows // W_SPLIT, cols), lambda e, ps, pr, pc, q=q: (e, q, 0, 0))
                for q in range(W_SPLIT)]

    split = lambda w: w.reshape(w.shape[0], W_SPLIT, w.shape[1] // W_SPLIT, w.shape[2])
    grid_spec = pltpu.PrefetchScalarGridSpec(
        num_scalar_prefetch=3,
        grid=(N_EXPERTS,),
        in_specs=(chunk_specs(d, D_EXPERT) + chunk_specs(d, D_EXPERT) + chunk_specs(D_EXPERT, d) + [anyspec]),
        out_specs=anyspec,
        scratch_shapes=[pltpu.VMEM((d, D_EXPERT), _BF16), pltpu.VMEM((d, D_EXPERT), _BF16),
                        pltpu.VMEM((D_EXPERT, d), _BF16),
                        pltpu.VMEM((2, tm, dh), _U32), pltpu.VMEM((2, tm, dh), _U32),
                        pltpu.SemaphoreType.DMA((2,)), pltpu.SemaphoreType.DMA((2,))],
    )
    return pl.pallas_call(
        _experts_kernel,
        grid_spec=grid_spec,
        out_shape=jax.ShapeDtypeStruct((n_rows, dh), _U32),
        compiler_params=pltpu.CompilerParams(
            dimension_semantics=("arbitrary",), vmem_limit_bytes=VMEM_LIMIT),
        name="experts",
    )(piece_start, piece_row, piece_cls, *([split(w_gate)] * W_SPLIT), *([split(w_up)] * W_SPLIT),
      *([split(w_down)] * W_SPLIT), xs)


def _final_kernel(x1_ref, ya_ref, yb_ref, route_ref, gf_ref, y_ref):
    route = route_ref[...]
    x2 = (x1_ref[...] + route[:, R_W1:R_W1 + 1] * _unpack_bf16_pair(ya_ref[...])
          + route[:, R_W2:R_W2 + 1] * _unpack_bf16_pair(yb_ref[...]))
    y_ref[...] = _rms(x2, gf_ref[...])


def _final(x1, yab, route, gf, n_prompt, n_sample):
    d = x1.shape[1]

    def call(tm, first_block, n_rows, name):
        tok = lambda w: pl.BlockSpec((tm, w), lambda i: (first_block + i, 0))
        sel = lambda k: pl.BlockSpec((None, tm, d // 2), lambda i: (k, first_block + i, 0))
        return pl.pallas_call(
            _final_kernel,
            grid=(n_rows // tm,),
            in_specs=[tok(d), sel(0), sel(1), tok(LANES), pl.BlockSpec((1, d), lambda i: (0, 0))],
            out_specs=pl.BlockSpec((tm, d), lambda i: (i, 0)),
            out_shape=jax.ShapeDtypeStruct((n_rows, d), _F32),
            compiler_params=pltpu.CompilerParams(
                dimension_semantics=("arbitrary",), vmem_limit_bytes=VMEM_LIMIT),
            name=name,
        )(x1, yab, yab, route, gf)

    return (call(FINAL_TM, 0, n_prompt, "final_prompt"),
            call(n_sample, n_prompt // n_sample, n_sample, "final_sample"))


def _powers(lam_re, lam_im, dt):
    out = []
    for m in range(SSM_BLK + 1):
        mag = jnp.exp(m * lam_re * dt)
        ang = m * lam_im * dt
        out.append((mag * jnp.cos(ang), mag * jnp.sin(ang)))
    return out


def _spread(x, copies):
    w = x.shape[1]
    src = lax.broadcasted_iota(jnp.int32, (w, w * copies), 0)
    dst = lax.broadcasted_iota(jnp.int32, (w, w * copies), 1)
    return _dot_f32(x, jnp.where(dst % w == src, 1.0, 0.0))


def _ssm_prep_kernel(lam_ref, b_re, b_im, c_re, c_im, v_ref, r_ref, wb_ref, wc_ref, coef_ref):
    n_p, n_h = SSM_STATE, SSM_GROUP
    lr, li, dt = lam_ref[0:1, :], lam_ref[1:2, :], lam_ref[2:3, :]
    pw = _powers(lr, li, dt)
    den = lr * lr + li * li
    nr, ni = pw[1][0] - 1.0, pw[1][1]
    k_re = (nr * lr + ni * li) / den
    k_im = (ni * lr - nr * li) / den
    coef_ref[...] = jnp.concatenate(
        [pw[1][0], pw[1][1], pw[SSM_BLK][0], pw[SSM_BLK][1], jnp.zeros((SUBLANES - 4, TILE_STATE), _F32)], axis=0)

    on_diag_b = (lax.broadcasted_iota(jnp.int32, (TILE_STATE, LANES), 0) // n_p
                 == lax.broadcasted_iota(jnp.int32, (TILE_STATE, LANES), 1) // n_h)
    bt_re = jnp.where(on_diag_b, _spread(b_re[...], SUBLANES), 0.0).T
    bt_im = jnp.where(on_diag_b, _spread(b_im[...], SUBLANES), 0.0).T
    bb_re = k_re * bt_re - k_im * bt_im
    bb_im = k_re * bt_im + k_im * bt_re
    wb_ref[0] = jnp.concatenate([bb_re, bb_im], axis=1)
    v_rows = []
    for s in range(SSM_BLK):
        pr, pi = pw[SSM_BLK - 1 - s]
        v_rows.append(jnp.concatenate([pr * bb_re - pi * bb_im, pr * bb_im + pi * bb_re], axis=1))
    v_ref[0] = jnp.concatenate(v_rows, axis=0).astype(v_ref.dtype)

    on_diag_c = (lax.broadcasted_iota(jnp.int32, (LANES, TILE_STATE), 0) // n_h
                 == lax.broadcasted_iota(jnp.int32, (LANES, TILE_STATE), 1) // n_p)
    ct_re = jnp.where(on_diag_c, _spread(c_re[...], SUBLANES), 0.0)
    ct_im = jnp.where(on_diag_c, _spread(c_im[...], SUBLANES), 0.0)
    cl = [(ct_re * pr - ct_im * pi, ct_re * pi + ct_im * pr) for pr, pi in pw]
    wc_ref[0] = jnp.concatenate([cl[0][0], -cl[0][1]], axis=1).T
    nt = lambda a, b: lax.dot_general(a, b, (((1,), (1,)), ((), ())), precision=lax.Precision.HIGHEST,
                                      preferred_element_type=_F32)
    direct = [nt(cl[m][0], bb_re) - nt(cl[m][1], bb_im) for m in range(SSM_BLK)]
    zero = jnp.zeros((LANES, LANES), _F32)
    rt = jnp.concatenate(
        [jnp.concatenate([cl[i + 1][0], -cl[i + 1][1]]
                         + [direct[i - s] if s <= i else zero for s in range(SSM_BLK)], axis=1)
         for i in range(SSM_BLK)], axis=0)
    r_ref[0] = rt.T.astype(r_ref.dtype)


def _ssm_params(lam_re, lam_im, log_dt, b_re, b_im, c_re, c_im, d_skip):
    n_g, n_p, n_h = N_SSM_GROUPS, SSM_STATE, SSM_GROUP
    dt = jnp.repeat(jnp.exp(log_dt), n_p)
    lam = jnp.zeros((SUBLANES, STATE_COLS), _F32).at[0].set(lam_re.reshape(-1)).at[1].set(
        lam_im.reshape(-1)).at[2].set(dt)
    tile = lambda rows, w: pl.BlockSpec((rows, w), lambda k: (k, 0))
    out3 = lambda rows, w: pl.BlockSpec((1, rows, w), lambda k: (k, 0, 0))
    cols = pl.BlockSpec((SUBLANES, TILE_STATE), lambda k: (0, k))
    k_blk = SSM_BLK * LANES
    v, r, wb, wc, coef = pl.pallas_call(
        _ssm_prep_kernel,
        grid=(N_LANE_TILES,),
        in_specs=[cols, tile(TILE_STATE, n_h), tile(TILE_STATE, n_h), tile(LANES, n_p), tile(LANES, n_p)],
        out_specs=[out3(k_blk, 2 * TILE_STATE), out3(2 * TILE_STATE + k_blk, k_blk),
                   out3(LANES, 2 * TILE_STATE), out3(2 * TILE_STATE, LANES), cols],
        out_shape=[jax.ShapeDtypeStruct((N_LANE_TILES, k_blk, 2 * TILE_STATE), _BF16),
                   jax.ShapeDtypeStruct((N_LANE_TILES, 2 * TILE_STATE + k_blk, k_blk), _BF16),
                   jax.ShapeDtypeStruct((N_LANE_TILES, LANES, 2 * TILE_STATE), _F32),
                   jax.ShapeDtypeStruct((N_LANE_TILES, 2 * TILE_STATE, LANES), _F32),
                   jax.ShapeDtypeStruct((SUBLANES, STATE_COLS), _F32)],
        compiler_params=pltpu.CompilerParams(
            dimension_semantics=("arbitrary",), vmem_limit_bytes=VMEM_LIMIT),
        name="ssm_prep",
    )(lam, b_re.reshape(n_g * n_p, n_h), b_im.reshape(n_g * n_p, n_h),
      c_re.reshape(n_g * n_h, n_p), c_im.reshape(n_g * n_h, n_p))
    return wb, wc, v, r, coef, d_skip.reshape(1, D_SSM)


def _dispatch_plan(route_t, cnt):
    t_all = route_t.shape[1]
    codes = route_t[R_CODE1:R_CODE2 + 1].astype(jnp.int32).reshape(-1)
    per_pass = SC_WORKERS * DISPATCH_CHUNK
    codes = jnp.pad(codes, (0, -(2 * t_all) % per_pass))
    counts = cnt[0, :N_EXPERTS].astype(jnp.int32)
    zero = jnp.zeros((1,), jnp.int32)
    units = (counts + EXP_UNIT - 1) // EXP_UNIT
    unit_start = jnp.concatenate([zero, jnp.cumsum(units)])
    start_row = jnp.zeros((LANES,), jnp.int32).at[:N_EXPERTS].set(unit_start[:N_EXPERTS] * EXP_UNIT)
    pieces = (units + EXP_CLASSES - 1) // EXP_CLASSES
    piece_start = jnp.concatenate([zero, jnp.cumsum(pieces)])
    tm = EXP_UNIT * EXP_CLASSES
    max_units = (2 * t_all + N_EXPERTS * (EXP_UNIT - 1)) // EXP_UNIT
    max_pieces = (max_units + N_EXPERTS * (EXP_CLASSES - 1)) // EXP_CLASSES
    g = jnp.arange(max_pieces, dtype=jnp.int32)
    owner = ((g[:, None] >= piece_start[None, :-1]) & (g[:, None] < piece_start[None, 1:])).astype(jnp.int32)
    pick = lambda table: jnp.sum(owner * table[None, :], axis=1)
    first_unit = pick(unit_start[:-1]) + (g - pick(piece_start[:-1])) * EXP_CLASSES
    piece_row = first_unit * EXP_UNIT
    piece_cls = jnp.clip(pick(unit_start[1:]) - first_unit, 1, EXP_CLASSES)
    n_rows = (max_units * EXP_UNIT + tm - 1) // tm * tm + tm
    return codes, start_row, n_rows, piece_start, piece_row, piece_cls


def kernel(x_prompt, x_sample, state_ssm_re, state_ssm_im, norm1_g, w_in, lam_re, lam_im, log_dt, ssm_b_re, ssm_b_im, ssm_c_re, ssm_c_im, ssm_d, gmlp_norm_g, gmlp_w_s, gmlp_b_s, out_norm_ssm_g, out_norm_gmlp_g, w_out, norm2_g, w_router_group, b_router_group, w_router_expert, b_router_expert, w_gate, w_up, w_down, final_norm_g):
    n, l, d = x_prompt.shape
    ns = x_sample.shape[0]
    t_all = n * l + ns
    li = 0
    g1 = norm1_g[li].reshape(1, d)
    gn = gmlp_norm_g[li].reshape(1, D_GMLP)
    tril = jnp.tril(jnp.ones((CHUNK, CHUNK), dtype=bool))
    ws_tril = jnp.where(tril[None], gmlp_w_s[li], 0.0)
    bs = gmlp_b_s[li]
    gog = out_norm_gmlp_g[li].reshape(1, D_GMLP)
    gos = out_norm_ssm_g[li].reshape(1, D_SSM)
    wb, wc, v_blk, r_blk, coef, dsk = _ssm_params(
        lam_re[li], lam_im[li], log_dt[li], ssm_b_re[li], ssm_b_im[li], ssm_c_re[li], ssm_c_im[li], ssm_d[li])
    g2 = norm2_g[li].reshape(1, d)
    pad = LANES - N_EXPERTS - N_EXPERT_GROUPS
    wr = jnp.concatenate([w_router_expert[li], w_router_group[li], jnp.zeros((d, pad), _F32)], axis=1)
    br = jnp.concatenate([b_router_expert[li], b_router_group[li], jnp.zeros((pad,), _F32)]).reshape(1, LANES)

    xa, sg, mixb = _front_prompt(x_prompt, g1, w_in[li], gn, ws_tril.astype(_BF16), bs.T, gog)
    mixa, hfin = _ssm_prompt(xa, sg, v_blk, r_blk, coef, dsk, gos)
    w00 = jnp.repeat(ws_tril[:, 0, 0], GMLP_HEAD).reshape(1, D_GMLP)
    b0 = jnp.repeat(bs[:, 0], GMLP_HEAD).reshape(1, D_GMLP)
    mix_s, hr_s, hi_s, vrow = _front_sample(
        x_sample.reshape(ns, d), g1, w_in[li], gn, w00, b0, gog, wb, wc, coef, dsk, gos,
        state_ssm_re[li].reshape(ns, STATE_COLS), state_ssm_im[li].reshape(ns, STATE_COLS))

    x1, xn, route, route_t, cnt = _mixer_out(x_prompt, mixa, mixb, x_sample.reshape(ns, d), mix_s,
                                             w_out[li], g2, wr, br)
    codes, start_row, n_rows, piece_start, piece_row, piece_cls = _dispatch_plan(route_t, cnt)
    xs, dest = _sc_dispatch(xn, codes, start_row, n_rows, DISPATCH_CHUNK)
    ys = _experts(piece_start, piece_row, piece_cls, n_rows, xs, w_gate[li], w_up[li], w_down[li])
    yab = _sc_combine(ys, dest, 2 * t_all, COMBINE_CHUNK).reshape(2, t_all, d // 2)
    y_p, y_s = _final(x1, yab, route, final_norm_g.reshape(1, d), n * l, ns)

    hf = hfin.reshape(n, N_LANE_TILES, 2, 8, SSM_STATE)
    re_p = hf[:, :, 0].reshape(1, n, N_SSM_GROUPS, SSM_STATE)
    im_p = hf[:, :, 1].reshape(1, n, N_SSM_GROUPS, SSM_STATE)
    re_s = hr_s.reshape(1, ns, N_SSM_GROUPS, SSM_STATE)
    im_s = hi_s.reshape(1, ns, N_SSM_GROUPS, SSM_STATE)
    return (y_p.reshape(n, l, d), y_s.reshape(ns, 1, d), re_p, im_p, re_s, im_s,
            vrow.reshape(1, ns, 1, D_GMLP))
```

```python
import math

import jax
import jax.numpy as jnp
from jax import lax
from jax.experimental import pallas as pl
from jax.experimental.pallas import tpu as pltpu
from jax.experimental.pallas import tpu_sc as plsc

D_MODEL = 1024
D_SSM = 512
D_GMLP = 512
SSM_GROUP = 16
N_SSM_GROUPS = 32
SSM_STATE = 64
CHUNK = 128
N_GMLP_HEADS = 4
GMLP_HEAD = 128
N_EXPERT_GROUPS = 4
EXPERTS_PER_GROUP = 8
N_EXPERTS = 32
D_EXPERT = 512
D_IN = 2048
EPS = 1e-6

LANES = 128
SUBLANES = 8
N_LANE_TILES = D_SSM // LANES
STATE_COLS = N_SSM_GROUPS * SSM_STATE
TILE_STATE = STATE_COLS // N_LANE_TILES
VMEM_LIMIT = 56 * 1024 * 1024

SC_CORES = 2
SC_SUBCORES = 16
SC_LANES = 16
SC_WORKERS = SC_CORES * SC_SUBCORES

FRONT_TL = 512
SSM_LC = 256
SSM_BLK = 4
COEF_LB_RE, COEF_LB_IM, COEF_LBLK_RE, COEF_LBLK_IM = 0, 1, 2, 3
TOK_TM = 512
FINAL_TM = 512
EXP_UNIT = 128
EXP_CLASSES = 4
W_SPLIT = 4
DISPATCH_CHUNK = 80
COMBINE_CHUNK = 24

R_E1, R_E2, R_W1, R_W2, R_RANK1, R_RANK2, R_CODE1, R_CODE2 = 0, 1, 2, 3, 4, 5, 6, 7
CODE_BITS = 16
CODE_SHIFT = float(1 << CODE_BITS)

_INV_SQRT2 = 1.0 / math.sqrt(2.0)
_BF16 = jnp.bfloat16
_F32 = jnp.float32
_U32 = jnp.uint32


def _gelu(x):
    return 0.5 * x * (1.0 + lax.erf(x * _INV_SQRT2))


def _rms(x, g):
    return x * lax.rsqrt(jnp.mean(x * x, axis=-1, keepdims=True) + EPS) * g


def _dot(a, b):
    return jnp.dot(a, b, preferred_element_type=_F32)


def _dot_f32(a, b):
    return jnp.dot(a, b, preferred_element_type=_F32, precision=lax.Precision.HIGHEST)


def _pack_bf16_pair(x):
    w = x.shape[1] // 2
    hi = lax.bitcast_convert_type(x[:, :w].astype(_BF16).astype(_F32), _U32)
    lo = lax.bitcast_convert_type(x[:, w:].astype(_BF16).astype(_F32), _U32)
    return hi | (lo >> 16)


def _unpack_bf16_pair(p):
    hi = lax.bitcast_convert_type(p & jnp.uint32(0xFFFF0000), _F32)
    lo = lax.bitcast_convert_type(p << 16, _F32)
    return jnp.concatenate([hi, lo], axis=-1)


def _head_norm_gelu(vb, gn):
    v = _gelu(vb)
    parts = []
    for h in range(N_GMLP_HEADS):
        vh = v[:, h * GMLP_HEAD:(h + 1) * GMLP_HEAD]
        parts.append(vh * lax.rsqrt(jnp.mean(vh * vh, axis=-1, keepdims=True) + EPS))
    return jnp.concatenate(parts, axis=-1) * gn


def _front_prompt_kernel(x_ref, g1_ref, win_ref, gn_ref, ws_ref, bs_ref, gog_ref,
                         xa_ref, sg_ref, mixb_ref, win_bf):
    @pl.when((pl.program_id(0) == 0) & (pl.program_id(1) == 0))
    def _():
        win_bf[...] = win_ref[...].astype(_BF16)

    x = x_ref[0]
    hn = _rms(x, g1_ref[...]).astype(_BF16)
    z = _dot(hn, win_bf[...])
    xa_ref[0] = z[:, :D_SSM]
    sg_ref[0] = jax.nn.sigmoid(z[:, D_SSM:2 * D_SSM])
    ub = _gelu(z[:, 2 * D_SSM:2 * D_SSM + D_GMLP])
    vbn = _head_norm_gelu(z[:, 2 * D_SSM + D_GMLP:], gn_ref[...]).astype(_BF16)
    tl = x.shape[0]
    rows = []
    for c in range(tl // CHUNK):
        heads = []
        for h in range(N_GMLP_HEADS):
            vh = vbn[c * CHUNK:(c + 1) * CHUNK, h * GMLP_HEAD:(h + 1) * GMLP_HEAD]
            heads.append(_dot(ws_ref[h], vh) + bs_ref[:, h:h + 1])
        rows.append(jnp.concatenate(heads, axis=-1))
    s = jnp.concatenate(rows, axis=0)
    mixb_ref[0] = _rms(ub * s, gog_ref[...]).astype(_BF16)


def _front_prompt(x, g1, win, gn, ws_tril_bf, bs_t, gog):
    n, l, d = x.shape
    tl = FRONT_TL
    grid = (n, l // tl)
    const = lambda *shape: pl.BlockSpec(shape, lambda b, i: (0,) * len(shape))
    seq = lambda w: pl.BlockSpec((1, tl, w), lambda b, i: (b, i, 0))
    return pl.pallas_call(
        _front_prompt_kernel,
        grid=grid,
        in_specs=[seq(d), const(1, d), const(d, D_IN), const(1, D_GMLP),
                  const(N_GMLP_HEADS, CHUNK, CHUNK), const(CHUNK, N_GMLP_HEADS), const(1, D_GMLP)],
        out_specs=[seq(D_SSM), seq(D_SSM), seq(D_GMLP)],
        out_shape=[jax.ShapeDtypeStruct((n, l, D_SSM), _F32),
                   jax.ShapeDtypeStruct((n, l, D_SSM), _F32),
                   jax.ShapeDtypeStruct((n, l, D_GMLP), _BF16)],
        scratch_shapes=[pltpu.VMEM((d, D_IN), _BF16)],
        compiler_params=pltpu.CompilerParams(
            dimension_semantics=("arbitrary", "arbitrary"), vmem_limit_bytes=VMEM_LIMIT),
        name="front_prompt",
    )(x, g1, win, gn, ws_tril_bf, bs_t, gog)


def _ssm_prompt_kernel(xa_ref, sg_ref, v_ref, r_ref, coef_ref, dsk_ref, gos_ref,
                       mixa_ref, hfin_ref, s_ref, st_ref):
    lc = xa_ref.shape[1]
    nblk = lc // SSM_BLK
    rows = nblk * SUBLANES

    @pl.when(pl.program_id(0) == 0)
    def _():
        st_ref[...] = jnp.zeros_like(st_ref)

    def by_position(ref):
        t = pltpu.einshape("btc->tbc", ref[...]).reshape(nblk, SSM_BLK, SUBLANES, D_SSM)
        return [t[:, i].reshape(rows, D_SSM) for i in range(SSM_BLK)]

    xs = by_position(xa_ref)
    xs_bf = [x.astype(_BF16) for x in xs]
    xk = [jnp.concatenate([x[:, k * LANES:(k + 1) * LANES] for x in xs_bf], axis=-1)
          for k in range(N_LANE_TILES)]
    for k in range(N_LANE_TILES):
        s_ref[:, 2 * TILE_STATE * k:2 * TILE_STATE * (k + 1)] = _dot(xk[k], v_ref[k])

    for kk in range(0, N_LANE_TILES, 2):
        tiles = (kk, kk + 1)
        cols = [(2 * TILE_STATE * k, 2 * TILE_STATE * k + TILE_STATE) for k in tiles]
        lbs = [tuple(jnp.broadcast_to(coef_ref[row:row + 1, k * TILE_STATE:(k + 1) * TILE_STATE],
                                      (SUBLANES, TILE_STATE)) for row in (COEF_LBLK_RE, COEF_LBLK_IM))
               for k in tiles]

        def body(j, carry, cols=cols, lbs=lbs):
            r0 = pl.multiple_of(j * SUBLANES, SUBLANES)
            out = []
            for q, ((c_re, c_im), (lr, li)) in enumerate(zip(cols, lbs)):
                hr, hi = carry[2 * q], carry[2 * q + 1]
                sr = s_ref[pl.ds(r0, SUBLANES), c_re:c_re + TILE_STATE]
                si = s_ref[pl.ds(r0, SUBLANES), c_im:c_im + TILE_STATE]
                s_ref[pl.ds(r0, SUBLANES), c_re:c_re + TILE_STATE] = hr
                s_ref[pl.ds(r0, SUBLANES), c_im:c_im + TILE_STATE] = hi
                out += [lr * hr - li * hi + sr, lr * hi + li * hr + si]
            return tuple(out)

        init = tuple(st_ref[:, c:c + TILE_STATE] for c_pair in cols for c in c_pair)
        fin = lax.fori_loop(0, nblk, body, init, unroll=2)
        for q, (c_re, c_im) in enumerate(cols):
            st_ref[:, c_re:c_re + TILE_STATE] = fin[2 * q]
            st_ref[:, c_im:c_im + TILE_STATE] = fin[2 * q + 1]

    yk = []
    for k in range(N_LANE_TILES):
        h_in = s_ref[:, 2 * TILE_STATE * k:2 * TILE_STATE * (k + 1)].astype(_BF16)
        yk.append(_dot(jnp.concatenate([h_in, xk[k]], axis=-1), r_ref[k]))
    sgs = by_position(sg_ref)
    outs = []
    for i in range(SSM_BLK):
        y = jnp.concatenate([y_k[:, i * LANES:(i + 1) * LANES] for y_k in yk], axis=-1) + dsk_ref[...] * xs[i]
        outs.append(_rms(_gelu(y) * sgs[i], gos_ref[...]).reshape(nblk, SUBLANES, D_SSM))
    mixa = jnp.stack(outs, axis=1).reshape(lc, SUBLANES, D_SSM)
    mixa_ref[...] = pltpu.einshape("tbc->btc", mixa).astype(_BF16)
    hfin_ref[...] = st_ref[...]


def _ssm_prompt(xa, sg, v, r, coef, dsk, gos):
    n, l, _ = xa.shape
    lc = SSM_LC
    const = lambda *shape: pl.BlockSpec(shape, lambda i: (0,) * len(shape))
    seq_spec = pl.BlockSpec((n, lc, D_SSM), lambda i: (0, i, 0))
    return pl.pallas_call(
        _ssm_prompt_kernel,
        grid=(l // lc,),
        in_specs=[seq_spec, seq_spec, const(*v.shape), const(*r.shape),
                  const(*coef.shape), const(1, D_SSM), const(1, D_SSM)],
        out_specs=[seq_spec, const(n, 2 * STATE_COLS)],
        out_shape=[jax.ShapeDtypeStruct((n, l, D_SSM), _BF16),
                   jax.ShapeDtypeStruct((n, 2 * STATE_COLS), _F32)],
        scratch_shapes=[pltpu.VMEM((lc // SSM_BLK * n, 2 * STATE_COLS), _F32),
                        pltpu.VMEM((n, 2 * STATE_COLS), _F32)],
        compiler_params=pltpu.CompilerParams(
            dimension_semantics=("arbitrary",), vmem_limit_bytes=VMEM_LIMIT),
        name="ssm_prompt",
    )(xa, sg, v, r, coef, dsk, gos)


def _front_sample_kernel(x_ref, g1_ref, win_ref, gn_ref, w00_ref, b0_ref, gog_ref,
                         wb_ref, wc_ref, coef_ref, dsk_ref, gos_ref, h0r_ref, h0i_ref,
                         mix_ref, hr_ref, hi_ref, vrow_ref):
    x = x_ref[...]
    hn = _rms(x, g1_ref[...])
    z = _dot_f32(hn, win_ref[...])
    xa = z[:, :D_SSM]
    ys = []
    for k in range(N_LANE_TILES):
        bu = _dot_f32(xa[:, k * LANES:(k + 1) * LANES], wb_ref[k])
        sl = slice(k * TILE_STATE, (k + 1) * TILE_STATE)
        lr, li = coef_ref[COEF_LB_RE:COEF_LB_RE + 1, sl], coef_ref[COEF_LB_IM:COEF_LB_IM + 1, sl]
        h0r, h0i = h0r_ref[:, sl], h0i_ref[:, sl]
        nr = lr * h0r - li * h0i + bu[:, :TILE_STATE]
        ni = lr * h0i + li * h0r + bu[:, TILE_STATE:]
        hr_ref[:, sl] = nr
        hi_ref[:, sl] = ni
        ys.append(_dot_f32(jnp.concatenate([nr, ni], axis=-1), wc_ref[k]))
    y = jnp.concatenate(ys, axis=-1) + dsk_ref[...] * xa
    ya = _gelu(y) * jax.nn.sigmoid(z[:, D_SSM:2 * D_SSM])
    mix_ref[:, :D_SSM] = _rms(ya, gos_ref[...])
    ub = _gelu(z[:, 2 * D_SSM:2 * D_SSM + D_GMLP])
    vbn = _head_norm_gelu(z[:, 2 * D_SSM + D_GMLP:], gn_ref[...])
    vrow_ref[...] = vbn
    s = w00_ref[...] * vbn + b0_ref[...]
    mix_ref[:, D_SSM:] = _rms(ub * s, gog_ref[...])


def _front_sample(x, g1, win, gn, w00, b0, gog, wb, wc, coef, dsk, gos, h0r, h0i):
    n = x.shape[0]
    vmem = pl.BlockSpec(memory_space=pltpu.VMEM)
    return pl.pallas_call(
        _front_sample_kernel,
        in_specs=[vmem] * 14,
        out_specs=[vmem] * 4,
        out_shape=[jax.ShapeDtypeStruct((n, D_MODEL), _F32),
                   jax.ShapeDtypeStruct((n, STATE_COLS), _F32),
                   jax.ShapeDtypeStruct((n, STATE_COLS), _F32),
                   jax.ShapeDtypeStruct((n, D_GMLP), _F32)],
        compiler_params=pltpu.CompilerParams(vmem_limit_bytes=VMEM_LIMIT),
        name="front_sample",
    )(x, g1, win, gn, w00, b0, gog, wb, wc, coef, dsk, gos, h0r, h0i)


def _route(logits, base):
    tm = logits.shape[0]
    lt = logits.T
    ex = lt[:N_EXPERTS, :]
    gr = lt[N_EXPERTS:N_EXPERTS + SUBLANES, :]
    row_e = lax.broadcasted_iota(jnp.int32, ex.shape, 0).astype(_F32)
    row_g = lax.broadcasted_iota(jnp.int32, gr.shape, 0).astype(_F32)
    neg = jnp.float32(-jnp.inf)
    big = jnp.float32(LANES)
    is_g = row_g < N_EXPERT_GROUPS
    gl = jnp.where(is_g, gr, neg)
    gmax = jnp.max(gl, axis=0, keepdims=True)
    gi = jnp.min(jnp.where(gl == gmax, row_g, big), axis=0, keepdims=True)
    p_top = 1.0 / jnp.sum(jnp.where(is_g, jnp.exp(gl - gmax), 0.0), axis=0, keepdims=True)
    lo = gi * EXPERTS_PER_GROUP
    in_grp = (row_e >= lo) & (row_e < lo + EXPERTS_PER_GROUP)
    m1 = jnp.max(jnp.where(in_grp, ex, neg), axis=0, keepdims=True)
    i1 = jnp.min(jnp.where(in_grp & (ex == m1), row_e, big), axis=0, keepdims=True)
    rest = in_grp & (row_e != i1)
    m2 = jnp.max(jnp.where(rest, ex, neg), axis=0, keepdims=True)
    i2 = jnp.min(jnp.where(rest & (ex == m2), row_e, big), axis=0, keepdims=True)
    e2 = jnp.exp(m2 - m1)
    w1 = p_top / (1.0 + e2)
    w2 = p_top * e2 / (1.0 + e2)
    sel1 = row_e == i1
    sel2 = row_e == i2
    hits = jnp.where(sel1 | sel2, 1.0, 0.0)
    src = lax.broadcasted_iota(jnp.int32, (tm, tm), 0)
    dst = lax.broadcasted_iota(jnp.int32, (tm, tm), 1)
    before = _dot(hits.astype(_BF16), jnp.where(src < dst, 1.0, 0.0).astype(_BF16)) + base
    rank1 = jnp.sum(jnp.where(sel1, before, 0.0), axis=0, keepdims=True)
    rank2 = jnp.sum(jnp.where(sel2, before, 0.0), axis=0, keepdims=True)
    fields = {R_E1: i1, R_E2: i2, R_W1: w1, R_W2: w2, R_RANK1: rank1, R_RANK2: rank2,
              R_CODE1: i1 * CODE_SHIFT + rank1, R_CODE2: i2 * CODE_SHIFT + rank2}
    row8 = lax.broadcasted_iota(jnp.int32, (SUBLANES, tm), 0)
    route_t = jnp.zeros((SUBLANES, tm), _F32)
    for r, val in fields.items():
        route_t = jnp.where(row8 == r, val, route_t)
    route = jnp.concatenate([route_t, jnp.zeros((LANES - SUBLANES, tm), _F32)], axis=0).T
    return route_t, route, base + jnp.sum(hits, axis=1, keepdims=True)


def _mixer_out_prompt_kernel(x_ref, mixa_ref, mixb_ref, wo_ref, g2_ref, wr_ref, br_ref,
                             x1_ref, xn_ref, route_ref, route_t_ref, cnt_ref, base_ref):
    @pl.when((pl.program_id(0) == 0) & (pl.program_id(1) == 0))
    def _():
        base_ref[...] = jnp.zeros_like(base_ref)

    x1 = x_ref[0] + _dot(mixa_ref[0], wo_ref[:D_SSM, :]) + _dot(mixb_ref[0], wo_ref[D_SSM:, :])
    xn = _rms(x1, g2_ref[...])
    logits = _dot(xn.astype(_BF16), wr_ref[...]) + br_ref[...]
    route_t, route, base = _route(logits, base_ref[...])
    x1_ref[...] = x1
    xn_ref[...] = _pack_bf16_pair(xn)
    route_ref[...] = route
    route_t_ref[...] = route_t
    base_ref[...] = base
    cnt_ref[...] = base


def _mixer_out_sample_kernel(x_ref, mix_ref, wo_ref, g2_ref, wr_ref, br_ref, cnt_in_ref,
                             x1_in, xn_in, route_in, route_t_in,
                             x1_ref, xn_ref, route_ref, route_t_ref, cnt_ref):
    del x1_in, xn_in, route_in, route_t_in
    x1 = (x_ref[...] + _dot_f32(mix_ref[:, :D_SSM], wo_ref[:D_SSM, :])
          + _dot_f32(mix_ref[:, D_SSM:], wo_ref[D_SSM:, :]))
    xn = _rms(x1, g2_ref[...])
    logits = _dot_f32(xn, wr_ref[...]) + br_ref[...]
    route_t, route, base = _route(logits, cnt_in_ref[...])
    x1_ref[...] = x1
    xn_ref[...] = _pack_bf16_pair(xn)
    route_ref[...] = route
    route_t_ref[...] = route_t
    cnt_ref[...] = base


def _mixer_out(x_p, mixa, mixb, x_s, mix_s, wo, g2, wr, br):
    n, l, d = x_p.shape
    ns = x_s.shape[0]
    t_all = n * l + ns
    tm = TOK_TM
    per_seq = l // tm
    const = lambda *shape: pl.BlockSpec(shape, lambda b, i: (0,) * len(shape))
    seq = lambda w: pl.BlockSpec((1, tm, w), lambda b, i: (b, i, 0))
    tok = lambda w: pl.BlockSpec((tm, w), lambda b, i: (b * per_seq + i, 0))
    tok_shapes = [jax.ShapeDtypeStruct((t_all, d), _F32),
                  jax.ShapeDtypeStruct((t_all, d // 2), _U32),
                  jax.ShapeDtypeStruct((t_all, LANES), _F32),
                  jax.ShapeDtypeStruct((SUBLANES, t_all), _F32)]
    cnt_shape = jax.ShapeDtypeStruct((N_EXPERTS, 1), _F32)
    x1, xn, route, route_t, cnt = pl.pallas_call(
        _mixer_out_prompt_kernel,
        grid=(n, per_seq),
        in_specs=[seq(d), seq(D_SSM), seq(D_GMLP),
                  const(d, d), const(1, d), const(d, LANES), const(1, LANES)],
        out_specs=[tok(d), tok(d // 2), tok(LANES),
                   pl.BlockSpec((SUBLANES, tm), lambda b, i: (0, b * per_seq + i)), const(N_EXPERTS, 1)],
        out_shape=tok_shapes + [cnt_shape],
        scratch_shapes=[pltpu.VMEM((N_EXPERTS, 1), _F32)],
        compiler_params=pltpu.CompilerParams(
            dimension_semantics=("arbitrary", "arbitrary"), vmem_limit_bytes=VMEM_LIMIT),
        name="mixer_out_prompt",
    )(x_p, mixa, mixb, wo.astype(_BF16), g2, wr.astype(_BF16), br)
    tail = (n * l) // ns
    c1 = lambda *shape: pl.BlockSpec(shape, lambda i: (0,) * len(shape))
    anyspec = pl.BlockSpec(memory_space=pl.ANY)
    tail_spec = lambda w: pl.BlockSpec((ns, w), lambda i: (tail, 0))
    return pl.pallas_call(
        _mixer_out_sample_kernel,
        grid=(1,),
        in_specs=[c1(ns, d), c1(ns, d), c1(d, d), c1(1, d), c1(d, LANES), c1(1, LANES), c1(N_EXPERTS, 1),
                  anyspec, anyspec, anyspec, anyspec],
        out_specs=[tail_spec(d), tail_spec(d // 2), tail_spec(LANES),
                   pl.BlockSpec((SUBLANES, ns), lambda i: (0, tail)), c1(N_EXPERTS, 1)],
        out_shape=tok_shapes + [cnt_shape],
        input_output_aliases={7: 0, 8: 1, 9: 2, 10: 3},
        compiler_params=pltpu.CompilerParams(
            dimension_semantics=("arbitrary",), vmem_limit_bytes=VMEM_LIMIT),
        name="mixer_out_sample",
    )(x_s, mix_s, wo, g2, wr, br, cnt, x1, xn, route, route_t)


def _sc_stream(n_chunks, gather, write):
    gather(0).start()
    for j in range(n_chunks):
        if j + 1 < n_chunks:
            if j >= 1:
                write(j - 1).wait()
            gather(j + 1).start()
        gather(j).wait()
        write(j).start()
    if n_chunks >= 2:
        write(n_chunks - 2).wait()
    write(n_chunks - 1).wait()


def _sc_mesh():
    return plsc.VectorSubcoreMesh(core_axis_name="c", subcore_axis_name="s",
                                  num_cores=SC_CORES, num_subcores=SC_SUBCORES)


def _sc_buffers(chunk, w, dtype):
    return [pltpu.VMEM((chunk, w), dtype), pltpu.VMEM((chunk, w), dtype)] + [pltpu.SemaphoreType.DMA] * 4


def _sc_combine(table, idx, n_out, chunk):
    w = table.shape[1]
    rows_w = n_out // SC_WORKERS
    n_chunks = rows_w // chunk
    assert rows_w * SC_WORKERS == n_out and n_chunks * chunk == rows_w and rows_w % SUBLANES == 0

    def body(table_hbm, idx_hbm, out_hbm, idx_v, buf0, buf1, g0, g1, w0, w1):
        wid = lax.axis_index("s") * SC_CORES + lax.axis_index("c")
        base = pl.multiple_of(wid * rows_w, SUBLANES)
        pltpu.sync_copy(idx_hbm.at[pl.ds(base, rows_w)], idx_v)
        bufs, gsems, wsems = (buf0, buf1), (g0, g1), (w0, w1)

        def gather(j):
            return pltpu.make_async_copy(table_hbm.at[idx_v.at[pl.ds(j * chunk, chunk)]], bufs[j % 2], gsems[j % 2])

        def write(j):
            return pltpu.make_async_copy(bufs[j % 2], out_hbm.at[pl.ds(base + j * chunk, chunk)], wsems[j % 2])

        _sc_stream(n_chunks, gather, write)

    return pl.kernel(
        body,
        out_type=jax.ShapeDtypeStruct((n_out, w), table.dtype),
        mesh=_sc_mesh(),
        scratch_types=[pltpu.VMEM((rows_w,), jnp.int32)] + _sc_buffers(chunk, w, table.dtype),
        compiler_params=pltpu.CompilerParams(use_tc_tiling_on_sc=True),
        name="sc_combine",
    )(table, idx)


def _sc_dispatch(table, codes, start_row, n_out, chunk):
    t_all, w = table.shape
    n_pad = codes.shape[0]
    n_ent = 2 * t_all
    ent_w = n_pad // SC_WORKERS
    n_chunks = ent_w // chunk
    per_chunk = chunk // SC_LANES
    trash = n_out - (n_pad - n_ent)
    assert ent_w * SC_WORKERS == n_pad and n_chunks * chunk == ent_w
    assert per_chunk * SC_LANES == chunk and chunk <= LANES and n_pad - n_ent <= t_all

    def body(table_hbm, code_hbm, start_hbm, out_hbm, dest_hbm,
             code_v, dest_v, tok_v, dst_v, start_v, buf0, buf1, g0, g1, w0, w1):
        wid = lax.axis_index("s") * SC_CORES + lax.axis_index("c")
        ebase = pl.multiple_of(wid * ent_w, SUBLANES)
        pltpu.sync_copy(code_hbm.at[pl.ds(ebase, ent_w)], code_v)
        pltpu.sync_copy(start_hbm, start_v)
        lane = lax.iota(jnp.int32, SC_LANES)
        for j in range(n_chunks):
            for c in range(per_chunk):
                off = j * chunk + c * SC_LANES
                ent = ebase + off + lane
                code = code_v[pl.ds(off, SC_LANES)]
                d = plsc.load_gather(start_v, [code >> CODE_BITS]) + (code & ((1 << CODE_BITS) - 1))
                d = jnp.where(ent >= n_ent, trash + (ent - n_ent), d)
                tok = jnp.where(ent >= t_all, ent - t_all, ent)
                tok = jnp.where(tok >= t_all, tok - t_all, tok)
                dest_v[pl.ds(off, SC_LANES)] = d
                dst_v[j, pl.ds(c * SC_LANES, SC_LANES)] = d
                tok_v[j, pl.ds(c * SC_LANES, SC_LANES)] = tok
        pltpu.sync_copy(dest_v, dest_hbm.at[pl.ds(ebase, ent_w)])
        bufs, gsems, wsems = (buf0, buf1), (g0, g1), (w0, w1)

        def gather(j):
            return pltpu.make_async_copy(table_hbm.at[tok_v.at[j]], bufs[j % 2], gsems[j % 2])

        def scatter(j):
            return pltpu.make_async_copy(bufs[j % 2], out_hbm.at[dst_v.at[j]], wsems[j % 2])

        _sc_stream(n_chunks, gather, scatter)

    return pl.kernel(
        body,
        out_type=(jax.ShapeDtypeStruct((n_out, w), table.dtype), jax.ShapeDtypeStruct((n_pad,), jnp.int32)),
        mesh=_sc_mesh(),
        scratch_types=([pltpu.VMEM((ent_w,), jnp.int32), pltpu.VMEM((ent_w,), jnp.int32),
                        pltpu.VMEM((n_chunks, chunk), jnp.int32), pltpu.VMEM((n_chunks, chunk), jnp.int32),
                        pltpu.VMEM((LANES,), jnp.int32)] + _sc_buffers(chunk, w, table.dtype)),
        compiler_params=pltpu.CompilerParams(use_tc_tiling_on_sc=True, needs_layout_passes=False),
        name="sc_dispatch",
    )(table, codes, start_row)


def _experts_kernel(piece_start_ref, piece_row_ref, piece_cls_ref, *refs):
    n_w = 3 * W_SPLIT
    wg_refs, wu_refs, wd_refs = refs[:W_SPLIT], refs[W_SPLIT:2 * W_SPLIT], refs[2 * W_SPLIT:n_w]
    xs_hbm, ys_hbm, wg_bf, wu_bf, wd_bf, xbuf, ybuf, xsem, ysem = refs[n_w:]
    e = pl.program_id(0)
    g0 = piece_start_ref[e]
    n_here = piece_start_ref[e + 1] - g0
    n_total = piece_start_ref[N_EXPERTS]

    def per_class(g, fn):
        cls = piece_cls_ref[g]
        row = pl.multiple_of(piece_row_ref[g], EXP_UNIT)
        for c in range(1, EXP_CLASSES + 1):
            pl.when(cls == c)(lambda c=c: fn(c * EXP_UNIT, row))

    def x_copy(slot, rows, row):
        return pltpu.make_async_copy(xs_hbm.at[pl.ds(row, rows)], xbuf.at[slot, pl.ds(0, rows)], xsem.at[slot])

    def y_copy(slot, rows, row):
        return pltpu.make_async_copy(ybuf.at[slot, pl.ds(0, rows)], ys_hbm.at[pl.ds(row, rows)], ysem.at[slot])

    @pl.when((e == 0) & (n_total > 0))
    def _():
        per_class(0, lambda rows, row: x_copy(0, rows, row).start())

    for dst, chunks in ((wg_bf, wg_refs), (wu_bf, wu_refs), (wd_bf, wd_refs)):
        rows = dst.shape[0] // W_SPLIT
        for q, src in enumerate(chunks):
            dst[q * rows:(q + 1) * rows, :] = src[0, 0].astype(_BF16)

    def piece(j, carry):
        g = g0 + j
        slot = lax.rem(g, 2)
        per_class(g, lambda rows, row: x_copy(slot, rows, row).wait())

        @pl.when(g + 1 < n_total)
        def _():
            per_class(g + 1, lambda rows, row: x_copy(1 - slot, rows, row).start())

        @pl.when(g >= 2)
        def _():
            per_class(g - 2, lambda rows, row: y_copy(slot, rows, row).wait())

        def compute(rows, row):
            x = _unpack_bf16_pair(xbuf[slot, pl.ds(0, rows)]).astype(_BF16)
            a = _dot(x, wg_bf[...])
            u = _dot(x, wu_bf[...])
            h = (a * jax.nn.sigmoid(a) * u).astype(_BF16)
            ybuf[slot, pl.ds(0, rows)] = _pack_bf16_pair(_dot(h, wd_bf[...]))
            y_copy(slot, rows, row).start()

        per_class(g, compute)
        return carry

    lax.fori_loop(0, n_here, piece, 0)

    @pl.when(e == N_EXPERTS - 1)
    def _():
        @pl.when(n_total >= 2)
        def _():
            per_class(n_total - 2, lambda rows, row: y_copy(lax.rem(n_total, 2), rows, row).wait())

        @pl.when(n_total >= 1)
        def _():
            per_class(n_total - 1, lambda rows, row: y_copy(lax.rem(n_total - 1, 2), rows, row).wait())


def _experts(piece_start, piece_row, piece_cls, n_rows, xs, w_gate, w_up, w_down):
    dh = xs.shape[1]
    d = 2 * dh
    tm = EXP_UNIT * EXP_CLASSES
    anyspec = pl.BlockSpec(memory_space=pl.ANY)

    def chunk_specs(rows, cols):
        return [pl.BlockSpec((1, 1, rows // W_SPLIT, cols), lambda e, ps, pr, pc, q=q: (e, q, 0, 0))
                for q in range(W_SPLIT)]

    split = lambda w: w.reshape(w.shape[0], W_SPLIT, w.shape[1] // W_SPLIT, w.shape[2])
    grid_spec = pltpu.PrefetchScalarGridSpec(
        num_scalar_prefetch=3,
        grid=(N_EXPERTS,),
        in_specs=(chunk_specs(d, D_EXPERT) + chunk_specs(d, D_EXPERT) + chunk_specs(D_EXPERT, d) + [anyspec]),
        out_specs=anyspec,
        scratch_shapes=[pltpu.VMEM((d, D_EXPERT), _BF16), pltpu.VMEM((d, D_EXPERT), _BF16),
                        pltpu.VMEM((D_EXPERT, d), _BF16),
                        pltpu.VMEM((2, tm, dh), _U32), pltpu.VMEM((2, tm, dh), _U32),
                        pltpu.SemaphoreType.DMA((2,)), pltpu.SemaphoreType.DMA((2,))],
    )
    return pl.pallas_call(
        _experts_kernel,
        grid_spec=grid_spec,
        out_shape=jax.ShapeDtypeStruct((n_rows, dh), _U32),
        compiler_params=pltpu.CompilerParams(
            dimension_semantics=("arbitrary",), vmem_limit_bytes=VMEM_LIMIT),
        name="experts",
    )(piece_start, piece_row, piece_cls, *([split(w_gate)] * W_SPLIT), *([split(w_up)] * W_SPLIT),
      *([split(w_down)] * W_SPLIT), xs)


def _final_kernel(x1_ref, ya_ref, yb_ref, route_ref, gf_ref, y_ref):
    route = route_ref[...]
    x2 = (x1_ref[...] + route[:, R_W1:R_W1 + 1] * _unpack_bf16_pair(ya_ref[...])
          + route[:, R_W2:R_W2 + 1] * _unpack_bf16_pair(yb_ref[...]))
    y_ref[...] = _rms(x2, gf_ref[...])


def _final(x1, yab, route, gf, n_prompt, n_sample):
    d = x1.shape[1]

    def call(tm, first_block, n_rows, name):
        tok = lambda w: pl.BlockSpec((tm, w), lambda i: (first_block + i, 0))
        sel = lambda k: pl.BlockSpec((None, tm, d // 2), lambda i: (k, first_block + i, 0))
        return pl.pallas_call(
            _final_kernel,
            grid=(n_rows // tm,),
            in_specs=[tok(d), sel(0), sel(1), tok(LANES), pl.BlockSpec((1, d), lambda i: (0, 0))],
            out_specs=pl.BlockSpec((tm, d), lambda i: (i, 0)),
            out_shape=jax.ShapeDtypeStruct((n_rows, d), _F32),
            compiler_params=pltpu.CompilerParams(
                dimension_semantics=("arbitrary",), vmem_limit_bytes=VMEM_LIMIT),
            name=name,
        )(x1, yab, yab, route, gf)

    return (call(FINAL_TM, 0, n_prompt, "final_prompt"),
            call(n_sample, n_prompt // n_sample, n_sample, "final_sample"))


def _powers(lam_re, lam_im, dt):
    out = []
    for m in range(SSM_BLK + 1):
        mag = jnp.exp(m * lam_re * dt)
        ang = m * lam_im * dt
        out.append((mag * jnp.cos(ang), mag * jnp.sin(ang)))
    return out


def _spread(x, copies):
    w = x.shape[1]
    src = lax.broadcasted_iota(jnp.int32, (w, w * copies), 0)
    dst = lax.broadcasted_iota(jnp.int32, (w, w * copies), 1)
    return _dot_f32(x, jnp.where(dst % w == src, 1.0, 0.0))


def _ssm_prep_kernel(lam_ref, b_re, b_im, c_re, c_im, v_ref, r_ref, wb_ref, wc_ref, coef_ref):
    n_p, n_h = SSM_STATE, SSM_GROUP
    lr, li, dt = lam_ref[0:1, :], lam_ref[1:2, :], lam_ref[2:3, :]
    pw = _powers(lr, li, dt)
    den = lr * lr + li * li
    nr, ni = pw[1][0] - 1.0, pw[1][1]
    k_re = (nr * lr + ni * li) / den
    k_im = (ni * lr - nr * li) / den
    coef_ref[...] = jnp.concatenate(
        [pw[1][0], pw[1][1], pw[SSM_BLK][0], pw[SSM_BLK][1], jnp.zeros((SUBLANES - 4, TILE_STATE), _F32)], axis=0)

    on_diag_b = (lax.broadcasted_iota(jnp.int32, (TILE_STATE, LANES), 0) // n_p
                 == lax.broadcasted_iota(jnp.int32, (TILE_STATE, LANES), 1) // n_h)
    bt_re = jnp.where(on_diag_b, _spread(b_re[...], SUBLANES), 0.0).T
    bt_im = jnp.where(on_diag_b, _spread(b_im[...], SUBLANES), 0.0).T
    bb_re = k_re * bt_re - k_im * bt_im
    bb_im = k_re * bt_im + k_im * bt_re
    wb_ref[0] = jnp.concatenate([bb_re, bb_im], axis=1)
    v_rows = []
    for s in range(SSM_BLK):
        pr, pi = pw[SSM_BLK - 1 - s]
        v_rows.append(jnp.concatenate([pr * bb_re - pi * bb_im, pr * bb_im + pi * bb_re], axis=1))
    v_ref[0] = jnp.concatenate(v_rows, axis=0).astype(v_ref.dtype)

    on_diag_c = (lax.broadcasted_iota(jnp.int32, (LANES, TILE_STATE), 0) // n_h
                 == lax.broadcasted_iota(jnp.int32, (LANES, TILE_STATE), 1) // n_p)
    ct_re = jnp.where(on_diag_c, _spread(c_re[...], SUBLANES), 0.0)
    ct_im = jnp.where(on_diag_c, _spread(c_im[...], SUBLANES), 0.0)
    cl = [(ct_re * pr - ct_im * pi, ct_re * pi + ct_im * pr) for pr, pi in pw]
    wc_ref[0] = jnp.concatenate([cl[0][0], -cl[0][1]], axis=1).T
    nt = lambda a, b: lax.dot_general(a, b, (((1,), (1,)), ((), ())), precision=lax.Precision.HIGHEST,
                                      preferred_element_type=_F32)
    direct = [nt(cl[m][0], bb_re) - nt(cl[m][1], bb_im) for m in range(SSM_BLK)]
    zero = jnp.zeros((LANES, LANES), _F32)
    rt = jnp.concatenate(
        [jnp.concatenate([cl[i + 1][0], -cl[i + 1][1]]
                         + [direct[i - s] if s <= i else zero for s in range(SSM_BLK)], axis=1)
         for i in range(SSM_BLK)], axis=0)
    r_ref[0] = rt.T.astype(r_ref.dtype)


def _ssm_params(lam_re, lam_im, log_dt, b_re, b_im, c_re, c_im, d_skip):
    n_g, n_p, n_h = N_SSM_GROUPS, SSM_STATE, SSM_GROUP
    dt = jnp.repeat(jnp.exp(log_dt), n_p)
    lam = jnp.zeros((SUBLANES, STATE_COLS), _F32).at[0].set(lam_re.reshape(-1)).at[1].set(
        lam_im.reshape(-1)).at[2].set(dt)
    tile = lambda rows, w: pl.BlockSpec((rows, w), lambda k: (k, 0))
    out3 = lambda rows, w: pl.BlockSpec((1, rows, w), lambda k: (k, 0, 0))
    cols = pl.BlockSpec((SUBLANES, TILE_STATE), lambda k: (0, k))
    k_blk = SSM_BLK * LANES
    v, r, wb, wc, coef = pl.pallas_call(
        _ssm_prep_kernel,
        grid=(N_LANE_TILES,),
        in_specs=[cols, tile(TILE_STATE, n_h), tile(TILE_STATE, n_h), tile(LANES, n_p), tile(LANES, n_p)],
        out_specs=[out3(k_blk, 2 * TILE_STATE), out3(2 * TILE_STATE + k_blk, k_blk),
                   out3(LANES, 2 * TILE_STATE), out3(2 * TILE_STATE, LANES), cols],
        out_shape=[jax.ShapeDtypeStruct((N_LANE_TILES, k_blk, 2 * TILE_STATE), _BF16),
                   jax.ShapeDtypeStruct((N_LANE_TILES, 2 * TILE_STATE + k_blk, k_blk), _BF16),
                   jax.ShapeDtypeStruct((N_LANE_TILES, LANES, 2 * TILE_STATE), _F32),
                   jax.ShapeDtypeStruct((N_LANE_TILES, 2 * TILE_STATE, LANES), _F32),
                   jax.ShapeDtypeStruct((SUBLANES, STATE_COLS), _F32)],
        compiler_params=pltpu.CompilerParams(
            dimension_semantics=("arbitrary",), vmem_limit_bytes=VMEM_LIMIT),
        name="ssm_prep",
    )(lam, b_re.reshape(n_g * n_p, n_h), b_im.reshape(n_g * n_p, n_h),
      c_re.reshape(n_g * n_h, n_p), c_im.reshape(n_g * n_h, n_p))
    return wb, wc, v, r, coef, d_skip.reshape(1, D_SSM)


def _dispatch_plan(route_t, cnt):
    t_all = route_t.shape[1]
    codes = route_t[R_CODE1:R_CODE2 + 1].astype(jnp.int32).reshape(-1)
    per_pass = SC_WORKERS * DISPATCH_CHUNK
    codes = jnp.pad(codes, (0, -(2 * t_all) % per_pass))
    counts = cnt[:, 0].astype(jnp.int32)
    zero = jnp.zeros((1,), jnp.int32)
    units = (counts + EXP_UNIT - 1) // EXP_UNIT
    unit_start = jnp.concatenate([zero, jnp.cumsum(units)])
    start_row = jnp.zeros((LANES,), jnp.int32).at[:N_EXPERTS].set(unit_start[:N_EXPERTS] * EXP_UNIT)
    pieces = (units + EXP_CLASSES - 1) // EXP_CLASSES
    piece_start = jnp.concatenate([zero, jnp.cumsum(pieces)])
    tm = EXP_UNIT * EXP_CLASSES
    max_units = (2 * t_all + N_EXPERTS * (EXP_UNIT - 1)) // EXP_UNIT
    max_pieces = (max_units + N_EXPERTS * (EXP_CLASSES - 1)) // EXP_CLASSES
    g = jnp.arange(max_pieces, dtype=jnp.int32)
    owner = ((g[:, None] >= piece_start[None, :-1]) & (g[:, None] < piece_start[None, 1:])).astype(jnp.int32)
    pick = lambda table: jnp.sum(owner * table[None, :], axis=1)
    first_unit = pick(unit_start[:-1]) + (g - pick(piece_start[:-1])) * EXP_CLASSES
    piece_row = first_unit * EXP_UNIT
    piece_cls = jnp.clip(pick(unit_start[1:]) - first_unit, 1, EXP_CLASSES)
    n_rows = (max_units * EXP_UNIT + tm - 1) // tm * tm + tm
    return codes, start_row, n_rows, piece_start, piece_row, piece_cls


def kernel(x_prompt, x_sample, state_ssm_re, state_ssm_im, norm1_g, w_in, lam_re, lam_im, log_dt, ssm_b_re, ssm_b_im, ssm_c_re, ssm_c_im, ssm_d, gmlp_norm_g, gmlp_w_s, gmlp_b_s, out_norm_ssm_g, out_norm_gmlp_g, w_out, norm2_g, w_router_group, b_router_group, w_router_expert, b_router_expert, w_gate, w_up, w_down, final_norm_g):
    n, l, d = x_prompt.shape
    ns = x_sample.shape[0]
    t_all = n * l + ns
    li = 0
    g1 = norm1_g[li].reshape(1, d)
    gn = gmlp_norm_g[li].reshape(1, D_GMLP)
    tril = jnp.tril(jnp.ones((CHUNK, CHUNK), dtype=bool))
    ws_tril = jnp.where(tril[None], gmlp_w_s[li], 0.0)
    bs = gmlp_b_s[li]
    gog = out_norm_gmlp_g[li].reshape(1, D_GMLP)
    gos = out_norm_ssm_g[li].reshape(1, D_SSM)
    wb, wc, v_blk, r_blk, coef, dsk = _ssm_params(
        lam_re[li], lam_im[li], log_dt[li], ssm_b_re[li], ssm_b_im[li], ssm_c_re[li], ssm_c_im[li], ssm_d[li])
    g2 = norm2_g[li].reshape(1, d)
    pad = LANES - N_EXPERTS - N_EXPERT_GROUPS
    wr = jnp.concatenate([w_router_expert[li], w_router_group[li], jnp.zeros((d, pad), _F32)], axis=1)
    br = jnp.concatenate([b_router_expert[li], b_router_group[li], jnp.zeros((pad,), _F32)]).reshape(1, LANES)

    xa, sg, mixb = _front_prompt(x_prompt, g1, w_in[li], gn, ws_tril.astype(_BF16), bs.T, gog)
    mixa, hfin = _ssm_prompt(xa, sg, v_blk, r_blk, coef, dsk, gos)
    w00 = jnp.repeat(ws_tril[:, 0, 0], GMLP_HEAD).reshape(1, D_GMLP)
    b0 = jnp.repeat(bs[:, 0], GMLP_HEAD).reshape(1, D_GMLP)
    mix_s, hr_s, hi_s, vrow = _front_sample(
        x_sample.reshape(ns, d), g1, w_in[li], gn, w00, b0, gog, wb, wc, coef, dsk, gos,
        state_ssm_re[li].reshape(ns, STATE_COLS), state_ssm_im[li].reshape(ns, STATE_COLS))

    x1, xn, route, route_t, cnt = _mixer_out(x_prompt, mixa, mixb, x_sample.reshape(ns, d), mix_s,
                                             w_out[li], g2, wr, br)
    codes, start_row, n_rows, piece_start, piece_row, piece_cls = _dispatch_plan(route_t, cnt)
    xs, dest = _sc_dispatch(xn, codes, start_row, n_rows, DISPATCH_CHUNK)
    ys = _experts(piece_start, piece_row, piece_cls, n_rows, xs, w_gate[li], w_up[li], w_down[li])
    yab = _sc_combine(ys, dest, 2 * t_all, COMBINE_CHUNK).reshape(2, t_all, d // 2)
    y_p, y_s = _final(x1, yab, route, final_norm_g.reshape(1, d), n * l, ns)

    hf = hfin.reshape(n, N_LANE_TILES, 2, 8, SSM_STATE)
    re_p = hf[:, :, 0].reshape(1, n, N_SSM_GROUPS, SSM_STATE)
    im_p = hf[:, :, 1].reshape(1, n, N_SSM_GROUPS, SSM_STATE)
    re_s = hr_s.reshape(1, ns, N_SSM_GROUPS, SSM_STATE)
    im_s = hi_s.reshape(1, ns, N_SSM_GROUPS, SSM_STATE)
    return (y_p.reshape(n, l, d), y_s.reshape(ns, 1, d), re_p, im_p, re_s, im_s,
            vrow.reshape(1, ns, 1, D_GMLP))
```

```python
import math

import jax
import jax.numpy as jnp
from jax import lax
from jax.experimental import pallas as pl
from jax.experimental.pallas import tpu as pltpu
from jax.experimental.pallas import tpu_sc as plsc

D_MODEL = 1024
D_SSM = 512
D_GMLP = 512
SSM_GROUP = 16
N_SSM_GROUPS = 32
SSM_STATE = 64
CHUNK = 128
N_GMLP_HEADS = 4
GMLP_HEAD = 128
N_EXPERT_GROUPS = 4
EXPERTS_PER_GROUP = 8
N_EXPERTS = 32
D_EXPERT = 512
D_IN = 2048
EPS = 1e-6

LANES = 128
SUBLANES = 8
N_LANE_TILES = D_SSM // LANES
STATE_COLS = N_SSM_GROUPS * SSM_STATE
TILE_STATE = STATE_COLS // N_LANE_TILES
VMEM_LIMIT = 56 * 1024 * 1024

SC_CORES = 2
SC_SUBCORES = 16
SC_LANES = 16
SC_WORKERS = SC_CORES * SC_SUBCORES

FRONT_TL = 512
SSM_LC = 256
SSM_BLK = 4
COEF_LB_RE, COEF_LB_IM, COEF_LBLK_RE, COEF_LBLK_IM = 0, 1, 2, 3
TOK_TM = 512
FINAL_TM = 1024
EXP_UNIT = 128
EXP_CLASSES = 4
W_SPLIT = 4
DISPATCH_CHUNK = 80
COMBINE_CHUNK = 24

R_E1, R_E2, R_W1, R_W2, R_RANK1, R_RANK2, R_CODE1, R_CODE2 = 0, 1, 2, 3, 4, 5, 6, 7
CODE_BITS = 16
CODE_SHIFT = float(1 << CODE_BITS)

_INV_SQRT2 = 1.0 / math.sqrt(2.0)
_BF16 = jnp.bfloat16
_F32 = jnp.float32
_U32 = jnp.uint32


def _gelu(x):
    return 0.5 * x * (1.0 + lax.erf(x * _INV_SQRT2))


def _rms(x, g):
    return x * lax.rsqrt(jnp.mean(x * x, axis=-1, keepdims=True) + EPS) * g


def _dot(a, b):
    return jnp.dot(a, b, preferred_element_type=_F32)


def _dot_f32(a, b):
    return jnp.dot(a, b, preferred_element_type=_F32, precision=lax.Precision.HIGHEST)


def _pack_bf16_pair(x):
    w = x.shape[1] // 2
    hi = lax.bitcast_convert_type(x[:, :w].astype(_BF16).astype(_F32), _U32)
    lo = lax.bitcast_convert_type(x[:, w:].astype(_BF16).astype(_F32), _U32)
    return hi | (lo >> 16)


def _unpack_bf16_pair(p):
    hi = lax.bitcast_convert_type(p & jnp.uint32(0xFFFF0000), _F32)
    lo = lax.bitcast_convert_type(p << 16, _F32)
    return jnp.concatenate([hi, lo], axis=-1)


def _head_norm_gelu(vb, gn):
    v = _gelu(vb)
    parts = []
    for h in range(N_GMLP_HEADS):
        vh = v[:, h * GMLP_HEAD:(h + 1) * GMLP_HEAD]
        parts.append(vh * lax.rsqrt(jnp.mean(vh * vh, axis=-1, keepdims=True) + EPS))
    return jnp.concatenate(parts, axis=-1) * gn


def _front_prompt_kernel(x_ref, g1_ref, win_ref, gn_ref, ws_ref, bs_ref, gog_ref,
                         xa_ref, sg_ref, mixb_ref, win_bf):
    @pl.when((pl.program_id(0) == 0) & (pl.program_id(1) == 0))
    def _():
        win_bf[...] = win_ref[...].astype(_BF16)

    x = x_ref[0]
    hn = _rms(x, g1_ref[...]).astype(_BF16)
    z = _dot(hn, win_bf[...])
    xa_ref[0] = z[:, :D_SSM]
    sg_ref[0] = jax.nn.sigmoid(z[:, D_SSM:2 * D_SSM])
    ub = _gelu(z[:, 2 * D_SSM:2 * D_SSM + D_GMLP])
    vbn = _head_norm_gelu(z[:, 2 * D_SSM + D_GMLP:], gn_ref[...]).astype(_BF16)
    tl = x.shape[0]
    rows = []
    for c in range(tl // CHUNK):
        heads = []
        for h in range(N_GMLP_HEADS):
            vh = vbn[c * CHUNK:(c + 1) * CHUNK, h * GMLP_HEAD:(h + 1) * GMLP_HEAD]
            heads.append(_dot(ws_ref[h], vh) + bs_ref[:, h:h + 1])
        rows.append(jnp.concatenate(heads, axis=-1))
    s = jnp.concatenate(rows, axis=0)
    mixb_ref[0] = _rms(ub * s, gog_ref[...]).astype(_BF16)


def _front_prompt(x, g1, win, gn, ws_tril_bf, bs_t, gog):
    n, l, d = x.shape
    tl = FRONT_TL
    grid = (n, l // tl)
    const = lambda *shape: pl.BlockSpec(shape, lambda b, i: (0,) * len(shape))
    seq = lambda w: pl.BlockSpec((1, tl, w), lambda b, i: (b, i, 0))
    return pl.pallas_call(
        _front_prompt_kernel,
        grid=grid,
        in_specs=[seq(d), const(1, d), const(d, D_IN), const(1, D_GMLP),
                  const(N_GMLP_HEADS, CHUNK, CHUNK), const(CHUNK, N_GMLP_HEADS), const(1, D_GMLP)],
        out_specs=[seq(D_SSM), seq(D_SSM), seq(D_GMLP)],
        out_shape=[jax.ShapeDtypeStruct((n, l, D_SSM), _F32),
                   jax.ShapeDtypeStruct((n, l, D_SSM), _F32),
                   jax.ShapeDtypeStruct((n, l, D_GMLP), _BF16)],
        scratch_shapes=[pltpu.VMEM((d, D_IN), _BF16)],
        compiler_params=pltpu.CompilerParams(
            dimension_semantics=("arbitrary", "arbitrary"), vmem_limit_bytes=VMEM_LIMIT),
        name="front_prompt",
    )(x, g1, win, gn, ws_tril_bf, bs_t, gog)


def _ssm_prompt_kernel(xa_ref, sg_ref, v_ref, r_ref, coef_ref, dsk_ref, gos_ref,
                       mixa_ref, hfin_ref, s_ref, st_ref):
    lc = xa_ref.shape[1]
    nblk = lc // SSM_BLK
    rows = nblk * SUBLANES

    @pl.when(pl.program_id(0) == 0)
    def _():
        st_ref[...] = jnp.zeros_like(st_ref)

    def by_position(ref):
        t = pltpu.einshape("btc->tbc", ref[...]).reshape(nblk, SSM_BLK, SUBLANES, D_SSM)
        return [t[:, i].reshape(rows, D_SSM) for i in range(SSM_BLK)]

    xs = by_position(xa_ref)
    xs_bf = [x.astype(_BF16) for x in xs]
    xk = [jnp.concatenate([x[:, k * LANES:(k + 1) * LANES] for x in xs_bf], axis=-1)
          for k in range(N_LANE_TILES)]
    for k in range(N_LANE_TILES):
        s_ref[:, 2 * TILE_STATE * k:2 * TILE_STATE * (k + 1)] = _dot(xk[k], v_ref[k])

    for kk in range(0, N_LANE_TILES, 2):
        tiles = (kk, kk + 1)
        cols = [(2 * TILE_STATE * k, 2 * TILE_STATE * k + TILE_STATE) for k in tiles]
        lbs = [tuple(jnp.broadcast_to(coef_ref[row:row + 1, k * TILE_STATE:(k + 1) * TILE_STATE],
                                      (SUBLANES, TILE_STATE)) for row in (COEF_LBLK_RE, COEF_LBLK_IM))
               for k in tiles]

        def body(j, carry, cols=cols, lbs=lbs):
            r0 = pl.multiple_of(j * SUBLANES, SUBLANES)
            out = []
            for q, ((c_re, c_im), (lr, li)) in enumerate(zip(cols, lbs)):
                hr, hi = carry[2 * q], carry[2 * q + 1]
                sr = s_ref[pl.ds(r0, SUBLANES), c_re:c_re + TILE_STATE]
                si = s_ref[pl.ds(r0, SUBLANES), c_im:c_im + TILE_STATE]
                s_ref[pl.ds(r0, SUBLANES), c_re:c_re + TILE_STATE] = hr
                s_ref[pl.ds(r0, SUBLANES), c_im:c_im + TILE_STATE] = hi
                out += [lr * hr - li * hi + sr, lr * hi + li * hr + si]
            return tuple(out)

        init = tuple(st_ref[:, c:c + TILE_STATE] for c_pair in cols for c in c_pair)
        fin = lax.fori_loop(0, nblk, body, init, unroll=2)
        for q, (c_re, c_im) in enumerate(cols):
            st_ref[:, c_re:c_re + TILE_STATE] = fin[2 * q]
            st_ref[:, c_im:c_im + TILE_STATE] = fin[2 * q + 1]

    yk = []
    for k in range(N_LANE_TILES):
        h_in = s_ref[:, 2 * TILE_STATE * k:2 * TILE_STATE * (k + 1)].astype(_BF16)
        yk.append(_dot(jnp.concatenate([h_in, xk[k]], axis=-1), r_ref[k]))
    sgs = by_position(sg_ref)
    outs = []
    for i in range(SSM_BLK):
        y = jnp.concatenate([y_k[:, i * LANES:(i + 1) * LANES] for y_k in yk], axis=-1) + dsk_ref[...] * xs[i]
        outs.append(_rms(_gelu(y) * sgs[i], gos_ref[...]).reshape(nblk, SUBLANES, D_SSM))
    mixa = jnp.stack(outs, axis=1).reshape(lc, SUBLANES, D_SSM)
    mixa_ref[...] = pltpu.einshape("tbc->btc", mixa).astype(_BF16)
    hfin_ref[...] = st_ref[...]


def _ssm_prompt(xa, sg, v, r, coef, dsk, gos):
    n, l, _ = xa.shape
    lc = SSM_LC
    const = lambda *shape: pl.BlockSpec(shape, lambda i: (0,) * len(shape))
    seq_spec = pl.BlockSpec((n, lc, D_SSM), lambda i: (0, i, 0))
    return pl.pallas_call(
        _ssm_prompt_kernel,
        grid=(l // lc,),
        in_specs=[seq_spec, seq_spec, const(*v.shape), const(*r.shape),
                  const(*coef.shape), const(1, D_SSM), const(1, D_SSM)],
        out_specs=[seq_spec, const(n, 2 * STATE_COLS)],
        out_shape=[jax.ShapeDtypeStruct((n, l, D_SSM), _BF16),
                   jax.ShapeDtypeStruct((n, 2 * STATE_COLS), _F32)],
        scratch_shapes=[pltpu.VMEM((lc // SSM_BLK * n, 2 * STATE_COLS), _F32),
                        pltpu.VMEM((n, 2 * STATE_COLS), _F32)],
        compiler_params=pltpu.CompilerParams(
            dimension_semantics=("arbitrary",), vmem_limit_bytes=VMEM_LIMIT),
        name="ssm_prompt",
    )(xa, sg, v, r, coef, dsk, gos)


def _front_sample_kernel(x_ref, g1_ref, win_ref, gn_ref, w00_ref, b0_ref, gog_ref,
                         wb_ref, wc_ref, coef_ref, dsk_ref, gos_ref, h0r_ref, h0i_ref,
                         mix_ref, hr_ref, hi_ref, vrow_ref):
    x = x_ref[...]
    hn = _rms(x, g1_ref[...])
    z = _dot_f32(hn, win_ref[...])
    xa = z[:, :D_SSM]
    ys = []
    for k in range(N_LANE_TILES):
        bu = _dot_f32(xa[:, k * LANES:(k + 1) * LANES], wb_ref[k])
        sl = slice(k * TILE_STATE, (k + 1) * TILE_STATE)
        lr, li = coef_ref[COEF_LB_RE:COEF_LB_RE + 1, sl], coef_ref[COEF_LB_IM:COEF_LB_IM + 1, sl]
        h0r, h0i = h0r_ref[:, sl], h0i_ref[:, sl]
        nr = lr * h0r - li * h0i + bu[:, :TILE_STATE]
        ni = lr * h0i + li * h0r + bu[:, TILE_STATE:]
        hr_ref[:, sl] = nr
        hi_ref[:, sl] = ni
        ys.append(_dot_f32(jnp.concatenate([nr, ni], axis=-1), wc_ref[k]))
    y = jnp.concatenate(ys, axis=-1) + dsk_ref[...] * xa
    ya = _gelu(y) * jax.nn.sigmoid(z[:, D_SSM:2 * D_SSM])
    mix_ref[:, :D_SSM] = _rms(ya, gos_ref[...])
    ub = _gelu(z[:, 2 * D_SSM:2 * D_SSM + D_GMLP])
    vbn = _head_norm_gelu(z[:, 2 * D_SSM + D_GMLP:], gn_ref[...])
    vrow_ref[...] = vbn
    s = w00_ref[...] * vbn + b0_ref[...]
    mix_ref[:, D_SSM:] = _rms(ub * s, gog_ref[...])


def _front_sample(x, g1, win, gn, w00, b0, gog, wb, wc, coef, dsk, gos, h0r, h0i):
    n = x.shape[0]
    vmem = pl.BlockSpec(memory_space=pltpu.VMEM)
    return pl.pallas_call(
        _front_sample_kernel,
        in_specs=[vmem] * 14,
        out_specs=[vmem] * 4,
        out_shape=[jax.ShapeDtypeStruct((n, D_MODEL), _F32),
                   jax.ShapeDtypeStruct((n, STATE_COLS), _F32),
                   jax.ShapeDtypeStruct((n, STATE_COLS), _F32),
                   jax.ShapeDtypeStruct((n, D_GMLP), _F32)],
        compiler_params=pltpu.CompilerParams(vmem_limit_bytes=VMEM_LIMIT),
        name="front_sample",
    )(x, g1, win, gn, w00, b0, gog, wb, wc, coef, dsk, gos, h0r, h0i)


def _route(logits, base):
    tm = logits.shape[0]
    lt = logits.T
    ex = lt[:N_EXPERTS, :]
    gr = lt[N_EXPERTS:N_EXPERTS + SUBLANES, :]
    row_e = lax.broadcasted_iota(jnp.int32, ex.shape, 0).astype(_F32)
    row_g = lax.broadcasted_iota(jnp.int32, gr.shape, 0).astype(_F32)
    neg = jnp.float32(-jnp.inf)
    big = jnp.float32(LANES)
    is_g = row_g < N_EXPERT_GROUPS
    gl = jnp.where(is_g, gr, neg)
    gmax = jnp.max(gl, axis=0, keepdims=True)
    gi = jnp.min(jnp.where(gl == gmax, row_g, big), axis=0, keepdims=True)
    p_top = 1.0 / jnp.sum(jnp.where(is_g, jnp.exp(gl - gmax), 0.0), axis=0, keepdims=True)
    lo = gi * EXPERTS_PER_GROUP
    in_grp = (row_e >= lo) & (row_e < lo + EXPERTS_PER_GROUP)
    m1 = jnp.max(jnp.where(in_grp, ex, neg), axis=0, keepdims=True)
    i1 = jnp.min(jnp.where(in_grp & (ex == m1), row_e, big), axis=0, keepdims=True)
    rest = in_grp & (row_e != i1)
    m2 = jnp.max(jnp.where(rest, ex, neg), axis=0, keepdims=True)
    i2 = jnp.min(jnp.where(rest & (ex == m2), row_e, big), axis=0, keepdims=True)
    e2 = jnp.exp(m2 - m1)
    w1 = p_top / (1.0 + e2)
    w2 = p_top * e2 / (1.0 + e2)
    sel1 = row_e == i1
    sel2 = row_e == i2
    hits = jnp.where(sel1 | sel2, 1.0, 0.0)
    src = lax.broadcasted_iota(jnp.int32, (tm, tm), 0)
    dst = lax.broadcasted_iota(jnp.int32, (tm, tm), 1)
    before = _dot(hits.astype(_BF16), jnp.where(src < dst, 1.0, 0.0).astype(_BF16)) + base
    rank1 = jnp.sum(jnp.where(sel1, before, 0.0), axis=0, keepdims=True)
    rank2 = jnp.sum(jnp.where(sel2, before, 0.0), axis=0, keepdims=True)
    fields = {R_E1: i1, R_E2: i2, R_W1: w1, R_W2: w2, R_RANK1: rank1, R_RANK2: rank2,
              R_CODE1: i1 * CODE_SHIFT + rank1, R_CODE2: i2 * CODE_SHIFT + rank2}
    row8 = lax.broadcasted_iota(jnp.int32, (SUBLANES, tm), 0)
    route_t = jnp.zeros((SUBLANES, tm), _F32)
    for r, val in fields.items():
        route_t = jnp.where(row8 == r, val, route_t)
    route = jnp.concatenate([route_t, jnp.zeros((LANES - SUBLANES, tm), _F32)], axis=0).T
    return route_t, route, base + jnp.sum(hits, axis=1, keepdims=True)


def _mixer_out_prompt_kernel(x_ref, mixa_ref, mixb_ref, wo_ref, g2_ref, wr_ref, br_ref,
                             x1_ref, xn_ref, route_ref, route_t_ref, cnt_ref, base_ref, logits_ref, wo_bf):
    i = pl.program_id(0)

    @pl.when(i == 0)
    def _():
        base_ref[...] = jnp.zeros_like(base_ref)
        logits_ref[...] = jnp.zeros_like(logits_ref)
        wo_bf[...] = wo_ref[...].astype(_BF16)

    prev_logits = logits_ref[...]
    x1 = x_ref[0] + _dot(mixa_ref[0], wo_bf[:D_SSM, :]) + _dot(mixb_ref[0], wo_bf[D_SSM:, :])
    xn = _rms(x1, g2_ref[...])
    x1_ref[...] = x1
    xn_ref[...] = _pack_bf16_pair(xn)
    logits_ref[...] = _dot(xn.astype(_BF16), wr_ref[...]) + br_ref[...]
    route_t, route, base = _route(prev_logits, base_ref[...])
    route_ref[...] = route
    route_t_ref[...] = route_t
    base = jnp.where(i >= 1, base, base_ref[...])
    base_ref[...] = base
    cnt_ref[...] = base


def _mixer_out_sample_kernel(x_ref, mix_ref, wo_ref, g2_ref, wr_ref, br_ref, cnt_in_ref,
                             x1_in, xn_in, route_in, route_t_in,
                             x1_ref, xn_ref, route_ref, route_t_ref, cnt_ref):
    del x1_in, xn_in, route_in, route_t_in
    x1 = (x_ref[...] + _dot_f32(mix_ref[:, :D_SSM], wo_ref[:D_SSM, :])
          + _dot_f32(mix_ref[:, D_SSM:], wo_ref[D_SSM:, :]))
    xn = _rms(x1, g2_ref[...])
    logits = _dot_f32(xn, wr_ref[...]) + br_ref[...]
    route_t, route, base = _route(logits, cnt_in_ref[...])
    x1_ref[...] = x1
    xn_ref[...] = _pack_bf16_pair(xn)
    route_ref[...] = route
    route_t_ref[...] = route_t
    cnt_ref[...] = base


def _mixer_out(x_p, mixa, mixb, x_s, mix_s, wo, g2, wr, br):
    n, l, d = x_p.shape
    ns = x_s.shape[0]
    t_all = n * l + ns
    tm = TOK_TM
    per_seq = l // tm
    n_tiles = n * per_seq
    cur = lambda i: jnp.minimum(i, n_tiles - 1)
    prev = lambda i: jnp.maximum(i - 1, 0)
    const = lambda *shape: pl.BlockSpec(shape, lambda i: (0,) * len(shape))
    seq = lambda w: pl.BlockSpec((1, tm, w), lambda i: (cur(i) // per_seq, cur(i) % per_seq, 0))
    tok = lambda w, which: pl.BlockSpec((tm, w), lambda i: (which(i), 0))
    tok_shapes = [jax.ShapeDtypeStruct((t_all, d), _F32),
                  jax.ShapeDtypeStruct((t_all, d // 2), _U32),
                  jax.ShapeDtypeStruct((t_all, LANES), _F32),
                  jax.ShapeDtypeStruct((SUBLANES, t_all), _F32)]
    cnt_shape = jax.ShapeDtypeStruct((N_EXPERTS, 1), _F32)
    x1, xn, route, route_t, cnt = pl.pallas_call(
        _mixer_out_prompt_kernel,
        grid=(n_tiles + 1,),
        in_specs=[seq(d), seq(D_SSM), seq(D_GMLP),
                  const(d, d), const(1, d), const(d, LANES), const(1, LANES)],
        out_specs=[tok(d, cur), tok(d // 2, cur), tok(LANES, prev),
                   pl.BlockSpec((SUBLANES, tm), lambda i: (0, prev(i))), const(N_EXPERTS, 1)],
        out_shape=tok_shapes + [cnt_shape],
        scratch_shapes=[pltpu.VMEM((N_EXPERTS, 1), _F32), pltpu.VMEM((tm, LANES), _F32),
                        pltpu.VMEM((d, d), _BF16)],
        compiler_params=pltpu.CompilerParams(
            dimension_semantics=("arbitrary",), vmem_limit_bytes=VMEM_LIMIT),
        name="mixer_out_prompt",
    )(x_p, mixa, mixb, wo, g2, wr.astype(_BF16), br)
    tail = (n * l) // ns
    c1 = lambda *shape: pl.BlockSpec(shape, lambda i: (0,) * len(shape))
    anyspec = pl.BlockSpec(memory_space=pl.ANY)
    tail_spec = lambda w: pl.BlockSpec((ns, w), lambda i: (tail, 0))
    return pl.pallas_call(
        _mixer_out_sample_kernel,
        grid=(1,),
        in_specs=[c1(ns, d), c1(ns, d), c1(d, d), c1(1, d), c1(d, LANES), c1(1, LANES), c1(N_EXPERTS, 1),
                  anyspec, anyspec, anyspec, anyspec],
        out_specs=[tail_spec(d), tail_spec(d // 2), tail_spec(LANES),
                   pl.BlockSpec((SUBLANES, ns), lambda i: (0, tail)), c1(N_EXPERTS, 1)],
        out_shape=tok_shapes + [cnt_shape],
        input_output_aliases={7: 0, 8: 1, 9: 2, 10: 3},
        compiler_params=pltpu.CompilerParams(
            dimension_semantics=("arbitrary",), vmem_limit_bytes=VMEM_LIMIT),
        name="mixer_out_sample",
    )(x_s, mix_s, wo, g2, wr, br, cnt, x1, xn, route, route_t)


def _sc_stream(n_chunks, gather, write):
    gather(0).start()
    for j in range(n_chunks):
        if j + 1 < n_chunks:
            if j >= 1:
                write(j - 1).wait()
            gather(j + 1).start()
        gather(j).wait()
        write(j).start()
    if n_chunks >= 2:
        write(n_chunks - 2).wait()
    write(n_chunks - 1).wait()


def _sc_mesh():
    return plsc.VectorSubcoreMesh(core_axis_name="c", subcore_axis_name="s",
                                  num_cores=SC_CORES, num_subcores=SC_SUBCORES)


def _sc_buffers(chunk, w, dtype):
    return [pltpu.VMEM((chunk, w), dtype), pltpu.VMEM((chunk, w), dtype)] + [pltpu.SemaphoreType.DMA] * 4


def _sc_combine(table, idx, n_out, chunk):
    w = table.shape[1]
    rows_w = n_out // SC_WORKERS
    n_chunks = rows_w // chunk
    assert rows_w * SC_WORKERS == n_out and n_chunks * chunk == rows_w and rows_w % SUBLANES == 0

    def body(table_hbm, idx_hbm, out_hbm, idx_v, buf0, buf1, g0, g1, w0, w1):
        wid = lax.axis_index("s") * SC_CORES + lax.axis_index("c")
        base = pl.multiple_of(wid * rows_w, SUBLANES)
        pltpu.sync_copy(idx_hbm.at[pl.ds(base, rows_w)], idx_v)
        bufs, gsems, wsems = (buf0, buf1), (g0, g1), (w0, w1)

        def gather(j):
            return pltpu.make_async_copy(table_hbm.at[idx_v.at[pl.ds(j * chunk, chunk)]], bufs[j % 2], gsems[j % 2])

        def write(j):
            return pltpu.make_async_copy(bufs[j % 2], out_hbm.at[pl.ds(base + j * chunk, chunk)], wsems[j % 2])

        _sc_stream(n_chunks, gather, write)

    return pl.kernel(
        body,
        out_type=jax.ShapeDtypeStruct((n_out, w), table.dtype),
        mesh=_sc_mesh(),
        scratch_types=[pltpu.VMEM((rows_w,), jnp.int32)] + _sc_buffers(chunk, w, table.dtype),
        compiler_params=pltpu.CompilerParams(use_tc_tiling_on_sc=True),
        name="sc_combine",
    )(table, idx)


def _sc_dispatch(table, codes, start_row, n_out, chunk):
    t_all, w = table.shape
    n_pad = codes.shape[0]
    n_ent = 2 * t_all
    ent_w = n_pad // SC_WORKERS
    n_chunks = ent_w // chunk
    per_chunk = chunk // SC_LANES
    trash = n_out - (n_pad - n_ent)
    assert ent_w * SC_WORKERS == n_pad and n_chunks * chunk == ent_w
    assert per_chunk * SC_LANES == chunk and chunk <= LANES and n_pad - n_ent <= t_all

    def body(table_hbm, code_hbm, start_hbm, out_hbm, dest_hbm,
             code_v, dest_v, tok_v, dst_v, start_v, buf0, buf1, g0, g1, w0, w1):
        wid = lax.axis_index("s") * SC_CORES + lax.axis_index("c")
        ebase = pl.multiple_of(wid * ent_w, SUBLANES)
        pltpu.sync_copy(code_hbm.at[pl.ds(ebase, ent_w)], code_v)
        pltpu.sync_copy(start_hbm, start_v)
        lane = lax.iota(jnp.int32, SC_LANES)
        for j in range(n_chunks):
            for c in range(per_chunk):
                off = j * chunk + c * SC_LANES
                ent = ebase + off + lane
                code = code_v[pl.ds(off, SC_LANES)]
                d = plsc.load_gather(start_v, [code >> CODE_BITS]) + (code & ((1 << CODE_BITS) - 1))
                d = jnp.where(ent >= n_ent, trash + (ent - n_ent), d)
                tok = jnp.where(ent >= t_all, ent - t_all, ent)
                tok = jnp.where(tok >= t_all, tok - t_all, tok)
                dest_v[pl.ds(off, SC_LANES)] = d
                dst_v[j, pl.ds(c * SC_LANES, SC_LANES)] = d
                tok_v[j, pl.ds(c * SC_LANES, SC_LANES)] = tok
        pltpu.sync_copy(dest_v, dest_hbm.at[pl.ds(ebase, ent_w)])
        bufs, gsems, wsems = (buf0, buf1), (g0, g1), (w0, w1)

        def gather(j):
            return pltpu.make_async_copy(table_hbm.at[tok_v.at[j]], bufs[j % 2], gsems[j % 2])

        def scatter(j):
            return pltpu.make_async_copy(bufs[j % 2], out_hbm.at[dst_v.at[j]], wsems[j % 2])

        _sc_stream(n_chunks, gather, scatter)

    return pl.kernel(
        body,
        out_type=(jax.ShapeDtypeStruct((n_out, w), table.dtype), jax.ShapeDtypeStruct((n_pad,), jnp.int32)),
        mesh=_sc_mesh(),
        scratch_types=([pltpu.VMEM((ent_w,), jnp.int32), pltpu.VMEM((ent_w,), jnp.int32),
                        pltpu.VMEM((n_chunks, chunk), jnp.int32), pltpu.VMEM((n_chunks, chunk), jnp.int32),
                        pltpu.VMEM((LANES,), jnp.int32)] + _sc_buffers(chunk, w, table.dtype)),
        compiler_params=pltpu.CompilerParams(use_tc_tiling_on_sc=True, needs_layout_passes=False),
        name="sc_dispatch",
    )(table, codes, start_row)


def _experts_kernel(piece_start_ref, piece_row_ref, piece_cls_ref, *refs):
    n_w = 3 * W_SPLIT
    wg_refs, wu_refs, wd_refs = refs[:W_SPLIT], refs[W_SPLIT:2 * W_SPLIT], refs[2 * W_SPLIT:n_w]
    xs_hbm, ys_hbm, wg_bf, wu_bf, wd_bf, xbuf, ybuf, xsem, ysem = refs[n_w:]
    e = pl.program_id(0)
    g0 = piece_start_ref[e]
    n_here = piece_start_ref[e + 1] - g0
    n_total = piece_start_ref[N_EXPERTS]

    def per_class(g, fn):
        cls = piece_cls_ref[g]
        row = pl.multiple_of(piece_row_ref[g], EXP_UNIT)
        for c in range(1, EXP_CLASSES + 1):
            pl.when(cls == c)(lambda c=c: fn(c * EXP_UNIT, row))

    def x_copy(slot, rows, row):
        return pltpu.make_async_copy(xs_hbm.at[pl.ds(row, rows)], xbuf.at[slot, pl.ds(0, rows)], xsem.at[slot])

    def y_copy(slot, rows, row):
        return pltpu.make_async_copy(ybuf.at[slot, pl.ds(0, rows)], ys_hbm.at[pl.ds(row, rows)], ysem.at[slot])

    @pl.when((e == 0) & (n_total > 0))
    def _():
        per_class(0, lambda rows, row: x_copy(0, rows, row).start())

    for dst, chunks in ((wg_bf, wg_refs), (wu_bf, wu_refs), (wd_bf, wd_refs)):
        rows = dst.shape[0] // W_SPLIT
        for q, src in enumerate(chunks):
            dst[q * rows:(q + 1) * rows, :] = src[0, 0].astype(_BF16)

    def piece(j, carry):
        g = g0 + j
        slot = lax.rem(g, 2)
        per_class(g, lambda rows, row: x_copy(slot, rows, row).wait())

        @pl.when(g + 1 < n_total)
        def _():
            per_class(g + 1, lambda rows, row: x_copy(1 - slot, rows, row).start())

        @pl.when(g >= 2)
        def _():
            per_class(g - 2, lambda rows, row: y_copy(slot, rows, row).wait())

        def compute(rows, row):
            x = _unpack_bf16_pair(xbuf[slot, pl.ds(0, rows)]).astype(_BF16)
            a = _dot(x, wg_bf[...])
            u = _dot(x, wu_bf[...])
            h = (a * jax.nn.sigmoid(a) * u).astype(_BF16)
            ybuf[slot, pl.ds(0, rows)] = _pack_bf16_pair(_dot(h, wd_bf[...]))
            y_copy(slot, rows, row).start()

        per_class(g, compute)
        return carry

    lax.fori_loop(0, n_here, piece, 0)

    @pl.when(e == N_EXPERTS - 1)
    def _():
        @pl.when(n_total >= 2)
        def _():
            per_class(n_total - 2, lambda rows, row: y_copy(lax.rem(n_total, 2), rows, row).wait())

        @pl.when(n_total >= 1)
        def _():
            per_class(n_total - 1, lambda rows, row: y_copy(lax.rem(n_total - 1, 2), rows, row).wait())


def _experts(piece_start, piece_row, piece_cls, n_rows, xs, w_gate, w_up, w_down):
    dh = xs.shape[1]
    d = 2 * dh
    tm = EXP_UNIT * EXP_CLASSES
    anyspec = pl.BlockSpec(memory_space=pl.ANY)

    def chunk_specs(rows, cols):
        return [pl.BlockSpec((1, 1, rows // W_SPLIT, cols), lambda e, ps, pr, pc, q=q: (e, q, 0, 0))
                for q in range(W_SPLIT)]

    split = lambda w: w.reshape(w.shape[0], W_SPLIT, w.shape[1] // W_SPLIT, w.shape[2])
    grid_spec = pltpu.PrefetchScalarGridSpec(
        num_scalar_prefetch=3,
        grid=(N_EXPERTS,),
        in_specs=(chunk_specs(d, D_EXPERT) + chunk_specs(d, D_EXPERT) + chunk_specs(D_EXPERT, d) + [anyspec]),
        out_specs=anyspec,
        scratch_shapes=[pltpu.VMEM((d, D_EXPERT), _BF16), pltpu.VMEM((d, D_EXPERT), _BF16),
                        pltpu.VMEM((D_EXPERT, d), _BF16),
                        pltpu.VMEM((2, tm, dh), _U32), pltpu.VMEM((2, tm, dh), _U32),
                        pltpu.SemaphoreType.DMA((2,)), pltpu.SemaphoreType.DMA((2,))],
    )
    return pl.pallas_call(
        _experts_kernel,
        grid_spec=grid_spec,
        out_shape=jax.ShapeDtypeStruct((n_rows, dh), _U32),
        compiler_params=pltpu.CompilerParams(
            dimension_semantics=("arbitrary",), vmem_limit_bytes=VMEM_LIMIT),
        name="experts",
    )(piece_start, piece_row, piece_cls, *([split(w_gate)] * W_SPLIT), *([split(w_up)] * W_SPLIT),
      *([split(w_down)] * W_SPLIT), xs)


def _final_kernel(x1_ref, ya_ref, yb_ref, route_ref, gf_ref, y_ref):
    route = route_ref[...]
    x2 = (x1_ref[...] + route[:, R_W1:R_W1 + 1] * _unpack_bf16_pair(ya_ref[...])
          + route[:, R_W2:R_W2 + 1] * _unpack_bf16_pair(yb_ref[...]))
    y_ref[...] = _rms(x2, gf_ref[...])


def _final(x1, yab, route, gf, n_prompt, n_sample):
    d = x1.shape[1]

    def call(tm, first_block, n_rows, name):
        tok = lambda w: pl.BlockSpec((tm, w), lambda i: (first_block + i, 0))
        sel = lambda k: pl.BlockSpec((None, tm, d // 2), lambda i: (k, first_block + i, 0))
        return pl.pallas_call(
            _final_kernel,
            grid=(n_rows // tm,),
            in_specs=[tok(d), sel(0), sel(1), tok(LANES), pl.BlockSpec((1, d), lambda i: (0, 0))],
            out_specs=pl.BlockSpec((tm, d), lambda i: (i, 0)),
            out_shape=jax.ShapeDtypeStruct((n_rows, d), _F32),
            compiler_params=pltpu.CompilerParams(
                dimension_semantics=("arbitrary",), vmem_limit_bytes=VMEM_LIMIT),
            name=name,
        )(x1, yab, yab, route, gf)

    return (call(FINAL_TM, 0, n_prompt, "final_prompt"),
            call(n_sample, n_prompt // n_sample, n_sample, "final_sample"))


def _powers(lam_re, lam_im, dt):
    out = []
    for m in range(SSM_BLK + 1):
        mag = jnp.exp(m * lam_re * dt)
        ang = m * lam_im * dt
        out.append((mag * jnp.cos(ang), mag * jnp.sin(ang)))
    return out


def _spread(x, copies):
    w = x.shape[1]
    src = lax.broadcasted_iota(jnp.int32, (w, w * copies), 0)
    dst = lax.broadcasted_iota(jnp.int32, (w, w * copies), 1)
    return _dot_f32(x, jnp.where(dst % w == src, 1.0, 0.0))


def _ssm_prep_kernel(lam_ref, b_re, b_im, c_re, c_im, v_ref, r_ref, wb_ref, wc_ref, coef_ref):
    n_p, n_h = SSM_STATE, SSM_GROUP
    lr, li, dt = lam_ref[0:1, :], lam_ref[1:2, :], lam_ref[2:3, :]
    pw = _powers(lr, li, dt)
    den = lr * lr + li * li
    nr, ni = pw[1][0] - 1.0, pw[1][1]
    k_re = (nr * lr + ni * li) / den
    k_im = (ni * lr - nr * li) / den
    coef_ref[...] = jnp.concatenate(
        [pw[1][0], pw[1][1], pw[SSM_BLK][0], pw[SSM_BLK][1], jnp.zeros((SUBLANES - 4, TILE_STATE), _F32)], axis=0)

    on_diag_b = (lax.broadcasted_iota(jnp.int32, (TILE_STATE, LANES), 0) // n_p
                 == lax.broadcasted_iota(jnp.int32, (TILE_STATE, LANES), 1) // n_h)
    rows_gp = lambda ref: ref[...].reshape(TILE_STATE, n_h)
    bt_re = jnp.where(on_diag_b, _spread(rows_gp(b_re), SUBLANES), 0.0).T
    bt_im = jnp.where(on_diag_b, _spread(rows_gp(b_im), SUBLANES), 0.0).T
    bb_re = k_re * bt_re - k_im * bt_im
    bb_im = k_re * bt_im + k_im * bt_re
    wb_ref[0] = jnp.concatenate([bb_re, bb_im], axis=1)
    v_rows = []
    for s in range(SSM_BLK):
        pr, pi = pw[SSM_BLK - 1 - s]
        v_rows.append(jnp.concatenate([pr * bb_re - pi * bb_im, pr * bb_im + pi * bb_re], axis=1))
    v_ref[0] = jnp.concatenate(v_rows, axis=0).astype(v_ref.dtype)

    on_diag_c = (lax.broadcasted_iota(jnp.int32, (LANES, TILE_STATE), 0) // n_h
                 == lax.broadcasted_iota(jnp.int32, (LANES, TILE_STATE), 1) // n_p)
    rows_gh = lambda ref: ref[...].reshape(LANES, n_p)
    ct_re = jnp.where(on_diag_c, _spread(rows_gh(c_re), SUBLANES), 0.0)
    ct_im = jnp.where(on_diag_c, _spread(rows_gh(c_im), SUBLANES), 0.0)
    cl = [(ct_re * pr - ct_im * pi, ct_re * pi + ct_im * pr) for pr, pi in pw]
    wc_ref[0] = jnp.concatenate([cl[0][0], -cl[0][1]], axis=1).T
    nt = lambda a, b: lax.dot_general(a, b, (((1,), (1,)), ((), ())), precision=lax.Precision.HIGHEST,
                                      preferred_element_type=_F32)
    direct = [nt(cl[m][0], bb_re) - nt(cl[m][1], bb_im) for m in range(SSM_BLK)]
    zero = jnp.zeros((LANES, LANES), _F32)
    rt = jnp.concatenate(
        [jnp.concatenate([cl[i + 1][0], -cl[i + 1][1]]
                         + [direct[i - s] if s <= i else zero for s in range(SSM_BLK)], axis=1)
         for i in range(SSM_BLK)], axis=0)
    r_ref[0] = rt.T.astype(r_ref.dtype)


def _ssm_params(lam_re, lam_im, log_dt, b_re, b_im, c_re, c_im, d_skip):
    n_g, n_p, n_h = N_SSM_GROUPS, SSM_STATE, SSM_GROUP
    dt = jnp.repeat(jnp.exp(log_dt), n_p)
    lam = jnp.zeros((SUBLANES, STATE_COLS), _F32).at[0].set(lam_re.reshape(-1)).at[1].set(
        lam_im.reshape(-1)).at[2].set(dt)
    groups = lambda r, c: pl.BlockSpec((SUBLANES, r, c), lambda k: (k, 0, 0))
    out3 = lambda rows, w: pl.BlockSpec((1, rows, w), lambda k: (k, 0, 0))
    cols = pl.BlockSpec((SUBLANES, TILE_STATE), lambda k: (0, k))
    k_blk = SSM_BLK * LANES
    v, r, wb, wc, coef = pl.pallas_call(
        _ssm_prep_kernel,
        grid=(N_LANE_TILES,),
        in_specs=[cols, groups(n_p, n_h), groups(n_p, n_h), groups(n_h, n_p), groups(n_h, n_p)],
        out_specs=[out3(k_blk, 2 * TILE_STATE), out3(2 * TILE_STATE + k_blk, k_blk),
                   out3(LANES, 2 * TILE_STATE), out3(2 * TILE_STATE, LANES), cols],
        out_shape=[jax.ShapeDtypeStruct((N_LANE_TILES, k_blk, 2 * TILE_STATE), _BF16),
                   jax.ShapeDtypeStruct((N_LANE_TILES, 2 * TILE_STATE + k_blk, k_blk), _BF16),
                   jax.ShapeDtypeStruct((N_LANE_TILES, LANES, 2 * TILE_STATE), _F32),
                   jax.ShapeDtypeStruct((N_LANE_TILES, 2 * TILE_STATE, LANES), _F32),
                   jax.ShapeDtypeStruct((SUBLANES, STATE_COLS), _F32)],
        compiler_params=pltpu.CompilerParams(
            dimension_semantics=("arbitrary",), vmem_limit_bytes=VMEM_LIMIT),
        name="ssm_prep",
    )(lam, b_re, b_im, c_re, c_im)
    return wb, wc, v, r, coef, d_skip.reshape(1, D_SSM)


def _dispatch_plan(route_t, cnt):
    t_all = route_t.shape[1]
    codes = route_t[R_CODE1:R_CODE2 + 1].astype(jnp.int32).reshape(-1)
    per_pass = SC_WORKERS * DISPATCH_CHUNK
    codes = jnp.pad(codes, (0, -(2 * t_all) % per_pass))
    counts = cnt[:, 0].astype(jnp.int32)
    zero = jnp.zeros((1,), jnp.int32)
    units = (counts + EXP_UNIT - 1) // EXP_UNIT
    unit_start = jnp.concatenate([zero, jnp.cumsum(units)])
    start_row = jnp.zeros((LANES,), jnp.int32).at[:N_EXPERTS].set(unit_start[:N_EXPERTS] * EXP_UNIT)
    pieces = (units + EXP_CLASSES - 1) // EXP_CLASSES
    piece_start = jnp.concatenate([zero, jnp.cumsum(pieces)])
    tm = EXP_UNIT * EXP_CLASSES
    max_units = (2 * t_all + N_EXPERTS * (EXP_UNIT - 1)) // EXP_UNIT
    max_pieces = (max_units + N_EXPERTS * (EXP_CLASSES - 1)) // EXP_CLASSES
    g = jnp.arange(max_pieces, dtype=jnp.int32)
    owner = ((g[:, None] >= piece_start[None, :-1]) & (g[:, None] < piece_start[None, 1:])).astype(jnp.int32)
    pick = lambda table: jnp.sum(owner * table[None, :], axis=1)
    first_unit = pick(unit_start[:-1]) + (g - pick(piece_start[:-1])) * EXP_CLASSES
    piece_row = first_unit * EXP_UNIT
    piece_cls = jnp.clip(pick(unit_start[1:]) - first_unit, 1, EXP_CLASSES)
    n_rows = (max_units * EXP_UNIT + tm - 1) // tm * tm + tm
    return codes, start_row, n_rows, piece_start, piece_row, piece_cls


def kernel(x_prompt, x_sample, state_ssm_re, state_ssm_im, norm1_g, w_in, lam_re, lam_im, log_dt, ssm_b_re, ssm_b_im, ssm_c_re, ssm_c_im, ssm_d, gmlp_norm_g, gmlp_w_s, gmlp_b_s, out_norm_ssm_g, out_norm_gmlp_g, w_out, norm2_g, w_router_group, b_router_group, w_router_expert, b_router_expert, w_gate, w_up, w_down, final_norm_g):
    n, l, d = x_prompt.shape
    ns = x_sample.shape[0]
    t_all = n * l + ns
    li = 0
    g1 = norm1_g[li].reshape(1, d)
    gn = gmlp_norm_g[li].reshape(1, D_GMLP)
    tril = jnp.tril(jnp.ones((CHUNK, CHUNK), dtype=bool))
    ws_tril = jnp.where(tril[None], gmlp_w_s[li], 0.0)
    bs = gmlp_b_s[li]
    gog = out_norm_gmlp_g[li].reshape(1, D_GMLP)
    gos = out_norm_ssm_g[li].reshape(1, D_SSM)
    wb, wc, v_blk, r_blk, coef, dsk = _ssm_params(
        lam_re[li], lam_im[li], log_dt[li], ssm_b_re[li], ssm_b_im[li], ssm_c_re[li], ssm_c_im[li], ssm_d[li])
    g2 = norm2_g[li].reshape(1, d)
    pad = LANES - N_EXPERTS - N_EXPERT_GROUPS
    wr = jnp.concatenate([w_router_expert[li], w_router_group[li], jnp.zeros((d, pad), _F32)], axis=1)
    br = jnp.concatenate([b_router_expert[li], b_router_group[li], jnp.zeros((pad,), _F32)]).reshape(1, LANES)

    xa, sg, mixb = _front_prompt(x_prompt, g1, w_in[li], gn, ws_tril.astype(_BF16), bs.T, gog)
    mixa, hfin = _ssm_prompt(xa, sg, v_blk, r_blk, coef, dsk, gos)
    w00 = jnp.repeat(ws_tril[:, 0, 0], GMLP_HEAD).reshape(1, D_GMLP)
    b0 = jnp.repeat(bs[:, 0], GMLP_HEAD).reshape(1, D_GMLP)
    mix_s, hr_s, hi_s, vrow = _front_sample(
        x_sample.reshape(ns, d), g1, w_in[li], gn, w00, b0, gog, wb, wc, coef, dsk, gos,
        state_ssm_re[li].reshape(ns, STATE_COLS), state_ssm_im[li].reshape(ns, STATE_COLS))

    x1, xn, route, route_t, cnt = _mixer_out(x_prompt, mixa, mixb, x_sample.reshape(ns, d), mix_s,
                                             w_out[li], g2, wr, br)
    codes, start_row, n_rows, piece_start, piece_row, piece_cls = _dispatch_plan(route_t, cnt)
    xs, dest = _sc_dispatch(xn, codes, start_row, n_rows, DISPATCH_CHUNK)
    ys = _experts(piece_start, piece_row, piece_cls, n_rows, xs, w_gate[li], w_up[li], w_down[li])
    yab = _sc_combine(ys, dest, 2 * t_all, COMBINE_CHUNK).reshape(2, t_all, d // 2)
    y_p, y_s = _final(x1, yab, route, final_norm_g.reshape(1, d), n * l, ns)

    hf = hfin.reshape(n, N_LANE_TILES, 2, 8, SSM_STATE)
    re_p = hf[:, :, 0].reshape(1, n, N_SSM_GROUPS, SSM_STATE)
    im_p = hf[:, :, 1].reshape(1, n, N_SSM_GROUPS, SSM_STATE)
    re_s = hr_s.reshape(1, ns, N_SSM_GROUPS, SSM_STATE)
    im_s = hi_s.reshape(1, ns, N_SSM_GROUPS, SSM_STATE)
    return (y_p.reshape(n, l, d), y_s.reshape(ns, 1, d), re_p, im_p, re_s, im_s,
            vrow.reshape(1, ns, 1, D_GMLP))
```

```python
import math

import jax
import jax.numpy as jnp
from jax import lax
from jax.experimental import pallas as pl
from jax.experimental.pallas import tpu as pltpu
from jax.experimental.pallas import tpu_sc as plsc

D_MODEL = 1024
D_SSM = 512
D_GMLP = 512
SSM_GROUP = 16
N_SSM_GROUPS = 32
SSM_STATE = 64
CHUNK = 128
N_GMLP_HEADS = 4
GMLP_HEAD = 128
N_EXPERT_GROUPS = 4
EXPERTS_PER_GROUP = 8
N_EXPERTS = 32
D_EXPERT = 512
D_IN = 2048
EPS = 1e-6

LANES = 128
SUBLANES = 8
N_LANE_TILES = D_SSM // LANES
STATE_COLS = N_SSM_GROUPS * SSM_STATE
TILE_STATE = STATE_COLS // N_LANE_TILES
VMEM_LIMIT = 56 * 1024 * 1024

SC_CORES = 2
SC_SUBCORES = 16
SC_LANES = 16
SC_WORKERS = SC_CORES * SC_SUBCORES

FRONT_TL = 512
SSM_LC = 256
SSM_BLK = 4
COEF_LB_RE, COEF_LB_IM, COEF_LBLK_RE, COEF_LBLK_IM = 0, 1, 2, 3
TOK_TM = 512
FINAL_TM = 1024
EXP_UNIT = 128
EXP_CLASSES = 4
W_SPLIT = 4
DISPATCH_CHUNK = 80
COMBINE_CHUNK = 24

R_E1, R_E2, R_W1, R_W2, R_RANK1, R_RANK2, R_CODE1, R_CODE2 = 0, 1, 2, 3, 4, 5, 6, 7
CODE_BITS = 16
CODE_SHIFT = float(1 << CODE_BITS)

_INV_SQRT2 = 1.0 / math.sqrt(2.0)
_BF16 = jnp.bfloat16
_F32 = jnp.float32
_U32 = jnp.uint32


def _gelu(x):
    return 0.5 * x * (1.0 + lax.erf(x * _INV_SQRT2))


def _rms(x, g):
    return x * lax.rsqrt(jnp.mean(x * x, axis=-1, keepdims=True) + EPS) * g


def _dot(a, b):
    return jnp.dot(a, b, preferred_element_type=_F32)


def _dot_f32(a, b):
    return jnp.dot(a, b, preferred_element_type=_F32, precision=lax.Precision.HIGHEST)


def _pack_bf16_pair(x):
    w = x.shape[1] // 2
    hi = lax.bitcast_convert_type(x[:, :w].astype(_BF16).astype(_F32), _U32)
    lo = lax.bitcast_convert_type(x[:, w:].astype(_BF16).astype(_F32), _U32)
    return hi | (lo >> 16)


def _unpack_bf16_pair(p):
    hi = lax.bitcast_convert_type(p & jnp.uint32(0xFFFF0000), _F32)
    lo = lax.bitcast_convert_type(p << 16, _F32)
    return jnp.concatenate([hi, lo], axis=-1)


def _head_norm_gelu(vb, gn):
    v = _gelu(vb)
    parts = []
    for h in range(N_GMLP_HEADS):
        vh = v[:, h * GMLP_HEAD:(h + 1) * GMLP_HEAD]
        parts.append(vh * lax.rsqrt(jnp.mean(vh * vh, axis=-1, keepdims=True) + EPS))
    return jnp.concatenate(parts, axis=-1) * gn


def _front_prompt_kernel(x_ref, g1_ref, win_ref, gn_ref, ws_ref, bs_ref, gog_ref,
                         xa_ref, sg_ref, mixb_ref, win_bf):
    @pl.when((pl.program_id(0) == 0) & (pl.program_id(1) == 0))
    def _():
        win_bf[...] = win_ref[...].astype(_BF16)

    x = x_ref[0]
    hn = _rms(x, g1_ref[...]).astype(_BF16)
    z = _dot(hn, win_bf[...])
    xa_ref[0] = z[:, :D_SSM]
    sg_ref[0] = jax.nn.sigmoid(z[:, D_SSM:2 * D_SSM])
    ub = _gelu(z[:, 2 * D_SSM:2 * D_SSM + D_GMLP])
    vbn = _head_norm_gelu(z[:, 2 * D_SSM + D_GMLP:], gn_ref[...]).astype(_BF16)
    tl = x.shape[0]
    rows = []
    for c in range(tl // CHUNK):
        heads = []
        for h in range(N_GMLP_HEADS):
            vh = vbn[c * CHUNK:(c + 1) * CHUNK, h * GMLP_HEAD:(h + 1) * GMLP_HEAD]
            heads.append(_dot(ws_ref[h], vh) + bs_ref[:, h:h + 1])
        rows.append(jnp.concatenate(heads, axis=-1))
    s = jnp.concatenate(rows, axis=0)
    mixb_ref[0] = _rms(ub * s, gog_ref[...]).astype(_BF16)


def _front_prompt(x, g1, win, gn, ws_tril_bf, bs_t, gog):
    n, l, d = x.shape
    tl = FRONT_TL
    grid = (n, l // tl)
    const = lambda *shape: pl.BlockSpec(shape, lambda b, i: (0,) * len(shape))
    seq = lambda w: pl.BlockSpec((1, tl, w), lambda b, i: (b, i, 0))
    return pl.pallas_call(
        _front_prompt_kernel,
        grid=grid,
        in_specs=[seq(d), const(1, d), const(d, D_IN), const(1, D_GMLP),
                  const(N_GMLP_HEADS, CHUNK, CHUNK), const(CHUNK, N_GMLP_HEADS), const(1, D_GMLP)],
        out_specs=[seq(D_SSM), seq(D_SSM), seq(D_GMLP)],
        out_shape=[jax.ShapeDtypeStruct((n, l, D_SSM), _F32),
                   jax.ShapeDtypeStruct((n, l, D_SSM), _F32),
                   jax.ShapeDtypeStruct((n, l, D_GMLP), _BF16)],
        scratch_shapes=[pltpu.VMEM((d, D_IN), _BF16)],
        compiler_params=pltpu.CompilerParams(
            dimension_semantics=("arbitrary", "arbitrary"), vmem_limit_bytes=VMEM_LIMIT),
        name="front_prompt",
    )(x, g1, win, gn, ws_tril_bf, bs_t, gog)


def _ssm_prompt_kernel(xa_ref, sg_ref, v_ref, r_ref, coef_ref, dsk_ref, gos_ref,
                       mixa_ref, hfin_ref, s_ref, st_ref):
    lc = xa_ref.shape[1]
    nblk = lc // SSM_BLK
    rows = nblk * SUBLANES

    @pl.when(pl.program_id(0) == 0)
    def _():
        st_ref[...] = jnp.zeros_like(st_ref)

    def by_position(ref):
        t = pltpu.einshape("btc->tbc", ref[...]).reshape(nblk, SSM_BLK, SUBLANES, D_SSM)
        return [t[:, i].reshape(rows, D_SSM) for i in range(SSM_BLK)]

    xs = by_position(xa_ref)
    xs_bf = [x.astype(_BF16) for x in xs]
    xk = [jnp.concatenate([x[:, k * LANES:(k + 1) * LANES] for x in xs_bf], axis=-1)
          for k in range(N_LANE_TILES)]
    for k in range(N_LANE_TILES):
        s_ref[:, 2 * TILE_STATE * k:2 * TILE_STATE * (k + 1)] = _dot(xk[k], v_ref[k])

    for kk in range(0, N_LANE_TILES, 2):
        tiles = (kk, kk + 1)
        cols = [(2 * TILE_STATE * k, 2 * TILE_STATE * k + TILE_STATE) for k in tiles]
        lbs = [tuple(jnp.broadcast_to(coef_ref[row:row + 1, k * TILE_STATE:(k + 1) * TILE_STATE],
                                      (SUBLANES, TILE_STATE)) for row in (COEF_LBLK_RE, COEF_LBLK_IM))
               for k in tiles]

        def body(j, carry, cols=cols, lbs=lbs):
            r0 = pl.multiple_of(j * SUBLANES, SUBLANES)
            out = []
            for q, ((c_re, c_im), (lr, li)) in enumerate(zip(cols, lbs)):
                hr, hi = carry[2 * q], carry[2 * q + 1]
                sr = s_ref[pl.ds(r0, SUBLANES), c_re:c_re + TILE_STATE]
                si = s_ref[pl.ds(r0, SUBLANES), c_im:c_im + TILE_STATE]
                s_ref[pl.ds(r0, SUBLANES), c_re:c_re + TILE_STATE] = hr
                s_ref[pl.ds(r0, SUBLANES), c_im:c_im + TILE_STATE] = hi
                out += [lr * hr - li * hi + sr, lr * hi + li * hr + si]
            return tuple(out)

        init = tuple(st_ref[:, c:c + TILE_STATE] for c_pair in cols for c in c_pair)
        fin = lax.fori_loop(0, nblk, body, init, unroll=2)
        for q, (c_re, c_im) in enumerate(cols):
            st_ref[:, c_re:c_re + TILE_STATE] = fin[2 * q]
            st_ref[:, c_im:c_im + TILE_STATE] = fin[2 * q + 1]

    yk = []
    for k in range(N_LANE_TILES):
        h_in = s_ref[:, 2 * TILE_STATE * k:2 * TILE_STATE * (k + 1)].astype(_BF16)
        yk.append(_dot(jnp.concatenate([h_in, xk[k]], axis=-1), r_ref[k]))
    sgs = by_position(sg_ref)
    outs = []
    for i in range(SSM_BLK):
        y = jnp.concatenate([y_k[:, i * LANES:(i + 1) * LANES] for y_k in yk], axis=-1) + dsk_ref[...] * xs[i]
        outs.append(_rms(_gelu(y) * sgs[i], gos_ref[...]).reshape(nblk, SUBLANES, D_SSM))
    mixa = jnp.stack(outs, axis=1).reshape(lc, SUBLANES, D_SSM)
    mixa_ref[...] = pltpu.einshape("tbc->btc", mixa).astype(_BF16)
    hfin_ref[...] = st_ref[...]


def _ssm_prompt(xa, sg, v, r, coef, dsk, gos):
    n, l, _ = xa.shape
    lc = SSM_LC
    const = lambda *shape: pl.BlockSpec(shape, lambda i: (0,) * len(shape))
    seq_spec = pl.BlockSpec((n, lc, D_SSM), lambda i: (0, i, 0))
    return pl.pallas_call(
        _ssm_prompt_kernel,
        grid=(l // lc,),
        in_specs=[seq_spec, seq_spec, const(*v.shape), const(*r.shape),
                  const(*coef.shape), const(1, D_SSM), const(1, D_SSM)],
        out_specs=[seq_spec, const(n, 2 * STATE_COLS)],
        out_shape=[jax.ShapeDtypeStruct((n, l, D_SSM), _BF16),
                   jax.ShapeDtypeStruct((n, 2 * STATE_COLS), _F32)],
        scratch_shapes=[pltpu.VMEM((lc // SSM_BLK * n, 2 * STATE_COLS), _F32),
                        pltpu.VMEM((n, 2 * STATE_COLS), _F32)],
        compiler_params=pltpu.CompilerParams(
            dimension_semantics=("arbitrary",), vmem_limit_bytes=VMEM_LIMIT),
        name="ssm_prompt",
    )(xa, sg, v, r, coef, dsk, gos)


def _front_sample_kernel(x_ref, g1_ref, win_ref, gn_ref, w00_ref, b0_ref, gog_ref,
                         wb_ref, wc_ref, coef_ref, dsk_ref, gos_ref, h0r_ref, h0i_ref,
                         mix_ref, hr_ref, hi_ref, vrow_ref):
    x = x_ref[...]
    hn = _rms(x, g1_ref[...])
    z = _dot_f32(hn, win_ref[...])
    xa = z[:, :D_SSM]
    ys = []
    for k in range(N_LANE_TILES):
        bu = _dot_f32(xa[:, k * LANES:(k + 1) * LANES], wb_ref[k])
        sl = slice(k * TILE_STATE, (k + 1) * TILE_STATE)
        lr, li = coef_ref[COEF_LB_RE:COEF_LB_RE + 1, sl], coef_ref[COEF_LB_IM:COEF_LB_IM + 1, sl]
        h0r, h0i = h0r_ref[:, sl], h0i_ref[:, sl]
        nr = lr * h0r - li * h0i + bu[:, :TILE_STATE]
        ni = lr * h0i + li * h0r + bu[:, TILE_STATE:]
        hr_ref[:, sl] = nr
        hi_ref[:, sl] = ni
        ys.append(_dot_f32(jnp.concatenate([nr, ni], axis=-1), wc_ref[k]))
    y = jnp.concatenate(ys, axis=-1) + dsk_ref[...] * xa
    ya = _gelu(y) * jax.nn.sigmoid(z[:, D_SSM:2 * D_SSM])
    mix_ref[:, :D_SSM] = _rms(ya, gos_ref[...])
    ub = _gelu(z[:, 2 * D_SSM:2 * D_SSM + D_GMLP])
    vbn = _head_norm_gelu(z[:, 2 * D_SSM + D_GMLP:], gn_ref[...])
    vrow_ref[...] = vbn
    s = w00_ref[...] * vbn + b0_ref[...]
    mix_ref[:, D_SSM:] = _rms(ub * s, gog_ref[...])


def _front_sample(x, g1, win, gn, w00, b0, gog, wb, wc, coef, dsk, gos, h0r, h0i):
    n = x.shape[0]
    vmem = pl.BlockSpec(memory_space=pltpu.VMEM)
    return pl.pallas_call(
        _front_sample_kernel,
        in_specs=[vmem] * 14,
        out_specs=[vmem] * 4,
        out_shape=[jax.ShapeDtypeStruct((n, D_MODEL), _F32),
                   jax.ShapeDtypeStruct((n, STATE_COLS), _F32),
                   jax.ShapeDtypeStruct((n, STATE_COLS), _F32),
                   jax.ShapeDtypeStruct((n, D_GMLP), _F32)],
        compiler_params=pltpu.CompilerParams(vmem_limit_bytes=VMEM_LIMIT),
        name="front_sample",
    )(x, g1, win, gn, w00, b0, gog, wb, wc, coef, dsk, gos, h0r, h0i)


def _route(logits, base):
    tm = logits.shape[0]
    lt = logits.T
    ex = lt[:N_EXPERTS, :]
    gr = lt[N_EXPERTS:N_EXPERTS + SUBLANES, :]
    row_e = lax.broadcasted_iota(jnp.int32, ex.shape, 0).astype(_F32)
    row_g = lax.broadcasted_iota(jnp.int32, gr.shape, 0).astype(_F32)
    neg = jnp.float32(-jnp.inf)
    big = jnp.float32(LANES)
    is_g = row_g < N_EXPERT_GROUPS
    gl = jnp.where(is_g, gr, neg)
    gmax = jnp.max(gl, axis=0, keepdims=True)
    gi = jnp.min(jnp.where(gl == gmax, row_g, big), axis=0, keepdims=True)
    p_top = 1.0 / jnp.sum(jnp.where(is_g, jnp.exp(gl - gmax), 0.0), axis=0, keepdims=True)
    lo = gi * EXPERTS_PER_GROUP
    in_grp = (row_e >= lo) & (row_e < lo + EXPERTS_PER_GROUP)
    m1 = jnp.max(jnp.where(in_grp, ex, neg), axis=0, keepdims=True)
    i1 = jnp.min(jnp.where(in_grp & (ex == m1), row_e, big), axis=0, keepdims=True)
    rest = in_grp & (row_e != i1)
    m2 = jnp.max(jnp.where(rest, ex, neg), axis=0, keepdims=True)
    i2 = jnp.min(jnp.where(rest & (ex == m2), row_e, big), axis=0, keepdims=True)
    e2 = jnp.exp(m2 - m1)
    w1 = p_top / (1.0 + e2)
    w2 = p_top * e2 / (1.0 + e2)
    sel1 = row_e == i1
    sel2 = row_e == i2
    hits = jnp.where(sel1 | sel2, 1.0, 0.0)
    src = lax.broadcasted_iota(jnp.int32, (tm, tm), 0)
    dst = lax.broadcasted_iota(jnp.int32, (tm, tm), 1)
    before = _dot(hits.astype(_BF16), jnp.where(src < dst, 1.0, 0.0).astype(_BF16)) + base
    rank1 = jnp.sum(jnp.where(sel1, before, 0.0), axis=0, keepdims=True)
    rank2 = jnp.sum(jnp.where(sel2, before, 0.0), axis=0, keepdims=True)
    fields = {R_E1: i1, R_E2: i2, R_W1: w1, R_W2: w2, R_RANK1: rank1, R_RANK2: rank2,
              R_CODE1: i1 * CODE_SHIFT + rank1, R_CODE2: i2 * CODE_SHIFT + rank2}
    row8 = lax.broadcasted_iota(jnp.int32, (SUBLANES, tm), 0)
    route_t = jnp.zeros((SUBLANES, tm), _F32)
    for r, val in fields.items():
        route_t = jnp.where(row8 == r, val, route_t)
    route = jnp.concatenate([route_t, jnp.zeros((LANES - SUBLANES, tm), _F32)], axis=0).T
    return route_t, route, base + jnp.sum(hits, axis=1, keepdims=True)


def _mixer_out_prompt_kernel(x_ref, mixa_ref, mixb_ref, wo_ref, g2_ref, wr_ref, br_ref,
                             x1_ref, xn_ref, route_ref, route_t_ref, cnt_ref, base_ref, logits_ref, wo_bf):
    i = pl.program_id(0)

    @pl.when(i == 0)
    def _():
        base_ref[...] = jnp.zeros_like(base_ref)
        logits_ref[...] = jnp.zeros_like(logits_ref)
        wo_bf[...] = wo_ref[...].astype(_BF16)

    prev_logits = logits_ref[...]
    x1 = x_ref[0] + _dot(mixa_ref[0], wo_bf[:D_SSM, :]) + _dot(mixb_ref[0], wo_bf[D_SSM:, :])
    xn = _rms(x1, g2_ref[...])
    x1_ref[...] = _pack_bf16_pair(x1)
    xn_ref[...] = _pack_bf16_pair(xn)
    logits_ref[...] = _dot(xn.astype(_BF16), wr_ref[...]) + br_ref[...]
    route_t, route, base = _route(prev_logits, base_ref[...])
    route_ref[...] = route
    route_t_ref[...] = route_t
    base = jnp.where(i >= 1, base, base_ref[...])
    base_ref[...] = base
    cnt_ref[...] = base


def _mixer_out_sample_kernel(x_ref, mix_ref, wo_ref, g2_ref, wr_ref, br_ref, cnt_in_ref,
                             x1_in, xn_in, route_in, route_t_in,
                             x1_ref, xn_ref, route_ref, route_t_ref, cnt_ref):
    del x1_in, xn_in, route_in, route_t_in
    x1 = (x_ref[...] + _dot_f32(mix_ref[:, :D_SSM], wo_ref[:D_SSM, :])
          + _dot_f32(mix_ref[:, D_SSM:], wo_ref[D_SSM:, :]))
    xn = _rms(x1, g2_ref[...])
    logits = _dot_f32(xn, wr_ref[...]) + br_ref[...]
    route_t, route, base = _route(logits, cnt_in_ref[...])
    x1_ref[...] = _pack_bf16_pair(x1)
    xn_ref[...] = _pack_bf16_pair(xn)
    route_ref[...] = route
    route_t_ref[...] = route_t
    cnt_ref[...] = base


def _mixer_out(x_p, mixa, mixb, x_s, mix_s, wo, g2, wr, br):
    n, l, d = x_p.shape
    ns = x_s.shape[0]
    t_all = n * l + ns
    tm = TOK_TM
    per_seq = l // tm
    n_tiles = n * per_seq
    cur = lambda i: jnp.minimum(i, n_tiles - 1)
    prev = lambda i: jnp.maximum(i - 1, 0)
    const = lambda *shape: pl.BlockSpec(shape, lambda i: (0,) * len(shape))
    seq = lambda w: pl.BlockSpec((1, tm, w), lambda i: (cur(i) // per_seq, cur(i) % per_seq, 0))
    tok = lambda w, which: pl.BlockSpec((tm, w), lambda i: (which(i), 0))
    tok_shapes = [jax.ShapeDtypeStruct((t_all, d // 2), _U32),
                  jax.ShapeDtypeStruct((t_all, d // 2), _U32),
                  jax.ShapeDtypeStruct((t_all, LANES), _F32),
                  jax.ShapeDtypeStruct((SUBLANES, t_all), _F32)]
    cnt_shape = jax.ShapeDtypeStruct((N_EXPERTS, 1), _F32)
    x1, xn, route, route_t, cnt = pl.pallas_call(
        _mixer_out_prompt_kernel,
        grid=(n_tiles + 1,),
        in_specs=[seq(d), seq(D_SSM), seq(D_GMLP),
                  const(d, d), const(1, d), const(d, LANES), const(1, LANES)],
        out_specs=[tok(d // 2, cur), tok(d // 2, cur), tok(LANES, prev),
                   pl.BlockSpec((SUBLANES, tm), lambda i: (0, prev(i))), const(N_EXPERTS, 1)],
        out_shape=tok_shapes + [cnt_shape],
        scratch_shapes=[pltpu.VMEM((N_EXPERTS, 1), _F32), pltpu.VMEM((tm, LANES), _F32),
                        pltpu.VMEM((d, d), _BF16)],
        compiler_params=pltpu.CompilerParams(
            dimension_semantics=("arbitrary",), vmem_limit_bytes=VMEM_LIMIT),
        name="mixer_out_prompt",
    )(x_p, mixa, mixb, wo, g2, wr.astype(_BF16), br)
    tail = (n * l) // ns
    c1 = lambda *shape: pl.BlockSpec(shape, lambda i: (0,) * len(shape))
    anyspec = pl.BlockSpec(memory_space=pl.ANY)
    tail_spec = lambda w: pl.BlockSpec((ns, w), lambda i: (tail, 0))
    return pl.pallas_call(
        _mixer_out_sample_kernel,
        grid=(1,),
        in_specs=[c1(ns, d), c1(ns, d), c1(d, d), c1(1, d), c1(d, LANES), c1(1, LANES), c1(N_EXPERTS, 1),
                  anyspec, anyspec, anyspec, anyspec],
        out_specs=[tail_spec(d // 2), tail_spec(d // 2), tail_spec(LANES),
                   pl.BlockSpec((SUBLANES, ns), lambda i: (0, tail)), c1(N_EXPERTS, 1)],
        out_shape=tok_shapes + [cnt_shape],
        input_output_aliases={7: 0, 8: 1, 9: 2, 10: 3},
        compiler_params=pltpu.CompilerParams(
            dimension_semantics=("arbitrary",), vmem_limit_bytes=VMEM_LIMIT),
        name="mixer_out_sample",
    )(x_s, mix_s, wo, g2, wr, br, cnt, x1, xn, route, route_t)


def _sc_stream(n_chunks, gather, write):
    gather(0).start()
    for j in range(n_chunks):
        if j + 1 < n_chunks:
            if j >= 1:
                write(j - 1).wait()
            gather(j + 1).start()
        gather(j).wait()
        write(j).start()
    if n_chunks >= 2:
        write(n_chunks - 2).wait()
    write(n_chunks - 1).wait()


def _sc_mesh():
    return plsc.VectorSubcoreMesh(core_axis_name="c", subcore_axis_name="s",
                                  num_cores=SC_CORES, num_subcores=SC_SUBCORES)


def _sc_buffers(chunk, w, dtype):
    return [pltpu.VMEM((chunk, w), dtype), pltpu.VMEM((chunk, w), dtype)] + [pltpu.SemaphoreType.DMA] * 4


def _sc_combine(table, idx, n_out, chunk):
    w = table.shape[1]
    rows_w = n_out // SC_WORKERS
    n_chunks = rows_w // chunk
    assert rows_w * SC_WORKERS == n_out and n_chunks * chunk == rows_w and rows_w % SUBLANES == 0

    def body(table_hbm, idx_hbm, out_hbm, idx_v, buf0, buf1, g0, g1, w0, w1):
        wid = lax.axis_index("s") * SC_CORES + lax.axis_index("c")
        base = pl.multiple_of(wid * rows_w, SUBLANES)
        pltpu.sync_copy(idx_hbm.at[pl.ds(base, rows_w)], idx_v)
        bufs, gsems, wsems = (buf0, buf1), (g0, g1), (w0, w1)

        def gather(j):
            return pltpu.make_async_copy(table_hbm.at[idx_v.at[pl.ds(j * chunk, chunk)]], bufs[j % 2], gsems[j % 2])

        def write(j):
            return pltpu.make_async_copy(bufs[j % 2], out_hbm.at[pl.ds(base + j * chunk, chunk)], wsems[j % 2])

        _sc_stream(n_chunks, gather, write)

    return pl.kernel(
        body,
        out_type=jax.ShapeDtypeStruct((n_out, w), table.dtype),
        mesh=_sc_mesh(),
        scratch_types=[pltpu.VMEM((rows_w,), jnp.int32)] + _sc_buffers(chunk, w, table.dtype),
        compiler_params=pltpu.CompilerParams(use_tc_tiling_on_sc=True),
        name="sc_combine",
    )(table, idx)


def _sc_dispatch(table, codes, start_row, n_out, chunk):
    t_all, w = table.shape
    n_pad = codes.shape[0]
    n_ent = 2 * t_all
    ent_w = n_pad // SC_WORKERS
    n_chunks = ent_w // chunk
    per_chunk = chunk // SC_LANES
    trash = n_out - (n_pad - n_ent)
    assert ent_w * SC_WORKERS == n_pad and n_chunks * chunk == ent_w
    assert per_chunk * SC_LANES == chunk and chunk <= LANES and n_pad - n_ent <= t_all

    def body(table_hbm, code_hbm, start_hbm, out_hbm, dest_hbm,
             code_v, dest_v, tok_v, dst_v, start_v, buf0, buf1, g0, g1, w0, w1):
        wid = lax.axis_index("s") * SC_CORES + lax.axis_index("c")
        ebase = pl.multiple_of(wid * ent_w, SUBLANES)
        pltpu.sync_copy(code_hbm.at[pl.ds(ebase, ent_w)], code_v)
        pltpu.sync_copy(start_hbm, start_v)
        lane = lax.iota(jnp.int32, SC_LANES)
        for j in range(n_chunks):
            for c in range(per_chunk):
                off = j * chunk + c * SC_LANES
                ent = ebase + off + lane
                code = code_v[pl.ds(off, SC_LANES)]
                d = plsc.load_gather(start_v, [code >> CODE_BITS]) + (code & ((1 << CODE_BITS) - 1))
                d = jnp.where(ent >= n_ent, trash + (ent - n_ent), d)
                tok = jnp.where(ent >= t_all, ent - t_all, ent)
                tok = jnp.where(tok >= t_all, tok - t_all, tok)
                dest_v[pl.ds(off, SC_LANES)] = d
                dst_v[j, pl.ds(c * SC_LANES, SC_LANES)] = d
                tok_v[j, pl.ds(c * SC_LANES, SC_LANES)] = tok
        pltpu.sync_copy(dest_v, dest_hbm.at[pl.ds(ebase, ent_w)])
        bufs, gsems, wsems = (buf0, buf1), (g0, g1), (w0, w1)

        def gather(j):
            return pltpu.make_async_copy(table_hbm.at[tok_v.at[j]], bufs[j % 2], gsems[j % 2])

        def scatter(j):
            return pltpu.make_async_copy(bufs[j % 2], out_hbm.at[dst_v.at[j]], wsems[j % 2])

        _sc_stream(n_chunks, gather, scatter)

    return pl.kernel(
        body,
        out_type=(jax.ShapeDtypeStruct((n_out, w), table.dtype), jax.ShapeDtypeStruct((n_pad,), jnp.int32)),
        mesh=_sc_mesh(),
        scratch_types=([pltpu.VMEM((ent_w,), jnp.int32), pltpu.VMEM((ent_w,), jnp.int32),
                        pltpu.VMEM((n_chunks, chunk), jnp.int32), pltpu.VMEM((n_chunks, chunk), jnp.int32),
                        pltpu.VMEM((LANES,), jnp.int32)] + _sc_buffers(chunk, w, table.dtype)),
        compiler_params=pltpu.CompilerParams(use_tc_tiling_on_sc=True, needs_layout_passes=False),
        name="sc_dispatch",
    )(table, codes, start_row)


def _experts_kernel(piece_start_ref, piece_row_ref, piece_cls_ref, *refs):
    n_w = 3 * W_SPLIT
    wg_refs, wu_refs, wd_refs = refs[:W_SPLIT], refs[W_SPLIT:2 * W_SPLIT], refs[2 * W_SPLIT:n_w]
    xs_hbm, ys_hbm, wg_bf, wu_bf, wd_bf, xbuf, ybuf, xsem, ysem = refs[n_w:]
    e = pl.program_id(0)
    g0 = piece_start_ref[e]
    n_here = piece_start_ref[e + 1] - g0
    n_total = piece_start_ref[N_EXPERTS]

    def per_class(g, fn):
        cls = piece_cls_ref[g]
        row = pl.multiple_of(piece_row_ref[g], EXP_UNIT)
        for c in range(1, EXP_CLASSES + 1):
            pl.when(cls == c)(lambda c=c: fn(c * EXP_UNIT, row))

    def x_copy(slot, rows, row):
        return pltpu.make_async_copy(xs_hbm.at[pl.ds(row, rows)], xbuf.at[slot, pl.ds(0, rows)], xsem.at[slot])

    def y_copy(slot, rows, row):
        return pltpu.make_async_copy(ybuf.at[slot, pl.ds(0, rows)], ys_hbm.at[pl.ds(row, rows)], ysem.at[slot])

    @pl.when((e == 0) & (n_total > 0))
    def _():
        per_class(0, lambda rows, row: x_copy(0, rows, row).start())

    for dst, chunks in ((wg_bf, wg_refs), (wu_bf, wu_refs), (wd_bf, wd_refs)):
        rows = dst.shape[0] // W_SPLIT
        for q, src in enumerate(chunks):
            dst[q * rows:(q + 1) * rows, :] = src[0, 0].astype(_BF16)

    def piece(j, carry):
        g = g0 + j
        slot = lax.rem(g, 2)
        per_class(g, lambda rows, row: x_copy(slot, rows, row).wait())

        @pl.when(g + 1 < n_total)
        def _():
            per_class(g + 1, lambda rows, row: x_copy(1 - slot, rows, row).start())

        @pl.when(g >= 2)
        def _():
            per_class(g - 2, lambda rows, row: y_copy(slot, rows, row).wait())

        def compute(rows, row):
            x = _unpack_bf16_pair(xbuf[slot, pl.ds(0, rows)]).astype(_BF16)
            a = _dot(x, wg_bf[...])
            u = _dot(x, wu_bf[...])
            h = (a * jax.nn.sigmoid(a) * u).astype(_BF16)
            ybuf[slot, pl.ds(0, rows)] = _pack_bf16_pair(_dot(h, wd_bf[...]))
            y_copy(slot, rows, row).start()

        per_class(g, compute)
        return carry

    lax.fori_loop(0, n_here, piece, 0)

    @pl.when(e == N_EXPERTS - 1)
    def _():
        @pl.when(n_total >= 2)
        def _():
            per_class(n_total - 2, lambda rows, row: y_copy(lax.rem(n_total, 2), rows, row).wait())

        @pl.when(n_total >= 1)
        def _():
            per_class(n_total - 1, lambda rows, row: y_copy(lax.rem(n_total - 1, 2), rows, row).wait())


def _experts(piece_start, piece_row, piece_cls, n_rows, xs, w_gate, w_up, w_down):
    dh = xs.shape[1]
    d = 2 * dh
    tm = EXP_UNIT * EXP_CLASSES
    anyspec = pl.BlockSpec(memory_space=pl.ANY)

    def chunk_specs(rows, cols):
        return [pl.BlockSpec((1, 1, rows // W_SPLIT, cols), lambda e, ps, pr, pc, q=q: (e, q, 0, 0))
                for q in range(W_SPLIT)]

    split = lambda w: w.reshape(w.shape[0], W_SPLIT, w.shape[1] // W_SPLIT, w.shape[2])
    grid_spec = pltpu.PrefetchScalarGridSpec(
        num_scalar_prefetch=3,
        grid=(N_EXPERTS,),
        in_specs=(chunk_specs(d, D_EXPERT) + chunk_specs(d, D_EXPERT) + chunk_specs(D_EXPERT, d) + [anyspec]),
        out_specs=anyspec,
        scratch_shapes=[pltpu.VMEM((d, D_EXPERT), _BF16), pltpu.VMEM((d, D_EXPERT), _BF16),
                        pltpu.VMEM((D_EXPERT, d), _BF16),
                        pltpu.VMEM((2, tm, dh), _U32), pltpu.VMEM((2, tm, dh), _U32),
                        pltpu.SemaphoreType.DMA((2,)), pltpu.SemaphoreType.DMA((2,))],
    )
    return pl.pallas_call(
        _experts_kernel,
        grid_spec=grid_spec,
        out_shape=jax.ShapeDtypeStruct((n_rows, dh), _U32),
        compiler_params=pltpu.CompilerParams(
            dimension_semantics=("arbitrary",), vmem_limit_bytes=VMEM_LIMIT),
        name="experts",
    )(piece_start, piece_row, piece_cls, *([split(w_gate)] * W_SPLIT), *([split(w_up)] * W_SPLIT),
      *([split(w_down)] * W_SPLIT), xs)


def _final_kernel(x1_ref, ya_ref, yb_ref, route_ref, gf_ref, y_ref):
    route = route_ref[...]
    x2 = (_unpack_bf16_pair(x1_ref[...]) + route[:, R_W1:R_W1 + 1] * _unpack_bf16_pair(ya_ref[...])
          + route[:, R_W2:R_W2 + 1] * _unpack_bf16_pair(yb_ref[...]))
    y_ref[...] = _rms(x2, gf_ref[...])


def _final(x1, yab, route, gf, n_prompt, n_sample):
    d = 2 * x1.shape[1]

    def call(tm, first_block, n_rows, name):
        tok = lambda w: pl.BlockSpec((tm, w), lambda i: (first_block + i, 0))
        sel = lambda k: pl.BlockSpec((None, tm, d // 2), lambda i: (k, first_block + i, 0))
        return pl.pallas_call(
            _final_kernel,
            grid=(n_rows // tm,),
            in_specs=[tok(d // 2), sel(0), sel(1), tok(LANES), pl.BlockSpec((1, d), lambda i: (0, 0))],
            out_specs=pl.BlockSpec((tm, d), lambda i: (i, 0)),
            out_shape=jax.ShapeDtypeStruct((n_rows, d), _F32),
            compiler_params=pltpu.CompilerParams(
                dimension_semantics=("arbitrary",), vmem_limit_bytes=VMEM_LIMIT),
            name=name,
        )(x1, yab, yab, route, gf)

    return (call(FINAL_TM, 0, n_prompt, "final_prompt"),
            call(n_sample, n_prompt // n_sample, n_sample, "final_sample"))


def _powers(lam_re, lam_im, dt):
    out = []
    for m in range(SSM_BLK + 1):
        mag = jnp.exp(m * lam_re * dt)
        ang = m * lam_im * dt
        out.append((mag * jnp.cos(ang), mag * jnp.sin(ang)))
    return out


def _spread(x, copies):
    w = x.shape[1]
    src = lax.broadcasted_iota(jnp.int32, (w, w * copies), 0)
    dst = lax.broadcasted_iota(jnp.int32, (w, w * copies), 1)
    return _dot_f32(x, jnp.where(dst % w == src, 1.0, 0.0))


def _ssm_prep_kernel(lam_ref, b_re, b_im, c_re, c_im, v_ref, r_ref, wb_ref, wc_ref, coef_ref):
    n_p, n_h = SSM_STATE, SSM_GROUP
    lr, li, dt = lam_ref[0:1, :], lam_ref[1:2, :], lam_ref[2:3, :]
    pw = _powers(lr, li, dt)
    den = lr * lr + li * li
    nr, ni = pw[1][0] - 1.0, pw[1][1]
    k_re = (nr * lr + ni * li) / den
    k_im = (ni * lr - nr * li) / den
    coef_ref[...] = jnp.concatenate(
        [pw[1][0], pw[1][1], pw[SSM_BLK][0], pw[SSM_BLK][1], jnp.zeros((SUBLANES - 4, TILE_STATE), _F32)], axis=0)

    on_diag_b = (lax.broadcasted_iota(jnp.int32, (TILE_STATE, LANES), 0) // n_p
                 == lax.broadcasted_iota(jnp.int32, (TILE_STATE, LANES), 1) // n_h)
    rows_gp = lambda ref: ref[...].reshape(TILE_STATE, n_h)
    bt_re = jnp.where(on_diag_b, _spread(rows_gp(b_re), SUBLANES), 0.0).T
    bt_im = jnp.where(on_diag_b, _spread(rows_gp(b_im), SUBLANES), 0.0).T
    bb_re = k_re * bt_re - k_im * bt_im
    bb_im = k_re * bt_im + k_im * bt_re
    wb_ref[0] = jnp.concatenate([bb_re, bb_im], axis=1)
    v_rows = []
    for s in range(SSM_BLK):
        pr, pi = pw[SSM_BLK - 1 - s]
        v_rows.append(jnp.concatenate([pr * bb_re - pi * bb_im, pr * bb_im + pi * bb_re], axis=1))
    v_ref[0] = jnp.concatenate(v_rows, axis=0).astype(v_ref.dtype)

    on_diag_c = (lax.broadcasted_iota(jnp.int32, (LANES, TILE_STATE), 0) // n_h
                 == lax.broadcasted_iota(jnp.int32, (LANES, TILE_STATE), 1) // n_p)
    rows_gh = lambda ref: ref[...].reshape(LANES, n_p)
    ct_re = jnp.where(on_diag_c, _spread(rows_gh(c_re), SUBLANES), 0.0)
    ct_im = jnp.where(on_diag_c, _spread(rows_gh(c_im), SUBLANES), 0.0)
    cl = [(ct_re * pr - ct_im * pi, ct_re * pi + ct_im * pr) for pr, pi in pw]
    wc_ref[0] = jnp.concatenate([cl[0][0], -cl[0][1]], axis=1).T
    nt = lambda a, b: lax.dot_general(a, b, (((1,), (1,)), ((), ())), precision=lax.Precision.HIGHEST,
                                      preferred_element_type=_F32)
    direct = [nt(cl[m][0], bb_re) - nt(cl[m][1], bb_im) for m in range(SSM_BLK)]
    zero = jnp.zeros((LANES, LANES), _F32)
    rt = jnp.concatenate(
        [jnp.concatenate([cl[i + 1][0], -cl[i + 1][1]]
                         + [direct[i - s] if s <= i else zero for s in range(SSM_BLK)], axis=1)
         for i in range(SSM_BLK)], axis=0)
    r_ref[0] = rt.T.astype(r_ref.dtype)


def _ssm_params(lam_re, lam_im, log_dt, b_re, b_im, c_re, c_im, d_skip):
    n_g, n_p, n_h = N_SSM_GROUPS, SSM_STATE, SSM_GROUP
    dt = jnp.repeat(jnp.exp(log_dt), n_p)
    lam = jnp.zeros((SUBLANES, STATE_COLS), _F32).at[0].set(lam_re.reshape(-1)).at[1].set(
        lam_im.reshape(-1)).at[2].set(dt)
    groups = lambda r, c: pl.BlockSpec((SUBLANES, r, c), lambda k: (k, 0, 0))
    out3 = lambda rows, w: pl.BlockSpec((1, rows, w), lambda k: (k, 0, 0))
    cols = pl.BlockSpec((SUBLANES, TILE_STATE), lambda k: (0, k))
    k_blk = SSM_BLK * LANES
    v, r, wb, wc, coef = pl.pallas_call(
        _ssm_prep_kernel,
        grid=(N_LANE_TILES,),
        in_specs=[cols, groups(n_p, n_h), groups(n_p, n_h), groups(n_h, n_p), groups(n_h, n_p)],
        out_specs=[out3(k_blk, 2 * TILE_STATE), out3(2 * TILE_STATE + k_blk, k_blk),
                   out3(LANES, 2 * TILE_STATE), out3(2 * TILE_STATE, LANES), cols],
        out_shape=[jax.ShapeDtypeStruct((N_LANE_TILES, k_blk, 2 * TILE_STATE), _BF16),
                   jax.ShapeDtypeStruct((N_LANE_TILES, 2 * TILE_STATE + k_blk, k_blk), _BF16),
                   jax.ShapeDtypeStruct((N_LANE_TILES, LANES, 2 * TILE_STATE), _F32),
                   jax.ShapeDtypeStruct((N_LANE_TILES, 2 * TILE_STATE, LANES), _F32),
                   jax.ShapeDtypeStruct((SUBLANES, STATE_COLS), _F32)],
        compiler_params=pltpu.CompilerParams(
            dimension_semantics=("arbitrary",), vmem_limit_bytes=VMEM_LIMIT),
        name="ssm_prep",
    )(lam, b_re, b_im, c_re, c_im)
    return wb, wc, v, r, coef, d_skip.reshape(1, D_SSM)


def _dispatch_plan(route_t, cnt):
    t_all = route_t.shape[1]
    codes = route_t[R_CODE1:R_CODE2 + 1].astype(jnp.int32).reshape(-1)
    per_pass = SC_WORKERS * DISPATCH_CHUNK
    codes = jnp.pad(codes, (0, -(2 * t_all) % per_pass))
    counts = cnt[:, 0].astype(jnp.int32)
    zero = jnp.zeros((1,), jnp.int32)
    units = (counts + EXP_UNIT - 1) // EXP_UNIT
    unit_start = jnp.concatenate([zero, jnp.cumsum(units)])
    start_row = jnp.zeros((LANES,), jnp.int32).at[:N_EXPERTS].set(unit_start[:N_EXPERTS] * EXP_UNIT)
    pieces = (units + EXP_CLASSES - 1) // EXP_CLASSES
    piece_start = jnp.concatenate([zero, jnp.cumsum(pieces)])
    tm = EXP_UNIT * EXP_CLASSES
    max_units = (2 * t_all + N_EXPERTS * (EXP_UNIT - 1)) // EXP_UNIT
    max_pieces = (max_units + N_EXPERTS * (EXP_CLASSES - 1)) // EXP_CLASSES
    g = jnp.arange(max_pieces, dtype=jnp.int32)
    owner = ((g[:, None] >= piece_start[None, :-1]) & (g[:, None] < piece_start[None, 1:])).astype(jnp.int32)
    pick = lambda table: jnp.sum(owner * table[None, :], axis=1)
    first_unit = pick(unit_start[:-1]) + (g - pick(piece_start[:-1])) * EXP_CLASSES
    piece_row = first_unit * EXP_UNIT
    piece_cls = jnp.clip(pick(unit_start[1:]) - first_unit, 1, EXP_CLASSES)
    n_rows = (max_units * EXP_UNIT + tm - 1) // tm * tm + tm
    return codes, start_row, n_rows, piece_start, piece_row, piece_cls


def kernel(x_prompt, x_sample, state_ssm_re, state_ssm_im, norm1_g, w_in, lam_re, lam_im, log_dt, ssm_b_re, ssm_b_im, ssm_c_re, ssm_c_im, ssm_d, gmlp_norm_g, gmlp_w_s, gmlp_b_s, out_norm_ssm_g, out_norm_gmlp_g, w_out, norm2_g, w_router_group, b_router_group, w_router_expert, b_router_expert, w_gate, w_up, w_down, final_norm_g):
    n, l, d = x_prompt.shape
    ns = x_sample.shape[0]
    t_all = n * l + ns
    li = 0
    g1 = norm1_g[li].reshape(1, d)
    gn = gmlp_norm_g[li].reshape(1, D_GMLP)
    tril = jnp.tril(jnp.ones((CHUNK, CHUNK), dtype=bool))
    ws_tril = jnp.where(tril[None], gmlp_w_s[li], 0.0)
    bs = gmlp_b_s[li]
    gog = out_norm_gmlp_g[li].reshape(1, D_GMLP)
    gos = out_norm_ssm_g[li].reshape(1, D_SSM)
    wb, wc, v_blk, r_blk, coef, dsk = _ssm_params(
        lam_re[li], lam_im[li], log_dt[li], ssm_b_re[li], ssm_b_im[li], ssm_c_re[li], ssm_c_im[li], ssm_d[li])
    g2 = norm2_g[li].reshape(1, d)
    pad = LANES - N_EXPERTS - N_EXPERT_GROUPS
    wr = jnp.concatenate([w_router_expert[li], w_router_group[li], jnp.zeros((d, pad), _F32)], axis=1)
    br = jnp.concatenate([b_router_expert[li], b_router_group[li], jnp.zeros((pad,), _F32)]).reshape(1, LANES)

    xa, sg, mixb = _front_prompt(x_prompt, g1, w_in[li], gn, ws_tril.astype(_BF16), bs.T, gog)
    mixa, hfin = _ssm_prompt(xa, sg, v_blk, r_blk, coef, dsk, gos)
    w00 = jnp.repeat(ws_tril[:, 0, 0], GMLP_HEAD).reshape(1, D_GMLP)
    b0 = jnp.repeat(bs[:, 0], GMLP_HEAD).reshape(1, D_GMLP)
    mix_s, hr_s, hi_s, vrow = _front_sample(
        x_sample.reshape(ns, d), g1, w_in[li], gn, w00, b0, gog, wb, wc, coef, dsk, gos,
        state_ssm_re[li].reshape(ns, STATE_COLS), state_ssm_im[li].reshape(ns, STATE_COLS))

    x1, xn, route, route_t, cnt = _mixer_out(x_prompt, mixa, mixb, x_sample.reshape(ns, d), mix_s,
                                             w_out[li], g2, wr, br)
    codes, start_row, n_rows, piece_start, piece_row, piece_cls = _dispatch_plan(route_t, cnt)
    xs, dest = _sc_dispatch(xn, codes, start_row, n_rows, DISPATCH_CHUNK)
    ys = _experts(piece_start, piece_row, piece_cls, n_rows, xs, w_gate[li], w_up[li], w_down[li])
    yab = _sc_combine(ys, dest, 2 * t_all, COMBINE_CHUNK).reshape(2, t_all, d // 2)
    y_p, y_s = _final(x1, yab, route, final_norm_g.reshape(1, d), n * l, ns)

    hf = hfin.reshape(n, N_LANE_TILES, 2, 8, SSM_STATE)
    re_p = hf[:, :, 0].reshape(1, n, N_SSM_GROUPS, SSM_STATE)
    im_p = hf[:, :, 1].reshape(1, n, N_SSM_GROUPS, SSM_STATE)
    re_s = hr_s.reshape(1, ns, N_SSM_GROUPS, SSM_STATE)
    im_s = hi_s.reshape(1, ns, N_SSM_GROUPS, SSM_STATE)
    return (y_p.reshape(n, l, d), y_s.reshape(ns, 1, d), re_p, im_p, re_s, im_s,
            vrow.reshape(1, ns, 1, D_GMLP))
```

```python
import math

import jax
import jax.numpy as jnp
from jax import lax
from jax.experimental import pallas as pl
from jax.experimental.pallas import tpu as pltpu
from jax.experimental.pallas import tpu_sc as plsc

D_MODEL = 1024
D_SSM = 512
D_GMLP = 512
SSM_GROUP = 16
N_SSM_GROUPS = 32
SSM_STATE = 64
CHUNK = 128
N_GMLP_HEADS = 4
GMLP_HEAD = 128
N_EXPERT_GROUPS = 4
EXPERTS_PER_GROUP = 8
N_EXPERTS = 32
D_EXPERT = 512
D_IN = 2048
EPS = 1e-6

LANES = 128
SUBLANES = 8
N_LANE_TILES = D_SSM // LANES
STATE_COLS = N_SSM_GROUPS * SSM_STATE
TILE_STATE = STATE_COLS // N_LANE_TILES
VMEM_LIMIT = 56 * 1024 * 1024

SC_CORES = 2
SC_SUBCORES = 16
SC_LANES = 16
SC_WORKERS = SC_CORES * SC_SUBCORES

FRONT_TL = 512
SSM_LC = 256
SSM_BLK = 4
COEF_LB_RE, COEF_LB_IM, COEF_LBLK_RE, COEF_LBLK_IM = 0, 1, 2, 3
TOK_TM = 512
FINAL_TM = 1024
EXP_UNIT = 128
EXP_CLASSES = 4
W_SPLIT = 4
ROW_DMA_QUEUE = 1
DISPATCH_CHUNK = 80
COMBINE_CHUNK = 24

R_E1, R_E2, R_W1, R_W2, R_RANK1, R_RANK2, R_CODE1, R_CODE2 = 0, 1, 2, 3, 4, 5, 6, 7
CODE_BITS = 16
CODE_SHIFT = float(1 << CODE_BITS)

_INV_SQRT2 = 1.0 / math.sqrt(2.0)
_BF16 = jnp.bfloat16
_F32 = jnp.float32
_U32 = jnp.uint32


def _gelu(x):
    return 0.5 * x * (1.0 + lax.erf(x * _INV_SQRT2))


def _rms(x, g):
    return x * lax.rsqrt(jnp.mean(x * x, axis=-1, keepdims=True) + EPS) * g


def _dot(a, b):
    return jnp.dot(a, b, preferred_element_type=_F32)


def _dot_f32(a, b):
    return jnp.dot(a, b, preferred_element_type=_F32, precision=lax.Precision.HIGHEST)


def _pack_bf16_pair(x):
    w = x.shape[1] // 2
    hi = lax.bitcast_convert_type(x[:, :w].astype(_BF16).astype(_F32), _U32)
    lo = lax.bitcast_convert_type(x[:, w:].astype(_BF16).astype(_F32), _U32)
    return hi | (lo >> 16)


def _unpack_bf16_pair(p):
    hi = lax.bitcast_convert_type(p & jnp.uint32(0xFFFF0000), _F32)
    lo = lax.bitcast_convert_type(p << 16, _F32)
    return jnp.concatenate([hi, lo], axis=-1)


def _head_norm_gelu(vb, gn):
    v = _gelu(vb)
    parts = []
    for h in range(N_GMLP_HEADS):
        vh = v[:, h * GMLP_HEAD:(h + 1) * GMLP_HEAD]
        parts.append(vh * lax.rsqrt(jnp.mean(vh * vh, axis=-1, keepdims=True) + EPS))
    return jnp.concatenate(parts, axis=-1) * gn


def _front_prompt_kernel(x_ref, g1_ref, win_ref, gn_ref, ws_ref, bs_ref, gog_ref,
                         xa_ref, sg_ref, mixb_ref, win_bf):
    @pl.when((pl.program_id(0) == 0) & (pl.program_id(1) == 0))
    def _():
        win_bf[...] = win_ref[...].astype(_BF16)

    x = x_ref[0]
    hn = _rms(x, g1_ref[...]).astype(_BF16)
    z = _dot(hn, win_bf[...])
    xa_ref[0] = z[:, :D_SSM]
    sg_ref[0] = jax.nn.sigmoid(z[:, D_SSM:2 * D_SSM])
    ub = _gelu(z[:, 2 * D_SSM:2 * D_SSM + D_GMLP])
    vbn = _head_norm_gelu(z[:, 2 * D_SSM + D_GMLP:], gn_ref[...]).astype(_BF16)
    tl = x.shape[0]
    rows = []
    for c in range(tl // CHUNK):
        heads = []
        for h in range(N_GMLP_HEADS):
            vh = vbn[c * CHUNK:(c + 1) * CHUNK, h * GMLP_HEAD:(h + 1) * GMLP_HEAD]
            heads.append(_dot(ws_ref[h], vh) + bs_ref[:, h:h + 1])
        rows.append(jnp.concatenate(heads, axis=-1))
    s = jnp.concatenate(rows, axis=0)
    mixb_ref[0] = _rms(ub * s, gog_ref[...]).astype(_BF16)


def _front_prompt(x, g1, win, gn, ws_tril_bf, bs_t, gog):
    n, l, d = x.shape
    tl = FRONT_TL
    grid = (n, l // tl)
    const = lambda *shape: pl.BlockSpec(shape, lambda b, i: (0,) * len(shape))
    seq = lambda w: pl.BlockSpec((1, tl, w), lambda b, i: (b, i, 0))
    return pl.pallas_call(
        _front_prompt_kernel,
        grid=grid,
        in_specs=[seq(d), const(1, d), const(d, D_IN), const(1, D_GMLP),
                  const(N_GMLP_HEADS, CHUNK, CHUNK), const(CHUNK, N_GMLP_HEADS), const(1, D_GMLP)],
        out_specs=[seq(D_SSM), seq(D_SSM), seq(D_GMLP)],
        out_shape=[jax.ShapeDtypeStruct((n, l, D_SSM), _F32),
                   jax.ShapeDtypeStruct((n, l, D_SSM), _F32),
                   jax.ShapeDtypeStruct((n, l, D_GMLP), _BF16)],
        scratch_shapes=[pltpu.VMEM((d, D_IN), _BF16)],
        compiler_params=pltpu.CompilerParams(
            dimension_semantics=("arbitrary", "arbitrary"), vmem_limit_bytes=VMEM_LIMIT),
        name="front_prompt",
    )(x, g1, win, gn, ws_tril_bf, bs_t, gog)


def _ssm_prompt_kernel(xa_ref, sg_ref, v_ref, r_ref, coef_ref, dsk_ref, gos_ref,
                       mixa_ref, hfin_ref, s_ref, st_ref):
    lc = xa_ref.shape[1]
    nblk = lc // SSM_BLK
    rows = nblk * SUBLANES

    @pl.when(pl.program_id(0) == 0)
    def _():
        st_ref[...] = jnp.zeros_like(st_ref)

    def by_position(ref):
        t = pltpu.einshape("btc->tbc", ref[...]).reshape(nblk, SSM_BLK, SUBLANES, D_SSM)
        return [t[:, i].reshape(rows, D_SSM) for i in range(SSM_BLK)]

    xs = by_position(xa_ref)
    xs_bf = [x.astype(_BF16) for x in xs]
    xk = [jnp.concatenate([x[:, k * LANES:(k + 1) * LANES] for x in xs_bf], axis=-1)
          for k in range(N_LANE_TILES)]
    for k in range(N_LANE_TILES):
        s_ref[:, 2 * TILE_STATE * k:2 * TILE_STATE * (k + 1)] = _dot(xk[k], v_ref[k])

    for kk in range(0, N_LANE_TILES, 2):
        tiles = (kk, kk + 1)
        cols = [(2 * TILE_STATE * k, 2 * TILE_STATE * k + TILE_STATE) for k in tiles]
        lbs = [tuple(jnp.broadcast_to(coef_ref[row:row + 1, k * TILE_STATE:(k + 1) * TILE_STATE],
                                      (SUBLANES, TILE_STATE)) for row in (COEF_LBLK_RE, COEF_LBLK_IM))
               for k in tiles]

        def body(j, carry, cols=cols, lbs=lbs):
            r0 = pl.multiple_of(j * SUBLANES, SUBLANES)
            out = []
            for q, ((c_re, c_im), (lr, li)) in enumerate(zip(cols, lbs)):
                hr, hi = carry[2 * q], carry[2 * q + 1]
                sr = s_ref[pl.ds(r0, SUBLANES), c_re:c_re + TILE_STATE]
                si = s_ref[pl.ds(r0, SUBLANES), c_im:c_im + TILE_STATE]
                s_ref[pl.ds(r0, SUBLANES), c_re:c_re + TILE_STATE] = hr
                s_ref[pl.ds(r0, SUBLANES), c_im:c_im + TILE_STATE] = hi
                out += [lr * hr - li * hi + sr, lr * hi + li * hr + si]
            return tuple(out)

        init = tuple(st_ref[:, c:c + TILE_STATE] for c_pair in cols for c in c_pair)
        fin = lax.fori_loop(0, nblk, body, init, unroll=2)
        for q, (c_re, c_im) in enumerate(cols):
            st_ref[:, c_re:c_re + TILE_STATE] = fin[2 * q]
            st_ref[:, c_im:c_im + TILE_STATE] = fin[2 * q + 1]

    yk = []
    for k in range(N_LANE_TILES):
        h_in = s_ref[:, 2 * TILE_STATE * k:2 * TILE_STATE * (k + 1)].astype(_BF16)
        yk.append(_dot(jnp.concatenate([h_in, xk[k]], axis=-1), r_ref[k]))
    sgs = by_position(sg_ref)
    outs = []
    for i in range(SSM_BLK):
        y = jnp.concatenate([y_k[:, i * LANES:(i + 1) * LANES] for y_k in yk], axis=-1) + dsk_ref[...] * xs[i]
        outs.append(_rms(_gelu(y) * sgs[i], gos_ref[...]).reshape(nblk, SUBLANES, D_SSM))
    mixa = jnp.stack(outs, axis=1).reshape(lc, SUBLANES, D_SSM)
    mixa_ref[...] = pltpu.einshape("tbc->btc", mixa).astype(_BF16)
    hfin_ref[...] = st_ref[...]


def _ssm_prompt(xa, sg, v, r, coef, dsk, gos):
    n, l, _ = xa.shape
    lc = SSM_LC
    const = lambda *shape: pl.BlockSpec(shape, lambda i: (0,) * len(shape))
    seq_spec = pl.BlockSpec((n, lc, D_SSM), lambda i: (0, i, 0))
    return pl.pallas_call(
        _ssm_prompt_kernel,
        grid=(l // lc,),
        in_specs=[seq_spec, seq_spec, const(*v.shape), const(*r.shape),
                  const(*coef.shape), const(1, D_SSM), const(1, D_SSM)],
        out_specs=[seq_spec, const(n, 2 * STATE_COLS)],
        out_shape=[jax.ShapeDtypeStruct((n, l, D_SSM), _BF16),
                   jax.ShapeDtypeStruct((n, 2 * STATE_COLS), _F32)],
        scratch_shapes=[pltpu.VMEM((lc // SSM_BLK * n, 2 * STATE_COLS), _F32),
                        pltpu.VMEM((n, 2 * STATE_COLS), _F32)],
        compiler_params=pltpu.CompilerParams(
            dimension_semantics=("arbitrary",), vmem_limit_bytes=VMEM_LIMIT),
        name="ssm_prompt",
    )(xa, sg, v, r, coef, dsk, gos)


def _front_sample_kernel(x_ref, g1_ref, win_ref, gn_ref, w00_ref, b0_ref, gog_ref,
                         wb_ref, wc_ref, coef_ref, dsk_ref, gos_ref, h0r_ref, h0i_ref,
                         mix_ref, hr_ref, hi_ref, vrow_ref):
    x = x_ref[...]
    hn = _rms(x, g1_ref[...])
    z = _dot_f32(hn, win_ref[...])
    xa = z[:, :D_SSM]
    ys = []
    for k in range(N_LANE_TILES):
        bu = _dot_f32(xa[:, k * LANES:(k + 1) * LANES], wb_ref[k])
        sl = slice(k * TILE_STATE, (k + 1) * TILE_STATE)
        lr, li = coef_ref[COEF_LB_RE:COEF_LB_RE + 1, sl], coef_ref[COEF_LB_IM:COEF_LB_IM + 1, sl]
        h0r, h0i = h0r_ref[:, sl], h0i_ref[:, sl]
        nr = lr * h0r - li * h0i + bu[:, :TILE_STATE]
        ni = lr * h0i + li * h0r + bu[:, TILE_STATE:]
        hr_ref[:, sl] = nr
        hi_ref[:, sl] = ni
        ys.append(_dot_f32(jnp.concatenate([nr, ni], axis=-1), wc_ref[k]))
    y = jnp.concatenate(ys, axis=-1) + dsk_ref[...] * xa
    ya = _gelu(y) * jax.nn.sigmoid(z[:, D_SSM:2 * D_SSM])
    mix_ref[:, :D_SSM] = _rms(ya, gos_ref[...])
    ub = _gelu(z[:, 2 * D_SSM:2 * D_SSM + D_GMLP])
    vbn = _head_norm_gelu(z[:, 2 * D_SSM + D_GMLP:], gn_ref[...])
    vrow_ref[...] = vbn
    s = w00_ref[...] * vbn + b0_ref[...]
    mix_ref[:, D_SSM:] = _rms(ub * s, gog_ref[...])


def _front_sample(x, g1, win, gn, w00, b0, gog, wb, wc, coef, dsk, gos, h0r, h0i):
    n = x.shape[0]
    vmem = pl.BlockSpec(memory_space=pltpu.VMEM)
    return pl.pallas_call(
        _front_sample_kernel,
        in_specs=[vmem] * 14,
        out_specs=[vmem] * 4,
        out_shape=[jax.ShapeDtypeStruct((n, D_MODEL), _F32),
                   jax.ShapeDtypeStruct((n, STATE_COLS), _F32),
                   jax.ShapeDtypeStruct((n, STATE_COLS), _F32),
                   jax.ShapeDtypeStruct((n, D_GMLP), _F32)],
        compiler_params=pltpu.CompilerParams(vmem_limit_bytes=VMEM_LIMIT),
        name="front_sample",
    )(x, g1, win, gn, w00, b0, gog, wb, wc, coef, dsk, gos, h0r, h0i)


def _route(logits, base):
    tm = logits.shape[0]
    lt = logits.T
    ex = lt[:N_EXPERTS, :]
    gr = lt[N_EXPERTS:N_EXPERTS + SUBLANES, :]
    row_e = lax.broadcasted_iota(jnp.int32, ex.shape, 0).astype(_F32)
    row_g = lax.broadcasted_iota(jnp.int32, gr.shape, 0).astype(_F32)
    neg = jnp.float32(-jnp.inf)
    big = jnp.float32(LANES)
    is_g = row_g < N_EXPERT_GROUPS
    gl = jnp.where(is_g, gr, neg)
    gmax = jnp.max(gl, axis=0, keepdims=True)
    gi = jnp.min(jnp.where(gl == gmax, row_g, big), axis=0, keepdims=True)
    p_top = 1.0 / jnp.sum(jnp.where(is_g, jnp.exp(gl - gmax), 0.0), axis=0, keepdims=True)
    lo = gi * EXPERTS_PER_GROUP
    in_grp = (row_e >= lo) & (row_e < lo + EXPERTS_PER_GROUP)
    m1 = jnp.max(jnp.where(in_grp, ex, neg), axis=0, keepdims=True)
    i1 = jnp.min(jnp.where(in_grp & (ex == m1), row_e, big), axis=0, keepdims=True)
    rest = in_grp & (row_e != i1)
    m2 = jnp.max(jnp.where(rest, ex, neg), axis=0, keepdims=True)
    i2 = jnp.min(jnp.where(rest & (ex == m2), row_e, big), axis=0, keepdims=True)
    e2 = jnp.exp(m2 - m1)
    w1 = p_top / (1.0 + e2)
    w2 = p_top * e2 / (1.0 + e2)
    sel1 = row_e == i1
    sel2 = row_e == i2
    hits = jnp.where(sel1 | sel2, 1.0, 0.0)
    src = lax.broadcasted_iota(jnp.int32, (tm, tm), 0)
    dst = lax.broadcasted_iota(jnp.int32, (tm, tm), 1)
    before = _dot(hits.astype(_BF16), jnp.where(src < dst, 1.0, 0.0).astype(_BF16)) + base
    rank1 = jnp.sum(jnp.where(sel1, before, 0.0), axis=0, keepdims=True)
    rank2 = jnp.sum(jnp.where(sel2, before, 0.0), axis=0, keepdims=True)
    fields = {R_E1: i1, R_E2: i2, R_W1: w1, R_W2: w2, R_RANK1: rank1, R_RANK2: rank2,
              R_CODE1: i1 * CODE_SHIFT + rank1, R_CODE2: i2 * CODE_SHIFT + rank2}
    row8 = lax.broadcasted_iota(jnp.int32, (SUBLANES, tm), 0)
    route_t = jnp.zeros((SUBLANES, tm), _F32)
    for r, val in fields.items():
        route_t = jnp.where(row8 == r, val, route_t)
    route = jnp.concatenate([route_t, jnp.zeros((LANES - SUBLANES, tm), _F32)], axis=0).T
    return route_t, route, base + jnp.sum(hits, axis=1, keepdims=True)


def _mixer_out_prompt_kernel(x_ref, mixa_ref, mixb_ref, wo_ref, g2_ref, wr_ref, br_ref,
                             x1_ref, xn_ref, route_ref, route_t_ref, cnt_ref, base_ref, logits_ref, wo_bf):
    i = pl.program_id(0)

    @pl.when(i == 0)
    def _():
        base_ref[...] = jnp.zeros_like(base_ref)
        logits_ref[...] = jnp.zeros_like(logits_ref)
        wo_bf[...] = wo_ref[...].astype(_BF16)

    prev_logits = logits_ref[...]
    x1 = x_ref[0] + _dot(mixa_ref[0], wo_bf[:D_SSM, :]) + _dot(mixb_ref[0], wo_bf[D_SSM:, :])
    xn = _rms(x1, g2_ref[...])
    x1_ref[...] = _pack_bf16_pair(x1)
    xn_ref[...] = _pack_bf16_pair(xn)
    logits_ref[...] = _dot(xn.astype(_BF16), wr_ref[...]) + br_ref[...]
    route_t, route, base = _route(prev_logits, base_ref[...])
    route_ref[...] = route
    route_t_ref[...] = route_t
    base = jnp.where(i >= 1, base, base_ref[...])
    base_ref[...] = base
    cnt_ref[...] = base


def _mixer_out_sample_kernel(x_ref, mix_ref, wo_ref, g2_ref, wr_ref, br_ref, cnt_in_ref,
                             x1_in, xn_in, route_in, route_t_in,
                             x1_ref, xn_ref, route_ref, route_t_ref, cnt_ref):
    del x1_in, xn_in, route_in, route_t_in
    x1 = (x_ref[...] + _dot_f32(mix_ref[:, :D_SSM], wo_ref[:D_SSM, :])
          + _dot_f32(mix_ref[:, D_SSM:], wo_ref[D_SSM:, :]))
    xn = _rms(x1, g2_ref[...])
    logits = _dot_f32(xn, wr_ref[...]) + br_ref[...]
    route_t, route, base = _route(logits, cnt_in_ref[...])
    x1_ref[...] = _pack_bf16_pair(x1)
    xn_ref[...] = _pack_bf16_pair(xn)
    route_ref[...] = route
    route_t_ref[...] = route_t
    cnt_ref[...] = base


def _mixer_out(x_p, mixa, mixb, x_s, mix_s, wo, g2, wr, br):
    n, l, d = x_p.shape
    ns = x_s.shape[0]
    t_all = n * l + ns
    tm = TOK_TM
    per_seq = l // tm
    n_tiles = n * per_seq
    cur = lambda i: jnp.minimum(i, n_tiles - 1)
    prev = lambda i: jnp.maximum(i - 1, 0)
    const = lambda *shape: pl.BlockSpec(shape, lambda i: (0,) * len(shape))
    seq = lambda w: pl.BlockSpec((1, tm, w), lambda i: (cur(i) // per_seq, cur(i) % per_seq, 0))
    tok = lambda w, which: pl.BlockSpec((tm, w), lambda i: (which(i), 0))
    tok_shapes = [jax.ShapeDtypeStruct((t_all, d // 2), _U32),
                  jax.ShapeDtypeStruct((t_all, d // 2), _U32),
                  jax.ShapeDtypeStruct((t_all, LANES), _F32),
                  jax.ShapeDtypeStruct((SUBLANES, t_all), _F32)]
    cnt_shape = jax.ShapeDtypeStruct((N_EXPERTS, 1), _F32)
    x1, xn, route, route_t, cnt = pl.pallas_call(
        _mixer_out_prompt_kernel,
        grid=(n_tiles + 1,),
        in_specs=[seq(d), seq(D_SSM), seq(D_GMLP),
                  const(d, d), const(1, d), const(d, LANES), const(1, LANES)],
        out_specs=[tok(d // 2, cur), tok(d // 2, cur), tok(LANES, prev),
                   pl.BlockSpec((SUBLANES, tm), lambda i: (0, prev(i))), const(N_EXPERTS, 1)],
        out_shape=tok_shapes + [cnt_shape],
        scratch_shapes=[pltpu.VMEM((N_EXPERTS, 1), _F32), pltpu.VMEM((tm, LANES), _F32),
                        pltpu.VMEM((d, d), _BF16)],
        compiler_params=pltpu.CompilerParams(
            dimension_semantics=("arbitrary",), vmem_limit_bytes=VMEM_LIMIT),
        name="mixer_out_prompt",
    )(x_p, mixa, mixb, wo, g2, wr.astype(_BF16), br)
    tail = (n * l) // ns
    c1 = lambda *shape: pl.BlockSpec(shape, lambda i: (0,) * len(shape))
    anyspec = pl.BlockSpec(memory_space=pl.ANY)
    tail_spec = lambda w: pl.BlockSpec((ns, w), lambda i: (tail, 0))
    return pl.pallas_call(
        _mixer_out_sample_kernel,
        grid=(1,),
        in_specs=[c1(ns, d), c1(ns, d), c1(d, d), c1(1, d), c1(d, LANES), c1(1, LANES), c1(N_EXPERTS, 1),
                  anyspec, anyspec, anyspec, anyspec],
        out_specs=[tail_spec(d // 2), tail_spec(d // 2), tail_spec(LANES),
                   pl.BlockSpec((SUBLANES, ns), lambda i: (0, tail)), c1(N_EXPERTS, 1)],
        out_shape=tok_shapes + [cnt_shape],
        input_output_aliases={7: 0, 8: 1, 9: 2, 10: 3},
        compiler_params=pltpu.CompilerParams(
            dimension_semantics=("arbitrary",), vmem_limit_bytes=VMEM_LIMIT),
        name="mixer_out_sample",
    )(x_s, mix_s, wo, g2, wr, br, cnt, x1, xn, route, route_t)


def _sc_stream(n_chunks, gather, write):
    gather(0).start()
    for j in range(n_chunks):
        if j + 1 < n_chunks:
            if j >= 1:
                write(j - 1).wait()
            gather(j + 1).start()
        gather(j).wait()
        write(j).start()
    if n_chunks >= 2:
        write(n_chunks - 2).wait()
    write(n_chunks - 1).wait()


def _sc_mesh():
    return plsc.VectorSubcoreMesh(core_axis_name="c", subcore_axis_name="s",
                                  num_cores=SC_CORES, num_subcores=SC_SUBCORES)


def _sc_buffers(chunk, w, dtype):
    return [pltpu.VMEM((chunk, w), dtype), pltpu.VMEM((chunk, w), dtype)] + [pltpu.SemaphoreType.DMA] * 4


def _sc_combine(table, idx, n_out, chunk):
    w = table.shape[1]
    rows_w = n_out // SC_WORKERS
    n_chunks = rows_w // chunk
    assert rows_w * SC_WORKERS == n_out and n_chunks * chunk == rows_w and rows_w % SUBLANES == 0

    def body(table_hbm, idx_hbm, out_hbm, idx_v, buf0, buf1, g0, g1, w0, w1):
        wid = lax.axis_index("s") * SC_CORES + lax.axis_index("c")
        base = pl.multiple_of(wid * rows_w, SUBLANES)
        pltpu.sync_copy(idx_hbm.at[pl.ds(base, rows_w)], idx_v)
        bufs, gsems, wsems = (buf0, buf1), (g0, g1), (w0, w1)

        def gather(j):
            return pltpu.make_async_copy(table_hbm.at[idx_v.at[pl.ds(j * chunk, chunk)]], bufs[j % 2], gsems[j % 2])

        def write(j):
            return pltpu.make_async_copy(bufs[j % 2], out_hbm.at[pl.ds(base + j * chunk, chunk)], wsems[j % 2])

        _sc_stream(n_chunks, gather, write)

    return pl.kernel(
        body,
        out_type=jax.ShapeDtypeStruct((n_out, w), table.dtype),
        mesh=_sc_mesh(),
        scratch_types=[pltpu.VMEM((rows_w,), jnp.int32)] + _sc_buffers(chunk, w, table.dtype),
        compiler_params=pltpu.CompilerParams(use_tc_tiling_on_sc=True),
        name="sc_combine",
    )(table, idx)


def _sc_dispatch(table, codes, start_row, n_out, chunk):
    t_all, w = table.shape
    n_pad = codes.shape[0]
    n_ent = 2 * t_all
    ent_w = n_pad // SC_WORKERS
    n_chunks = ent_w // chunk
    per_chunk = chunk // SC_LANES
    trash = n_out - (n_pad - n_ent)
    assert ent_w * SC_WORKERS == n_pad and n_chunks * chunk == ent_w
    assert per_chunk * SC_LANES == chunk and chunk <= LANES and n_pad - n_ent <= t_all

    def body(table_hbm, code_hbm, start_hbm, out_hbm, dest_hbm,
             code_v, dest_v, tok_v, dst_v, start_v, buf0, buf1, g0, g1, w0, w1):
        wid = lax.axis_index("s") * SC_CORES + lax.axis_index("c")
        ebase = pl.multiple_of(wid * ent_w, SUBLANES)
        pltpu.sync_copy(code_hbm.at[pl.ds(ebase, ent_w)], code_v)
        pltpu.sync_copy(start_hbm, start_v)
        lane = lax.iota(jnp.int32, SC_LANES)
        for j in range(n_chunks):
            for c in range(per_chunk):
                off = j * chunk + c * SC_LANES
                ent = ebase + off + lane
                code = code_v[pl.ds(off, SC_LANES)]
                d = plsc.load_gather(start_v, [code >> CODE_BITS]) + (code & ((1 << CODE_BITS) - 1))
                d = jnp.where(ent >= n_ent, trash + (ent - n_ent), d)
                tok = jnp.where(ent >= t_all, ent - t_all, ent)
                tok = jnp.where(tok >= t_all, tok - t_all, tok)
                dest_v[pl.ds(off, SC_LANES)] = d
                dst_v[j, pl.ds(c * SC_LANES, SC_LANES)] = d
                tok_v[j, pl.ds(c * SC_LANES, SC_LANES)] = tok
        pltpu.sync_copy(dest_v, dest_hbm.at[pl.ds(ebase, ent_w)])
        bufs, gsems, wsems = (buf0, buf1), (g0, g1), (w0, w1)

        def gather(j):
            return pltpu.make_async_copy(table_hbm.at[tok_v.at[j]], bufs[j % 2], gsems[j % 2])

        def scatter(j):
            return pltpu.make_async_copy(bufs[j % 2], out_hbm.at[dst_v.at[j]], wsems[j % 2])

        _sc_stream(n_chunks, gather, scatter)

    return pl.kernel(
        body,
        out_type=(jax.ShapeDtypeStruct((n_out, w), table.dtype), jax.ShapeDtypeStruct((n_pad,), jnp.int32)),
        mesh=_sc_mesh(),
        scratch_types=([pltpu.VMEM((ent_w,), jnp.int32), pltpu.VMEM((ent_w,), jnp.int32),
                        pltpu.VMEM((n_chunks, chunk), jnp.int32), pltpu.VMEM((n_chunks, chunk), jnp.int32),
                        pltpu.VMEM((LANES,), jnp.int32)] + _sc_buffers(chunk, w, table.dtype)),
        compiler_params=pltpu.CompilerParams(use_tc_tiling_on_sc=True, needs_layout_passes=False),
        name="sc_dispatch",
    )(table, codes, start_row)


def _experts_kernel(piece_start_ref, piece_row_ref, piece_cls_ref, *refs):
    n_w = 3 * W_SPLIT
    wg_refs, wu_refs, wd_refs = refs[:W_SPLIT], refs[W_SPLIT:2 * W_SPLIT], refs[2 * W_SPLIT:n_w]
    xs_hbm, ys_hbm, wg_bf, wu_bf, wd_bf, xbuf, ybuf, xsem, ysem = refs[n_w:]
    e = pl.program_id(0)
    g0 = piece_start_ref[e]
    n_here = piece_start_ref[e + 1] - g0
    n_total = piece_start_ref[N_EXPERTS]

    def per_class(g, fn):
        cls = piece_cls_ref[g]
        row = pl.multiple_of(piece_row_ref[g], EXP_UNIT)
        for c in range(1, EXP_CLASSES + 1):
            pl.when(cls == c)(lambda c=c: fn(c * EXP_UNIT, row))

    def x_copy(slot, rows, row):
        return pltpu.make_async_copy(xs_hbm.at[pl.ds(row, rows)], xbuf.at[slot, pl.ds(0, rows)], xsem.at[slot])

    def y_copy(slot, rows, row):
        return pltpu.make_async_copy(ybuf.at[slot, pl.ds(0, rows)], ys_hbm.at[pl.ds(row, rows)], ysem.at[slot])

    @pl.when((e == 0) & (n_total > 0))
    def _():
        per_class(0, lambda rows, row: x_copy(0, rows, row).start(priority=ROW_DMA_QUEUE))

    for dst, chunks in ((wg_bf, wg_refs), (wu_bf, wu_refs), (wd_bf, wd_refs)):
        rows = dst.shape[0] // W_SPLIT
        for q, src in enumerate(chunks):
            dst[q * rows:(q + 1) * rows, :] = src[0, 0].astype(_BF16)

    def piece(j, carry):
        g = g0 + j
        slot = lax.rem(g, 2)
        per_class(g, lambda rows, row: x_copy(slot, rows, row).wait())

        @pl.when(g + 1 < n_total)
        def _():
            per_class(g + 1, lambda rows, row: x_copy(1 - slot, rows, row).start(priority=ROW_DMA_QUEUE))

        @pl.when(g >= 2)
        def _():
            per_class(g - 2, lambda rows, row: y_copy(slot, rows, row).wait())

        def compute(rows, row):
            x = _unpack_bf16_pair(xbuf[slot, pl.ds(0, rows)]).astype(_BF16)
            a = _dot(x, wg_bf[...])
            u = _dot(x, wu_bf[...])
            h = (a * jax.nn.sigmoid(a) * u).astype(_BF16)
            ybuf[slot, pl.ds(0, rows)] = _pack_bf16_pair(_dot(h, wd_bf[...]))
            y_copy(slot, rows, row).start(priority=ROW_DMA_QUEUE)

        per_class(g, compute)
        return carry

    lax.fori_loop(0, n_here, piece, 0)

    @pl.when(e == N_EXPERTS - 1)
    def _():
        @pl.when(n_total >= 2)
        def _():
            per_class(n_total - 2, lambda rows, row: y_copy(lax.rem(n_total, 2), rows, row).wait())

        @pl.when(n_total >= 1)
        def _():
            per_class(n_total - 1, lambda rows, row: y_copy(lax.rem(n_total - 1, 2), rows, row).wait())


def _experts(piece_start, piece_row, piece_cls, n_rows, xs, w_gate, w_up, w_down):
    dh = xs.shape[1]
    d = 2 * dh
    tm = EXP_UNIT * EXP_CLASSES
    anyspec = pl.BlockSpec(memory_space=pl.ANY)

    def chunk_specs(rows, cols):
        return [pl.BlockSpec((1, 1, rows // W_SPLIT, cols), lambda e, ps, pr, pc, q=q: (e, q, 0, 0))
                for q in range(W_SPLIT)]

    split = lambda w: w.reshape(w.shape[0], W_SPLIT, w.shape[1] // W_SPLIT, w.shape[2])
    grid_spec = pltpu.PrefetchScalarGridSpec(
        num_scalar_prefetch=3,
        grid=(N_EXPERTS,),
        in_specs=(chunk_specs(d, D_EXPERT) + chunk_specs(d, D_EXPERT) + chunk_specs(D_EXPERT, d) + [anyspec]),
        out_specs=anyspec,
        scratch_shapes=[pltpu.VMEM((d, D_EXPERT), _BF16), pltpu.VMEM((d, D_EXPERT), _BF16),
                        pltpu.VMEM((D_EXPERT, d), _BF16),
                        pltpu.VMEM((2, tm, dh), _U32), pltpu.VMEM((2, tm, dh), _U32),
                        pltpu.SemaphoreType.DMA((2,)), pltpu.SemaphoreType.DMA((2,))],
    )
    return pl.pallas_call(
        _experts_kernel,
        grid_spec=grid_spec,
        out_shape=jax.ShapeDtypeStruct((n_rows, dh), _U32),
        compiler_params=pltpu.CompilerParams(
            dimension_semantics=("arbitrary",), vmem_limit_bytes=VMEM_LIMIT),
        name="experts",
    )(piece_start, piece_row, piece_cls, *([split(w_gate)] * W_SPLIT), *([split(w_up)] * W_SPLIT),
      *([split(w_down)] * W_SPLIT), xs)


def _final_kernel(x1_ref, ya_ref, yb_ref, route_ref, gf_ref, y_ref):
    route = route_ref[...]
    x2 = (_unpack_bf16_pair(x1_ref[...]) + route[:, R_W1:R_W1 + 1] * _unpack_bf16_pair(ya_ref[...])
          + route[:, R_W2:R_W2 + 1] * _unpack_bf16_pair(yb_ref[...]))
    y_ref[...] = _rms(x2, gf_ref[...])


def _final(x1, yab, route, gf, n_prompt, n_sample):
    d = 2 * x1.shape[1]

    def call(tm, first_block, n_rows, name):
        tok = lambda w: pl.BlockSpec((tm, w), lambda i: (first_block + i, 0))
        sel = lambda k: pl.BlockSpec((None, tm, d // 2), lambda i: (k, first_block + i, 0))
        return pl.pallas_call(
            _final_kernel,
            grid=(n_rows // tm,),
            in_specs=[tok(d // 2), sel(0), sel(1), tok(LANES), pl.BlockSpec((1, d), lambda i: (0, 0))],
            out_specs=pl.BlockSpec((tm, d), lambda i: (i, 0)),
            out_shape=jax.ShapeDtypeStruct((n_rows, d), _F32),
            compiler_params=pltpu.CompilerParams(
                dimension_semantics=("arbitrary",), vmem_limit_bytes=VMEM_LIMIT),
            name=name,
        )(x1, yab, yab, route, gf)

    return (call(FINAL_TM, 0, n_prompt, "final_prompt"),
            call(n_sample, n_prompt // n_sample, n_sample, "final_sample"))


def _powers(lam_re, lam_im, dt):
    out = []
    for m in range(SSM_BLK + 1):
        mag = jnp.exp(m * lam_re * dt)
        ang = m * lam_im * dt
        out.append((mag * jnp.cos(ang), mag * jnp.sin(ang)))
    return out


def _spread(x, copies):
    w = x.shape[1]
    src = lax.broadcasted_iota(jnp.int32, (w, w * copies), 0)
    dst = lax.broadcasted_iota(jnp.int32, (w, w * copies), 1)
    return _dot_f32(x, jnp.where(dst % w == src, 1.0, 0.0))


def _ssm_prep_kernel(lam_ref, b_re, b_im, c_re, c_im, v_ref, r_ref, wb_ref, wc_ref, coef_ref):
    n_p, n_h = SSM_STATE, SSM_GROUP
    lr, li, dt = lam_ref[0:1, :], lam_ref[1:2, :], lam_ref[2:3, :]
    pw = _powers(lr, li, dt)
    den = lr * lr + li * li
    nr, ni = pw[1][0] - 1.0, pw[1][1]
    k_re = (nr * lr + ni * li) / den
    k_im = (ni * lr - nr * li) / den
    coef_ref[...] = jnp.concatenate(
        [pw[1][0], pw[1][1], pw[SSM_BLK][0], pw[SSM_BLK][1], jnp.zeros((SUBLANES - 4, TILE_STATE), _F32)], axis=0)

    on_diag_b = (lax.broadcasted_iota(jnp.int32, (TILE_STATE, LANES), 0) // n_p
                 == lax.broadcasted_iota(jnp.int32, (TILE_STATE, LANES), 1) // n_h)
    rows_gp = lambda ref: ref[...].reshape(TILE_STATE, n_h)
    bt_re = jnp.where(on_diag_b, _spread(rows_gp(b_re), SUBLANES), 0.0).T
    bt_im = jnp.where(on_diag_b, _spread(rows_gp(b_im), SUBLANES), 0.0).T
    bb_re = k_re * bt_re - k_im * bt_im
    bb_im = k_re * bt_im + k_im * bt_re
    wb_ref[0] = jnp.concatenate([bb_re, bb_im], axis=1)
    v_rows = []
    for s in range(SSM_BLK):
        pr, pi = pw[SSM_BLK - 1 - s]
        v_rows.append(jnp.concatenate([pr * bb_re - pi * bb_im, pr * bb_im + pi * bb_re], axis=1))
    v_ref[0] = jnp.concatenate(v_rows, axis=0).astype(v_ref.dtype)

    on_diag_c = (lax.broadcasted_iota(jnp.int32, (LANES, TILE_STATE), 0) // n_h
                 == lax.broadcasted_iota(jnp.int32, (LANES, TILE_STATE), 1) // n_p)
    rows_gh = lambda ref: ref[...].reshape(LANES, n_p)
    ct_re = jnp.where(on_diag_c, _spread(rows_gh(c_re), SUBLANES), 0.0)
    ct_im = jnp.where(on_diag_c, _spread(rows_gh(c_im), SUBLANES), 0.0)
    cl = [(ct_re * pr - ct_im * pi, ct_re * pi + ct_im * pr) for pr, pi in pw]
    wc_ref[0] = jnp.concatenate([cl[0][0], -cl[0][1]], axis=1).T
    nt = lambda a, b: lax.dot_general(a, b, (((1,), (1,)), ((), ())), precision=lax.Precision.HIGHEST,
                                      preferred_element_type=_F32)
    direct = [nt(cl[m][0], bb_re) - nt(cl[m][1], bb_im) for m in range(SSM_BLK)]
    zero = jnp.zeros((LANES, LANES), _F32)
    rt = jnp.concatenate(
        [jnp.concatenate([cl[i + 1][0], -cl[i + 1][1]]
                         + [direct[i - s] if s <= i else zero for s in range(SSM_BLK)], axis=1)
         for i in range(SSM_BLK)], axis=0)
    r_ref[0] = rt.T.astype(r_ref.dtype)


def _ssm_params(lam_re, lam_im, log_dt, b_re, b_im, c_re, c_im, d_skip):
    n_g, n_p, n_h = N_SSM_GROUPS, SSM_STATE, SSM_GROUP
    dt = jnp.repeat(jnp.exp(log_dt), n_p)
    lam = jnp.zeros((SUBLANES, STATE_COLS), _F32).at[0].set(lam_re.reshape(-1)).at[1].set(
        lam_im.reshape(-1)).at[2].set(dt)
    groups = lambda r, c: pl.BlockSpec((SUBLANES, r, c), lambda k: (k, 0, 0))
    out3 = lambda rows, w: pl.BlockSpec((1, rows, w), lambda k: (k, 0, 0))
    cols = pl.BlockSpec((SUBLANES, TILE_STATE), lambda k: (0, k))
    k_blk = SSM_BLK * LANES
    v, r, wb, wc, coef = pl.pallas_call(
        _ssm_prep_kernel,
        grid=(N_LANE_TILES,),
        in_specs=[cols, groups(n_p, n_h), groups(n_p, n_h), groups(n_h, n_p), groups(n_h, n_p)],
        out_specs=[out3(k_blk, 2 * TILE_STATE), out3(2 * TILE_STATE + k_blk, k_blk),
                   out3(LANES, 2 * TILE_STATE), out3(2 * TILE_STATE, LANES), cols],
        out_shape=[jax.ShapeDtypeStruct((N_LANE_TILES, k_blk, 2 * TILE_STATE), _BF16),
                   jax.ShapeDtypeStruct((N_LANE_TILES, 2 * TILE_STATE + k_blk, k_blk), _BF16),
                   jax.ShapeDtypeStruct((N_LANE_TILES, LANES, 2 * TILE_STATE), _F32),
                   jax.ShapeDtypeStruct((N_LANE_TILES, 2 * TILE_STATE, LANES), _F32),
                   jax.ShapeDtypeStruct((SUBLANES, STATE_COLS), _F32)],
        compiler_params=pltpu.CompilerParams(
            dimension_semantics=("arbitrary",), vmem_limit_bytes=VMEM_LIMIT),
        name="ssm_prep",
    )(lam, b_re, b_im, c_re, c_im)
    return wb, wc, v, r, coef, d_skip.reshape(1, D_SSM)


def _dispatch_plan(route_t, cnt):
    t_all = route_t.shape[1]
    codes = route_t[R_CODE1:R_CODE2 + 1].astype(jnp.int32).reshape(-1)
    per_pass = SC_WORKERS * DISPATCH_CHUNK
    codes = jnp.pad(codes, (0, -(2 * t_all) % per_pass))
    counts = cnt[:, 0].astype(jnp.int32)
    zero = jnp.zeros((1,), jnp.int32)
    units = (counts + EXP_UNIT - 1) // EXP_UNIT
    unit_start = jnp.concatenate([zero, jnp.cumsum(units)])
    start_row = jnp.zeros((LANES,), jnp.int32).at[:N_EXPERTS].set(unit_start[:N_EXPERTS] * EXP_UNIT)
    pieces = (units + EXP_CLASSES - 1) // EXP_CLASSES
    piece_start = jnp.concatenate([zero, jnp.cumsum(pieces)])
    tm = EXP_UNIT * EXP_CLASSES
    max_units = (2 * t_all + N_EXPERTS * (EXP_UNIT - 1)) // EXP_UNIT
    max_pieces = (max_units + N_EXPERTS * (EXP_CLASSES - 1)) // EXP_CLASSES
    g = jnp.arange(max_pieces, dtype=jnp.int32)
    owner = ((g[:, None] >= piece_start[None, :-1]) & (g[:, None] < piece_start[None, 1:])).astype(jnp.int32)
    pick = lambda table: jnp.sum(owner * table[None, :], axis=1)
    first_unit = pick(unit_start[:-1]) + (g - pick(piece_start[:-1])) * EXP_CLASSES
    piece_row = first_unit * EXP_UNIT
    piece_cls = jnp.clip(pick(unit_start[1:]) - first_unit, 1, EXP_CLASSES)
    n_rows = (max_units * EXP_UNIT + tm - 1) // tm * tm + tm
    return codes, start_row, n_rows, piece_start, piece_row, piece_cls


def kernel(x_prompt, x_sample, state_ssm_re, state_ssm_im, norm1_g, w_in, lam_re, lam_im, log_dt, ssm_b_re, ssm_b_im, ssm_c_re, ssm_c_im, ssm_d, gmlp_norm_g, gmlp_w_s, gmlp_b_s, out_norm_ssm_g, out_norm_gmlp_g, w_out, norm2_g, w_router_group, b_router_group, w_router_expert, b_router_expert, w_gate, w_up, w_down, final_norm_g):
    n, l, d = x_prompt.shape
    ns = x_sample.shape[0]
    t_all = n * l + ns
    li = 0
    g1 = norm1_g[li].reshape(1, d)
    gn = gmlp_norm_g[li].reshape(1, D_GMLP)
    tril = jnp.tril(jnp.ones((CHUNK, CHUNK), dtype=bool))
    ws_tril = jnp.where(tril[None], gmlp_w_s[li], 0.0)
    bs = gmlp_b_s[li]
    gog = out_norm_gmlp_g[li].reshape(1, D_GMLP)
    gos = out_norm_ssm_g[li].reshape(1, D_SSM)
    wb, wc, v_blk, r_blk, coef, dsk = _ssm_params(
        lam_re[li], lam_im[li], log_dt[li], ssm_b_re[li], ssm_b_im[li], ssm_c_re[li], ssm_c_im[li], ssm_d[li])
    g2 = norm2_g[li].reshape(1, d)
    pad = LANES - N_EXPERTS - N_EXPERT_GROUPS
    wr = jnp.concatenate([w_router_expert[li], w_router_group[li], jnp.zeros((d, pad), _F32)], axis=1)
    br = jnp.concatenate([b_router_expert[li], b_router_group[li], jnp.zeros((pad,), _F32)]).reshape(1, LANES)

    xa, sg, mixb = _front_prompt(x_prompt, g1, w_in[li], gn, ws_tril.astype(_BF16), bs.T, gog)
    mixa, hfin = _ssm_prompt(xa, sg, v_blk, r_blk, coef, dsk, gos)
    w00 = jnp.repeat(ws_tril[:, 0, 0], GMLP_HEAD).reshape(1, D_GMLP)
    b0 = jnp.repeat(bs[:, 0], GMLP_HEAD).reshape(1, D_GMLP)
    mix_s, hr_s, hi_s, vrow = _front_sample(
        x_sample.reshape(ns, d), g1, w_in[li], gn, w00, b0, gog, wb, wc, coef, dsk, gos,
        state_ssm_re[li].reshape(ns, STATE_COLS), state_ssm_im[li].reshape(ns, STATE_COLS))

    x1, xn, route, route_t, cnt = _mixer_out(x_prompt, mixa, mixb, x_sample.reshape(ns, d), mix_s,
                                             w_out[li], g2, wr, br)
    codes, start_row, n_rows, piece_start, piece_row, piece_cls = _dispatch_plan(route_t, cnt)
    xs, dest = _sc_dispatch(xn, codes, start_row, n_rows, DISPATCH_CHUNK)
    ys = _experts(piece_start, piece_row, piece_cls, n_rows, xs, w_gate[li], w_up[li], w_down[li])
    yab = _sc_combine(ys, dest, 2 * t_all, COMBINE_CHUNK).reshape(2, t_all, d // 2)
    y_p, y_s = _final(x1, yab, route, final_norm_g.reshape(1, d), n * l, ns)

    hf = hfin.reshape(n, N_LANE_TILES, 2, 8, SSM_STATE)
    re_p = hf[:, :, 0].reshape(1, n, N_SSM_GROUPS, SSM_STATE)
    im_p = hf[:, :, 1].reshape(1, n, N_SSM_GROUPS, SSM_STATE)
    re_s = hr_s.reshape(1, ns, N_SSM_GROUPS, SSM_STATE)
    im_s = hi_s.reshape(1, ns, N_SSM_GROUPS, SSM_STATE)
    return (y_p.reshape(n, l, d), y_s.reshape(ns, 1, d), re_p, im_p, re_s, im_s,
            vrow.reshape(1, ns, 1, D_GMLP))
```

```python
import math

import jax
import jax.numpy as jnp
from jax import lax
from jax.experimental import pallas as pl
from jax.experimental.pallas import tpu as pltpu
from jax.experimental.pallas import tpu_sc as plsc

D_MODEL = 1024
D_SSM = 512
D_GMLP = 512
SSM_GROUP = 16
N_SSM_GROUPS = 32
SSM_STATE = 64
CHUNK = 128
N_GMLP_HEADS = 4
GMLP_HEAD = 128
N_EXPERT_GROUPS = 4
EXPERTS_PER_GROUP = 8
N_EXPERTS = 32
D_EXPERT = 512
D_IN = 2048
EPS = 1e-6

LANES = 128
SUBLANES = 8
N_LANE_TILES = D_SSM // LANES
STATE_COLS = N_SSM_GROUPS * SSM_STATE
TILE_STATE = STATE_COLS // N_LANE_TILES
VMEM_LIMIT = 56 * 1024 * 1024

SC_CORES = 2
SC_SUBCORES = 16
SC_LANES = 16
SC_WORKERS = SC_CORES * SC_SUBCORES

FRONT_TL = 512
SSM_LC = 256
SSM_BLK = 4
COEF_LB_RE, COEF_LB_IM, COEF_LBLK_RE, COEF_LBLK_IM = 0, 1, 2, 3
TOK_TM = 512
FINAL_TM = 1024
EXP_UNIT = 128
EXP_CLASSES = 8
W_SPLIT = 4
DISPATCH_CHUNK = 80
COMBINE_CHUNK = 24

R_E1, R_E2, R_W1, R_W2, R_RANK1, R_RANK2, R_CODE1, R_CODE2 = 0, 1, 2, 3, 4, 5, 6, 7
CODE_BITS = 16
CODE_SHIFT = float(1 << CODE_BITS)

_INV_SQRT2 = 1.0 / math.sqrt(2.0)
_BF16 = jnp.bfloat16
_F32 = jnp.float32
_U32 = jnp.uint32


def _gelu(x):
    return 0.5 * x * (1.0 + lax.erf(x * _INV_SQRT2))


def _rms(x, g):
    return x * lax.rsqrt(jnp.mean(x * x, axis=-1, keepdims=True) + EPS) * g


def _dot(a, b):
    return jnp.dot(a, b, preferred_element_type=_F32)


def _dot_f32(a, b):
    return jnp.dot(a, b, preferred_element_type=_F32, precision=lax.Precision.HIGHEST)


def _pack_bf16_pair(x):
    w = x.shape[1] // 2
    hi = lax.bitcast_convert_type(x[:, :w].astype(_BF16).astype(_F32), _U32)
    lo = lax.bitcast_convert_type(x[:, w:].astype(_BF16).astype(_F32), _U32)
    return hi | (lo >> 16)


def _unpack_bf16_pair(p):
    hi = lax.bitcast_convert_type(p & jnp.uint32(0xFFFF0000), _F32)
    lo = lax.bitcast_convert_type(p << 16, _F32)
    return jnp.concatenate([hi, lo], axis=-1)


def _head_norm_gelu(vb, gn):
    v = _gelu(vb)
    parts = []
    for h in range(N_GMLP_HEADS):
        vh = v[:, h * GMLP_HEAD:(h + 1) * GMLP_HEAD]
        parts.append(vh * lax.rsqrt(jnp.mean(vh * vh, axis=-1, keepdims=True) + EPS))
    return jnp.concatenate(parts, axis=-1) * gn


def _front_prompt_kernel(x_ref, g1_ref, win_ref, gn_ref, ws_ref, bs_ref, gog_ref,
                         xa_ref, sg_ref, mixb_ref, win_bf, z_ref):
    @pl.when(pl.program_id(0) == 0)
    def _():
        win_bf[...] = win_ref[...].astype(_BF16)
        z_ref[...] = jnp.zeros_like(z_ref)

    z = z_ref[...]
    x = x_ref[0]
    hn = _rms(x, g1_ref[...]).astype(_BF16)
    z_ref[...] = _dot(hn, win_bf[...])
    xa_ref[0] = z[:, :D_SSM]
    sg_ref[0] = jax.nn.sigmoid(z[:, D_SSM:2 * D_SSM])
    ub = _gelu(z[:, 2 * D_SSM:2 * D_SSM + D_GMLP])
    vbn = _head_norm_gelu(z[:, 2 * D_SSM + D_GMLP:], gn_ref[...]).astype(_BF16)
    tl = x.shape[0]
    rows = []
    for c in range(tl // CHUNK):
        heads = []
        for h in range(N_GMLP_HEADS):
            vh = vbn[c * CHUNK:(c + 1) * CHUNK, h * GMLP_HEAD:(h + 1) * GMLP_HEAD]
            heads.append(_dot(ws_ref[h], vh) + bs_ref[:, h:h + 1])
        rows.append(jnp.concatenate(heads, axis=-1))
    s = jnp.concatenate(rows, axis=0)
    mixb_ref[0] = _rms(ub * s, gog_ref[...]).astype(_BF16)


def _front_prompt(x, g1, win, gn, ws_tril_bf, bs_t, gog):
    n, l, d = x.shape
    tl = FRONT_TL
    per_seq = l // tl
    n_tiles = n * per_seq
    cur = lambda i: jnp.minimum(i, n_tiles - 1)
    prev = lambda i: jnp.maximum(i - 1, 0)
    const = lambda *shape: pl.BlockSpec(shape, lambda i: (0,) * len(shape))
    seq = lambda w, which: pl.BlockSpec((1, tl, w), lambda i: (which(i) // per_seq, which(i) % per_seq, 0))
    return pl.pallas_call(
        _front_prompt_kernel,
        grid=(n_tiles + 1,),
        in_specs=[seq(d, cur), const(1, d), const(d, D_IN), const(1, D_GMLP),
                  const(N_GMLP_HEADS, CHUNK, CHUNK), const(CHUNK, N_GMLP_HEADS), const(1, D_GMLP)],
        out_specs=[seq(D_SSM, prev), seq(D_SSM, prev), seq(D_GMLP, prev)],
        out_shape=[jax.ShapeDtypeStruct((n, l, D_SSM), _F32),
                   jax.ShapeDtypeStruct((n, l, D_SSM), _F32),
                   jax.ShapeDtypeStruct((n, l, D_GMLP), _BF16)],
        scratch_shapes=[pltpu.VMEM((d, D_IN), _BF16), pltpu.VMEM((tl, D_IN), _F32)],
        compiler_params=pltpu.CompilerParams(
            dimension_semantics=("arbitrary",), vmem_limit_bytes=VMEM_LIMIT),
        name="front_prompt",
    )(x, g1, win, gn, ws_tril_bf, bs_t, gog)


def _ssm_prompt_kernel(xa_ref, sg_ref, v_ref, r_ref, coef_ref, dsk_ref, gos_ref,
                       mixa_ref, hfin_ref, s_ref, st_ref):
    lc = xa_ref.shape[1]
    nblk = lc // SSM_BLK
    rows = nblk * SUBLANES

    @pl.when(pl.program_id(0) == 0)
    def _():
        st_ref[...] = jnp.zeros_like(st_ref)

    def by_position(ref):
        t = pltpu.einshape("btc->tbc", ref[...]).reshape(nblk, SSM_BLK, SUBLANES, D_SSM)
        return [t[:, i].reshape(rows, D_SSM) for i in range(SSM_BLK)]

    xs = by_position(xa_ref)
    xs_bf = [x.astype(_BF16) for x in xs]
    xk = [jnp.concatenate([x[:, k * LANES:(k + 1) * LANES] for x in xs_bf], axis=-1)
          for k in range(N_LANE_TILES)]
    for k in range(N_LANE_TILES):
        s_ref[:, 2 * TILE_STATE * k:2 * TILE_STATE * (k + 1)] = _dot(xk[k], v_ref[k])

    for kk in range(0, N_LANE_TILES, 2):
        tiles = (kk, kk + 1)
        cols = [(2 * TILE_STATE * k, 2 * TILE_STATE * k + TILE_STATE) for k in tiles]
        lbs = [tuple(jnp.broadcast_to(coef_ref[row:row + 1, k * TILE_STATE:(k + 1) * TILE_STATE],
                                      (SUBLANES, TILE_STATE)) for row in (COEF_LBLK_RE, COEF_LBLK_IM))
               for k in tiles]

        def body(j, carry, cols=cols, lbs=lbs):
            r0 = pl.multiple_of(j * SUBLANES, SUBLANES)
            out = []
            for q, ((c_re, c_im), (lr, li)) in enumerate(zip(cols, lbs)):
                hr, hi = carry[2 * q], carry[2 * q + 1]
                sr = s_ref[pl.ds(r0, SUBLANES), c_re:c_re + TILE_STATE]
                si = s_ref[pl.ds(r0, SUBLANES), c_im:c_im + TILE_STATE]
                s_ref[pl.ds(r0, SUBLANES), c_re:c_re + TILE_STATE] = hr
                s_ref[pl.ds(r0, SUBLANES), c_im:c_im + TILE_STATE] = hi
                out += [lr * hr - li * hi + sr, lr * hi + li * hr + si]
            return tuple(out)

        init = tuple(st_ref[:, c:c + TILE_STATE] for c_pair in cols for c in c_pair)
        fin = lax.fori_loop(0, nblk, body, init, unroll=2)
        for q, (c_re, c_im) in enumerate(cols):
            st_ref[:, c_re:c_re + TILE_STATE] = fin[2 * q]
            st_ref[:, c_im:c_im + TILE_STATE] = fin[2 * q + 1]

    yk = []
    for k in range(N_LANE_TILES):
        h_in = s_ref[:, 2 * TILE_STATE * k:2 * TILE_STATE * (k + 1)].astype(_BF16)
        yk.append(_dot(jnp.concatenate([h_in, xk[k]], axis=-1), r_ref[k]))
    sgs = by_position(sg_ref)
    outs = []
    for i in range(SSM_BLK):
        y = jnp.concatenate([y_k[:, i * LANES:(i + 1) * LANES] for y_k in yk], axis=-1) + dsk_ref[...] * xs[i]
        outs.append(_rms(_gelu(y) * sgs[i], gos_ref[...]).reshape(nblk, SUBLANES, D_SSM))
    mixa = jnp.stack(outs, axis=1).reshape(lc, SUBLANES, D_SSM)
    mixa_ref[...] = pltpu.einshape("tbc->btc", mixa).astype(_BF16)
    hfin_ref[...] = st_ref[...]


def _ssm_prompt(xa, sg, v, r, coef, dsk, gos):
    n, l, _ = xa.shape
    lc = SSM_LC
    const = lambda *shape: pl.BlockSpec(shape, lambda i: (0,) * len(shape))
    seq_spec = pl.BlockSpec((n, lc, D_SSM), lambda i: (0, i, 0))
    return pl.pallas_call(
        _ssm_prompt_kernel,
        grid=(l // lc,),
        in_specs=[seq_spec, seq_spec, const(*v.shape), const(*r.shape),
                  const(*coef.shape), const(1, D_SSM), const(1, D_SSM)],
        out_specs=[seq_spec, const(n, 2 * STATE_COLS)],
        out_shape=[jax.ShapeDtypeStruct((n, l, D_SSM), _BF16),
                   jax.ShapeDtypeStruct((n, 2 * STATE_COLS), _F32)],
        scratch_shapes=[pltpu.VMEM((lc // SSM_BLK * n, 2 * STATE_COLS), _F32),
                        pltpu.VMEM((n, 2 * STATE_COLS), _F32)],
        compiler_params=pltpu.CompilerParams(
            dimension_semantics=("arbitrary",), vmem_limit_bytes=VMEM_LIMIT),
        name="ssm_prompt",
    )(xa, sg, v, r, coef, dsk, gos)


def _front_sample_kernel(x_ref, g1_ref, win_ref, gn_ref, w00_ref, b0_ref, gog_ref,
                         wb_ref, wc_ref, coef_ref, dsk_ref, gos_ref, h0r_ref, h0i_ref,
                         mix_ref, hr_ref, hi_ref, vrow_ref):
    x = x_ref[...]
    hn = _rms(x, g1_ref[...])
    z = _dot_f32(hn, win_ref[...])
    xa = z[:, :D_SSM]
    ys = []
    for k in range(N_LANE_TILES):
        bu = _dot_f32(xa[:, k * LANES:(k + 1) * LANES], wb_ref[k])
        sl = slice(k * TILE_STATE, (k + 1) * TILE_STATE)
        lr, li = coef_ref[COEF_LB_RE:COEF_LB_RE + 1, sl], coef_ref[COEF_LB_IM:COEF_LB_IM + 1, sl]
        h0r, h0i = h0r_ref[:, sl], h0i_ref[:, sl]
        nr = lr * h0r - li * h0i + bu[:, :TILE_STATE]
        ni = lr * h0i + li * h0r + bu[:, TILE_STATE:]
        hr_ref[:, sl] = nr
        hi_ref[:, sl] = ni
        ys.append(_dot_f32(jnp.concatenate([nr, ni], axis=-1), wc_ref[k]))
    y = jnp.concatenate(ys, axis=-1) + dsk_ref[...] * xa
    ya = _gelu(y) * jax.nn.sigmoid(z[:, D_SSM:2 * D_SSM])
    mix_ref[:, :D_SSM] = _rms(ya, gos_ref[...])
    ub = _gelu(z[:, 2 * D_SSM:2 * D_SSM + D_GMLP])
    vbn = _head_norm_gelu(z[:, 2 * D_SSM + D_GMLP:], gn_ref[...])
    vrow_ref[...] = vbn
    s = w00_ref[...] * vbn + b0_ref[...]
    mix_ref[:, D_SSM:] = _rms(ub * s, gog_ref[...])


def _front_sample(x, g1, win, gn, w00, b0, gog, wb, wc, coef, dsk, gos, h0r, h0i):
    n = x.shape[0]
    vmem = pl.BlockSpec(memory_space=pltpu.VMEM)
    return pl.pallas_call(
        _front_sample_kernel,
        in_specs=[vmem] * 14,
        out_specs=[vmem] * 4,
        out_shape=[jax.ShapeDtypeStruct((n, D_MODEL), _F32),
                   jax.ShapeDtypeStruct((n, STATE_COLS), _F32),
                   jax.ShapeDtypeStruct((n, STATE_COLS), _F32),
                   jax.ShapeDtypeStruct((n, D_GMLP), _F32)],
        compiler_params=pltpu.CompilerParams(vmem_limit_bytes=VMEM_LIMIT),
        name="front_sample",
    )(x, g1, win, gn, w00, b0, gog, wb, wc, coef, dsk, gos, h0r, h0i)


def _route(logits, base):
    tm = logits.shape[0]
    lt = logits.T
    ex = lt[:N_EXPERTS, :]
    gr = lt[N_EXPERTS:N_EXPERTS + SUBLANES, :]
    row_e = lax.broadcasted_iota(jnp.int32, ex.shape, 0).astype(_F32)
    row_g = lax.broadcasted_iota(jnp.int32, gr.shape, 0).astype(_F32)
    neg = jnp.float32(-jnp.inf)
    big = jnp.float32(LANES)
    is_g = row_g < N_EXPERT_GROUPS
    gl = jnp.where(is_g, gr, neg)
    gmax = jnp.max(gl, axis=0, keepdims=True)
    gi = jnp.min(jnp.where(gl == gmax, row_g, big), axis=0, keepdims=True)
    p_top = 1.0 / jnp.sum(jnp.where(is_g, jnp.exp(gl - gmax), 0.0), axis=0, keepdims=True)
    lo = gi * EXPERTS_PER_GROUP
    in_grp = (row_e >= lo) & (row_e < lo + EXPERTS_PER_GROUP)
    m1 = jnp.max(jnp.where(in_grp, ex, neg), axis=0, keepdims=True)
    i1 = jnp.min(jnp.where(in_grp & (ex == m1), row_e, big), axis=0, keepdims=True)
    rest = in_grp & (row_e != i1)
    m2 = jnp.max(jnp.where(rest, ex, neg), axis=0, keepdims=True)
    i2 = jnp.min(jnp.where(rest & (ex == m2), row_e, big), axis=0, keepdims=True)
    e2 = jnp.exp(m2 - m1)
    w1 = p_top / (1.0 + e2)
    w2 = p_top * e2 / (1.0 + e2)
    sel1 = row_e == i1
    sel2 = row_e == i2
    hits = jnp.where(sel1 | sel2, 1.0, 0.0)
    src = lax.broadcasted_iota(jnp.int32, (tm, tm), 0)
    dst = lax.broadcasted_iota(jnp.int32, (tm, tm), 1)
    before = _dot(hits.astype(_BF16), jnp.where(src < dst, 1.0, 0.0).astype(_BF16)) + base
    rank1 = jnp.sum(jnp.where(sel1, before, 0.0), axis=0, keepdims=True)
    rank2 = jnp.sum(jnp.where(sel2, before, 0.0), axis=0, keepdims=True)
    fields = {R_E1: i1, R_E2: i2, R_W1: w1, R_W2: w2, R_RANK1: rank1, R_RANK2: rank2,
              R_CODE1: i1 * CODE_SHIFT + rank1, R_CODE2: i2 * CODE_SHIFT + rank2}
    row8 = lax.broadcasted_iota(jnp.int32, (SUBLANES, tm), 0)
    route_t = jnp.zeros((SUBLANES, tm), _F32)
    for r, val in fields.items():
        route_t = jnp.where(row8 == r, val, route_t)
    route = jnp.concatenate([route_t, jnp.zeros((LANES - SUBLANES, tm), _F32)], axis=0).T
    return route_t, route, base + jnp.sum(hits, axis=1, keepdims=True)


def _mixer_out_prompt_kernel(x_ref, mixa_ref, mixb_ref, wo_ref, g2_ref, wr_ref, br_ref,
                             x1_ref, xn_ref, route_ref, route_t_ref, cnt_ref, base_ref, logits_ref, wo_bf):
    i = pl.program_id(0)

    @pl.when(i == 0)
    def _():
        base_ref[...] = jnp.zeros_like(base_ref)
        logits_ref[...] = jnp.zeros_like(logits_ref)
        wo_bf[...] = wo_ref[...].astype(_BF16)

    prev_logits = logits_ref[...]
    x1 = x_ref[0] + _dot(mixa_ref[0], wo_bf[:D_SSM, :]) + _dot(mixb_ref[0], wo_bf[D_SSM:, :])
    xn = _rms(x1, g2_ref[...])
    x1_ref[...] = _pack_bf16_pair(x1)
    xn_ref[...] = _pack_bf16_pair(xn)
    logits_ref[...] = _dot(xn.astype(_BF16), wr_ref[...]) + br_ref[...]
    route_t, route, base = _route(prev_logits, base_ref[...])
    route_ref[...] = route
    route_t_ref[...] = route_t
    base = jnp.where(i >= 1, base, base_ref[...])
    base_ref[...] = base
    cnt_ref[...] = base


def _mixer_out_sample_kernel(x_ref, mix_ref, wo_ref, g2_ref, wr_ref, br_ref, cnt_in_ref,
                             x1_in, xn_in, route_in, route_t_in,
                             x1_ref, xn_ref, route_ref, route_t_ref, cnt_ref):
    del x1_in, xn_in, route_in, route_t_in
    x1 = (x_ref[...] + _dot_f32(mix_ref[:, :D_SSM], wo_ref[:D_SSM, :])
          + _dot_f32(mix_ref[:, D_SSM:], wo_ref[D_SSM:, :]))
    xn = _rms(x1, g2_ref[...])
    logits = _dot_f32(xn, wr_ref[...]) + br_ref[...]
    route_t, route, base = _route(logits, cnt_in_ref[...])
    x1_ref[...] = _pack_bf16_pair(x1)
    xn_ref[...] = _pack_bf16_pair(xn)
    route_ref[...] = route
    route_t_ref[...] = route_t
    cnt_ref[...] = base


def _mixer_out(x_p, mixa, mixb, x_s, mix_s, wo, g2, wr, br):
    n, l, d = x_p.shape
    ns = x_s.shape[0]
    t_all = n * l + ns
    tm = TOK_TM
    per_seq = l // tm
    n_tiles = n * per_seq
    cur = lambda i: jnp.minimum(i, n_tiles - 1)
    prev = lambda i: jnp.maximum(i - 1, 0)
    const = lambda *shape: pl.BlockSpec(shape, lambda i: (0,) * len(shape))
    seq = lambda w: pl.BlockSpec((1, tm, w), lambda i: (cur(i) // per_seq, cur(i) % per_seq, 0))
    tok = lambda w, which: pl.BlockSpec((tm, w), lambda i: (which(i), 0))
    tok_shapes = [jax.ShapeDtypeStruct((t_all, d // 2), _U32),
                  jax.ShapeDtypeStruct((t_all, d // 2), _U32),
                  jax.ShapeDtypeStruct((t_all, LANES), _F32),
                  jax.ShapeDtypeStruct((SUBLANES, t_all), _F32)]
    cnt_shape = jax.ShapeDtypeStruct((N_EXPERTS, 1), _F32)
    x1, xn, route, route_t, cnt = pl.pallas_call(
        _mixer_out_prompt_kernel,
        grid=(n_tiles + 1,),
        in_specs=[seq(d), seq(D_SSM), seq(D_GMLP),
                  const(d, d), const(1, d), const(d, LANES), const(1, LANES)],
        out_specs=[tok(d // 2, cur), tok(d // 2, cur), tok(LANES, prev),
                   pl.BlockSpec((SUBLANES, tm), lambda i: (0, prev(i))), const(N_EXPERTS, 1)],
        out_shape=tok_shapes + [cnt_shape],
        scratch_shapes=[pltpu.VMEM((N_EXPERTS, 1), _F32), pltpu.VMEM((tm, LANES), _F32),
                        pltpu.VMEM((d, d), _BF16)],
        compiler_params=pltpu.CompilerParams(
            dimension_semantics=("arbitrary",), vmem_limit_bytes=VMEM_LIMIT),
        name="mixer_out_prompt",
    )(x_p, mixa, mixb, wo, g2, wr.astype(_BF16), br)
    tail = (n * l) // ns
    c1 = lambda *shape: pl.BlockSpec(shape, lambda i: (0,) * len(shape))
    anyspec = pl.BlockSpec(memory_space=pl.ANY)
    tail_spec = lambda w: pl.BlockSpec((ns, w), lambda i: (tail, 0))
    return pl.pallas_call(
        _mixer_out_sample_kernel,
        grid=(1,),
        in_specs=[c1(ns, d), c1(ns, d), c1(d, d), c1(1, d), c1(d, LANES), c1(1, LANES), c1(N_EXPERTS, 1),
                  anyspec, anyspec, anyspec, anyspec],
        out_specs=[tail_spec(d // 2), tail_spec(d // 2), tail_spec(LANES),
                   pl.BlockSpec((SUBLANES, ns), lambda i: (0, tail)), c1(N_EXPERTS, 1)],
        out_shape=tok_shapes + [cnt_shape],
        input_output_aliases={7: 0, 8: 1, 9: 2, 10: 3},
        compiler_params=pltpu.CompilerParams(
            dimension_semantics=("arbitrary",), vmem_limit_bytes=VMEM_LIMIT),
        name="mixer_out_sample",
    )(x_s, mix_s, wo, g2, wr, br, cnt, x1, xn, route, route_t)


def _sc_stream(n_chunks, gather, write):
    gather(0).start()
    for j in range(n_chunks):
        if j + 1 < n_chunks:
            if j >= 1:
                write(j - 1).wait()
            gather(j + 1).start()
        gather(j).wait()
        write(j).start()
    if n_chunks >= 2:
        write(n_chunks - 2).wait()
    write(n_chunks - 1).wait()


def _sc_mesh():
    return plsc.VectorSubcoreMesh(core_axis_name="c", subcore_axis_name="s",
                                  num_cores=SC_CORES, num_subcores=SC_SUBCORES)


def _sc_buffers(chunk, w, dtype):
    return [pltpu.VMEM((chunk, w), dtype), pltpu.VMEM((chunk, w), dtype)] + [pltpu.SemaphoreType.DMA] * 4


def _sc_combine(table, idx, n_out, chunk):
    w = table.shape[1]
    rows_w = n_out // SC_WORKERS
    n_chunks = rows_w // chunk
    assert rows_w * SC_WORKERS == n_out and n_chunks * chunk == rows_w and rows_w % SUBLANES == 0

    def body(table_hbm, idx_hbm, out_hbm, idx_v, buf0, buf1, g0, g1, w0, w1):
        wid = lax.axis_index("s") * SC_CORES + lax.axis_index("c")
        base = pl.multiple_of(wid * rows_w, SUBLANES)
        pltpu.sync_copy(idx_hbm.at[pl.ds(base, rows_w)], idx_v)
        bufs, gsems, wsems = (buf0, buf1), (g0, g1), (w0, w1)

        def gather(j):
            return pltpu.make_async_copy(table_hbm.at[idx_v.at[pl.ds(j * chunk, chunk)]], bufs[j % 2], gsems[j % 2])

        def write(j):
            return pltpu.make_async_copy(bufs[j % 2], out_hbm.at[pl.ds(base + j * chunk, chunk)], wsems[j % 2])

        _sc_stream(n_chunks, gather, write)

    return pl.kernel(
        body,
        out_type=jax.ShapeDtypeStruct((n_out, w), table.dtype),
        mesh=_sc_mesh(),
        scratch_types=[pltpu.VMEM((rows_w,), jnp.int32)] + _sc_buffers(chunk, w, table.dtype),
        compiler_params=pltpu.CompilerParams(use_tc_tiling_on_sc=True),
        name="sc_combine",
    )(table, idx)


def _sc_dispatch(table, codes, start_row, n_out, chunk):
    t_all, w = table.shape
    n_pad = codes.shape[0]
    n_ent = 2 * t_all
    ent_w = n_pad // SC_WORKERS
    n_chunks = ent_w // chunk
    per_chunk = chunk // SC_LANES
    trash = n_out - (n_pad - n_ent)
    assert ent_w * SC_WORKERS == n_pad and n_chunks * chunk == ent_w
    assert per_chunk * SC_LANES == chunk and chunk <= LANES and n_pad - n_ent <= t_all

    def body(table_hbm, code_hbm, start_hbm, out_hbm, dest_hbm,
             code_v, dest_v, tok_v, dst_v, start_v, buf0, buf1, g0, g1, w0, w1):
        wid = lax.axis_index("s") * SC_CORES + lax.axis_index("c")
        ebase = pl.multiple_of(wid * ent_w, SUBLANES)
        pltpu.sync_copy(code_hbm.at[pl.ds(ebase, ent_w)], code_v)
        pltpu.sync_copy(start_hbm, start_v)
        lane = lax.iota(jnp.int32, SC_LANES)
        for j in range(n_chunks):
            for c in range(per_chunk):
                off = j * chunk + c * SC_LANES
                ent = ebase + off + lane
                code = code_v[pl.ds(off, SC_LANES)]
                d = plsc.load_gather(start_v, [code >> CODE_BITS]) + (code & ((1 << CODE_BITS) - 1))
                d = jnp.where(ent >= n_ent, trash + (ent - n_ent), d)
                tok = jnp.where(ent >= t_all, ent - t_all, ent)
                tok = jnp.where(tok >= t_all, tok - t_all, tok)
                dest_v[pl.ds(off, SC_LANES)] = d
                dst_v[j, pl.ds(c * SC_LANES, SC_LANES)] = d
                tok_v[j, pl.ds(c * SC_LANES, SC_LANES)] = tok
        pltpu.sync_copy(dest_v, dest_hbm.at[pl.ds(ebase, ent_w)])
        bufs, gsems, wsems = (buf0, buf1), (g0, g1), (w0, w1)

        def gather(j):
            return pltpu.make_async_copy(table_hbm.at[tok_v.at[j]], bufs[j % 2], gsems[j % 2])

        def scatter(j):
            return pltpu.make_async_copy(bufs[j % 2], out_hbm.at[dst_v.at[j]], wsems[j % 2])

        _sc_stream(n_chunks, gather, scatter)

    return pl.kernel(
        body,
        out_type=(jax.ShapeDtypeStruct((n_out, w), table.dtype), jax.ShapeDtypeStruct((n_pad,), jnp.int32)),
        mesh=_sc_mesh(),
        scratch_types=([pltpu.VMEM((ent_w,), jnp.int32), pltpu.VMEM((ent_w,), jnp.int32),
                        pltpu.VMEM((n_chunks, chunk), jnp.int32), pltpu.VMEM((n_chunks, chunk), jnp.int32),
                        pltpu.VMEM((LANES,), jnp.int32)] + _sc_buffers(chunk, w, table.dtype)),
        compiler_params=pltpu.CompilerParams(use_tc_tiling_on_sc=True, needs_layout_passes=False),
        name="sc_dispatch",
    )(table, codes, start_row)


def _experts_kernel(piece_start_ref, piece_row_ref, piece_cls_ref, *refs):
    n_w = 3 * W_SPLIT
    wg_refs, wu_refs, wd_refs = refs[:W_SPLIT], refs[W_SPLIT:2 * W_SPLIT], refs[2 * W_SPLIT:n_w]
    xs_hbm, ys_hbm, wg_bf, wu_bf, wd_bf, xbuf, ybuf, xsem, ysem = refs[n_w:]
    e = pl.program_id(0)
    g0 = piece_start_ref[e]
    n_here = piece_start_ref[e + 1] - g0
    n_total = piece_start_ref[N_EXPERTS]

    def per_class(g, fn):
        cls = piece_cls_ref[g]
        row = pl.multiple_of(piece_row_ref[g], EXP_UNIT)
        for c in range(1, EXP_CLASSES + 1):
            pl.when(cls == c)(lambda c=c: fn(c * EXP_UNIT, row))

    def x_copy(slot, rows, row):
        return pltpu.make_async_copy(xs_hbm.at[pl.ds(row, rows)], xbuf.at[slot, pl.ds(0, rows)], xsem.at[slot])

    def y_copy(slot, rows, row):
        return pltpu.make_async_copy(ybuf.at[slot, pl.ds(0, rows)], ys_hbm.at[pl.ds(row, rows)], ysem.at[slot])

    @pl.when((e == 0) & (n_total > 0))
    def _():
        per_class(0, lambda rows, row: x_copy(0, rows, row).start())

    for dst, chunks in ((wg_bf, wg_refs), (wu_bf, wu_refs), (wd_bf, wd_refs)):
        rows = dst.shape[0] // W_SPLIT
        for q, src in enumerate(chunks):
            dst[q * rows:(q + 1) * rows, :] = src[0, 0].astype(_BF16)

    def piece(j, carry):
        g = g0 + j
        slot = lax.rem(g, 2)
        per_class(g, lambda rows, row: x_copy(slot, rows, row).wait())

        @pl.when(g + 1 < n_total)
        def _():
            per_class(g + 1, lambda rows, row: x_copy(1 - slot, rows, row).start())

        @pl.when(g >= 2)
        def _():
            per_class(g - 2, lambda rows, row: y_copy(slot, rows, row).wait())

        def compute(rows, row):
            x = _unpack_bf16_pair(xbuf[slot, pl.ds(0, rows)]).astype(_BF16)
            a = _dot(x, wg_bf[...])
            u = _dot(x, wu_bf[...])
            h = (a * jax.nn.sigmoid(a) * u).astype(_BF16)
            ybuf[slot, pl.ds(0, rows)] = _pack_bf16_pair(_dot(h, wd_bf[...]))
            y_copy(slot, rows, row).start()

        per_class(g, compute)
        return carry

    lax.fori_loop(0, n_here, piece, 0)

    @pl.when(e == N_EXPERTS - 1)
    def _():
        @pl.when(n_total >= 2)
        def _():
            per_class(n_total - 2, lambda rows, row: y_copy(lax.rem(n_total, 2), rows, row).wait())

        @pl.when(n_total >= 1)
        def _():
            per_class(n_total - 1, lambda rows, row: y_copy(lax.rem(n_total - 1, 2), rows, row).wait())


def _experts(piece_start, piece_row, piece_cls, n_rows, xs, w_gate, w_up, w_down):
    dh = xs.shape[1]
    d = 2 * dh
    tm = EXP_UNIT * EXP_CLASSES
    anyspec = pl.BlockSpec(memory_space=pl.ANY)

    def chunk_specs(rows, cols):
        return [pl.BlockSpec((1, 1, rows // W_SPLIT, cols), lambda e, ps, pr, pc, q=q: (e, q, 0, 0))
                for q in range(W_SPLIT)]

    split = lambda w: w.reshape(w.shape[0], W_SPLIT, w.shape[1] // W_SPLIT, w.shape[2])
    grid_spec = pltpu.PrefetchScalarGridSpec(
        num_scalar_prefetch=3,
        grid=(N_EXPERTS,),
        in_specs=(chunk_specs(d, D_EXPERT) + chunk_specs(d, D_EXPERT) + chunk_specs(D_EXPERT, d) + [anyspec]),
        out_specs=anyspec,
        scratch_shapes=[pltpu.VMEM((d, D_EXPERT), _BF16), pltpu.VMEM((d, D_EXPERT), _BF16),
                        pltpu.VMEM((D_EXPERT, d), _BF16),
                        pltpu.VMEM((2, tm, dh), _U32), pltpu.VMEM((2, tm, dh), _U32),
                        pltpu.SemaphoreType.DMA((2,)), pltpu.SemaphoreType.DMA((2,))],
    )
    return pl.pallas_call(
        _experts_kernel,
        grid_spec=grid_spec,
        out_shape=jax.ShapeDtypeStruct((n_rows, dh), _U32),
        compiler_params=pltpu.CompilerParams(
            dimension_semantics=("arbitrary",), vmem_limit_bytes=VMEM_LIMIT),
        name="experts",
    )(piece_start, piece_row, piece_cls, *([split(w_gate)] * W_SPLIT), *([split(w_up)] * W_SPLIT),
      *([split(w_down)] * W_SPLIT), xs)


def _final_kernel(x1_ref, ya_ref, yb_ref, route_ref, gf_ref, y_ref):
    route = route_ref[...]
    x2 = (_unpack_bf16_pair(x1_ref[...]) + route[:, R_W1:R_W1 + 1] * _unpack_bf16_pair(ya_ref[...])
          + route[:, R_W2:R_W2 + 1] * _unpack_bf16_pair(yb_ref[...]))
    y_ref[...] = _rms(x2, gf_ref[...])


def _final(x1, yab, route, gf, n_prompt, n_sample):
    d = 2 * x1.shape[1]

    def call(tm, first_block, n_rows, name):
        tok = lambda w: pl.BlockSpec((tm, w), lambda i: (first_block + i, 0))
        sel = lambda k: pl.BlockSpec((None, tm, d // 2), lambda i: (k, first_block + i, 0))
        return pl.pallas_call(
            _final_kernel,
            grid=(n_rows // tm,),
            in_specs=[tok(d // 2), sel(0), sel(1), tok(LANES), pl.BlockSpec((1, d), lambda i: (0, 0))],
            out_specs=pl.BlockSpec((tm, d), lambda i: (i, 0)),
            out_shape=jax.ShapeDtypeStruct((n_rows, d), _F32),
            compiler_params=pltpu.CompilerParams(
                dimension_semantics=("arbitrary",), vmem_limit_bytes=VMEM_LIMIT),
            name=name,
        )(x1, yab, yab, route, gf)

    return (call(FINAL_TM, 0, n_prompt, "final_prompt"),
            call(n_sample, n_prompt // n_sample, n_sample, "final_sample"))


def _powers(lam_re, lam_im, dt):
    out = []
    for m in range(SSM_BLK + 1):
        mag = jnp.exp(m * lam_re * dt)
        ang = m * lam_im * dt
        out.append((mag * jnp.cos(ang), mag * jnp.sin(ang)))
    return out


def _spread(x, copies):
    w = x.shape[1]
    src = lax.broadcasted_iota(jnp.int32, (w, w * copies), 0)
    dst = lax.broadcasted_iota(jnp.int32, (w, w * copies), 1)
    return _dot_f32(x, jnp.where(dst % w == src, 1.0, 0.0))


def _ssm_prep_kernel(lam_ref, b_re, b_im, c_re, c_im, v_ref, r_ref, wb_ref, wc_ref, coef_ref):
    n_p, n_h = SSM_STATE, SSM_GROUP
    lr, li, dt = lam_ref[0:1, :], lam_ref[1:2, :], lam_ref[2:3, :]
    pw = _powers(lr, li, dt)
    den = lr * lr + li * li
    nr, ni = pw[1][0] - 1.0, pw[1][1]
    k_re = (nr * lr + ni * li) / den
    k_im = (ni * lr - nr * li) / den
    coef_ref[...] = jnp.concatenate(
        [pw[1][0], pw[1][1], pw[SSM_BLK][0], pw[SSM_BLK][1], jnp.zeros((SUBLANES - 4, TILE_STATE), _F32)], axis=0)

    on_diag_b = (lax.broadcasted_iota(jnp.int32, (TILE_STATE, LANES), 0) // n_p
                 == lax.broadcasted_iota(jnp.int32, (TILE_STATE, LANES), 1) // n_h)
    rows_gp = lambda ref: ref[...].reshape(TILE_STATE, n_h)
    bt_re = jnp.where(on_diag_b, _spread(rows_gp(b_re), SUBLANES), 0.0).T
    bt_im = jnp.where(on_diag_b, _spread(rows_gp(b_im), SUBLANES), 0.0).T
    bb_re = k_re * bt_re - k_im * bt_im
    bb_im = k_re * bt_im + k_im * bt_re
    wb_ref[0] = jnp.concatenate([bb_re, bb_im], axis=1)
    v_rows = []
    for s in range(SSM_BLK):
        pr, pi = pw[SSM_BLK - 1 - s]
        v_rows.append(jnp.concatenate([pr * bb_re - pi * bb_im, pr * bb_im + pi * bb_re], axis=1))
    v_ref[0] = jnp.concatenate(v_rows, axis=0).astype(v_ref.dtype)

    on_diag_c = (lax.broadcasted_iota(jnp.int32, (LANES, TILE_STATE), 0) // n_h
                 == lax.broadcasted_iota(jnp.int32, (LANES, TILE_STATE), 1) // n_p)
    rows_gh = lambda ref: ref[...].reshape(LANES, n_p)
    ct_re = jnp.where(on_diag_c, _spread(rows_gh(c_re), SUBLANES), 0.0)
    ct_im = jnp.where(on_diag_c, _spread(rows_gh(c_im), SUBLANES), 0.0)
    cl = [(ct_re * pr - ct_im * pi, ct_re * pi + ct_im * pr) for pr, pi in pw]
    wc_ref[0] = jnp.concatenate([cl[0][0], -cl[0][1]], axis=1).T
    nt = lambda a, b: lax.dot_general(a, b, (((1,), (1,)), ((), ())), precision=lax.Precision.HIGHEST,
                                      preferred_element_type=_F32)
    direct = [nt(cl[m][0], bb_re) - nt(cl[m][1], bb_im) for m in range(SSM_BLK)]
    zero = jnp.zeros((LANES, LANES), _F32)
    rt = jnp.concatenate(
        [jnp.concatenate([cl[i + 1][0], -cl[i + 1][1]]
                         + [direct[i - s] if s <= i else zero for s in range(SSM_BLK)], axis=1)
         for i in range(SSM_BLK)], axis=0)
    r_ref[0] = rt.T.astype(r_ref.dtype)


def _ssm_params(lam_re, lam_im, log_dt, b_re, b_im, c_re, c_im, d_skip):
    n_g, n_p, n_h = N_SSM_GROUPS, SSM_STATE, SSM_GROUP
    dt = jnp.repeat(jnp.exp(log_dt), n_p)
    lam = jnp.zeros((SUBLANES, STATE_COLS), _F32).at[0].set(lam_re.reshape(-1)).at[1].set(
        lam_im.reshape(-1)).at[2].set(dt)
    groups = lambda r, c: pl.BlockSpec((SUBLANES, r, c), lambda k: (k, 0, 0))
    out3 = lambda rows, w: pl.BlockSpec((1, rows, w), lambda k: (k, 0, 0))
    cols = pl.BlockSpec((SUBLANES, TILE_STATE), lambda k: (0, k))
    k_blk = SSM_BLK * LANES
    v, r, wb, wc, coef = pl.pallas_call(
        _ssm_prep_kernel,
        grid=(N_LANE_TILES,),
        in_specs=[cols, groups(n_p, n_h), groups(n_p, n_h), groups(n_h, n_p), groups(n_h, n_p)],
        out_specs=[out3(k_blk, 2 * TILE_STATE), out3(2 * TILE_STATE + k_blk, k_blk),
                   out3(LANES, 2 * TILE_STATE), out3(2 * TILE_STATE, LANES), cols],
        out_shape=[jax.ShapeDtypeStruct((N_LANE_TILES, k_blk, 2 * TILE_STATE), _BF16),
                   jax.ShapeDtypeStruct((N_LANE_TILES, 2 * TILE_STATE + k_blk, k_blk), _BF16),
                   jax.ShapeDtypeStruct((N_LANE_TILES, LANES, 2 * TILE_STATE), _F32),
                   jax.ShapeDtypeStruct((N_LANE_TILES, 2 * TILE_STATE, LANES), _F32),
                   jax.ShapeDtypeStruct((SUBLANES, STATE_COLS), _F32)],
        compiler_params=pltpu.CompilerParams(
            dimension_semantics=("arbitrary",), vmem_limit_bytes=VMEM_LIMIT),
        name="ssm_prep",
    )(lam, b_re, b_im, c_re, c_im)
    return wb, wc, v, r, coef, d_skip.reshape(1, D_SSM)


def _dispatch_plan(route_t, cnt):
    t_all = route_t.shape[1]
    codes = route_t[R_CODE1:R_CODE2 + 1].astype(jnp.int32).reshape(-1)
    per_pass = SC_WORKERS * DISPATCH_CHUNK
    codes = jnp.pad(codes, (0, -(2 * t_all) % per_pass))
    counts = cnt[:, 0].astype(jnp.int32)
    zero = jnp.zeros((1,), jnp.int32)
    units = (counts + EXP_UNIT - 1) // EXP_UNIT
    unit_start = jnp.concatenate([zero, jnp.cumsum(units)])
    start_row = jnp.zeros((LANES,), jnp.int32).at[:N_EXPERTS].set(unit_start[:N_EXPERTS] * EXP_UNIT)
    pieces = (units + EXP_CLASSES - 1) // EXP_CLASSES
    piece_start = jnp.concatenate([zero, jnp.cumsum(pieces)])
    tm = EXP_UNIT * EXP_CLASSES
    max_units = (2 * t_all + N_EXPERTS * (EXP_UNIT - 1)) // EXP_UNIT
    max_pieces = (max_units + N_EXPERTS * (EXP_CLASSES - 1)) // EXP_CLASSES
    g = jnp.arange(max_pieces, dtype=jnp.int32)
    owner = ((g[:, None] >= piece_start[None, :-1]) & (g[:, None] < piece_start[None, 1:])).astype(jnp.int32)
    pick = lambda table: jnp.sum(owner * table[None, :], axis=1)
    first_unit = pick(unit_start[:-1]) + (g - pick(piece_start[:-1])) * EXP_CLASSES
    piece_row = first_unit * EXP_UNIT
    piece_cls = jnp.clip(pick(unit_start[1:]) - first_unit, 1, EXP_CLASSES)
    n_rows = (max_units * EXP_UNIT + tm - 1) // tm * tm + tm
    return codes, start_row, n_rows, piece_start, piece_row, piece_cls


def kernel(x_prompt, x_sample, state_ssm_re, state_ssm_im, norm1_g, w_in, lam_re, lam_im, log_dt, ssm_b_re, ssm_b_im, ssm_c_re, ssm_c_im, ssm_d, gmlp_norm_g, gmlp_w_s, gmlp_b_s, out_norm_ssm_g, out_norm_gmlp_g, w_out, norm2_g, w_router_group, b_router_group, w_router_expert, b_router_expert, w_gate, w_up, w_down, final_norm_g):
    n, l, d = x_prompt.shape
    ns = x_sample.shape[0]
    t_all = n * l + ns
    li = 0
    g1 = norm1_g[li].reshape(1, d)
    gn = gmlp_norm_g[li].reshape(1, D_GMLP)
    tril = jnp.tril(jnp.ones((CHUNK, CHUNK), dtype=bool))
    ws_tril = jnp.where(tril[None], gmlp_w_s[li], 0.0)
    bs = gmlp_b_s[li]
    gog = out_norm_gmlp_g[li].reshape(1, D_GMLP)
    gos = out_norm_ssm_g[li].reshape(1, D_SSM)
    wb, wc, v_blk, r_blk, coef, dsk = _ssm_params(
        lam_re[li], lam_im[li], log_dt[li], ssm_b_re[li], ssm_b_im[li], ssm_c_re[li], ssm_c_im[li], ssm_d[li])
    g2 = norm2_g[li].reshape(1, d)
    pad = LANES - N_EXPERTS - N_EXPERT_GROUPS
    wr = jnp.concatenate([w_router_expert[li], w_router_group[li], jnp.zeros((d, pad), _F32)], axis=1)
    br = jnp.concatenate([b_router_expert[li], b_router_group[li], jnp.zeros((pad,), _F32)]).reshape(1, LANES)

    xa, sg, mixb = _front_prompt(x_prompt, g1, w_in[li], gn, ws_tril.astype(_BF16), bs.T, gog)
    mixa, hfin = _ssm_prompt(xa, sg, v_blk, r_blk, coef, dsk, gos)
    w00 = jnp.repeat(ws_tril[:, 0, 0], GMLP_HEAD).reshape(1, D_GMLP)
    b0 = jnp.repeat(bs[:, 0], GMLP_HEAD).reshape(1, D_GMLP)
    mix_s, hr_s, hi_s, vrow = _front_sample(
        x_sample.reshape(ns, d), g1, w_in[li], gn, w00, b0, gog, wb, wc, coef, dsk, gos,
        state_ssm_re[li].reshape(ns, STATE_COLS), state_ssm_im[li].reshape(ns, STATE_COLS))

    x1, xn, route, route_t, cnt = _mixer_out(x_prompt, mixa, mixb, x_sample.reshape(ns, d), mix_s,
                                             w_out[li], g2, wr, br)
    codes, start_row, n_rows, piece_start, piece_row, piece_cls = _dispatch_plan(route_t, cnt)
    xs, dest = _sc_dispatch(xn, codes, start_row, n_rows, DISPATCH_CHUNK)
    ys = _experts(piece_start, piece_row, piece_cls, n_rows, xs, w_gate[li], w_up[li], w_down[li])
    yab = _sc_combine(ys, dest, 2 * t_all, COMBINE_CHUNK).reshape(2, t_all, d // 2)
    y_p, y_s = _final(x1, yab, route, final_norm_g.reshape(1, d), n * l, ns)

    hf = hfin.reshape(n, N_LANE_TILES, 2, 8, SSM_STATE)
    re_p = hf[:, :, 0].reshape(1, n, N_SSM_GROUPS, SSM_STATE)
    im_p = hf[:, :, 1].reshape(1, n, N_SSM_GROUPS, SSM_STATE)
    re_s = hr_s.reshape(1, ns, N_SSM_GROUPS, SSM_STATE)
    im_s = hi_s.reshape(1, ns, N_SSM_GROUPS, SSM_STATE)
    return (y_p.reshape(n, l, d), y_s.reshape(ns, 1, d), re_p, im_p, re_s, im_s,
            vrow.reshape(1, ns, 1, D_GMLP))
```

```python
import math

import jax
import jax.numpy as jnp
from jax import lax
from jax.experimental import pallas as pl
from jax.experimental.pallas import tpu as pltpu
from jax.experimental.pallas import tpu_sc as plsc

D_MODEL = 1024
D_SSM = 512
D_GMLP = 512
SSM_GROUP = 16
N_SSM_GROUPS = 32
SSM_STATE = 64
CHUNK = 128
N_GMLP_HEADS = 4
GMLP_HEAD = 128
N_EXPERT_GROUPS = 4
EXPERTS_PER_GROUP = 8
N_EXPERTS = 32
D_EXPERT = 512
D_IN = 2048
EPS = 1e-6

LANES = 128
SUBLANES = 8
N_LANE_TILES = D_SSM // LANES
STATE_COLS = N_SSM_GROUPS * SSM_STATE
TILE_STATE = STATE_COLS // N_LANE_TILES
VMEM_LIMIT = 56 * 1024 * 1024

SC_CORES = 2
SC_SUBCORES = 16
SC_LANES = 16
SC_WORKERS = SC_CORES * SC_SUBCORES

FRONT_TL = 512
SSM_LC = 256
SSM_BLK = 4
COEF_LB_RE, COEF_LB_IM, COEF_LBLK_RE, COEF_LBLK_IM = 0, 1, 2, 3
TOK_TM = 512
FINAL_TM = 1024
EXP_UNIT = 128
EXP_CLASSES = 12
W_SPLIT = 4
DISPATCH_CHUNK = 80
COMBINE_CHUNK = 24

R_E1, R_E2, R_W1, R_W2, R_RANK1, R_RANK2, R_CODE1, R_CODE2 = 0, 1, 2, 3, 4, 5, 6, 7
CODE_BITS = 16
CODE_SHIFT = float(1 << CODE_BITS)

_INV_SQRT2 = 1.0 / math.sqrt(2.0)
_BF16 = jnp.bfloat16
_F32 = jnp.float32
_U32 = jnp.uint32


def _gelu(x):
    return 0.5 * x * (1.0 + lax.erf(x * _INV_SQRT2))


def _rms(x, g):
    return x * lax.rsqrt(jnp.mean(x * x, axis=-1, keepdims=True) + EPS) * g


def _dot(a, b):
    return jnp.dot(a, b, preferred_element_type=_F32)


def _dot_f32(a, b):
    return jnp.dot(a, b, preferred_element_type=_F32, precision=lax.Precision.HIGHEST)


def _pack_bf16_pair(x):
    w = x.shape[1] // 2
    hi = lax.bitcast_convert_type(x[:, :w].astype(_BF16).astype(_F32), _U32)
    lo = lax.bitcast_convert_type(x[:, w:].astype(_BF16).astype(_F32), _U32)
    return hi | (lo >> 16)


def _unpack_bf16_pair(p):
    hi = lax.bitcast_convert_type(p & jnp.uint32(0xFFFF0000), _F32)
    lo = lax.bitcast_convert_type(p << 16, _F32)
    return jnp.concatenate([hi, lo], axis=-1)


def _head_norm_gelu(vb, gn):
    v = _gelu(vb)
    parts = []
    for h in range(N_GMLP_HEADS):
        vh = v[:, h * GMLP_HEAD:(h + 1) * GMLP_HEAD]
        parts.append(vh * lax.rsqrt(jnp.mean(vh * vh, axis=-1, keepdims=True) + EPS))
    return jnp.concatenate(parts, axis=-1) * gn


def _front_prompt_kernel(x_ref, g1_ref, win_ref, gn_ref, ws_ref, bs_ref, gog_ref,
                         xa_ref, sg_ref, mixb_ref, win_bf, z_ref):
    @pl.when(pl.program_id(0) == 0)
    def _():
        win_bf[...] = win_ref[...].astype(_BF16)
        z_ref[...] = jnp.zeros_like(z_ref)

    z = z_ref[...]
    x = x_ref[0]
    hn = _rms(x, g1_ref[...]).astype(_BF16)
    z_ref[...] = _dot(hn, win_bf[...])
    xa_ref[0] = z[:, :D_SSM]
    sg_ref[0] = jax.nn.sigmoid(z[:, D_SSM:2 * D_SSM])
    ub = _gelu(z[:, 2 * D_SSM:2 * D_SSM + D_GMLP])
    vbn = _head_norm_gelu(z[:, 2 * D_SSM + D_GMLP:], gn_ref[...]).astype(_BF16)
    tl = x.shape[0]
    rows = []
    for c in range(tl // CHUNK):
        heads = []
        for h in range(N_GMLP_HEADS):
            vh = vbn[c * CHUNK:(c + 1) * CHUNK, h * GMLP_HEAD:(h + 1) * GMLP_HEAD]
            heads.append(_dot(ws_ref[h], vh) + bs_ref[:, h:h + 1])
        rows.append(jnp.concatenate(heads, axis=-1))
    s = jnp.concatenate(rows, axis=0)
    mixb_ref[0] = _rms(ub * s, gog_ref[...]).astype(_BF16)


def _front_prompt(x, g1, win, gn, ws_tril_bf, bs_t, gog):
    n, l, d = x.shape
    tl = FRONT_TL
    per_seq = l // tl
    n_tiles = n * per_seq
    cur = lambda i: jnp.minimum(i, n_tiles - 1)
    prev = lambda i: jnp.maximum(i - 1, 0)
    const = lambda *shape: pl.BlockSpec(shape, lambda i: (0,) * len(shape))
    seq = lambda w, which: pl.BlockSpec((1, tl, w), lambda i: (which(i) // per_seq, which(i) % per_seq, 0))
    return pl.pallas_call(
        _front_prompt_kernel,
        grid=(n_tiles + 1,),
        in_specs=[seq(d, cur), const(1, d), const(d, D_IN), const(1, D_GMLP),
                  const(N_GMLP_HEADS, CHUNK, CHUNK), const(CHUNK, N_GMLP_HEADS), const(1, D_GMLP)],
        out_specs=[seq(D_SSM, prev), seq(D_SSM, prev), seq(D_GMLP, prev)],
        out_shape=[jax.ShapeDtypeStruct((n, l, D_SSM), _F32),
                   jax.ShapeDtypeStruct((n, l, D_SSM), _F32),
                   jax.ShapeDtypeStruct((n, l, D_GMLP), _BF16)],
        scratch_shapes=[pltpu.VMEM((d, D_IN), _BF16), pltpu.VMEM((tl, D_IN), _F32)],
        compiler_params=pltpu.CompilerParams(
            dimension_semantics=("arbitrary",), vmem_limit_bytes=VMEM_LIMIT),
        name="front_prompt",
    )(x, g1, win, gn, ws_tril_bf, bs_t, gog)


def _ssm_prompt_kernel(xa_ref, sg_ref, v_ref, r_ref, coef_ref, dsk_ref, gos_ref,
                       mixa_ref, hfin_ref, s_ref, st_ref):
    lc = xa_ref.shape[1]
    nblk = lc // SSM_BLK
    rows = nblk * SUBLANES

    @pl.when(pl.program_id(0) == 0)
    def _():
        st_ref[...] = jnp.zeros_like(st_ref)

    def by_position(ref):
        t = pltpu.einshape("btc->tbc", ref[...]).reshape(nblk, SSM_BLK, SUBLANES, D_SSM)
        return [t[:, i].reshape(rows, D_SSM) for i in range(SSM_BLK)]

    xs = by_position(xa_ref)
    xs_bf = [x.astype(_BF16) for x in xs]
    xk = [jnp.concatenate([x[:, k * LANES:(k + 1) * LANES] for x in xs_bf], axis=-1)
          for k in range(N_LANE_TILES)]
    for k in range(N_LANE_TILES):
        s_ref[:, 2 * TILE_STATE * k:2 * TILE_STATE * (k + 1)] = _dot(xk[k], v_ref[k])

    for kk in range(0, N_LANE_TILES, 2):
        tiles = (kk, kk + 1)
        cols = [(2 * TILE_STATE * k, 2 * TILE_STATE * k + TILE_STATE) for k in tiles]
        lbs = [tuple(jnp.broadcast_to(coef_ref[row:row + 1, k * TILE_STATE:(k + 1) * TILE_STATE],
                                      (SUBLANES, TILE_STATE)) for row in (COEF_LBLK_RE, COEF_LBLK_IM))
               for k in tiles]

        def body(j, carry, cols=cols, lbs=lbs):
            r0 = pl.multiple_of(j * SUBLANES, SUBLANES)
            out = []
            for q, ((c_re, c_im), (lr, li)) in enumerate(zip(cols, lbs)):
                hr, hi = carry[2 * q], carry[2 * q + 1]
                sr = s_ref[pl.ds(r0, SUBLANES), c_re:c_re + TILE_STATE]
                si = s_ref[pl.ds(r0, SUBLANES), c_im:c_im + TILE_STATE]
                s_ref[pl.ds(r0, SUBLANES), c_re:c_re + TILE_STATE] = hr
                s_ref[pl.ds(r0, SUBLANES), c_im:c_im + TILE_STATE] = hi
                out += [lr * hr - li * hi + sr, lr * hi + li * hr + si]
            return tuple(out)

        init = tuple(st_ref[:, c:c + TILE_STATE] for c_pair in cols for c in c_pair)
        fin = lax.fori_loop(0, nblk, body, init, unroll=2)
        for q, (c_re, c_im) in enumerate(cols):
            st_ref[:, c_re:c_re + TILE_STATE] = fin[2 * q]
            st_ref[:, c_im:c_im + TILE_STATE] = fin[2 * q + 1]

    yk = []
    for k in range(N_LANE_TILES):
        h_in = s_ref[:, 2 * TILE_STATE * k:2 * TILE_STATE * (k + 1)].astype(_BF16)
        yk.append(_dot(jnp.concatenate([h_in, xk[k]], axis=-1), r_ref[k]))
    sgs = by_position(sg_ref)
    outs = []
    for i in range(SSM_BLK):
        y = jnp.concatenate([y_k[:, i * LANES:(i + 1) * LANES] for y_k in yk], axis=-1) + dsk_ref[...] * xs[i]
        outs.append(_rms(_gelu(y) * sgs[i], gos_ref[...]).reshape(nblk, SUBLANES, D_SSM))
    mixa = jnp.stack(outs, axis=1).reshape(lc, SUBLANES, D_SSM)
    mixa_ref[...] = pltpu.einshape("tbc->btc", mixa).astype(_BF16)
    hfin_ref[...] = st_ref[...]


def _ssm_prompt(xa, sg, v, r, coef, dsk, gos):
    n, l, _ = xa.shape
    lc = SSM_LC
    const = lambda *shape: pl.BlockSpec(shape, lambda i: (0,) * len(shape))
    seq_spec = pl.BlockSpec((n, lc, D_SSM), lambda i: (0, i, 0))
    return pl.pallas_call(
        _ssm_prompt_kernel,
        grid=(l // lc,),
        in_specs=[seq_spec, seq_spec, const(*v.shape), const(*r.shape),
                  const(*coef.shape), const(1, D_SSM), const(1, D_SSM)],
        out_specs=[seq_spec, const(n, 2 * STATE_COLS)],
        out_shape=[jax.ShapeDtypeStruct((n, l, D_SSM), _BF16),
                   jax.ShapeDtypeStruct((n, 2 * STATE_COLS), _F32)],
        scratch_shapes=[pltpu.VMEM((lc // SSM_BLK * n, 2 * STATE_COLS), _F32),
                        pltpu.VMEM((n, 2 * STATE_COLS), _F32)],
        compiler_params=pltpu.CompilerParams(
            dimension_semantics=("arbitrary",), vmem_limit_bytes=VMEM_LIMIT),
        name="ssm_prompt",
    )(xa, sg, v, r, coef, dsk, gos)


def _front_sample_kernel(x_ref, g1_ref, win_ref, gn_ref, w00_ref, b0_ref, gog_ref,
                         wb_ref, wc_ref, coef_ref, dsk_ref, gos_ref, h0r_ref, h0i_ref,
                         mix_ref, hr_ref, hi_ref, vrow_ref):
    x = x_ref[...]
    hn = _rms(x, g1_ref[...])
    z = _dot_f32(hn, win_ref[...])
    xa = z[:, :D_SSM]
    ys = []
    for k in range(N_LANE_TILES):
        bu = _dot_f32(xa[:, k * LANES:(k + 1) * LANES], wb_ref[k])
        sl = slice(k * TILE_STATE, (k + 1) * TILE_STATE)
        lr, li = coef_ref[COEF_LB_RE:COEF_LB_RE + 1, sl], coef_ref[COEF_LB_IM:COEF_LB_IM + 1, sl]
        h0r, h0i = h0r_ref[:, sl], h0i_ref[:, sl]
        nr = lr * h0r - li * h0i + bu[:, :TILE_STATE]
        ni = lr * h0i + li * h0r + bu[:, TILE_STATE:]
        hr_ref[:, sl] = nr
        hi_ref[:, sl] = ni
        ys.append(_dot_f32(jnp.concatenate([nr, ni], axis=-1), wc_ref[k]))
    y = jnp.concatenate(ys, axis=-1) + dsk_ref[...] * xa
    ya = _gelu(y) * jax.nn.sigmoid(z[:, D_SSM:2 * D_SSM])
    mix_ref[:, :D_SSM] = _rms(ya, gos_ref[...])
    ub = _gelu(z[:, 2 * D_SSM:2 * D_SSM + D_GMLP])
    vbn = _head_norm_gelu(z[:, 2 * D_SSM + D_GMLP:], gn_ref[...])
    vrow_ref[...] = vbn
    s = w00_ref[...] * vbn + b0_ref[...]
    mix_ref[:, D_SSM:] = _rms(ub * s, gog_ref[...])


def _front_sample(x, g1, win, gn, w00, b0, gog, wb, wc, coef, dsk, gos, h0r, h0i):
    n = x.shape[0]
    vmem = pl.BlockSpec(memory_space=pltpu.VMEM)
    return pl.pallas_call(
        _front_sample_kernel,
        in_specs=[vmem] * 14,
        out_specs=[vmem] * 4,
        out_shape=[jax.ShapeDtypeStruct((n, D_MODEL), _F32),
                   jax.ShapeDtypeStruct((n, STATE_COLS), _F32),
                   jax.ShapeDtypeStruct((n, STATE_COLS), _F32),
                   jax.ShapeDtypeStruct((n, D_GMLP), _F32)],
        compiler_params=pltpu.CompilerParams(vmem_limit_bytes=VMEM_LIMIT),
        name="front_sample",
    )(x, g1, win, gn, w00, b0, gog, wb, wc, coef, dsk, gos, h0r, h0i)


def _route(logits, base):
    tm = logits.shape[0]
    lt = logits.T
    ex = lt[:N_EXPERTS, :]
    gr = lt[N_EXPERTS:N_EXPERTS + SUBLANES, :]
    row_e = lax.broadcasted_iota(jnp.int32, ex.shape, 0).astype(_F32)
    row_g = lax.broadcasted_iota(jnp.int32, gr.shape, 0).astype(_F32)
    neg = jnp.float32(-jnp.inf)
    big = jnp.float32(LANES)
    is_g = row_g < N_EXPERT_GROUPS
    gl = jnp.where(is_g, gr, neg)
    gmax = jnp.max(gl, axis=0, keepdims=True)
    gi = jnp.min(jnp.where(gl == gmax, row_g, big), axis=0, keepdims=True)
    p_top = 1.0 / jnp.sum(jnp.where(is_g, jnp.exp(gl - gmax), 0.0), axis=0, keepdims=True)
    lo = gi * EXPERTS_PER_GROUP
    in_grp = (row_e >= lo) & (row_e < lo + EXPERTS_PER_GROUP)
    m1 = jnp.max(jnp.where(in_grp, ex, neg), axis=0, keepdims=True)
    i1 = jnp.min(jnp.where(in_grp & (ex == m1), row_e, big), axis=0, keepdims=True)
    rest = in_grp & (row_e != i1)
    m2 = jnp.max(jnp.where(rest, ex, neg), axis=0, keepdims=True)
    i2 = jnp.min(jnp.where(rest & (ex == m2), row_e, big), axis=0, keepdims=True)
    e2 = jnp.exp(m2 - m1)
    w1 = p_top / (1.0 + e2)
    w2 = p_top * e2 / (1.0 + e2)
    sel1 = row_e == i1
    sel2 = row_e == i2
    hits = jnp.where(sel1 | sel2, 1.0, 0.0)
    src = lax.broadcasted_iota(jnp.int32, (tm, tm), 0)
    dst = lax.broadcasted_iota(jnp.int32, (tm, tm), 1)
    before = _dot(hits.astype(_BF16), jnp.where(src < dst, 1.0, 0.0).astype(_BF16)) + base
    rank1 = jnp.sum(jnp.where(sel1, before, 0.0), axis=0, keepdims=True)
    rank2 = jnp.sum(jnp.where(sel2, before, 0.0), axis=0, keepdims=True)
    fields = {R_E1: i1, R_E2: i2, R_W1: w1, R_W2: w2, R_RANK1: rank1, R_RANK2: rank2,
              R_CODE1: i1 * CODE_SHIFT + rank1, R_CODE2: i2 * CODE_SHIFT + rank2}
    row8 = lax.broadcasted_iota(jnp.int32, (SUBLANES, tm), 0)
    route_t = jnp.zeros((SUBLANES, tm), _F32)
    for r, val in fields.items():
        route_t = jnp.where(row8 == r, val, route_t)
    route = jnp.concatenate([route_t, jnp.zeros((LANES - SUBLANES, tm), _F32)], axis=0).T
    return route_t, route, base + jnp.sum(hits, axis=1, keepdims=True)


def _mixer_out_prompt_kernel(x_ref, mixa_ref, mixb_ref, wo_ref, g2_ref, wr_ref, br_ref,
                             x1_ref, xn_ref, route_ref, route_t_ref, cnt_ref, base_ref, logits_ref, wo_bf):
    i = pl.program_id(0)

    @pl.when(i == 0)
    def _():
        base_ref[...] = jnp.zeros_like(base_ref)
        logits_ref[...] = jnp.zeros_like(logits_ref)
        wo_bf[...] = wo_ref[...].astype(_BF16)

    prev_logits = logits_ref[...]
    x1 = x_ref[0] + _dot(mixa_ref[0], wo_bf[:D_SSM, :]) + _dot(mixb_ref[0], wo_bf[D_SSM:, :])
    xn = _rms(x1, g2_ref[...])
    x1_ref[...] = _pack_bf16_pair(x1)
    xn_ref[...] = _pack_bf16_pair(xn)
    logits_ref[...] = _dot(xn.astype(_BF16), wr_ref[...]) + br_ref[...]
    route_t, route, base = _route(prev_logits, base_ref[...])
    route_ref[...] = route
    route_t_ref[...] = route_t
    base = jnp.where(i >= 1, base, base_ref[...])
    base_ref[...] = base
    cnt_ref[...] = base


def _mixer_out_sample_kernel(x_ref, mix_ref, wo_ref, g2_ref, wr_ref, br_ref, cnt_in_ref,
                             x1_in, xn_in, route_in, route_t_in,
                             x1_ref, xn_ref, route_ref, route_t_ref, cnt_ref):
    del x1_in, xn_in, route_in, route_t_in
    x1 = (x_ref[...] + _dot_f32(mix_ref[:, :D_SSM], wo_ref[:D_SSM, :])
          + _dot_f32(mix_ref[:, D_SSM:], wo_ref[D_SSM:, :]))
    xn = _rms(x1, g2_ref[...])
    logits = _dot_f32(xn, wr_ref[...]) + br_ref[...]
    route_t, route, base = _route(logits, cnt_in_ref[...])
    x1_ref[...] = _pack_bf16_pair(x1)
    xn_ref[...] = _pack_bf16_pair(xn)
    route_ref[...] = route
    route_t_ref[...] = route_t
    cnt_ref[...] = base


def _mixer_out(x_p, mixa, mixb, x_s, mix_s, wo, g2, wr, br):
    n, l, d = x_p.shape
    ns = x_s.shape[0]
    t_all = n * l + ns
    tm = TOK_TM
    per_seq = l // tm
    n_tiles = n * per_seq
    cur = lambda i: jnp.minimum(i, n_tiles - 1)
    prev = lambda i: jnp.maximum(i - 1, 0)
    const = lambda *shape: pl.BlockSpec(shape, lambda i: (0,) * len(shape))
    seq = lambda w: pl.BlockSpec((1, tm, w), lambda i: (cur(i) // per_seq, cur(i) % per_seq, 0))
    tok = lambda w, which: pl.BlockSpec((tm, w), lambda i: (which(i), 0))
    tok_shapes = [jax.ShapeDtypeStruct((t_all, d // 2), _U32),
                  jax.ShapeDtypeStruct((t_all, d // 2), _U32),
                  jax.ShapeDtypeStruct((t_all, LANES), _F32),
                  jax.ShapeDtypeStruct((SUBLANES, t_all), _F32)]
    cnt_shape = jax.ShapeDtypeStruct((N_EXPERTS, 1), _F32)
    x1, xn, route, route_t, cnt = pl.pallas_call(
        _mixer_out_prompt_kernel,
        grid=(n_tiles + 1,),
        in_specs=[seq(d), seq(D_SSM), seq(D_GMLP),
                  const(d, d), const(1, d), const(d, LANES), const(1, LANES)],
        out_specs=[tok(d // 2, cur), tok(d // 2, cur), tok(LANES, prev),
                   pl.BlockSpec((SUBLANES, tm), lambda i: (0, prev(i))), const(N_EXPERTS, 1)],
        out_shape=tok_shapes + [cnt_shape],
        scratch_shapes=[pltpu.VMEM((N_EXPERTS, 1), _F32), pltpu.VMEM((tm, LANES), _F32),
                        pltpu.VMEM((d, d), _BF16)],
        compiler_params=pltpu.CompilerParams(
            dimension_semantics=("arbitrary",), vmem_limit_bytes=VMEM_LIMIT),
        name="mixer_out_prompt",
    )(x_p, mixa, mixb, wo, g2, wr.astype(_BF16), br)
    tail = (n * l) // ns
    c1 = lambda *shape: pl.BlockSpec(shape, lambda i: (0,) * len(shape))
    anyspec = pl.BlockSpec(memory_space=pl.ANY)
    tail_spec = lambda w: pl.BlockSpec((ns, w), lambda i: (tail, 0))
    return pl.pallas_call(
        _mixer_out_sample_kernel,
        grid=(1,),
        in_specs=[c1(ns, d), c1(ns, d), c1(d, d), c1(1, d), c1(d, LANES), c1(1, LANES), c1(N_EXPERTS, 1),
                  anyspec, anyspec, anyspec, anyspec],
        out_specs=[tail_spec(d // 2), tail_spec(d // 2), tail_spec(LANES),
                   pl.BlockSpec((SUBLANES, ns), lambda i: (0, tail)), c1(N_EXPERTS, 1)],
        out_shape=tok_shapes + [cnt_shape],
        input_output_aliases={7: 0, 8: 1, 9: 2, 10: 3},
        compiler_params=pltpu.CompilerParams(
            dimension_semantics=("arbitrary",), vmem_limit_bytes=VMEM_LIMIT),
        name="mixer_out_sample",
    )(x_s, mix_s, wo, g2, wr, br, cnt, x1, xn, route, route_t)


def _sc_stream(n_chunks, gather, write):
    gather(0).start()
    for j in range(n_chunks):
        if j + 1 < n_chunks:
            if j >= 1:
                write(j - 1).wait()
            gather(j + 1).start()
        gather(j).wait()
        write(j).start()
    if n_chunks >= 2:
        write(n_chunks - 2).wait()
    write(n_chunks - 1).wait()


def _sc_mesh():
    return plsc.VectorSubcoreMesh(core_axis_name="c", subcore_axis_name="s",
                                  num_cores=SC_CORES, num_subcores=SC_SUBCORES)


def _sc_buffers(chunk, w, dtype):
    return [pltpu.VMEM((chunk, w), dtype), pltpu.VMEM((chunk, w), dtype)] + [pltpu.SemaphoreType.DMA] * 4


def _sc_combine(table, idx, n_out, chunk):
    w = table.shape[1]
    rows_w = n_out // SC_WORKERS
    n_chunks = rows_w // chunk
    assert rows_w * SC_WORKERS == n_out and n_chunks * chunk == rows_w and rows_w % SUBLANES == 0

    def body(table_hbm, idx_hbm, out_hbm, idx_v, buf0, buf1, g0, g1, w0, w1):
        wid = lax.axis_index("s") * SC_CORES + lax.axis_index("c")
        base = pl.multiple_of(wid * rows_w, SUBLANES)
        pltpu.sync_copy(idx_hbm.at[pl.ds(base, rows_w)], idx_v)
        bufs, gsems, wsems = (buf0, buf1), (g0, g1), (w0, w1)

        def gather(j):
            return pltpu.make_async_copy(table_hbm.at[idx_v.at[pl.ds(j * chunk, chunk)]], bufs[j % 2], gsems[j % 2])

        def write(j):
            return pltpu.make_async_copy(bufs[j % 2], out_hbm.at[pl.ds(base + j * chunk, chunk)], wsems[j % 2])

        _sc_stream(n_chunks, gather, write)

    return pl.kernel(
        body,
        out_type=jax.ShapeDtypeStruct((n_out, w), table.dtype),
        mesh=_sc_mesh(),
        scratch_types=[pltpu.VMEM((rows_w,), jnp.int32)] + _sc_buffers(chunk, w, table.dtype),
        compiler_params=pltpu.CompilerParams(use_tc_tiling_on_sc=True),
        name="sc_combine",
    )(table, idx)


def _sc_dispatch(table, codes, start_row, n_out, chunk):
    t_all, w = table.shape
    n_pad = codes.shape[0]
    n_ent = 2 * t_all
    ent_w = n_pad // SC_WORKERS
    n_chunks = ent_w // chunk
    per_chunk = chunk // SC_LANES
    trash = n_out - (n_pad - n_ent)
    assert ent_w * SC_WORKERS == n_pad and n_chunks * chunk == ent_w
    assert per_chunk * SC_LANES == chunk and chunk <= LANES and n_pad - n_ent <= t_all

    def body(table_hbm, code_hbm, start_hbm, out_hbm, dest_hbm,
             code_v, dest_v, tok_v, dst_v, start_v, buf0, buf1, g0, g1, w0, w1):
        wid = lax.axis_index("s") * SC_CORES + lax.axis_index("c")
        ebase = pl.multiple_of(wid * ent_w, SUBLANES)
        pltpu.sync_copy(code_hbm.at[pl.ds(ebase, ent_w)], code_v)
        pltpu.sync_copy(start_hbm, start_v)
        lane = lax.iota(jnp.int32, SC_LANES)
        for j in range(n_chunks):
            for c in range(per_chunk):
                off = j * chunk + c * SC_LANES
                ent = ebase + off + lane
                code = code_v[pl.ds(off, SC_LANES)]
                d = plsc.load_gather(start_v, [code >> CODE_BITS]) + (code & ((1 << CODE_BITS) - 1))
                d = jnp.where(ent >= n_ent, trash + (ent - n_ent), d)
                tok = jnp.where(ent >= t_all, ent - t_all, ent)
                tok = jnp.where(tok >= t_all, tok - t_all, tok)
                dest_v[pl.ds(off, SC_LANES)] = d
                dst_v[j, pl.ds(c * SC_LANES, SC_LANES)] = d
                tok_v[j, pl.ds(c * SC_LANES, SC_LANES)] = tok
        pltpu.sync_copy(dest_v, dest_hbm.at[pl.ds(ebase, ent_w)])
        bufs, gsems, wsems = (buf0, buf1), (g0, g1), (w0, w1)

        def gather(j):
            return pltpu.make_async_copy(table_hbm.at[tok_v.at[j]], bufs[j % 2], gsems[j % 2])

        def scatter(j):
            return pltpu.make_async_copy(bufs[j % 2], out_hbm.at[dst_v.at[j]], wsems[j % 2])

        _sc_stream(n_chunks, gather, scatter)

    return pl.kernel(
        body,
        out_type=(jax.ShapeDtypeStruct((n_out, w), table.dtype), jax.ShapeDtypeStruct((n_pad,), jnp.int32)),
        mesh=_sc_mesh(),
        scratch_types=([pltpu.VMEM((ent_w,), jnp.int32), pltpu.VMEM((ent_w,), jnp.int32),
                        pltpu.VMEM((n_chunks, chunk), jnp.int32), pltpu.VMEM((n_chunks, chunk), jnp.int32),
                        pltpu.VMEM((LANES,), jnp.int32)] + _sc_buffers(chunk, w, table.dtype)),
        compiler_params=pltpu.CompilerParams(use_tc_tiling_on_sc=True, needs_layout_passes=False),
        name="sc_dispatch",
    )(table, codes, start_row)


def _experts_kernel(piece_start_ref, piece_row_ref, piece_cls_ref, *refs):
    n_w = 3 * W_SPLIT
    wg_refs, wu_refs, wd_refs = refs[:W_SPLIT], refs[W_SPLIT:2 * W_SPLIT], refs[2 * W_SPLIT:n_w]
    xs_hbm, ys_hbm, wg_bf, wu_bf, wd_bf, xbuf, ybuf, xsem, ysem = refs[n_w:]
    e = pl.program_id(0)
    g0 = piece_start_ref[e]
    n_here = piece_start_ref[e + 1] - g0
    n_total = piece_start_ref[N_EXPERTS]

    def per_class(g, fn):
        cls = piece_cls_ref[g]
        row = pl.multiple_of(piece_row_ref[g], EXP_UNIT)
        for c in range(1, EXP_CLASSES + 1):
            pl.when(cls == c)(lambda c=c: fn(c * EXP_UNIT, row))

    def x_copy(slot, rows, row):
        return pltpu.make_async_copy(xs_hbm.at[pl.ds(row, rows)], xbuf.at[slot, pl.ds(0, rows)], xsem.at[slot])

    def y_copy(slot, rows, row):
        return pltpu.make_async_copy(ybuf.at[slot, pl.ds(0, rows)], ys_hbm.at[pl.ds(row, rows)], ysem.at[slot])

    @pl.when((e == 0) & (n_total > 0))
    def _():
        per_class(0, lambda rows, row: x_copy(0, rows, row).start())

    for dst, chunks in ((wg_bf, wg_refs), (wu_bf, wu_refs), (wd_bf, wd_refs)):
        rows = dst.shape[0] // W_SPLIT
        for q, src in enumerate(chunks):
            dst[q * rows:(q + 1) * rows, :] = src[0, 0].astype(_BF16)

    def piece(j, carry):
        g = g0 + j
        slot = lax.rem(g, 2)
        per_class(g, lambda rows, row: x_copy(slot, rows, row).wait())

        @pl.when(g + 1 < n_total)
        def _():
            per_class(g + 1, lambda rows, row: x_copy(1 - slot, rows, row).start())

        @pl.when(g >= 2)
        def _():
            per_class(g - 2, lambda rows, row: y_copy(slot, rows, row).wait())

        def compute(rows, row):
            x = _unpack_bf16_pair(xbuf[slot, pl.ds(0, rows)]).astype(_BF16)
            a = _dot(x, wg_bf[...])
            u = _dot(x, wu_bf[...])
            h = (a * jax.nn.sigmoid(a) * u).astype(_BF16)
            ybuf[slot, pl.ds(0, rows)] = _pack_bf16_pair(_dot(h, wd_bf[...]))
            y_copy(slot, rows, row).start()

        per_class(g, compute)
        return carry

    lax.fori_loop(0, n_here, piece, 0)

    @pl.when(e == N_EXPERTS - 1)
    def _():
        @pl.when(n_total >= 2)
        def _():
            per_class(n_total - 2, lambda rows, row: y_copy(lax.rem(n_total, 2), rows, row).wait())

        @pl.when(n_total >= 1)
        def _():
            per_class(n_total - 1, lambda rows, row: y_copy(lax.rem(n_total - 1, 2), rows, row).wait())


def _experts(piece_start, piece_row, piece_cls, n_rows, xs, w_gate, w_up, w_down):
    dh = xs.shape[1]
    d = 2 * dh
    tm = EXP_UNIT * EXP_CLASSES
    anyspec = pl.BlockSpec(memory_space=pl.ANY)

    def chunk_specs(rows, cols):
        return [pl.BlockSpec((1, 1, rows // W_SPLIT, cols), lambda e, ps, pr, pc, q=q: (e, q, 0, 0))
                for q in range(W_SPLIT)]

    split = lambda w: w.reshape(w.shape[0], W_SPLIT, w.shape[1] // W_SPLIT, w.shape[2])
    grid_spec = pltpu.PrefetchScalarGridSpec(
        num_scalar_prefetch=3,
        grid=(N_EXPERTS,),
        in_specs=(chunk_specs(d, D_EXPERT) + chunk_specs(d, D_EXPERT) + chunk_specs(D_EXPERT, d) + [anyspec]),
        out_specs=anyspec,
        scratch_shapes=[pltpu.VMEM((d, D_EXPERT), _BF16), pltpu.VMEM((d, D_EXPERT), _BF16),
                        pltpu.VMEM((D_EXPERT, d), _BF16),
                        pltpu.VMEM((2, tm, dh), _U32), pltpu.VMEM((2, tm, dh), _U32),
                        pltpu.SemaphoreType.DMA((2,)), pltpu.SemaphoreType.DMA((2,))],
    )
    return pl.pallas_call(
        _experts_kernel,
        grid_spec=grid_spec,
        out_shape=jax.ShapeDtypeStruct((n_rows, dh), _U32),
        compiler_params=pltpu.CompilerParams(
            dimension_semantics=("arbitrary",), vmem_limit_bytes=VMEM_LIMIT),
        name="experts",
    )(piece_start, piece_row, piece_cls, *([split(w_gate)] * W_SPLIT), *([split(w_up)] * W_SPLIT),
      *([split(w_down)] * W_SPLIT), xs)


def _final_kernel(x1_ref, ya_ref, yb_ref, route_ref, gf_ref, y_ref):
    route = route_ref[...]
    x2 = (_unpack_bf16_pair(x1_ref[...]) + route[:, R_W1:R_W1 + 1] * _unpack_bf16_pair(ya_ref[...])
          + route[:, R_W2:R_W2 + 1] * _unpack_bf16_pair(yb_ref[...]))
    y_ref[...] = _rms(x2, gf_ref[...])


def _final(x1, yab, route, gf, n_prompt, n_sample):
    d = 2 * x1.shape[1]

    def call(tm, first_block, n_rows, name):
        tok = lambda w: pl.BlockSpec((tm, w), lambda i: (first_block + i, 0))
        sel = lambda k: pl.BlockSpec((None, tm, d // 2), lambda i: (k, first_block + i, 0))
        return pl.pallas_call(
            _final_kernel,
            grid=(n_rows // tm,),
            in_specs=[tok(d // 2), sel(0), sel(1), tok(LANES), pl.BlockSpec((1, d), lambda i: (0, 0))],
            out_specs=pl.BlockSpec((tm, d), lambda i: (i, 0)),
            out_shape=jax.ShapeDtypeStruct((n_rows, d), _F32),
            compiler_params=pltpu.CompilerParams(
                dimension_semantics=("arbitrary",), vmem_limit_bytes=VMEM_LIMIT),
            name=name,
        )(x1, yab, yab, route, gf)

    return (call(FINAL_TM, 0, n_prompt, "final_prompt"),
            call(n_sample, n_prompt // n_sample, n_sample, "final_sample"))


def _powers(lam_re, lam_im, dt):
    out = []
    for m in range(SSM_BLK + 1):
        mag = jnp.exp(m * lam_re * dt)
        ang = m * lam_im * dt
        out.append((mag * jnp.cos(ang), mag * jnp.sin(ang)))
    return out


def _spread(x, copies):
    w = x.shape[1]
    src = lax.broadcasted_iota(jnp.int32, (w, w * copies), 0)
    dst = lax.broadcasted_iota(jnp.int32, (w, w * copies), 1)
    return _dot_f32(x, jnp.where(dst % w == src, 1.0, 0.0))


def _ssm_prep_kernel(lam_ref, b_re, b_im, c_re, c_im, v_ref, r_ref, wb_ref, wc_ref, coef_ref):
    n_p, n_h = SSM_STATE, SSM_GROUP
    lr, li, dt = lam_ref[0:1, :], lam_ref[1:2, :], lam_ref[2:3, :]
    pw = _powers(lr, li, dt)
    den = lr * lr + li * li
    nr, ni = pw[1][0] - 1.0, pw[1][1]
    k_re = (nr * lr + ni * li) / den
    k_im = (ni * lr - nr * li) / den
    coef_ref[...] = jnp.concatenate(
        [pw[1][0], pw[1][1], pw[SSM_BLK][0], pw[SSM_BLK][1], jnp.zeros((SUBLANES - 4, TILE_STATE), _F32)], axis=0)

    on_diag_b = (lax.broadcasted_iota(jnp.int32, (TILE_STATE, LANES), 0) // n_p
                 == lax.broadcasted_iota(jnp.int32, (TILE_STATE, LANES), 1) // n_h)
    rows_gp = lambda ref: ref[...].reshape(TILE_STATE, n_h)
    bt_re = jnp.where(on_diag_b, _spread(rows_gp(b_re), SUBLANES), 0.0).T
    bt_im = jnp.where(on_diag_b, _spread(rows_gp(b_im), SUBLANES), 0.0).T
    bb_re = k_re * bt_re - k_im * bt_im
    bb_im = k_re * bt_im + k_im * bt_re
    wb_ref[0] = jnp.concatenate([bb_re, bb_im], axis=1)
    v_rows = []
    for s in range(SSM_BLK):
        pr, pi = pw[SSM_BLK - 1 - s]
        v_rows.append(jnp.concatenate([pr * bb_re - pi * bb_im, pr * bb_im + pi * bb_re], axis=1))
    v_ref[0] = jnp.concatenate(v_rows, axis=0).astype(v_ref.dtype)

    on_diag_c = (lax.broadcasted_iota(jnp.int32, (LANES, TILE_STATE), 0) // n_h
                 == lax.broadcasted_iota(jnp.int32, (LANES, TILE_STATE), 1) // n_p)
    rows_gh = lambda ref: ref[...].reshape(LANES, n_p)
    ct_re = jnp.where(on_diag_c, _spread(rows_gh(c_re), SUBLANES), 0.0)
    ct_im = jnp.where(on_diag_c, _spread(rows_gh(c_im), SUBLANES), 0.0)
    cl = [(ct_re * pr - ct_im * pi, ct_re * pi + ct_im * pr) for pr, pi in pw]
    wc_ref[0] = jnp.concatenate([cl[0][0], -cl[0][1]], axis=1).T
    nt = lambda a, b: lax.dot_general(a, b, (((1,), (1,)), ((), ())), precision=lax.Precision.HIGHEST,
                                      preferred_element_type=_F32)
    direct = [nt(cl[m][0], bb_re) - nt(cl[m][1], bb_im) for m in range(SSM_BLK)]
    zero = jnp.zeros((LANES, LANES), _F32)
    rt = jnp.concatenate(
        [jnp.concatenate([cl[i + 1][0], -cl[i + 1][1]]
                         + [direct[i - s] if s <= i else zero for s in range(SSM_BLK)], axis=1)
         for i in range(SSM_BLK)], axis=0)
    r_ref[0] = rt.T.astype(r_ref.dtype)


def _ssm_params(lam_re, lam_im, log_dt, b_re, b_im, c_re, c_im, d_skip):
    n_g, n_p, n_h = N_SSM_GROUPS, SSM_STATE, SSM_GROUP
    dt = jnp.repeat(jnp.exp(log_dt), n_p)
    lam = jnp.zeros((SUBLANES, STATE_COLS), _F32).at[0].set(lam_re.reshape(-1)).at[1].set(
        lam_im.reshape(-1)).at[2].set(dt)
    groups = lambda r, c: pl.BlockSpec((SUBLANES, r, c), lambda k: (k, 0, 0))
    out3 = lambda rows, w: pl.BlockSpec((1, rows, w), lambda k: (k, 0, 0))
    cols = pl.BlockSpec((SUBLANES, TILE_STATE), lambda k: (0, k))
    k_blk = SSM_BLK * LANES
    v, r, wb, wc, coef = pl.pallas_call(
        _ssm_prep_kernel,
        grid=(N_LANE_TILES,),
        in_specs=[cols, groups(n_p, n_h), groups(n_p, n_h), groups(n_h, n_p), groups(n_h, n_p)],
        out_specs=[out3(k_blk, 2 * TILE_STATE), out3(2 * TILE_STATE + k_blk, k_blk),
                   out3(LANES, 2 * TILE_STATE), out3(2 * TILE_STATE, LANES), cols],
        out_shape=[jax.ShapeDtypeStruct((N_LANE_TILES, k_blk, 2 * TILE_STATE), _BF16),
                   jax.ShapeDtypeStruct((N_LANE_TILES, 2 * TILE_STATE + k_blk, k_blk), _BF16),
                   jax.ShapeDtypeStruct((N_LANE_TILES, LANES, 2 * TILE_STATE), _F32),
                   jax.ShapeDtypeStruct((N_LANE_TILES, 2 * TILE_STATE, LANES), _F32),
                   jax.ShapeDtypeStruct((SUBLANES, STATE_COLS), _F32)],
        compiler_params=pltpu.CompilerParams(
            dimension_semantics=("arbitrary",), vmem_limit_bytes=VMEM_LIMIT),
        name="ssm_prep",
    )(lam, b_re, b_im, c_re, c_im)
    return wb, wc, v, r, coef, d_skip.reshape(1, D_SSM)


def _dispatch_plan(route_t, cnt):
    t_all = route_t.shape[1]
    codes = route_t[R_CODE1:R_CODE2 + 1].astype(jnp.int32).reshape(-1)
    per_pass = SC_WORKERS * DISPATCH_CHUNK
    codes = jnp.pad(codes, (0, -(2 * t_all) % per_pass))
    counts = cnt[:, 0].astype(jnp.int32)
    zero = jnp.zeros((1,), jnp.int32)
    units = (counts + EXP_UNIT - 1) // EXP_UNIT
    unit_start = jnp.concatenate([zero, jnp.cumsum(units)])
    start_row = jnp.zeros((LANES,), jnp.int32).at[:N_EXPERTS].set(unit_start[:N_EXPERTS] * EXP_UNIT)
    pieces = (units + EXP_CLASSES - 1) // EXP_CLASSES
    piece_start = jnp.concatenate([zero, jnp.cumsum(pieces)])
    tm = EXP_UNIT * EXP_CLASSES
    max_units = (2 * t_all + N_EXPERTS * (EXP_UNIT - 1)) // EXP_UNIT
    max_pieces = (max_units + N_EXPERTS * (EXP_CLASSES - 1)) // EXP_CLASSES
    g = jnp.arange(max_pieces, dtype=jnp.int32)
    owner = ((g[:, None] >= piece_start[None, :-1]) & (g[:, None] < piece_start[None, 1:])).astype(jnp.int32)
    pick = lambda table: jnp.sum(owner * table[None, :], axis=1)
    first_unit = pick(unit_start[:-1]) + (g - pick(piece_start[:-1])) * EXP_CLASSES
    piece_row = first_unit * EXP_UNIT
    piece_cls = jnp.clip(pick(unit_start[1:]) - first_unit, 1, EXP_CLASSES)
    n_rows = (max_units * EXP_UNIT + tm - 1) // tm * tm + tm
    return codes, start_row, n_rows, piece_start, piece_row, piece_cls


def kernel(x_prompt, x_sample, state_ssm_re, state_ssm_im, norm1_g, w_in, lam_re, lam_im, log_dt, ssm_b_re, ssm_b_im, ssm_c_re, ssm_c_im, ssm_d, gmlp_norm_g, gmlp_w_s, gmlp_b_s, out_norm_ssm_g, out_norm_gmlp_g, w_out, norm2_g, w_router_group, b_router_group, w_router_expert, b_router_expert, w_gate, w_up, w_down, final_norm_g):
    n, l, d = x_prompt.shape
    ns = x_sample.shape[0]
    t_all = n * l + ns
    li = 0
    g1 = norm1_g[li].reshape(1, d)
    gn = gmlp_norm_g[li].reshape(1, D_GMLP)
    tril = jnp.tril(jnp.ones((CHUNK, CHUNK), dtype=bool))
    ws_tril = jnp.where(tril[None], gmlp_w_s[li], 0.0)
    bs = gmlp_b_s[li]
    gog = out_norm_gmlp_g[li].reshape(1, D_GMLP)
    gos = out_norm_ssm_g[li].reshape(1, D_SSM)
    wb, wc, v_blk, r_blk, coef, dsk = _ssm_params(
        lam_re[li], lam_im[li], log_dt[li], ssm_b_re[li], ssm_b_im[li], ssm_c_re[li], ssm_c_im[li], ssm_d[li])
    g2 = norm2_g[li].reshape(1, d)
    pad = LANES - N_EXPERTS - N_EXPERT_GROUPS
    wr = jnp.concatenate([w_router_expert[li], w_router_group[li], jnp.zeros((d, pad), _F32)], axis=1)
    br = jnp.concatenate([b_router_expert[li], b_router_group[li], jnp.zeros((pad,), _F32)]).reshape(1, LANES)

    xa, sg, mixb = _front_prompt(x_prompt, g1, w_in[li], gn, ws_tril.astype(_BF16), bs.T, gog)
    mixa, hfin = _ssm_prompt(xa, sg, v_blk, r_blk, coef, dsk, gos)
    w00 = jnp.repeat(ws_tril[:, 0, 0], GMLP_HEAD).reshape(1, D_GMLP)
    b0 = jnp.repeat(bs[:, 0], GMLP_HEAD).reshape(1, D_GMLP)
    mix_s, hr_s, hi_s, vrow = _front_sample(
        x_sample.reshape(ns, d), g1, w_in[li], gn, w00, b0, gog, wb, wc, coef, dsk, gos,
        state_ssm_re[li].reshape(ns, STATE_COLS), state_ssm_im[li].reshape(ns, STATE_COLS))

    x1, xn, route, route_t, cnt = _mixer_out(x_prompt, mixa, mixb, x_sample.reshape(ns, d), mix_s,
                                             w_out[li], g2, wr, br)
    codes, start_row, n_rows, piece_start, piece_row, piece_cls = _dispatch_plan(route_t, cnt)
    xs, dest = _sc_dispatch(xn, codes, start_row, n_rows, DISPATCH_CHUNK)
    ys = _experts(piece_start, piece_row, piece_cls, n_rows, xs, w_gate[li], w_up[li], w_down[li])
    yab = _sc_combine(ys, dest, 2 * t_all, COMBINE_CHUNK).reshape(2, t_all, d // 2)
    y_p, y_s = _final(x1, yab, route, final_norm_g.reshape(1, d), n * l, ns)

    hf = hfin.reshape(n, N_LANE_TILES, 2, 8, SSM_STATE)
    re_p = hf[:, :, 0].reshape(1, n, N_SSM_GROUPS, SSM_STATE)
    im_p = hf[:, :, 1].reshape(1, n, N_SSM_GROUPS, SSM_STATE)
    re_s = hr_s.reshape(1, ns, N_SSM_GROUPS, SSM_STATE)
    im_s = hi_s.reshape(1, ns, N_SSM_GROUPS, SSM_STATE)
    return (y_p.reshape(n, l, d), y_s.reshape(ns, 1, d), re_p, im_p, re_s, im_s,
            vrow.reshape(1, ns, 1, D_GMLP))
```

```python
import math

import jax
import jax.numpy as jnp
from jax import lax
from jax.experimental import pallas as pl
from jax.experimental.pallas import tpu as pltpu
from jax.experimental.pallas import tpu_sc as plsc

D_MODEL = 1024
D_SSM = 512
D_GMLP = 512
SSM_GROUP = 16
N_SSM_GROUPS = 32
SSM_STATE = 64
CHUNK = 128
N_GMLP_HEADS = 4
GMLP_HEAD = 128
N_EXPERT_GROUPS = 4
EXPERTS_PER_GROUP = 8
N_EXPERTS = 32
D_EXPERT = 512
D_IN = 2048
EPS = 1e-6

LANES = 128
SUBLANES = 8
N_LANE_TILES = D_SSM // LANES
STATE_COLS = N_SSM_GROUPS * SSM_STATE
TILE_STATE = STATE_COLS // N_LANE_TILES
VMEM_LIMIT = 56 * 1024 * 1024

SC_CORES = 2
SC_SUBCORES = 16
SC_LANES = 16
SC_WORKERS = SC_CORES * SC_SUBCORES

FRONT_TL = 512
SSM_LC = 256
SSM_BLK = 4
COEF_LB_RE, COEF_LB_IM, COEF_LBLK_RE, COEF_LBLK_IM = 0, 1, 2, 3
TOK_TM = 512
FINAL_TM = 1024
EXP_UNIT = 128
EXP_CLASSES = 6
W_SPLIT = 4
DISPATCH_CHUNK = 80
COMBINE_CHUNK = 24

R_E1, R_E2, R_W1, R_W2, R_RANK1, R_RANK2, R_CODE1, R_CODE2 = 0, 1, 2, 3, 4, 5, 6, 7
CODE_BITS = 16
CODE_SHIFT = float(1 << CODE_BITS)

_INV_SQRT2 = 1.0 / math.sqrt(2.0)
_BF16 = jnp.bfloat16
_F32 = jnp.float32
_U32 = jnp.uint32


def _gelu(x):
    return 0.5 * x * (1.0 + lax.erf(x * _INV_SQRT2))


def _rms(x, g):
    return x * lax.rsqrt(jnp.mean(x * x, axis=-1, keepdims=True) + EPS) * g


def _dot(a, b):
    return jnp.dot(a, b, preferred_element_type=_F32)


def _dot_f32(a, b):
    return jnp.dot(a, b, preferred_element_type=_F32, precision=lax.Precision.HIGHEST)


def _pack_bf16_pair(x):
    w = x.shape[1] // 2
    hi = lax.bitcast_convert_type(x[:, :w].astype(_BF16).astype(_F32), _U32)
    lo = lax.bitcast_convert_type(x[:, w:].astype(_BF16).astype(_F32), _U32)
    return hi | (lo >> 16)


def _unpack_bf16_pair(p):
    hi = lax.bitcast_convert_type(p & jnp.uint32(0xFFFF0000), _F32)
    lo = lax.bitcast_convert_type(p << 16, _F32)
    return jnp.concatenate([hi, lo], axis=-1)


def _head_norm_gelu(vb, gn):
    v = _gelu(vb)
    parts = []
    for h in range(N_GMLP_HEADS):
        vh = v[:, h * GMLP_HEAD:(h + 1) * GMLP_HEAD]
        parts.append(vh * lax.rsqrt(jnp.mean(vh * vh, axis=-1, keepdims=True) + EPS))
    return jnp.concatenate(parts, axis=-1) * gn


def _front_prompt_kernel(x_ref, g1_ref, win_ref, gn_ref, ws_ref, bs_ref, gog_ref,
                         xa_ref, sg_ref, mixb_ref, win_bf, z_ref):
    @pl.when(pl.program_id(0) == 0)
    def _():
        win_bf[...] = win_ref[...].astype(_BF16)
        z_ref[...] = jnp.zeros_like(z_ref)

    z = z_ref[...]
    x = x_ref[0]
    hn = _rms(x, g1_ref[...]).astype(_BF16)
    z_ref[...] = _dot(hn, win_bf[...])
    xa_ref[0] = z[:, :D_SSM]
    sg_ref[0] = jax.nn.sigmoid(z[:, D_SSM:2 * D_SSM])
    ub = _gelu(z[:, 2 * D_SSM:2 * D_SSM + D_GMLP])
    vbn = _head_norm_gelu(z[:, 2 * D_SSM + D_GMLP:], gn_ref[...]).astype(_BF16)
    tl = x.shape[0]
    rows = []
    for c in range(tl // CHUNK):
        heads = []
        for h in range(N_GMLP_HEADS):
            vh = vbn[c * CHUNK:(c + 1) * CHUNK, h * GMLP_HEAD:(h + 1) * GMLP_HEAD]
            heads.append(_dot(ws_ref[h], vh) + bs_ref[:, h:h + 1])
        rows.append(jnp.concatenate(heads, axis=-1))
    s = jnp.concatenate(rows, axis=0)
    mixb_ref[0] = _rms(ub * s, gog_ref[...]).astype(_BF16)


def _front_prompt(x, g1, win, gn, ws_tril_bf, bs_t, gog):
    n, l, d = x.shape
    tl = FRONT_TL
    per_seq = l // tl
    n_tiles = n * per_seq
    cur = lambda i: jnp.minimum(i, n_tiles - 1)
    prev = lambda i: jnp.maximum(i - 1, 0)
    const = lambda *shape: pl.BlockSpec(shape, lambda i: (0,) * len(shape))
    seq = lambda w, which: pl.BlockSpec((1, tl, w), lambda i: (which(i) // per_seq, which(i) % per_seq, 0))
    return pl.pallas_call(
        _front_prompt_kernel,
        grid=(n_tiles + 1,),
        in_specs=[seq(d, cur), const(1, d), const(d, D_IN), const(1, D_GMLP),
                  const(N_GMLP_HEADS, CHUNK, CHUNK), const(CHUNK, N_GMLP_HEADS), const(1, D_GMLP)],
        out_specs=[seq(D_SSM, prev), seq(D_SSM, prev), seq(D_GMLP, prev)],
        out_shape=[jax.ShapeDtypeStruct((n, l, D_SSM), _F32),
                   jax.ShapeDtypeStruct((n, l, D_SSM), _F32),
                   jax.ShapeDtypeStruct((n, l, D_GMLP), _BF16)],
        scratch_shapes=[pltpu.VMEM((d, D_IN), _BF16), pltpu.VMEM((tl, D_IN), _F32)],
        compiler_params=pltpu.CompilerParams(
            dimension_semantics=("arbitrary",), vmem_limit_bytes=VMEM_LIMIT),
        name="front_prompt",
    )(x, g1, win, gn, ws_tril_bf, bs_t, gog)


def _ssm_prompt_kernel(xa_ref, sg_ref, v_ref, r_ref, coef_ref, dsk_ref, gos_ref,
                       mixa_ref, hfin_ref, s_ref, st_ref):
    lc = xa_ref.shape[1]
    nblk = lc // SSM_BLK
    rows = nblk * SUBLANES

    @pl.when(pl.program_id(0) == 0)
    def _():
        st_ref[...] = jnp.zeros_like(st_ref)

    def by_position(ref):
        t = pltpu.einshape("btc->tbc", ref[...]).reshape(nblk, SSM_BLK, SUBLANES, D_SSM)
        return [t[:, i].reshape(rows, D_SSM) for i in range(SSM_BLK)]

    xs = by_position(xa_ref)
    xs_bf = [x.astype(_BF16) for x in xs]
    xk = [jnp.concatenate([x[:, k * LANES:(k + 1) * LANES] for x in xs_bf], axis=-1)
          for k in range(N_LANE_TILES)]
    for k in range(N_LANE_TILES):
        s_ref[:, 2 * TILE_STATE * k:2 * TILE_STATE * (k + 1)] = _dot(xk[k], v_ref[k])

    for kk in range(0, N_LANE_TILES, 2):
        tiles = (kk, kk + 1)
        cols = [(2 * TILE_STATE * k, 2 * TILE_STATE * k + TILE_STATE) for k in tiles]
        lbs = [tuple(jnp.broadcast_to(coef_ref[row:row + 1, k * TILE_STATE:(k + 1) * TILE_STATE],
                                      (SUBLANES, TILE_STATE)) for row in (COEF_LBLK_RE, COEF_LBLK_IM))
               for k in tiles]

        def body(j, carry, cols=cols, lbs=lbs):
            r0 = pl.multiple_of(j * SUBLANES, SUBLANES)
            out = []
            for q, ((c_re, c_im), (lr, li)) in enumerate(zip(cols, lbs)):
                hr, hi = carry[2 * q], carry[2 * q + 1]
                sr = s_ref[pl.ds(r0, SUBLANES), c_re:c_re + TILE_STATE]
                si = s_ref[pl.ds(r0, SUBLANES), c_im:c_im + TILE_STATE]
                s_ref[pl.ds(r0, SUBLANES), c_re:c_re + TILE_STATE] = hr
                s_ref[pl.ds(r0, SUBLANES), c_im:c_im + TILE_STATE] = hi
                out += [lr * hr - li * hi + sr, lr * hi + li * hr + si]
            return tuple(out)

        init = tuple(st_ref[:, c:c + TILE_STATE] for c_pair in cols for c in c_pair)
        fin = lax.fori_loop(0, nblk, body, init, unroll=2)
        for q, (c_re, c_im) in enumerate(cols):
            st_ref[:, c_re:c_re + TILE_STATE] = fin[2 * q]
            st_ref[:, c_im:c_im + TILE_STATE] = fin[2 * q + 1]

    yk = []
    for k in range(N_LANE_TILES):
        h_in = s_ref[:, 2 * TILE_STATE * k:2 * TILE_STATE * (k + 1)].astype(_BF16)
        yk.append(_dot(jnp.concatenate([h_in, xk[k]], axis=-1), r_ref[k]))
    sgs = by_position(sg_ref)
    outs = []
    for i in range(SSM_BLK):
        y = jnp.concatenate([y_k[:, i * LANES:(i + 1) * LANES] for y_k in yk], axis=-1) + dsk_ref[...] * xs[i]
        outs.append(_rms(_gelu(y) * sgs[i], gos_ref[...]).reshape(nblk, SUBLANES, D_SSM))
    mixa = jnp.stack(outs, axis=1).reshape(lc, SUBLANES, D_SSM)
    mixa_ref[...] = pltpu.einshape("tbc->btc", mixa).astype(_BF16)
    hfin_ref[...] = st_ref[...]


def _ssm_prompt(xa, sg, v, r, coef, dsk, gos):
    n, l, _ = xa.shape
    lc = SSM_LC
    const = lambda *shape: pl.BlockSpec(shape, lambda i: (0,) * len(shape))
    seq_spec = pl.BlockSpec((n, lc, D_SSM), lambda i: (0, i, 0))
    return pl.pallas_call(
        _ssm_prompt_kernel,
        grid=(l // lc,),
        in_specs=[seq_spec, seq_spec, const(*v.shape), const(*r.shape),
                  const(*coef.shape), const(1, D_SSM), const(1, D_SSM)],
        out_specs=[seq_spec, const(n, 2 * STATE_COLS)],
        out_shape=[jax.ShapeDtypeStruct((n, l, D_SSM), _BF16),
                   jax.ShapeDtypeStruct((n, 2 * STATE_COLS), _F32)],
        scratch_shapes=[pltpu.VMEM((lc // SSM_BLK * n, 2 * STATE_COLS), _F32),
                        pltpu.VMEM((n, 2 * STATE_COLS), _F32)],
        compiler_params=pltpu.CompilerParams(
            dimension_semantics=("arbitrary",), vmem_limit_bytes=VMEM_LIMIT),
        name="ssm_prompt",
    )(xa, sg, v, r, coef, dsk, gos)


def _front_sample_kernel(x_ref, g1_ref, win_ref, gn_ref, w00_ref, b0_ref, gog_ref,
                         wb_ref, wc_ref, coef_ref, dsk_ref, gos_ref, h0r_ref, h0i_ref,
                         mix_ref, hr_ref, hi_ref, vrow_ref):
    x = x_ref[...]
    hn = _rms(x, g1_ref[...])
    z = _dot_f32(hn, win_ref[...])
    xa = z[:, :D_SSM]
    ys = []
    for k in range(N_LANE_TILES):
        bu = _dot_f32(xa[:, k * LANES:(k + 1) * LANES], wb_ref[k])
        sl = slice(k * TILE_STATE, (k + 1) * TILE_STATE)
        lr, li = coef_ref[COEF_LB_RE:COEF_LB_RE + 1, sl], coef_ref[COEF_LB_IM:COEF_LB_IM + 1, sl]
        h0r, h0i = h0r_ref[:, sl], h0i_ref[:, sl]
        nr = lr * h0r - li * h0i + bu[:, :TILE_STATE]
        ni = lr * h0i + li * h0r + bu[:, TILE_STATE:]
        hr_ref[:, sl] = nr
        hi_ref[:, sl] = ni
        ys.append(_dot_f32(jnp.concatenate([nr, ni], axis=-1), wc_ref[k]))
    y = jnp.concatenate(ys, axis=-1) + dsk_ref[...] * xa
    ya = _gelu(y) * jax.nn.sigmoid(z[:, D_SSM:2 * D_SSM])
    mix_ref[:, :D_SSM] = _rms(ya, gos_ref[...])
    ub = _gelu(z[:, 2 * D_SSM:2 * D_SSM + D_GMLP])
    vbn = _head_norm_gelu(z[:, 2 * D_SSM + D_GMLP:], gn_ref[...])
    vrow_ref[...] = vbn
    s = w00_ref[...] * vbn + b0_ref[...]
    mix_ref[:, D_SSM:] = _rms(ub * s, gog_ref[...])


def _front_sample(x, g1, win, gn, w00, b0, gog, wb, wc, coef, dsk, gos, h0r, h0i):
    n = x.shape[0]
    vmem = pl.BlockSpec(memory_space=pltpu.VMEM)
    return pl.pallas_call(
        _front_sample_kernel,
        in_specs=[vmem] * 14,
        out_specs=[vmem] * 4,
        out_shape=[jax.ShapeDtypeStruct((n, D_MODEL), _F32),
                   jax.ShapeDtypeStruct((n, STATE_COLS), _F32),
                   jax.ShapeDtypeStruct((n, STATE_COLS), _F32),
                   jax.ShapeDtypeStruct((n, D_GMLP), _F32)],
        compiler_params=pltpu.CompilerParams(vmem_limit_bytes=VMEM_LIMIT),
        name="front_sample",
    )(x, g1, win, gn, w00, b0, gog, wb, wc, coef, dsk, gos, h0r, h0i)


def _route(logits, base):
    tm = logits.shape[0]
    lt = logits.T
    ex = lt[:N_EXPERTS, :]
    gr = lt[N_EXPERTS:N_EXPERTS + SUBLANES, :]
    row_e = lax.broadcasted_iota(jnp.int32, ex.shape, 0).astype(_F32)
    row_g = lax.broadcasted_iota(jnp.int32, gr.shape, 0).astype(_F32)
    neg = jnp.float32(-jnp.inf)
    big = jnp.float32(LANES)
    is_g = row_g < N_EXPERT_GROUPS
    gl = jnp.where(is_g, gr, neg)
    gmax = jnp.max(gl, axis=0, keepdims=True)
    gi = jnp.min(jnp.where(gl == gmax, row_g, big), axis=0, keepdims=True)
    p_top = 1.0 / jnp.sum(jnp.where(is_g, jnp.exp(gl - gmax), 0.0), axis=0, keepdims=True)
    lo = gi * EXPERTS_PER_GROUP
    in_grp = (row_e >= lo) & (row_e < lo + EXPERTS_PER_GROUP)
    m1 = jnp.max(jnp.where(in_grp, ex, neg), axis=0, keepdims=True)
    i1 = jnp.min(jnp.where(in_grp & (ex == m1), row_e, big), axis=0, keepdims=True)
    rest = in_grp & (row_e != i1)
    m2 = jnp.max(jnp.where(rest, ex, neg), axis=0, keepdims=True)
    i2 = jnp.min(jnp.where(rest & (ex == m2), row_e, big), axis=0, keepdims=True)
    e2 = jnp.exp(m2 - m1)
    w1 = p_top / (1.0 + e2)
    w2 = p_top * e2 / (1.0 + e2)
    sel1 = row_e == i1
    sel2 = row_e == i2
    hits = jnp.where(sel1 | sel2, 1.0, 0.0)
    src = lax.broadcasted_iota(jnp.int32, (tm, tm), 0)
    dst = lax.broadcasted_iota(jnp.int32, (tm, tm), 1)
    before = _dot(hits.astype(_BF16), jnp.where(src < dst, 1.0, 0.0).astype(_BF16)) + base
    rank1 = jnp.sum(jnp.where(sel1, before, 0.0), axis=0, keepdims=True)
    rank2 = jnp.sum(jnp.where(sel2, before, 0.0), axis=0, keepdims=True)
    fields = {R_E1: i1, R_E2: i2, R_W1: w1, R_W2: w2, R_RANK1: rank1, R_RANK2: rank2,
              R_CODE1: i1 * CODE_SHIFT + rank1, R_CODE2: i2 * CODE_SHIFT + rank2}
    row8 = lax.broadcasted_iota(jnp.int32, (SUBLANES, tm), 0)
    route_t = jnp.zeros((SUBLANES, tm), _F32)
    for r, val in fields.items():
        route_t = jnp.where(row8 == r, val, route_t)
    route = jnp.concatenate([route_t, jnp.zeros((LANES - SUBLANES, tm), _F32)], axis=0).T
    return route_t, route, base + jnp.sum(hits, axis=1, keepdims=True)


def _mixer_out_prompt_kernel(x_ref, mixa_ref, mixb_ref, wo_ref, g2_ref, wr_ref, br_ref,
                             x1_ref, xn_ref, route_ref, route_t_ref, cnt_ref, base_ref, logits_ref, wo_bf):
    i = pl.program_id(0)

    @pl.when(i == 0)
    def _():
        base_ref[...] = jnp.zeros_like(base_ref)
        logits_ref[...] = jnp.zeros_like(logits_ref)
        wo_bf[...] = wo_ref[...].astype(_BF16)

    prev_logits = logits_ref[...]
    x1 = x_ref[0] + _dot(mixa_ref[0], wo_bf[:D_SSM, :]) + _dot(mixb_ref[0], wo_bf[D_SSM:, :])
    xn = _rms(x1, g2_ref[...])
    x1_ref[...] = _pack_bf16_pair(x1)
    xn_ref[...] = _pack_bf16_pair(xn)
    logits_ref[...] = _dot(xn.astype(_BF16), wr_ref[...]) + br_ref[...]
    route_t, route, base = _route(prev_logits, base_ref[...])
    route_ref[...] = route
    route_t_ref[...] = route_t
    base = jnp.where(i >= 1, base, base_ref[...])
    base_ref[...] = base
    cnt_ref[...] = base


def _mixer_out_sample_kernel(x_ref, mix_ref, wo_ref, g2_ref, wr_ref, br_ref, cnt_in_ref,
                             x1_in, xn_in, route_in, route_t_in,
                             x1_ref, xn_ref, route_ref, route_t_ref, cnt_ref):
    del x1_in, xn_in, route_in, route_t_in
    x1 = (x_ref[...] + _dot_f32(mix_ref[:, :D_SSM], wo_ref[:D_SSM, :])
          + _dot_f32(mix_ref[:, D_SSM:], wo_ref[D_SSM:, :]))
    xn = _rms(x1, g2_ref[...])
    logits = _dot_f32(xn, wr_ref[...]) + br_ref[...]
    route_t, route, base = _route(logits, cnt_in_ref[...])
    x1_ref[...] = _pack_bf16_pair(x1)
    xn_ref[...] = _pack_bf16_pair(xn)
    route_ref[...] = route
    route_t_ref[...] = route_t
    cnt_ref[...] = base


def _mixer_out(x_p, mixa, mixb, x_s, mix_s, wo, g2, wr, br):
    n, l, d = x_p.shape
    ns = x_s.shape[0]
    t_all = n * l + ns
    tm = TOK_TM
    per_seq = l // tm
    n_tiles = n * per_seq
    cur = lambda i: jnp.minimum(i, n_tiles - 1)
    prev = lambda i: jnp.maximum(i - 1, 0)
    const = lambda *shape: pl.BlockSpec(shape, lambda i: (0,) * len(shape))
    seq = lambda w: pl.BlockSpec((1, tm, w), lambda i: (cur(i) // per_seq, cur(i) % per_seq, 0))
    tok = lambda w, which: pl.BlockSpec((tm, w), lambda i: (which(i), 0))
    tok_shapes = [jax.ShapeDtypeStruct((t_all, d // 2), _U32),
                  jax.ShapeDtypeStruct((t_all, d // 2), _U32),
                  jax.ShapeDtypeStruct((t_all, LANES), _F32),
                  jax.ShapeDtypeStruct((SUBLANES, t_all), _F32)]
    cnt_shape = jax.ShapeDtypeStruct((N_EXPERTS, 1), _F32)
    x1, xn, route, route_t, cnt = pl.pallas_call(
        _mixer_out_prompt_kernel,
        grid=(n_tiles + 1,),
        in_specs=[seq(d), seq(D_SSM), seq(D_GMLP),
                  const(d, d), const(1, d), const(d, LANES), const(1, LANES)],
        out_specs=[tok(d // 2, cur), tok(d // 2, cur), tok(LANES, prev),
                   pl.BlockSpec((SUBLANES, tm), lambda i: (0, prev(i))), const(N_EXPERTS, 1)],
        out_shape=tok_shapes + [cnt_shape],
        scratch_shapes=[pltpu.VMEM((N_EXPERTS, 1), _F32), pltpu.VMEM((tm, LANES), _F32),
                        pltpu.VMEM((d, d), _BF16)],
        compiler_params=pltpu.CompilerParams(
            dimension_semantics=("arbitrary",), vmem_limit_bytes=VMEM_LIMIT),
        name="mixer_out_prompt",
    )(x_p, mixa, mixb, wo, g2, wr.astype(_BF16), br)
    tail = (n * l) // ns
    c1 = lambda *shape: pl.BlockSpec(shape, lambda i: (0,) * len(shape))
    anyspec = pl.BlockSpec(memory_space=pl.ANY)
    tail_spec = lambda w: pl.BlockSpec((ns, w), lambda i: (tail, 0))
    return pl.pallas_call(
        _mixer_out_sample_kernel,
        grid=(1,),
        in_specs=[c1(ns, d), c1(ns, d), c1(d, d), c1(1, d), c1(d, LANES), c1(1, LANES), c1(N_EXPERTS, 1),
                  anyspec, anyspec, anyspec, anyspec],
        out_specs=[tail_spec(d // 2), tail_spec(d // 2), tail_spec(LANES),
                   pl.BlockSpec((SUBLANES, ns), lambda i: (0, tail)), c1(N_EXPERTS, 1)],
        out_shape=tok_shapes + [cnt_shape],
        input_output_aliases={7: 0, 8: 1, 9: 2, 10: 3},
        compiler_params=pltpu.CompilerParams(
            dimension_semantics=("arbitrary",), vmem_limit_bytes=VMEM_LIMIT),
        name="mixer_out_sample",
    )(x_s, mix_s, wo, g2, wr, br, cnt, x1, xn, route, route_t)


def _sc_stream(n_chunks, gather, write):
    gather(0).start()
    for j in range(n_chunks):
        if j + 1 < n_chunks:
            if j >= 1:
                write(j - 1).wait()
            gather(j + 1).start()
        gather(j).wait()
        write(j).start()
    if n_chunks >= 2:
        write(n_chunks - 2).wait()
    write(n_chunks - 1).wait()


def _sc_mesh():
    return plsc.VectorSubcoreMesh(core_axis_name="c", subcore_axis_name="s",
                                  num_cores=SC_CORES, num_subcores=SC_SUBCORES)


def _sc_buffers(chunk, w, dtype):
    return [pltpu.VMEM((chunk, w), dtype), pltpu.VMEM((chunk, w), dtype)] + [pltpu.SemaphoreType.DMA] * 4


def _sc_combine(table, idx, n_out, chunk):
    w = table.shape[1]
    rows_w = n_out // SC_WORKERS
    n_chunks = rows_w // chunk
    assert rows_w * SC_WORKERS == n_out and n_chunks * chunk == rows_w and rows_w % SUBLANES == 0

    def body(table_hbm, idx_hbm, out_hbm, idx_v, buf0, buf1, g0, g1, w0, w1):
        wid = lax.axis_index("s") * SC_CORES + lax.axis_index("c")
        base = pl.multiple_of(wid * rows_w, SUBLANES)
        pltpu.sync_copy(idx_hbm.at[pl.ds(base, rows_w)], idx_v)
        bufs, gsems, wsems = (buf0, buf1), (g0, g1), (w0, w1)

        def gather(j):
            return pltpu.make_async_copy(table_hbm.at[idx_v.at[pl.ds(j * chunk, chunk)]], bufs[j % 2], gsems[j % 2])

        def write(j):
            return pltpu.make_async_copy(bufs[j % 2], out_hbm.at[pl.ds(base + j * chunk, chunk)], wsems[j % 2])

        _sc_stream(n_chunks, gather, write)

    return pl.kernel(
        body,
        out_type=jax.ShapeDtypeStruct((n_out, w), table.dtype),
        mesh=_sc_mesh(),
        scratch_types=[pltpu.VMEM((rows_w,), jnp.int32)] + _sc_buffers(chunk, w, table.dtype),
        compiler_params=pltpu.CompilerParams(use_tc_tiling_on_sc=True),
        name="sc_combine",
    )(table, idx)


def _sc_dispatch(table, codes, start_row, n_out, chunk):
    t_all, w = table.shape
    n_pad = codes.shape[0]
    n_ent = 2 * t_all
    ent_w = n_pad // SC_WORKERS
    n_chunks = ent_w // chunk
    per_chunk = chunk // SC_LANES
    trash = n_out - (n_pad - n_ent)
    assert ent_w * SC_WORKERS == n_pad and n_chunks * chunk == ent_w
    assert per_chunk * SC_LANES == chunk and chunk <= LANES and n_pad - n_ent <= t_all

    def body(table_hbm, code_hbm, start_hbm, out_hbm, dest_hbm,
             code_v, dest_v, tok_v, dst_v, start_v, buf0, buf1, g0, g1, w0, w1):
        wid = lax.axis_index("s") * SC_CORES + lax.axis_index("c")
        ebase = pl.multiple_of(wid * ent_w, SUBLANES)
        pltpu.sync_copy(code_hbm.at[pl.ds(ebase, ent_w)], code_v)
        pltpu.sync_copy(start_hbm, start_v)
        lane = lax.iota(jnp.int32, SC_LANES)
        for j in range(n_chunks):
            for c in range(per_chunk):
                off = j * chunk + c * SC_LANES
                ent = ebase + off + lane
                code = code_v[pl.ds(off, SC_LANES)]
                d = plsc.load_gather(start_v, [code >> CODE_BITS]) + (code & ((1 << CODE_BITS) - 1))
                d = jnp.where(ent >= n_ent, trash + (ent - n_ent), d)
                tok = jnp.where(ent >= t_all, ent - t_all, ent)
                tok = jnp.where(tok >= t_all, tok - t_all, tok)
                dest_v[pl.ds(off, SC_LANES)] = d
                dst_v[j, pl.ds(c * SC_LANES, SC_LANES)] = d
                tok_v[j, pl.ds(c * SC_LANES, SC_LANES)] = tok
        pltpu.sync_copy(dest_v, dest_hbm.at[pl.ds(ebase, ent_w)])
        bufs, gsems, wsems = (buf0, buf1), (g0, g1), (w0, w1)

        def gather(j):
            return pltpu.make_async_copy(table_hbm.at[tok_v.at[j]], bufs[j % 2], gsems[j % 2])

        def scatter(j):
            return pltpu.make_async_copy(bufs[j % 2], out_hbm.at[dst_v.at[j]], wsems[j % 2])

        _sc_stream(n_chunks, gather, scatter)

    return pl.kernel(
        body,
        out_type=(jax.ShapeDtypeStruct((n_out, w), table.dtype), jax.ShapeDtypeStruct((n_pad,), jnp.int32)),
        mesh=_sc_mesh(),
        scratch_types=([pltpu.VMEM((ent_w,), jnp.int32), pltpu.VMEM((ent_w,), jnp.int32),
                        pltpu.VMEM((n_chunks, chunk), jnp.int32), pltpu.VMEM((n_chunks, chunk), jnp.int32),
                        pltpu.VMEM((LANES,), jnp.int32)] + _sc_buffers(chunk, w, table.dtype)),
        compiler_params=pltpu.CompilerParams(use_tc_tiling_on_sc=True, needs_layout_passes=False),
        name="sc_dispatch",
    )(table, codes, start_row)


def _experts_kernel(piece_start_ref, piece_row_ref, piece_cls_ref, *refs):
    n_w = 3 * W_SPLIT
    wg_refs, wu_refs, wd_refs = refs[:W_SPLIT], refs[W_SPLIT:2 * W_SPLIT], refs[2 * W_SPLIT:n_w]
    xs_hbm, ys_hbm, wg_bf, wu_bf, wd_bf, xbuf, ybuf, xsem, ysem = refs[n_w:]
    e = pl.program_id(0)
    g0 = piece_start_ref[e]
    n_here = piece_start_ref[e + 1] - g0
    n_total = piece_start_ref[N_EXPERTS]

    def per_class(g, fn):
        cls = piece_cls_ref[g]
        row = pl.multiple_of(piece_row_ref[g], EXP_UNIT)
        for c in range(1, EXP_CLASSES + 1):
            pl.when(cls == c)(lambda c=c: fn(c * EXP_UNIT, row))

    def x_copy(slot, rows, row):
        return pltpu.make_async_copy(xs_hbm.at[pl.ds(row, rows)], xbuf.at[slot, pl.ds(0, rows)], xsem.at[slot])

    def y_copy(slot, rows, row):
        return pltpu.make_async_copy(ybuf.at[slot, pl.ds(0, rows)], ys_hbm.at[pl.ds(row, rows)], ysem.at[slot])

    @pl.when((e == 0) & (n_total > 0))
    def _():
        per_class(0, lambda rows, row: x_copy(0, rows, row).start())

    for dst, chunks in ((wg_bf, wg_refs), (wu_bf, wu_refs), (wd_bf, wd_refs)):
        rows = dst.shape[0] // W_SPLIT
        for q, src in enumerate(chunks):
            dst[q * rows:(q + 1) * rows, :] = src[0, 0].astype(_BF16)

    def piece(j, carry):
        g = g0 + j
        slot = lax.rem(g, 2)
        per_class(g, lambda rows, row: x_copy(slot, rows, row).wait())

        @pl.when(g + 1 < n_total)
        def _():
            per_class(g + 1, lambda rows, row: x_copy(1 - slot, rows, row).start())

        @pl.when(g >= 2)
        def _():
            per_class(g - 2, lambda rows, row: y_copy(slot, rows, row).wait())

        def compute(rows, row):
            x = _unpack_bf16_pair(xbuf[slot, pl.ds(0, rows)]).astype(_BF16)
            a = _dot(x, wg_bf[...])
            u = _dot(x, wu_bf[...])
            h = (a * jax.nn.sigmoid(a) * u).astype(_BF16)
            ybuf[slot, pl.ds(0, rows)] = _pack_bf16_pair(_dot(h, wd_bf[...]))
            y_copy(slot, rows, row).start()

        per_class(g, compute)
        return carry

    lax.fori_loop(0, n_here, piece, 0)

    @pl.when(e == N_EXPERTS - 1)
    def _():
        @pl.when(n_total >= 2)
        def _():
            per_class(n_total - 2, lambda rows, row: y_copy(lax.rem(n_total, 2), rows, row).wait())

        @pl.when(n_total >= 1)
        def _():
            per_class(n_total - 1, lambda rows, row: y_copy(lax.rem(n_total - 1, 2), rows, row).wait())


def _experts(piece_start, piece_row, piece_cls, n_rows, xs, w_gate, w_up, w_down):
    dh = xs.shape[1]
    d = 2 * dh
    tm = EXP_UNIT * EXP_CLASSES
    anyspec = pl.BlockSpec(memory_space=pl.ANY)

    def chunk_specs(rows, cols):
        return [pl.BlockSpec((1, 1, rows // W_SPLIT, cols), lambda e, ps, pr, pc, q=q: (e, q, 0, 0))
                for q in range(W_SPLIT)]

    split = lambda w: w.reshape(w.shape[0], W_SPLIT, w.shape[1] // W_SPLIT, w.shape[2])
    grid_spec = pltpu.PrefetchScalarGridSpec(
        num_scalar_prefetch=3,
        grid=(N_EXPERTS,),
        in_specs=(chunk_specs(d, D_EXPERT) + chunk_specs(d, D_EXPERT) + chunk_specs(D_EXPERT, d) + [anyspec]),
        out_specs=anyspec,
        scratch_shapes=[pltpu.VMEM((d, D_EXPERT), _BF16), pltpu.VMEM((d, D_EXPERT), _BF16),
                        pltpu.VMEM((D_EXPERT, d), _BF16),
                        pltpu.VMEM((2, tm, dh), _U32), pltpu.VMEM((2, tm, dh), _U32),
                        pltpu.SemaphoreType.DMA((2,)), pltpu.SemaphoreType.DMA((2,))],
    )
    return pl.pallas_call(
        _experts_kernel,
        grid_spec=grid_spec,
        out_shape=jax.ShapeDtypeStruct((n_rows, dh), _U32),
        compiler_params=pltpu.CompilerParams(
            dimension_semantics=("arbitrary",), vmem_limit_bytes=VMEM_LIMIT),
        name="experts",
    )(piece_start, piece_row, piece_cls, *([split(w_gate)] * W_SPLIT), *([split(w_up)] * W_SPLIT),
      *([split(w_down)] * W_SPLIT), xs)


def _final_kernel(x1_ref, ya_ref, yb_ref, route_ref, gf_ref, y_ref):
    route = route_ref[...]
    x2 = (_unpack_bf16_pair(x1_ref[...]) + route[:, R_W1:R_W1 + 1] * _unpack_bf16_pair(ya_ref[...])
          + route[:, R_W2:R_W2 + 1] * _unpack_bf16_pair(yb_ref[...]))
    y_ref[...] = _rms(x2, gf_ref[...])


def _final(x1, yab, route, gf, n_prompt, n_sample):
    d = 2 * x1.shape[1]

    def call(tm, first_block, n_rows, name):
        tok = lambda w: pl.BlockSpec((tm, w), lambda i: (first_block + i, 0))
        sel = lambda k: pl.BlockSpec((None, tm, d // 2), lambda i: (k, first_block + i, 0))
        return pl.pallas_call(
            _final_kernel,
            grid=(n_rows // tm,),
            in_specs=[tok(d // 2), sel(0), sel(1), tok(LANES), pl.BlockSpec((1, d), lambda i: (0, 0))],
            out_specs=pl.BlockSpec((tm, d), lambda i: (i, 0)),
            out_shape=jax.ShapeDtypeStruct((n_rows, d), _F32),
            compiler_params=pltpu.CompilerParams(
                dimension_semantics=("arbitrary",), vmem_limit_bytes=VMEM_LIMIT),
            name=name,
        )(x1, yab, yab, route, gf)

    return (call(FINAL_TM, 0, n_prompt, "final_prompt"),
            call(n_sample, n_prompt // n_sample, n_sample, "final_sample"))


def _powers(lam_re, lam_im, dt):
    out = []
    for m in range(SSM_BLK + 1):
        mag = jnp.exp(m * lam_re * dt)
        ang = m * lam_im * dt
        out.append((mag * jnp.cos(ang), mag * jnp.sin(ang)))
    return out


def _spread(x, copies):
    w = x.shape[1]
    src = lax.broadcasted_iota(jnp.int32, (w, w * copies), 0)
    dst = lax.broadcasted_iota(jnp.int32, (w, w * copies), 1)
    return _dot_f32(x, jnp.where(dst % w == src, 1.0, 0.0))


def _ssm_prep_kernel(lam_ref, b_re, b_im, c_re, c_im, v_ref, r_ref, wb_ref, wc_ref, coef_ref):
    n_p, n_h = SSM_STATE, SSM_GROUP
    lr, li, dt = lam_ref[0:1, :], lam_ref[1:2, :], lam_ref[2:3, :]
    pw = _powers(lr, li, dt)
    den = lr * lr + li * li
    nr, ni = pw[1][0] - 1.0, pw[1][1]
    k_re = (nr * lr + ni * li) / den
    k_im = (ni * lr - nr * li) / den
    coef_ref[...] = jnp.concatenate(
        [pw[1][0], pw[1][1], pw[SSM_BLK][0], pw[SSM_BLK][1], jnp.zeros((SUBLANES - 4, TILE_STATE), _F32)], axis=0)

    on_diag_b = (lax.broadcasted_iota(jnp.int32, (TILE_STATE, LANES), 0) // n_p
                 == lax.broadcasted_iota(jnp.int32, (TILE_STATE, LANES), 1) // n_h)
    rows_gp = lambda ref: ref[...].reshape(TILE_STATE, n_h)
    bt_re = jnp.where(on_diag_b, _spread(rows_gp(b_re), SUBLANES), 0.0).T
    bt_im = jnp.where(on_diag_b, _spread(rows_gp(b_im), SUBLANES), 0.0).T
    bb_re = k_re * bt_re - k_im * bt_im
    bb_im = k_re * bt_im + k_im * bt_re
    wb_ref[0] = jnp.concatenate([bb_re, bb_im], axis=1)
    v_rows = []
    for s in range(SSM_BLK):
        pr, pi = pw[SSM_BLK - 1 - s]
        v_rows.append(jnp.concatenate([pr * bb_re - pi * bb_im, pr * bb_im + pi * bb_re], axis=1))
    v_ref[0] = jnp.concatenate(v_rows, axis=0).astype(v_ref.dtype)

    on_diag_c = (lax.broadcasted_iota(jnp.int32, (LANES, TILE_STATE), 0) // n_h
                 == lax.broadcasted_iota(jnp.int32, (LANES, TILE_STATE), 1) // n_p)
    rows_gh = lambda ref: ref[...].reshape(LANES, n_p)
    ct_re = jnp.where(on_diag_c, _spread(rows_gh(c_re), SUBLANES), 0.0)
    ct_im = jnp.where(on_diag_c, _spread(rows_gh(c_im), SUBLANES), 0.0)
    cl = [(ct_re * pr - ct_im * pi, ct_re * pi + ct_im * pr) for pr, pi in pw]
    wc_ref[0] = jnp.concatenate([cl[0][0], -cl[0][1]], axis=1).T
    nt = lambda a, b: lax.dot_general(a, b, (((1,), (1,)), ((), ())), precision=lax.Precision.HIGHEST,
                                      preferred_element_type=_F32)
    direct = [nt(cl[m][0], bb_re) - nt(cl[m][1], bb_im) for m in range(SSM_BLK)]
    zero = jnp.zeros((LANES, LANES), _F32)
    rt = jnp.concatenate(
        [jnp.concatenate([cl[i + 1][0], -cl[i + 1][1]]
                         + [direct[i - s] if s <= i else zero for s in range(SSM_BLK)], axis=1)
         for i in range(SSM_BLK)], axis=0)
    r_ref[0] = rt.T.astype(r_ref.dtype)


def _ssm_params(lam_re, lam_im, log_dt, b_re, b_im, c_re, c_im, d_skip):
    n_g, n_p, n_h = N_SSM_GROUPS, SSM_STATE, SSM_GROUP
    dt = jnp.repeat(jnp.exp(log_dt), n_p)
    lam = jnp.zeros((SUBLANES, STATE_COLS), _F32).at[0].set(lam_re.reshape(-1)).at[1].set(
        lam_im.reshape(-1)).at[2].set(dt)
    groups = lambda r, c: pl.BlockSpec((SUBLANES, r, c), lambda k: (k, 0, 0))
    out3 = lambda rows, w: pl.BlockSpec((1, rows, w), lambda k: (k, 0, 0))
    cols = pl.BlockSpec((SUBLANES, TILE_STATE), lambda k: (0, k))
    k_blk = SSM_BLK * LANES
    v, r, wb, wc, coef = pl.pallas_call(
        _ssm_prep_kernel,
        grid=(N_LANE_TILES,),
        in_specs=[cols, groups(n_p, n_h), groups(n_p, n_h), groups(n_h, n_p), groups(n_h, n_p)],
        out_specs=[out3(k_blk, 2 * TILE_STATE), out3(2 * TILE_STATE + k_blk, k_blk),
                   out3(LANES, 2 * TILE_STATE), out3(2 * TILE_STATE, LANES), cols],
        out_shape=[jax.ShapeDtypeStruct((N_LANE_TILES, k_blk, 2 * TILE_STATE), _BF16),
                   jax.ShapeDtypeStruct((N_LANE_TILES, 2 * TILE_STATE + k_blk, k_blk), _BF16),
                   jax.ShapeDtypeStruct((N_LANE_TILES, LANES, 2 * TILE_STATE), _F32),
                   jax.ShapeDtypeStruct((N_LANE_TILES, 2 * TILE_STATE, LANES), _F32),
                   jax.ShapeDtypeStruct((SUBLANES, STATE_COLS), _F32)],
        compiler_params=pltpu.CompilerParams(
            dimension_semantics=("arbitrary",), vmem_limit_bytes=VMEM_LIMIT),
        name="ssm_prep",
    )(lam, b_re, b_im, c_re, c_im)
    return wb, wc, v, r, coef, d_skip.reshape(1, D_SSM)


def _dispatch_plan(route_t, cnt):
    t_all = route_t.shape[1]
    codes = route_t[R_CODE1:R_CODE2 + 1].astype(jnp.int32).reshape(-1)
    per_pass = SC_WORKERS * DISPATCH_CHUNK
    codes = jnp.pad(codes, (0, -(2 * t_all) % per_pass))
    counts = cnt[:, 0].astype(jnp.int32)
    zero = jnp.zeros((1,), jnp.int32)
    units = (counts + EXP_UNIT - 1) // EXP_UNIT
    unit_start = jnp.concatenate([zero, jnp.cumsum(units)])
    start_row = jnp.zeros((LANES,), jnp.int32).at[:N_EXPERTS].set(unit_start[:N_EXPERTS] * EXP_UNIT)
    pieces = (units + EXP_CLASSES - 1) // EXP_CLASSES
    piece_start = jnp.concatenate([zero, jnp.cumsum(pieces)])
    tm = EXP_UNIT * EXP_CLASSES
    max_units = (2 * t_all + N_EXPERTS * (EXP_UNIT - 1)) // EXP_UNIT
    max_pieces = (max_units + N_EXPERTS * (EXP_CLASSES - 1)) // EXP_CLASSES
    g = jnp.arange(max_pieces, dtype=jnp.int32)
    owner = ((g[:, None] >= piece_start[None, :-1]) & (g[:, None] < piece_start[None, 1:])).astype(jnp.int32)
    pick = lambda table: jnp.sum(owner * table[None, :], axis=1)
    first_unit = pick(unit_start[:-1]) + (g - pick(piece_start[:-1])) * EXP_CLASSES
    piece_row = first_unit * EXP_UNIT
    piece_cls = jnp.clip(pick(unit_start[1:]) - first_unit, 1, EXP_CLASSES)
    n_rows = (max_units * EXP_UNIT + tm - 1) // tm * tm + tm
    return codes, start_row, n_rows, piece_start, piece_row, piece_cls


def kernel(x_prompt, x_sample, state_ssm_re, state_ssm_im, norm1_g, w_in, lam_re, lam_im, log_dt, ssm_b_re, ssm_b_im, ssm_c_re, ssm_c_im, ssm_d, gmlp_norm_g, gmlp_w_s, gmlp_b_s, out_norm_ssm_g, out_norm_gmlp_g, w_out, norm2_g, w_router_group, b_router_group, w_router_expert, b_router_expert, w_gate, w_up, w_down, final_norm_g):
    n, l, d = x_prompt.shape
    ns = x_sample.shape[0]
    t_all = n * l + ns
    li = 0
    g1 = norm1_g[li].reshape(1, d)
    gn = gmlp_norm_g[li].reshape(1, D_GMLP)
    tril = jnp.tril(jnp.ones((CHUNK, CHUNK), dtype=bool))
    ws_tril = jnp.where(tril[None], gmlp_w_s[li], 0.0)
    bs = gmlp_b_s[li]
    gog = out_norm_gmlp_g[li].reshape(1, D_GMLP)
    gos = out_norm_ssm_g[li].reshape(1, D_SSM)
    wb, wc, v_blk, r_blk, coef, dsk = _ssm_params(
        lam_re[li], lam_im[li], log_dt[li], ssm_b_re[li], ssm_b_im[li], ssm_c_re[li], ssm_c_im[li], ssm_d[li])
    g2 = norm2_g[li].reshape(1, d)
    pad = LANES - N_EXPERTS - N_EXPERT_GROUPS
    wr = jnp.concatenate([w_router_expert[li], w_router_group[li], jnp.zeros((d, pad), _F32)], axis=1)
    br = jnp.concatenate([b_router_expert[li], b_router_group[li], jnp.zeros((pad,), _F32)]).reshape(1, LANES)

    xa, sg, mixb = _front_prompt(x_prompt, g1, w_in[li], gn, ws_tril.astype(_BF16), bs.T, gog)
    mixa, hfin = _ssm_prompt(xa, sg, v_blk, r_blk, coef, dsk, gos)
    w00 = jnp.repeat(ws_tril[:, 0, 0], GMLP_HEAD).reshape(1, D_GMLP)
    b0 = jnp.repeat(bs[:, 0], GMLP_HEAD).reshape(1, D_GMLP)
    mix_s, hr_s, hi_s, vrow = _front_sample(
        x_sample.reshape(ns, d), g1, w_in[li], gn, w00, b0, gog, wb, wc, coef, dsk, gos,
        state_ssm_re[li].reshape(ns, STATE_COLS), state_ssm_im[li].reshape(ns, STATE_COLS))

    x1, xn, route, route_t, cnt = _mixer_out(x_prompt, mixa, mixb, x_sample.reshape(ns, d), mix_s,
                                             w_out[li], g2, wr, br)
    codes, start_row, n_rows, piece_start, piece_row, piece_cls = _dispatch_plan(route_t, cnt)
    xs, dest = _sc_dispatch(xn, codes, start_row, n_rows, DISPATCH_CHUNK)
    ys = _experts(piece_start, piece_row, piece_cls, n_rows, xs, w_gate[li], w_up[li], w_down[li])
    yab = _sc_combine(ys, dest, 2 * t_all, COMBINE_CHUNK).reshape(2, t_all, d // 2)
    y_p, y_s = _final(x1, yab, route, final_norm_g.reshape(1, d), n * l, ns)

    hf = hfin.reshape(n, N_LANE_TILES, 2, 8, SSM_STATE)
    re_p = hf[:, :, 0].reshape(1, n, N_SSM_GROUPS, SSM_STATE)
    im_p = hf[:, :, 1].reshape(1, n, N_SSM_GROUPS, SSM_STATE)
    re_s = hr_s.reshape(1, ns, N_SSM_GROUPS, SSM_STATE)
    im_s = hi_s.reshape(1, ns, N_SSM_GROUPS, SSM_STATE)
    return (y_p.reshape(n, l, d), y_s.reshape(ns, 1, d), re_p, im_p, re_s, im_s,
            vrow.reshape(1, ns, 1, D_GMLP))
```

```python
import math

import jax
import jax.numpy as jnp
from jax import lax
from jax.experimental import pallas as pl
from jax.experimental.pallas import tpu as pltpu
from jax.experimental.pallas import tpu_sc as plsc

D_MODEL = 1024
D_SSM = 512
D_GMLP = 512
SSM_GROUP = 16
N_SSM_GROUPS = 32
SSM_STATE = 64
CHUNK = 128
N_GMLP_HEADS = 4
GMLP_HEAD = 128
N_EXPERT_GROUPS = 4
EXPERTS_PER_GROUP = 8
N_EXPERTS = 32
D_EXPERT = 512
D_IN = 2048
EPS = 1e-6

LANES = 128
SUBLANES = 8
N_LANE_TILES = D_SSM // LANES
STATE_COLS = N_SSM_GROUPS * SSM_STATE
TILE_STATE = STATE_COLS // N_LANE_TILES
VMEM_LIMIT = 56 * 1024 * 1024

SC_CORES = 2
SC_SUBCORES = 16
SC_LANES = 16
SC_WORKERS = SC_CORES * SC_SUBCORES

FRONT_TL = 512
SSM_LC = 256
SSM_BLK = 4
COEF_LB_RE, COEF_LB_IM, COEF_LBLK_RE, COEF_LBLK_IM = 0, 1, 2, 3
TOK_TM = 512
FINAL_TM = 1024
EXP_UNIT = 128
EXP_CLASSES = 8
W_SPLIT = 4
DISPATCH_CHUNK = 80
COMBINE_CHUNK = 24

R_E1, R_E2, R_W1, R_W2, R_RANK1, R_RANK2, R_CODE1, R_CODE2 = 0, 1, 2, 3, 4, 5, 6, 7
CODE_BITS = 16
CODE_SHIFT = float(1 << CODE_BITS)

_INV_SQRT2 = 1.0 / math.sqrt(2.0)
_BF16 = jnp.bfloat16
_F32 = jnp.float32
_U32 = jnp.uint32


def _gelu(x):
    return 0.5 * x * (1.0 + lax.erf(x * _INV_SQRT2))


def _rms(x, g):
    return x * lax.rsqrt(jnp.mean(x * x, axis=-1, keepdims=True) + EPS) * g


def _dot(a, b):
    return jnp.dot(a, b, preferred_element_type=_F32)


def _dot_f32(a, b):
    return jnp.dot(a, b, preferred_element_type=_F32, precision=lax.Precision.HIGHEST)


def _pack_bf16_pair(x):
    w = x.shape[1] // 2
    hi = lax.bitcast_convert_type(x[:, :w].astype(_BF16).astype(_F32), _U32)
    lo = lax.bitcast_convert_type(x[:, w:].astype(_BF16).astype(_F32), _U32)
    return hi | (lo >> 16)


def _unpack_bf16_pair(p):
    hi = lax.bitcast_convert_type(p & jnp.uint32(0xFFFF0000), _F32)
    lo = lax.bitcast_convert_type(p << 16, _F32)
    return jnp.concatenate([hi, lo], axis=-1)


def _head_norm_gelu(vb, gn):
    v = _gelu(vb)
    parts = []
    for h in range(N_GMLP_HEADS):
        vh = v[:, h * GMLP_HEAD:(h + 1) * GMLP_HEAD]
        parts.append(vh * lax.rsqrt(jnp.mean(vh * vh, axis=-1, keepdims=True) + EPS))
    return jnp.concatenate(parts, axis=-1) * gn


def _front_prompt_kernel(x_ref, g1_ref, win_ref, gn_ref, ws_ref, bs_ref, gog_ref,
                         xa_ref, sg_ref, mixb_ref, win_bf, z_ref):
    @pl.when(pl.program_id(0) == 0)
    def _():
        win_bf[...] = win_ref[...].astype(_BF16)
        z_ref[...] = jnp.zeros_like(z_ref)

    z = z_ref[...]
    x = x_ref[0]
    hn = _rms(x, g1_ref[...]).astype(_BF16)
    z_ref[...] = _dot(hn, win_bf[...])
    xa_ref[0] = z[:, :D_SSM]
    sg_ref[0] = jax.nn.sigmoid(z[:, D_SSM:2 * D_SSM])
    ub = _gelu(z[:, 2 * D_SSM:2 * D_SSM + D_GMLP])
    vbn = _head_norm_gelu(z[:, 2 * D_SSM + D_GMLP:], gn_ref[...]).astype(_BF16)
    tl = x.shape[0]
    rows = []
    for c in range(tl // CHUNK):
        heads = []
        for h in range(N_GMLP_HEADS):
            vh = vbn[c * CHUNK:(c + 1) * CHUNK, h * GMLP_HEAD:(h + 1) * GMLP_HEAD]
            heads.append(_dot(ws_ref[h], vh) + bs_ref[:, h:h + 1])
        rows.append(jnp.concatenate(heads, axis=-1))
    s = jnp.concatenate(rows, axis=0)
    mixb_ref[0] = _rms(ub * s, gog_ref[...]).astype(_BF16)


def _front_prompt(x, g1, win, gn, ws_tril_bf, bs_t, gog):
    n, l, d = x.shape
    tl = FRONT_TL
    per_seq = l // tl
    n_tiles = n * per_seq
    cur = lambda i: jnp.minimum(i, n_tiles - 1)
    prev = lambda i: jnp.maximum(i - 1, 0)
    const = lambda *shape: pl.BlockSpec(shape, lambda i: (0,) * len(shape))
    seq = lambda w, which: pl.BlockSpec((1, tl, w), lambda i: (which(i) // per_seq, which(i) % per_seq, 0))
    return pl.pallas_call(
        _front_prompt_kernel,
        grid=(n_tiles + 1,),
        in_specs=[seq(d, cur), const(1, d), const(d, D_IN), const(1, D_GMLP),
                  const(N_GMLP_HEADS, CHUNK, CHUNK), const(CHUNK, N_GMLP_HEADS), const(1, D_GMLP)],
        out_specs=[seq(D_SSM, prev), seq(D_SSM, prev), seq(D_GMLP, prev)],
        out_shape=[jax.ShapeDtypeStruct((n, l, D_SSM), _F32),
                   jax.ShapeDtypeStruct((n, l, D_SSM), _F32),
                   jax.ShapeDtypeStruct((n, l, D_GMLP), _BF16)],
        scratch_shapes=[pltpu.VMEM((d, D_IN), _BF16), pltpu.VMEM((tl, D_IN), _F32)],
        compiler_params=pltpu.CompilerParams(
            dimension_semantics=("arbitrary",), vmem_limit_bytes=VMEM_LIMIT),
        name="front_prompt",
    )(x, g1, win, gn, ws_tril_bf, bs_t, gog)


def _ssm_prompt_kernel(xa_ref, sg_ref, v_ref, r_ref, coef_ref, dsk_ref, gos_ref,
                       mixa_ref, hfin_ref, s_ref, st_ref):
    lc = xa_ref.shape[1]
    nblk = lc // SSM_BLK
    rows = nblk * SUBLANES

    @pl.when(pl.program_id(0) == 0)
    def _():
        st_ref[...] = jnp.zeros_like(st_ref)

    def by_position(ref):
        t = pltpu.einshape("btc->tbc", ref[...]).reshape(nblk, SSM_BLK, SUBLANES, D_SSM)
        return [t[:, i].reshape(rows, D_SSM) for i in range(SSM_BLK)]

    xs = by_position(xa_ref)
    xs_bf = [x.astype(_BF16) for x in xs]
    xk = [jnp.concatenate([x[:, k * LANES:(k + 1) * LANES] for x in xs_bf], axis=-1)
          for k in range(N_LANE_TILES)]
    for k in range(N_LANE_TILES):
        s_ref[:, 2 * TILE_STATE * k:2 * TILE_STATE * (k + 1)] = _dot(xk[k], v_ref[k])

    for kk in range(0, N_LANE_TILES, 2):
        tiles = (kk, kk + 1)
        cols = [(2 * TILE_STATE * k, 2 * TILE_STATE * k + TILE_STATE) for k in tiles]
        lbs = [tuple(jnp.broadcast_to(coef_ref[row:row + 1, k * TILE_STATE:(k + 1) * TILE_STATE],
                                      (SUBLANES, TILE_STATE)) for row in (COEF_LBLK_RE, COEF_LBLK_IM))
               for k in tiles]

        def body(j, carry, cols=cols, lbs=lbs):
            r0 = pl.multiple_of(j * SUBLANES, SUBLANES)
            out = []
            for q, ((c_re, c_im), (lr, li)) in enumerate(zip(cols, lbs)):
                hr, hi = carry[2 * q], carry[2 * q + 1]
                sr = s_ref[pl.ds(r0, SUBLANES), c_re:c_re + TILE_STATE]
                si = s_ref[pl.ds(r0, SUBLANES), c_im:c_im + TILE_STATE]
                s_ref[pl.ds(r0, SUBLANES), c_re:c_re + TILE_STATE] = hr
                s_ref[pl.ds(r0, SUBLANES), c_im:c_im + TILE_STATE] = hi
                out += [lr * hr - li * hi + sr, lr * hi + li * hr + si]
            return tuple(out)

        init = tuple(st_ref[:, c:c + TILE_STATE] for c_pair in cols for c in c_pair)
        fin = lax.fori_loop(0, nblk, body, init, unroll=2)
        for q, (c_re, c_im) in enumerate(cols):
            st_ref[:, c_re:c_re + TILE_STATE] = fin[2 * q]
            st_ref[:, c_im:c_im + TILE_STATE] = fin[2 * q + 1]

    yk = []
    for k in range(N_LANE_TILES):
        h_in = s_ref[:, 2 * TILE_STATE * k:2 * TILE_STATE * (k + 1)].astype(_BF16)
        yk.append(_dot(jnp.concatenate([h_in, xk[k]], axis=-1), r_ref[k]))
    sgs = by_position(sg_ref)
    outs = []
    for i in range(SSM_BLK):
        y = jnp.concatenate([y_k[:, i * LANES:(i + 1) * LANES] for y_k in yk], axis=-1) + dsk_ref[...] * xs[i]
        outs.append(_rms(_gelu(y) * sgs[i], gos_ref[...]).reshape(nblk, SUBLANES, D_SSM))
    mixa = jnp.stack(outs, axis=1).reshape(lc, SUBLANES, D_SSM)
    mixa_ref[...] = pltpu.einshape("tbc->btc", mixa).astype(_BF16)
    hfin_ref[...] = st_ref[...]


def _ssm_prompt(xa, sg, v, r, coef, dsk, gos):
    n, l, _ = xa.shape
    lc = SSM_LC
    const = lambda *shape: pl.BlockSpec(shape, lambda i: (0,) * len(shape))
    seq_spec = pl.BlockSpec((n, lc, D_SSM), lambda i: (0, i, 0))
    return pl.pallas_call(
        _ssm_prompt_kernel,
        grid=(l // lc,),
        in_specs=[seq_spec, seq_spec, const(*v.shape), const(*r.shape),
                  const(*coef.shape), const(1, D_SSM), const(1, D_SSM)],
        out_specs=[seq_spec, const(n, 2 * STATE_COLS)],
        out_shape=[jax.ShapeDtypeStruct((n, l, D_SSM), _BF16),
                   jax.ShapeDtypeStruct((n, 2 * STATE_COLS), _F32)],
        scratch_shapes=[pltpu.VMEM((lc // SSM_BLK * n, 2 * STATE_COLS), _F32),
                        pltpu.VMEM((n, 2 * STATE_COLS), _F32)],
        compiler_params=pltpu.CompilerParams(
            dimension_semantics=("arbitrary",), vmem_limit_bytes=VMEM_LIMIT),
        name="ssm_prompt",
    )(xa, sg, v, r, coef, dsk, gos)


def _front_sample_kernel(x_ref, g1_ref, win_ref, gn_ref, w00_ref, b0_ref, gog_ref,
                         wb_ref, wc_ref, coef_ref, dsk_ref, gos_ref, h0r_ref, h0i_ref,
                         mix_ref, hr_ref, hi_ref, vrow_ref):
    x = x_ref[...]
    hn = _rms(x, g1_ref[...])
    z = _dot_f32(hn, win_ref[...])
    xa = z[:, :D_SSM]
    ys = []
    for k in range(N_LANE_TILES):
        bu = _dot_f32(xa[:, k * LANES:(k + 1) * LANES], wb_ref[k])
        sl = slice(k * TILE_STATE, (k + 1) * TILE_STATE)
        lr, li = coef_ref[COEF_LB_RE:COEF_LB_RE + 1, sl], coef_ref[COEF_LB_IM:COEF_LB_IM + 1, sl]
        h0r, h0i = h0r_ref[:, sl], h0i_ref[:, sl]
        nr = lr * h0r - li * h0i + bu[:, :TILE_STATE]
        ni = lr * h0i + li * h0r + bu[:, TILE_STATE:]
        hr_ref[:, sl] = nr
        hi_ref[:, sl] = ni
        ys.append(_dot_f32(jnp.concatenate([nr, ni], axis=-1), wc_ref[k]))
    y = jnp.concatenate(ys, axis=-1) + dsk_ref[...] * xa
    ya = _gelu(y) * jax.nn.sigmoid(z[:, D_SSM:2 * D_SSM])
    mix_ref[:, :D_SSM] = _rms(ya, gos_ref[...])
    ub = _gelu(z[:, 2 * D_SSM:2 * D_SSM + D_GMLP])
    vbn = _head_norm_gelu(z[:, 2 * D_SSM + D_GMLP:], gn_ref[...])
    vrow_ref[...] = vbn
    s = w00_ref[...] * vbn + b0_ref[...]
    mix_ref[:, D_SSM:] = _rms(ub * s, gog_ref[...])


def _front_sample(x, g1, win, gn, w00, b0, gog, wb, wc, coef, dsk, gos, h0r, h0i):
    n = x.shape[0]
    vmem = pl.BlockSpec(memory_space=pltpu.VMEM)
    return pl.pallas_call(
        _front_sample_kernel,
        in_specs=[vmem] * 14,
        out_specs=[vmem] * 4,
        out_shape=[jax.ShapeDtypeStruct((n, D_MODEL), _F32),
                   jax.ShapeDtypeStruct((n, STATE_COLS), _F32),
                   jax.ShapeDtypeStruct((n, STATE_COLS), _F32),
                   jax.ShapeDtypeStruct((n, D_GMLP), _F32)],
        compiler_params=pltpu.CompilerParams(vmem_limit_bytes=VMEM_LIMIT),
        name="front_sample",
    )(x, g1, win, gn, w00, b0, gog, wb, wc, coef, dsk, gos, h0r, h0i)


def _route(logits, base):
    tm = logits.shape[0]
    lt = logits.T
    ex = lt[:N_EXPERTS, :]
    gr = lt[N_EXPERTS:N_EXPERTS + SUBLANES, :]
    row_e = lax.broadcasted_iota(jnp.int32, ex.shape, 0).astype(_F32)
    row_g = lax.broadcasted_iota(jnp.int32, gr.shape, 0).astype(_F32)
    neg = jnp.float32(-jnp.inf)
    big = jnp.float32(LANES)
    is_g = row_g < N_EXPERT_GROUPS
    gl = jnp.where(is_g, gr, neg)
    gmax = jnp.max(gl, axis=0, keepdims=True)
    gi = jnp.min(jnp.where(gl == gmax, row_g, big), axis=0, keepdims=True)
    p_top = 1.0 / jnp.sum(jnp.where(is_g, jnp.exp(gl - gmax), 0.0), axis=0, keepdims=True)
    lo = gi * EXPERTS_PER_GROUP
    in_grp = (row_e >= lo) & (row_e < lo + EXPERTS_PER_GROUP)
    m1 = jnp.max(jnp.where(in_grp, ex, neg), axis=0, keepdims=True)
    i1 = jnp.min(jnp.where(in_grp & (ex == m1), row_e, big), axis=0, keepdims=True)
    rest = in_grp & (row_e != i1)
    m2 = jnp.max(jnp.where(rest, ex, neg), axis=0, keepdims=True)
    i2 = jnp.min(jnp.where(rest & (ex == m2), row_e, big), axis=0, keepdims=True)
    e2 = jnp.exp(m2 - m1)
    w1 = p_top / (1.0 + e2)
    w2 = p_top * e2 / (1.0 + e2)
    sel1 = row_e == i1
    sel2 = row_e == i2
    hits = jnp.where(sel1 | sel2, 1.0, 0.0)
    src = lax.broadcasted_iota(jnp.int32, (tm, tm), 0)
    dst = lax.broadcasted_iota(jnp.int32, (tm, tm), 1)
    before = _dot(hits.astype(_BF16), jnp.where(src < dst, 1.0, 0.0).astype(_BF16)) + base
    rank1 = jnp.sum(jnp.where(sel1, before, 0.0), axis=0, keepdims=True)
    rank2 = jnp.sum(jnp.where(sel2, before, 0.0), axis=0, keepdims=True)
    fields = {R_E1: i1, R_E2: i2, R_W1: w1, R_W2: w2, R_RANK1: rank1, R_RANK2: rank2,
              R_CODE1: i1 * CODE_SHIFT + rank1, R_CODE2: i2 * CODE_SHIFT + rank2}
    row8 = lax.broadcasted_iota(jnp.int32, (SUBLANES, tm), 0)
    route_t = jnp.zeros((SUBLANES, tm), _F32)
    for r, val in fields.items():
        route_t = jnp.where(row8 == r, val, route_t)
    route = jnp.concatenate([route_t, jnp.zeros((LANES - SUBLANES, tm), _F32)], axis=0).T
    return route_t, route, base + jnp.sum(hits, axis=1, keepdims=True)


def _mixer_out_prompt_kernel(x_ref, mixa_ref, mixb_ref, wo_ref, g2_ref, wr_ref, br_ref,
                             x1_ref, xn_ref, route_ref, route_t_ref, cnt_ref, base_ref, logits_ref, wo_bf):
    i = pl.program_id(0)

    @pl.when(i == 0)
    def _():
        base_ref[...] = jnp.zeros_like(base_ref)
        logits_ref[...] = jnp.zeros_like(logits_ref)
        wo_bf[...] = wo_ref[...].astype(_BF16)

    prev_logits = logits_ref[...]
    x1 = x_ref[0] + _dot(mixa_ref[0], wo_bf[:D_SSM, :]) + _dot(mixb_ref[0], wo_bf[D_SSM:, :])
    xn = _rms(x1, g2_ref[...])
    x1_ref[...] = _pack_bf16_pair(x1)
    xn_ref[...] = _pack_bf16_pair(xn)
    logits_ref[...] = _dot(xn.astype(_BF16), wr_ref[...]) + br_ref[...]
    route_t, route, base = _route(prev_logits, base_ref[...])
    route_ref[...] = route
    route_t_ref[...] = route_t
    base = jnp.where(i >= 1, base, base_ref[...])
    base_ref[...] = base
    cnt_ref[...] = base


def _mixer_out_sample_kernel(x_ref, mix_ref, wo_ref, g2_ref, wr_ref, br_ref, cnt_in_ref,
                             x1_in, xn_in, route_in, route_t_in,
                             x1_ref, xn_ref, route_ref, route_t_ref, cnt_ref):
    del x1_in, xn_in, route_in, route_t_in
    x1 = (x_ref[...] + _dot_f32(mix_ref[:, :D_SSM], wo_ref[:D_SSM, :])
          + _dot_f32(mix_ref[:, D_SSM:], wo_ref[D_SSM:, :]))
    xn = _rms(x1, g2_ref[...])
    logits = _dot_f32(xn, wr_ref[...]) + br_ref[...]
    route_t, route, base = _route(logits, cnt_in_ref[...])
    x1_ref[...] = _pack_bf16_pair(x1)
    xn_ref[...] = _pack_bf16_pair(xn)
    route_ref[...] = route
    route_t_ref[...] = route_t
    cnt_ref[...] = base


def _mixer_out(x_p, mixa, mixb, x_s, mix_s, wo, g2, wr, br):
    n, l, d = x_p.shape
    ns = x_s.shape[0]
    t_all = n * l + ns
    tm = TOK_TM
    per_seq = l // tm
    n_tiles = n * per_seq
    cur = lambda i: jnp.minimum(i, n_tiles - 1)
    prev = lambda i: jnp.maximum(i - 1, 0)
    const = lambda *shape: pl.BlockSpec(shape, lambda i: (0,) * len(shape))
    seq = lambda w: pl.BlockSpec((1, tm, w), lambda i: (cur(i) // per_seq, cur(i) % per_seq, 0))
    tok = lambda w, which: pl.BlockSpec((tm, w), lambda i: (which(i), 0))
    tok_shapes = [jax.ShapeDtypeStruct((t_all, d // 2), _U32),
                  jax.ShapeDtypeStruct((t_all, d // 2), _U32),
                  jax.ShapeDtypeStruct((t_all, LANES), _F32),
                  jax.ShapeDtypeStruct((SUBLANES, t_all), _F32)]
    cnt_shape = jax.ShapeDtypeStruct((N_EXPERTS, 1), _F32)
    x1, xn, route, route_t, cnt = pl.pallas_call(
        _mixer_out_prompt_kernel,
        grid=(n_tiles + 1,),
        in_specs=[seq(d), seq(D_SSM), seq(D_GMLP),
                  const(d, d), const(1, d), const(d, LANES), const(1, LANES)],
        out_specs=[tok(d // 2, cur), tok(d // 2, cur), tok(LANES, prev),
                   pl.BlockSpec((SUBLANES, tm), lambda i: (0, prev(i))), const(N_EXPERTS, 1)],
        out_shape=tok_shapes + [cnt_shape],
        scratch_shapes=[pltpu.VMEM((N_EXPERTS, 1), _F32), pltpu.VMEM((tm, LANES), _F32),
                        pltpu.VMEM((d, d), _BF16)],
        compiler_params=pltpu.CompilerParams(
            dimension_semantics=("arbitrary",), vmem_limit_bytes=VMEM_LIMIT),
        name="mixer_out_prompt",
    )(x_p, mixa, mixb, wo, g2, wr.astype(_BF16), br)
    tail = (n * l) // ns
    c1 = lambda *shape: pl.BlockSpec(shape, lambda i: (0,) * len(shape))
    anyspec = pl.BlockSpec(memory_space=pl.ANY)
    tail_spec = lambda w: pl.BlockSpec((ns, w), lambda i: (tail, 0))
    return pl.pallas_call(
        _mixer_out_sample_kernel,
        grid=(1,),
        in_specs=[c1(ns, d), c1(ns, d), c1(d, d), c1(1, d), c1(d, LANES), c1(1, LANES), c1(N_EXPERTS, 1),
                  anyspec, anyspec, anyspec, anyspec],
        out_specs=[tail_spec(d // 2), tail_spec(d // 2), tail_spec(LANES),
                   pl.BlockSpec((SUBLANES, ns), lambda i: (0, tail)), c1(N_EXPERTS, 1)],
        out_shape=tok_shapes + [cnt_shape],
        input_output_aliases={7: 0, 8: 1, 9: 2, 10: 3},
        compiler_params=pltpu.CompilerParams(
            dimension_semantics=("arbitrary",), vmem_limit_bytes=VMEM_LIMIT),
        name="mixer_out_sample",
    )(x_s, mix_s, wo, g2, wr, br, cnt, x1, xn, route, route_t)


def _sc_stream(n_chunks, gather, write):
    gather(0).start()
    for j in range(n_chunks):
        if j + 1 < n_chunks:
            if j >= 1:
                write(j - 1).wait()
            gather(j + 1).start()
        gather(j).wait()
        write(j).start()
    if n_chunks >= 2:
        write(n_chunks - 2).wait()
    write(n_chunks - 1).wait()


def _sc_mesh():
    return plsc.VectorSubcoreMesh(core_axis_name="c", subcore_axis_name="s",
                                  num_cores=SC_CORES, num_subcores=SC_SUBCORES)


def _sc_buffers(chunk, w, dtype):
    return [pltpu.VMEM((chunk, w), dtype), pltpu.VMEM((chunk, w), dtype)] + [pltpu.SemaphoreType.DMA] * 4


def _sc_combine(table, idx, n_out, chunk):
    w = table.shape[1]
    rows_w = n_out // SC_WORKERS
    n_chunks = rows_w // chunk
    assert rows_w * SC_WORKERS == n_out and n_chunks * chunk == rows_w and rows_w % SUBLANES == 0

    def body(table_hbm, idx_hbm, out_hbm, idx_v, buf0, buf1, g0, g1, w0, w1):
        wid = lax.axis_index("s") * SC_CORES + lax.axis_index("c")
        base = pl.multiple_of(wid * rows_w, SUBLANES)
        pltpu.sync_copy(idx_hbm.at[pl.ds(base, rows_w)], idx_v)
        bufs, gsems, wsems = (buf0, buf1), (g0, g1), (w0, w1)

        def gather(j):
            return pltpu.make_async_copy(table_hbm.at[idx_v.at[pl.ds(j * chunk, chunk)]], bufs[j % 2], gsems[j % 2])

        def write(j):
            return pltpu.make_async_copy(bufs[j % 2], out_hbm.at[pl.ds(base + j * chunk, chunk)], wsems[j % 2])

        _sc_stream(n_chunks, gather, write)

    return pl.kernel(
        body,
        out_type=jax.ShapeDtypeStruct((n_out, w), table.dtype),
        mesh=_sc_mesh(),
        scratch_types=[pltpu.VMEM((rows_w,), jnp.int32)] + _sc_buffers(chunk, w, table.dtype),
        compiler_params=pltpu.CompilerParams(use_tc_tiling_on_sc=True),
        name="sc_combine",
    )(table, idx)


def _sc_dispatch(table, codes, start_row, n_out, chunk):
    t_all, w = table.shape
    n_pad = codes.shape[0]
    n_ent = 2 * t_all
    ent_w = n_pad // SC_WORKERS
    n_chunks = ent_w // chunk
    per_chunk = chunk // SC_LANES
    trash = n_out - (n_pad - n_ent)
    assert ent_w * SC_WORKERS == n_pad and n_chunks * chunk == ent_w
    assert per_chunk * SC_LANES == chunk and chunk <= LANES and n_pad - n_ent <= t_all

    def body(table_hbm, code_hbm, start_hbm, out_hbm, dest_hbm,
             code_v, dest_v, tok_v, dst_v, start_v, buf0, buf1, g0, g1, w0, w1):
        wid = lax.axis_index("s") * SC_CORES + lax.axis_index("c")
        ebase = pl.multiple_of(wid * ent_w, SUBLANES)
        pltpu.sync_copy(code_hbm.at[pl.ds(ebase, ent_w)], code_v)
        pltpu.sync_copy(start_hbm, start_v)
        lane = lax.iota(jnp.int32, SC_LANES)
        for j in range(n_chunks):
            for c in range(per_chunk):
                off = j * chunk + c * SC_LANES
                ent = ebase + off + lane
                code = code_v[pl.ds(off, SC_LANES)]
                d = plsc.load_gather(start_v, [code >> CODE_BITS]) + (code & ((1 << CODE_BITS) - 1))
                d = jnp.where(ent >= n_ent, trash + (ent - n_ent), d)
                tok = jnp.where(ent >= t_all, ent - t_all, ent)
                tok = jnp.where(tok >= t_all, tok - t_all, tok)
                dest_v[pl.ds(off, SC_LANES)] = d
                dst_v[j, pl.ds(c * SC_LANES, SC_LANES)] = d
                tok_v[j, pl.ds(c * SC_LANES, SC_LANES)] = tok
        pltpu.sync_copy(dest_v, dest_hbm.at[pl.ds(ebase, ent_w)])
        bufs, gsems, wsems = (buf0, buf1), (g0, g1), (w0, w1)

        def gather(j):
            return pltpu.make_async_copy(table_hbm.at[tok_v.at[j]], bufs[j % 2], gsems[j % 2])

        def scatter(j):
            return pltpu.make_async_copy(bufs[j % 2], out_hbm.at[dst_v.at[j]], wsems[j % 2])

        _sc_stream(n_chunks, gather, scatter)

    return pl.kernel(
        body,
        out_type=(jax.ShapeDtypeStruct((n_out, w), table.dtype), jax.ShapeDtypeStruct((n_pad,), jnp.int32)),
        mesh=_sc_mesh(),
        scratch_types=([pltpu.VMEM((ent_w,), jnp.int32), pltpu.VMEM((ent_w,), jnp.int32),
                        pltpu.VMEM((n_chunks, chunk), jnp.int32), pltpu.VMEM((n_chunks, chunk), jnp.int32),
                        pltpu.VMEM((LANES,), jnp.int32)] + _sc_buffers(chunk, w, table.dtype)),
        compiler_params=pltpu.CompilerParams(use_tc_tiling_on_sc=True, needs_layout_passes=False),
        name="sc_dispatch",
    )(table, codes, start_row)


def _experts_kernel(piece_start_ref, piece_row_ref, piece_cls_ref, *refs):
    n_w = 3 * W_SPLIT
    wg_refs, wu_refs, wd_refs = refs[:W_SPLIT], refs[W_SPLIT:2 * W_SPLIT], refs[2 * W_SPLIT:n_w]
    xs_hbm, ys_hbm, wg_bf, wu_bf, wd_bf, xbuf, ybuf, xsem, ysem = refs[n_w:]
    e = pl.program_id(0)
    g0 = piece_start_ref[e]
    n_here = piece_start_ref[e + 1] - g0
    n_total = piece_start_ref[N_EXPERTS]

    def per_class(g, fn):
        cls = piece_cls_ref[g]
        row = pl.multiple_of(piece_row_ref[g], EXP_UNIT)
        for c in range(1, EXP_CLASSES + 1):
            pl.when(cls == c)(lambda c=c: fn(c * EXP_UNIT, row))

    def x_copy(slot, rows, row):
        return pltpu.make_async_copy(xs_hbm.at[pl.ds(row, rows)], xbuf.at[slot, pl.ds(0, rows)], xsem.at[slot])

    def y_copy(slot, rows, row):
        return pltpu.make_async_copy(ybuf.at[slot, pl.ds(0, rows)], ys_hbm.at[pl.ds(row, rows)], ysem.at[slot])

    @pl.when((e == 0) & (n_total > 0))
    def _():
        per_class(0, lambda rows, row: x_copy(0, rows, row).start())

    for dst, chunks in ((wg_bf, wg_refs), (wu_bf, wu_refs), (wd_bf, wd_refs)):
        rows = dst.shape[0] // W_SPLIT
        for q, src in enumerate(chunks):
            dst[q * rows:(q + 1) * rows, :] = src[0, 0].astype(_BF16)

    def piece(j, carry):
        g = g0 + j
        slot = lax.rem(g, 2)
        per_class(g, lambda rows, row: x_copy(slot, rows, row).wait())

        @pl.when(g + 1 < n_total)
        def _():
            per_class(g + 1, lambda rows, row: x_copy(1 - slot, rows, row).start())

        @pl.when(g >= 2)
        def _():
            per_class(g - 2, lambda rows, row: y_copy(slot, rows, row).wait())

        def compute(rows, row):
            x = _unpack_bf16_pair(xbuf[slot, pl.ds(0, rows)]).astype(_BF16)
            a = _dot(x, wg_bf[...])
            u = _dot(x, wu_bf[...])
            h = (a * jax.nn.sigmoid(a) * u).astype(_BF16)
            ybuf[slot, pl.ds(0, rows)] = _pack_bf16_pair(_dot(h, wd_bf[...]))
            y_copy(slot, rows, row).start()

        per_class(g, compute)
        return carry

    lax.fori_loop(0, n_here, piece, 0)

    @pl.when(e == N_EXPERTS - 1)
    def _():
        @pl.when(n_total >= 2)
        def _():
            per_class(n_total - 2, lambda rows, row: y_copy(lax.rem(n_total, 2), rows, row).wait())

        @pl.when(n_total >= 1)
        def _():
            per_class(n_total - 1, lambda rows, row: y_copy(lax.rem(n_total - 1, 2), rows, row).wait())


def _experts(piece_start, piece_row, piece_cls, n_rows, xs, w_gate, w_up, w_down):
    dh = xs.shape[1]
    d = 2 * dh
    tm = EXP_UNIT * EXP_CLASSES
    anyspec = pl.BlockSpec(memory_space=pl.ANY)

    def chunk_specs(rows, cols):
        return [pl.BlockSpec((1, 1, rows // W_SPLIT, cols), lambda e, ps, pr, pc, q=q: (e, q, 0, 0))
                for q in range(W_SPLIT)]

    split = lambda w: w.reshape(w.shape[0], W_SPLIT, w.shape[1] // W_SPLIT, w.shape[2])
    grid_spec = pltpu.PrefetchScalarGridSpec(
        num_scalar_prefetch=3,
        grid=(N_EXPERTS,),
        in_specs=(chunk_specs(d, D_EXPERT) + chunk_specs(d, D_EXPERT) + chunk_specs(D_EXPERT, d) + [anyspec]),
        out_specs=anyspec,
        scratch_shapes=[pltpu.VMEM((d, D_EXPERT), _BF16), pltpu.VMEM((d, D_EXPERT), _BF16),
                        pltpu.VMEM((D_EXPERT, d), _BF16),
                        pltpu.VMEM((2, tm, dh), _U32), pltpu.VMEM((2, tm, dh), _U32),
                        pltpu.SemaphoreType.DMA((2,)), pltpu.SemaphoreType.DMA((2,))],
    )
    return pl.pallas_call(
        _experts_kernel,
        grid_spec=grid_spec,
        out_shape=jax.ShapeDtypeStruct((n_rows, dh), _U32),
        compiler_params=pltpu.CompilerParams(
            dimension_semantics=("arbitrary",), vmem_limit_bytes=VMEM_LIMIT),
        name="experts",
    )(piece_start, piece_row, piece_cls, *([split(w_gate)] * W_SPLIT), *([split(w_up)] * W_SPLIT),
      *([split(w_down)] * W_SPLIT), xs)


def _final_kernel(x1_ref, ya_ref, yb_ref, route_ref, gf_ref, y_ref):
    route = route_ref[...]
    x2 = (_unpack_bf16_pair(x1_ref[...]) + route[:, R_W1:R_W1 + 1] * _unpack_bf16_pair(ya_ref[...])
          + route[:, R_W2:R_W2 + 1] * _unpack_bf16_pair(yb_ref[...]))
    y_ref[...] = _rms(x2, gf_ref[...])


def _final(x1, yab, route, gf, n_prompt, n_sample):
    d = 2 * x1.shape[1]

    def call(tm, first_block, n_rows, name):
        tok = lambda w: pl.BlockSpec((tm, w), lambda i: (first_block + i, 0))
        sel = lambda k: pl.BlockSpec((None, tm, d // 2), lambda i: (k, first_block + i, 0))
        return pl.pallas_call(
            _final_kernel,
            grid=(n_rows // tm,),
            in_specs=[tok(d // 2), sel(0), sel(1), tok(LANES), pl.BlockSpec((1, d), lambda i: (0, 0))],
            out_specs=pl.BlockSpec((tm, d), lambda i: (i, 0)),
            out_shape=jax.ShapeDtypeStruct((n_rows, d), _F32),
            compiler_params=pltpu.CompilerParams(
                dimension_semantics=("arbitrary",), vmem_limit_bytes=VMEM_LIMIT),
            name=name,
        )(x1, yab, yab, route, gf)

    return (call(FINAL_TM, 0, n_prompt, "final_prompt"),
            call(n_sample, n_prompt // n_sample, n_sample, "final_sample"))


def _powers(lam_re, lam_im, dt):
    out = []
    for m in range(SSM_BLK + 1):
        mag = jnp.exp(m * lam_re * dt)
        ang = m * lam_im * dt
        out.append((mag * jnp.cos(ang), mag * jnp.sin(ang)))
    return out


def _spread(x, copies):
    w = x.shape[1]
    src = lax.broadcasted_iota(jnp.int32, (w, w * copies), 0)
    dst = lax.broadcasted_iota(jnp.int32, (w, w * copies), 1)
    return _dot_f32(x, jnp.where(dst % w == src, 1.0, 0.0))


def _ssm_prep_kernel(lam_ref, b_re, b_im, c_re, c_im, v_ref, r_ref, wb_ref, wc_ref, coef_ref):
    n_p, n_h = SSM_STATE, SSM_GROUP
    lr, li, dt = lam_ref[0:1, :], lam_ref[1:2, :], lam_ref[2:3, :]
    pw = _powers(lr, li, dt)
    den = lr * lr + li * li
    nr, ni = pw[1][0] - 1.0, pw[1][1]
    k_re = (nr * lr + ni * li) / den
    k_im = (ni * lr - nr * li) / den
    coef_ref[...] = jnp.concatenate(
        [pw[1][0], pw[1][1], pw[SSM_BLK][0], pw[SSM_BLK][1], jnp.zeros((SUBLANES - 4, TILE_STATE), _F32)], axis=0)

    on_diag_b = (lax.broadcasted_iota(jnp.int32, (TILE_STATE, LANES), 0) // n_p
                 == lax.broadcasted_iota(jnp.int32, (TILE_STATE, LANES), 1) // n_h)
    rows_gp = lambda ref: ref[...].reshape(TILE_STATE, n_h)
    bt_re = jnp.where(on_diag_b, _spread(rows_gp(b_re), SUBLANES), 0.0).T
    bt_im = jnp.where(on_diag_b, _spread(rows_gp(b_im), SUBLANES), 0.0).T
    bb_re = k_re * bt_re - k_im * bt_im
    bb_im = k_re * bt_im + k_im * bt_re
    wb_ref[0] = jnp.concatenate([bb_re, bb_im], axis=1)
    v_rows = []
    for s in range(SSM_BLK):
        pr, pi = pw[SSM_BLK - 1 - s]
        v_rows.append(jnp.concatenate([pr * bb_re - pi * bb_im, pr * bb_im + pi * bb_re], axis=1))
    v_ref[0] = jnp.concatenate(v_rows, axis=0).astype(v_ref.dtype)

    on_diag_c = (lax.broadcasted_iota(jnp.int32, (LANES, TILE_STATE), 0) // n_h
                 == lax.broadcasted_iota(jnp.int32, (LANES, TILE_STATE), 1) // n_p)
    rows_gh = lambda ref: ref[...].reshape(LANES, n_p)
    ct_re = jnp.where(on_diag_c, _spread(rows_gh(c_re), SUBLANES), 0.0)
    ct_im = jnp.where(on_diag_c, _spread(rows_gh(c_im), SUBLANES), 0.0)
    cl = [(ct_re * pr - ct_im * pi, ct_re * pi + ct_im * pr) for pr, pi in pw]
    wc_ref[0] = jnp.concatenate([cl[0][0], -cl[0][1]], axis=1).T
    nt = lambda a, b: lax.dot_general(a, b, (((1,), (1,)), ((), ())), precision=lax.Precision.HIGHEST,
                                      preferred_element_type=_F32)
    direct = [nt(cl[m][0], bb_re) - nt(cl[m][1], bb_im) for m in range(SSM_BLK)]
    zero = jnp.zeros((LANES, LANES), _F32)
    rt = jnp.concatenate(
        [jnp.concatenate([cl[i + 1][0], -cl[i + 1][1]]
                         + [direct[i - s] if s <= i else zero for s in range(SSM_BLK)], axis=1)
         for i in range(SSM_BLK)], axis=0)
    r_ref[0] = rt.T.astype(r_ref.dtype)


def _ssm_params(lam_re, lam_im, log_dt, b_re, b_im, c_re, c_im, d_skip):
    n_g, n_p, n_h = N_SSM_GROUPS, SSM_STATE, SSM_GROUP
    dt = jnp.repeat(jnp.exp(log_dt), n_p)
    lam = jnp.zeros((SUBLANES, STATE_COLS), _F32).at[0].set(lam_re.reshape(-1)).at[1].set(
        lam_im.reshape(-1)).at[2].set(dt)
    groups = lambda r, c: pl.BlockSpec((SUBLANES, r, c), lambda k: (k, 0, 0))
    out3 = lambda rows, w: pl.BlockSpec((1, rows, w), lambda k: (k, 0, 0))
    cols = pl.BlockSpec((SUBLANES, TILE_STATE), lambda k: (0, k))
    k_blk = SSM_BLK * LANES
    v, r, wb, wc, coef = pl.pallas_call(
        _ssm_prep_kernel,
        grid=(N_LANE_TILES,),
        in_specs=[cols, groups(n_p, n_h), groups(n_p, n_h), groups(n_h, n_p), groups(n_h, n_p)],
        out_specs=[out3(k_blk, 2 * TILE_STATE), out3(2 * TILE_STATE + k_blk, k_blk),
                   out3(LANES, 2 * TILE_STATE), out3(2 * TILE_STATE, LANES), cols],
        out_shape=[jax.ShapeDtypeStruct((N_LANE_TILES, k_blk, 2 * TILE_STATE), _BF16),
                   jax.ShapeDtypeStruct((N_LANE_TILES, 2 * TILE_STATE + k_blk, k_blk), _BF16),
                   jax.ShapeDtypeStruct((N_LANE_TILES, LANES, 2 * TILE_STATE), _F32),
                   jax.ShapeDtypeStruct((N_LANE_TILES, 2 * TILE_STATE, LANES), _F32),
                   jax.ShapeDtypeStruct((SUBLANES, STATE_COLS), _F32)],
        compiler_params=pltpu.CompilerParams(
            dimension_semantics=("arbitrary",), vmem_limit_bytes=VMEM_LIMIT),
        name="ssm_prep",
    )(lam, b_re, b_im, c_re, c_im)
    return wb, wc, v, r, coef, d_skip.reshape(1, D_SSM)


def _dispatch_plan(route_t, cnt):
    t_all = route_t.shape[1]
    codes = route_t[R_CODE1:R_CODE2 + 1].astype(jnp.int32).reshape(-1)
    per_pass = SC_WORKERS * DISPATCH_CHUNK
    codes = jnp.pad(codes, (0, -(2 * t_all) % per_pass))
    counts = cnt[:, 0].astype(jnp.int32)
    zero = jnp.zeros((1,), jnp.int32)
    units = (counts + EXP_UNIT - 1) // EXP_UNIT
    unit_start = jnp.concatenate([zero, jnp.cumsum(units)])
    start_row = jnp.zeros((LANES,), jnp.int32).at[:N_EXPERTS].set(unit_start[:N_EXPERTS] * EXP_UNIT)
    pieces = (units + EXP_CLASSES - 1) // EXP_CLASSES
    piece_start = jnp.concatenate([zero, jnp.cumsum(pieces)])
    tm = EXP_UNIT * EXP_CLASSES
    max_units = (2 * t_all + N_EXPERTS * (EXP_UNIT - 1)) // EXP_UNIT
    max_pieces = (max_units + N_EXPERTS * (EXP_CLASSES - 1)) // EXP_CLASSES
    g = jnp.arange(max_pieces, dtype=jnp.int32)
    owner = ((g[:, None] >= piece_start[None, :-1]) & (g[:, None] < piece_start[None, 1:])).astype(jnp.int32)
    pick = lambda table: jnp.sum(owner * table[None, :], axis=1)
    n_pc = jnp.maximum(pick(pieces), 1)
    small, n_big = pick(units) // n_pc, pick(units) % n_pc
    j = g - pick(piece_start[:-1])
    first_unit = pick(unit_start[:-1]) + j * small + jnp.minimum(j, n_big)
    piece_row = first_unit * EXP_UNIT
    piece_cls = jnp.clip(small + (j < n_big).astype(jnp.int32), 1, EXP_CLASSES)
    n_rows = (max_units * EXP_UNIT + tm - 1) // tm * tm + tm
    return codes, start_row, n_rows, piece_start, piece_row, piece_cls


def kernel(x_prompt, x_sample, state_ssm_re, state_ssm_im, norm1_g, w_in, lam_re, lam_im, log_dt, ssm_b_re, ssm_b_im, ssm_c_re, ssm_c_im, ssm_d, gmlp_norm_g, gmlp_w_s, gmlp_b_s, out_norm_ssm_g, out_norm_gmlp_g, w_out, norm2_g, w_router_group, b_router_group, w_router_expert, b_router_expert, w_gate, w_up, w_down, final_norm_g):
    n, l, d = x_prompt.shape
    ns = x_sample.shape[0]
    t_all = n * l + ns
    li = 0
    g1 = norm1_g[li].reshape(1, d)
    gn = gmlp_norm_g[li].reshape(1, D_GMLP)
    tril = jnp.tril(jnp.ones((CHUNK, CHUNK), dtype=bool))
    ws_tril = jnp.where(tril[None], gmlp_w_s[li], 0.0)
    bs = gmlp_b_s[li]
    gog = out_norm_gmlp_g[li].reshape(1, D_GMLP)
    gos = out_norm_ssm_g[li].reshape(1, D_SSM)
    wb, wc, v_blk, r_blk, coef, dsk = _ssm_params(
        lam_re[li], lam_im[li], log_dt[li], ssm_b_re[li], ssm_b_im[li], ssm_c_re[li], ssm_c_im[li], ssm_d[li])
    g2 = norm2_g[li].reshape(1, d)
    pad = LANES - N_EXPERTS - N_EXPERT_GROUPS
    wr = jnp.concatenate([w_router_expert[li], w_router_group[li], jnp.zeros((d, pad), _F32)], axis=1)
    br = jnp.concatenate([b_router_expert[li], b_router_group[li], jnp.zeros((pad,), _F32)]).reshape(1, LANES)

    xa, sg, mixb = _front_prompt(x_prompt, g1, w_in[li], gn, ws_tril.astype(_BF16), bs.T, gog)
    mixa, hfin = _ssm_prompt(xa, sg, v_blk, r_blk, coef, dsk, gos)
    w00 = jnp.repeat(ws_tril[:, 0, 0], GMLP_HEAD).reshape(1, D_GMLP)
    b0 = jnp.repeat(bs[:, 0], GMLP_HEAD).reshape(1, D_GMLP)
    mix_s, hr_s, hi_s, vrow = _front_sample(
        x_sample.reshape(ns, d), g1, w_in[li], gn, w00, b0, gog, wb, wc, coef, dsk, gos,
        state_ssm_re[li].reshape(ns, STATE_COLS), state_ssm_im[li].reshape(ns, STATE_COLS))

    x1, xn, route, route_t, cnt = _mixer_out(x_prompt, mixa, mixb, x_sample.reshape(ns, d), mix_s,
                                             w_out[li], g2, wr, br)
    codes, start_row, n_rows, piece_start, piece_row, piece_cls = _dispatch_plan(route_t, cnt)
    xs, dest = _sc_dispatch(xn, codes, start_row, n_rows, DISPATCH_CHUNK)
    ys = _experts(piece_start, piece_row, piece_cls, n_rows, xs, w_gate[li], w_up[li], w_down[li])
    yab = _sc_combine(ys, dest, 2 * t_all, COMBINE_CHUNK).reshape(2, t_all, d // 2)
    y_p, y_s = _final(x1, yab, route, final_norm_g.reshape(1, d), n * l, ns)

    hf = hfin.reshape(n, N_LANE_TILES, 2, 8, SSM_STATE)
    re_p = hf[:, :, 0].reshape(1, n, N_SSM_GROUPS, SSM_STATE)
    im_p = hf[:, :, 1].reshape(1, n, N_SSM_GROUPS, SSM_STATE)
    re_s = hr_s.reshape(1, ns, N_SSM_GROUPS, SSM_STATE)
    im_s = hi_s.reshape(1, ns, N_SSM_GROUPS, SSM_STATE)
    return (y_p.reshape(n, l, d), y_s.reshape(ns, 1, d), re_p, im_p, re_s, im_s,
            vrow.reshape(1, ns, 1, D_GMLP))
```

```python
import math

import jax
import jax.numpy as jnp
from jax import lax
from jax.experimental import pallas as pl
from jax.experimental.pallas import tpu as pltpu
from jax.experimental.pallas import tpu_sc as plsc

D_MODEL = 1024
D_SSM = 512
D_GMLP = 512
SSM_GROUP = 16
N_SSM_GROUPS = 32
SSM_STATE = 64
CHUNK = 128
N_GMLP_HEADS = 4
GMLP_HEAD = 128
N_EXPERT_GROUPS = 4
EXPERTS_PER_GROUP = 8
N_EXPERTS = 32
D_EXPERT = 512
D_IN = 2048
EPS = 1e-6

LANES = 128
SUBLANES = 8
N_LANE_TILES = D_SSM // LANES
STATE_COLS = N_SSM_GROUPS * SSM_STATE
TILE_STATE = STATE_COLS // N_LANE_TILES
VMEM_LIMIT = 56 * 1024 * 1024

SC_CORES = 2
SC_SUBCORES = 16
SC_LANES = 16
SC_WORKERS = SC_CORES * SC_SUBCORES

FRONT_TL = 512
SSM_LC = 256
SSM_BLK = 4
COEF_LB_RE, COEF_LB_IM, COEF_LBLK_RE, COEF_LBLK_IM = 0, 1, 2, 3
TOK_TM = 512
FINAL_TM = 1024
EXP_UNIT = 128
EXP_CLASSES = 8
W_SPLIT = 4
DISPATCH_CHUNK = 80
COMBINE_CHUNK = 24

R_E1, R_E2, R_W1, R_W2, R_RANK1, R_RANK2, R_CODE1, R_CODE2 = 0, 1, 2, 3, 4, 5, 6, 7
CODE_BITS = 16
CODE_SHIFT = float(1 << CODE_BITS)

_INV_SQRT2 = 1.0 / math.sqrt(2.0)
_BF16 = jnp.bfloat16
_F32 = jnp.float32
_U32 = jnp.uint32


def _gelu(x):
    return 0.5 * x * (1.0 + lax.erf(x * _INV_SQRT2))


def _rms(x, g):
    return x * lax.rsqrt(jnp.mean(x * x, axis=-1, keepdims=True) + EPS) * g


def _dot(a, b):
    return jnp.dot(a, b, preferred_element_type=_F32)


def _dot_f32(a, b):
    return jnp.dot(a, b, preferred_element_type=_F32, precision=lax.Precision.HIGHEST)


def _dot_hi(a, b):
    def split(x):
        hi = x.astype(_BF16)
        return hi, (x - hi.astype(_F32)).astype(_BF16)

    a_hi, a_lo = split(a)
    b_hi, b_lo = split(b)
    return _dot(a_hi, b_hi) + _dot(a_hi, b_lo) + _dot(a_lo, b_hi)


def _pack_bf16_pair(x):
    w = x.shape[1] // 2
    hi = lax.bitcast_convert_type(x[:, :w].astype(_BF16).astype(_F32), _U32)
    lo = lax.bitcast_convert_type(x[:, w:].astype(_BF16).astype(_F32), _U32)
    return hi | (lo >> 16)


def _unpack_bf16_pair(p):
    hi = lax.bitcast_convert_type(p & jnp.uint32(0xFFFF0000), _F32)
    lo = lax.bitcast_convert_type(p << 16, _F32)
    return jnp.concatenate([hi, lo], axis=-1)


def _head_norm_gelu(vb, gn):
    v = _gelu(vb)
    parts = []
    for h in range(N_GMLP_HEADS):
        vh = v[:, h * GMLP_HEAD:(h + 1) * GMLP_HEAD]
        parts.append(vh * lax.rsqrt(jnp.mean(vh * vh, axis=-1, keepdims=True) + EPS))
    return jnp.concatenate(parts, axis=-1) * gn


def _front_prompt_kernel(x_ref, g1_ref, win_ref, gn_ref, ws_ref, bs_ref, gog_ref,
                         xa_ref, sg_ref, mixb_ref, win_bf, z_ref):
    @pl.when(pl.program_id(0) == 0)
    def _():
        win_bf[...] = win_ref[...].astype(_BF16)
        z_ref[...] = jnp.zeros_like(z_ref)

    z = z_ref[...]
    x = x_ref[0]
    hn = _rms(x, g1_ref[...]).astype(_BF16)
    z_ref[...] = _dot(hn, win_bf[...])
    xa_ref[0] = z[:, :D_SSM]
    sg_ref[0] = jax.nn.sigmoid(z[:, D_SSM:2 * D_SSM])
    ub = _gelu(z[:, 2 * D_SSM:2 * D_SSM + D_GMLP])
    vbn = _head_norm_gelu(z[:, 2 * D_SSM + D_GMLP:], gn_ref[...]).astype(_BF16)
    tl = x.shape[0]
    rows = []
    for c in range(tl // CHUNK):
        heads = []
        for h in range(N_GMLP_HEADS):
            vh = vbn[c * CHUNK:(c + 1) * CHUNK, h * GMLP_HEAD:(h + 1) * GMLP_HEAD]
            heads.append(_dot(ws_ref[h], vh) + bs_ref[:, h:h + 1])
        rows.append(jnp.concatenate(heads, axis=-1))
    s = jnp.concatenate(rows, axis=0)
    mixb_ref[0] = _rms(ub * s, gog_ref[...]).astype(_BF16)


def _front_prompt(x, g1, win, gn, ws_tril_bf, bs_t, gog):
    n, l, d = x.shape
    tl = FRONT_TL
    per_seq = l // tl
    n_tiles = n * per_seq
    cur = lambda i: jnp.minimum(i, n_tiles - 1)
    prev = lambda i: jnp.maximum(i - 1, 0)
    const = lambda *shape: pl.BlockSpec(shape, lambda i: (0,) * len(shape))
    seq = lambda w, which: pl.BlockSpec((1, tl, w), lambda i: (which(i) // per_seq, which(i) % per_seq, 0))
    return pl.pallas_call(
        _front_prompt_kernel,
        grid=(n_tiles + 1,),
        in_specs=[seq(d, cur), const(1, d), const(d, D_IN), const(1, D_GMLP),
                  const(N_GMLP_HEADS, CHUNK, CHUNK), const(CHUNK, N_GMLP_HEADS), const(1, D_GMLP)],
        out_specs=[seq(D_SSM, prev), seq(D_SSM, prev), seq(D_GMLP, prev)],
        out_shape=[jax.ShapeDtypeStruct((n, l, D_SSM), _F32),
                   jax.ShapeDtypeStruct((n, l, D_SSM), _F32),
                   jax.ShapeDtypeStruct((n, l, D_GMLP), _BF16)],
        scratch_shapes=[pltpu.VMEM((d, D_IN), _BF16), pltpu.VMEM((tl, D_IN), _F32)],
        compiler_params=pltpu.CompilerParams(
            dimension_semantics=("arbitrary",), vmem_limit_bytes=VMEM_LIMIT),
        name="front_prompt",
    )(x, g1, win, gn, ws_tril_bf, bs_t, gog)


def _ssm_prompt_kernel(xa_ref, sg_ref, v_ref, r_ref, coef_ref, dsk_ref, gos_ref,
                       mixa_ref, hfin_ref, s_ref, st_ref):
    lc = xa_ref.shape[1]
    nblk = lc // SSM_BLK
    rows = nblk * SUBLANES

    @pl.when(pl.program_id(0) == 0)
    def _():
        st_ref[...] = jnp.zeros_like(st_ref)

    def by_position(ref):
        t = pltpu.einshape("btc->tbc", ref[...]).reshape(nblk, SSM_BLK, SUBLANES, D_SSM)
        return [t[:, i].reshape(rows, D_SSM) for i in range(SSM_BLK)]

    xs = by_position(xa_ref)
    xs_bf = [x.astype(_BF16) for x in xs]
    xk = [jnp.concatenate([x[:, k * LANES:(k + 1) * LANES] for x in xs_bf], axis=-1)
          for k in range(N_LANE_TILES)]
    for k in range(N_LANE_TILES):
        s_ref[:, 2 * TILE_STATE * k:2 * TILE_STATE * (k + 1)] = _dot(xk[k], v_ref[k])

    for kk in range(0, N_LANE_TILES, 2):
        tiles = (kk, kk + 1)
        cols = [(2 * TILE_STATE * k, 2 * TILE_STATE * k + TILE_STATE) for k in tiles]
        lbs = [tuple(jnp.broadcast_to(coef_ref[row:row + 1, k * TILE_STATE:(k + 1) * TILE_STATE],
                                      (SUBLANES, TILE_STATE)) for row in (COEF_LBLK_RE, COEF_LBLK_IM))
               for k in tiles]

        def body(j, carry, cols=cols, lbs=lbs):
            r0 = pl.multiple_of(j * SUBLANES, SUBLANES)
            out = []
            for q, ((c_re, c_im), (lr, li)) in enumerate(zip(cols, lbs)):
                hr, hi = carry[2 * q], carry[2 * q + 1]
                sr = s_ref[pl.ds(r0, SUBLANES), c_re:c_re + TILE_STATE]
                si = s_ref[pl.ds(r0, SUBLANES), c_im:c_im + TILE_STATE]
                s_ref[pl.ds(r0, SUBLANES), c_re:c_re + TILE_STATE] = hr
                s_ref[pl.ds(r0, SUBLANES), c_im:c_im + TILE_STATE] = hi
                out += [lr * hr - li * hi + sr, lr * hi + li * hr + si]
            return tuple(out)

        init = tuple(st_ref[:, c:c + TILE_STATE] for c_pair in cols for c in c_pair)
        fin = lax.fori_loop(0, nblk, body, init, unroll=2)
        for q, (c_re, c_im) in enumerate(cols):
            st_ref[:, c_re:c_re + TILE_STATE] = fin[2 * q]
            st_ref[:, c_im:c_im + TILE_STATE] = fin[2 * q + 1]

    yk = []
    for k in range(N_LANE_TILES):
        h_in = s_ref[:, 2 * TILE_STATE * k:2 * TILE_STATE * (k + 1)].astype(_BF16)
        yk.append(_dot(jnp.concatenate([h_in, xk[k]], axis=-1), r_ref[k]))
    sgs = by_position(sg_ref)
    outs = []
    for i in range(SSM_BLK):
        y = jnp.concatenate([y_k[:, i * LANES:(i + 1) * LANES] for y_k in yk], axis=-1) + dsk_ref[...] * xs[i]
        outs.append(_rms(_gelu(y) * sgs[i], gos_ref[...]).reshape(nblk, SUBLANES, D_SSM))
    mixa = jnp.stack(outs, axis=1).reshape(lc, SUBLANES, D_SSM)
    mixa_ref[...] = pltpu.einshape("tbc->btc", mixa).astype(_BF16)
    hfin_ref[...] = st_ref[...]


def _ssm_prompt(xa, sg, v, r, coef, dsk, gos):
    n, l, _ = xa.shape
    lc = SSM_LC
    const = lambda *shape: pl.BlockSpec(shape, lambda i: (0,) * len(shape))
    seq_spec = pl.BlockSpec((n, lc, D_SSM), lambda i: (0, i, 0))
    return pl.pallas_call(
        _ssm_prompt_kernel,
        grid=(l // lc,),
        in_specs=[seq_spec, seq_spec, const(*v.shape), const(*r.shape),
                  const(*coef.shape), const(1, D_SSM), const(1, D_SSM)],
        out_specs=[seq_spec, const(n, 2 * STATE_COLS)],
        out_shape=[jax.ShapeDtypeStruct((n, l, D_SSM), _BF16),
                   jax.ShapeDtypeStruct((n, 2 * STATE_COLS), _F32)],
        scratch_shapes=[pltpu.VMEM((lc // SSM_BLK * n, 2 * STATE_COLS), _F32),
                        pltpu.VMEM((n, 2 * STATE_COLS), _F32)],
        compiler_params=pltpu.CompilerParams(
            dimension_semantics=("arbitrary",), vmem_limit_bytes=VMEM_LIMIT),
        name="ssm_prompt",
    )(xa, sg, v, r, coef, dsk, gos)


def _front_sample_kernel(x_ref, g1_ref, win_ref, gn_ref, w00_ref, b0_ref, gog_ref,
                         wb_ref, wc_ref, coef_ref, dsk_ref, gos_ref, h0r_ref, h0i_ref,
                         mix_ref, hr_ref, hi_ref, vrow_ref):
    x = x_ref[...]
    hn = _rms(x, g1_ref[...])
    z = _dot_hi(hn, win_ref[...])
    xa = z[:, :D_SSM]
    ys = []
    for k in range(N_LANE_TILES):
        bu = _dot_hi(xa[:, k * LANES:(k + 1) * LANES], wb_ref[k])
        sl = slice(k * TILE_STATE, (k + 1) * TILE_STATE)
        lr, li = coef_ref[COEF_LB_RE:COEF_LB_RE + 1, sl], coef_ref[COEF_LB_IM:COEF_LB_IM + 1, sl]
        h0r, h0i = h0r_ref[:, sl], h0i_ref[:, sl]
        nr = lr * h0r - li * h0i + bu[:, :TILE_STATE]
        ni = lr * h0i + li * h0r + bu[:, TILE_STATE:]
        hr_ref[:, sl] = nr
        hi_ref[:, sl] = ni
        ys.append(_dot_hi(jnp.concatenate([nr, ni], axis=-1), wc_ref[k]))
    y = jnp.concatenate(ys, axis=-1) + dsk_ref[...] * xa
    ya = _gelu(y) * jax.nn.sigmoid(z[:, D_SSM:2 * D_SSM])
    mix_ref[:, :D_SSM] = _rms(ya, gos_ref[...])
    ub = _gelu(z[:, 2 * D_SSM:2 * D_SSM + D_GMLP])
    vbn = _head_norm_gelu(z[:, 2 * D_SSM + D_GMLP:], gn_ref[...])
    vrow_ref[...] = vbn
    s = w00_ref[...] * vbn + b0_ref[...]
    mix_ref[:, D_SSM:] = _rms(ub * s, gog_ref[...])


def _front_sample(x, g1, win, gn, w00, b0, gog, wb, wc, coef, dsk, gos, h0r, h0i):
    n = x.shape[0]
    vmem = pl.BlockSpec(memory_space=pltpu.VMEM)
    return pl.pallas_call(
        _front_sample_kernel,
        in_specs=[vmem] * 14,
        out_specs=[vmem] * 4,
        out_shape=[jax.ShapeDtypeStruct((n, D_MODEL), _F32),
                   jax.ShapeDtypeStruct((n, STATE_COLS), _F32),
                   jax.ShapeDtypeStruct((n, STATE_COLS), _F32),
                   jax.ShapeDtypeStruct((n, D_GMLP), _F32)],
        compiler_params=pltpu.CompilerParams(vmem_limit_bytes=VMEM_LIMIT),
        name="front_sample",
    )(x, g1, win, gn, w00, b0, gog, wb, wc, coef, dsk, gos, h0r, h0i)


def _route(logits, base):
    tm = logits.shape[0]
    lt = logits.T
    ex = lt[:N_EXPERTS, :]
    gr = lt[N_EXPERTS:N_EXPERTS + SUBLANES, :]
    row_e = lax.broadcasted_iota(jnp.int32, ex.shape, 0).astype(_F32)
    row_g = lax.broadcasted_iota(jnp.int32, gr.shape, 0).astype(_F32)
    neg = jnp.float32(-jnp.inf)
    big = jnp.float32(LANES)
    is_g = row_g < N_EXPERT_GROUPS
    gl = jnp.where(is_g, gr, neg)
    gmax = jnp.max(gl, axis=0, keepdims=True)
    gi = jnp.min(jnp.where(gl == gmax, row_g, big), axis=0, keepdims=True)
    p_top = 1.0 / jnp.sum(jnp.where(is_g, jnp.exp(gl - gmax), 0.0), axis=0, keepdims=True)
    lo = gi * EXPERTS_PER_GROUP
    in_grp = (row_e >= lo) & (row_e < lo + EXPERTS_PER_GROUP)
    m1 = jnp.max(jnp.where(in_grp, ex, neg), axis=0, keepdims=True)
    i1 = jnp.min(jnp.where(in_grp & (ex == m1), row_e, big), axis=0, keepdims=True)
    rest = in_grp & (row_e != i1)
    m2 = jnp.max(jnp.where(rest, ex, neg), axis=0, keepdims=True)
    i2 = jnp.min(jnp.where(rest & (ex == m2), row_e, big), axis=0, keepdims=True)
    e2 = jnp.exp(m2 - m1)
    w1 = p_top / (1.0 + e2)
    w2 = p_top * e2 / (1.0 + e2)
    sel1 = row_e == i1
    sel2 = row_e == i2
    hits = jnp.where(sel1 | sel2, 1.0, 0.0)
    src = lax.broadcasted_iota(jnp.int32, (tm, tm), 0)
    dst = lax.broadcasted_iota(jnp.int32, (tm, tm), 1)
    before = _dot(hits.astype(_BF16), jnp.where(src < dst, 1.0, 0.0).astype(_BF16)) + base
    rank1 = jnp.sum(jnp.where(sel1, before, 0.0), axis=0, keepdims=True)
    rank2 = jnp.sum(jnp.where(sel2, before, 0.0), axis=0, keepdims=True)
    fields = {R_E1: i1, R_E2: i2, R_W1: w1, R_W2: w2, R_RANK1: rank1, R_RANK2: rank2,
              R_CODE1: i1 * CODE_SHIFT + rank1, R_CODE2: i2 * CODE_SHIFT + rank2}
    row8 = lax.broadcasted_iota(jnp.int32, (SUBLANES, tm), 0)
    route_t = jnp.zeros((SUBLANES, tm), _F32)
    for r, val in fields.items():
        route_t = jnp.where(row8 == r, val, route_t)
    route = jnp.concatenate([route_t, jnp.zeros((LANES - SUBLANES, tm), _F32)], axis=0).T
    return route_t, route, base + jnp.sum(hits, axis=1, keepdims=True)


def _mixer_out_prompt_kernel(x_ref, mixa_ref, mixb_ref, wo_ref, g2_ref, wr_ref, br_ref,
                             x1_ref, xn_ref, route_ref, route_t_ref, cnt_ref, base_ref, logits_ref, wo_bf):
    i = pl.program_id(0)

    @pl.when(i == 0)
    def _():
        base_ref[...] = jnp.zeros_like(base_ref)
        logits_ref[...] = jnp.zeros_like(logits_ref)
        wo_bf[...] = wo_ref[...].astype(_BF16)

    prev_logits = logits_ref[...]
    x1 = x_ref[0] + _dot(mixa_ref[0], wo_bf[:D_SSM, :]) + _dot(mixb_ref[0], wo_bf[D_SSM:, :])
    xn = _rms(x1, g2_ref[...])
    x1_ref[...] = _pack_bf16_pair(x1)
    xn_ref[...] = _pack_bf16_pair(xn)
    logits_ref[...] = _dot(xn.astype(_BF16), wr_ref[...]) + br_ref[...]
    route_t, route, base = _route(prev_logits, base_ref[...])
    route_ref[...] = route
    route_t_ref[...] = route_t
    base = jnp.where(i >= 1, base, base_ref[...])
    base_ref[...] = base
    cnt_ref[...] = base


def _mixer_out_sample_kernel(x_ref, mix_ref, wo_ref, g2_ref, wr_ref, br_ref, cnt_in_ref,
                             x1_in, xn_in, route_in, route_t_in,
                             x1_ref, xn_ref, route_ref, route_t_ref, cnt_ref):
    del x1_in, xn_in, route_in, route_t_in
    x1 = (x_ref[...] + _dot_hi(mix_ref[:, :D_SSM], wo_ref[:D_SSM, :])
          + _dot_hi(mix_ref[:, D_SSM:], wo_ref[D_SSM:, :]))
    xn = _rms(x1, g2_ref[...])
    logits = _dot_hi(xn, wr_ref[...]) + br_ref[...]
    route_t, route, base = _route(logits, cnt_in_ref[...])
    x1_ref[...] = _pack_bf16_pair(x1)
    xn_ref[...] = _pack_bf16_pair(xn)
    route_ref[...] = route
    route_t_ref[...] = route_t
    cnt_ref[...] = base


def _mixer_out(x_p, mixa, mixb, x_s, mix_s, wo, g2, wr, br):
    n, l, d = x_p.shape
    ns = x_s.shape[0]
    t_all = n * l + ns
    tm = TOK_TM
    per_seq = l // tm
    n_tiles = n * per_seq
    cur = lambda i: jnp.minimum(i, n_tiles - 1)
    prev = lambda i: jnp.maximum(i - 1, 0)
    const = lambda *shape: pl.BlockSpec(shape, lambda i: (0,) * len(shape))
    seq = lambda w: pl.BlockSpec((1, tm, w), lambda i: (cur(i) // per_seq, cur(i) % per_seq, 0))
    tok = lambda w, which: pl.BlockSpec((tm, w), lambda i: (which(i), 0))
    tok_shapes = [jax.ShapeDtypeStruct((t_all, d // 2), _U32),
                  jax.ShapeDtypeStruct((t_all, d // 2), _U32),
                  jax.ShapeDtypeStruct((t_all, LANES), _F32),
                  jax.ShapeDtypeStruct((SUBLANES, t_all), _F32)]
    cnt_shape = jax.ShapeDtypeStruct((N_EXPERTS, 1), _F32)
    x1, xn, route, route_t, cnt = pl.pallas_call(
        _mixer_out_prompt_kernel,
        grid=(n_tiles + 1,),
        in_specs=[seq(d), seq(D_SSM), seq(D_GMLP),
                  const(d, d), const(1, d), const(d, LANES), const(1, LANES)],
        out_specs=[tok(d // 2, cur), tok(d // 2, cur), tok(LANES, prev),
                   pl.BlockSpec((SUBLANES, tm), lambda i: (0, prev(i))), const(N_EXPERTS, 1)],
        out_shape=tok_shapes + [cnt_shape],
        scratch_shapes=[pltpu.VMEM((N_EXPERTS, 1), _F32), pltpu.VMEM((tm, LANES), _F32),
                        pltpu.VMEM((d, d), _BF16)],
        compiler_params=pltpu.CompilerParams(
            dimension_semantics=("arbitrary",), vmem_limit_bytes=VMEM_LIMIT),
        name="mixer_out_prompt",
    )(x_p, mixa, mixb, wo, g2, wr.astype(_BF16), br)
    tail = (n * l) // ns
    c1 = lambda *shape: pl.BlockSpec(shape, lambda i: (0,) * len(shape))
    anyspec = pl.BlockSpec(memory_space=pl.ANY)
    tail_spec = lambda w: pl.BlockSpec((ns, w), lambda i: (tail, 0))
    return pl.pallas_call(
        _mixer_out_sample_kernel,
        grid=(1,),
        in_specs=[c1(ns, d), c1(ns, d), c1(d, d), c1(1, d), c1(d, LANES), c1(1, LANES), c1(N_EXPERTS, 1),
                  anyspec, anyspec, anyspec, anyspec],
        out_specs=[tail_spec(d // 2), tail_spec(d // 2), tail_spec(LANES),
                   pl.BlockSpec((SUBLANES, ns), lambda i: (0, tail)), c1(N_EXPERTS, 1)],
        out_shape=tok_shapes + [cnt_shape],
        input_output_aliases={7: 0, 8: 1, 9: 2, 10: 3},
        compiler_params=pltpu.CompilerParams(
            dimension_semantics=("arbitrary",), vmem_limit_bytes=VMEM_LIMIT),
        name="mixer_out_sample",
    )(x_s, mix_s, wo, g2, wr, br, cnt, x1, xn, route, route_t)


def _sc_stream(n_chunks, gather, write):
    gather(0).start()
    for j in range(n_chunks):
        if j + 1 < n_chunks:
            if j >= 1:
                write(j - 1).wait()
            gather(j + 1).start()
        gather(j).wait()
        write(j).start()
    if n_chunks >= 2:
        write(n_chunks - 2).wait()
    write(n_chunks - 1).wait()


def _sc_mesh():
    return plsc.VectorSubcoreMesh(core_axis_name="c", subcore_axis_name="s",
                                  num_cores=SC_CORES, num_subcores=SC_SUBCORES)


def _sc_buffers(chunk, w, dtype):
    return [pltpu.VMEM((chunk, w), dtype), pltpu.VMEM((chunk, w), dtype)] + [pltpu.SemaphoreType.DMA] * 4


def _sc_combine(table, idx, n_out, chunk):
    w = table.shape[1]
    rows_w = n_out // SC_WORKERS
    n_chunks = rows_w // chunk
    assert rows_w * SC_WORKERS == n_out and n_chunks * chunk == rows_w and rows_w % SUBLANES == 0

    def body(table_hbm, idx_hbm, out_hbm, idx_v, buf0, buf1, g0, g1, w0, w1):
        wid = lax.axis_index("s") * SC_CORES + lax.axis_index("c")
        base = pl.multiple_of(wid * rows_w, SUBLANES)
        pltpu.sync_copy(idx_hbm.at[pl.ds(base, rows_w)], idx_v)
        bufs, gsems, wsems = (buf0, buf1), (g0, g1), (w0, w1)

        def gather(j):
            return pltpu.make_async_copy(table_hbm.at[idx_v.at[pl.ds(j * chunk, chunk)]], bufs[j % 2], gsems[j % 2])

        def write(j):
            return pltpu.make_async_copy(bufs[j % 2], out_hbm.at[pl.ds(base + j * chunk, chunk)], wsems[j % 2])

        _sc_stream(n_chunks, gather, write)

    return pl.kernel(
        body,
        out_type=jax.ShapeDtypeStruct((n_out, w), table.dtype),
        mesh=_sc_mesh(),
        scratch_types=[pltpu.VMEM((rows_w,), jnp.int32)] + _sc_buffers(chunk, w, table.dtype),
        compiler_params=pltpu.CompilerParams(use_tc_tiling_on_sc=True),
        name="sc_combine",
    )(table, idx)


def _sc_dispatch(table, codes, start_row, n_out, chunk):
    t_all, w = table.shape
    n_pad = codes.shape[0]
    n_ent = 2 * t_all
    ent_w = n_pad // SC_WORKERS
    n_chunks = ent_w // chunk
    per_chunk = chunk // SC_LANES
    trash = n_out - (n_pad - n_ent)
    assert ent_w * SC_WORKERS == n_pad and n_chunks * chunk == ent_w
    assert per_chunk * SC_LANES == chunk and chunk <= LANES and n_pad - n_ent <= t_all

    def body(table_hbm, code_hbm, start_hbm, out_hbm, dest_hbm,
             code_v, dest_v, tok_v, dst_v, start_v, buf0, buf1, g0, g1, w0, w1):
        wid = lax.axis_index("s") * SC_CORES + lax.axis_index("c")
        ebase = pl.multiple_of(wid * ent_w, SUBLANES)
        pltpu.sync_copy(code_hbm.at[pl.ds(ebase, ent_w)], code_v)
        pltpu.sync_copy(start_hbm, start_v)
        lane = lax.iota(jnp.int32, SC_LANES)
        for j in range(n_chunks):
            for c in range(per_chunk):
                off = j * chunk + c * SC_LANES
                ent = ebase + off + lane
                code = code_v[pl.ds(off, SC_LANES)]
                d = plsc.load_gather(start_v, [code >> CODE_BITS]) + (code & ((1 << CODE_BITS) - 1))
                d = jnp.where(ent >= n_ent, trash + (ent - n_ent), d)
                tok = jnp.where(ent >= t_all, ent - t_all, ent)
                tok = jnp.where(tok >= t_all, tok - t_all, tok)
                dest_v[pl.ds(off, SC_LANES)] = d
                dst_v[j, pl.ds(c * SC_LANES, SC_LANES)] = d
                tok_v[j, pl.ds(c * SC_LANES, SC_LANES)] = tok
        pltpu.sync_copy(dest_v, dest_hbm.at[pl.ds(ebase, ent_w)])
        bufs, gsems, wsems = (buf0, buf1), (g0, g1), (w0, w1)

        def gather(j):
            return pltpu.make_async_copy(table_hbm.at[tok_v.at[j]], bufs[j % 2], gsems[j % 2])

        def scatter(j):
            return pltpu.make_async_copy(bufs[j % 2], out_hbm.at[dst_v.at[j]], wsems[j % 2])

        _sc_stream(n_chunks, gather, scatter)

    return pl.kernel(
        body,
        out_type=(jax.ShapeDtypeStruct((n_out, w), table.dtype), jax.ShapeDtypeStruct((n_pad,), jnp.int32)),
        mesh=_sc_mesh(),
        scratch_types=([pltpu.VMEM((ent_w,), jnp.int32), pltpu.VMEM((ent_w,), jnp.int32),
                        pltpu.VMEM((n_chunks, chunk), jnp.int32), pltpu.VMEM((n_chunks, chunk), jnp.int32),
                        pltpu.VMEM((LANES,), jnp.int32)] + _sc_buffers(chunk, w, table.dtype)),
        compiler_params=pltpu.CompilerParams(use_tc_tiling_on_sc=True, needs_layout_passes=False),
        name="sc_dispatch",
    )(table, codes, start_row)


def _experts_kernel(piece_start_ref, piece_row_ref, piece_cls_ref, *refs):
    n_w = 3 * W_SPLIT
    wg_refs, wu_refs, wd_refs = refs[:W_SPLIT], refs[W_SPLIT:2 * W_SPLIT], refs[2 * W_SPLIT:n_w]
    xs_hbm, ys_hbm, wg_bf, wu_bf, wd_bf, xbuf, ybuf, xsem, ysem = refs[n_w:]
    e = pl.program_id(0)
    g0 = piece_start_ref[e]
    n_here = piece_start_ref[e + 1] - g0
    n_total = piece_start_ref[N_EXPERTS]

    def per_class(g, fn):
        cls = piece_cls_ref[g]
        row = pl.multiple_of(piece_row_ref[g], EXP_UNIT)
        for c in range(1, EXP_CLASSES + 1):
            pl.when(cls == c)(lambda c=c: fn(c * EXP_UNIT, row))

    def x_copy(slot, rows, row):
        return pltpu.make_async_copy(xs_hbm.at[pl.ds(row, rows)], xbuf.at[slot, pl.ds(0, rows)], xsem.at[slot])

    def y_copy(slot, rows, row):
        return pltpu.make_async_copy(ybuf.at[slot, pl.ds(0, rows)], ys_hbm.at[pl.ds(row, rows)], ysem.at[slot])

    @pl.when((e == 0) & (n_total > 0))
    def _():
        per_class(0, lambda rows, row: x_copy(0, rows, row).start())

    for dst, chunks in ((wg_bf, wg_refs), (wu_bf, wu_refs), (wd_bf, wd_refs)):
        rows = dst.shape[0] // W_SPLIT
        for q, src in enumerate(chunks):
            dst[q * rows:(q + 1) * rows, :] = src[0, 0].astype(_BF16)

    def piece(j, carry):
        g = g0 + j
        slot = lax.rem(g, 2)
        per_class(g, lambda rows, row: x_copy(slot, rows, row).wait())

        @pl.when(g + 1 < n_total)
        def _():
            per_class(g + 1, lambda rows, row: x_copy(1 - slot, rows, row).start())

        @pl.when(g >= 2)
        def _():
            per_class(g - 2, lambda rows, row: y_copy(slot, rows, row).wait())

        def compute(rows, row):
            x = _unpack_bf16_pair(xbuf[slot, pl.ds(0, rows)]).astype(_BF16)
            a = _dot(x, wg_bf[...])
            u = _dot(x, wu_bf[...])
            h = (a * jax.nn.sigmoid(a) * u).astype(_BF16)
            ybuf[slot, pl.ds(0, rows)] = _pack_bf16_pair(_dot(h, wd_bf[...]))
            y_copy(slot, rows, row).start()

        per_class(g, compute)
        return carry

    lax.fori_loop(0, n_here, piece, 0)

    @pl.when(e == N_EXPERTS - 1)
    def _():
        @pl.when(n_total >= 2)
        def _():
            per_class(n_total - 2, lambda rows, row: y_copy(lax.rem(n_total, 2), rows, row).wait())

        @pl.when(n_total >= 1)
        def _():
            per_class(n_total - 1, lambda rows, row: y_copy(lax.rem(n_total - 1, 2), rows, row).wait())


def _experts(piece_start, piece_row, piece_cls, n_rows, xs, w_gate, w_up, w_down):
    dh = xs.shape[1]
    d = 2 * dh
    tm = EXP_UNIT * EXP_CLASSES
    anyspec = pl.BlockSpec(memory_space=pl.ANY)

    def chunk_specs(rows, cols):
        return [pl.BlockSpec((1, 1, rows // W_SPLIT, cols), lambda e, ps, pr, pc, q=q: (e, q, 0, 0))
                for q in range(W_SPLIT)]

    split = lambda w: w.reshape(w.shape[0], W_SPLIT, w.shape[1] // W_SPLIT, w.shape[2])
    grid_spec = pltpu.PrefetchScalarGridSpec(
        num_scalar_prefetch=3,
        grid=(N_EXPERTS,),
        in_specs=(chunk_specs(d, D_EXPERT) + chunk_specs(d, D_EXPERT) + chunk_specs(D_EXPERT, d) + [anyspec]),
        out_specs=anyspec,
        scratch_shapes=[pltpu.VMEM((d, D_EXPERT), _BF16), pltpu.VMEM((d, D_EXPERT), _BF16),
                        pltpu.VMEM((D_EXPERT, d), _BF16),
                        pltpu.VMEM((2, tm, dh), _U32), pltpu.VMEM((2, tm, dh), _U32),
                        pltpu.SemaphoreType.DMA((2,)), pltpu.SemaphoreType.DMA((2,))],
    )
    return pl.pallas_call(
        _experts_kernel,
        grid_spec=grid_spec,
        out_shape=jax.ShapeDtypeStruct((n_rows, dh), _U32),
        compiler_params=pltpu.CompilerParams(
            dimension_semantics=("arbitrary",), vmem_limit_bytes=VMEM_LIMIT),
        name="experts",
    )(piece_start, piece_row, piece_cls, *([split(w_gate)] * W_SPLIT), *([split(w_up)] * W_SPLIT),
      *([split(w_down)] * W_SPLIT), xs)


def _final_kernel(x1_ref, ya_ref, yb_ref, route_ref, gf_ref, y_ref):
    route = route_ref[...]
    x2 = (_unpack_bf16_pair(x1_ref[...]) + route[:, R_W1:R_W1 + 1] * _unpack_bf16_pair(ya_ref[...])
          + route[:, R_W2:R_W2 + 1] * _unpack_bf16_pair(yb_ref[...]))
    y_ref[...] = _rms(x2, gf_ref[...])


def _final(x1, yab, route, gf, n_prompt, n_sample):
    d = 2 * x1.shape[1]

    def call(tm, first_block, n_rows, name):
        tok = lambda w: pl.BlockSpec((tm, w), lambda i: (first_block + i, 0))
        sel = lambda k: pl.BlockSpec((None, tm, d // 2), lambda i: (k, first_block + i, 0))
        return pl.pallas_call(
            _final_kernel,
            grid=(n_rows // tm,),
            in_specs=[tok(d // 2), sel(0), sel(1), tok(LANES), pl.BlockSpec((1, d), lambda i: (0, 0))],
            out_specs=pl.BlockSpec((tm, d), lambda i: (i, 0)),
            out_shape=jax.ShapeDtypeStruct((n_rows, d), _F32),
            compiler_params=pltpu.CompilerParams(
                dimension_semantics=("arbitrary",), vmem_limit_bytes=VMEM_LIMIT),
            name=name,
        )(x1, yab, yab, route, gf)

    return (call(FINAL_TM, 0, n_prompt, "final_prompt"),
            call(n_sample, n_prompt // n_sample, n_sample, "final_sample"))


def _powers(lam_re, lam_im, dt):
    out = []
    for m in range(SSM_BLK + 1):
        mag = jnp.exp(m * lam_re * dt)
        ang = m * lam_im * dt
        out.append((mag * jnp.cos(ang), mag * jnp.sin(ang)))
    return out


def _spread(x, copies):
    w = x.shape[1]
    src = lax.broadcasted_iota(jnp.int32, (w, w * copies), 0)
    dst = lax.broadcasted_iota(jnp.int32, (w, w * copies), 1)
    return _dot_f32(x, jnp.where(dst % w == src, 1.0, 0.0))


def _ssm_prep_kernel(lam_ref, b_re, b_im, c_re, c_im, v_ref, r_ref, wb_ref, wc_ref, coef_ref):
    n_p, n_h = SSM_STATE, SSM_GROUP
    lr, li, dt = lam_ref[0:1, :], lam_ref[1:2, :], lam_ref[2:3, :]
    pw = _powers(lr, li, dt)
    den = lr * lr + li * li
    nr, ni = pw[1][0] - 1.0, pw[1][1]
    k_re = (nr * lr + ni * li) / den
    k_im = (ni * lr - nr * li) / den
    coef_ref[...] = jnp.concatenate(
        [pw[1][0], pw[1][1], pw[SSM_BLK][0], pw[SSM_BLK][1], jnp.zeros((SUBLANES - 4, TILE_STATE), _F32)], axis=0)

    on_diag_b = (lax.broadcasted_iota(jnp.int32, (TILE_STATE, LANES), 0) // n_p
                 == lax.broadcasted_iota(jnp.int32, (TILE_STATE, LANES), 1) // n_h)
    rows_gp = lambda ref: ref[...].reshape(TILE_STATE, n_h)
    bt_re = jnp.where(on_diag_b, _spread(rows_gp(b_re), SUBLANES), 0.0).T
    bt_im = jnp.where(on_diag_b, _spread(rows_gp(b_im), SUBLANES), 0.0).T
    bb_re = k_re * bt_re - k_im * bt_im
    bb_im = k_re * bt_im + k_im * bt_re
    wb_ref[0] = jnp.concatenate([bb_re, bb_im], axis=1)
    v_rows = []
    for s in range(SSM_BLK):
        pr, pi = pw[SSM_BLK - 1 - s]
        v_rows.append(jnp.concatenate([pr * bb_re - pi * bb_im, pr * bb_im + pi * bb_re], axis=1))
    v_ref[0] = jnp.concatenate(v_rows, axis=0).astype(v_ref.dtype)

    on_diag_c = (lax.broadcasted_iota(jnp.int32, (LANES, TILE_STATE), 0) // n_h
                 == lax.broadcasted_iota(jnp.int32, (LANES, TILE_STATE), 1) // n_p)
    rows_gh = lambda ref: ref[...].reshape(LANES, n_p)
    ct_re = jnp.where(on_diag_c, _spread(rows_gh(c_re), SUBLANES), 0.0)
    ct_im = jnp.where(on_diag_c, _spread(rows_gh(c_im), SUBLANES), 0.0)
    cl = [(ct_re * pr - ct_im * pi, ct_re * pi + ct_im * pr) for pr, pi in pw]
    wc_ref[0] = jnp.concatenate([cl[0][0], -cl[0][1]], axis=1).T
    nt = lambda a, b: lax.dot_general(a, b, (((1,), (1,)), ((), ())), precision=lax.Precision.HIGHEST,
                                      preferred_element_type=_F32)
    direct = [nt(cl[m][0], bb_re) - nt(cl[m][1], bb_im) for m in range(SSM_BLK)]
    zero = jnp.zeros((LANES, LANES), _F32)
    rt = jnp.concatenate(
        [jnp.concatenate([cl[i + 1][0], -cl[i + 1][1]]
                         + [direct[i - s] if s <= i else zero for s in range(SSM_BLK)], axis=1)
         for i in range(SSM_BLK)], axis=0)
    r_ref[0] = rt.T.astype(r_ref.dtype)


def _ssm_params(lam_re, lam_im, log_dt, b_re, b_im, c_re, c_im, d_skip):
    n_g, n_p, n_h = N_SSM_GROUPS, SSM_STATE, SSM_GROUP
    dt = jnp.repeat(jnp.exp(log_dt), n_p)
    lam = jnp.zeros((SUBLANES, STATE_COLS), _F32).at[0].set(lam_re.reshape(-1)).at[1].set(
        lam_im.reshape(-1)).at[2].set(dt)
    groups = lambda r, c: pl.BlockSpec((SUBLANES, r, c), lambda k: (k, 0, 0))
    out3 = lambda rows, w: pl.BlockSpec((1, rows, w), lambda k: (k, 0, 0))
    cols = pl.BlockSpec((SUBLANES, TILE_STATE), lambda k: (0, k))
    k_blk = SSM_BLK * LANES
    v, r, wb, wc, coef = pl.pallas_call(
        _ssm_prep_kernel,
        grid=(N_LANE_TILES,),
        in_specs=[cols, groups(n_p, n_h), groups(n_p, n_h), groups(n_h, n_p), groups(n_h, n_p)],
        out_specs=[out3(k_blk, 2 * TILE_STATE), out3(2 * TILE_STATE + k_blk, k_blk),
                   out3(LANES, 2 * TILE_STATE), out3(2 * TILE_STATE, LANES), cols],
        out_shape=[jax.ShapeDtypeStruct((N_LANE_TILES, k_blk, 2 * TILE_STATE), _BF16),
                   jax.ShapeDtypeStruct((N_LANE_TILES, 2 * TILE_STATE + k_blk, k_blk), _BF16),
                   jax.ShapeDtypeStruct((N_LANE_TILES, LANES, 2 * TILE_STATE), _F32),
                   jax.ShapeDtypeStruct((N_LANE_TILES, 2 * TILE_STATE, LANES), _F32),
                   jax.ShapeDtypeStruct((SUBLANES, STATE_COLS), _F32)],
        compiler_params=pltpu.CompilerParams(
            dimension_semantics=("arbitrary",), vmem_limit_bytes=VMEM_LIMIT),
        name="ssm_prep",
    )(lam, b_re, b_im, c_re, c_im)
    return wb, wc, v, r, coef, d_skip.reshape(1, D_SSM)


def _dispatch_plan(route_t, cnt):
    t_all = route_t.shape[1]
    codes = route_t[R_CODE1:R_CODE2 + 1].astype(jnp.int32).reshape(-1)
    per_pass = SC_WORKERS * DISPATCH_CHUNK
    codes = jnp.pad(codes, (0, -(2 * t_all) % per_pass))
    counts = cnt[:, 0].astype(jnp.int32)
    zero = jnp.zeros((1,), jnp.int32)
    units = (counts + EXP_UNIT - 1) // EXP_UNIT
    unit_start = jnp.concatenate([zero, jnp.cumsum(units)])
    start_row = jnp.zeros((LANES,), jnp.int32).at[:N_EXPERTS].set(unit_start[:N_EXPERTS] * EXP_UNIT)
    pieces = (units + EXP_CLASSES - 1) // EXP_CLASSES
    piece_start = jnp.concatenate([zero, jnp.cumsum(pieces)])
    tm = EXP_UNIT * EXP_CLASSES
    max_units = (2 * t_all + N_EXPERTS * (EXP_UNIT - 1)) // EXP_UNIT
    max_pieces = (max_units + N_EXPERTS * (EXP_CLASSES - 1)) // EXP_CLASSES
    g = jnp.arange(max_pieces, dtype=jnp.int32)
    owner = ((g[:, None] >= piece_start[None, :-1]) & (g[:, None] < piece_start[None, 1:])).astype(jnp.int32)
    pick = lambda table: jnp.sum(owner * table[None, :], axis=1)
    first_unit = pick(unit_start[:-1]) + (g - pick(piece_start[:-1])) * EXP_CLASSES
    piece_row = first_unit * EXP_UNIT
    piece_cls = jnp.clip(pick(unit_start[1:]) - first_unit, 1, EXP_CLASSES)
    n_rows = (max_units * EXP_UNIT + tm - 1) // tm * tm + tm
    return codes, start_row, n_rows, piece_start, piece_row, piece_cls


def kernel(x_prompt, x_sample, state_ssm_re, state_ssm_im, norm1_g, w_in, lam_re, lam_im, log_dt, ssm_b_re, ssm_b_im, ssm_c_re, ssm_c_im, ssm_d, gmlp_norm_g, gmlp_w_s, gmlp_b_s, out_norm_ssm_g, out_norm_gmlp_g, w_out, norm2_g, w_router_group, b_router_group, w_router_expert, b_router_expert, w_gate, w_up, w_down, final_norm_g):
    n, l, d = x_prompt.shape
    ns = x_sample.shape[0]
    t_all = n * l + ns
    li = 0
    g1 = norm1_g[li].reshape(1, d)
    gn = gmlp_norm_g[li].reshape(1, D_GMLP)
    tril = jnp.tril(jnp.ones((CHUNK, CHUNK), dtype=bool))
    ws_tril = jnp.where(tril[None], gmlp_w_s[li], 0.0)
    bs = gmlp_b_s[li]
    gog = out_norm_gmlp_g[li].reshape(1, D_GMLP)
    gos = out_norm_ssm_g[li].reshape(1, D_SSM)
    wb, wc, v_blk, r_blk, coef, dsk = _ssm_params(
        lam_re[li], lam_im[li], log_dt[li], ssm_b_re[li], ssm_b_im[li], ssm_c_re[li], ssm_c_im[li], ssm_d[li])
    g2 = norm2_g[li].reshape(1, d)
    pad = LANES - N_EXPERTS - N_EXPERT_GROUPS
    wr = jnp.concatenate([w_router_expert[li], w_router_group[li], jnp.zeros((d, pad), _F32)], axis=1)
    br = jnp.concatenate([b_router_expert[li], b_router_group[li], jnp.zeros((pad,), _F32)]).reshape(1, LANES)

    xa, sg, mixb = _front_prompt(x_prompt, g1, w_in[li], gn, ws_tril.astype(_BF16), bs.T, gog)
    mixa, hfin = _ssm_prompt(xa, sg, v_blk, r_blk, coef, dsk, gos)
    w00 = jnp.repeat(ws_tril[:, 0, 0], GMLP_HEAD).reshape(1, D_GMLP)
    b0 = jnp.repeat(bs[:, 0], GMLP_HEAD).reshape(1, D_GMLP)
    mix_s, hr_s, hi_s, vrow = _front_sample(
        x_sample.reshape(ns, d), g1, w_in[li], gn, w00, b0, gog, wb, wc, coef, dsk, gos,
        state_ssm_re[li].reshape(ns, STATE_COLS), state_ssm_im[li].reshape(ns, STATE_COLS))

    x1, xn, route, route_t, cnt = _mixer_out(x_prompt, mixa, mixb, x_sample.reshape(ns, d), mix_s,
                                             w_out[li], g2, wr, br)
    codes, start_row, n_rows, piece_start, piece_row, piece_cls = _dispatch_plan(route_t, cnt)
    xs, dest = _sc_dispatch(xn, codes, start_row, n_rows, DISPATCH_CHUNK)
    ys = _experts(piece_start, piece_row, piece_cls, n_rows, xs, w_gate[li], w_up[li], w_down[li])
    yab = _sc_combine(ys, dest, 2 * t_all, COMBINE_CHUNK).reshape(2, t_all, d // 2)
    y_p, y_s = _final(x1, yab, route, final_norm_g.reshape(1, d), n * l, ns)

    hf = hfin.reshape(n, N_LANE_TILES, 2, 8, SSM_STATE)
    re_p = hf[:, :, 0].reshape(1, n, N_SSM_GROUPS, SSM_STATE)
    im_p = hf[:, :, 1].reshape(1, n, N_SSM_GROUPS, SSM_STATE)
    re_s = hr_s.reshape(1, ns, N_SSM_GROUPS, SSM_STATE)
    im_s = hi_s.reshape(1, ns, N_SSM_GROUPS, SSM_STATE)
    return (y_p.reshape(n, l, d), y_s.reshape(ns, 1, d), re_p, im_p, re_s, im_s,
            vrow.reshape(1, ns, 1, D_GMLP))
```

```python
import math

import jax
import jax.numpy as jnp
from jax import lax
from jax.experimental import pallas as pl
from jax.experimental.pallas import tpu as pltpu
from jax.experimental.pallas import tpu_sc as plsc

D_MODEL = 1024
D_SSM = 512
D_GMLP = 512
SSM_GROUP = 16
N_SSM_GROUPS = 32
SSM_STATE = 64
CHUNK = 128
N_GMLP_HEADS = 4
GMLP_HEAD = 128
N_EXPERT_GROUPS = 4
EXPERTS_PER_GROUP = 8
N_EXPERTS = 32
D_EXPERT = 512
D_IN = 2048
EPS = 1e-6

LANES = 128
SUBLANES = 8
N_LANE_TILES = D_SSM // LANES
STATE_COLS = N_SSM_GROUPS * SSM_STATE
TILE_STATE = STATE_COLS // N_LANE_TILES
VMEM_LIMIT = 56 * 1024 * 1024

SC_CORES = 2
SC_SUBCORES = 16
SC_LANES = 16
SC_WORKERS = SC_CORES * SC_SUBCORES

FRONT_TL = 512
SSM_LC = 256
SSM_BLK = 4
COEF_LB_RE, COEF_LB_IM, COEF_LBLK_RE, COEF_LBLK_IM = 0, 1, 2, 3
TOK_TM = 512
FINAL_TM = 1024
EXP_UNIT = 128
EXP_CLASSES = 8
W_SPLIT = 1
DISPATCH_CHUNK = 80
COMBINE_CHUNK = 24

R_E1, R_E2, R_W1, R_W2, R_RANK1, R_RANK2, R_CODE1, R_CODE2 = 0, 1, 2, 3, 4, 5, 6, 7
CODE_BITS = 16
CODE_SHIFT = float(1 << CODE_BITS)

_INV_SQRT2 = 1.0 / math.sqrt(2.0)
_BF16 = jnp.bfloat16
_F32 = jnp.float32
_U32 = jnp.uint32


def _gelu(x):
    return 0.5 * x * (1.0 + lax.erf(x * _INV_SQRT2))


def _rms(x, g):
    return x * lax.rsqrt(jnp.mean(x * x, axis=-1, keepdims=True) + EPS) * g


def _dot(a, b):
    return jnp.dot(a, b, preferred_element_type=_F32)


def _dot_f32(a, b):
    return jnp.dot(a, b, preferred_element_type=_F32, precision=lax.Precision.HIGHEST)


def _dot_hi(a, b):
    def split(x):
        hi = x.astype(_BF16)
        return hi, (x - hi.astype(_F32)).astype(_BF16)

    a_hi, a_lo = split(a)
    b_hi, b_lo = split(b)
    return _dot(a_hi, b_hi) + _dot(a_hi, b_lo) + _dot(a_lo, b_hi)


def _pack_bf16_pair(x):
    w = x.shape[1] // 2
    hi = lax.bitcast_convert_type(x[:, :w].astype(_BF16).astype(_F32), _U32)
    lo = lax.bitcast_convert_type(x[:, w:].astype(_BF16).astype(_F32), _U32)
    return hi | (lo >> 16)


def _unpack_bf16_pair(p):
    hi = lax.bitcast_convert_type(p & jnp.uint32(0xFFFF0000), _F32)
    lo = lax.bitcast_convert_type(p << 16, _F32)
    return jnp.concatenate([hi, lo], axis=-1)


def _head_norm_gelu(vb, gn):
    v = _gelu(vb)
    parts = []
    for h in range(N_GMLP_HEADS):
        vh = v[:, h * GMLP_HEAD:(h + 1) * GMLP_HEAD]
        parts.append(vh * lax.rsqrt(jnp.mean(vh * vh, axis=-1, keepdims=True) + EPS))
    return jnp.concatenate(parts, axis=-1) * gn


def _front_prompt_kernel(x_ref, g1_ref, win_ref, gn_ref, ws_ref, bs_ref, gog_ref,
                         xa_ref, sg_ref, mixb_ref, win_bf, z_ref):
    @pl.when(pl.program_id(0) == 0)
    def _():
        win_bf[...] = win_ref[...].astype(_BF16)
        z_ref[...] = jnp.zeros_like(z_ref)

    z = z_ref[...]
    x = x_ref[0]
    hn = _rms(x, g1_ref[...]).astype(_BF16)
    z_ref[...] = _dot(hn, win_bf[...])
    xa_ref[0] = z[:, :D_SSM]
    sg_ref[0] = jax.nn.sigmoid(z[:, D_SSM:2 * D_SSM])
    ub = _gelu(z[:, 2 * D_SSM:2 * D_SSM + D_GMLP])
    vbn = _head_norm_gelu(z[:, 2 * D_SSM + D_GMLP:], gn_ref[...]).astype(_BF16)
    tl = x.shape[0]
    rows = []
    for c in range(tl // CHUNK):
        heads = []
        for h in range(N_GMLP_HEADS):
            vh = vbn[c * CHUNK:(c + 1) * CHUNK, h * GMLP_HEAD:(h + 1) * GMLP_HEAD]
            heads.append(_dot(ws_ref[h], vh) + bs_ref[:, h:h + 1])
        rows.append(jnp.concatenate(heads, axis=-1))
    s = jnp.concatenate(rows, axis=0)
    mixb_ref[0] = _rms(ub * s, gog_ref[...]).astype(_BF16)


def _front_prompt(x, g1, win, gn, ws_tril_bf, bs_t, gog):
    n, l, d = x.shape
    tl = FRONT_TL
    per_seq = l // tl
    n_tiles = n * per_seq
    cur = lambda i: jnp.minimum(i, n_tiles - 1)
    prev = lambda i: jnp.maximum(i - 1, 0)
    const = lambda *shape: pl.BlockSpec(shape, lambda i: (0,) * len(shape))
    seq = lambda w, which: pl.BlockSpec((1, tl, w), lambda i: (which(i) // per_seq, which(i) % per_seq, 0))
    return pl.pallas_call(
        _front_prompt_kernel,
        grid=(n_tiles + 1,),
        in_specs=[seq(d, cur), const(1, d), const(d, D_IN), const(1, D_GMLP),
                  const(N_GMLP_HEADS, CHUNK, CHUNK), const(CHUNK, N_GMLP_HEADS), const(1, D_GMLP)],
        out_specs=[seq(D_SSM, prev), seq(D_SSM, prev), seq(D_GMLP, prev)],
        out_shape=[jax.ShapeDtypeStruct((n, l, D_SSM), _F32),
                   jax.ShapeDtypeStruct((n, l, D_SSM), _F32),
                   jax.ShapeDtypeStruct((n, l, D_GMLP), _BF16)],
        scratch_shapes=[pltpu.VMEM((d, D_IN), _BF16), pltpu.VMEM((tl, D_IN), _F32)],
        compiler_params=pltpu.CompilerParams(
            dimension_semantics=("arbitrary",), vmem_limit_bytes=VMEM_LIMIT),
        name="front_prompt",
    )(x, g1, win, gn, ws_tril_bf, bs_t, gog)


def _ssm_prompt_kernel(xa_ref, sg_ref, v_ref, r_ref, coef_ref, dsk_ref, gos_ref,
                       mixa_ref, hfin_ref, s_ref, st_ref):
    lc = xa_ref.shape[1]
    nblk = lc // SSM_BLK
    rows = nblk * SUBLANES

    @pl.when(pl.program_id(0) == 0)
    def _():
        st_ref[...] = jnp.zeros_like(st_ref)

    def by_position(ref):
        t = pltpu.einshape("btc->tbc", ref[...]).reshape(nblk, SSM_BLK, SUBLANES, D_SSM)
        return [t[:, i].reshape(rows, D_SSM) for i in range(SSM_BLK)]

    xs = by_position(xa_ref)
    xs_bf = [x.astype(_BF16) for x in xs]
    xk = [jnp.concatenate([x[:, k * LANES:(k + 1) * LANES] for x in xs_bf], axis=-1)
          for k in range(N_LANE_TILES)]
    for k in range(N_LANE_TILES):
        s_ref[:, 2 * TILE_STATE * k:2 * TILE_STATE * (k + 1)] = _dot(xk[k], v_ref[k])

    for kk in range(0, N_LANE_TILES, 2):
        tiles = (kk, kk + 1)
        cols = [(2 * TILE_STATE * k, 2 * TILE_STATE * k + TILE_STATE) for k in tiles]
        lbs = [tuple(jnp.broadcast_to(coef_ref[row:row + 1, k * TILE_STATE:(k + 1) * TILE_STATE],
                                      (SUBLANES, TILE_STATE)) for row in (COEF_LBLK_RE, COEF_LBLK_IM))
               for k in tiles]

        def body(j, carry, cols=cols, lbs=lbs):
            r0 = pl.multiple_of(j * SUBLANES, SUBLANES)
            out = []
            for q, ((c_re, c_im), (lr, li)) in enumerate(zip(cols, lbs)):
                hr, hi = carry[2 * q], carry[2 * q + 1]
                sr = s_ref[pl.ds(r0, SUBLANES), c_re:c_re + TILE_STATE]
                si = s_ref[pl.ds(r0, SUBLANES), c_im:c_im + TILE_STATE]
                s_ref[pl.ds(r0, SUBLANES), c_re:c_re + TILE_STATE] = hr
                s_ref[pl.ds(r0, SUBLANES), c_im:c_im + TILE_STATE] = hi
                out += [lr * hr - li * hi + sr, lr * hi + li * hr + si]
            return tuple(out)

        init = tuple(st_ref[:, c:c + TILE_STATE] for c_pair in cols for c in c_pair)
        fin = lax.fori_loop(0, nblk, body, init, unroll=2)
        for q, (c_re, c_im) in enumerate(cols):
            st_ref[:, c_re:c_re + TILE_STATE] = fin[2 * q]
            st_ref[:, c_im:c_im + TILE_STATE] = fin[2 * q + 1]

    yk = []
    for k in range(N_LANE_TILES):
        h_in = s_ref[:, 2 * TILE_STATE * k:2 * TILE_STATE * (k + 1)].astype(_BF16)
        yk.append(_dot(jnp.concatenate([h_in, xk[k]], axis=-1), r_ref[k]))
    sgs = by_position(sg_ref)
    outs = []
    for i in range(SSM_BLK):
        y = jnp.concatenate([y_k[:, i * LANES:(i + 1) * LANES] for y_k in yk], axis=-1) + dsk_ref[...] * xs[i]
        outs.append(_rms(_gelu(y) * sgs[i], gos_ref[...]).reshape(nblk, SUBLANES, D_SSM))
    mixa = jnp.stack(outs, axis=1).reshape(lc, SUBLANES, D_SSM)
    mixa_ref[...] = pltpu.einshape("tbc->btc", mixa).astype(_BF16)
    hfin_ref[...] = st_ref[...]


def _ssm_prompt(xa, sg, v, r, coef, dsk, gos):
    n, l, _ = xa.shape
    lc = SSM_LC
    const = lambda *shape: pl.BlockSpec(shape, lambda i: (0,) * len(shape))
    seq_spec = pl.BlockSpec((n, lc, D_SSM), lambda i: (0, i, 0))
    return pl.pallas_call(
        _ssm_prompt_kernel,
        grid=(l // lc,),
        in_specs=[seq_spec, seq_spec, const(*v.shape), const(*r.shape),
                  const(*coef.shape), const(1, D_SSM), const(1, D_SSM)],
        out_specs=[seq_spec, const(n, 2 * STATE_COLS)],
        out_shape=[jax.ShapeDtypeStruct((n, l, D_SSM), _BF16),
                   jax.ShapeDtypeStruct((n, 2 * STATE_COLS), _F32)],
        scratch_shapes=[pltpu.VMEM((lc // SSM_BLK * n, 2 * STATE_COLS), _F32),
                        pltpu.VMEM((n, 2 * STATE_COLS), _F32)],
        compiler_params=pltpu.CompilerParams(
            dimension_semantics=("arbitrary",), vmem_limit_bytes=VMEM_LIMIT),
        name="ssm_prompt",
    )(xa, sg, v, r, coef, dsk, gos)


def _front_sample_kernel(x_ref, g1_ref, win_ref, gn_ref, w00_ref, b0_ref, gog_ref,
                         wb_ref, wc_ref, coef_ref, dsk_ref, gos_ref, h0r_ref, h0i_ref,
                         mix_ref, hr_ref, hi_ref, vrow_ref):
    x = x_ref[...]
    hn = _rms(x, g1_ref[...])
    z = _dot_hi(hn, win_ref[...])
    xa = z[:, :D_SSM]
    ys = []
    for k in range(N_LANE_TILES):
        bu = _dot_hi(xa[:, k * LANES:(k + 1) * LANES], wb_ref[k])
        sl = slice(k * TILE_STATE, (k + 1) * TILE_STATE)
        lr, li = coef_ref[COEF_LB_RE:COEF_LB_RE + 1, sl], coef_ref[COEF_LB_IM:COEF_LB_IM + 1, sl]
        h0r, h0i = h0r_ref[:, sl], h0i_ref[:, sl]
        nr = lr * h0r - li * h0i + bu[:, :TILE_STATE]
        ni = lr * h0i + li * h0r + bu[:, TILE_STATE:]
        hr_ref[:, sl] = nr
        hi_ref[:, sl] = ni
        ys.append(_dot_hi(jnp.concatenate([nr, ni], axis=-1), wc_ref[k]))
    y = jnp.concatenate(ys, axis=-1) + dsk_ref[...] * xa
    ya = _gelu(y) * jax.nn.sigmoid(z[:, D_SSM:2 * D_SSM])
    mix_ref[:, :D_SSM] = _rms(ya, gos_ref[...])
    ub = _gelu(z[:, 2 * D_SSM:2 * D_SSM + D_GMLP])
    vbn = _head_norm_gelu(z[:, 2 * D_SSM + D_GMLP:], gn_ref[...])
    vrow_ref[...] = vbn
    s = w00_ref[...] * vbn + b0_ref[...]
    mix_ref[:, D_SSM:] = _rms(ub * s, gog_ref[...])


def _front_sample(x, g1, win, gn, w00, b0, gog, wb, wc, coef, dsk, gos, h0r, h0i):
    n = x.shape[0]
    vmem = pl.BlockSpec(memory_space=pltpu.VMEM)
    return pl.pallas_call(
        _front_sample_kernel,
        in_specs=[vmem] * 14,
        out_specs=[vmem] * 4,
        out_shape=[jax.ShapeDtypeStruct((n, D_MODEL), _F32),
                   jax.ShapeDtypeStruct((n, STATE_COLS), _F32),
                   jax.ShapeDtypeStruct((n, STATE_COLS), _F32),
                   jax.ShapeDtypeStruct((n, D_GMLP), _F32)],
        compiler_params=pltpu.CompilerParams(vmem_limit_bytes=VMEM_LIMIT),
        name="front_sample",
    )(x, g1, win, gn, w00, b0, gog, wb, wc, coef, dsk, gos, h0r, h0i)


def _route(logits, base):
    tm = logits.shape[0]
    lt = logits.T
    ex = lt[:N_EXPERTS, :]
    gr = lt[N_EXPERTS:N_EXPERTS + SUBLANES, :]
    row_e = lax.broadcasted_iota(jnp.int32, ex.shape, 0).astype(_F32)
    row_g = lax.broadcasted_iota(jnp.int32, gr.shape, 0).astype(_F32)
    neg = jnp.float32(-jnp.inf)
    big = jnp.float32(LANES)
    is_g = row_g < N_EXPERT_GROUPS
    gl = jnp.where(is_g, gr, neg)
    gmax = jnp.max(gl, axis=0, keepdims=True)
    gi = jnp.min(jnp.where(gl == gmax, row_g, big), axis=0, keepdims=True)
    p_top = 1.0 / jnp.sum(jnp.where(is_g, jnp.exp(gl - gmax), 0.0), axis=0, keepdims=True)
    lo = gi * EXPERTS_PER_GROUP
    in_grp = (row_e >= lo) & (row_e < lo + EXPERTS_PER_GROUP)
    m1 = jnp.max(jnp.where(in_grp, ex, neg), axis=0, keepdims=True)
    i1 = jnp.min(jnp.where(in_grp & (ex == m1), row_e, big), axis=0, keepdims=True)
    rest = in_grp & (row_e != i1)
    m2 = jnp.max(jnp.where(rest, ex, neg), axis=0, keepdims=True)
    i2 = jnp.min(jnp.where(rest & (ex == m2), row_e, big), axis=0, keepdims=True)
    e2 = jnp.exp(m2 - m1)
    w1 = p_top / (1.0 + e2)
    w2 = p_top * e2 / (1.0 + e2)
    sel1 = row_e == i1
    sel2 = row_e == i2
    hits = jnp.where(sel1 | sel2, 1.0, 0.0)
    src = lax.broadcasted_iota(jnp.int32, (tm, tm), 0)
    dst = lax.broadcasted_iota(jnp.int32, (tm, tm), 1)
    before = _dot(hits.astype(_BF16), jnp.where(src < dst, 1.0, 0.0).astype(_BF16)) + base
    rank1 = jnp.sum(jnp.where(sel1, before, 0.0), axis=0, keepdims=True)
    rank2 = jnp.sum(jnp.where(sel2, before, 0.0), axis=0, keepdims=True)
    fields = {R_E1: i1, R_E2: i2, R_W1: w1, R_W2: w2, R_RANK1: rank1, R_RANK2: rank2,
              R_CODE1: i1 * CODE_SHIFT + rank1, R_CODE2: i2 * CODE_SHIFT + rank2}
    row8 = lax.broadcasted_iota(jnp.int32, (SUBLANES, tm), 0)
    route_t = jnp.zeros((SUBLANES, tm), _F32)
    for r, val in fields.items():
        route_t = jnp.where(row8 == r, val, route_t)
    route = jnp.concatenate([route_t, jnp.zeros((LANES - SUBLANES, tm), _F32)], axis=0).T
    return route_t, route, base + jnp.sum(hits, axis=1, keepdims=True)


def _mixer_out_prompt_kernel(x_ref, mixa_ref, mixb_ref, wo_ref, g2_ref, wr_ref, br_ref,
                             x1_ref, xn_ref, route_ref, route_t_ref, cnt_ref, base_ref, logits_ref, wo_bf):
    i = pl.program_id(0)

    @pl.when(i == 0)
    def _():
        base_ref[...] = jnp.zeros_like(base_ref)
        logits_ref[...] = jnp.zeros_like(logits_ref)
        wo_bf[...] = wo_ref[...].astype(_BF16)

    prev_logits = logits_ref[...]
    x1 = x_ref[0] + _dot(mixa_ref[0], wo_bf[:D_SSM, :]) + _dot(mixb_ref[0], wo_bf[D_SSM:, :])
    xn = _rms(x1, g2_ref[...])
    x1_ref[...] = _pack_bf16_pair(x1)
    xn_ref[...] = _pack_bf16_pair(xn)
    logits_ref[...] = _dot(xn.astype(_BF16), wr_ref[...]) + br_ref[...]
    route_t, route, base = _route(prev_logits, base_ref[...])
    route_ref[...] = route
    route_t_ref[...] = route_t
    base = jnp.where(i >= 1, base, base_ref[...])
    base_ref[...] = base
    cnt_ref[...] = base


def _mixer_out_sample_kernel(x_ref, mix_ref, wo_ref, g2_ref, wr_ref, br_ref, cnt_in_ref,
                             x1_in, xn_in, route_in, route_t_in,
                             x1_ref, xn_ref, route_ref, route_t_ref, cnt_ref):
    del x1_in, xn_in, route_in, route_t_in
    x1 = (x_ref[...] + _dot_hi(mix_ref[:, :D_SSM], wo_ref[:D_SSM, :])
          + _dot_hi(mix_ref[:, D_SSM:], wo_ref[D_SSM:, :]))
    xn = _rms(x1, g2_ref[...])
    logits = _dot_hi(xn, wr_ref[...]) + br_ref[...]
    route_t, route, base = _route(logits, cnt_in_ref[...])
    x1_ref[...] = _pack_bf16_pair(x1)
    xn_ref[...] = _pack_bf16_pair(xn)
    route_ref[...] = route
    route_t_ref[...] = route_t
    cnt_ref[...] = base


def _mixer_out(x_p, mixa, mixb, x_s, mix_s, wo, g2, wr, br):
    n, l, d = x_p.shape
    ns = x_s.shape[0]
    t_all = n * l + ns
    tm = TOK_TM
    per_seq = l // tm
    n_tiles = n * per_seq
    cur = lambda i: jnp.minimum(i, n_tiles - 1)
    prev = lambda i: jnp.maximum(i - 1, 0)
    const = lambda *shape: pl.BlockSpec(shape, lambda i: (0,) * len(shape))
    seq = lambda w: pl.BlockSpec((1, tm, w), lambda i: (cur(i) // per_seq, cur(i) % per_seq, 0))
    tok = lambda w, which: pl.BlockSpec((tm, w), lambda i: (which(i), 0))
    tok_shapes = [jax.ShapeDtypeStruct((t_all, d // 2), _U32),
                  jax.ShapeDtypeStruct((t_all, d // 2), _U32),
                  jax.ShapeDtypeStruct((t_all, LANES), _F32),
                  jax.ShapeDtypeStruct((SUBLANES, t_all), _F32)]
    cnt_shape = jax.ShapeDtypeStruct((N_EXPERTS, 1), _F32)
    x1, xn, route, route_t, cnt = pl.pallas_call(
        _mixer_out_prompt_kernel,
        grid=(n_tiles + 1,),
        in_specs=[seq(d), seq(D_SSM), seq(D_GMLP),
                  const(d, d), const(1, d), const(d, LANES), const(1, LANES)],
        out_specs=[tok(d // 2, cur), tok(d // 2, cur), tok(LANES, prev),
                   pl.BlockSpec((SUBLANES, tm), lambda i: (0, prev(i))), const(N_EXPERTS, 1)],
        out_shape=tok_shapes + [cnt_shape],
        scratch_shapes=[pltpu.VMEM((N_EXPERTS, 1), _F32), pltpu.VMEM((tm, LANES), _F32),
                        pltpu.VMEM((d, d), _BF16)],
        compiler_params=pltpu.CompilerParams(
            dimension_semantics=("arbitrary",), vmem_limit_bytes=VMEM_LIMIT),
        name="mixer_out_prompt",
    )(x_p, mixa, mixb, wo, g2, wr.astype(_BF16), br)
    tail = (n * l) // ns
    c1 = lambda *shape: pl.BlockSpec(shape, lambda i: (0,) * len(shape))
    anyspec = pl.BlockSpec(memory_space=pl.ANY)
    tail_spec = lambda w: pl.BlockSpec((ns, w), lambda i: (tail, 0))
    return pl.pallas_call(
        _mixer_out_sample_kernel,
        grid=(1,),
        in_specs=[c1(ns, d), c1(ns, d), c1(d, d), c1(1, d), c1(d, LANES), c1(1, LANES), c1(N_EXPERTS, 1),
                  anyspec, anyspec, anyspec, anyspec],
        out_specs=[tail_spec(d // 2), tail_spec(d // 2), tail_spec(LANES),
                   pl.BlockSpec((SUBLANES, ns), lambda i: (0, tail)), c1(N_EXPERTS, 1)],
        out_shape=tok_shapes + [cnt_shape],
        input_output_aliases={7: 0, 8: 1, 9: 2, 10: 3},
        compiler_params=pltpu.CompilerParams(
            dimension_semantics=("arbitrary",), vmem_limit_bytes=VMEM_LIMIT),
        name="mixer_out_sample",
    )(x_s, mix_s, wo, g2, wr, br, cnt, x1, xn, route, route_t)


def _sc_stream(n_chunks, gather, write):
    gather(0).start()
    for j in range(n_chunks):
        if j + 1 < n_chunks:
            if j >= 1:
                write(j - 1).wait()
            gather(j + 1).start()
        gather(j).wait()
        write(j).start()
    if n_chunks >= 2:
        write(n_chunks - 2).wait()
    write(n_chunks - 1).wait()


def _sc_mesh():
    return plsc.VectorSubcoreMesh(core_axis_name="c", subcore_axis_name="s",
                                  num_cores=SC_CORES, num_subcores=SC_SUBCORES)


def _sc_buffers(chunk, w, dtype):
    return [pltpu.VMEM((chunk, w), dtype), pltpu.VMEM((chunk, w), dtype)] + [pltpu.SemaphoreType.DMA] * 4


def _sc_combine(table, idx, n_out, chunk):
    w = table.shape[1]
    rows_w = n_out // SC_WORKERS
    n_chunks = rows_w // chunk
    assert rows_w * SC_WORKERS == n_out and n_chunks * chunk == rows_w and rows_w % SUBLANES == 0

    def body(table_hbm, idx_hbm, out_hbm, idx_v, buf0, buf1, g0, g1, w0, w1):
        wid = lax.axis_index("s") * SC_CORES + lax.axis_index("c")
        base = pl.multiple_of(wid * rows_w, SUBLANES)
        pltpu.sync_copy(idx_hbm.at[pl.ds(base, rows_w)], idx_v)
        bufs, gsems, wsems = (buf0, buf1), (g0, g1), (w0, w1)

        def gather(j):
            return pltpu.make_async_copy(table_hbm.at[idx_v.at[pl.ds(j * chunk, chunk)]], bufs[j % 2], gsems[j % 2])

        def write(j):
            return pltpu.make_async_copy(bufs[j % 2], out_hbm.at[pl.ds(base + j * chunk, chunk)], wsems[j % 2])

        _sc_stream(n_chunks, gather, write)

    return pl.kernel(
        body,
        out_type=jax.ShapeDtypeStruct((n_out, w), table.dtype),
        mesh=_sc_mesh(),
        scratch_types=[pltpu.VMEM((rows_w,), jnp.int32)] + _sc_buffers(chunk, w, table.dtype),
        compiler_params=pltpu.CompilerParams(use_tc_tiling_on_sc=True),
        name="sc_combine",
    )(table, idx)


def _sc_dispatch(table, codes, start_row, n_out, chunk):
    t_all, w = table.shape
    n_pad = codes.shape[0]
    n_ent = 2 * t_all
    ent_w = n_pad // SC_WORKERS
    n_chunks = ent_w // chunk
    per_chunk = chunk // SC_LANES
    trash = n_out - (n_pad - n_ent)
    assert ent_w * SC_WORKERS == n_pad and n_chunks * chunk == ent_w
    assert per_chunk * SC_LANES == chunk and chunk <= LANES and n_pad - n_ent <= t_all

    def body(table_hbm, code_hbm, start_hbm, out_hbm, dest_hbm,
             code_v, dest_v, tok_v, dst_v, start_v, buf0, buf1, g0, g1, w0, w1):
        wid = lax.axis_index("s") * SC_CORES + lax.axis_index("c")
        ebase = pl.multiple_of(wid * ent_w, SUBLANES)
        pltpu.sync_copy(code_hbm.at[pl.ds(ebase, ent_w)], code_v)
        pltpu.sync_copy(start_hbm, start_v)
        lane = lax.iota(jnp.int32, SC_LANES)
        for j in range(n_chunks):
            for c in range(per_chunk):
                off = j * chunk + c * SC_LANES
                ent = ebase + off + lane
                code = code_v[pl.ds(off, SC_LANES)]
                d = plsc.load_gather(start_v, [code >> CODE_BITS]) + (code & ((1 << CODE_BITS) - 1))
                d = jnp.where(ent >= n_ent, trash + (ent - n_ent), d)
                tok = jnp.where(ent >= t_all, ent - t_all, ent)
                tok = jnp.where(tok >= t_all, tok - t_all, tok)
                dest_v[pl.ds(off, SC_LANES)] = d
                dst_v[j, pl.ds(c * SC_LANES, SC_LANES)] = d
                tok_v[j, pl.ds(c * SC_LANES, SC_LANES)] = tok
        pltpu.sync_copy(dest_v, dest_hbm.at[pl.ds(ebase, ent_w)])
        bufs, gsems, wsems = (buf0, buf1), (g0, g1), (w0, w1)

        def gather(j):
            return pltpu.make_async_copy(table_hbm.at[tok_v.at[j]], bufs[j % 2], gsems[j % 2])

        def scatter(j):
            return pltpu.make_async_copy(bufs[j % 2], out_hbm.at[dst_v.at[j]], wsems[j % 2])

        _sc_stream(n_chunks, gather, scatter)

    return pl.kernel(
        body,
        out_type=(jax.ShapeDtypeStruct((n_out, w), table.dtype), jax.ShapeDtypeStruct((n_pad,), jnp.int32)),
        mesh=_sc_mesh(),
        scratch_types=([pltpu.VMEM((ent_w,), jnp.int32), pltpu.VMEM((ent_w,), jnp.int32),
                        pltpu.VMEM((n_chunks, chunk), jnp.int32), pltpu.VMEM((n_chunks, chunk), jnp.int32),
                        pltpu.VMEM((LANES,), jnp.int32)] + _sc_buffers(chunk, w, table.dtype)),
        compiler_params=pltpu.CompilerParams(use_tc_tiling_on_sc=True, needs_layout_passes=False),
        name="sc_dispatch",
    )(table, codes, start_row)


def _experts_kernel(piece_start_ref, piece_row_ref, piece_cls_ref, *refs):
    n_w = 3 * W_SPLIT
    wg_refs, wu_refs, wd_refs = refs[:W_SPLIT], refs[W_SPLIT:2 * W_SPLIT], refs[2 * W_SPLIT:n_w]
    xs_hbm, ys_hbm, wg_bf, wu_bf, wd_bf, xbuf, ybuf, xsem, ysem = refs[n_w:]
    e = pl.program_id(0)
    g0 = piece_start_ref[e]
    n_here = piece_start_ref[e + 1] - g0
    n_total = piece_start_ref[N_EXPERTS]

    def per_class(g, fn):
        cls = piece_cls_ref[g]
        row = pl.multiple_of(piece_row_ref[g], EXP_UNIT)
        for c in range(1, EXP_CLASSES + 1):
            pl.when(cls == c)(lambda c=c: fn(c * EXP_UNIT, row))

    def x_copy(slot, rows, row):
        return pltpu.make_async_copy(xs_hbm.at[pl.ds(row, rows)], xbuf.at[slot, pl.ds(0, rows)], xsem.at[slot])

    def y_copy(slot, rows, row):
        return pltpu.make_async_copy(ybuf.at[slot, pl.ds(0, rows)], ys_hbm.at[pl.ds(row, rows)], ysem.at[slot])

    @pl.when((e == 0) & (n_total > 0))
    def _():
        per_class(0, lambda rows, row: x_copy(0, rows, row).start())

    for dst, chunks in ((wg_bf, wg_refs), (wu_bf, wu_refs), (wd_bf, wd_refs)):
        rows = dst.shape[0] // W_SPLIT
        for q, src in enumerate(chunks):
            dst[q * rows:(q + 1) * rows, :] = src[0, 0].astype(_BF16)

    def piece(j, carry):
        g = g0 + j
        slot = lax.rem(g, 2)
        per_class(g, lambda rows, row: x_copy(slot, rows, row).wait())

        @pl.when(g + 1 < n_total)
        def _():
            per_class(g + 1, lambda rows, row: x_copy(1 - slot, rows, row).start())

        @pl.when(g >= 2)
        def _():
            per_class(g - 2, lambda rows, row: y_copy(slot, rows, row).wait())

        def compute(rows, row):
            x = _unpack_bf16_pair(xbuf[slot, pl.ds(0, rows)]).astype(_BF16)
            a = _dot(x, wg_bf[...])
            u = _dot(x, wu_bf[...])
            h = (a * jax.nn.sigmoid(a) * u).astype(_BF16)
            ybuf[slot, pl.ds(0, rows)] = _pack_bf16_pair(_dot(h, wd_bf[...]))
            y_copy(slot, rows, row).start()

        per_class(g, compute)
        return carry

    lax.fori_loop(0, n_here, piece, 0)

    @pl.when(e == N_EXPERTS - 1)
    def _():
        @pl.when(n_total >= 2)
        def _():
            per_class(n_total - 2, lambda rows, row: y_copy(lax.rem(n_total, 2), rows, row).wait())

        @pl.when(n_total >= 1)
        def _():
            per_class(n_total - 1, lambda rows, row: y_copy(lax.rem(n_total - 1, 2), rows, row).wait())


def _experts(piece_start, piece_row, piece_cls, n_rows, xs, w_gate, w_up, w_down):
    dh = xs.shape[1]
    d = 2 * dh
    tm = EXP_UNIT * EXP_CLASSES
    anyspec = pl.BlockSpec(memory_space=pl.ANY)

    def chunk_specs(rows, cols):
        return [pl.BlockSpec((1, 1, rows // W_SPLIT, cols), lambda e, ps, pr, pc, q=q: (e, q, 0, 0))
                for q in range(W_SPLIT)]

    split = lambda w: w.reshape(w.shape[0], W_SPLIT, w.shape[1] // W_SPLIT, w.shape[2])
    grid_spec = pltpu.PrefetchScalarGridSpec(
        num_scalar_prefetch=3,
        grid=(N_EXPERTS,),
        in_specs=(chunk_specs(d, D_EXPERT) + chunk_specs(d, D_EXPERT) + chunk_specs(D_EXPERT, d) + [anyspec]),
        out_specs=anyspec,
        scratch_shapes=[pltpu.VMEM((d, D_EXPERT), _BF16), pltpu.VMEM((d, D_EXPERT), _BF16),
                        pltpu.VMEM((D_EXPERT, d), _BF16),
                        pltpu.VMEM((2, tm, dh), _U32), pltpu.VMEM((2, tm, dh), _U32),
                        pltpu.SemaphoreType.DMA((2,)), pltpu.SemaphoreType.DMA((2,))],
    )
    return pl.pallas_call(
        _experts_kernel,
        grid_spec=grid_spec,
        out_shape=jax.ShapeDtypeStruct((n_rows, dh), _U32),
        compiler_params=pltpu.CompilerParams(
            dimension_semantics=("arbitrary",), vmem_limit_bytes=VMEM_LIMIT),
        name="experts",
    )(piece_start, piece_row, piece_cls, *([split(w_gate)] * W_SPLIT), *([split(w_up)] * W_SPLIT),
      *([split(w_down)] * W_SPLIT), xs)


def _final_kernel(x1_ref, ya_ref, yb_ref, route_ref, gf_ref, y_ref):
    route = route_ref[...]
    x2 = (_unpack_bf16_pair(x1_ref[...]) + route[:, R_W1:R_W1 + 1] * _unpack_bf16_pair(ya_ref[...])
          + route[:, R_W2:R_W2 + 1] * _unpack_bf16_pair(yb_ref[...]))
    y_ref[...] = _rms(x2, gf_ref[...])


def _final(x1, yab, route, gf, n_prompt, n_sample):
    d = 2 * x1.shape[1]

    def call(tm, first_block, n_rows, name):
        tok = lambda w: pl.BlockSpec((tm, w), lambda i: (first_block + i, 0))
        sel = lambda k: pl.BlockSpec((None, tm, d // 2), lambda i: (k, first_block + i, 0))
        return pl.pallas_call(
            _final_kernel,
            grid=(n_rows // tm,),
            in_specs=[tok(d // 2), sel(0), sel(1), tok(LANES), pl.BlockSpec((1, d), lambda i: (0, 0))],
            out_specs=pl.BlockSpec((tm, d), lambda i: (i, 0)),
            out_shape=jax.ShapeDtypeStruct((n_rows, d), _F32),
            compiler_params=pltpu.CompilerParams(
                dimension_semantics=("arbitrary",), vmem_limit_bytes=VMEM_LIMIT),
            name=name,
        )(x1, yab, yab, route, gf)

    return (call(FINAL_TM, 0, n_prompt, "final_prompt"),
            call(n_sample, n_prompt // n_sample, n_sample, "final_sample"))


def _powers(lam_re, lam_im, dt):
    out = []
    for m in range(SSM_BLK + 1):
        mag = jnp.exp(m * lam_re * dt)
        ang = m * lam_im * dt
        out.append((mag * jnp.cos(ang), mag * jnp.sin(ang)))
    return out


def _spread(x, copies):
    w = x.shape[1]
    src = lax.broadcasted_iota(jnp.int32, (w, w * copies), 0)
    dst = lax.broadcasted_iota(jnp.int32, (w, w * copies), 1)
    return _dot_f32(x, jnp.where(dst % w == src, 1.0, 0.0))


def _ssm_prep_kernel(lam_ref, b_re, b_im, c_re, c_im, v_ref, r_ref, wb_ref, wc_ref, coef_ref):
    n_p, n_h = SSM_STATE, SSM_GROUP
    lr, li, dt = lam_ref[0:1, :], lam_ref[1:2, :], lam_ref[2:3, :]
    pw = _powers(lr, li, dt)
    den = lr * lr + li * li
    nr, ni = pw[1][0] - 1.0, pw[1][1]
    k_re = (nr * lr + ni * li) / den
    k_im = (ni * lr - nr * li) / den
    coef_ref[...] = jnp.concatenate(
        [pw[1][0], pw[1][1], pw[SSM_BLK][0], pw[SSM_BLK][1], jnp.zeros((SUBLANES - 4, TILE_STATE), _F32)], axis=0)

    on_diag_b = (lax.broadcasted_iota(jnp.int32, (TILE_STATE, LANES), 0) // n_p
                 == lax.broadcasted_iota(jnp.int32, (TILE_STATE, LANES), 1) // n_h)
    rows_gp = lambda ref: ref[...].reshape(TILE_STATE, n_h)
    bt_re = jnp.where(on_diag_b, _spread(rows_gp(b_re), SUBLANES), 0.0).T
    bt_im = jnp.where(on_diag_b, _spread(rows_gp(b_im), SUBLANES), 0.0).T
    bb_re = k_re * bt_re - k_im * bt_im
    bb_im = k_re * bt_im + k_im * bt_re
    wb_ref[0] = jnp.concatenate([bb_re, bb_im], axis=1)
    v_rows = []
    for s in range(SSM_BLK):
        pr, pi = pw[SSM_BLK - 1 - s]
        v_rows.append(jnp.concatenate([pr * bb_re - pi * bb_im, pr * bb_im + pi * bb_re], axis=1))
    v_ref[0] = jnp.concatenate(v_rows, axis=0).astype(v_ref.dtype)

    on_diag_c = (lax.broadcasted_iota(jnp.int32, (LANES, TILE_STATE), 0) // n_h
                 == lax.broadcasted_iota(jnp.int32, (LANES, TILE_STATE), 1) // n_p)
    rows_gh = lambda ref: ref[...].reshape(LANES, n_p)
    ct_re = jnp.where(on_diag_c, _spread(rows_gh(c_re), SUBLANES), 0.0)
    ct_im = jnp.where(on_diag_c, _spread(rows_gh(c_im), SUBLANES), 0.0)
    cl = [(ct_re * pr - ct_im * pi, ct_re * pi + ct_im * pr) for pr, pi in pw]
    wc_ref[0] = jnp.concatenate([cl[0][0], -cl[0][1]], axis=1).T
    nt = lambda a, b: lax.dot_general(a, b, (((1,), (1,)), ((), ())), precision=lax.Precision.HIGHEST,
                                      preferred_element_type=_F32)
    direct = [nt(cl[m][0], bb_re) - nt(cl[m][1], bb_im) for m in range(SSM_BLK)]
    zero = jnp.zeros((LANES, LANES), _F32)
    rt = jnp.concatenate(
        [jnp.concatenate([cl[i + 1][0], -cl[i + 1][1]]
                         + [direct[i - s] if s <= i else zero for s in range(SSM_BLK)], axis=1)
         for i in range(SSM_BLK)], axis=0)
    r_ref[0] = rt.T.astype(r_ref.dtype)


def _ssm_params(lam_re, lam_im, log_dt, b_re, b_im, c_re, c_im, d_skip):
    n_g, n_p, n_h = N_SSM_GROUPS, SSM_STATE, SSM_GROUP
    dt = jnp.repeat(jnp.exp(log_dt), n_p)
    lam = jnp.zeros((SUBLANES, STATE_COLS), _F32).at[0].set(lam_re.reshape(-1)).at[1].set(
        lam_im.reshape(-1)).at[2].set(dt)
    groups = lambda r, c: pl.BlockSpec((SUBLANES, r, c), lambda k: (k, 0, 0))
    out3 = lambda rows, w: pl.BlockSpec((1, rows, w), lambda k: (k, 0, 0))
    cols = pl.BlockSpec((SUBLANES, TILE_STATE), lambda k: (0, k))
    k_blk = SSM_BLK * LANES
    v, r, wb, wc, coef = pl.pallas_call(
        _ssm_prep_kernel,
        grid=(N_LANE_TILES,),
        in_specs=[cols, groups(n_p, n_h), groups(n_p, n_h), groups(n_h, n_p), groups(n_h, n_p)],
        out_specs=[out3(k_blk, 2 * TILE_STATE), out3(2 * TILE_STATE + k_blk, k_blk),
                   out3(LANES, 2 * TILE_STATE), out3(2 * TILE_STATE, LANES), cols],
        out_shape=[jax.ShapeDtypeStruct((N_LANE_TILES, k_blk, 2 * TILE_STATE), _BF16),
                   jax.ShapeDtypeStruct((N_LANE_TILES, 2 * TILE_STATE + k_blk, k_blk), _BF16),
                   jax.ShapeDtypeStruct((N_LANE_TILES, LANES, 2 * TILE_STATE), _F32),
                   jax.ShapeDtypeStruct((N_LANE_TILES, 2 * TILE_STATE, LANES), _F32),
                   jax.ShapeDtypeStruct((SUBLANES, STATE_COLS), _F32)],
        compiler_params=pltpu.CompilerParams(
            dimension_semantics=("arbitrary",), vmem_limit_bytes=VMEM_LIMIT),
        name="ssm_prep",
    )(lam, b_re, b_im, c_re, c_im)
    return wb, wc, v, r, coef, d_skip.reshape(1, D_SSM)


def _dispatch_plan(route_t, cnt):
    t_all = route_t.shape[1]
    codes = route_t[R_CODE1:R_CODE2 + 1].astype(jnp.int32).reshape(-1)
    per_pass = SC_WORKERS * DISPATCH_CHUNK
    codes = jnp.pad(codes, (0, -(2 * t_all) % per_pass))
    counts = cnt[:, 0].astype(jnp.int32)
    zero = jnp.zeros((1,), jnp.int32)
    units = (counts + EXP_UNIT - 1) // EXP_UNIT
    unit_start = jnp.concatenate([zero, jnp.cumsum(units)])
    start_row = jnp.zeros((LANES,), jnp.int32).at[:N_EXPERTS].set(unit_start[:N_EXPERTS] * EXP_UNIT)
    pieces = (units + EXP_CLASSES - 1) // EXP_CLASSES
    piece_start = jnp.concatenate([zero, jnp.cumsum(pieces)])
    tm = EXP_UNIT * EXP_CLASSES
    max_units = (2 * t_all + N_EXPERTS * (EXP_UNIT - 1)) // EXP_UNIT
    max_pieces = (max_units + N_EXPERTS * (EXP_CLASSES - 1)) // EXP_CLASSES
    g = jnp.arange(max_pieces, dtype=jnp.int32)
    owner = ((g[:, None] >= piece_start[None, :-1]) & (g[:, None] < piece_start[None, 1:])).astype(jnp.int32)
    pick = lambda table: jnp.sum(owner * table[None, :], axis=1)
    first_unit = pick(unit_start[:-1]) + (g - pick(piece_start[:-1])) * EXP_CLASSES
    piece_row = first_unit * EXP_UNIT
    piece_cls = jnp.clip(pick(unit_start[1:]) - first_unit, 1, EXP_CLASSES)
    n_rows = (max_units * EXP_UNIT + tm - 1) // tm * tm + tm
    return codes, start_row, n_rows, piece_start, piece_row, piece_cls


def kernel(x_prompt, x_sample, state_ssm_re, state_ssm_im, norm1_g, w_in, lam_re, lam_im, log_dt, ssm_b_re, ssm_b_im, ssm_c_re, ssm_c_im, ssm_d, gmlp_norm_g, gmlp_w_s, gmlp_b_s, out_norm_ssm_g, out_norm_gmlp_g, w_out, norm2_g, w_router_group, b_router_group, w_router_expert, b_router_expert, w_gate, w_up, w_down, final_norm_g):
    n, l, d = x_prompt.shape
    ns = x_sample.shape[0]
    t_all = n * l + ns
    li = 0
    g1 = norm1_g[li].reshape(1, d)
    gn = gmlp_norm_g[li].reshape(1, D_GMLP)
    tril = jnp.tril(jnp.ones((CHUNK, CHUNK), dtype=bool))
    ws_tril = jnp.where(tril[None], gmlp_w_s[li], 0.0)
    bs = gmlp_b_s[li]
    gog = out_norm_gmlp_g[li].reshape(1, D_GMLP)
    gos = out_norm_ssm_g[li].reshape(1, D_SSM)
    wb, wc, v_blk, r_blk, coef, dsk = _ssm_params(
        lam_re[li], lam_im[li], log_dt[li], ssm_b_re[li], ssm_b_im[li], ssm_c_re[li], ssm_c_im[li], ssm_d[li])
    g2 = norm2_g[li].reshape(1, d)
    pad = LANES - N_EXPERTS - N_EXPERT_GROUPS
    wr = jnp.concatenate([w_router_expert[li], w_router_group[li], jnp.zeros((d, pad), _F32)], axis=1)
    br = jnp.concatenate([b_router_expert[li], b_router_group[li], jnp.zeros((pad,), _F32)]).reshape(1, LANES)

    xa, sg, mixb = _front_prompt(x_prompt, g1, w_in[li], gn, ws_tril.astype(_BF16), bs.T, gog)
    mixa, hfin = _ssm_prompt(xa, sg, v_blk, r_blk, coef, dsk, gos)
    w00 = jnp.repeat(ws_tril[:, 0, 0], GMLP_HEAD).reshape(1, D_GMLP)
    b0 = jnp.repeat(bs[:, 0], GMLP_HEAD).reshape(1, D_GMLP)
    mix_s, hr_s, hi_s, vrow = _front_sample(
        x_sample.reshape(ns, d), g1, w_in[li], gn, w00, b0, gog, wb, wc, coef, dsk, gos,
        state_ssm_re[li].reshape(ns, STATE_COLS), state_ssm_im[li].reshape(ns, STATE_COLS))

    x1, xn, route, route_t, cnt = _mixer_out(x_prompt, mixa, mixb, x_sample.reshape(ns, d), mix_s,
                                             w_out[li], g2, wr, br)
    codes, start_row, n_rows, piece_start, piece_row, piece_cls = _dispatch_plan(route_t, cnt)
    xs, dest = _sc_dispatch(xn, codes, start_row, n_rows, DISPATCH_CHUNK)
    ys = _experts(piece_start, piece_row, piece_cls, n_rows, xs, w_gate[li], w_up[li], w_down[li])
    yab = _sc_combine(ys, dest, 2 * t_all, COMBINE_CHUNK).reshape(2, t_all, d // 2)
    y_p, y_s = _final(x1, yab, route, final_norm_g.reshape(1, d), n * l, ns)

    hf = hfin.reshape(n, N_LANE_TILES, 2, 8, SSM_STATE)
    re_p = hf[:, :, 0].reshape(1, n, N_SSM_GROUPS, SSM_STATE)
    im_p = hf[:, :, 1].reshape(1, n, N_SSM_GROUPS, SSM_STATE)
    re_s = hr_s.reshape(1, ns, N_SSM_GROUPS, SSM_STATE)
    im_s = hi_s.reshape(1, ns, N_SSM_GROUPS, SSM_STATE)
    return (y_p.reshape(n, l, d), y_s.reshape(ns, 1, d), re_p, im_p, re_s, im_s,
            vrow.reshape(1, ns, 1, D_GMLP))
```

```python
import math

import jax
import jax.numpy as jnp
from jax import lax
from jax.experimental import pallas as pl
from jax.experimental.pallas import tpu as pltpu
from jax.experimental.pallas import tpu_sc as plsc

D_MODEL = 1024
D_SSM = 512
D_GMLP = 512
SSM_GROUP = 16
N_SSM_GROUPS = 32
SSM_STATE = 64
CHUNK = 128
N_GMLP_HEADS = 4
GMLP_HEAD = 128
N_EXPERT_GROUPS = 4
EXPERTS_PER_GROUP = 8
N_EXPERTS = 32
D_EXPERT = 512
D_IN = 2048
EPS = 1e-6

LANES = 128
SUBLANES = 8
N_LANE_TILES = D_SSM // LANES
STATE_COLS = N_SSM_GROUPS * SSM_STATE
TILE_STATE = STATE_COLS // N_LANE_TILES
VMEM_LIMIT = 56 * 1024 * 1024

SC_CORES = 2
SC_SUBCORES = 16
SC_LANES = 16
SC_WORKERS = SC_CORES * SC_SUBCORES

FRONT_TL = 512
SSM_LC = 256
SSM_BLK = 4
COEF_LB_RE, COEF_LB_IM, COEF_LBLK_RE, COEF_LBLK_IM = 0, 1, 2, 3
TOK_TM = 512
FINAL_TM = 1024
EXP_UNIT = 128
EXP_CLASSES = 8
DISPATCH_CHUNK = 80
COMBINE_CHUNK = 24

R_E1, R_E2, R_W1, R_W2, R_RANK1, R_RANK2, R_CODE1, R_CODE2 = 0, 1, 2, 3, 4, 5, 6, 7
CODE_BITS = 16
CODE_SHIFT = float(1 << CODE_BITS)

_INV_SQRT2 = 1.0 / math.sqrt(2.0)
_BF16 = jnp.bfloat16
_F32 = jnp.float32
_U32 = jnp.uint32


def _gelu(x):
    return 0.5 * x * (1.0 + lax.erf(x * _INV_SQRT2))


def _rms(x, g):
    return x * lax.rsqrt(jnp.mean(x * x, axis=-1, keepdims=True) + EPS) * g


def _dot(a, b):
    return jnp.dot(a, b, preferred_element_type=_F32)


def _dot_f32(a, b):
    return jnp.dot(a, b, preferred_element_type=_F32, precision=lax.Precision.HIGHEST)


def _dot_hi(a, b, transpose_b=False):
    def split(x):
        hi = x.astype(_BF16)
        return hi, (x - hi.astype(_F32)).astype(_BF16)

    dims = (((1,), (1 if transpose_b else 0,)), ((), ()))
    dot = lambda u, v: lax.dot_general(u, v, dims, preferred_element_type=_F32)
    a_hi, a_lo = split(a)
    b_hi, b_lo = split(b)
    return dot(a_hi, b_hi) + dot(a_hi, b_lo) + dot(a_lo, b_hi)


def _pack_bf16_pair(x):
    w = x.shape[1] // 2
    hi = lax.bitcast_convert_type(x[:, :w].astype(_BF16).astype(_F32), _U32)
    lo = lax.bitcast_convert_type(x[:, w:].astype(_BF16).astype(_F32), _U32)
    return hi | (lo >> 16)


def _unpack_bf16_pair(p):
    hi = lax.bitcast_convert_type(p & jnp.uint32(0xFFFF0000), _F32)
    lo = lax.bitcast_convert_type(p << 16, _F32)
    return jnp.concatenate([hi, lo], axis=-1)


def _head_norm_gelu(vb, gn):
    v = _gelu(vb)
    parts = []
    for h in range(N_GMLP_HEADS):
        vh = v[:, h * GMLP_HEAD:(h + 1) * GMLP_HEAD]
        parts.append(vh * lax.rsqrt(jnp.mean(vh * vh, axis=-1, keepdims=True) + EPS))
    return jnp.concatenate(parts, axis=-1) * gn


def _front_prompt_kernel(x_ref, g1_ref, win_ref, gn_ref, ws_ref, bs_ref, gog_ref,
                         xa_ref, sg_ref, mixb_ref, win_bf, z_ref):
    @pl.when(pl.program_id(0) == 0)
    def _():
        win_bf[...] = win_ref[...].astype(_BF16)
        z_ref[...] = jnp.zeros_like(z_ref)

    z = z_ref[...]
    x = x_ref[0]
    hn = _rms(x, g1_ref[...]).astype(_BF16)
    z_ref[...] = _dot(hn, win_bf[...])
    xa_ref[0] = z[:, :D_SSM]
    sg_ref[0] = jax.nn.sigmoid(z[:, D_SSM:2 * D_SSM])
    ub = _gelu(z[:, 2 * D_SSM:2 * D_SSM + D_GMLP])
    vbn = _head_norm_gelu(z[:, 2 * D_SSM + D_GMLP:], gn_ref[...]).astype(_BF16)
    tl = x.shape[0]
    rows = []
    for c in range(tl // CHUNK):
        heads = []
        for h in range(N_GMLP_HEADS):
            vh = vbn[c * CHUNK:(c + 1) * CHUNK, h * GMLP_HEAD:(h + 1) * GMLP_HEAD]
            heads.append(_dot(ws_ref[h], vh) + bs_ref[:, h:h + 1])
        rows.append(jnp.concatenate(heads, axis=-1))
    s = jnp.concatenate(rows, axis=0)
    mixb_ref[0] = _rms(ub * s, gog_ref[...]).astype(_BF16)


def _front_prompt(x, g1, win, gn, ws_tril_bf, bs_t, gog):
    n, l, d = x.shape
    tl = FRONT_TL
    per_seq = l // tl
    n_tiles = n * per_seq
    cur = lambda i: jnp.minimum(i, n_tiles - 1)
    prev = lambda i: jnp.maximum(i - 1, 0)
    const = lambda *shape: pl.BlockSpec(shape, lambda i: (0,) * len(shape))
    seq = lambda w, which: pl.BlockSpec((1, tl, w), lambda i: (which(i) // per_seq, which(i) % per_seq, 0))
    return pl.pallas_call(
        _front_prompt_kernel,
        grid=(n_tiles + 1,),
        in_specs=[seq(d, cur), const(1, d), const(d, D_IN), const(1, D_GMLP),
                  const(N_GMLP_HEADS, CHUNK, CHUNK), const(CHUNK, N_GMLP_HEADS), const(1, D_GMLP)],
        out_specs=[seq(D_SSM, prev), seq(D_SSM, prev), seq(D_GMLP, prev)],
        out_shape=[jax.ShapeDtypeStruct((n, l, D_SSM), _F32),
                   jax.ShapeDtypeStruct((n, l, D_SSM), _F32),
                   jax.ShapeDtypeStruct((n, l, D_GMLP), _BF16)],
        scratch_shapes=[pltpu.VMEM((d, D_IN), _BF16), pltpu.VMEM((tl, D_IN), _F32)],
        compiler_params=pltpu.CompilerParams(
            dimension_semantics=("arbitrary",), vmem_limit_bytes=VMEM_LIMIT),
        name="front_prompt",
    )(x, g1, win, gn, ws_tril_bf, bs_t, gog)


def _ssm_prompt_kernel(xa_ref, sg_ref, v_ref, r_ref, coef_ref, dsk_ref, gos_ref,
                       mixa_ref, hfin_ref, s_ref, st_ref):
    lc = xa_ref.shape[1]
    nblk = lc // SSM_BLK
    rows = nblk * SUBLANES

    @pl.when(pl.program_id(0) == 0)
    def _():
        st_ref[...] = jnp.zeros_like(st_ref)

    def by_position(ref):
        t = pltpu.einshape("btc->tbc", ref[...]).reshape(nblk, SSM_BLK, SUBLANES, D_SSM)
        return [t[:, i].reshape(rows, D_SSM) for i in range(SSM_BLK)]

    xs = by_position(xa_ref)
    xs_bf = [x.astype(_BF16) for x in xs]
    xk = [jnp.concatenate([x[:, k * LANES:(k + 1) * LANES] for x in xs_bf], axis=-1)
          for k in range(N_LANE_TILES)]
    for k in range(N_LANE_TILES):
        s_ref[:, 2 * TILE_STATE * k:2 * TILE_STATE * (k + 1)] = _dot(xk[k], v_ref[k])

    for kk in range(0, N_LANE_TILES, 2):
        tiles = (kk, kk + 1)
        cols = [(2 * TILE_STATE * k, 2 * TILE_STATE * k + TILE_STATE) for k in tiles]
        lbs = [tuple(jnp.broadcast_to(coef_ref[row:row + 1, k * TILE_STATE:(k + 1) * TILE_STATE],
                                      (SUBLANES, TILE_STATE)) for row in (COEF_LBLK_RE, COEF_LBLK_IM))
               for k in tiles]

        def body(j, carry, cols=cols, lbs=lbs):
            r0 = pl.multiple_of(j * SUBLANES, SUBLANES)
            out = []
            for q, ((c_re, c_im), (lr, li)) in enumerate(zip(cols, lbs)):
                hr, hi = carry[2 * q], carry[2 * q + 1]
                sr = s_ref[pl.ds(r0, SUBLANES), c_re:c_re + TILE_STATE]
                si = s_ref[pl.ds(r0, SUBLANES), c_im:c_im + TILE_STATE]
                s_ref[pl.ds(r0, SUBLANES), c_re:c_re + TILE_STATE] = hr
                s_ref[pl.ds(r0, SUBLANES), c_im:c_im + TILE_STATE] = hi
                out += [lr * hr - li * hi + sr, lr * hi + li * hr + si]
            return tuple(out)

        init = tuple(st_ref[:, c:c + TILE_STATE] for c_pair in cols for c in c_pair)
        fin = lax.fori_loop(0, nblk, body, init, unroll=2)
        for q, (c_re, c_im) in enumerate(cols):
            st_ref[:, c_re:c_re + TILE_STATE] = fin[2 * q]
            st_ref[:, c_im:c_im + TILE_STATE] = fin[2 * q + 1]

    yk = []
    for k in range(N_LANE_TILES):
        h_in = s_ref[:, 2 * TILE_STATE * k:2 * TILE_STATE * (k + 1)].astype(_BF16)
        yk.append(_dot(jnp.concatenate([h_in, xk[k]], axis=-1), r_ref[k]))
    sgs = by_position(sg_ref)
    outs = []
    for i in range(SSM_BLK):
        y = jnp.concatenate([y_k[:, i * LANES:(i + 1) * LANES] for y_k in yk], axis=-1) + dsk_ref[...] * xs[i]
        outs.append(_rms(_gelu(y) * sgs[i], gos_ref[...]).reshape(nblk, SUBLANES, D_SSM))
    mixa = jnp.stack(outs, axis=1).reshape(lc, SUBLANES, D_SSM)
    mixa_ref[...] = pltpu.einshape("tbc->btc", mixa).astype(_BF16)
    hfin_ref[...] = st_ref[...]


def _ssm_prompt(xa, sg, v, r, coef, dsk, gos):
    n, l, _ = xa.shape
    lc = SSM_LC
    const = lambda *shape: pl.BlockSpec(shape, lambda i: (0,) * len(shape))
    seq_spec = pl.BlockSpec((n, lc, D_SSM), lambda i: (0, i, 0))
    return pl.pallas_call(
        _ssm_prompt_kernel,
        grid=(l // lc,),
        in_specs=[seq_spec, seq_spec, const(*v.shape), const(*r.shape),
                  const(*coef.shape), const(1, D_SSM), const(1, D_SSM)],
        out_specs=[seq_spec, const(n, 2 * STATE_COLS)],
        out_shape=[jax.ShapeDtypeStruct((n, l, D_SSM), _BF16),
                   jax.ShapeDtypeStruct((n, 2 * STATE_COLS), _F32)],
        scratch_shapes=[pltpu.VMEM((lc // SSM_BLK * n, 2 * STATE_COLS), _F32),
                        pltpu.VMEM((n, 2 * STATE_COLS), _F32)],
        compiler_params=pltpu.CompilerParams(
            dimension_semantics=("arbitrary",), vmem_limit_bytes=VMEM_LIMIT),
        name="ssm_prompt",
    )(xa, sg, v, r, coef, dsk, gos)


def _front_sample_kernel(x_ref, g1_ref, win_ref, gn_ref, w00_ref, b0_ref, gog_ref,
                         wb_ref, wc_ref, coef_ref, dsk_ref, gos_ref, h0r_ref, h0i_ref,
                         mix_ref, hr_ref, hi_ref, vrow_ref):
    x = x_ref[...]
    hn = _rms(x, g1_ref[...])
    z = _dot_hi(hn, win_ref[...])
    xa = z[:, :D_SSM]
    ys = []
    for k in range(N_LANE_TILES):
        bu = _dot_hi(xa[:, k * LANES:(k + 1) * LANES], wb_ref[k])
        sl = slice(k * TILE_STATE, (k + 1) * TILE_STATE)
        lr, li = coef_ref[COEF_LB_RE:COEF_LB_RE + 1, sl], coef_ref[COEF_LB_IM:COEF_LB_IM + 1, sl]
        h0r, h0i = h0r_ref[:, sl], h0i_ref[:, sl]
        nr = lr * h0r - li * h0i + bu[:, :TILE_STATE]
        ni = lr * h0i + li * h0r + bu[:, TILE_STATE:]
        hr_ref[:, sl] = nr
        hi_ref[:, sl] = ni
        ys.append(_dot_hi(jnp.concatenate([nr, ni], axis=-1), wc_ref[k]))
    y = jnp.concatenate(ys, axis=-1) + dsk_ref[...] * xa
    ya = _gelu(y) * jax.nn.sigmoid(z[:, D_SSM:2 * D_SSM])
    mix_ref[:, :D_SSM] = _rms(ya, gos_ref[...])
    ub = _gelu(z[:, 2 * D_SSM:2 * D_SSM + D_GMLP])
    vbn = _head_norm_gelu(z[:, 2 * D_SSM + D_GMLP:], gn_ref[...])
    vrow_ref[...] = vbn
    s = w00_ref[...] * vbn + b0_ref[...]
    mix_ref[:, D_SSM:] = _rms(ub * s, gog_ref[...])


def _front_sample(x, g1, win, gn, w00, b0, gog, wb, wc, coef, dsk, gos, h0r, h0i):
    n = x.shape[0]
    vmem = pl.BlockSpec(memory_space=pltpu.VMEM)
    return pl.pallas_call(
        _front_sample_kernel,
        in_specs=[vmem] * 14,
        out_specs=[vmem] * 4,
        out_shape=[jax.ShapeDtypeStruct((n, D_MODEL), _F32),
                   jax.ShapeDtypeStruct((n, STATE_COLS), _F32),
                   jax.ShapeDtypeStruct((n, STATE_COLS), _F32),
                   jax.ShapeDtypeStruct((n, D_GMLP), _F32)],
        compiler_params=pltpu.CompilerParams(vmem_limit_bytes=VMEM_LIMIT),
        name="front_sample",
    )(x, g1, win, gn, w00, b0, gog, wb, wc, coef, dsk, gos, h0r, h0i)


def _route(logits, base):
    tm = logits.shape[0]
    lt = logits.T
    ex = lt[:N_EXPERTS, :]
    gr = lt[N_EXPERTS:N_EXPERTS + SUBLANES, :]
    row_e = lax.broadcasted_iota(jnp.int32, ex.shape, 0).astype(_F32)
    row_g = lax.broadcasted_iota(jnp.int32, gr.shape, 0).astype(_F32)
    neg = jnp.float32(-jnp.inf)
    big = jnp.float32(LANES)
    is_g = row_g < N_EXPERT_GROUPS
    gl = jnp.where(is_g, gr, neg)
    gmax = jnp.max(gl, axis=0, keepdims=True)
    gi = jnp.min(jnp.where(gl == gmax, row_g, big), axis=0, keepdims=True)
    p_top = 1.0 / jnp.sum(jnp.where(is_g, jnp.exp(gl - gmax), 0.0), axis=0, keepdims=True)
    lo = gi * EXPERTS_PER_GROUP
    in_grp = (row_e >= lo) & (row_e < lo + EXPERTS_PER_GROUP)
    m1 = jnp.max(jnp.where(in_grp, ex, neg), axis=0, keepdims=True)
    i1 = jnp.min(jnp.where(in_grp & (ex == m1), row_e, big), axis=0, keepdims=True)
    rest = in_grp & (row_e != i1)
    m2 = jnp.max(jnp.where(rest, ex, neg), axis=0, keepdims=True)
    i2 = jnp.min(jnp.where(rest & (ex == m2), row_e, big), axis=0, keepdims=True)
    e2 = jnp.exp(m2 - m1)
    w1 = p_top / (1.0 + e2)
    w2 = p_top * e2 / (1.0 + e2)
    sel1 = row_e == i1
    sel2 = row_e == i2
    hits = jnp.where(sel1 | sel2, 1.0, 0.0)
    src = lax.broadcasted_iota(jnp.int32, (tm, tm), 0)
    dst = lax.broadcasted_iota(jnp.int32, (tm, tm), 1)
    before = _dot(hits.astype(_BF16), jnp.where(src < dst, 1.0, 0.0).astype(_BF16)) + base
    rank1 = jnp.sum(jnp.where(sel1, before, 0.0), axis=0, keepdims=True)
    rank2 = jnp.sum(jnp.where(sel2, before, 0.0), axis=0, keepdims=True)
    fields = {R_E1: i1, R_E2: i2, R_W1: w1, R_W2: w2, R_RANK1: rank1, R_RANK2: rank2,
              R_CODE1: i1 * CODE_SHIFT + rank1, R_CODE2: i2 * CODE_SHIFT + rank2}
    row8 = lax.broadcasted_iota(jnp.int32, (SUBLANES, tm), 0)
    route_t = jnp.zeros((SUBLANES, tm), _F32)
    for r, val in fields.items():
        route_t = jnp.where(row8 == r, val, route_t)
    route = jnp.concatenate([route_t, jnp.zeros((LANES - SUBLANES, tm), _F32)], axis=0).T
    return route_t, route, base + jnp.sum(hits, axis=1, keepdims=True)


def _mixer_out_prompt_kernel(x_ref, mixa_ref, mixb_ref, wo_ref, g2_ref, wr_ref, br_ref,
                             x1_ref, xn_ref, route_ref, route_t_ref, cnt_ref, base_ref, logits_ref, wo_bf):
    i = pl.program_id(0)

    @pl.when(i == 0)
    def _():
        base_ref[...] = jnp.zeros_like(base_ref)
        logits_ref[...] = jnp.zeros_like(logits_ref)
        wo_bf[...] = wo_ref[...].astype(_BF16)

    prev_logits = logits_ref[...]
    x1 = x_ref[0] + _dot(mixa_ref[0], wo_bf[:D_SSM, :]) + _dot(mixb_ref[0], wo_bf[D_SSM:, :])
    xn = _rms(x1, g2_ref[...])
    x1_ref[...] = _pack_bf16_pair(x1)
    xn_ref[...] = _pack_bf16_pair(xn)
    logits_ref[...] = _dot(xn.astype(_BF16), wr_ref[...]) + br_ref[...]
    route_t, route, base = _route(prev_logits, base_ref[...])
    route_ref[...] = route
    route_t_ref[...] = route_t
    base = jnp.where(i >= 1, base, base_ref[...])
    base_ref[...] = base
    cnt_ref[...] = base


def _mixer_out_sample_kernel(x_ref, mix_ref, wo_ref, g2_ref, wr_ref, br_ref, cnt_in_ref,
                             x1_in, xn_in, route_in, route_t_in,
                             x1_ref, xn_ref, route_ref, route_t_ref, cnt_ref):
    del x1_in, xn_in, route_in, route_t_in
    x1 = (x_ref[...] + _dot_hi(mix_ref[:, :D_SSM], wo_ref[:D_SSM, :])
          + _dot_hi(mix_ref[:, D_SSM:], wo_ref[D_SSM:, :]))
    xn = _rms(x1, g2_ref[...])
    logits = _dot_hi(xn, wr_ref[...]) + br_ref[...]
    route_t, route, base = _route(logits, cnt_in_ref[...])
    x1_ref[...] = _pack_bf16_pair(x1)
    xn_ref[...] = _pack_bf16_pair(xn)
    route_ref[...] = route
    route_t_ref[...] = route_t
    cnt_ref[...] = base


def _mixer_out(x_p, mixa, mixb, x_s, mix_s, wo, g2, wr, br):
    n, l, d = x_p.shape
    ns = x_s.shape[0]
    t_all = n * l + ns
    tm = TOK_TM
    per_seq = l // tm
    n_tiles = n * per_seq
    cur = lambda i: jnp.minimum(i, n_tiles - 1)
    prev = lambda i: jnp.maximum(i - 1, 0)
    const = lambda *shape: pl.BlockSpec(shape, lambda i: (0,) * len(shape))
    seq = lambda w: pl.BlockSpec((1, tm, w), lambda i: (cur(i) // per_seq, cur(i) % per_seq, 0))
    tok = lambda w, which: pl.BlockSpec((tm, w), lambda i: (which(i), 0))
    tok_shapes = [jax.ShapeDtypeStruct((t_all, d // 2), _U32),
                  jax.ShapeDtypeStruct((t_all, d // 2), _U32),
                  jax.ShapeDtypeStruct((t_all, LANES), _F32),
                  jax.ShapeDtypeStruct((SUBLANES, t_all), _F32)]
    cnt_shape = jax.ShapeDtypeStruct((N_EXPERTS, 1), _F32)
    x1, xn, route, route_t, cnt = pl.pallas_call(
        _mixer_out_prompt_kernel,
        grid=(n_tiles + 1,),
        in_specs=[seq(d), seq(D_SSM), seq(D_GMLP),
                  const(d, d), const(1, d), const(d, LANES), const(1, LANES)],
        out_specs=[tok(d // 2, cur), tok(d // 2, cur), tok(LANES, prev),
                   pl.BlockSpec((SUBLANES, tm), lambda i: (0, prev(i))), const(N_EXPERTS, 1)],
        out_shape=tok_shapes + [cnt_shape],
        scratch_shapes=[pltpu.VMEM((N_EXPERTS, 1), _F32), pltpu.VMEM((tm, LANES), _F32),
                        pltpu.VMEM((d, d), _BF16)],
        compiler_params=pltpu.CompilerParams(
            dimension_semantics=("arbitrary",), vmem_limit_bytes=VMEM_LIMIT),
        name="mixer_out_prompt",
    )(x_p, mixa, mixb, wo, g2, wr.astype(_BF16), br)
    tail = (n * l) // ns
    c1 = lambda *shape: pl.BlockSpec(shape, lambda i: (0,) * len(shape))
    anyspec = pl.BlockSpec(memory_space=pl.ANY)
    tail_spec = lambda w: pl.BlockSpec((ns, w), lambda i: (tail, 0))
    return pl.pallas_call(
        _mixer_out_sample_kernel,
        grid=(1,),
        in_specs=[c1(ns, d), c1(ns, d), c1(d, d), c1(1, d), c1(d, LANES), c1(1, LANES), c1(N_EXPERTS, 1),
                  anyspec, anyspec, anyspec, anyspec],
        out_specs=[tail_spec(d // 2), tail_spec(d // 2), tail_spec(LANES),
                   pl.BlockSpec((SUBLANES, ns), lambda i: (0, tail)), c1(N_EXPERTS, 1)],
        out_shape=tok_shapes + [cnt_shape],
        input_output_aliases={7: 0, 8: 1, 9: 2, 10: 3},
        compiler_params=pltpu.CompilerParams(
            dimension_semantics=("arbitrary",), vmem_limit_bytes=VMEM_LIMIT),
        name="mixer_out_sample",
    )(x_s, mix_s, wo, g2, wr, br, cnt, x1, xn, route, route_t)


def _sc_stream(n_chunks, gather, write):
    gather(0).start()
    for j in range(n_chunks):
        if j + 1 < n_chunks:
            if j >= 1:
                write(j - 1).wait()
            gather(j + 1).start()
        gather(j).wait()
        write(j).start()
    if n_chunks >= 2:
        write(n_chunks - 2).wait()
    write(n_chunks - 1).wait()


def _sc_mesh():
    return plsc.VectorSubcoreMesh(core_axis_name="c", subcore_axis_name="s",
                                  num_cores=SC_CORES, num_subcores=SC_SUBCORES)


def _sc_buffers(chunk, w, dtype):
    return [pltpu.VMEM((chunk, w), dtype), pltpu.VMEM((chunk, w), dtype)] + [pltpu.SemaphoreType.DMA] * 4


def _sc_combine(table, idx, n_out, chunk):
    w = table.shape[1]
    rows_w = n_out // SC_WORKERS
    n_chunks = rows_w // chunk
    assert rows_w * SC_WORKERS == n_out and n_chunks * chunk == rows_w and rows_w % SUBLANES == 0

    def body(table_hbm, idx_hbm, out_hbm, idx_v, buf0, buf1, g0, g1, w0, w1):
        wid = lax.axis_index("s") * SC_CORES + lax.axis_index("c")
        base = pl.multiple_of(wid * rows_w, SUBLANES)
        pltpu.sync_copy(idx_hbm.at[pl.ds(base, rows_w)], idx_v)
        bufs, gsems, wsems = (buf0, buf1), (g0, g1), (w0, w1)

        def gather(j):
            return pltpu.make_async_copy(table_hbm.at[idx_v.at[pl.ds(j * chunk, chunk)]], bufs[j % 2], gsems[j % 2])

        def write(j):
            return pltpu.make_async_copy(bufs[j % 2], out_hbm.at[pl.ds(base + j * chunk, chunk)], wsems[j % 2])

        _sc_stream(n_chunks, gather, write)

    return pl.kernel(
        body,
        out_type=jax.ShapeDtypeStruct((n_out, w), table.dtype),
        mesh=_sc_mesh(),
        scratch_types=[pltpu.VMEM((rows_w,), jnp.int32)] + _sc_buffers(chunk, w, table.dtype),
        compiler_params=pltpu.CompilerParams(use_tc_tiling_on_sc=True),
        name="sc_combine",
    )(table, idx)


def _sc_dispatch(table, codes, start_row, n_out, chunk):
    t_all, w = table.shape
    n_pad = codes.shape[0]
    n_ent = 2 * t_all
    ent_w = n_pad // SC_WORKERS
    n_chunks = ent_w // chunk
    per_chunk = chunk // SC_LANES
    trash = n_out - (n_pad - n_ent)
    assert ent_w * SC_WORKERS == n_pad and n_chunks * chunk == ent_w
    assert per_chunk * SC_LANES == chunk and chunk <= LANES and n_pad - n_ent <= t_all

    def body(table_hbm, code_hbm, start_hbm, out_hbm, dest_hbm,
             code_v, dest_v, tok_v, dst_v, start_v, buf0, buf1, g0, g1, w0, w1):
        wid = lax.axis_index("s") * SC_CORES + lax.axis_index("c")
        ebase = pl.multiple_of(wid * ent_w, SUBLANES)
        pltpu.sync_copy(code_hbm.at[pl.ds(ebase, ent_w)], code_v)
        pltpu.sync_copy(start_hbm, start_v)
        lane = lax.iota(jnp.int32, SC_LANES)
        for j in range(n_chunks):
            for c in range(per_chunk):
                off = j * chunk + c * SC_LANES
                ent = ebase + off + lane
                code = code_v[pl.ds(off, SC_LANES)]
                d = plsc.load_gather(start_v, [code >> CODE_BITS]) + (code & ((1 << CODE_BITS) - 1))
                d = jnp.where(ent >= n_ent, trash + (ent - n_ent), d)
                tok = jnp.where(ent >= t_all, ent - t_all, ent)
                tok = jnp.where(tok >= t_all, tok - t_all, tok)
                dest_v[pl.ds(off, SC_LANES)] = d
                dst_v[j, pl.ds(c * SC_LANES, SC_LANES)] = d
                tok_v[j, pl.ds(c * SC_LANES, SC_LANES)] = tok
        pltpu.sync_copy(dest_v, dest_hbm.at[pl.ds(ebase, ent_w)])
        bufs, gsems, wsems = (buf0, buf1), (g0, g1), (w0, w1)

        def gather(j):
            return pltpu.make_async_copy(table_hbm.at[tok_v.at[j]], bufs[j % 2], gsems[j % 2])

        def scatter(j):
            return pltpu.make_async_copy(bufs[j % 2], out_hbm.at[dst_v.at[j]], wsems[j % 2])

        _sc_stream(n_chunks, gather, scatter)

    return pl.kernel(
        body,
        out_type=(jax.ShapeDtypeStruct((n_out, w), table.dtype), jax.ShapeDtypeStruct((n_pad,), jnp.int32)),
        mesh=_sc_mesh(),
        scratch_types=([pltpu.VMEM((ent_w,), jnp.int32), pltpu.VMEM((ent_w,), jnp.int32),
                        pltpu.VMEM((n_chunks, chunk), jnp.int32), pltpu.VMEM((n_chunks, chunk), jnp.int32),
                        pltpu.VMEM((LANES,), jnp.int32)] + _sc_buffers(chunk, w, table.dtype)),
        compiler_params=pltpu.CompilerParams(use_tc_tiling_on_sc=True, needs_layout_passes=False),
        name="sc_dispatch",
    )(table, codes, start_row)


def _experts_kernel(piece_start_ref, piece_row_ref, piece_cls_ref, wg_ref, wu_ref, wd_ref, xs_hbm, ys_hbm,
                    wg_bf, wu_bf, wd_bf, xbuf, ybuf, xsem, ysem):
    e = pl.program_id(0)
    g0 = piece_start_ref[e]
    n_here = piece_start_ref[e + 1] - g0
    n_total = piece_start_ref[N_EXPERTS]

    def per_class(g, fn):
        cls = piece_cls_ref[g]
        row = pl.multiple_of(piece_row_ref[g], EXP_UNIT)
        for c in range(1, EXP_CLASSES + 1):
            pl.when(cls == c)(lambda c=c: fn(c * EXP_UNIT, row))

    def x_copy(slot, rows, row):
        return pltpu.make_async_copy(xs_hbm.at[pl.ds(row, rows)], xbuf.at[slot, pl.ds(0, rows)], xsem.at[slot])

    def y_copy(slot, rows, row):
        return pltpu.make_async_copy(ybuf.at[slot, pl.ds(0, rows)], ys_hbm.at[pl.ds(row, rows)], ysem.at[slot])

    @pl.when((e == 0) & (n_total > 0))
    def _():
        per_class(0, lambda rows, row: x_copy(0, rows, row).start())

    wg_bf[...] = wg_ref[0].astype(_BF16)
    wu_bf[...] = wu_ref[0].astype(_BF16)
    wd_bf[...] = wd_ref[0].astype(_BF16)

    def piece(j, carry):
        g = g0 + j
        slot = lax.rem(g, 2)
        per_class(g, lambda rows, row: x_copy(slot, rows, row).wait())

        @pl.when(g + 1 < n_total)
        def _():
            per_class(g + 1, lambda rows, row: x_copy(1 - slot, rows, row).start())

        @pl.when(g >= 2)
        def _():
            per_class(g - 2, lambda rows, row: y_copy(slot, rows, row).wait())

        def compute(rows, row):
            x = _unpack_bf16_pair(xbuf[slot, pl.ds(0, rows)]).astype(_BF16)
            a = _dot(x, wg_bf[...])
            u = _dot(x, wu_bf[...])
            h = (a * jax.nn.sigmoid(a) * u).astype(_BF16)
            ybuf[slot, pl.ds(0, rows)] = _pack_bf16_pair(_dot(h, wd_bf[...]))
            y_copy(slot, rows, row).start()

        per_class(g, compute)
        return carry

    lax.fori_loop(0, n_here, piece, 0)

    @pl.when(e == N_EXPERTS - 1)
    def _():
        @pl.when(n_total >= 2)
        def _():
            per_class(n_total - 2, lambda rows, row: y_copy(lax.rem(n_total, 2), rows, row).wait())

        @pl.when(n_total >= 1)
        def _():
            per_class(n_total - 1, lambda rows, row: y_copy(lax.rem(n_total - 1, 2), rows, row).wait())


def _experts(piece_start, piece_row, piece_cls, n_rows, xs, w_gate, w_up, w_down):
    dh = xs.shape[1]
    d = 2 * dh
    tm = EXP_UNIT * EXP_CLASSES
    anyspec = pl.BlockSpec(memory_space=pl.ANY)
    wsel = lambda e, ps, pr, pc: (e, 0, 0)
    grid_spec = pltpu.PrefetchScalarGridSpec(
        num_scalar_prefetch=3,
        grid=(N_EXPERTS,),
        in_specs=[pl.BlockSpec((1, d, D_EXPERT), wsel), pl.BlockSpec((1, d, D_EXPERT), wsel),
                  pl.BlockSpec((1, D_EXPERT, d), wsel), anyspec],
        out_specs=anyspec,
        scratch_shapes=[pltpu.VMEM((d, D_EXPERT), _BF16), pltpu.VMEM((d, D_EXPERT), _BF16),
                        pltpu.VMEM((D_EXPERT, d), _BF16),
                        pltpu.VMEM((2, tm, dh), _U32), pltpu.VMEM((2, tm, dh), _U32),
                        pltpu.SemaphoreType.DMA((2,)), pltpu.SemaphoreType.DMA((2,))],
    )
    return pl.pallas_call(
        _experts_kernel,
        grid_spec=grid_spec,
        out_shape=jax.ShapeDtypeStruct((n_rows, dh), _U32),
        compiler_params=pltpu.CompilerParams(
            dimension_semantics=("arbitrary",), vmem_limit_bytes=VMEM_LIMIT),
        name="experts",
    )(piece_start, piece_row, piece_cls, w_gate, w_up, w_down, xs)


def _final_kernel(x1_ref, ya_ref, yb_ref, route_ref, gf_ref, y_ref):
    route = route_ref[...]
    x2 = (_unpack_bf16_pair(x1_ref[...]) + route[:, R_W1:R_W1 + 1] * _unpack_bf16_pair(ya_ref[...])
          + route[:, R_W2:R_W2 + 1] * _unpack_bf16_pair(yb_ref[...]))
    y_ref[...] = _rms(x2, gf_ref[...])


def _final(x1, yab, route, gf, n_prompt, n_sample):
    d = 2 * x1.shape[1]

    def call(tm, first_block, n_rows, name):
        tok = lambda w: pl.BlockSpec((tm, w), lambda i: (first_block + i, 0))
        sel = lambda k: pl.BlockSpec((None, tm, d // 2), lambda i: (k, first_block + i, 0))
        return pl.pallas_call(
            _final_kernel,
            grid=(n_rows // tm,),
            in_specs=[tok(d // 2), sel(0), sel(1), tok(LANES), pl.BlockSpec((1, d), lambda i: (0, 0))],
            out_specs=pl.BlockSpec((tm, d), lambda i: (i, 0)),
            out_shape=jax.ShapeDtypeStruct((n_rows, d), _F32),
            compiler_params=pltpu.CompilerParams(
                dimension_semantics=("arbitrary",), vmem_limit_bytes=VMEM_LIMIT),
            name=name,
        )(x1, yab, yab, route, gf)

    return (call(FINAL_TM, 0, n_prompt, "final_prompt"),
            call(n_sample, n_prompt // n_sample, n_sample, "final_sample"))


def _powers(lam_re, lam_im, dt):
    out = []
    for m in range(SSM_BLK + 1):
        mag = jnp.exp(m * lam_re * dt)
        ang = m * lam_im * dt
        out.append((mag * jnp.cos(ang), mag * jnp.sin(ang)))
    return out


def _spread(x, copies):
    w = x.shape[1]
    src = lax.broadcasted_iota(jnp.int32, (w, w * copies), 0)
    dst = lax.broadcasted_iota(jnp.int32, (w, w * copies), 1)
    return _dot_f32(x, jnp.where(dst % w == src, 1.0, 0.0))


def _ssm_prep_kernel(lam_ref, b_re, b_im, c_re, c_im, v_ref, r_ref, wb_ref, wc_ref, coef_ref):
    n_p, n_h = SSM_STATE, SSM_GROUP
    lr, li, dt = lam_ref[0:1, :], lam_ref[1:2, :], lam_ref[2:3, :]
    pw = _powers(lr, li, dt)
    den = lr * lr + li * li
    nr, ni = pw[1][0] - 1.0, pw[1][1]
    k_re = (nr * lr + ni * li) / den
    k_im = (ni * lr - nr * li) / den
    coef_ref[...] = jnp.concatenate(
        [pw[1][0], pw[1][1], pw[SSM_BLK][0], pw[SSM_BLK][1], jnp.zeros((SUBLANES - 4, TILE_STATE), _F32)], axis=0)

    on_diag_b = (lax.broadcasted_iota(jnp.int32, (TILE_STATE, LANES), 0) // n_p
                 == lax.broadcasted_iota(jnp.int32, (TILE_STATE, LANES), 1) // n_h)
    rows_gp = lambda ref: ref[...].reshape(TILE_STATE, n_h)
    bt_re = jnp.where(on_diag_b, _spread(rows_gp(b_re), SUBLANES), 0.0).T
    bt_im = jnp.where(on_diag_b, _spread(rows_gp(b_im), SUBLANES), 0.0).T
    bb_re = k_re * bt_re - k_im * bt_im
    bb_im = k_re * bt_im + k_im * bt_re
    wb_ref[0] = jnp.concatenate([bb_re, bb_im], axis=1)
    v_rows = []
    for s in range(SSM_BLK):
        pr, pi = pw[SSM_BLK - 1 - s]
        v_rows.append(jnp.concatenate([pr * bb_re - pi * bb_im, pr * bb_im + pi * bb_re], axis=1))
    v_ref[0] = jnp.concatenate(v_rows, axis=0).astype(v_ref.dtype)

    on_diag_c = (lax.broadcasted_iota(jnp.int32, (LANES, TILE_STATE), 0) // n_h
                 == lax.broadcasted_iota(jnp.int32, (LANES, TILE_STATE), 1) // n_p)
    rows_gh = lambda ref: ref[...].reshape(LANES, n_p)
    ct_re = jnp.where(on_diag_c, _spread(rows_gh(c_re), SUBLANES), 0.0)
    ct_im = jnp.where(on_diag_c, _spread(rows_gh(c_im), SUBLANES), 0.0)
    cl = [(ct_re * pr - ct_im * pi, ct_re * pi + ct_im * pr) for pr, pi in pw]
    wc_ref[0] = jnp.concatenate([cl[0][0], -cl[0][1]], axis=1).T
    direct = [_dot_hi(cl[m][0], bb_re, transpose_b=True) - _dot_hi(cl[m][1], bb_im, transpose_b=True)
              for m in range(SSM_BLK)]
    zero = jnp.zeros((LANES, LANES), _F32)
    rt = jnp.concatenate(
        [jnp.concatenate([cl[i + 1][0], -cl[i + 1][1]]
                         + [direct[i - s] if s <= i else zero for s in range(SSM_BLK)], axis=1)
         for i in range(SSM_BLK)], axis=0)
    r_ref[0] = rt.T.astype(r_ref.dtype)


def _ssm_params(lam_re, lam_im, log_dt, b_re, b_im, c_re, c_im, d_skip):
    n_g, n_p, n_h = N_SSM_GROUPS, SSM_STATE, SSM_GROUP
    dt = jnp.repeat(jnp.exp(log_dt), n_p)
    lam = jnp.zeros((SUBLANES, STATE_COLS), _F32).at[0].set(lam_re.reshape(-1)).at[1].set(
        lam_im.reshape(-1)).at[2].set(dt)
    groups = lambda r, c: pl.BlockSpec((SUBLANES, r, c), lambda k: (k, 0, 0))
    out3 = lambda rows, w: pl.BlockSpec((1, rows, w), lambda k: (k, 0, 0))
    cols = pl.BlockSpec((SUBLANES, TILE_STATE), lambda k: (0, k))
    k_blk = SSM_BLK * LANES
    v, r, wb, wc, coef = pl.pallas_call(
        _ssm_prep_kernel,
        grid=(N_LANE_TILES,),
        in_specs=[cols, groups(n_p, n_h), groups(n_p, n_h), groups(n_h, n_p), groups(n_h, n_p)],
        out_specs=[out3(k_blk, 2 * TILE_STATE), out3(2 * TILE_STATE + k_blk, k_blk),
                   out3(LANES, 2 * TILE_STATE), out3(2 * TILE_STATE, LANES), cols],
        out_shape=[jax.ShapeDtypeStruct((N_LANE_TILES, k_blk, 2 * TILE_STATE), _BF16),
                   jax.ShapeDtypeStruct((N_LANE_TILES, 2 * TILE_STATE + k_blk, k_blk), _BF16),
                   jax.ShapeDtypeStruct((N_LANE_TILES, LANES, 2 * TILE_STATE), _F32),
                   jax.ShapeDtypeStruct((N_LANE_TILES, 2 * TILE_STATE, LANES), _F32),
                   jax.ShapeDtypeStruct((SUBLANES, STATE_COLS), _F32)],
        compiler_params=pltpu.CompilerParams(
            dimension_semantics=("arbitrary",), vmem_limit_bytes=VMEM_LIMIT),
        name="ssm_prep",
    )(lam, b_re, b_im, c_re, c_im)
    return wb, wc, v, r, coef, d_skip.reshape(1, D_SSM)


def _dispatch_plan(route_t, cnt):
    t_all = route_t.shape[1]
    codes = route_t[R_CODE1:R_CODE2 + 1].astype(jnp.int32).reshape(-1)
    per_pass = SC_WORKERS * DISPATCH_CHUNK
    codes = jnp.pad(codes, (0, -(2 * t_all) % per_pass))
    counts = cnt[:, 0].astype(jnp.int32)
    zero = jnp.zeros((1,), jnp.int32)
    units = (counts + EXP_UNIT - 1) // EXP_UNIT
    unit_start = jnp.concatenate([zero, jnp.cumsum(units)])
    start_row = jnp.zeros((LANES,), jnp.int32).at[:N_EXPERTS].set(unit_start[:N_EXPERTS] * EXP_UNIT)
    pieces = (units + EXP_CLASSES - 1) // EXP_CLASSES
    piece_start = jnp.concatenate([zero, jnp.cumsum(pieces)])
    tm = EXP_UNIT * EXP_CLASSES
    max_units = (2 * t_all + N_EXPERTS * (EXP_UNIT - 1)) // EXP_UNIT
    max_pieces = (max_units + N_EXPERTS * (EXP_CLASSES - 1)) // EXP_CLASSES
    g = jnp.arange(max_pieces, dtype=jnp.int32)
    owner = ((g[:, None] >= piece_start[None, :-1]) & (g[:, None] < piece_start[None, 1:])).astype(jnp.int32)
    pick = lambda table: jnp.sum(owner * table[None, :], axis=1)
    first_unit = pick(unit_start[:-1]) + (g - pick(piece_start[:-1])) * EXP_CLASSES
    piece_row = first_unit * EXP_UNIT
    piece_cls = jnp.clip(pick(unit_start[1:]) - first_unit, 1, EXP_CLASSES)
    n_rows = (max_units * EXP_UNIT + tm - 1) // tm * tm + tm
    return codes, start_row, n_rows, piece_start, piece_row, piece_cls


def kernel(x_prompt, x_sample, state_ssm_re, state_ssm_im, norm1_g, w_in, lam_re, lam_im, log_dt, ssm_b_re, ssm_b_im, ssm_c_re, ssm_c_im, ssm_d, gmlp_norm_g, gmlp_w_s, gmlp_b_s, out_norm_ssm_g, out_norm_gmlp_g, w_out, norm2_g, w_router_group, b_router_group, w_router_expert, b_router_expert, w_gate, w_up, w_down, final_norm_g):
    n, l, d = x_prompt.shape
    ns = x_sample.shape[0]
    t_all = n * l + ns
    li = 0
    g1 = norm1_g[li].reshape(1, d)
    gn = gmlp_norm_g[li].reshape(1, D_GMLP)
    tril = jnp.tril(jnp.ones((CHUNK, CHUNK), dtype=bool))
    ws_tril = jnp.where(tril[None], gmlp_w_s[li], 0.0)
    bs = gmlp_b_s[li]
    gog = out_norm_gmlp_g[li].reshape(1, D_GMLP)
    gos = out_norm_ssm_g[li].reshape(1, D_SSM)
    wb, wc, v_blk, r_blk, coef, dsk = _ssm_params(
        lam_re[li], lam_im[li], log_dt[li], ssm_b_re[li], ssm_b_im[li], ssm_c_re[li], ssm_c_im[li], ssm_d[li])
    g2 = norm2_g[li].reshape(1, d)
    pad = LANES - N_EXPERTS - N_EXPERT_GROUPS
    wr = jnp.concatenate([w_router_expert[li], w_router_group[li], jnp.zeros((d, pad), _F32)], axis=1)
    br = jnp.concatenate([b_router_expert[li], b_router_group[li], jnp.zeros((pad,), _F32)]).reshape(1, LANES)

    xa, sg, mixb = _front_prompt(x_prompt, g1, w_in[li], gn, ws_tril.astype(_BF16), bs.T, gog)
    mixa, hfin = _ssm_prompt(xa, sg, v_blk, r_blk, coef, dsk, gos)
    w00 = jnp.repeat(ws_tril[:, 0, 0], GMLP_HEAD).reshape(1, D_GMLP)
    b0 = jnp.repeat(bs[:, 0], GMLP_HEAD).reshape(1, D_GMLP)
    mix_s, hr_s, hi_s, vrow = _front_sample(
        x_sample.reshape(ns, d), g1, w_in[li], gn, w00, b0, gog, wb, wc, coef, dsk, gos,
        state_ssm_re[li].reshape(ns, STATE_COLS), state_ssm_im[li].reshape(ns, STATE_COLS))

    x1, xn, route, route_t, cnt = _mixer_out(x_prompt, mixa, mixb, x_sample.reshape(ns, d), mix_s,
                                             w_out[li], g2, wr, br)
    codes, start_row, n_rows, piece_start, piece_row, piece_cls = _dispatch_plan(route_t, cnt)
    xs, dest = _sc_dispatch(xn, codes, start_row, n_rows, DISPATCH_CHUNK)
    ys = _experts(piece_start, piece_row, piece_cls, n_rows, xs, w_gate[li], w_up[li], w_down[li])
    yab = _sc_combine(ys, dest, 2 * t_all, COMBINE_CHUNK).reshape(2, t_all, d // 2)
    y_p, y_s = _final(x1, yab, route, final_norm_g.reshape(1, d), n * l, ns)

    hf = hfin.reshape(n, N_LANE_TILES, 2, 8, SSM_STATE)
    re_p = hf[:, :, 0].reshape(1, n, N_SSM_GROUPS, SSM_STATE)
    im_p = hf[:, :, 1].reshape(1, n, N_SSM_GROUPS, SSM_STATE)
    re_s = hr_s.reshape(1, ns, N_SSM_GROUPS, SSM_STATE)
    im_s = hi_s.reshape(1, ns, N_SSM_GROUPS, SSM_STATE)
    return (y_p.reshape(n, l, d), y_s.reshape(ns, 1, d), re_p, im_p, re_s, im_s,
            vrow.reshape(1, ns, 1, D_GMLP))
```

```python
import math

import jax
import jax.numpy as jnp
from jax import lax
from jax.experimental import pallas as pl
from jax.experimental.pallas import tpu as pltpu
from jax.experimental.pallas import tpu_sc as plsc

D_MODEL = 1024
D_SSM = 512
D_GMLP = 512
SSM_GROUP = 16
N_SSM_GROUPS = 32
SSM_STATE = 64
CHUNK = 128
N_GMLP_HEADS = 4
GMLP_HEAD = 128
N_EXPERT_GROUPS = 4
EXPERTS_PER_GROUP = 8
N_EXPERTS = 32
D_EXPERT = 512
D_IN = 2048
EPS = 1e-6

LANES = 128
SUBLANES = 8
N_LANE_TILES = D_SSM // LANES
STATE_COLS = N_SSM_GROUPS * SSM_STATE
TILE_STATE = STATE_COLS // N_LANE_TILES
VMEM_LIMIT = 56 * 1024 * 1024

SC_CORES = 2
SC_SUBCORES = 16
SC_LANES = 16
SC_WORKERS = SC_CORES * SC_SUBCORES

FRONT_TL = 512
SSM_LC = 256
SSM_BLK = 4
COEF_LB_RE, COEF_LB_IM, COEF_LBLK_RE, COEF_LBLK_IM = 0, 1, 2, 3
TOK_TM = 1024
FINAL_TM = 1024
EXP_UNIT = 128
EXP_CLASSES = 8
DISPATCH_CHUNK = 80
COMBINE_CHUNK = 24

R_E1, R_E2, R_W1, R_W2, R_RANK1, R_RANK2, R_CODE1, R_CODE2 = 0, 1, 2, 3, 4, 5, 6, 7
CODE_BITS = 16
CODE_SHIFT = float(1 << CODE_BITS)

_INV_SQRT2 = 1.0 / math.sqrt(2.0)
_BF16 = jnp.bfloat16
_F32 = jnp.float32
_U32 = jnp.uint32


def _gelu(x):
    return 0.5 * x * (1.0 + lax.erf(x * _INV_SQRT2))


def _rms(x, g):
    return x * lax.rsqrt(jnp.mean(x * x, axis=-1, keepdims=True) + EPS) * g


def _dot(a, b):
    return jnp.dot(a, b, preferred_element_type=_F32)


def _dot_f32(a, b):
    return jnp.dot(a, b, preferred_element_type=_F32, precision=lax.Precision.HIGHEST)


def _dot_hi(a, b, transpose_b=False):
    def split(x):
        hi = x.astype(_BF16)
        return hi, (x - hi.astype(_F32)).astype(_BF16)

    dims = (((1,), (1 if transpose_b else 0,)), ((), ()))
    dot = lambda u, v: lax.dot_general(u, v, dims, preferred_element_type=_F32)
    a_hi, a_lo = split(a)
    b_hi, b_lo = split(b)
    return dot(a_hi, b_hi) + dot(a_hi, b_lo) + dot(a_lo, b_hi)


def _pack_bf16_pair(x):
    w = x.shape[1] // 2
    hi = lax.bitcast_convert_type(x[:, :w].astype(_BF16).astype(_F32), _U32)
    lo = lax.bitcast_convert_type(x[:, w:].astype(_BF16).astype(_F32), _U32)
    return hi | (lo >> 16)


def _unpack_bf16_pair(p):
    hi = lax.bitcast_convert_type(p & jnp.uint32(0xFFFF0000), _F32)
    lo = lax.bitcast_convert_type(p << 16, _F32)
    return jnp.concatenate([hi, lo], axis=-1)


def _head_norm_gelu(vb, gn):
    v = _gelu(vb)
    parts = []
    for h in range(N_GMLP_HEADS):
        vh = v[:, h * GMLP_HEAD:(h + 1) * GMLP_HEAD]
        parts.append(vh * lax.rsqrt(jnp.mean(vh * vh, axis=-1, keepdims=True) + EPS))
    return jnp.concatenate(parts, axis=-1) * gn


def _front_prompt_kernel(x_ref, g1_ref, win_ref, gn_ref, ws_ref, bs_ref, gog_ref,
                         xa_ref, sg_ref, mixb_ref, win_bf, z_ref):
    @pl.when(pl.program_id(0) == 0)
    def _():
        win_bf[...] = win_ref[...].astype(_BF16)
        z_ref[...] = jnp.zeros_like(z_ref)

    z = z_ref[...]
    x = x_ref[0]
    hn = _rms(x, g1_ref[...]).astype(_BF16)
    z_ref[...] = _dot(hn, win_bf[...])
    xa_ref[0] = z[:, :D_SSM]
    sg_ref[0] = jax.nn.sigmoid(z[:, D_SSM:2 * D_SSM])
    ub = _gelu(z[:, 2 * D_SSM:2 * D_SSM + D_GMLP])
    vbn = _head_norm_gelu(z[:, 2 * D_SSM + D_GMLP:], gn_ref[...]).astype(_BF16)
    tl = x.shape[0]
    rows = []
    for c in range(tl // CHUNK):
        heads = []
        for h in range(N_GMLP_HEADS):
            vh = vbn[c * CHUNK:(c + 1) * CHUNK, h * GMLP_HEAD:(h + 1) * GMLP_HEAD]
            heads.append(_dot(ws_ref[h], vh) + bs_ref[:, h:h + 1])
        rows.append(jnp.concatenate(heads, axis=-1))
    s = jnp.concatenate(rows, axis=0)
    mixb_ref[0] = _rms(ub * s, gog_ref[...]).astype(_BF16)


def _front_prompt(x, g1, win, gn, ws_tril_bf, bs_t, gog):
    n, l, d = x.shape
    tl = FRONT_TL
    per_seq = l // tl
    n_tiles = n * per_seq
    cur = lambda i: jnp.minimum(i, n_tiles - 1)
    prev = lambda i: jnp.maximum(i - 1, 0)
    const = lambda *shape: pl.BlockSpec(shape, lambda i: (0,) * len(shape))
    seq = lambda w, which: pl.BlockSpec((1, tl, w), lambda i: (which(i) // per_seq, which(i) % per_seq, 0))
    return pl.pallas_call(
        _front_prompt_kernel,
        grid=(n_tiles + 1,),
        in_specs=[seq(d, cur), const(1, d), const(d, D_IN), const(1, D_GMLP),
                  const(N_GMLP_HEADS, CHUNK, CHUNK), const(CHUNK, N_GMLP_HEADS), const(1, D_GMLP)],
        out_specs=[seq(D_SSM, prev), seq(D_SSM, prev), seq(D_GMLP, prev)],
        out_shape=[jax.ShapeDtypeStruct((n, l, D_SSM), _F32),
                   jax.ShapeDtypeStruct((n, l, D_SSM), _F32),
                   jax.ShapeDtypeStruct((n, l, D_GMLP), _BF16)],
        scratch_shapes=[pltpu.VMEM((d, D_IN), _BF16), pltpu.VMEM((tl, D_IN), _F32)],
        compiler_params=pltpu.CompilerParams(
            dimension_semantics=("arbitrary",), vmem_limit_bytes=VMEM_LIMIT),
        name="front_prompt",
    )(x, g1, win, gn, ws_tril_bf, bs_t, gog)


def _ssm_prompt_kernel(xa_ref, sg_ref, v_ref, r_ref, coef_ref, dsk_ref, gos_ref,
                       mixa_ref, hfin_ref, s_ref, st_ref):
    lc = xa_ref.shape[1]
    nblk = lc // SSM_BLK
    rows = nblk * SUBLANES

    @pl.when(pl.program_id(0) == 0)
    def _():
        st_ref[...] = jnp.zeros_like(st_ref)

    def by_position(ref):
        t = pltpu.einshape("btc->tbc", ref[...]).reshape(nblk, SSM_BLK, SUBLANES, D_SSM)
        return [t[:, i].reshape(rows, D_SSM) for i in range(SSM_BLK)]

    xs = by_position(xa_ref)
    xs_bf = [x.astype(_BF16) for x in xs]
    xk = [jnp.concatenate([x[:, k * LANES:(k + 1) * LANES] for x in xs_bf], axis=-1)
          for k in range(N_LANE_TILES)]
    for k in range(N_LANE_TILES):
        s_ref[:, 2 * TILE_STATE * k:2 * TILE_STATE * (k + 1)] = _dot(xk[k], v_ref[k])

    for kk in range(0, N_LANE_TILES, 2):
        tiles = (kk, kk + 1)
        cols = [(2 * TILE_STATE * k, 2 * TILE_STATE * k + TILE_STATE) for k in tiles]
        lbs = [tuple(jnp.broadcast_to(coef_ref[row:row + 1, k * TILE_STATE:(k + 1) * TILE_STATE],
                                      (SUBLANES, TILE_STATE)) for row in (COEF_LBLK_RE, COEF_LBLK_IM))
               for k in tiles]

        def body(j, carry, cols=cols, lbs=lbs):
            r0 = pl.multiple_of(j * SUBLANES, SUBLANES)
            out = []
            for q, ((c_re, c_im), (lr, li)) in enumerate(zip(cols, lbs)):
                hr, hi = carry[2 * q], carry[2 * q + 1]
                sr = s_ref[pl.ds(r0, SUBLANES), c_re:c_re + TILE_STATE]
                si = s_ref[pl.ds(r0, SUBLANES), c_im:c_im + TILE_STATE]
                s_ref[pl.ds(r0, SUBLANES), c_re:c_re + TILE_STATE] = hr
                s_ref[pl.ds(r0, SUBLANES), c_im:c_im + TILE_STATE] = hi
                out += [lr * hr - li * hi + sr, lr * hi + li * hr + si]
            return tuple(out)

        init = tuple(st_ref[:, c:c + TILE_STATE] for c_pair in cols for c in c_pair)
        fin = lax.fori_loop(0, nblk, body, init, unroll=2)
        for q, (c_re, c_im) in enumerate(cols):
            st_ref[:, c_re:c_re + TILE_STATE] = fin[2 * q]
            st_ref[:, c_im:c_im + TILE_STATE] = fin[2 * q + 1]

    yk = []
    for k in range(N_LANE_TILES):
        h_in = s_ref[:, 2 * TILE_STATE * k:2 * TILE_STATE * (k + 1)].astype(_BF16)
        yk.append(_dot(jnp.concatenate([h_in, xk[k]], axis=-1), r_ref[k]))
    sgs = by_position(sg_ref)
    outs = []
    for i in range(SSM_BLK):
        y = jnp.concatenate([y_k[:, i * LANES:(i + 1) * LANES] for y_k in yk], axis=-1) + dsk_ref[...] * xs[i]
        outs.append(_rms(_gelu(y) * sgs[i], gos_ref[...]).reshape(nblk, SUBLANES, D_SSM))
    mixa = jnp.stack(outs, axis=1).reshape(lc, SUBLANES, D_SSM)
    mixa_ref[...] = pltpu.einshape("tbc->btc", mixa).astype(_BF16)
    hfin_ref[...] = st_ref[...]


def _ssm_prompt(xa, sg, v, r, coef, dsk, gos):
    n, l, _ = xa.shape
    lc = SSM_LC
    const = lambda *shape: pl.BlockSpec(shape, lambda i: (0,) * len(shape))
    seq_spec = pl.BlockSpec((n, lc, D_SSM), lambda i: (0, i, 0))
    return pl.pallas_call(
        _ssm_prompt_kernel,
        grid=(l // lc,),
        in_specs=[seq_spec, seq_spec, const(*v.shape), const(*r.shape),
                  const(*coef.shape), const(1, D_SSM), const(1, D_SSM)],
        out_specs=[seq_spec, const(n, 2 * STATE_COLS)],
        out_shape=[jax.ShapeDtypeStruct((n, l, D_SSM), _BF16),
                   jax.ShapeDtypeStruct((n, 2 * STATE_COLS), _F32)],
        scratch_shapes=[pltpu.VMEM((lc // SSM_BLK * n, 2 * STATE_COLS), _F32),
                        pltpu.VMEM((n, 2 * STATE_COLS), _F32)],
        compiler_params=pltpu.CompilerParams(
            dimension_semantics=("arbitrary",), vmem_limit_bytes=VMEM_LIMIT),
        name="ssm_prompt",
    )(xa, sg, v, r, coef, dsk, gos)


def _front_sample_kernel(x_ref, g1_ref, win_ref, gn_ref, w00_ref, b0_ref, gog_ref,
                         wb_ref, wc_ref, coef_ref, dsk_ref, gos_ref, h0r_ref, h0i_ref,
                         mix_ref, hr_ref, hi_ref, vrow_ref):
    x = x_ref[...]
    hn = _rms(x, g1_ref[...])
    z = _dot_hi(hn, win_ref[...])
    xa = z[:, :D_SSM]
    ys = []
    for k in range(N_LANE_TILES):
        bu = _dot_hi(xa[:, k * LANES:(k + 1) * LANES], wb_ref[k])
        sl = slice(k * TILE_STATE, (k + 1) * TILE_STATE)
        lr, li = coef_ref[COEF_LB_RE:COEF_LB_RE + 1, sl], coef_ref[COEF_LB_IM:COEF_LB_IM + 1, sl]
        h0r, h0i = h0r_ref[:, sl], h0i_ref[:, sl]
        nr = lr * h0r - li * h0i + bu[:, :TILE_STATE]
        ni = lr * h0i + li * h0r + bu[:, TILE_STATE:]
        hr_ref[:, sl] = nr
        hi_ref[:, sl] = ni
        ys.append(_dot_hi(jnp.concatenate([nr, ni], axis=-1), wc_ref[k]))
    y = jnp.concatenate(ys, axis=-1) + dsk_ref[...] * xa
    ya = _gelu(y) * jax.nn.sigmoid(z[:, D_SSM:2 * D_SSM])
    mix_ref[:, :D_SSM] = _rms(ya, gos_ref[...])
    ub = _gelu(z[:, 2 * D_SSM:2 * D_SSM + D_GMLP])
    vbn = _head_norm_gelu(z[:, 2 * D_SSM + D_GMLP:], gn_ref[...])
    vrow_ref[...] = vbn
    s = w00_ref[...] * vbn + b0_ref[...]
    mix_ref[:, D_SSM:] = _rms(ub * s, gog_ref[...])


def _front_sample(x, g1, win, gn, w00, b0, gog, wb, wc, coef, dsk, gos, h0r, h0i):
    n = x.shape[0]
    vmem = pl.BlockSpec(memory_space=pltpu.VMEM)
    return pl.pallas_call(
        _front_sample_kernel,
        in_specs=[vmem] * 14,
        out_specs=[vmem] * 4,
        out_shape=[jax.ShapeDtypeStruct((n, D_MODEL), _F32),
                   jax.ShapeDtypeStruct((n, STATE_COLS), _F32),
                   jax.ShapeDtypeStruct((n, STATE_COLS), _F32),
                   jax.ShapeDtypeStruct((n, D_GMLP), _F32)],
        compiler_params=pltpu.CompilerParams(vmem_limit_bytes=VMEM_LIMIT),
        name="front_sample",
    )(x, g1, win, gn, w00, b0, gog, wb, wc, coef, dsk, gos, h0r, h0i)


def _route(logits, base):
    tm = logits.shape[0]
    lt = logits.T
    ex = lt[:N_EXPERTS, :]
    gr = lt[N_EXPERTS:N_EXPERTS + SUBLANES, :]
    row_e = lax.broadcasted_iota(jnp.int32, ex.shape, 0).astype(_F32)
    row_g = lax.broadcasted_iota(jnp.int32, gr.shape, 0).astype(_F32)
    neg = jnp.float32(-jnp.inf)
    big = jnp.float32(LANES)
    is_g = row_g < N_EXPERT_GROUPS
    gl = jnp.where(is_g, gr, neg)
    gmax = jnp.max(gl, axis=0, keepdims=True)
    gi = jnp.min(jnp.where(gl == gmax, row_g, big), axis=0, keepdims=True)
    p_top = 1.0 / jnp.sum(jnp.where(is_g, jnp.exp(gl - gmax), 0.0), axis=0, keepdims=True)
    lo = gi * EXPERTS_PER_GROUP
    in_grp = (row_e >= lo) & (row_e < lo + EXPERTS_PER_GROUP)
    m1 = jnp.max(jnp.where(in_grp, ex, neg), axis=0, keepdims=True)
    i1 = jnp.min(jnp.where(in_grp & (ex == m1), row_e, big), axis=0, keepdims=True)
    rest = in_grp & (row_e != i1)
    m2 = jnp.max(jnp.where(rest, ex, neg), axis=0, keepdims=True)
    i2 = jnp.min(jnp.where(rest & (ex == m2), row_e, big), axis=0, keepdims=True)
    e2 = jnp.exp(m2 - m1)
    w1 = p_top / (1.0 + e2)
    w2 = p_top * e2 / (1.0 + e2)
    sel1 = row_e == i1
    sel2 = row_e == i2
    hits = jnp.where(sel1 | sel2, 1.0, 0.0)
    src = lax.broadcasted_iota(jnp.int32, (tm, tm), 0)
    dst = lax.broadcasted_iota(jnp.int32, (tm, tm), 1)
    before = _dot(hits.astype(_BF16), jnp.where(src < dst, 1.0, 0.0).astype(_BF16)) + base
    rank1 = jnp.sum(jnp.where(sel1, before, 0.0), axis=0, keepdims=True)
    rank2 = jnp.sum(jnp.where(sel2, before, 0.0), axis=0, keepdims=True)
    fields = {R_E1: i1, R_E2: i2, R_W1: w1, R_W2: w2, R_RANK1: rank1, R_RANK2: rank2,
              R_CODE1: i1 * CODE_SHIFT + rank1, R_CODE2: i2 * CODE_SHIFT + rank2}
    row8 = lax.broadcasted_iota(jnp.int32, (SUBLANES, tm), 0)
    route_t = jnp.zeros((SUBLANES, tm), _F32)
    for r, val in fields.items():
        route_t = jnp.where(row8 == r, val, route_t)
    route = jnp.concatenate([route_t, jnp.zeros((LANES - SUBLANES, tm), _F32)], axis=0).T
    return route_t, route, base + jnp.sum(hits, axis=1, keepdims=True)


def _mixer_out_prompt_kernel(x_ref, mixa_ref, mixb_ref, wo_ref, g2_ref, wr_ref, br_ref,
                             x1_ref, xn_ref, route_ref, route_t_ref, cnt_ref, base_ref, logits_ref, wo_bf):
    i = pl.program_id(0)

    @pl.when(i == 0)
    def _():
        base_ref[...] = jnp.zeros_like(base_ref)
        logits_ref[...] = jnp.zeros_like(logits_ref)
        wo_bf[...] = wo_ref[...].astype(_BF16)

    prev_logits = logits_ref[...]
    x1 = x_ref[0] + _dot(mixa_ref[0], wo_bf[:D_SSM, :]) + _dot(mixb_ref[0], wo_bf[D_SSM:, :])
    xn = _rms(x1, g2_ref[...])
    x1_ref[...] = _pack_bf16_pair(x1)
    xn_ref[...] = _pack_bf16_pair(xn)
    logits_ref[...] = _dot(xn.astype(_BF16), wr_ref[...]) + br_ref[...]
    route_t, route, base = _route(prev_logits, base_ref[...])
    route_ref[...] = route
    route_t_ref[...] = route_t
    base = jnp.where(i >= 1, base, base_ref[...])
    base_ref[...] = base
    cnt_ref[...] = base


def _mixer_out_sample_kernel(x_ref, mix_ref, wo_ref, g2_ref, wr_ref, br_ref, cnt_in_ref,
                             x1_in, xn_in, route_in, route_t_in,
                             x1_ref, xn_ref, route_ref, route_t_ref, cnt_ref):
    del x1_in, xn_in, route_in, route_t_in
    x1 = (x_ref[...] + _dot_hi(mix_ref[:, :D_SSM], wo_ref[:D_SSM, :])
          + _dot_hi(mix_ref[:, D_SSM:], wo_ref[D_SSM:, :]))
    xn = _rms(x1, g2_ref[...])
    logits = _dot_hi(xn, wr_ref[...]) + br_ref[...]
    route_t, route, base = _route(logits, cnt_in_ref[...])
    x1_ref[...] = _pack_bf16_pair(x1)
    xn_ref[...] = _pack_bf16_pair(xn)
    route_ref[...] = route
    route_t_ref[...] = route_t
    cnt_ref[...] = base


def _mixer_out(x_p, mixa, mixb, x_s, mix_s, wo, g2, wr, br):
    n, l, d = x_p.shape
    ns = x_s.shape[0]
    t_all = n * l + ns
    tm = TOK_TM
    per_seq = l // tm
    n_tiles = n * per_seq
    cur = lambda i: jnp.minimum(i, n_tiles - 1)
    prev = lambda i: jnp.maximum(i - 1, 0)
    const = lambda *shape: pl.BlockSpec(shape, lambda i: (0,) * len(shape))
    seq = lambda w: pl.BlockSpec((1, tm, w), lambda i: (cur(i) // per_seq, cur(i) % per_seq, 0))
    tok = lambda w, which: pl.BlockSpec((tm, w), lambda i: (which(i), 0))
    tok_shapes = [jax.ShapeDtypeStruct((t_all, d // 2), _U32),
                  jax.ShapeDtypeStruct((t_all, d // 2), _U32),
                  jax.ShapeDtypeStruct((t_all, LANES), _F32),
                  jax.ShapeDtypeStruct((SUBLANES, t_all), _F32)]
    cnt_shape = jax.ShapeDtypeStruct((N_EXPERTS, 1), _F32)
    x1, xn, route, route_t, cnt = pl.pallas_call(
        _mixer_out_prompt_kernel,
        grid=(n_tiles + 1,),
        in_specs=[seq(d), seq(D_SSM), seq(D_GMLP),
                  const(d, d), const(1, d), const(d, LANES), const(1, LANES)],
        out_specs=[tok(d // 2, cur), tok(d // 2, cur), tok(LANES, prev),
                   pl.BlockSpec((SUBLANES, tm), lambda i: (0, prev(i))), const(N_EXPERTS, 1)],
        out_shape=tok_shapes + [cnt_shape],
        scratch_shapes=[pltpu.VMEM((N_EXPERTS, 1), _F32), pltpu.VMEM((tm, LANES), _F32),
                        pltpu.VMEM((d, d), _BF16)],
        compiler_params=pltpu.CompilerParams(
            dimension_semantics=("arbitrary",), vmem_limit_bytes=VMEM_LIMIT),
        name="mixer_out_prompt",
    )(x_p, mixa, mixb, wo, g2, wr.astype(_BF16), br)
    tail = (n * l) // ns
    c1 = lambda *shape: pl.BlockSpec(shape, lambda i: (0,) * len(shape))
    anyspec = pl.BlockSpec(memory_space=pl.ANY)
    tail_spec = lambda w: pl.BlockSpec((ns, w), lambda i: (tail, 0))
    return pl.pallas_call(
        _mixer_out_sample_kernel,
        grid=(1,),
        in_specs=[c1(ns, d), c1(ns, d), c1(d, d), c1(1, d), c1(d, LANES), c1(1, LANES), c1(N_EXPERTS, 1),
                  anyspec, anyspec, anyspec, anyspec],
        out_specs=[tail_spec(d // 2), tail_spec(d // 2), tail_spec(LANES),
                   pl.BlockSpec((SUBLANES, ns), lambda i: (0, tail)), c1(N_EXPERTS, 1)],
        out_shape=tok_shapes + [cnt_shape],
        input_output_aliases={7: 0, 8: 1, 9: 2, 10: 3},
        compiler_params=pltpu.CompilerParams(
            dimension_semantics=("arbitrary",), vmem_limit_bytes=VMEM_LIMIT),
        name="mixer_out_sample",
    )(x_s, mix_s, wo, g2, wr, br, cnt, x1, xn, route, route_t)


def _sc_stream(n_chunks, gather, write):
    gather(0).start()
    for j in range(n_chunks):
        if j + 1 < n_chunks:
            if j >= 1:
                write(j - 1).wait()
            gather(j + 1).start()
        gather(j).wait()
        write(j).start()
    if n_chunks >= 2:
        write(n_chunks - 2).wait()
    write(n_chunks - 1).wait()


def _sc_mesh():
    return plsc.VectorSubcoreMesh(core_axis_name="c", subcore_axis_name="s",
                                  num_cores=SC_CORES, num_subcores=SC_SUBCORES)


def _sc_buffers(chunk, w, dtype):
    return [pltpu.VMEM((chunk, w), dtype), pltpu.VMEM((chunk, w), dtype)] + [pltpu.SemaphoreType.DMA] * 4


def _sc_combine(table, idx, n_out, chunk):
    w = table.shape[1]
    rows_w = n_out // SC_WORKERS
    n_chunks = rows_w // chunk
    assert rows_w * SC_WORKERS == n_out and n_chunks * chunk == rows_w and rows_w % SUBLANES == 0

    def body(table_hbm, idx_hbm, out_hbm, idx_v, buf0, buf1, g0, g1, w0, w1):
        wid = lax.axis_index("s") * SC_CORES + lax.axis_index("c")
        base = pl.multiple_of(wid * rows_w, SUBLANES)
        pltpu.sync_copy(idx_hbm.at[pl.ds(base, rows_w)], idx_v)
        bufs, gsems, wsems = (buf0, buf1), (g0, g1), (w0, w1)

        def gather(j):
            return pltpu.make_async_copy(table_hbm.at[idx_v.at[pl.ds(j * chunk, chunk)]], bufs[j % 2], gsems[j % 2])

        def write(j):
            return pltpu.make_async_copy(bufs[j % 2], out_hbm.at[pl.ds(base + j * chunk, chunk)], wsems[j % 2])

        _sc_stream(n_chunks, gather, write)

    return pl.kernel(
        body,
        out_type=jax.ShapeDtypeStruct((n_out, w), table.dtype),
        mesh=_sc_mesh(),
        scratch_types=[pltpu.VMEM((rows_w,), jnp.int32)] + _sc_buffers(chunk, w, table.dtype),
        compiler_params=pltpu.CompilerParams(use_tc_tiling_on_sc=True),
        name="sc_combine",
    )(table, idx)


def _sc_dispatch(table, codes, start_row, n_out, chunk):
    t_all, w = table.shape
    n_pad = codes.shape[0]
    n_ent = 2 * t_all
    ent_w = n_pad // SC_WORKERS
    n_chunks = ent_w // chunk
    per_chunk = chunk // SC_LANES
    trash = n_out - (n_pad - n_ent)
    assert ent_w * SC_WORKERS == n_pad and n_chunks * chunk == ent_w
    assert per_chunk * SC_LANES == chunk and chunk <= LANES and n_pad - n_ent <= t_all

    def body(table_hbm, code_hbm, start_hbm, out_hbm, dest_hbm,
             code_v, dest_v, tok_v, dst_v, start_v, buf0, buf1, g0, g1, w0, w1):
        wid = lax.axis_index("s") * SC_CORES + lax.axis_index("c")
        ebase = pl.multiple_of(wid * ent_w, SUBLANES)
        pltpu.sync_copy(code_hbm.at[pl.ds(ebase, ent_w)], code_v)
        pltpu.sync_copy(start_hbm, start_v)
        lane = lax.iota(jnp.int32, SC_LANES)
        for j in range(n_chunks):
            for c in range(per_chunk):
                off = j * chunk + c * SC_LANES
                ent = ebase + off + lane
                code = code_v[pl.ds(off, SC_LANES)]
                d = plsc.load_gather(start_v, [code >> CODE_BITS]) + (code & ((1 << CODE_BITS) - 1))
                d = jnp.where(ent >= n_ent, trash + (ent - n_ent), d)
                tok = jnp.where(ent >= t_all, ent - t_all, ent)
                tok = jnp.where(tok >= t_all, tok - t_all, tok)
                dest_v[pl.ds(off, SC_LANES)] = d
                dst_v[j, pl.ds(c * SC_LANES, SC_LANES)] = d
                tok_v[j, pl.ds(c * SC_LANES, SC_LANES)] = tok
        pltpu.sync_copy(dest_v, dest_hbm.at[pl.ds(ebase, ent_w)])
        bufs, gsems, wsems = (buf0, buf1), (g0, g1), (w0, w1)

        def gather(j):
            return pltpu.make_async_copy(table_hbm.at[tok_v.at[j]], bufs[j % 2], gsems[j % 2])

        def scatter(j):
            return pltpu.make_async_copy(bufs[j % 2], out_hbm.at[dst_v.at[j]], wsems[j % 2])

        _sc_stream(n_chunks, gather, scatter)

    return pl.kernel(
        body,
        out_type=(jax.ShapeDtypeStruct((n_out, w), table.dtype), jax.ShapeDtypeStruct((n_pad,), jnp.int32)),
        mesh=_sc_mesh(),
        scratch_types=([pltpu.VMEM((ent_w,), jnp.int32), pltpu.VMEM((ent_w,), jnp.int32),
                        pltpu.VMEM((n_chunks, chunk), jnp.int32), pltpu.VMEM((n_chunks, chunk), jnp.int32),
                        pltpu.VMEM((LANES,), jnp.int32)] + _sc_buffers(chunk, w, table.dtype)),
        compiler_params=pltpu.CompilerParams(use_tc_tiling_on_sc=True, needs_layout_passes=False),
        name="sc_dispatch",
    )(table, codes, start_row)


def _experts_kernel(piece_start_ref, piece_row_ref, piece_cls_ref, wg_ref, wu_ref, wd_ref, xs_hbm, ys_hbm,
                    wg_bf, wu_bf, wd_bf, xbuf, ybuf, xsem, ysem):
    e = pl.program_id(0)
    g0 = piece_start_ref[e]
    n_here = piece_start_ref[e + 1] - g0
    n_total = piece_start_ref[N_EXPERTS]

    def per_class(g, fn):
        cls = piece_cls_ref[g]
        row = pl.multiple_of(piece_row_ref[g], EXP_UNIT)
        for c in range(1, EXP_CLASSES + 1):
            pl.when(cls == c)(lambda c=c: fn(c * EXP_UNIT, row))

    def x_copy(slot, rows, row):
        return pltpu.make_async_copy(xs_hbm.at[pl.ds(row, rows)], xbuf.at[slot, pl.ds(0, rows)], xsem.at[slot])

    def y_copy(slot, rows, row):
        return pltpu.make_async_copy(ybuf.at[slot, pl.ds(0, rows)], ys_hbm.at[pl.ds(row, rows)], ysem.at[slot])

    @pl.when((e == 0) & (n_total > 0))
    def _():
        per_class(0, lambda rows, row: x_copy(0, rows, row).start())

    wg_bf[...] = wg_ref[0].astype(_BF16)
    wu_bf[...] = wu_ref[0].astype(_BF16)
    wd_bf[...] = wd_ref[0].astype(_BF16)

    def piece(j, carry):
        g = g0 + j
        slot = lax.rem(g, 2)
        per_class(g, lambda rows, row: x_copy(slot, rows, row).wait())

        @pl.when(g + 1 < n_total)
        def _():
            per_class(g + 1, lambda rows, row: x_copy(1 - slot, rows, row).start())

        @pl.when(g >= 2)
        def _():
            per_class(g - 2, lambda rows, row: y_copy(slot, rows, row).wait())

        def compute(rows, row):
            x = _unpack_bf16_pair(xbuf[slot, pl.ds(0, rows)]).astype(_BF16)
            a = _dot(x, wg_bf[...])
            u = _dot(x, wu_bf[...])
            h = (a * jax.nn.sigmoid(a) * u).astype(_BF16)
            ybuf[slot, pl.ds(0, rows)] = _pack_bf16_pair(_dot(h, wd_bf[...]))
            y_copy(slot, rows, row).start()

        per_class(g, compute)
        return carry

    lax.fori_loop(0, n_here, piece, 0)

    @pl.when(e == N_EXPERTS - 1)
    def _():
        @pl.when(n_total >= 2)
        def _():
            per_class(n_total - 2, lambda rows, row: y_copy(lax.rem(n_total, 2), rows, row).wait())

        @pl.when(n_total >= 1)
        def _():
            per_class(n_total - 1, lambda rows, row: y_copy(lax.rem(n_total - 1, 2), rows, row).wait())


def _experts(piece_start, piece_row, piece_cls, n_rows, xs, w_gate, w_up, w_down):
    dh = xs.shape[1]
    d = 2 * dh
    tm = EXP_UNIT * EXP_CLASSES
    anyspec = pl.BlockSpec(memory_space=pl.ANY)
    wsel = lambda e, ps, pr, pc: (e, 0, 0)
    grid_spec = pltpu.PrefetchScalarGridSpec(
        num_scalar_prefetch=3,
        grid=(N_EXPERTS,),
        in_specs=[pl.BlockSpec((1, d, D_EXPERT), wsel), pl.BlockSpec((1, d, D_EXPERT), wsel),
                  pl.BlockSpec((1, D_EXPERT, d), wsel), anyspec],
        out_specs=anyspec,
        scratch_shapes=[pltpu.VMEM((d, D_EXPERT), _BF16), pltpu.VMEM((d, D_EXPERT), _BF16),
                        pltpu.VMEM((D_EXPERT, d), _BF16),
                        pltpu.VMEM((2, tm, dh), _U32), pltpu.VMEM((2, tm, dh), _U32),
                        pltpu.SemaphoreType.DMA((2,)), pltpu.SemaphoreType.DMA((2,))],
    )
    return pl.pallas_call(
        _experts_kernel,
        grid_spec=grid_spec,
        out_shape=jax.ShapeDtypeStruct((n_rows, dh), _U32),
        compiler_params=pltpu.CompilerParams(
            dimension_semantics=("arbitrary",), vmem_limit_bytes=VMEM_LIMIT),
        name="experts",
    )(piece_start, piece_row, piece_cls, w_gate, w_up, w_down, xs)


def _final_kernel(x1_ref, ya_ref, yb_ref, route_ref, gf_ref, y_ref):
    route = route_ref[...]
    x2 = (_unpack_bf16_pair(x1_ref[...]) + route[:, R_W1:R_W1 + 1] * _unpack_bf16_pair(ya_ref[...])
          + route[:, R_W2:R_W2 + 1] * _unpack_bf16_pair(yb_ref[...]))
    y_ref[...] = _rms(x2, gf_ref[...])


def _final(x1, yab, route, gf, n_prompt, n_sample):
    d = 2 * x1.shape[1]

    def call(tm, first_block, n_rows, name):
        tok = lambda w: pl.BlockSpec((tm, w), lambda i: (first_block + i, 0))
        sel = lambda k: pl.BlockSpec((None, tm, d // 2), lambda i: (k, first_block + i, 0))
        return pl.pallas_call(
            _final_kernel,
            grid=(n_rows // tm,),
            in_specs=[tok(d // 2), sel(0), sel(1), tok(LANES), pl.BlockSpec((1, d), lambda i: (0, 0))],
            out_specs=pl.BlockSpec((tm, d), lambda i: (i, 0)),
            out_shape=jax.ShapeDtypeStruct((n_rows, d), _F32),
            compiler_params=pltpu.CompilerParams(
                dimension_semantics=("arbitrary",), vmem_limit_bytes=VMEM_LIMIT),
            name=name,
        )(x1, yab, yab, route, gf)

    return (call(FINAL_TM, 0, n_prompt, "final_prompt"),
            call(n_sample, n_prompt // n_sample, n_sample, "final_sample"))


def _powers(lam_re, lam_im, dt):
    out = []
    for m in range(SSM_BLK + 1):
        mag = jnp.exp(m * lam_re * dt)
        ang = m * lam_im * dt
        out.append((mag * jnp.cos(ang), mag * jnp.sin(ang)))
    return out


def _spread(x, copies):
    w = x.shape[1]
    src = lax.broadcasted_iota(jnp.int32, (w, w * copies), 0)
    dst = lax.broadcasted_iota(jnp.int32, (w, w * copies), 1)
    return _dot_f32(x, jnp.where(dst % w == src, 1.0, 0.0))


def _ssm_prep_kernel(lam_ref, b_re, b_im, c_re, c_im, v_ref, r_ref, wb_ref, wc_ref, coef_ref):
    n_p, n_h = SSM_STATE, SSM_GROUP
    lr, li, dt = lam_ref[0:1, :], lam_ref[1:2, :], lam_ref[2:3, :]
    pw = _powers(lr, li, dt)
    den = lr * lr + li * li
    nr, ni = pw[1][0] - 1.0, pw[1][1]
    k_re = (nr * lr + ni * li) / den
    k_im = (ni * lr - nr * li) / den
    coef_ref[...] = jnp.concatenate(
        [pw[1][0], pw[1][1], pw[SSM_BLK][0], pw[SSM_BLK][1], jnp.zeros((SUBLANES - 4, TILE_STATE), _F32)], axis=0)

    on_diag_b = (lax.broadcasted_iota(jnp.int32, (TILE_STATE, LANES), 0) // n_p
                 == lax.broadcasted_iota(jnp.int32, (TILE_STATE, LANES), 1) // n_h)
    rows_gp = lambda ref: ref[...].reshape(TILE_STATE, n_h)
    bt_re = jnp.where(on_diag_b, _spread(rows_gp(b_re), SUBLANES), 0.0).T
    bt_im = jnp.where(on_diag_b, _spread(rows_gp(b_im), SUBLANES), 0.0).T
    bb_re = k_re * bt_re - k_im * bt_im
    bb_im = k_re * bt_im + k_im * bt_re
    wb_ref[0] = jnp.concatenate([bb_re, bb_im], axis=1)
    v_rows = []
    for s in range(SSM_BLK):
        pr, pi = pw[SSM_BLK - 1 - s]
        v_rows.append(jnp.concatenate([pr * bb_re - pi * bb_im, pr * bb_im + pi * bb_re], axis=1))
    v_ref[0] = jnp.concatenate(v_rows, axis=0).astype(v_ref.dtype)

    on_diag_c = (lax.broadcasted_iota(jnp.int32, (LANES, TILE_STATE), 0) // n_h
                 == lax.broadcasted_iota(jnp.int32, (LANES, TILE_STATE), 1) // n_p)
    rows_gh = lambda ref: ref[...].reshape(LANES, n_p)
    ct_re = jnp.where(on_diag_c, _spread(rows_gh(c_re), SUBLANES), 0.0)
    ct_im = jnp.where(on_diag_c, _spread(rows_gh(c_im), SUBLANES), 0.0)
    cl = [(ct_re * pr - ct_im * pi, ct_re * pi + ct_im * pr) for pr, pi in pw]
    wc_ref[0] = jnp.concatenate([cl[0][0], -cl[0][1]], axis=1).T
    direct = [_dot_hi(cl[m][0], bb_re, transpose_b=True) - _dot_hi(cl[m][1], bb_im, transpose_b=True)
              for m in range(SSM_BLK)]
    zero = jnp.zeros((LANES, LANES), _F32)
    rt = jnp.concatenate(
        [jnp.concatenate([cl[i + 1][0], -cl[i + 1][1]]
                         + [direct[i - s] if s <= i else zero for s in range(SSM_BLK)], axis=1)
         for i in range(SSM_BLK)], axis=0)
    r_ref[0] = rt.T.astype(r_ref.dtype)


def _ssm_params(lam_re, lam_im, log_dt, b_re, b_im, c_re, c_im, d_skip):
    n_g, n_p, n_h = N_SSM_GROUPS, SSM_STATE, SSM_GROUP
    dt = jnp.repeat(jnp.exp(log_dt), n_p)
    lam = jnp.zeros((SUBLANES, STATE_COLS), _F32).at[0].set(lam_re.reshape(-1)).at[1].set(
        lam_im.reshape(-1)).at[2].set(dt)
    groups = lambda r, c: pl.BlockSpec((SUBLANES, r, c), lambda k: (k, 0, 0))
    out3 = lambda rows, w: pl.BlockSpec((1, rows, w), lambda k: (k, 0, 0))
    cols = pl.BlockSpec((SUBLANES, TILE_STATE), lambda k: (0, k))
    k_blk = SSM_BLK * LANES
    v, r, wb, wc, coef = pl.pallas_call(
        _ssm_prep_kernel,
        grid=(N_LANE_TILES,),
        in_specs=[cols, groups(n_p, n_h), groups(n_p, n_h), groups(n_h, n_p), groups(n_h, n_p)],
        out_specs=[out3(k_blk, 2 * TILE_STATE), out3(2 * TILE_STATE + k_blk, k_blk),
                   out3(LANES, 2 * TILE_STATE), out3(2 * TILE_STATE, LANES), cols],
        out_shape=[jax.ShapeDtypeStruct((N_LANE_TILES, k_blk, 2 * TILE_STATE), _BF16),
                   jax.ShapeDtypeStruct((N_LANE_TILES, 2 * TILE_STATE + k_blk, k_blk), _BF16),
                   jax.ShapeDtypeStruct((N_LANE_TILES, LANES, 2 * TILE_STATE), _F32),
                   jax.ShapeDtypeStruct((N_LANE_TILES, 2 * TILE_STATE, LANES), _F32),
                   jax.ShapeDtypeStruct((SUBLANES, STATE_COLS), _F32)],
        compiler_params=pltpu.CompilerParams(
            dimension_semantics=("arbitrary",), vmem_limit_bytes=VMEM_LIMIT),
        name="ssm_prep",
    )(lam, b_re, b_im, c_re, c_im)
    return wb, wc, v, r, coef, d_skip.reshape(1, D_SSM)


def _dispatch_plan(route_t, cnt):
    t_all = route_t.shape[1]
    codes = route_t[R_CODE1:R_CODE2 + 1].astype(jnp.int32).reshape(-1)
    per_pass = SC_WORKERS * DISPATCH_CHUNK
    codes = jnp.pad(codes, (0, -(2 * t_all) % per_pass))
    counts = cnt[:, 0].astype(jnp.int32)
    zero = jnp.zeros((1,), jnp.int32)
    units = (counts + EXP_UNIT - 1) // EXP_UNIT
    unit_start = jnp.concatenate([zero, jnp.cumsum(units)])
    start_row = jnp.zeros((LANES,), jnp.int32).at[:N_EXPERTS].set(unit_start[:N_EXPERTS] * EXP_UNIT)
    pieces = (units + EXP_CLASSES - 1) // EXP_CLASSES
    piece_start = jnp.concatenate([zero, jnp.cumsum(pieces)])
    tm = EXP_UNIT * EXP_CLASSES
    max_units = (2 * t_all + N_EXPERTS * (EXP_UNIT - 1)) // EXP_UNIT
    max_pieces = (max_units + N_EXPERTS * (EXP_CLASSES - 1)) // EXP_CLASSES
    g = jnp.arange(max_pieces, dtype=jnp.int32)
    owner = ((g[:, None] >= piece_start[None, :-1]) & (g[:, None] < piece_start[None, 1:])).astype(jnp.int32)
    pick = lambda table: jnp.sum(owner * table[None, :], axis=1)
    first_unit = pick(unit_start[:-1]) + (g - pick(piece_start[:-1])) * EXP_CLASSES
    piece_row = first_unit * EXP_UNIT
    piece_cls = jnp.clip(pick(unit_start[1:]) - first_unit, 1, EXP_CLASSES)
    n_rows = (max_units * EXP_UNIT + tm - 1) // tm * tm + tm
    return codes, start_row, n_rows, piece_start, piece_row, piece_cls


def kernel(x_prompt, x_sample, state_ssm_re, state_ssm_im, norm1_g, w_in, lam_re, lam_im, log_dt, ssm_b_re, ssm_b_im, ssm_c_re, ssm_c_im, ssm_d, gmlp_norm_g, gmlp_w_s, gmlp_b_s, out_norm_ssm_g, out_norm_gmlp_g, w_out, norm2_g, w_router_group, b_router_group, w_router_expert, b_router_expert, w_gate, w_up, w_down, final_norm_g):
    n, l, d = x_prompt.shape
    ns = x_sample.shape[0]
    t_all = n * l + ns
    li = 0
    g1 = norm1_g[li].reshape(1, d)
    gn = gmlp_norm_g[li].reshape(1, D_GMLP)
    tril = jnp.tril(jnp.ones((CHUNK, CHUNK), dtype=bool))
    ws_tril = jnp.where(tril[None], gmlp_w_s[li], 0.0)
    bs = gmlp_b_s[li]
    gog = out_norm_gmlp_g[li].reshape(1, D_GMLP)
    gos = out_norm_ssm_g[li].reshape(1, D_SSM)
    wb, wc, v_blk, r_blk, coef, dsk = _ssm_params(
        lam_re[li], lam_im[li], log_dt[li], ssm_b_re[li], ssm_b_im[li], ssm_c_re[li], ssm_c_im[li], ssm_d[li])
    g2 = norm2_g[li].reshape(1, d)
    pad = LANES - N_EXPERTS - N_EXPERT_GROUPS
    wr = jnp.concatenate([w_router_expert[li], w_router_group[li], jnp.zeros((d, pad), _F32)], axis=1)
    br = jnp.concatenate([b_router_expert[li], b_router_group[li], jnp.zeros((pad,), _F32)]).reshape(1, LANES)

    xa, sg, mixb = _front_prompt(x_prompt, g1, w_in[li], gn, ws_tril.astype(_BF16), bs.T, gog)
    mixa, hfin = _ssm_prompt(xa, sg, v_blk, r_blk, coef, dsk, gos)
    w00 = jnp.repeat(ws_tril[:, 0, 0], GMLP_HEAD).reshape(1, D_GMLP)
    b0 = jnp.repeat(bs[:, 0], GMLP_HEAD).reshape(1, D_GMLP)
    mix_s, hr_s, hi_s, vrow = _front_sample(
        x_sample.reshape(ns, d), g1, w_in[li], gn, w00, b0, gog, wb, wc, coef, dsk, gos,
        state_ssm_re[li].reshape(ns, STATE_COLS), state_ssm_im[li].reshape(ns, STATE_COLS))

    x1, xn, route, route_t, cnt = _mixer_out(x_prompt, mixa, mixb, x_sample.reshape(ns, d), mix_s,
                                             w_out[li], g2, wr, br)
    codes, start_row, n_rows, piece_start, piece_row, piece_cls = _dispatch_plan(route_t, cnt)
    xs, dest = _sc_dispatch(xn, codes, start_row, n_rows, DISPATCH_CHUNK)
    ys = _experts(piece_start, piece_row, piece_cls, n_rows, xs, w_gate[li], w_up[li], w_down[li])
    yab = _sc_combine(ys, dest, 2 * t_all, COMBINE_CHUNK).reshape(2, t_all, d // 2)
    y_p, y_s = _final(x1, yab, route, final_norm_g.reshape(1, d), n * l, ns)

    hf = hfin.reshape(n, N_LANE_TILES, 2, 8, SSM_STATE)
    re_p = hf[:, :, 0].reshape(1, n, N_SSM_GROUPS, SSM_STATE)
    im_p = hf[:, :, 1].reshape(1, n, N_SSM_GROUPS, SSM_STATE)
    re_s = hr_s.reshape(1, ns, N_SSM_GROUPS, SSM_STATE)
    im_s = hi_s.reshape(1, ns, N_SSM_GROUPS, SSM_STATE)
    return (y_p.reshape(n, l, d), y_s.reshape(ns, 1, d), re_p, im_p, re_s, im_s,
            vrow.reshape(1, ns, 1, D_GMLP))
```

```python
import math

import jax
import jax.numpy as jnp
from jax import lax
from jax.experimental import pallas as pl
from jax.experimental.pallas import tpu as pltpu
from jax.experimental.pallas import tpu_sc as plsc

D_MODEL = 1024
D_SSM = 512
D_GMLP = 512
SSM_GROUP = 16
N_SSM_GROUPS = 32
SSM_STATE = 64
CHUNK = 128
N_GMLP_HEADS = 4
GMLP_HEAD = 128
N_EXPERT_GROUPS = 4
EXPERTS_PER_GROUP = 8
N_EXPERTS = 32
D_EXPERT = 512
D_IN = 2048
EPS = 1e-6

LANES = 128
SUBLANES = 8
N_LANE_TILES = D_SSM // LANES
STATE_COLS = N_SSM_GROUPS * SSM_STATE
TILE_STATE = STATE_COLS // N_LANE_TILES
VMEM_LIMIT = 56 * 1024 * 1024

SC_CORES = 2
SC_SUBCORES = 16
SC_LANES = 16
SC_WORKERS = SC_CORES * SC_SUBCORES

FRONT_TL = 512
SSM_LC = 256
SSM_BLK = 4
COEF_LB_RE, COEF_LB_IM, COEF_LBLK_RE, COEF_LBLK_IM = 0, 1, 2, 3
TOK_TM = 512
FINAL_TM = 1024
EXP_UNIT = 128
EXP_CLASSES = 9
DISPATCH_CHUNK = 80
COMBINE_CHUNK = 24

R_E1, R_E2, R_W1, R_W2, R_RANK1, R_RANK2, R_CODE1, R_CODE2 = 0, 1, 2, 3, 4, 5, 6, 7
CODE_BITS = 16
CODE_SHIFT = float(1 << CODE_BITS)

_INV_SQRT2 = 1.0 / math.sqrt(2.0)
_BF16 = jnp.bfloat16
_F32 = jnp.float32
_U32 = jnp.uint32


def _gelu(x):
    return 0.5 * x * (1.0 + lax.erf(x * _INV_SQRT2))


def _rms(x, g):
    return x * lax.rsqrt(jnp.mean(x * x, axis=-1, keepdims=True) + EPS) * g


def _dot(a, b):
    return jnp.dot(a, b, preferred_element_type=_F32)


def _dot_f32(a, b):
    return jnp.dot(a, b, preferred_element_type=_F32, precision=lax.Precision.HIGHEST)


def _dot_hi(a, b, transpose_b=False):
    def split(x):
        hi = x.astype(_BF16)
        return hi, (x - hi.astype(_F32)).astype(_BF16)

    dims = (((1,), (1 if transpose_b else 0,)), ((), ()))
    dot = lambda u, v: lax.dot_general(u, v, dims, preferred_element_type=_F32)
    a_hi, a_lo = split(a)
    b_hi, b_lo = split(b)
    return dot(a_hi, b_hi) + dot(a_hi, b_lo) + dot(a_lo, b_hi)


def _pack_bf16_pair(x):
    w = x.shape[1] // 2
    hi = lax.bitcast_convert_type(x[:, :w].astype(_BF16).astype(_F32), _U32)
    lo = lax.bitcast_convert_type(x[:, w:].astype(_BF16).astype(_F32), _U32)
    return hi | (lo >> 16)


def _unpack_bf16_pair(p):
    hi = lax.bitcast_convert_type(p & jnp.uint32(0xFFFF0000), _F32)
    lo = lax.bitcast_convert_type(p << 16, _F32)
    return jnp.concatenate([hi, lo], axis=-1)


def _head_norm_gelu(vb, gn):
    v = _gelu(vb)
    parts = []
    for h in range(N_GMLP_HEADS):
        vh = v[:, h * GMLP_HEAD:(h + 1) * GMLP_HEAD]
        parts.append(vh * lax.rsqrt(jnp.mean(vh * vh, axis=-1, keepdims=True) + EPS))
    return jnp.concatenate(parts, axis=-1) * gn


def _front_prompt_kernel(x_ref, g1_ref, win_ref, gn_ref, ws_ref, bs_ref, gog_ref,
                         xa_ref, sg_ref, mixb_ref, win_bf, z_ref):
    @pl.when(pl.program_id(0) == 0)
    def _():
        win_bf[...] = win_ref[...].astype(_BF16)
        z_ref[...] = jnp.zeros_like(z_ref)

    z = z_ref[...]
    x = x_ref[0]
    hn = _rms(x, g1_ref[...]).astype(_BF16)
    z_ref[...] = _dot(hn, win_bf[...])
    xa_ref[0] = z[:, :D_SSM]
    sg_ref[0] = jax.nn.sigmoid(z[:, D_SSM:2 * D_SSM])
    ub = _gelu(z[:, 2 * D_SSM:2 * D_SSM + D_GMLP])
    vbn = _head_norm_gelu(z[:, 2 * D_SSM + D_GMLP:], gn_ref[...]).astype(_BF16)
    tl = x.shape[0]
    rows = []
    for c in range(tl // CHUNK):
        heads = []
        for h in range(N_GMLP_HEADS):
            vh = vbn[c * CHUNK:(c + 1) * CHUNK, h * GMLP_HEAD:(h + 1) * GMLP_HEAD]
            heads.append(_dot(ws_ref[h], vh) + bs_ref[:, h:h + 1])
        rows.append(jnp.concatenate(heads, axis=-1))
    s = jnp.concatenate(rows, axis=0)
    mixb_ref[0] = _rms(ub * s, gog_ref[...]).astype(_BF16)


def _front_prompt(x, g1, win, gn, ws_tril_bf, bs_t, gog):
    n, l, d = x.shape
    tl = FRONT_TL
    per_seq = l // tl
    n_tiles = n * per_seq
    cur = lambda i: jnp.minimum(i, n_tiles - 1)
    prev = lambda i: jnp.maximum(i - 1, 0)
    const = lambda *shape: pl.BlockSpec(shape, lambda i: (0,) * len(shape))
    seq = lambda w, which: pl.BlockSpec((1, tl, w), lambda i: (which(i) // per_seq, which(i) % per_seq, 0))
    return pl.pallas_call(
        _front_prompt_kernel,
        grid=(n_tiles + 1,),
        in_specs=[seq(d, cur), const(1, d), const(d, D_IN), const(1, D_GMLP),
                  const(N_GMLP_HEADS, CHUNK, CHUNK), const(CHUNK, N_GMLP_HEADS), const(1, D_GMLP)],
        out_specs=[seq(D_SSM, prev), seq(D_SSM, prev), seq(D_GMLP, prev)],
        out_shape=[jax.ShapeDtypeStruct((n, l, D_SSM), _F32),
                   jax.ShapeDtypeStruct((n, l, D_SSM), _F32),
                   jax.ShapeDtypeStruct((n, l, D_GMLP), _BF16)],
        scratch_shapes=[pltpu.VMEM((d, D_IN), _BF16), pltpu.VMEM((tl, D_IN), _F32)],
        compiler_params=pltpu.CompilerParams(
            dimension_semantics=("arbitrary",), vmem_limit_bytes=VMEM_LIMIT),
        name="front_prompt",
    )(x, g1, win, gn, ws_tril_bf, bs_t, gog)


def _ssm_prompt_kernel(xa_ref, sg_ref, v_ref, r_ref, coef_ref, dsk_ref, gos_ref,
                       mixa_ref, hfin_ref, s_ref, st_ref):
    lc = xa_ref.shape[1]
    nblk = lc // SSM_BLK
    rows = nblk * SUBLANES

    @pl.when(pl.program_id(0) == 0)
    def _():
        st_ref[...] = jnp.zeros_like(st_ref)

    def by_position(ref):
        t = pltpu.einshape("btc->tbc", ref[...]).reshape(nblk, SSM_BLK, SUBLANES, D_SSM)
        return [t[:, i].reshape(rows, D_SSM) for i in range(SSM_BLK)]

    xs = by_position(xa_ref)
    xs_bf = [x.astype(_BF16) for x in xs]
    xk = [jnp.concatenate([x[:, k * LANES:(k + 1) * LANES] for x in xs_bf], axis=-1)
          for k in range(N_LANE_TILES)]
    for k in range(N_LANE_TILES):
        s_ref[:, 2 * TILE_STATE * k:2 * TILE_STATE * (k + 1)] = _dot(xk[k], v_ref[k])

    for kk in range(0, N_LANE_TILES, 2):
        tiles = (kk, kk + 1)
        cols = [(2 * TILE_STATE * k, 2 * TILE_STATE * k + TILE_STATE) for k in tiles]
        lbs = [tuple(jnp.broadcast_to(coef_ref[row:row + 1, k * TILE_STATE:(k + 1) * TILE_STATE],
                                      (SUBLANES, TILE_STATE)) for row in (COEF_LBLK_RE, COEF_LBLK_IM))
               for k in tiles]

        def body(j, carry, cols=cols, lbs=lbs):
            r0 = pl.multiple_of(j * SUBLANES, SUBLANES)
            out = []
            for q, ((c_re, c_im), (lr, li)) in enumerate(zip(cols, lbs)):
                hr, hi = carry[2 * q], carry[2 * q + 1]
                sr = s_ref[pl.ds(r0, SUBLANES), c_re:c_re + TILE_STATE]
                si = s_ref[pl.ds(r0, SUBLANES), c_im:c_im + TILE_STATE]
                s_ref[pl.ds(r0, SUBLANES), c_re:c_re + TILE_STATE] = hr
                s_ref[pl.ds(r0, SUBLANES), c_im:c_im + TILE_STATE] = hi
                out += [lr * hr - li * hi + sr, lr * hi + li * hr + si]
            return tuple(out)

        init = tuple(st_ref[:, c:c + TILE_STATE] for c_pair in cols for c in c_pair)
        fin = lax.fori_loop(0, nblk, body, init, unroll=2)
        for q, (c_re, c_im) in enumerate(cols):
            st_ref[:, c_re:c_re + TILE_STATE] = fin[2 * q]
            st_ref[:, c_im:c_im + TILE_STATE] = fin[2 * q + 1]

    yk = []
    for k in range(N_LANE_TILES):
        h_in = s_ref[:, 2 * TILE_STATE * k:2 * TILE_STATE * (k + 1)].astype(_BF16)
        yk.append(_dot(jnp.concatenate([h_in, xk[k]], axis=-1), r_ref[k]))
    sgs = by_position(sg_ref)
    outs = []
    for i in range(SSM_BLK):
        y = jnp.concatenate([y_k[:, i * LANES:(i + 1) * LANES] for y_k in yk], axis=-1) + dsk_ref[...] * xs[i]
        outs.append(_rms(_gelu(y) * sgs[i], gos_ref[...]).reshape(nblk, SUBLANES, D_SSM))
    mixa = jnp.stack(outs, axis=1).reshape(lc, SUBLANES, D_SSM)
    mixa_ref[...] = pltpu.einshape("tbc->btc", mixa).astype(_BF16)
    hfin_ref[...] = st_ref[...]


def _ssm_prompt(xa, sg, v, r, coef, dsk, gos):
    n, l, _ = xa.shape
    lc = SSM_LC
    const = lambda *shape: pl.BlockSpec(shape, lambda i: (0,) * len(shape))
    seq_spec = pl.BlockSpec((n, lc, D_SSM), lambda i: (0, i, 0))
    return pl.pallas_call(
        _ssm_prompt_kernel,
        grid=(l // lc,),
        in_specs=[seq_spec, seq_spec, const(*v.shape), const(*r.shape),
                  const(*coef.shape), const(1, D_SSM), const(1, D_SSM)],
        out_specs=[seq_spec, const(n, 2 * STATE_COLS)],
        out_shape=[jax.ShapeDtypeStruct((n, l, D_SSM), _BF16),
                   jax.ShapeDtypeStruct((n, 2 * STATE_COLS), _F32)],
        scratch_shapes=[pltpu.VMEM((lc // SSM_BLK * n, 2 * STATE_COLS), _F32),
                        pltpu.VMEM((n, 2 * STATE_COLS), _F32)],
        compiler_params=pltpu.CompilerParams(
            dimension_semantics=("arbitrary",), vmem_limit_bytes=VMEM_LIMIT),
        name="ssm_prompt",
    )(xa, sg, v, r, coef, dsk, gos)


def _front_sample_kernel(x_ref, g1_ref, win_ref, gn_ref, w00_ref, b0_ref, gog_ref,
                         wb_ref, wc_ref, coef_ref, dsk_ref, gos_ref, h0r_ref, h0i_ref,
                         mix_ref, hr_ref, hi_ref, vrow_ref):
    x = x_ref[...]
    hn = _rms(x, g1_ref[...])
    z = _dot_hi(hn, win_ref[...])
    xa = z[:, :D_SSM]
    ys = []
    for k in range(N_LANE_TILES):
        bu = _dot_hi(xa[:, k * LANES:(k + 1) * LANES], wb_ref[k])
        sl = slice(k * TILE_STATE, (k + 1) * TILE_STATE)
        lr, li = coef_ref[COEF_LB_RE:COEF_LB_RE + 1, sl], coef_ref[COEF_LB_IM:COEF_LB_IM + 1, sl]
        h0r, h0i = h0r_ref[:, sl], h0i_ref[:, sl]
        nr = lr * h0r - li * h0i + bu[:, :TILE_STATE]
        ni = lr * h0i + li * h0r + bu[:, TILE_STATE:]
        hr_ref[:, sl] = nr
        hi_ref[:, sl] = ni
        ys.append(_dot_hi(jnp.concatenate([nr, ni], axis=-1), wc_ref[k]))
    y = jnp.concatenate(ys, axis=-1) + dsk_ref[...] * xa
    ya = _gelu(y) * jax.nn.sigmoid(z[:, D_SSM:2 * D_SSM])
    mix_ref[:, :D_SSM] = _rms(ya, gos_ref[...])
    ub = _gelu(z[:, 2 * D_SSM:2 * D_SSM + D_GMLP])
    vbn = _head_norm_gelu(z[:, 2 * D_SSM + D_GMLP:], gn_ref[...])
    vrow_ref[...] = vbn
    s = w00_ref[...] * vbn + b0_ref[...]
    mix_ref[:, D_SSM:] = _rms(ub * s, gog_ref[...])


def _front_sample(x, g1, win, gn, w00, b0, gog, wb, wc, coef, dsk, gos, h0r, h0i):
    n = x.shape[0]
    vmem = pl.BlockSpec(memory_space=pltpu.VMEM)
    return pl.pallas_call(
        _front_sample_kernel,
        in_specs=[vmem] * 14,
        out_specs=[vmem] * 4,
        out_shape=[jax.ShapeDtypeStruct((n, D_MODEL), _F32),
                   jax.ShapeDtypeStruct((n, STATE_COLS), _F32),
                   jax.ShapeDtypeStruct((n, STATE_COLS), _F32),
                   jax.ShapeDtypeStruct((n, D_GMLP), _F32)],
        compiler_params=pltpu.CompilerParams(vmem_limit_bytes=VMEM_LIMIT),
        name="front_sample",
    )(x, g1, win, gn, w00, b0, gog, wb, wc, coef, dsk, gos, h0r, h0i)


def _route(logits, base):
    tm = logits.shape[0]
    lt = logits.T
    ex = lt[:N_EXPERTS, :]
    gr = lt[N_EXPERTS:N_EXPERTS + SUBLANES, :]
    row_e = lax.broadcasted_iota(jnp.int32, ex.shape, 0).astype(_F32)
    row_g = lax.broadcasted_iota(jnp.int32, gr.shape, 0).astype(_F32)
    neg = jnp.float32(-jnp.inf)
    big = jnp.float32(LANES)
    is_g = row_g < N_EXPERT_GROUPS
    gl = jnp.where(is_g, gr, neg)
    gmax = jnp.max(gl, axis=0, keepdims=True)
    gi = jnp.min(jnp.where(gl == gmax, row_g, big), axis=0, keepdims=True)
    p_top = 1.0 / jnp.sum(jnp.where(is_g, jnp.exp(gl - gmax), 0.0), axis=0, keepdims=True)
    lo = gi * EXPERTS_PER_GROUP
    in_grp = (row_e >= lo) & (row_e < lo + EXPERTS_PER_GROUP)
    m1 = jnp.max(jnp.where(in_grp, ex, neg), axis=0, keepdims=True)
    i1 = jnp.min(jnp.where(in_grp & (ex == m1), row_e, big), axis=0, keepdims=True)
    rest = in_grp & (row_e != i1)
    m2 = jnp.max(jnp.where(rest, ex, neg), axis=0, keepdims=True)
    i2 = jnp.min(jnp.where(rest & (ex == m2), row_e, big), axis=0, keepdims=True)
    e2 = jnp.exp(m2 - m1)
    w1 = p_top / (1.0 + e2)
    w2 = p_top * e2 / (1.0 + e2)
    sel1 = row_e == i1
    sel2 = row_e == i2
    hits = jnp.where(sel1 | sel2, 1.0, 0.0)
    src = lax.broadcasted_iota(jnp.int32, (tm, tm), 0)
    dst = lax.broadcasted_iota(jnp.int32, (tm, tm), 1)
    before = _dot(hits.astype(_BF16), jnp.where(src < dst, 1.0, 0.0).astype(_BF16)) + base
    rank1 = jnp.sum(jnp.where(sel1, before, 0.0), axis=0, keepdims=True)
    rank2 = jnp.sum(jnp.where(sel2, before, 0.0), axis=0, keepdims=True)
    fields = {R_E1: i1, R_E2: i2, R_W1: w1, R_W2: w2, R_RANK1: rank1, R_RANK2: rank2,
              R_CODE1: i1 * CODE_SHIFT + rank1, R_CODE2: i2 * CODE_SHIFT + rank2}
    row8 = lax.broadcasted_iota(jnp.int32, (SUBLANES, tm), 0)
    route_t = jnp.zeros((SUBLANES, tm), _F32)
    for r, val in fields.items():
        route_t = jnp.where(row8 == r, val, route_t)
    route = jnp.concatenate([route_t, jnp.zeros((LANES - SUBLANES, tm), _F32)], axis=0).T
    return route_t, route, base + jnp.sum(hits, axis=1, keepdims=True)


def _mixer_out_prompt_kernel(x_ref, mixa_ref, mixb_ref, wo_ref, g2_ref, wr_ref, br_ref,
                             x1_ref, xn_ref, route_ref, route_t_ref, cnt_ref, base_ref, logits_ref, wo_bf):
    i = pl.program_id(0)

    @pl.when(i == 0)
    def _():
        base_ref[...] = jnp.zeros_like(base_ref)
        logits_ref[...] = jnp.zeros_like(logits_ref)
        wo_bf[...] = wo_ref[...].astype(_BF16)

    prev_logits = logits_ref[...]
    x1 = x_ref[0] + _dot(mixa_ref[0], wo_bf[:D_SSM, :]) + _dot(mixb_ref[0], wo_bf[D_SSM:, :])
    xn = _rms(x1, g2_ref[...])
    x1_ref[...] = _pack_bf16_pair(x1)
    xn_ref[...] = _pack_bf16_pair(xn)
    logits_ref[...] = _dot(xn.astype(_BF16), wr_ref[...]) + br_ref[...]
    route_t, route, base = _route(prev_logits, base_ref[...])
    route_ref[...] = route
    route_t_ref[...] = route_t
    base = jnp.where(i >= 1, base, base_ref[...])
    base_ref[...] = base
    cnt_ref[...] = base


def _mixer_out_sample_kernel(x_ref, mix_ref, wo_ref, g2_ref, wr_ref, br_ref, cnt_in_ref,
                             x1_in, xn_in, route_in, route_t_in,
                             x1_ref, xn_ref, route_ref, route_t_ref, cnt_ref):
    del x1_in, xn_in, route_in, route_t_in
    x1 = (x_ref[...] + _dot_hi(mix_ref[:, :D_SSM], wo_ref[:D_SSM, :])
          + _dot_hi(mix_ref[:, D_SSM:], wo_ref[D_SSM:, :]))
    xn = _rms(x1, g2_ref[...])
    logits = _dot_hi(xn, wr_ref[...]) + br_ref[...]
    route_t, route, base = _route(logits, cnt_in_ref[...])
    x1_ref[...] = _pack_bf16_pair(x1)
    xn_ref[...] = _pack_bf16_pair(xn)
    route_ref[...] = route
    route_t_ref[...] = route_t
    cnt_ref[...] = base


def _mixer_out(x_p, mixa, mixb, x_s, mix_s, wo, g2, wr, br):
    n, l, d = x_p.shape
    ns = x_s.shape[0]
    t_all = n * l + ns
    tm = TOK_TM
    per_seq = l // tm
    n_tiles = n * per_seq
    cur = lambda i: jnp.minimum(i, n_tiles - 1)
    prev = lambda i: jnp.maximum(i - 1, 0)
    const = lambda *shape: pl.BlockSpec(shape, lambda i: (0,) * len(shape))
    seq = lambda w: pl.BlockSpec((1, tm, w), lambda i: (cur(i) // per_seq, cur(i) % per_seq, 0))
    tok = lambda w, which: pl.BlockSpec((tm, w), lambda i: (which(i), 0))
    tok_shapes = [jax.ShapeDtypeStruct((t_all, d // 2), _U32),
                  jax.ShapeDtypeStruct((t_all, d // 2), _U32),
                  jax.ShapeDtypeStruct((t_all, LANES), _F32),
                  jax.ShapeDtypeStruct((SUBLANES, t_all), _F32)]
    cnt_shape = jax.ShapeDtypeStruct((N_EXPERTS, 1), _F32)
    x1, xn, route, route_t, cnt = pl.pallas_call(
        _mixer_out_prompt_kernel,
        grid=(n_tiles + 1,),
        in_specs=[seq(d), seq(D_SSM), seq(D_GMLP),
                  const(d, d), const(1, d), const(d, LANES), const(1, LANES)],
        out_specs=[tok(d // 2, cur), tok(d // 2, cur), tok(LANES, prev),
                   pl.BlockSpec((SUBLANES, tm), lambda i: (0, prev(i))), const(N_EXPERTS, 1)],
        out_shape=tok_shapes + [cnt_shape],
        scratch_shapes=[pltpu.VMEM((N_EXPERTS, 1), _F32), pltpu.VMEM((tm, LANES), _F32),
                        pltpu.VMEM((d, d), _BF16)],
        compiler_params=pltpu.CompilerParams(
            dimension_semantics=("arbitrary",), vmem_limit_bytes=VMEM_LIMIT),
        name="mixer_out_prompt",
    )(x_p, mixa, mixb, wo, g2, wr.astype(_BF16), br)
    tail = (n * l) // ns
    c1 = lambda *shape: pl.BlockSpec(shape, lambda i: (0,) * len(shape))
    anyspec = pl.BlockSpec(memory_space=pl.ANY)
    tail_spec = lambda w: pl.BlockSpec((ns, w), lambda i: (tail, 0))
    return pl.pallas_call(
        _mixer_out_sample_kernel,
        grid=(1,),
        in_specs=[c1(ns, d), c1(ns, d), c1(d, d), c1(1, d), c1(d, LANES), c1(1, LANES), c1(N_EXPERTS, 1),
                  anyspec, anyspec, anyspec, anyspec],
        out_specs=[tail_spec(d // 2), tail_spec(d // 2), tail_spec(LANES),
                   pl.BlockSpec((SUBLANES, ns), lambda i: (0, tail)), c1(N_EXPERTS, 1)],
        out_shape=tok_shapes + [cnt_shape],
        input_output_aliases={7: 0, 8: 1, 9: 2, 10: 3},
        compiler_params=pltpu.CompilerParams(
            dimension_semantics=("arbitrary",), vmem_limit_bytes=VMEM_LIMIT),
        name="mixer_out_sample",
    )(x_s, mix_s, wo, g2, wr, br, cnt, x1, xn, route, route_t)


def _sc_stream(n_chunks, gather, write):
    gather(0).start()
    for j in range(n_chunks):
        if j + 1 < n_chunks:
            if j >= 1:
                write(j - 1).wait()
            gather(j + 1).start()
        gather(j).wait()
        write(j).start()
    if n_chunks >= 2:
        write(n_chunks - 2).wait()
    write(n_chunks - 1).wait()


def _sc_mesh():
    return plsc.VectorSubcoreMesh(core_axis_name="c", subcore_axis_name="s",
                                  num_cores=SC_CORES, num_subcores=SC_SUBCORES)


def _sc_buffers(chunk, w, dtype):
    return [pltpu.VMEM((chunk, w), dtype), pltpu.VMEM((chunk, w), dtype)] + [pltpu.SemaphoreType.DMA] * 4


def _sc_combine(table, idx, n_out, chunk):
    w = table.shape[1]
    rows_w = n_out // SC_WORKERS
    n_chunks = rows_w // chunk
    assert rows_w * SC_WORKERS == n_out and n_chunks * chunk == rows_w and rows_w % SUBLANES == 0

    def body(table_hbm, idx_hbm, out_hbm, idx_v, buf0, buf1, g0, g1, w0, w1):
        wid = lax.axis_index("s") * SC_CORES + lax.axis_index("c")
        base = pl.multiple_of(wid * rows_w, SUBLANES)
        pltpu.sync_copy(idx_hbm.at[pl.ds(base, rows_w)], idx_v)
        bufs, gsems, wsems = (buf0, buf1), (g0, g1), (w0, w1)

        def gather(j):
            return pltpu.make_async_copy(table_hbm.at[idx_v.at[pl.ds(j * chunk, chunk)]], bufs[j % 2], gsems[j % 2])

        def write(j):
            return pltpu.make_async_copy(bufs[j % 2], out_hbm.at[pl.ds(base + j * chunk, chunk)], wsems[j % 2])

        _sc_stream(n_chunks, gather, write)

    return pl.kernel(
        body,
        out_type=jax.ShapeDtypeStruct((n_out, w), table.dtype),
        mesh=_sc_mesh(),
        scratch_types=[pltpu.VMEM((rows_w,), jnp.int32)] + _sc_buffers(chunk, w, table.dtype),
        compiler_params=pltpu.CompilerParams(use_tc_tiling_on_sc=True),
        name="sc_combine",
    )(table, idx)


def _sc_dispatch(table, codes, start_row, n_out, chunk):
    t_all, w = table.shape
    n_pad = codes.shape[0]
    n_ent = 2 * t_all
    ent_w = n_pad // SC_WORKERS
    n_chunks = ent_w // chunk
    per_chunk = chunk // SC_LANES
    trash = n_out - (n_pad - n_ent)
    assert ent_w * SC_WORKERS == n_pad and n_chunks * chunk == ent_w
    assert per_chunk * SC_LANES == chunk and chunk <= LANES and n_pad - n_ent <= t_all

    def body(table_hbm, code_hbm, start_hbm, out_hbm, dest_hbm,
             code_v, dest_v, tok_v, dst_v, start_v, buf0, buf1, g0, g1, w0, w1):
        wid = lax.axis_index("s") * SC_CORES + lax.axis_index("c")
        ebase = pl.multiple_of(wid * ent_w, SUBLANES)
        pltpu.sync_copy(code_hbm.at[pl.ds(ebase, ent_w)], code_v)
        pltpu.sync_copy(start_hbm, start_v)
        lane = lax.iota(jnp.int32, SC_LANES)
        for j in range(n_chunks):
            for c in range(per_chunk):
                off = j * chunk + c * SC_LANES
                ent = ebase + off + lane
                code = code_v[pl.ds(off, SC_LANES)]
                d = plsc.load_gather(start_v, [code >> CODE_BITS]) + (code & ((1 << CODE_BITS) - 1))
                d = jnp.where(ent >= n_ent, trash + (ent - n_ent), d)
                tok = jnp.where(ent >= t_all, ent - t_all, ent)
                tok = jnp.where(tok >= t_all, tok - t_all, tok)
                dest_v[pl.ds(off, SC_LANES)] = d
                dst_v[j, pl.ds(c * SC_LANES, SC_LANES)] = d
                tok_v[j, pl.ds(c * SC_LANES, SC_LANES)] = tok
        pltpu.sync_copy(dest_v, dest_hbm.at[pl.ds(ebase, ent_w)])
        bufs, gsems, wsems = (buf0, buf1), (g0, g1), (w0, w1)

        def gather(j):
            return pltpu.make_async_copy(table_hbm.at[tok_v.at[j]], bufs[j % 2], gsems[j % 2])

        def scatter(j):
            return pltpu.make_async_copy(bufs[j % 2], out_hbm.at[dst_v.at[j]], wsems[j % 2])

        _sc_stream(n_chunks, gather, scatter)

    return pl.kernel(
        body,
        out_type=(jax.ShapeDtypeStruct((n_out, w), table.dtype), jax.ShapeDtypeStruct((n_pad,), jnp.int32)),
        mesh=_sc_mesh(),
        scratch_types=([pltpu.VMEM((ent_w,), jnp.int32), pltpu.VMEM((ent_w,), jnp.int32),
                        pltpu.VMEM((n_chunks, chunk), jnp.int32), pltpu.VMEM((n_chunks, chunk), jnp.int32),
                        pltpu.VMEM((LANES,), jnp.int32)] + _sc_buffers(chunk, w, table.dtype)),
        compiler_params=pltpu.CompilerParams(use_tc_tiling_on_sc=True, needs_layout_passes=False),
        name="sc_dispatch",
    )(table, codes, start_row)


def _experts_kernel(piece_start_ref, piece_row_ref, piece_cls_ref, wg_ref, wu_ref, wd_ref, xs_hbm, ys_hbm,
                    wg_bf, wu_bf, wd_bf, xbuf, ybuf, xsem, ysem):
    e = pl.program_id(0)
    g0 = piece_start_ref[e]
    n_here = piece_start_ref[e + 1] - g0
    n_total = piece_start_ref[N_EXPERTS]

    def per_class(g, fn):
        cls = piece_cls_ref[g]
        row = pl.multiple_of(piece_row_ref[g], EXP_UNIT)
        for c in range(1, EXP_CLASSES + 1):
            pl.when(cls == c)(lambda c=c: fn(c * EXP_UNIT, row))

    def x_copy(slot, rows, row):
        return pltpu.make_async_copy(xs_hbm.at[pl.ds(row, rows)], xbuf.at[slot, pl.ds(0, rows)], xsem.at[slot])

    def y_copy(slot, rows, row):
        return pltpu.make_async_copy(ybuf.at[slot, pl.ds(0, rows)], ys_hbm.at[pl.ds(row, rows)], ysem.at[slot])

    @pl.when((e == 0) & (n_total > 0))
    def _():
        per_class(0, lambda rows, row: x_copy(0, rows, row).start())

    wg_bf[...] = wg_ref[0].astype(_BF16)
    wu_bf[...] = wu_ref[0].astype(_BF16)
    wd_bf[...] = wd_ref[0].astype(_BF16)

    def piece(j, carry):
        g = g0 + j
        slot = lax.rem(g, 2)
        per_class(g, lambda rows, row: x_copy(slot, rows, row).wait())

        @pl.when(g + 1 < n_total)
        def _():
            per_class(g + 1, lambda rows, row: x_copy(1 - slot, rows, row).start())

        @pl.when(g >= 2)
        def _():
            per_class(g - 2, lambda rows, row: y_copy(slot, rows, row).wait())

        def compute(rows, row):
            x = _unpack_bf16_pair(xbuf[slot, pl.ds(0, rows)]).astype(_BF16)
            a = _dot(x, wg_bf[...])
            u = _dot(x, wu_bf[...])
            h = (a * jax.nn.sigmoid(a) * u).astype(_BF16)
            ybuf[slot, pl.ds(0, rows)] = _pack_bf16_pair(_dot(h, wd_bf[...]))
            y_copy(slot, rows, row).start()

        per_class(g, compute)
        return carry

    lax.fori_loop(0, n_here, piece, 0)

    @pl.when(e == N_EXPERTS - 1)
    def _():
        @pl.when(n_total >= 2)
        def _():
            per_class(n_total - 2, lambda rows, row: y_copy(lax.rem(n_total, 2), rows, row).wait())

        @pl.when(n_total >= 1)
        def _():
            per_class(n_total - 1, lambda rows, row: y_copy(lax.rem(n_total - 1, 2), rows, row).wait())


def _experts(piece_start, piece_row, piece_cls, n_rows, xs, w_gate, w_up, w_down):
    dh = xs.shape[1]
    d = 2 * dh
    tm = EXP_UNIT * EXP_CLASSES
    anyspec = pl.BlockSpec(memory_space=pl.ANY)
    wsel = lambda e, ps, pr, pc: (e, 0, 0)
    grid_spec = pltpu.PrefetchScalarGridSpec(
        num_scalar_prefetch=3,
        grid=(N_EXPERTS,),
        in_specs=[pl.BlockSpec((1, d, D_EXPERT), wsel), pl.BlockSpec((1, d, D_EXPERT), wsel),
                  pl.BlockSpec((1, D_EXPERT, d), wsel), anyspec],
        out_specs=anyspec,
        scratch_shapes=[pltpu.VMEM((d, D_EXPERT), _BF16), pltpu.VMEM((d, D_EXPERT), _BF16),
                        pltpu.VMEM((D_EXPERT, d), _BF16),
                        pltpu.VMEM((2, tm, dh), _U32), pltpu.VMEM((2, tm, dh), _U32),
                        pltpu.SemaphoreType.DMA((2,)), pltpu.SemaphoreType.DMA((2,))],
    )
    return pl.pallas_call(
        _experts_kernel,
        grid_spec=grid_spec,
        out_shape=jax.ShapeDtypeStruct((n_rows, dh), _U32),
        compiler_params=pltpu.CompilerParams(
            dimension_semantics=("arbitrary",), vmem_limit_bytes=VMEM_LIMIT),
        name="experts",
    )(piece_start, piece_row, piece_cls, w_gate, w_up, w_down, xs)


def _final_kernel(x1_ref, ya_ref, yb_ref, route_ref, gf_ref, y_ref):
    route = route_ref[...]
    x2 = (_unpack_bf16_pair(x1_ref[...]) + route[:, R_W1:R_W1 + 1] * _unpack_bf16_pair(ya_ref[...])
          + route[:, R_W2:R_W2 + 1] * _unpack_bf16_pair(yb_ref[...]))
    y_ref[...] = _rms(x2, gf_ref[...])


def _final(x1, yab, route, gf, n_prompt, n_sample):
    d = 2 * x1.shape[1]

    def call(tm, first_block, n_rows, name):
        tok = lambda w: pl.BlockSpec((tm, w), lambda i: (first_block + i, 0))
        sel = lambda k: pl.BlockSpec((None, tm, d // 2), lambda i: (k, first_block + i, 0))
        return pl.pallas_call(
            _final_kernel,
            grid=(n_rows // tm,),
            in_specs=[tok(d // 2), sel(0), sel(1), tok(LANES), pl.BlockSpec((1, d), lambda i: (0, 0))],
            out_specs=pl.BlockSpec((tm, d), lambda i: (i, 0)),
            out_shape=jax.ShapeDtypeStruct((n_rows, d), _F32),
            compiler_params=pltpu.CompilerParams(
                dimension_semantics=("arbitrary",), vmem_limit_bytes=VMEM_LIMIT),
            name=name,
        )(x1, yab, yab, route, gf)

    return (call(FINAL_TM, 0, n_prompt, "final_prompt"),
            call(n_sample, n_prompt // n_sample, n_sample, "final_sample"))


def _powers(lam_re, lam_im, dt):
    out = []
    for m in range(SSM_BLK + 1):
        mag = jnp.exp(m * lam_re * dt)
        ang = m * lam_im * dt
        out.append((mag * jnp.cos(ang), mag * jnp.sin(ang)))
    return out


def _spread(x, copies):
    w = x.shape[1]
    src = lax.broadcasted_iota(jnp.int32, (w, w * copies), 0)
    dst = lax.broadcasted_iota(jnp.int32, (w, w * copies), 1)
    return _dot_f32(x, jnp.where(dst % w == src, 1.0, 0.0))


def _ssm_prep_kernel(lam_ref, b_re, b_im, c_re, c_im, v_ref, r_ref, wb_ref, wc_ref, coef_ref):
    n_p, n_h = SSM_STATE, SSM_GROUP
    lr, li, dt = lam_ref[0:1, :], lam_ref[1:2, :], lam_ref[2:3, :]
    pw = _powers(lr, li, dt)
    den = lr * lr + li * li
    nr, ni = pw[1][0] - 1.0, pw[1][1]
    k_re = (nr * lr + ni * li) / den
    k_im = (ni * lr - nr * li) / den
    coef_ref[...] = jnp.concatenate(
        [pw[1][0], pw[1][1], pw[SSM_BLK][0], pw[SSM_BLK][1], jnp.zeros((SUBLANES - 4, TILE_STATE), _F32)], axis=0)

    on_diag_b = (lax.broadcasted_iota(jnp.int32, (TILE_STATE, LANES), 0) // n_p
                 == lax.broadcasted_iota(jnp.int32, (TILE_STATE, LANES), 1) // n_h)
    rows_gp = lambda ref: ref[...].reshape(TILE_STATE, n_h)
    bt_re = jnp.where(on_diag_b, _spread(rows_gp(b_re), SUBLANES), 0.0).T
    bt_im = jnp.where(on_diag_b, _spread(rows_gp(b_im), SUBLANES), 0.0).T
    bb_re = k_re * bt_re - k_im * bt_im
    bb_im = k_re * bt_im + k_im * bt_re
    wb_ref[0] = jnp.concatenate([bb_re, bb_im], axis=1)
    v_rows = []
    for s in range(SSM_BLK):
        pr, pi = pw[SSM_BLK - 1 - s]
        v_rows.append(jnp.concatenate([pr * bb_re - pi * bb_im, pr * bb_im + pi * bb_re], axis=1))
    v_ref[0] = jnp.concatenate(v_rows, axis=0).astype(v_ref.dtype)

    on_diag_c = (lax.broadcasted_iota(jnp.int32, (LANES, TILE_STATE), 0) // n_h
                 == lax.broadcasted_iota(jnp.int32, (LANES, TILE_STATE), 1) // n_p)
    rows_gh = lambda ref: ref[...].reshape(LANES, n_p)
    ct_re = jnp.where(on_diag_c, _spread(rows_gh(c_re), SUBLANES), 0.0)
    ct_im = jnp.where(on_diag_c, _spread(rows_gh(c_im), SUBLANES), 0.0)
    cl = [(ct_re * pr - ct_im * pi, ct_re * pi + ct_im * pr) for pr, pi in pw]
    wc_ref[0] = jnp.concatenate([cl[0][0], -cl[0][1]], axis=1).T
    direct = [_dot_hi(cl[m][0], bb_re, transpose_b=True) - _dot_hi(cl[m][1], bb_im, transpose_b=True)
              for m in range(SSM_BLK)]
    zero = jnp.zeros((LANES, LANES), _F32)
    rt = jnp.concatenate(
        [jnp.concatenate([cl[i + 1][0], -cl[i + 1][1]]
                         + [direct[i - s] if s <= i else zero for s in range(SSM_BLK)], axis=1)
         for i in range(SSM_BLK)], axis=0)
    r_ref[0] = rt.T.astype(r_ref.dtype)


def _ssm_params(lam_re, lam_im, log_dt, b_re, b_im, c_re, c_im, d_skip):
    n_g, n_p, n_h = N_SSM_GROUPS, SSM_STATE, SSM_GROUP
    dt = jnp.repeat(jnp.exp(log_dt), n_p)
    lam = jnp.zeros((SUBLANES, STATE_COLS), _F32).at[0].set(lam_re.reshape(-1)).at[1].set(
        lam_im.reshape(-1)).at[2].set(dt)
    groups = lambda r, c: pl.BlockSpec((SUBLANES, r, c), lambda k: (k, 0, 0))
    out3 = lambda rows, w: pl.BlockSpec((1, rows, w), lambda k: (k, 0, 0))
    cols = pl.BlockSpec((SUBLANES, TILE_STATE), lambda k: (0, k))
    k_blk = SSM_BLK * LANES
    v, r, wb, wc, coef = pl.pallas_call(
        _ssm_prep_kernel,
        grid=(N_LANE_TILES,),
        in_specs=[cols, groups(n_p, n_h), groups(n_p, n_h), groups(n_h, n_p), groups(n_h, n_p)],
        out_specs=[out3(k_blk, 2 * TILE_STATE), out3(2 * TILE_STATE + k_blk, k_blk),
                   out3(LANES, 2 * TILE_STATE), out3(2 * TILE_STATE, LANES), cols],
        out_shape=[jax.ShapeDtypeStruct((N_LANE_TILES, k_blk, 2 * TILE_STATE), _BF16),
                   jax.ShapeDtypeStruct((N_LANE_TILES, 2 * TILE_STATE + k_blk, k_blk), _BF16),
                   jax.ShapeDtypeStruct((N_LANE_TILES, LANES, 2 * TILE_STATE), _F32),
                   jax.ShapeDtypeStruct((N_LANE_TILES, 2 * TILE_STATE, LANES), _F32),
                   jax.ShapeDtypeStruct((SUBLANES, STATE_COLS), _F32)],
        compiler_params=pltpu.CompilerParams(
            dimension_semantics=("arbitrary",), vmem_limit_bytes=VMEM_LIMIT),
        name="ssm_prep",
    )(lam, b_re, b_im, c_re, c_im)
    return wb, wc, v, r, coef, d_skip.reshape(1, D_SSM)


def _dispatch_plan(route_t, cnt):
    t_all = route_t.shape[1]
    codes = route_t[R_CODE1:R_CODE2 + 1].astype(jnp.int32).reshape(-1)
    per_pass = SC_WORKERS * DISPATCH_CHUNK
    codes = jnp.pad(codes, (0, -(2 * t_all) % per_pass))
    counts = cnt[:, 0].astype(jnp.int32)
    zero = jnp.zeros((1,), jnp.int32)
    units = (counts + EXP_UNIT - 1) // EXP_UNIT
    unit_start = jnp.concatenate([zero, jnp.cumsum(units)])
    start_row = jnp.zeros((LANES,), jnp.int32).at[:N_EXPERTS].set(unit_start[:N_EXPERTS] * EXP_UNIT)
    pieces = (units + EXP_CLASSES - 1) // EXP_CLASSES
    piece_start = jnp.concatenate([zero, jnp.cumsum(pieces)])
    tm = EXP_UNIT * EXP_CLASSES
    max_units = (2 * t_all + N_EXPERTS * (EXP_UNIT - 1)) // EXP_UNIT
    max_pieces = (max_units + N_EXPERTS * (EXP_CLASSES - 1)) // EXP_CLASSES
    g = jnp.arange(max_pieces, dtype=jnp.int32)
    owner = ((g[:, None] >= piece_start[None, :-1]) & (g[:, None] < piece_start[None, 1:])).astype(jnp.int32)
    pick = lambda table: jnp.sum(owner * table[None, :], axis=1)
    first_unit = pick(unit_start[:-1]) + (g - pick(piece_start[:-1])) * EXP_CLASSES
    piece_row = first_unit * EXP_UNIT
    piece_cls = jnp.clip(pick(unit_start[1:]) - first_unit, 1, EXP_CLASSES)
    n_rows = (max_units * EXP_UNIT + tm - 1) // tm * tm + tm
    return codes, start_row, n_rows, piece_start, piece_row, piece_cls


def kernel(x_prompt, x_sample, state_ssm_re, state_ssm_im, norm1_g, w_in, lam_re, lam_im, log_dt, ssm_b_re, ssm_b_im, ssm_c_re, ssm_c_im, ssm_d, gmlp_norm_g, gmlp_w_s, gmlp_b_s, out_norm_ssm_g, out_norm_gmlp_g, w_out, norm2_g, w_router_group, b_router_group, w_router_expert, b_router_expert, w_gate, w_up, w_down, final_norm_g):
    n, l, d = x_prompt.shape
    ns = x_sample.shape[0]
    t_all = n * l + ns
    li = 0
    g1 = norm1_g[li].reshape(1, d)
    gn = gmlp_norm_g[li].reshape(1, D_GMLP)
    tril = jnp.tril(jnp.ones((CHUNK, CHUNK), dtype=bool))
    ws_tril = jnp.where(tril[None], gmlp_w_s[li], 0.0)
    bs = gmlp_b_s[li]
    gog = out_norm_gmlp_g[li].reshape(1, D_GMLP)
    gos = out_norm_ssm_g[li].reshape(1, D_SSM)
    wb, wc, v_blk, r_blk, coef, dsk = _ssm_params(
        lam_re[li], lam_im[li], log_dt[li], ssm_b_re[li], ssm_b_im[li], ssm_c_re[li], ssm_c_im[li], ssm_d[li])
    g2 = norm2_g[li].reshape(1, d)
    pad = LANES - N_EXPERTS - N_EXPERT_GROUPS
    wr = jnp.concatenate([w_router_expert[li], w_router_group[li], jnp.zeros((d, pad), _F32)], axis=1)
    br = jnp.concatenate([b_router_expert[li], b_router_group[li], jnp.zeros((pad,), _F32)]).reshape(1, LANES)

    xa, sg, mixb = _front_prompt(x_prompt, g1, w_in[li], gn, ws_tril.astype(_BF16), bs.T, gog)
    mixa, hfin = _ssm_prompt(xa, sg, v_blk, r_blk, coef, dsk, gos)
    w00 = jnp.repeat(ws_tril[:, 0, 0], GMLP_HEAD).reshape(1, D_GMLP)
    b0 = jnp.repeat(bs[:, 0], GMLP_HEAD).reshape(1, D_GMLP)
    mix_s, hr_s, hi_s, vrow = _front_sample(
        x_sample.reshape(ns, d), g1, w_in[li], gn, w00, b0, gog, wb, wc, coef, dsk, gos,
        state_ssm_re[li].reshape(ns, STATE_COLS), state_ssm_im[li].reshape(ns, STATE_COLS))

    x1, xn, route, route_t, cnt = _mixer_out(x_prompt, mixa, mixb, x_sample.reshape(ns, d), mix_s,
                                             w_out[li], g2, wr, br)
    codes, start_row, n_rows, piece_start, piece_row, piece_cls = _dispatch_plan(route_t, cnt)
    xs, dest = _sc_dispatch(xn, codes, start_row, n_rows, DISPATCH_CHUNK)
    ys = _experts(piece_start, piece_row, piece_cls, n_rows, xs, w_gate[li], w_up[li], w_down[li])
    yab = _sc_combine(ys, dest, 2 * t_all, COMBINE_CHUNK).reshape(2, t_all, d // 2)
    y_p, y_s = _final(x1, yab, route, final_norm_g.reshape(1, d), n * l, ns)

    hf = hfin.reshape(n, N_LANE_TILES, 2, 8, SSM_STATE)
    re_p = hf[:, :, 0].reshape(1, n, N_SSM_GROUPS, SSM_STATE)
    im_p = hf[:, :, 1].reshape(1, n, N_SSM_GROUPS, SSM_STATE)
    re_s = hr_s.reshape(1, ns, N_SSM_GROUPS, SSM_STATE)
    im_s = hi_s.reshape(1, ns, N_SSM_GROUPS, SSM_STATE)
    return (y_p.reshape(n, l, d), y_s.reshape(ns, 1, d), re_p, im_p, re_s, im_s,
            vrow.reshape(1, ns, 1, D_GMLP))
```

```python
import math

import jax
import jax.numpy as jnp
from jax import lax
from jax.experimental import pallas as pl
from jax.experimental.pallas import tpu as pltpu
from jax.experimental.pallas import tpu_sc as plsc

D_MODEL = 1024
D_SSM = 512
D_GMLP = 512
SSM_GROUP = 16
N_SSM_GROUPS = 32
SSM_STATE = 64
CHUNK = 128
N_GMLP_HEADS = 4
GMLP_HEAD = 128
N_EXPERT_GROUPS = 4
EXPERTS_PER_GROUP = 8
N_EXPERTS = 32
D_EXPERT = 512
D_IN = 2048
EPS = 1e-6

LANES = 128
SUBLANES = 8
N_LANE_TILES = D_SSM // LANES
STATE_COLS = N_SSM_GROUPS * SSM_STATE
TILE_STATE = STATE_COLS // N_LANE_TILES
VMEM_LIMIT = 56 * 1024 * 1024

SC_CORES = 2
SC_SUBCORES = 16
SC_LANES = 16
SC_WORKERS = SC_CORES * SC_SUBCORES

FRONT_TL = 512
SSM_LC = 256
SSM_BLK = 4
COEF_LB_RE, COEF_LB_IM, COEF_LBLK_RE, COEF_LBLK_IM = 0, 1, 2, 3
TOK_TM = 512
FINAL_TM = 1024
EXP_UNIT = 128
EXP_CLASSES = 8
DISPATCH_CHUNK = 80
COMBINE_CHUNK = 24

R_E1, R_E2, R_W1, R_W2, R_RANK1, R_RANK2, R_CODE1, R_CODE2 = 0, 1, 2, 3, 4, 5, 6, 7
CODE_BITS = 16
CODE_SHIFT = float(1 << CODE_BITS)

_INV_SQRT2 = 1.0 / math.sqrt(2.0)
_BF16 = jnp.bfloat16
_F32 = jnp.float32
_U32 = jnp.uint32


def _gelu(x):
    return 0.5 * x * (1.0 + lax.erf(x * _INV_SQRT2))


def _rms(x, g):
    return x * lax.rsqrt(jnp.mean(x * x, axis=-1, keepdims=True) + EPS) * g


def _dot(a, b):
    return jnp.dot(a, b, preferred_element_type=_F32)


def _dot_f32(a, b):
    return jnp.dot(a, b, preferred_element_type=_F32, precision=lax.Precision.HIGHEST)


def _dot_hi(a, b, transpose_b=False):
    def split(x):
        hi = x.astype(_BF16)
        return hi, (x - hi.astype(_F32)).astype(_BF16)

    dims = (((1,), (1 if transpose_b else 0,)), ((), ()))
    dot = lambda u, v: lax.dot_general(u, v, dims, preferred_element_type=_F32)
    a_hi, a_lo = split(a)
    b_hi, b_lo = split(b)
    return dot(a_hi, b_hi) + dot(a_hi, b_lo) + dot(a_lo, b_hi)


def _pack_bf16_pair(x):
    w = x.shape[1] // 2
    hi = lax.bitcast_convert_type(x[:, :w].astype(_BF16).astype(_F32), _U32)
    lo = lax.bitcast_convert_type(x[:, w:].astype(_BF16).astype(_F32), _U32)
    return hi | (lo >> 16)


def _unpack_bf16_pair(p):
    hi = lax.bitcast_convert_type(p & jnp.uint32(0xFFFF0000), _F32)
    lo = lax.bitcast_convert_type(p << 16, _F32)
    return jnp.concatenate([hi, lo], axis=-1)


def _head_norm_gelu(vb, gn):
    v = _gelu(vb)
    parts = []
    for h in range(N_GMLP_HEADS):
        vh = v[:, h * GMLP_HEAD:(h + 1) * GMLP_HEAD]
        parts.append(vh * lax.rsqrt(jnp.mean(vh * vh, axis=-1, keepdims=True) + EPS))
    return jnp.concatenate(parts, axis=-1) * gn


def _front_prompt_kernel(x_ref, g1_ref, win_ref, gn_ref, ws_ref, bs_ref, gog_ref,
                         xa_ref, sg_ref, mixb_ref, win_bf, z_ref):
    @pl.when(pl.program_id(0) == 0)
    def _():
        win_bf[...] = win_ref[...].astype(_BF16)
        z_ref[...] = jnp.zeros_like(z_ref)

    z = z_ref[...]
    x = x_ref[0]
    hn = _rms(x, g1_ref[...]).astype(_BF16)
    z_ref[...] = _dot(hn, win_bf[...])
    xa_ref[0] = z[:, :D_SSM]
    sg_ref[0] = jax.nn.sigmoid(z[:, D_SSM:2 * D_SSM])
    ub = _gelu(z[:, 2 * D_SSM:2 * D_SSM + D_GMLP])
    vbn = _head_norm_gelu(z[:, 2 * D_SSM + D_GMLP:], gn_ref[...]).astype(_BF16)
    tl = x.shape[0]
    rows = []
    for c in range(tl // CHUNK):
        heads = []
        for h in range(N_GMLP_HEADS):
            vh = vbn[c * CHUNK:(c + 1) * CHUNK, h * GMLP_HEAD:(h + 1) * GMLP_HEAD]
            heads.append(_dot(ws_ref[h], vh) + bs_ref[:, h:h + 1])
        rows.append(jnp.concatenate(heads, axis=-1))
    s = jnp.concatenate(rows, axis=0)
    mixb_ref[0] = _rms(ub * s, gog_ref[...]).astype(_BF16)


def _front_prompt(x, g1, win, gn, ws_tril_bf, bs_t, gog):
    n, l, d = x.shape
    tl = FRONT_TL
    per_seq = l // tl
    n_tiles = n * per_seq
    cur = lambda i: jnp.minimum(i, n_tiles - 1)
    prev = lambda i: jnp.maximum(i - 1, 0)
    const = lambda *shape: pl.BlockSpec(shape, lambda i: (0,) * len(shape))
    seq = lambda w, which: pl.BlockSpec((1, tl, w), lambda i: (which(i) // per_seq, which(i) % per_seq, 0))
    return pl.pallas_call(
        _front_prompt_kernel,
        grid=(n_tiles + 1,),
        in_specs=[seq(d, cur), const(1, d), const(d, D_IN), const(1, D_GMLP),
                  const(N_GMLP_HEADS, CHUNK, CHUNK), const(CHUNK, N_GMLP_HEADS), const(1, D_GMLP)],
        out_specs=[seq(D_SSM, prev), seq(D_SSM, prev), seq(D_GMLP, prev)],
        out_shape=[jax.ShapeDtypeStruct((n, l, D_SSM), _F32),
                   jax.ShapeDtypeStruct((n, l, D_SSM), _F32),
                   jax.ShapeDtypeStruct((n, l, D_GMLP), _BF16)],
        scratch_shapes=[pltpu.VMEM((d, D_IN), _BF16), pltpu.VMEM((tl, D_IN), _F32)],
        compiler_params=pltpu.CompilerParams(
            dimension_semantics=("arbitrary",), vmem_limit_bytes=VMEM_LIMIT),
        name="front_prompt",
    )(x, g1, win, gn, ws_tril_bf, bs_t, gog)


def _ssm_prompt_kernel(xa_ref, sg_ref, v_ref, r_ref, coef_ref, dsk_ref, gos_ref,
                       mixa_ref, hfin_ref, s_ref, st_ref):
    lc = xa_ref.shape[1]
    nblk = lc // SSM_BLK
    rows = nblk * SUBLANES

    @pl.when(pl.program_id(0) == 0)
    def _():
        st_ref[...] = jnp.zeros_like(st_ref)

    def by_position(ref):
        t = pltpu.einshape("btc->tbc", ref[...]).reshape(nblk, SSM_BLK, SUBLANES, D_SSM)
        return [t[:, i].reshape(rows, D_SSM) for i in range(SSM_BLK)]

    xs = by_position(xa_ref)
    xs_bf = [x.astype(_BF16) for x in xs]
    xk = [jnp.concatenate([x[:, k * LANES:(k + 1) * LANES] for x in xs_bf], axis=-1)
          for k in range(N_LANE_TILES)]
    for k in range(N_LANE_TILES):
        s_ref[:, 2 * TILE_STATE * k:2 * TILE_STATE * (k + 1)] = _dot(xk[k], v_ref[k])

    for kk in range(0, N_LANE_TILES, 2):
        tiles = (kk, kk + 1)
        cols = [(2 * TILE_STATE * k, 2 * TILE_STATE * k + TILE_STATE) for k in tiles]
        lbs = [tuple(jnp.broadcast_to(coef_ref[row:row + 1, k * TILE_STATE:(k + 1) * TILE_STATE],
                                      (SUBLANES, TILE_STATE)) for row in (COEF_LBLK_RE, COEF_LBLK_IM))
               for k in tiles]

        def body(j, carry, cols=cols, lbs=lbs):
            r0 = pl.multiple_of(j * SUBLANES, SUBLANES)
            out = []
            for q, ((c_re, c_im), (lr, li)) in enumerate(zip(cols, lbs)):
                hr, hi = carry[2 * q], carry[2 * q + 1]
                sr = s_ref[pl.ds(r0, SUBLANES), c_re:c_re + TILE_STATE]
                si = s_ref[pl.ds(r0, SUBLANES), c_im:c_im + TILE_STATE]
                s_ref[pl.ds(r0, SUBLANES), c_re:c_re + TILE_STATE] = hr
                s_ref[pl.ds(r0, SUBLANES), c_im:c_im + TILE_STATE] = hi
                out += [lr * hr - li * hi + sr, lr * hi + li * hr + si]
            return tuple(out)

        init = tuple(st_ref[:, c:c + TILE_STATE] for c_pair in cols for c in c_pair)
        fin = lax.fori_loop(0, nblk, body, init, unroll=2)
        for q, (c_re, c_im) in enumerate(cols):
            st_ref[:, c_re:c_re + TILE_STATE] = fin[2 * q]
            st_ref[:, c_im:c_im + TILE_STATE] = fin[2 * q + 1]

    yk = []
    for k in range(N_LANE_TILES):
        h_in = s_ref[:, 2 * TILE_STATE * k:2 * TILE_STATE * (k + 1)].astype(_BF16)
        yk.append(_dot(jnp.concatenate([h_in, xk[k]], axis=-1), r_ref[k]))
    ys = []
    for i in range(SSM_BLK):
        y = jnp.concatenate([y_k[:, i * LANES:(i + 1) * LANES] for y_k in yk], axis=-1) + dsk_ref[...] * xs[i]
        ys.append(y.reshape(nblk, SUBLANES, D_SSM))
    y = pltpu.einshape("tbc->btc", jnp.stack(ys, axis=1).reshape(lc, SUBLANES, D_SSM))
    mixa_ref[...] = _rms(_gelu(y) * sg_ref[...], gos_ref[...]).astype(_BF16)
    hfin_ref[...] = st_ref[...]


def _ssm_prompt(xa, sg, v, r, coef, dsk, gos):
    n, l, _ = xa.shape
    lc = SSM_LC
    const = lambda *shape: pl.BlockSpec(shape, lambda i: (0,) * len(shape))
    seq_spec = pl.BlockSpec((n, lc, D_SSM), lambda i: (0, i, 0))
    return pl.pallas_call(
        _ssm_prompt_kernel,
        grid=(l // lc,),
        in_specs=[seq_spec, seq_spec, const(*v.shape), const(*r.shape),
                  const(*coef.shape), const(1, D_SSM), const(1, D_SSM)],
        out_specs=[seq_spec, const(n, 2 * STATE_COLS)],
        out_shape=[jax.ShapeDtypeStruct((n, l, D_SSM), _BF16),
                   jax.ShapeDtypeStruct((n, 2 * STATE_COLS), _F32)],
        scratch_shapes=[pltpu.VMEM((lc // SSM_BLK * n, 2 * STATE_COLS), _F32),
                        pltpu.VMEM((n, 2 * STATE_COLS), _F32)],
        compiler_params=pltpu.CompilerParams(
            dimension_semantics=("arbitrary",), vmem_limit_bytes=VMEM_LIMIT),
        name="ssm_prompt",
    )(xa, sg, v, r, coef, dsk, gos)


def _front_sample_kernel(x_ref, g1_ref, win_ref, gn_ref, w00_ref, b0_ref, gog_ref,
                         wb_ref, wc_ref, coef_ref, dsk_ref, gos_ref, h0r_ref, h0i_ref,
                         mix_ref, hr_ref, hi_ref, vrow_ref):
    x = x_ref[...]
    hn = _rms(x, g1_ref[...])
    z = _dot_hi(hn, win_ref[...])
    xa = z[:, :D_SSM]
    ys = []
    for k in range(N_LANE_TILES):
        bu = _dot_hi(xa[:, k * LANES:(k + 1) * LANES], wb_ref[k])
        sl = slice(k * TILE_STATE, (k + 1) * TILE_STATE)
        lr, li = coef_ref[COEF_LB_RE:COEF_LB_RE + 1, sl], coef_ref[COEF_LB_IM:COEF_LB_IM + 1, sl]
        h0r, h0i = h0r_ref[:, sl], h0i_ref[:, sl]
        nr = lr * h0r - li * h0i + bu[:, :TILE_STATE]
        ni = lr * h0i + li * h0r + bu[:, TILE_STATE:]
        hr_ref[:, sl] = nr
        hi_ref[:, sl] = ni
        ys.append(_dot_hi(jnp.concatenate([nr, ni], axis=-1), wc_ref[k]))
    y = jnp.concatenate(ys, axis=-1) + dsk_ref[...] * xa
    ya = _gelu(y) * jax.nn.sigmoid(z[:, D_SSM:2 * D_SSM])
    mix_ref[:, :D_SSM] = _rms(ya, gos_ref[...])
    ub = _gelu(z[:, 2 * D_SSM:2 * D_SSM + D_GMLP])
    vbn = _head_norm_gelu(z[:, 2 * D_SSM + D_GMLP:], gn_ref[...])
    vrow_ref[...] = vbn
    s = w00_ref[...] * vbn + b0_ref[...]
    mix_ref[:, D_SSM:] = _rms(ub * s, gog_ref[...])


def _front_sample(x, g1, win, gn, w00, b0, gog, wb, wc, coef, dsk, gos, h0r, h0i):
    n = x.shape[0]
    vmem = pl.BlockSpec(memory_space=pltpu.VMEM)
    return pl.pallas_call(
        _front_sample_kernel,
        in_specs=[vmem] * 14,
        out_specs=[vmem] * 4,
        out_shape=[jax.ShapeDtypeStruct((n, D_MODEL), _F32),
                   jax.ShapeDtypeStruct((n, STATE_COLS), _F32),
                   jax.ShapeDtypeStruct((n, STATE_COLS), _F32),
                   jax.ShapeDtypeStruct((n, D_GMLP), _F32)],
        compiler_params=pltpu.CompilerParams(vmem_limit_bytes=VMEM_LIMIT),
        name="front_sample",
    )(x, g1, win, gn, w00, b0, gog, wb, wc, coef, dsk, gos, h0r, h0i)


def _route(logits, base):
    tm = logits.shape[0]
    lt = logits.T
    ex = lt[:N_EXPERTS, :]
    gr = lt[N_EXPERTS:N_EXPERTS + SUBLANES, :]
    row_e = lax.broadcasted_iota(jnp.int32, ex.shape, 0).astype(_F32)
    row_g = lax.broadcasted_iota(jnp.int32, gr.shape, 0).astype(_F32)
    neg = jnp.float32(-jnp.inf)
    big = jnp.float32(LANES)
    is_g = row_g < N_EXPERT_GROUPS
    gl = jnp.where(is_g, gr, neg)
    gmax = jnp.max(gl, axis=0, keepdims=True)
    gi = jnp.min(jnp.where(gl == gmax, row_g, big), axis=0, keepdims=True)
    p_top = 1.0 / jnp.sum(jnp.where(is_g, jnp.exp(gl - gmax), 0.0), axis=0, keepdims=True)
    lo = gi * EXPERTS_PER_GROUP
    in_grp = (row_e >= lo) & (row_e < lo + EXPERTS_PER_GROUP)
    m1 = jnp.max(jnp.where(in_grp, ex, neg), axis=0, keepdims=True)
    i1 = jnp.min(jnp.where(in_grp & (ex == m1), row_e, big), axis=0, keepdims=True)
    rest = in_grp & (row_e != i1)
    m2 = jnp.max(jnp.where(rest, ex, neg), axis=0, keepdims=True)
    i2 = jnp.min(jnp.where(rest & (ex == m2), row_e, big), axis=0, keepdims=True)
    e2 = jnp.exp(m2 - m1)
    w1 = p_top / (1.0 + e2)
    w2 = p_top * e2 / (1.0 + e2)
    sel1 = row_e == i1
    sel2 = row_e == i2
    hits = jnp.where(sel1 | sel2, 1.0, 0.0)
    src = lax.broadcasted_iota(jnp.int32, (tm, tm), 0)
    dst = lax.broadcasted_iota(jnp.int32, (tm, tm), 1)
    before = _dot(hits.astype(_BF16), jnp.where(src < dst, 1.0, 0.0).astype(_BF16)) + base
    rank1 = jnp.sum(jnp.where(sel1, before, 0.0), axis=0, keepdims=True)
    rank2 = jnp.sum(jnp.where(sel2, before, 0.0), axis=0, keepdims=True)
    fields = {R_E1: i1, R_E2: i2, R_W1: w1, R_W2: w2, R_RANK1: rank1, R_RANK2: rank2,
              R_CODE1: i1 * CODE_SHIFT + rank1, R_CODE2: i2 * CODE_SHIFT + rank2}
    row8 = lax.broadcasted_iota(jnp.int32, (SUBLANES, tm), 0)
    route_t = jnp.zeros((SUBLANES, tm), _F32)
    for r, val in fields.items():
        route_t = jnp.where(row8 == r, val, route_t)
    route = jnp.concatenate([route_t, jnp.zeros((LANES - SUBLANES, tm), _F32)], axis=0).T
    return route_t, route, base + jnp.sum(hits, axis=1, keepdims=True)


def _mixer_out_prompt_kernel(x_ref, mixa_ref, mixb_ref, wo_ref, g2_ref, wr_ref, br_ref,
                             x1_ref, xn_ref, route_ref, route_t_ref, cnt_ref, base_ref, logits_ref, wo_bf):
    i = pl.program_id(0)

    @pl.when(i == 0)
    def _():
        base_ref[...] = jnp.zeros_like(base_ref)
        logits_ref[...] = jnp.zeros_like(logits_ref)
        wo_bf[...] = wo_ref[...].astype(_BF16)

    prev_logits = logits_ref[...]
    x1 = x_ref[0] + _dot(mixa_ref[0], wo_bf[:D_SSM, :]) + _dot(mixb_ref[0], wo_bf[D_SSM:, :])
    xn = _rms(x1, g2_ref[...])
    x1_ref[...] = _pack_bf16_pair(x1)
    xn_ref[...] = _pack_bf16_pair(xn)
    logits_ref[...] = _dot(xn.astype(_BF16), wr_ref[...]) + br_ref[...]
    route_t, route, base = _route(prev_logits, base_ref[...])
    route_ref[...] = route
    route_t_ref[...] = route_t
    base = jnp.where(i >= 1, base, base_ref[...])
    base_ref[...] = base
    cnt_ref[...] = base


def _mixer_out_sample_kernel(x_ref, mix_ref, wo_ref, g2_ref, wr_ref, br_ref, cnt_in_ref,
                             x1_in, xn_in, route_in, route_t_in,
                             x1_ref, xn_ref, route_ref, route_t_ref, cnt_ref):
    del x1_in, xn_in, route_in, route_t_in
    x1 = (x_ref[...] + _dot_hi(mix_ref[:, :D_SSM], wo_ref[:D_SSM, :])
          + _dot_hi(mix_ref[:, D_SSM:], wo_ref[D_SSM:, :]))
    xn = _rms(x1, g2_ref[...])
    logits = _dot_hi(xn, wr_ref[...]) + br_ref[...]
    route_t, route, base = _route(logits, cnt_in_ref[...])
    x1_ref[...] = _pack_bf16_pair(x1)
    xn_ref[...] = _pack_bf16_pair(xn)
    route_ref[...] = route
    route_t_ref[...] = route_t
    cnt_ref[...] = base


def _mixer_out(x_p, mixa, mixb, x_s, mix_s, wo, g2, wr, br):
    n, l, d = x_p.shape
    ns = x_s.shape[0]
    t_all = n * l + ns
    tm = TOK_TM
    per_seq = l // tm
    n_tiles = n * per_seq
    cur = lambda i: jnp.minimum(i, n_tiles - 1)
    prev = lambda i: jnp.maximum(i - 1, 0)
    const = lambda *shape: pl.BlockSpec(shape, lambda i: (0,) * len(shape))
    seq = lambda w: pl.BlockSpec((1, tm, w), lambda i: (cur(i) // per_seq, cur(i) % per_seq, 0))
    tok = lambda w, which: pl.BlockSpec((tm, w), lambda i: (which(i), 0))
    tok_shapes = [jax.ShapeDtypeStruct((t_all, d // 2), _U32),
                  jax.ShapeDtypeStruct((t_all, d // 2), _U32),
                  jax.ShapeDtypeStruct((t_all, LANES), _F32),
                  jax.ShapeDtypeStruct((SUBLANES, t_all), _F32)]
    cnt_shape = jax.ShapeDtypeStruct((N_EXPERTS, 1), _F32)
    x1, xn, route, route_t, cnt = pl.pallas_call(
        _mixer_out_prompt_kernel,
        grid=(n_tiles + 1,),
        in_specs=[seq(d), seq(D_SSM), seq(D_GMLP),
                  const(d, d), const(1, d), const(d, LANES), const(1, LANES)],
        out_specs=[tok(d // 2, cur), tok(d // 2, cur), tok(LANES, prev),
                   pl.BlockSpec((SUBLANES, tm), lambda i: (0, prev(i))), const(N_EXPERTS, 1)],
        out_shape=tok_shapes + [cnt_shape],
        scratch_shapes=[pltpu.VMEM((N_EXPERTS, 1), _F32), pltpu.VMEM((tm, LANES), _F32),
                        pltpu.VMEM((d, d), _BF16)],
        compiler_params=pltpu.CompilerParams(
            dimension_semantics=("arbitrary",), vmem_limit_bytes=VMEM_LIMIT),
        name="mixer_out_prompt",
    )(x_p, mixa, mixb, wo, g2, wr.astype(_BF16), br)
    tail = (n * l) // ns
    c1 = lambda *shape: pl.BlockSpec(shape, lambda i: (0,) * len(shape))
    anyspec = pl.BlockSpec(memory_space=pl.ANY)
    tail_spec = lambda w: pl.BlockSpec((ns, w), lambda i: (tail, 0))
    return pl.pallas_call(
        _mixer_out_sample_kernel,
        grid=(1,),
        in_specs=[c1(ns, d), c1(ns, d), c1(d, d), c1(1, d), c1(d, LANES), c1(1, LANES), c1(N_EXPERTS, 1),
                  anyspec, anyspec, anyspec, anyspec],
        out_specs=[tail_spec(d // 2), tail_spec(d // 2), tail_spec(LANES),
                   pl.BlockSpec((SUBLANES, ns), lambda i: (0, tail)), c1(N_EXPERTS, 1)],
        out_shape=tok_shapes + [cnt_shape],
        input_output_aliases={7: 0, 8: 1, 9: 2, 10: 3},
        compiler_params=pltpu.CompilerParams(
            dimension_semantics=("arbitrary",), vmem_limit_bytes=VMEM_LIMIT),
        name="mixer_out_sample",
    )(x_s, mix_s, wo, g2, wr, br, cnt, x1, xn, route, route_t)


def _sc_stream(n_chunks, gather, write):
    gather(0).start()
    for j in range(n_chunks):
        if j + 1 < n_chunks:
            if j >= 1:
                write(j - 1).wait()
            gather(j + 1).start()
        gather(j).wait()
        write(j).start()
    if n_chunks >= 2:
        write(n_chunks - 2).wait()
    write(n_chunks - 1).wait()


def _sc_mesh():
    return plsc.VectorSubcoreMesh(core_axis_name="c", subcore_axis_name="s",
                                  num_cores=SC_CORES, num_subcores=SC_SUBCORES)


def _sc_buffers(chunk, w, dtype):
    return [pltpu.VMEM((chunk, w), dtype), pltpu.VMEM((chunk, w), dtype)] + [pltpu.SemaphoreType.DMA] * 4


def _sc_combine(table, idx, n_out, chunk):
    w = table.shape[1]
    rows_w = n_out // SC_WORKERS
    n_chunks = rows_w // chunk
    assert rows_w * SC_WORKERS == n_out and n_chunks * chunk == rows_w and rows_w % SUBLANES == 0

    def body(table_hbm, idx_hbm, out_hbm, idx_v, buf0, buf1, g0, g1, w0, w1):
        wid = lax.axis_index("s") * SC_CORES + lax.axis_index("c")
        base = pl.multiple_of(wid * rows_w, SUBLANES)
        pltpu.sync_copy(idx_hbm.at[pl.ds(base, rows_w)], idx_v)
        bufs, gsems, wsems = (buf0, buf1), (g0, g1), (w0, w1)

        def gather(j):
            return pltpu.make_async_copy(table_hbm.at[idx_v.at[pl.ds(j * chunk, chunk)]], bufs[j % 2], gsems[j % 2])

        def write(j):
            return pltpu.make_async_copy(bufs[j % 2], out_hbm.at[pl.ds(base + j * chunk, chunk)], wsems[j % 2])

        _sc_stream(n_chunks, gather, write)

    return pl.kernel(
        body,
        out_type=jax.ShapeDtypeStruct((n_out, w), table.dtype),
        mesh=_sc_mesh(),
        scratch_types=[pltpu.VMEM((rows_w,), jnp.int32)] + _sc_buffers(chunk, w, table.dtype),
        compiler_params=pltpu.CompilerParams(use_tc_tiling_on_sc=True),
        name="sc_combine",
    )(table, idx)


def _sc_dispatch(table, codes, start_row, n_out, chunk):
    t_all, w = table.shape
    n_pad = codes.shape[0]
    n_ent = 2 * t_all
    ent_w = n_pad // SC_WORKERS
    n_chunks = ent_w // chunk
    per_chunk = chunk // SC_LANES
    trash = n_out - (n_pad - n_ent)
    assert ent_w * SC_WORKERS == n_pad and n_chunks * chunk == ent_w
    assert per_chunk * SC_LANES == chunk and chunk <= LANES and n_pad - n_ent <= t_all

    def body(table_hbm, code_hbm, start_hbm, out_hbm, dest_hbm,
             code_v, dest_v, tok_v, dst_v, start_v, buf0, buf1, g0, g1, w0, w1):
        wid = lax.axis_index("s") * SC_CORES + lax.axis_index("c")
        ebase = pl.multiple_of(wid * ent_w, SUBLANES)
        pltpu.sync_copy(code_hbm.at[pl.ds(ebase, ent_w)], code_v)
        pltpu.sync_copy(start_hbm, start_v)
        lane = lax.iota(jnp.int32, SC_LANES)
        for j in range(n_chunks):
            for c in range(per_chunk):
                off = j * chunk + c * SC_LANES
                ent = ebase + off + lane
                code = code_v[pl.ds(off, SC_LANES)]
                d = plsc.load_gather(start_v, [code >> CODE_BITS]) + (code & ((1 << CODE_BITS) - 1))
                d = jnp.where(ent >= n_ent, trash + (ent - n_ent), d)
                tok = jnp.where(ent >= t_all, ent - t_all, ent)
                tok = jnp.where(tok >= t_all, tok - t_all, tok)
                dest_v[pl.ds(off, SC_LANES)] = d
                dst_v[j, pl.ds(c * SC_LANES, SC_LANES)] = d
                tok_v[j, pl.ds(c * SC_LANES, SC_LANES)] = tok
        pltpu.sync_copy(dest_v, dest_hbm.at[pl.ds(ebase, ent_w)])
        bufs, gsems, wsems = (buf0, buf1), (g0, g1), (w0, w1)

        def gather(j):
            return pltpu.make_async_copy(table_hbm.at[tok_v.at[j]], bufs[j % 2], gsems[j % 2])

        def scatter(j):
            return pltpu.make_async_copy(bufs[j % 2], out_hbm.at[dst_v.at[j]], wsems[j % 2])

        _sc_stream(n_chunks, gather, scatter)

    return pl.kernel(
        body,
        out_type=(jax.ShapeDtypeStruct((n_out, w), table.dtype), jax.ShapeDtypeStruct((n_pad,), jnp.int32)),
        mesh=_sc_mesh(),
        scratch_types=([pltpu.VMEM((ent_w,), jnp.int32), pltpu.VMEM((ent_w,), jnp.int32),
                        pltpu.VMEM((n_chunks, chunk), jnp.int32), pltpu.VMEM((n_chunks, chunk), jnp.int32),
                        pltpu.VMEM((LANES,), jnp.int32)] + _sc_buffers(chunk, w, table.dtype)),
        compiler_params=pltpu.CompilerParams(use_tc_tiling_on_sc=True, needs_layout_passes=False),
        name="sc_dispatch",
    )(table, codes, start_row)


def _experts_kernel(piece_start_ref, piece_row_ref, piece_cls_ref, wg_ref, wu_ref, wd_ref, xs_hbm, ys_hbm,
                    wg_bf, wu_bf, wd_bf, xbuf, ybuf, xsem, ysem):
    e = pl.program_id(0)
    g0 = piece_start_ref[e]
    n_here = piece_start_ref[e + 1] - g0
    n_total = piece_start_ref[N_EXPERTS]

    def per_class(g, fn):
        cls = piece_cls_ref[g]
        row = pl.multiple_of(piece_row_ref[g], EXP_UNIT)
        for c in range(1, EXP_CLASSES + 1):
            pl.when(cls == c)(lambda c=c: fn(c * EXP_UNIT, row))

    def x_copy(slot, rows, row):
        return pltpu.make_async_copy(xs_hbm.at[pl.ds(row, rows)], xbuf.at[slot, pl.ds(0, rows)], xsem.at[slot])

    def y_copy(slot, rows, row):
        return pltpu.make_async_copy(ybuf.at[slot, pl.ds(0, rows)], ys_hbm.at[pl.ds(row, rows)], ysem.at[slot])

    @pl.when((e == 0) & (n_total > 0))
    def _():
        per_class(0, lambda rows, row: x_copy(0, rows, row).start())

    wg_bf[...] = wg_ref[0].astype(_BF16)
    wu_bf[...] = wu_ref[0].astype(_BF16)
    wd_bf[...] = wd_ref[0].astype(_BF16)

    def piece(j, carry):
        g = g0 + j
        slot = lax.rem(g, 2)
        per_class(g, lambda rows, row: x_copy(slot, rows, row).wait())

        @pl.when(g + 1 < n_total)
        def _():
            per_class(g + 1, lambda rows, row: x_copy(1 - slot, rows, row).start())

        @pl.when(g >= 2)
        def _():
            per_class(g - 2, lambda rows, row: y_copy(slot, rows, row).wait())

        def compute(rows, row):
            x = _unpack_bf16_pair(xbuf[slot, pl.ds(0, rows)]).astype(_BF16)
            a = _dot(x, wg_bf[...])
            u = _dot(x, wu_bf[...])
            h = (a * jax.nn.sigmoid(a) * u).astype(_BF16)
            ybuf[slot, pl.ds(0, rows)] = _pack_bf16_pair(_dot(h, wd_bf[...]))
            y_copy(slot, rows, row).start()

        per_class(g, compute)
        return carry

    lax.fori_loop(0, n_here, piece, 0)

    @pl.when(e == N_EXPERTS - 1)
    def _():
        @pl.when(n_total >= 2)
        def _():
            per_class(n_total - 2, lambda rows, row: y_copy(lax.rem(n_total, 2), rows, row).wait())

        @pl.when(n_total >= 1)
        def _():
            per_class(n_total - 1, lambda rows, row: y_copy(lax.rem(n_total - 1, 2), rows, row).wait())


def _experts(piece_start, piece_row, piece_cls, n_rows, xs, w_gate, w_up, w_down):
    dh = xs.shape[1]
    d = 2 * dh
    tm = EXP_UNIT * EXP_CLASSES
    anyspec = pl.BlockSpec(memory_space=pl.ANY)
    wsel = lambda e, ps, pr, pc: (e, 0, 0)
    grid_spec = pltpu.PrefetchScalarGridSpec(
        num_scalar_prefetch=3,
        grid=(N_EXPERTS,),
        in_specs=[pl.BlockSpec((1, d, D_EXPERT), wsel), pl.BlockSpec((1, d, D_EXPERT), wsel),
                  pl.BlockSpec((1, D_EXPERT, d), wsel), anyspec],
        out_specs=anyspec,
        scratch_shapes=[pltpu.VMEM((d, D_EXPERT), _BF16), pltpu.VMEM((d, D_EXPERT), _BF16),
                        pltpu.VMEM((D_EXPERT, d), _BF16),
                        pltpu.VMEM((2, tm, dh), _U32), pltpu.VMEM((2, tm, dh), _U32),
                        pltpu.SemaphoreType.DMA((2,)), pltpu.SemaphoreType.DMA((2,))],
    )
    return pl.pallas_call(
        _experts_kernel,
        grid_spec=grid_spec,
        out_shape=jax.ShapeDtypeStruct((n_rows, dh), _U32),
        compiler_params=pltpu.CompilerParams(
            dimension_semantics=("arbitrary",), vmem_limit_bytes=VMEM_LIMIT),
        name="experts",
    )(piece_start, piece_row, piece_cls, w_gate, w_up, w_down, xs)


def _final_kernel(x1_ref, ya_ref, yb_ref, route_ref, gf_ref, y_ref):
    route = route_ref[...]
    x2 = (_unpack_bf16_pair(x1_ref[...]) + route[:, R_W1:R_W1 + 1] * _unpack_bf16_pair(ya_ref[...])
          + route[:, R_W2:R_W2 + 1] * _unpack_bf16_pair(yb_ref[...]))
    y_ref[...] = _rms(x2, gf_ref[...])


def _final(x1, yab, route, gf, n_prompt, n_sample):
    d = 2 * x1.shape[1]

    def call(tm, first_block, n_rows, name):
        tok = lambda w: pl.BlockSpec((tm, w), lambda i: (first_block + i, 0))
        sel = lambda k: pl.BlockSpec((None, tm, d // 2), lambda i: (k, first_block + i, 0))
        return pl.pallas_call(
            _final_kernel,
            grid=(n_rows // tm,),
            in_specs=[tok(d // 2), sel(0), sel(1), tok(LANES), pl.BlockSpec((1, d), lambda i: (0, 0))],
            out_specs=pl.BlockSpec((tm, d), lambda i: (i, 0)),
            out_shape=jax.ShapeDtypeStruct((n_rows, d), _F32),
            compiler_params=pltpu.CompilerParams(
                dimension_semantics=("arbitrary",), vmem_limit_bytes=VMEM_LIMIT),
            name=name,
        )(x1, yab, yab, route, gf)

    return (call(FINAL_TM, 0, n_prompt, "final_prompt"),
            call(n_sample, n_prompt // n_sample, n_sample, "final_sample"))


def _powers(lam_re, lam_im, dt):
    out = []
    for m in range(SSM_BLK + 1):
        mag = jnp.exp(m * lam_re * dt)
        ang = m * lam_im * dt
        out.append((mag * jnp.cos(ang), mag * jnp.sin(ang)))
    return out


def _spread(x, copies):
    w = x.shape[1]
    src = lax.broadcasted_iota(jnp.int32, (w, w * copies), 0)
    dst = lax.broadcasted_iota(jnp.int32, (w, w * copies), 1)
    return _dot_f32(x, jnp.where(dst % w == src, 1.0, 0.0))


def _ssm_prep_kernel(lam_ref, b_re, b_im, c_re, c_im, v_ref, r_ref, wb_ref, wc_ref, coef_ref):
    n_p, n_h = SSM_STATE, SSM_GROUP
    lr, li, dt = lam_ref[0:1, :], lam_ref[1:2, :], lam_ref[2:3, :]
    pw = _powers(lr, li, dt)
    den = lr * lr + li * li
    nr, ni = pw[1][0] - 1.0, pw[1][1]
    k_re = (nr * lr + ni * li) / den
    k_im = (ni * lr - nr * li) / den
    coef_ref[...] = jnp.concatenate(
        [pw[1][0], pw[1][1], pw[SSM_BLK][0], pw[SSM_BLK][1], jnp.zeros((SUBLANES - 4, TILE_STATE), _F32)], axis=0)

    on_diag_b = (lax.broadcasted_iota(jnp.int32, (TILE_STATE, LANES), 0) // n_p
                 == lax.broadcasted_iota(jnp.int32, (TILE_STATE, LANES), 1) // n_h)
    rows_gp = lambda ref: ref[...].reshape(TILE_STATE, n_h)
    bt_re = jnp.where(on_diag_b, _spread(rows_gp(b_re), SUBLANES), 0.0).T
    bt_im = jnp.where(on_diag_b, _spread(rows_gp(b_im), SUBLANES), 0.0).T
    bb_re = k_re * bt_re - k_im * bt_im
    bb_im = k_re * bt_im + k_im * bt_re
    wb_ref[0] = jnp.concatenate([bb_re, bb_im], axis=1)
    v_rows = []
    for s in range(SSM_BLK):
        pr, pi = pw[SSM_BLK - 1 - s]
        v_rows.append(jnp.concatenate([pr * bb_re - pi * bb_im, pr * bb_im + pi * bb_re], axis=1))
    v_ref[0] = jnp.concatenate(v_rows, axis=0).astype(v_ref.dtype)

    on_diag_c = (lax.broadcasted_iota(jnp.int32, (LANES, TILE_STATE), 0) // n_h
                 == lax.broadcasted_iota(jnp.int32, (LANES, TILE_STATE), 1) // n_p)
    rows_gh = lambda ref: ref[...].reshape(LANES, n_p)
    ct_re = jnp.where(on_diag_c, _spread(rows_gh(c_re), SUBLANES), 0.0)
    ct_im = jnp.where(on_diag_c, _spread(rows_gh(c_im), SUBLANES), 0.0)
    cl = [(ct_re * pr - ct_im * pi, ct_re * pi + ct_im * pr) for pr, pi in pw]
    wc_ref[0] = jnp.concatenate([cl[0][0], -cl[0][1]], axis=1).T
    direct = [_dot_hi(cl[m][0], bb_re, transpose_b=True) - _dot_hi(cl[m][1], bb_im, transpose_b=True)
              for m in range(SSM_BLK)]
    zero = jnp.zeros((LANES, LANES), _F32)
    rt = jnp.concatenate(
        [jnp.concatenate([cl[i + 1][0], -cl[i + 1][1]]
                         + [direct[i - s] if s <= i else zero for s in range(SSM_BLK)], axis=1)
         for i in range(SSM_BLK)], axis=0)
    r_ref[0] = rt.T.astype(r_ref.dtype)


def _ssm_params(lam_re, lam_im, log_dt, b_re, b_im, c_re, c_im, d_skip):
    n_g, n_p, n_h = N_SSM_GROUPS, SSM_STATE, SSM_GROUP
    dt = jnp.repeat(jnp.exp(log_dt), n_p)
    lam = jnp.zeros((SUBLANES, STATE_COLS), _F32).at[0].set(lam_re.reshape(-1)).at[1].set(
        lam_im.reshape(-1)).at[2].set(dt)
    groups = lambda r, c: pl.BlockSpec((SUBLANES, r, c), lambda k: (k, 0, 0))
    out3 = lambda rows, w: pl.BlockSpec((1, rows, w), lambda k: (k, 0, 0))
    cols = pl.BlockSpec((SUBLANES, TILE_STATE), lambda k: (0, k))
    k_blk = SSM_BLK * LANES
    v, r, wb, wc, coef = pl.pallas_call(
        _ssm_prep_kernel,
        grid=(N_LANE_TILES,),
        in_specs=[cols, groups(n_p, n_h), groups(n_p, n_h), groups(n_h, n_p), groups(n_h, n_p)],
        out_specs=[out3(k_blk, 2 * TILE_STATE), out3(2 * TILE_STATE + k_blk, k_blk),
                   out3(LANES, 2 * TILE_STATE), out3(2 * TILE_STATE, LANES), cols],
        out_shape=[jax.ShapeDtypeStruct((N_LANE_TILES, k_blk, 2 * TILE_STATE), _BF16),
                   jax.ShapeDtypeStruct((N_LANE_TILES, 2 * TILE_STATE + k_blk, k_blk), _BF16),
                   jax.ShapeDtypeStruct((N_LANE_TILES, LANES, 2 * TILE_STATE), _F32),
                   jax.ShapeDtypeStruct((N_LANE_TILES, 2 * TILE_STATE, LANES), _F32),
                   jax.ShapeDtypeStruct((SUBLANES, STATE_COLS), _F32)],
        compiler_params=pltpu.CompilerParams(
            dimension_semantics=("arbitrary",), vmem_limit_bytes=VMEM_LIMIT),
        name="ssm_prep",
    )(lam, b_re, b_im, c_re, c_im)
    return wb, wc, v, r, coef, d_skip.reshape(1, D_SSM)


def _dispatch_plan(route_t, cnt):
    t_all = route_t.shape[1]
    codes = route_t[R_CODE1:R_CODE2 + 1].astype(jnp.int32).reshape(-1)
    per_pass = SC_WORKERS * DISPATCH_CHUNK
    codes = jnp.pad(codes, (0, -(2 * t_all) % per_pass))
    counts = cnt[:, 0].astype(jnp.int32)
    zero = jnp.zeros((1,), jnp.int32)
    units = (counts + EXP_UNIT - 1) // EXP_UNIT
    unit_start = jnp.concatenate([zero, jnp.cumsum(units)])
    start_row = jnp.zeros((LANES,), jnp.int32).at[:N_EXPERTS].set(unit_start[:N_EXPERTS] * EXP_UNIT)
    pieces = (units + EXP_CLASSES - 1) // EXP_CLASSES
    piece_start = jnp.concatenate([zero, jnp.cumsum(pieces)])
    tm = EXP_UNIT * EXP_CLASSES
    max_units = (2 * t_all + N_EXPERTS * (EXP_UNIT - 1)) // EXP_UNIT
    max_pieces = (max_units + N_EXPERTS * (EXP_CLASSES - 1)) // EXP_CLASSES
    g = jnp.arange(max_pieces, dtype=jnp.int32)
    owner = ((g[:, None] >= piece_start[None, :-1]) & (g[:, None] < piece_start[None, 1:])).astype(jnp.int32)
    pick = lambda table: jnp.sum(owner * table[None, :], axis=1)
    first_unit = pick(unit_start[:-1]) + (g - pick(piece_start[:-1])) * EXP_CLASSES
    piece_row = first_unit * EXP_UNIT
    piece_cls = jnp.clip(pick(unit_start[1:]) - first_unit, 1, EXP_CLASSES)
    n_rows = (max_units * EXP_UNIT + tm - 1) // tm * tm + tm
    return codes, start_row, n_rows, piece_start, piece_row, piece_cls


def kernel(x_prompt, x_sample, state_ssm_re, state_ssm_im, norm1_g, w_in, lam_re, lam_im, log_dt, ssm_b_re, ssm_b_im, ssm_c_re, ssm_c_im, ssm_d, gmlp_norm_g, gmlp_w_s, gmlp_b_s, out_norm_ssm_g, out_norm_gmlp_g, w_out, norm2_g, w_router_group, b_router_group, w_router_expert, b_router_expert, w_gate, w_up, w_down, final_norm_g):
    n, l, d = x_prompt.shape
    ns = x_sample.shape[0]
    t_all = n * l + ns
    li = 0
    g1 = norm1_g[li].reshape(1, d)
    gn = gmlp_norm_g[li].reshape(1, D_GMLP)
    tril = jnp.tril(jnp.ones((CHUNK, CHUNK), dtype=bool))
    ws_tril = jnp.where(tril[None], gmlp_w_s[li], 0.0)
    bs = gmlp_b_s[li]
    gog = out_norm_gmlp_g[li].reshape(1, D_GMLP)
    gos = out_norm_ssm_g[li].reshape(1, D_SSM)
    wb, wc, v_blk, r_blk, coef, dsk = _ssm_params(
        lam_re[li], lam_im[li], log_dt[li], ssm_b_re[li], ssm_b_im[li], ssm_c_re[li], ssm_c_im[li], ssm_d[li])
    g2 = norm2_g[li].reshape(1, d)
    pad = LANES - N_EXPERTS - N_EXPERT_GROUPS
    wr = jnp.concatenate([w_router_expert[li], w_router_group[li], jnp.zeros((d, pad), _F32)], axis=1)
    br = jnp.concatenate([b_router_expert[li], b_router_group[li], jnp.zeros((pad,), _F32)]).reshape(1, LANES)

    xa, sg, mixb = _front_prompt(x_prompt, g1, w_in[li], gn, ws_tril.astype(_BF16), bs.T, gog)
    mixa, hfin = _ssm_prompt(xa, sg, v_blk, r_blk, coef, dsk, gos)
    w00 = jnp.repeat(ws_tril[:, 0, 0], GMLP_HEAD).reshape(1, D_GMLP)
    b0 = jnp.repeat(bs[:, 0], GMLP_HEAD).reshape(1, D_GMLP)
    mix_s, hr_s, hi_s, vrow = _front_sample(
        x_sample.reshape(ns, d), g1, w_in[li], gn, w00, b0, gog, wb, wc, coef, dsk, gos,
        state_ssm_re[li].reshape(ns, STATE_COLS), state_ssm_im[li].reshape(ns, STATE_COLS))

    x1, xn, route, route_t, cnt = _mixer_out(x_prompt, mixa, mixb, x_sample.reshape(ns, d), mix_s,
                                             w_out[li], g2, wr, br)
    codes, start_row, n_rows, piece_start, piece_row, piece_cls = _dispatch_plan(route_t, cnt)
    xs, dest = _sc_dispatch(xn, codes, start_row, n_rows, DISPATCH_CHUNK)
    ys = _experts(piece_start, piece_row, piece_cls, n_rows, xs, w_gate[li], w_up[li], w_down[li])
    yab = _sc_combine(ys, dest, 2 * t_all, COMBINE_CHUNK).reshape(2, t_all, d // 2)
    y_p, y_s = _final(x1, yab, route, final_norm_g.reshape(1, d), n * l, ns)

    hf = hfin.reshape(n, N_LANE_TILES, 2, 8, SSM_STATE)
    re_p = hf[:, :, 0].reshape(1, n, N_SSM_GROUPS, SSM_STATE)
    im_p = hf[:, :, 1].reshape(1, n, N_SSM_GROUPS, SSM_STATE)
    re_s = hr_s.reshape(1, ns, N_SSM_GROUPS, SSM_STATE)
    im_s = hi_s.reshape(1, ns, N_SSM_GROUPS, SSM_STATE)
    return (y_p.reshape(n, l, d), y_s.reshape(ns, 1, d), re_p, im_p, re_s, im_s,
            vrow.reshape(1, ns, 1, D_GMLP))
```

```python
import math

import jax
import jax.numpy as jnp
from jax import lax
from jax.experimental import pallas as pl
from jax.experimental.pallas import tpu as pltpu
from jax.experimental.pallas import tpu_sc as plsc

D_MODEL = 1024
D_SSM = 512
D_GMLP = 512
SSM_GROUP = 16
N_SSM_GROUPS = 32
SSM_STATE = 64
CHUNK = 128
N_GMLP_HEADS = 4
GMLP_HEAD = 128
N_EXPERT_GROUPS = 4
EXPERTS_PER_GROUP = 8
N_EXPERTS = 32
D_EXPERT = 512
D_IN = 2048
EPS = 1e-6

LANES = 128
SUBLANES = 8
N_LANE_TILES = D_SSM // LANES
STATE_COLS = N_SSM_GROUPS * SSM_STATE
TILE_STATE = STATE_COLS // N_LANE_TILES
VMEM_LIMIT = 56 * 1024 * 1024

SC_CORES = 2
SC_SUBCORES = 16
SC_LANES = 16
SC_WORKERS = SC_CORES * SC_SUBCORES

FRONT_TL = 512
SSM_LC = 256
SSM_BLK = 4
COEF_LB_RE, COEF_LB_IM, COEF_LBLK_RE, COEF_LBLK_IM = 0, 1, 2, 3
TOK_TM = 512
FINAL_TM = 1024
EXP_UNIT = 128
EXP_CLASSES = 8
DISPATCH_CHUNK = 80
COMBINE_CHUNK = 24

R_E1, R_E2, R_W1, R_W2, R_RANK1, R_RANK2, R_CODE1, R_CODE2 = 0, 1, 2, 3, 4, 5, 6, 7
CODE_BITS = 16
CODE_SHIFT = float(1 << CODE_BITS)

_INV_SQRT2 = 1.0 / math.sqrt(2.0)
_BF16 = jnp.bfloat16
_F32 = jnp.float32
_U32 = jnp.uint32


def _gelu(x):
    return 0.5 * x * (1.0 + lax.erf(x * _INV_SQRT2))


def _rms(x, g):
    return x * lax.rsqrt(jnp.mean(x * x, axis=-1, keepdims=True) + EPS) * g


def _dot(a, b):
    return jnp.dot(a, b, preferred_element_type=_F32)


def _dot_f32(a, b):
    return jnp.dot(a, b, preferred_element_type=_F32, precision=lax.Precision.HIGHEST)


def _dot_hi(a, b, transpose_b=False):
    def split(x):
        hi = x.astype(_BF16)
        return hi, (x - hi.astype(_F32)).astype(_BF16)

    dims = (((1,), (1 if transpose_b else 0,)), ((), ()))
    dot = lambda u, v: lax.dot_general(u, v, dims, preferred_element_type=_F32)
    a_hi, a_lo = split(a)
    b_hi, b_lo = split(b)
    return dot(a_hi, b_hi) + dot(a_hi, b_lo) + dot(a_lo, b_hi)


def _pack_bf16_pair(x):
    w = x.shape[1] // 2
    hi = lax.bitcast_convert_type(x[:, :w].astype(_BF16).astype(_F32), _U32)
    lo = lax.bitcast_convert_type(x[:, w:].astype(_BF16).astype(_F32), _U32)
    return hi | (lo >> 16)


def _unpack_bf16_pair(p):
    hi = lax.bitcast_convert_type(p & jnp.uint32(0xFFFF0000), _F32)
    lo = lax.bitcast_convert_type(p << 16, _F32)
    return jnp.concatenate([hi, lo], axis=-1)


def _head_norm_gelu(vb, gn):
    v = _gelu(vb)
    parts = []
    for h in range(N_GMLP_HEADS):
        vh = v[:, h * GMLP_HEAD:(h + 1) * GMLP_HEAD]
        parts.append(vh * lax.rsqrt(jnp.mean(vh * vh, axis=-1, keepdims=True) + EPS))
    return jnp.concatenate(parts, axis=-1) * gn


def _front_prompt_kernel(x_ref, g1_ref, win_ref, gn_ref, ws_ref, bs_ref, gog_ref,
                         xa_ref, sg_ref, mixb_ref, win_bf, z_ref):
    @pl.when(pl.program_id(0) == 0)
    def _():
        win_bf[...] = win_ref[...].astype(_BF16)
        z_ref[...] = jnp.zeros_like(z_ref)

    z = z_ref[...]
    x = x_ref[0]
    hn = _rms(x, g1_ref[...]).astype(_BF16)
    z_ref[...] = _dot(hn, win_bf[...])
    xa_ref[0] = z[:, :D_SSM]
    sg_ref[0] = jax.nn.sigmoid(z[:, D_SSM:2 * D_SSM])
    ub = _gelu(z[:, 2 * D_SSM:2 * D_SSM + D_GMLP])
    vbn = _head_norm_gelu(z[:, 2 * D_SSM + D_GMLP:], gn_ref[...]).astype(_BF16)
    tl = x.shape[0]
    rows = []
    for c in range(tl // CHUNK):
        heads = []
        for h in range(N_GMLP_HEADS):
            vh = vbn[c * CHUNK:(c + 1) * CHUNK, h * GMLP_HEAD:(h + 1) * GMLP_HEAD]
            heads.append(_dot(ws_ref[h], vh) + bs_ref[:, h:h + 1])
        rows.append(jnp.concatenate(heads, axis=-1))
    s = jnp.concatenate(rows, axis=0)
    mixb_ref[0] = _rms(ub * s, gog_ref[...]).astype(_BF16)


def _front_prompt(x, g1, win, gn, ws_tril_bf, bs_t, gog):
    n, l, d = x.shape
    tl = FRONT_TL
    per_seq = l // tl
    n_tiles = n * per_seq
    cur = lambda i: jnp.minimum(i, n_tiles - 1)
    prev = lambda i: jnp.maximum(i - 1, 0)
    const = lambda *shape: pl.BlockSpec(shape, lambda i: (0,) * len(shape))
    seq = lambda w, which: pl.BlockSpec((1, tl, w), lambda i: (which(i) // per_seq, which(i) % per_seq, 0))
    return pl.pallas_call(
        _front_prompt_kernel,
        grid=(n_tiles + 1,),
        in_specs=[seq(d, cur), const(1, d), const(d, D_IN), const(1, D_GMLP),
                  const(N_GMLP_HEADS, CHUNK, CHUNK), const(CHUNK, N_GMLP_HEADS), const(1, D_GMLP)],
        out_specs=[seq(D_SSM, prev), seq(D_SSM, prev), seq(D_GMLP, prev)],
        out_shape=[jax.ShapeDtypeStruct((n, l, D_SSM), _F32),
                   jax.ShapeDtypeStruct((n, l, D_SSM), _F32),
                   jax.ShapeDtypeStruct((n, l, D_GMLP), _BF16)],
        scratch_shapes=[pltpu.VMEM((d, D_IN), _BF16), pltpu.VMEM((tl, D_IN), _F32)],
        compiler_params=pltpu.CompilerParams(
            dimension_semantics=("arbitrary",), vmem_limit_bytes=VMEM_LIMIT),
        name="front_prompt",
    )(x, g1, win, gn, ws_tril_bf, bs_t, gog)


def _ssm_prompt_kernel(xa_ref, sg_ref, v_ref, r_ref, coef_ref, dsk_ref, gos_ref,
                       mixa_ref, hfin_ref, s_ref, st_ref):
    lc = xa_ref.shape[1]
    nblk = lc // SSM_BLK
    rows = nblk * SUBLANES

    @pl.when(pl.program_id(0) == 0)
    def _():
        st_ref[...] = jnp.zeros_like(st_ref)

    def by_position(ref):
        t = pltpu.einshape("btc->tbc", ref[...]).reshape(nblk, SSM_BLK, SUBLANES, D_SSM)
        return [t[:, i].reshape(rows, D_SSM) for i in range(SSM_BLK)]

    xs = by_position(xa_ref)
    xs_bf = [x.astype(_BF16) for x in xs]
    xk = [jnp.concatenate([x[:, k * LANES:(k + 1) * LANES] for x in xs_bf], axis=-1)
          for k in range(N_LANE_TILES)]
    for k in range(N_LANE_TILES):
        s_ref[:, 2 * TILE_STATE * k:2 * TILE_STATE * (k + 1)] = _dot(xk[k], v_ref[k])

    for kk in range(0, N_LANE_TILES, 2):
        tiles = (kk, kk + 1)
        cols = [(2 * TILE_STATE * k, 2 * TILE_STATE * k + TILE_STATE) for k in tiles]
        lbs = [tuple(jnp.broadcast_to(coef_ref[row:row + 1, k * TILE_STATE:(k + 1) * TILE_STATE],
                                      (SUBLANES, TILE_STATE)) for row in (COEF_LBLK_RE, COEF_LBLK_IM))
               for k in tiles]

        def body(j, carry, cols=cols, lbs=lbs):
            r0 = pl.multiple_of(j * SUBLANES, SUBLANES)
            out = []
            for q, ((c_re, c_im), (lr, li)) in enumerate(zip(cols, lbs)):
                hr, hi = carry[2 * q], carry[2 * q + 1]
                sr = s_ref[pl.ds(r0, SUBLANES), c_re:c_re + TILE_STATE]
                si = s_ref[pl.ds(r0, SUBLANES), c_im:c_im + TILE_STATE]
                s_ref[pl.ds(r0, SUBLANES), c_re:c_re + TILE_STATE] = hr
                s_ref[pl.ds(r0, SUBLANES), c_im:c_im + TILE_STATE] = hi
                out += [lr * hr - li * hi + sr, lr * hi + li * hr + si]
            return tuple(out)

        init = tuple(st_ref[:, c:c + TILE_STATE] for c_pair in cols for c in c_pair)
        fin = lax.fori_loop(0, nblk, body, init, unroll=2)
        for q, (c_re, c_im) in enumerate(cols):
            st_ref[:, c_re:c_re + TILE_STATE] = fin[2 * q]
            st_ref[:, c_im:c_im + TILE_STATE] = fin[2 * q + 1]

    yk = []
    for k in range(N_LANE_TILES):
        h_in = s_ref[:, 2 * TILE_STATE * k:2 * TILE_STATE * (k + 1)].astype(_BF16)
        yk.append(_dot(jnp.concatenate([h_in, xk[k]], axis=-1), r_ref[k]))
    ys = []
    for i in range(SSM_BLK):
        y = jnp.concatenate([y_k[:, i * LANES:(i + 1) * LANES] for y_k in yk], axis=-1) + dsk_ref[...] * xs[i]
        ys.append(y.reshape(nblk, SUBLANES, D_SSM))
    y = pltpu.einshape("tbc->btc", jnp.stack(ys, axis=1).reshape(lc, SUBLANES, D_SSM))
    mixa_ref[...] = _rms(_gelu(y) * sg_ref[...], gos_ref[...]).astype(_BF16)
    hfin_ref[...] = st_ref[...]


def _ssm_prompt(xa, sg, v, r, coef, dsk, gos):
    n, l, _ = xa.shape
    lc = SSM_LC
    const = lambda *shape: pl.BlockSpec(shape, lambda i: (0,) * len(shape))
    seq_spec = pl.BlockSpec((n, lc, D_SSM), lambda i: (0, i, 0))
    return pl.pallas_call(
        _ssm_prompt_kernel,
        grid=(l // lc,),
        in_specs=[seq_spec, seq_spec, const(*v.shape), const(*r.shape),
                  const(*coef.shape), const(1, D_SSM), const(1, D_SSM)],
        out_specs=[seq_spec, const(n, 2 * STATE_COLS)],
        out_shape=[jax.ShapeDtypeStruct((n, l, D_SSM), _BF16),
                   jax.ShapeDtypeStruct((n, 2 * STATE_COLS), _F32)],
        scratch_shapes=[pltpu.VMEM((lc // SSM_BLK * n, 2 * STATE_COLS), _F32),
                        pltpu.VMEM((n, 2 * STATE_COLS), _F32)],
        compiler_params=pltpu.CompilerParams(
            dimension_semantics=("arbitrary",), vmem_limit_bytes=VMEM_LIMIT),
        name="ssm_prompt",
    )(xa, sg, v, r, coef, dsk, gos)


def _front_sample_kernel(x_ref, g1_ref, win_ref, gn_ref, w00_ref, b0_ref, gog_ref,
                         wb_ref, wc_ref, coef_ref, dsk_ref, gos_ref, h0r_ref, h0i_ref,
                         mix_ref, hr_ref, hi_ref, vrow_ref):
    x = x_ref[...]
    hn = _rms(x, g1_ref[...])
    z = _dot_hi(hn, win_ref[...])
    xa = z[:, :D_SSM]
    ys = []
    for k in range(N_LANE_TILES):
        bu = _dot_hi(xa[:, k * LANES:(k + 1) * LANES], wb_ref[k])
        sl = slice(k * TILE_STATE, (k + 1) * TILE_STATE)
        lr, li = coef_ref[COEF_LB_RE:COEF_LB_RE + 1, sl], coef_ref[COEF_LB_IM:COEF_LB_IM + 1, sl]
        h0r, h0i = h0r_ref[:, sl], h0i_ref[:, sl]
        nr = lr * h0r - li * h0i + bu[:, :TILE_STATE]
        ni = lr * h0i + li * h0r + bu[:, TILE_STATE:]
        hr_ref[:, sl] = nr
        hi_ref[:, sl] = ni
        ys.append(_dot_hi(jnp.concatenate([nr, ni], axis=-1), wc_ref[k]))
    y = jnp.concatenate(ys, axis=-1) + dsk_ref[...] * xa
    ya = _gelu(y) * jax.nn.sigmoid(z[:, D_SSM:2 * D_SSM])
    mix_ref[:, :D_SSM] = _rms(ya, gos_ref[...])
    ub = _gelu(z[:, 2 * D_SSM:2 * D_SSM + D_GMLP])
    vbn = _head_norm_gelu(z[:, 2 * D_SSM + D_GMLP:], gn_ref[...])
    vrow_ref[...] = vbn
    s = w00_ref[...] * vbn + b0_ref[...]
    mix_ref[:, D_SSM:] = _rms(ub * s, gog_ref[...])


def _front_sample(x, g1, win, gn, w00, b0, gog, wb, wc, coef, dsk, gos, h0r, h0i):
    n = x.shape[0]
    vmem = pl.BlockSpec(memory_space=pltpu.VMEM)
    return pl.pallas_call(
        _front_sample_kernel,
        in_specs=[vmem] * 14,
        out_specs=[vmem] * 4,
        out_shape=[jax.ShapeDtypeStruct((n, D_MODEL), _F32),
                   jax.ShapeDtypeStruct((n, STATE_COLS), _F32),
                   jax.ShapeDtypeStruct((n, STATE_COLS), _F32),
                   jax.ShapeDtypeStruct((n, D_GMLP), _F32)],
        compiler_params=pltpu.CompilerParams(vmem_limit_bytes=VMEM_LIMIT),
        name="front_sample",
    )(x, g1, win, gn, w00, b0, gog, wb, wc, coef, dsk, gos, h0r, h0i)


def _route(logits, base):
    tm = logits.shape[0]
    lt = logits.T
    ex = lt[:N_EXPERTS, :]
    gr = lt[N_EXPERTS:N_EXPERTS + SUBLANES, :]
    row_e = lax.broadcasted_iota(jnp.int32, ex.shape, 0).astype(_F32)
    row_g = lax.broadcasted_iota(jnp.int32, gr.shape, 0).astype(_F32)
    neg = jnp.float32(-jnp.inf)
    big = jnp.float32(LANES)
    is_g = row_g < N_EXPERT_GROUPS
    gl = jnp.where(is_g, gr, neg)
    gmax = jnp.max(gl, axis=0, keepdims=True)
    gi = jnp.min(jnp.where(gl == gmax, row_g, big), axis=0, keepdims=True)
    p_top = 1.0 / jnp.sum(jnp.where(is_g, jnp.exp(gl - gmax), 0.0), axis=0, keepdims=True)
    lo = gi * EXPERTS_PER_GROUP
    in_grp = (row_e >= lo) & (row_e < lo + EXPERTS_PER_GROUP)
    m1 = jnp.max(jnp.where(in_grp, ex, neg), axis=0, keepdims=True)
    i1 = jnp.min(jnp.where(in_grp & (ex == m1), row_e, big), axis=0, keepdims=True)
    rest = in_grp & (row_e != i1)
    m2 = jnp.max(jnp.where(rest, ex, neg), axis=0, keepdims=True)
    i2 = jnp.min(jnp.where(rest & (ex == m2), row_e, big), axis=0, keepdims=True)
    e2 = jnp.exp(m2 - m1)
    w1 = p_top / (1.0 + e2)
    w2 = p_top * e2 / (1.0 + e2)
    sel1 = row_e == i1
    sel2 = row_e == i2
    hits = jnp.where(sel1 | sel2, 1.0, 0.0)
    src = lax.broadcasted_iota(jnp.int32, (tm, tm), 0)
    dst = lax.broadcasted_iota(jnp.int32, (tm, tm), 1)
    before = _dot(hits.astype(_BF16), jnp.where(src < dst, 1.0, 0.0).astype(_BF16)) + base
    rank1 = jnp.sum(jnp.where(sel1, before, 0.0), axis=0, keepdims=True)
    rank2 = jnp.sum(jnp.where(sel2, before, 0.0), axis=0, keepdims=True)
    fields = {R_E1: i1, R_E2: i2, R_W1: w1, R_W2: w2, R_RANK1: rank1, R_RANK2: rank2,
              R_CODE1: i1 * CODE_SHIFT + rank1, R_CODE2: i2 * CODE_SHIFT + rank2}
    row8 = lax.broadcasted_iota(jnp.int32, (SUBLANES, tm), 0)
    route_t = jnp.zeros((SUBLANES, tm), _F32)
    for r, val in fields.items():
        route_t = jnp.where(row8 == r, val, route_t)
    return route_t, base + jnp.sum(hits, axis=1, keepdims=True)


def _mixer_out_prompt_kernel(x_ref, mixa_ref, mixb_ref, wo_ref, g2_ref, wr_ref, br_ref,
                             x1_ref, xn_ref, route_t_ref, cnt_ref, base_ref, logits_ref, wo_bf):
    i = pl.program_id(0)

    @pl.when(i == 0)
    def _():
        base_ref[...] = jnp.zeros_like(base_ref)
        logits_ref[...] = jnp.zeros_like(logits_ref)
        wo_bf[...] = wo_ref[...].astype(_BF16)

    prev_logits = logits_ref[...]
    x1 = x_ref[0] + _dot(mixa_ref[0], wo_bf[:D_SSM, :]) + _dot(mixb_ref[0], wo_bf[D_SSM:, :])
    xn = _rms(x1, g2_ref[...])
    x1_ref[...] = _pack_bf16_pair(x1)
    xn_ref[...] = _pack_bf16_pair(xn)
    logits_ref[...] = _dot(xn.astype(_BF16), wr_ref[...]) + br_ref[...]
    route_t, base = _route(prev_logits, base_ref[...])
    route_t_ref[...] = route_t
    base = jnp.where(i >= 1, base, base_ref[...])
    base_ref[...] = base
    cnt_ref[...] = base


def _mixer_out_sample_kernel(x_ref, mix_ref, wo_ref, g2_ref, wr_ref, br_ref, cnt_in_ref,
                             x1_in, xn_in, route_t_in,
                             x1_ref, xn_ref, route_t_ref, cnt_ref):
    del x1_in, xn_in, route_t_in
    x1 = (x_ref[...] + _dot_hi(mix_ref[:, :D_SSM], wo_ref[:D_SSM, :])
          + _dot_hi(mix_ref[:, D_SSM:], wo_ref[D_SSM:, :]))
    xn = _rms(x1, g2_ref[...])
    logits = _dot_hi(xn, wr_ref[...]) + br_ref[...]
    route_t, base = _route(logits, cnt_in_ref[...])
    x1_ref[...] = _pack_bf16_pair(x1)
    xn_ref[...] = _pack_bf16_pair(xn)
    route_t_ref[...] = route_t
    cnt_ref[...] = base


def _mixer_out(x_p, mixa, mixb, x_s, mix_s, wo, g2, wr, br):
    n, l, d = x_p.shape
    ns = x_s.shape[0]
    t_all = n * l + ns
    tm = TOK_TM
    per_seq = l // tm
    n_tiles = n * per_seq
    cur = lambda i: jnp.minimum(i, n_tiles - 1)
    prev = lambda i: jnp.maximum(i - 1, 0)
    const = lambda *shape: pl.BlockSpec(shape, lambda i: (0,) * len(shape))
    seq = lambda w: pl.BlockSpec((1, tm, w), lambda i: (cur(i) // per_seq, cur(i) % per_seq, 0))
    tok = lambda w, which: pl.BlockSpec((tm, w), lambda i: (which(i), 0))
    tok_shapes = [jax.ShapeDtypeStruct((t_all, d // 2), _U32),
                  jax.ShapeDtypeStruct((t_all, d // 2), _U32),
                  jax.ShapeDtypeStruct((SUBLANES, t_all), _F32)]
    cnt_shape = jax.ShapeDtypeStruct((N_EXPERTS, 1), _F32)
    x1, xn, route_t, cnt = pl.pallas_call(
        _mixer_out_prompt_kernel,
        grid=(n_tiles + 1,),
        in_specs=[seq(d), seq(D_SSM), seq(D_GMLP),
                  const(d, d), const(1, d), const(d, LANES), const(1, LANES)],
        out_specs=[tok(d // 2, cur), tok(d // 2, cur),
                   pl.BlockSpec((SUBLANES, tm), lambda i: (0, prev(i))), const(N_EXPERTS, 1)],
        out_shape=tok_shapes + [cnt_shape],
        scratch_shapes=[pltpu.VMEM((N_EXPERTS, 1), _F32), pltpu.VMEM((tm, LANES), _F32),
                        pltpu.VMEM((d, d), _BF16)],
        compiler_params=pltpu.CompilerParams(
            dimension_semantics=("arbitrary",), vmem_limit_bytes=VMEM_LIMIT),
        name="mixer_out_prompt",
    )(x_p, mixa, mixb, wo, g2, wr.astype(_BF16), br)
    tail = (n * l) // ns
    c1 = lambda *shape: pl.BlockSpec(shape, lambda i: (0,) * len(shape))
    anyspec = pl.BlockSpec(memory_space=pl.ANY)
    tail_spec = lambda w: pl.BlockSpec((ns, w), lambda i: (tail, 0))
    return pl.pallas_call(
        _mixer_out_sample_kernel,
        grid=(1,),
        in_specs=[c1(ns, d), c1(ns, d), c1(d, d), c1(1, d), c1(d, LANES), c1(1, LANES), c1(N_EXPERTS, 1),
                  anyspec, anyspec, anyspec],
        out_specs=[tail_spec(d // 2), tail_spec(d // 2),
                   pl.BlockSpec((SUBLANES, ns), lambda i: (0, tail)), c1(N_EXPERTS, 1)],
        out_shape=tok_shapes + [cnt_shape],
        input_output_aliases={7: 0, 8: 1, 9: 2},
        compiler_params=pltpu.CompilerParams(
            dimension_semantics=("arbitrary",), vmem_limit_bytes=VMEM_LIMIT),
        name="mixer_out_sample",
    )(x_s, mix_s, wo, g2, wr, br, cnt, x1, xn, route_t)


def _sc_stream(n_chunks, gather, write):
    gather(0).start()
    for j in range(n_chunks):
        if j + 1 < n_chunks:
            if j >= 1:
                write(j - 1).wait()
            gather(j + 1).start()
        gather(j).wait()
        write(j).start()
    if n_chunks >= 2:
        write(n_chunks - 2).wait()
    write(n_chunks - 1).wait()


def _sc_mesh():
    return plsc.VectorSubcoreMesh(core_axis_name="c", subcore_axis_name="s",
                                  num_cores=SC_CORES, num_subcores=SC_SUBCORES)


def _sc_buffers(chunk, w, dtype):
    return [pltpu.VMEM((chunk, w), dtype), pltpu.VMEM((chunk, w), dtype)] + [pltpu.SemaphoreType.DMA] * 4


def _sc_combine(table, idx, n_out, chunk):
    w = table.shape[1]
    rows_w = n_out // SC_WORKERS
    n_chunks = rows_w // chunk
    assert rows_w * SC_WORKERS == n_out and n_chunks * chunk == rows_w and rows_w % SUBLANES == 0

    def body(table_hbm, idx_hbm, out_hbm, idx_v, buf0, buf1, g0, g1, w0, w1):
        wid = lax.axis_index("s") * SC_CORES + lax.axis_index("c")
        base = pl.multiple_of(wid * rows_w, SUBLANES)
        pltpu.sync_copy(idx_hbm.at[pl.ds(base, rows_w)], idx_v)
        bufs, gsems, wsems = (buf0, buf1), (g0, g1), (w0, w1)

        def gather(j):
            return pltpu.make_async_copy(table_hbm.at[idx_v.at[pl.ds(j * chunk, chunk)]], bufs[j % 2], gsems[j % 2])

        def write(j):
            return pltpu.make_async_copy(bufs[j % 2], out_hbm.at[pl.ds(base + j * chunk, chunk)], wsems[j % 2])

        _sc_stream(n_chunks, gather, write)

    return pl.kernel(
        body,
        out_type=jax.ShapeDtypeStruct((n_out, w), table.dtype),
        mesh=_sc_mesh(),
        scratch_types=[pltpu.VMEM((rows_w,), jnp.int32)] + _sc_buffers(chunk, w, table.dtype),
        compiler_params=pltpu.CompilerParams(use_tc_tiling_on_sc=True),
        name="sc_combine",
    )(table, idx)


def _sc_dispatch(table, codes, start_row, n_out, chunk):
    t_all, w = table.shape
    n_pad = codes.shape[0]
    n_ent = 2 * t_all
    ent_w = n_pad // SC_WORKERS
    n_chunks = ent_w // chunk
    per_chunk = chunk // SC_LANES
    trash = n_out - (n_pad - n_ent)
    assert ent_w * SC_WORKERS == n_pad and n_chunks * chunk == ent_w
    assert per_chunk * SC_LANES == chunk and chunk <= LANES and n_pad - n_ent <= t_all

    def body(table_hbm, code_hbm, start_hbm, out_hbm, dest_hbm,
             code_v, dest_v, tok_v, dst_v, start_v, buf0, buf1, g0, g1, w0, w1):
        wid = lax.axis_index("s") * SC_CORES + lax.axis_index("c")
        ebase = pl.multiple_of(wid * ent_w, SUBLANES)
        pltpu.sync_copy(code_hbm.at[pl.ds(ebase, ent_w)], code_v)
        pltpu.sync_copy(start_hbm, start_v)
        lane = lax.iota(jnp.int32, SC_LANES)
        for j in range(n_chunks):
            for c in range(per_chunk):
                off = j * chunk + c * SC_LANES
                ent = ebase + off + lane
                code = code_v[pl.ds(off, SC_LANES)]
                d = plsc.load_gather(start_v, [code >> CODE_BITS]) + (code & ((1 << CODE_BITS) - 1))
                d = jnp.where(ent >= n_ent, trash + (ent - n_ent), d)
                tok = jnp.where(ent >= t_all, ent - t_all, ent)
                tok = jnp.where(tok >= t_all, tok - t_all, tok)
                dest_v[pl.ds(off, SC_LANES)] = d
                dst_v[j, pl.ds(c * SC_LANES, SC_LANES)] = d
                tok_v[j, pl.ds(c * SC_LANES, SC_LANES)] = tok
        pltpu.sync_copy(dest_v, dest_hbm.at[pl.ds(ebase, ent_w)])
        bufs, gsems, wsems = (buf0, buf1), (g0, g1), (w0, w1)

        def gather(j):
            return pltpu.make_async_copy(table_hbm.at[tok_v.at[j]], bufs[j % 2], gsems[j % 2])

        def scatter(j):
            return pltpu.make_async_copy(bufs[j % 2], out_hbm.at[dst_v.at[j]], wsems[j % 2])

        _sc_stream(n_chunks, gather, scatter)

    return pl.kernel(
        body,
        out_type=(jax.ShapeDtypeStruct((n_out, w), table.dtype), jax.ShapeDtypeStruct((n_pad,), jnp.int32)),
        mesh=_sc_mesh(),
        scratch_types=([pltpu.VMEM((ent_w,), jnp.int32), pltpu.VMEM((ent_w,), jnp.int32),
                        pltpu.VMEM((n_chunks, chunk), jnp.int32), pltpu.VMEM((n_chunks, chunk), jnp.int32),
                        pltpu.VMEM((LANES,), jnp.int32)] + _sc_buffers(chunk, w, table.dtype)),
        compiler_params=pltpu.CompilerParams(use_tc_tiling_on_sc=True, needs_layout_passes=False),
        name="sc_dispatch",
    )(table, codes, start_row)


def _experts_kernel(piece_start_ref, piece_row_ref, piece_cls_ref, wg_ref, wu_ref, wd_ref, xs_hbm, ys_hbm,
                    wg_bf, wu_bf, wd_bf, xbuf, ybuf, xsem, ysem):
    e = pl.program_id(0)
    g0 = piece_start_ref[e]
    n_here = piece_start_ref[e + 1] - g0
    n_total = piece_start_ref[N_EXPERTS]

    def per_class(g, fn):
        cls = piece_cls_ref[g]
        row = pl.multiple_of(piece_row_ref[g], EXP_UNIT)
        for c in range(1, EXP_CLASSES + 1):
            pl.when(cls == c)(lambda c=c: fn(c * EXP_UNIT, row))

    def x_copy(slot, rows, row):
        return pltpu.make_async_copy(xs_hbm.at[pl.ds(row, rows)], xbuf.at[slot, pl.ds(0, rows)], xsem.at[slot])

    def y_copy(slot, rows, row):
        return pltpu.make_async_copy(ybuf.at[slot, pl.ds(0, rows)], ys_hbm.at[pl.ds(row, rows)], ysem.at[slot])

    @pl.when((e == 0) & (n_total > 0))
    def _():
        per_class(0, lambda rows, row: x_copy(0, rows, row).start())

    wg_bf[...] = wg_ref[0].astype(_BF16)
    wu_bf[...] = wu_ref[0].astype(_BF16)
    wd_bf[...] = wd_ref[0].astype(_BF16)

    def piece(j, carry):
        g = g0 + j
        slot = lax.rem(g, 2)
        per_class(g, lambda rows, row: x_copy(slot, rows, row).wait())

        @pl.when(g + 1 < n_total)
        def _():
            per_class(g + 1, lambda rows, row: x_copy(1 - slot, rows, row).start())

        @pl.when(g >= 2)
        def _():
            per_class(g - 2, lambda rows, row: y_copy(slot, rows, row).wait())

        def compute(rows, row):
            x = _unpack_bf16_pair(xbuf[slot, pl.ds(0, rows)]).astype(_BF16)
            a = _dot(x, wg_bf[...])
            u = _dot(x, wu_bf[...])
            h = (a * jax.nn.sigmoid(a) * u).astype(_BF16)
            ybuf[slot, pl.ds(0, rows)] = _pack_bf16_pair(_dot(h, wd_bf[...]))
            y_copy(slot, rows, row).start()

        per_class(g, compute)
        return carry

    lax.fori_loop(0, n_here, piece, 0)

    @pl.when(e == N_EXPERTS - 1)
    def _():
        @pl.when(n_total >= 2)
        def _():
            per_class(n_total - 2, lambda rows, row: y_copy(lax.rem(n_total, 2), rows, row).wait())

        @pl.when(n_total >= 1)
        def _():
            per_class(n_total - 1, lambda rows, row: y_copy(lax.rem(n_total - 1, 2), rows, row).wait())


def _experts(piece_start, piece_row, piece_cls, n_rows, xs, w_gate, w_up, w_down):
    dh = xs.shape[1]
    d = 2 * dh
    tm = EXP_UNIT * EXP_CLASSES
    anyspec = pl.BlockSpec(memory_space=pl.ANY)
    wsel = lambda e, ps, pr, pc: (e, 0, 0)
    grid_spec = pltpu.PrefetchScalarGridSpec(
        num_scalar_prefetch=3,
        grid=(N_EXPERTS,),
        in_specs=[pl.BlockSpec((1, d, D_EXPERT), wsel), pl.BlockSpec((1, d, D_EXPERT), wsel),
                  pl.BlockSpec((1, D_EXPERT, d), wsel), anyspec],
        out_specs=anyspec,
        scratch_shapes=[pltpu.VMEM((d, D_EXPERT), _BF16), pltpu.VMEM((d, D_EXPERT), _BF16),
                        pltpu.VMEM((D_EXPERT, d), _BF16),
                        pltpu.VMEM((2, tm, dh), _U32), pltpu.VMEM((2, tm, dh), _U32),
                        pltpu.SemaphoreType.DMA((2,)), pltpu.SemaphoreType.DMA((2,))],
    )
    return pl.pallas_call(
        _experts_kernel,
        grid_spec=grid_spec,
        out_shape=jax.ShapeDtypeStruct((n_rows, dh), _U32),
        compiler_params=pltpu.CompilerParams(
            dimension_semantics=("arbitrary",), vmem_limit_bytes=VMEM_LIMIT),
        name="experts",
    )(piece_start, piece_row, piece_cls, w_gate, w_up, w_down, xs)


def _final_kernel(x1_ref, ya_ref, yb_ref, route_t_ref, gf_ref, y_ref):
    tm = x1_ref.shape[0]
    route = jnp.concatenate([route_t_ref[...], jnp.zeros((LANES - SUBLANES, tm), _F32)], axis=0).T
    x2 = (_unpack_bf16_pair(x1_ref[...]) + route[:, R_W1:R_W1 + 1] * _unpack_bf16_pair(ya_ref[...])
          + route[:, R_W2:R_W2 + 1] * _unpack_bf16_pair(yb_ref[...]))
    y_ref[...] = _rms(x2, gf_ref[...])


def _final(x1, yab, route_t, gf, n_prompt, n_sample):
    d = 2 * x1.shape[1]

    def call(tm, first_block, n_rows, name):
        tok = lambda w: pl.BlockSpec((tm, w), lambda i: (first_block + i, 0))
        sel = lambda k: pl.BlockSpec((None, tm, d // 2), lambda i: (k, first_block + i, 0))
        return pl.pallas_call(
            _final_kernel,
            grid=(n_rows // tm,),
            in_specs=[tok(d // 2), sel(0), sel(1),
                      pl.BlockSpec((SUBLANES, tm), lambda i: (0, first_block + i)),
                      pl.BlockSpec((1, d), lambda i: (0, 0))],
            out_specs=pl.BlockSpec((tm, d), lambda i: (i, 0)),
            out_shape=jax.ShapeDtypeStruct((n_rows, d), _F32),
            compiler_params=pltpu.CompilerParams(
                dimension_semantics=("arbitrary",), vmem_limit_bytes=VMEM_LIMIT),
            name=name,
        )(x1, yab, yab, route_t, gf)

    return (call(FINAL_TM, 0, n_prompt, "final_prompt"),
            call(n_sample, n_prompt // n_sample, n_sample, "final_sample"))


def _powers(lam_re, lam_im, dt):
    out = []
    for m in range(SSM_BLK + 1):
        mag = jnp.exp(m * lam_re * dt)
        ang = m * lam_im * dt
        out.append((mag * jnp.cos(ang), mag * jnp.sin(ang)))
    return out


def _spread(x, copies):
    w = x.shape[1]
    src = lax.broadcasted_iota(jnp.int32, (w, w * copies), 0)
    dst = lax.broadcasted_iota(jnp.int32, (w, w * copies), 1)
    return _dot_f32(x, jnp.where(dst % w == src, 1.0, 0.0))


def _ssm_prep_kernel(lam_ref, b_re, b_im, c_re, c_im, v_ref, r_ref, wb_ref, wc_ref, coef_ref):
    n_p, n_h = SSM_STATE, SSM_GROUP
    lr, li, dt = lam_ref[0:1, :], lam_ref[1:2, :], lam_ref[2:3, :]
    pw = _powers(lr, li, dt)
    den = lr * lr + li * li
    nr, ni = pw[1][0] - 1.0, pw[1][1]
    k_re = (nr * lr + ni * li) / den
    k_im = (ni * lr - nr * li) / den
    coef_ref[...] = jnp.concatenate(
        [pw[1][0], pw[1][1], pw[SSM_BLK][0], pw[SSM_BLK][1], jnp.zeros((SUBLANES - 4, TILE_STATE), _F32)], axis=0)

    on_diag_b = (lax.broadcasted_iota(jnp.int32, (TILE_STATE, LANES), 0) // n_p
                 == lax.broadcasted_iota(jnp.int32, (TILE_STATE, LANES), 1) // n_h)
    rows_gp = lambda ref: ref[...].reshape(TILE_STATE, n_h)
    bt_re = jnp.where(on_diag_b, _spread(rows_gp(b_re), SUBLANES), 0.0).T
    bt_im = jnp.where(on_diag_b, _spread(rows_gp(b_im), SUBLANES), 0.0).T
    bb_re = k_re * bt_re - k_im * bt_im
    bb_im = k_re * bt_im + k_im * bt_re
    wb_ref[0] = jnp.concatenate([bb_re, bb_im], axis=1)
    v_rows = []
    for s in range(SSM_BLK):
        pr, pi = pw[SSM_BLK - 1 - s]
        v_rows.append(jnp.concatenate([pr * bb_re - pi * bb_im, pr * bb_im + pi * bb_re], axis=1))
    v_ref[0] = jnp.concatenate(v_rows, axis=0).astype(v_ref.dtype)

    on_diag_c = (lax.broadcasted_iota(jnp.int32, (LANES, TILE_STATE), 0) // n_h
                 == lax.broadcasted_iota(jnp.int32, (LANES, TILE_STATE), 1) // n_p)
    rows_gh = lambda ref: ref[...].reshape(LANES, n_p)
    ct_re = jnp.where(on_diag_c, _spread(rows_gh(c_re), SUBLANES), 0.0)
    ct_im = jnp.where(on_diag_c, _spread(rows_gh(c_im), SUBLANES), 0.0)
    cl = [(ct_re * pr - ct_im * pi, ct_re * pi + ct_im * pr) for pr, pi in pw]
    wc_ref[0] = jnp.concatenate([cl[0][0], -cl[0][1]], axis=1).T
    direct = [_dot_hi(cl[m][0], bb_re, transpose_b=True) - _dot_hi(cl[m][1], bb_im, transpose_b=True)
              for m in range(SSM_BLK)]
    zero = jnp.zeros((LANES, LANES), _F32)
    rt = jnp.concatenate(
        [jnp.concatenate([cl[i + 1][0], -cl[i + 1][1]]
                         + [direct[i - s] if s <= i else zero for s in range(SSM_BLK)], axis=1)
         for i in range(SSM_BLK)], axis=0)
    r_ref[0] = rt.T.astype(r_ref.dtype)


def _ssm_params(lam_re, lam_im, log_dt, b_re, b_im, c_re, c_im, d_skip):
    n_g, n_p, n_h = N_SSM_GROUPS, SSM_STATE, SSM_GROUP
    dt = jnp.repeat(jnp.exp(log_dt), n_p)
    lam = jnp.zeros((SUBLANES, STATE_COLS), _F32).at[0].set(lam_re.reshape(-1)).at[1].set(
        lam_im.reshape(-1)).at[2].set(dt)
    groups = lambda r, c: pl.BlockSpec((SUBLANES, r, c), lambda k: (k, 0, 0))
    out3 = lambda rows, w: pl.BlockSpec((1, rows, w), lambda k: (k, 0, 0))
    cols = pl.BlockSpec((SUBLANES, TILE_STATE), lambda k: (0, k))
    k_blk = SSM_BLK * LANES
    v, r, wb, wc, coef = pl.pallas_call(
        _ssm_prep_kernel,
        grid=(N_LANE_TILES,),
        in_specs=[cols, groups(n_p, n_h), groups(n_p, n_h), groups(n_h, n_p), groups(n_h, n_p)],
        out_specs=[out3(k_blk, 2 * TILE_STATE), out3(2 * TILE_STATE + k_blk, k_blk),
                   out3(LANES, 2 * TILE_STATE), out3(2 * TILE_STATE, LANES), cols],
        out_shape=[jax.ShapeDtypeStruct((N_LANE_TILES, k_blk, 2 * TILE_STATE), _BF16),
                   jax.ShapeDtypeStruct((N_LANE_TILES, 2 * TILE_STATE + k_blk, k_blk), _BF16),
                   jax.ShapeDtypeStruct((N_LANE_TILES, LANES, 2 * TILE_STATE), _F32),
                   jax.ShapeDtypeStruct((N_LANE_TILES, 2 * TILE_STATE, LANES), _F32),
                   jax.ShapeDtypeStruct((SUBLANES, STATE_COLS), _F32)],
        compiler_params=pltpu.CompilerParams(
            dimension_semantics=("arbitrary",), vmem_limit_bytes=VMEM_LIMIT),
        name="ssm_prep",
    )(lam, b_re, b_im, c_re, c_im)
    return wb, wc, v, r, coef, d_skip.reshape(1, D_SSM)


def _dispatch_plan(route_t, cnt):
    t_all = route_t.shape[1]
    codes = route_t[R_CODE1:R_CODE2 + 1].astype(jnp.int32).reshape(-1)
    per_pass = SC_WORKERS * DISPATCH_CHUNK
    codes = jnp.pad(codes, (0, -(2 * t_all) % per_pass))
    counts = cnt[:, 0].astype(jnp.int32)
    zero = jnp.zeros((1,), jnp.int32)
    units = (counts + EXP_UNIT - 1) // EXP_UNIT
    unit_start = jnp.concatenate([zero, jnp.cumsum(units)])
    start_row = jnp.zeros((LANES,), jnp.int32).at[:N_EXPERTS].set(unit_start[:N_EXPERTS] * EXP_UNIT)
    pieces = (units + EXP_CLASSES - 1) // EXP_CLASSES
    piece_start = jnp.concatenate([zero, jnp.cumsum(pieces)])
    tm = EXP_UNIT * EXP_CLASSES
    max_units = (2 * t_all + N_EXPERTS * (EXP_UNIT - 1)) // EXP_UNIT
    max_pieces = (max_units + N_EXPERTS * (EXP_CLASSES - 1)) // EXP_CLASSES
    g = jnp.arange(max_pieces, dtype=jnp.int32)
    owner = ((g[:, None] >= piece_start[None, :-1]) & (g[:, None] < piece_start[None, 1:])).astype(jnp.int32)
    pick = lambda table: jnp.sum(owner * table[None, :], axis=1)
    first_unit = pick(unit_start[:-1]) + (g - pick(piece_start[:-1])) * EXP_CLASSES
    piece_row = first_unit * EXP_UNIT
    piece_cls = jnp.clip(pick(unit_start[1:]) - first_unit, 1, EXP_CLASSES)
    n_rows = (max_units * EXP_UNIT + tm - 1) // tm * tm + tm
    return codes, start_row, n_rows, piece_start, piece_row, piece_cls


def kernel(x_prompt, x_sample, state_ssm_re, state_ssm_im, norm1_g, w_in, lam_re, lam_im, log_dt, ssm_b_re, ssm_b_im, ssm_c_re, ssm_c_im, ssm_d, gmlp_norm_g, gmlp_w_s, gmlp_b_s, out_norm_ssm_g, out_norm_gmlp_g, w_out, norm2_g, w_router_group, b_router_group, w_router_expert, b_router_expert, w_gate, w_up, w_down, final_norm_g):
    n, l, d = x_prompt.shape
    ns = x_sample.shape[0]
    t_all = n * l + ns
    li = 0
    g1 = norm1_g[li].reshape(1, d)
    gn = gmlp_norm_g[li].reshape(1, D_GMLP)
    tril = jnp.tril(jnp.ones((CHUNK, CHUNK), dtype=bool))
    ws_tril = jnp.where(tril[None], gmlp_w_s[li], 0.0)
    bs = gmlp_b_s[li]
    gog = out_norm_gmlp_g[li].reshape(1, D_GMLP)
    gos = out_norm_ssm_g[li].reshape(1, D_SSM)
    wb, wc, v_blk, r_blk, coef, dsk = _ssm_params(
        lam_re[li], lam_im[li], log_dt[li], ssm_b_re[li], ssm_b_im[li], ssm_c_re[li], ssm_c_im[li], ssm_d[li])
    g2 = norm2_g[li].reshape(1, d)
    pad = LANES - N_EXPERTS - N_EXPERT_GROUPS
    wr = jnp.concatenate([w_router_expert[li], w_router_group[li], jnp.zeros((d, pad), _F32)], axis=1)
    br = jnp.concatenate([b_router_expert[li], b_router_group[li], jnp.zeros((pad,), _F32)]).reshape(1, LANES)

    xa, sg, mixb = _front_prompt(x_prompt, g1, w_in[li], gn, ws_tril.astype(_BF16), bs.T, gog)
    mixa, hfin = _ssm_prompt(xa, sg, v_blk, r_blk, coef, dsk, gos)
    w00 = jnp.repeat(ws_tril[:, 0, 0], GMLP_HEAD).reshape(1, D_GMLP)
    b0 = jnp.repeat(bs[:, 0], GMLP_HEAD).reshape(1, D_GMLP)
    mix_s, hr_s, hi_s, vrow = _front_sample(
        x_sample.reshape(ns, d), g1, w_in[li], gn, w00, b0, gog, wb, wc, coef, dsk, gos,
        state_ssm_re[li].reshape(ns, STATE_COLS), state_ssm_im[li].reshape(ns, STATE_COLS))

    x1, xn, route_t, cnt = _mixer_out(x_prompt, mixa, mixb, x_sample.reshape(ns, d), mix_s,
                                      w_out[li], g2, wr, br)
    codes, start_row, n_rows, piece_start, piece_row, piece_cls = _dispatch_plan(route_t, cnt)
    xs, dest = _sc_dispatch(xn, codes, start_row, n_rows, DISPATCH_CHUNK)
    ys = _experts(piece_start, piece_row, piece_cls, n_rows, xs, w_gate[li], w_up[li], w_down[li])
    yab = _sc_combine(ys, dest, 2 * t_all, COMBINE_CHUNK).reshape(2, t_all, d // 2)
    y_p, y_s = _final(x1, yab, route_t, final_norm_g.reshape(1, d), n * l, ns)

    hf = hfin.reshape(n, N_LANE_TILES, 2, 8, SSM_STATE)
    re_p = hf[:, :, 0].reshape(1, n, N_SSM_GROUPS, SSM_STATE)
    im_p = hf[:, :, 1].reshape(1, n, N_SSM_GROUPS, SSM_STATE)
    re_s = hr_s.reshape(1, ns, N_SSM_GROUPS, SSM_STATE)
    im_s = hi_s.reshape(1, ns, N_SSM_GROUPS, SSM_STATE)
    return (y_p.reshape(n, l, d), y_s.reshape(ns, 1, d), re_p, im_p, re_s, im_s,
            vrow.reshape(1, ns, 1, D_GMLP))
```

```python
import math

import jax
import jax.numpy as jnp
from jax import lax
from jax.experimental import pallas as pl
from jax.experimental.pallas import tpu as pltpu
from jax.experimental.pallas import tpu_sc as plsc

D_MODEL = 1024
D_SSM = 512
D_GMLP = 512
SSM_GROUP = 16
N_SSM_GROUPS = 32
SSM_STATE = 64
CHUNK = 128
N_GMLP_HEADS = 4
GMLP_HEAD = 128
N_EXPERT_GROUPS = 4
EXPERTS_PER_GROUP = 8
N_EXPERTS = 32
D_EXPERT = 512
D_IN = 2048
EPS = 1e-6

LANES = 128
SUBLANES = 8
N_LANE_TILES = D_SSM // LANES
STATE_COLS = N_SSM_GROUPS * SSM_STATE
TILE_STATE = STATE_COLS // N_LANE_TILES
VMEM_LIMIT = 56 * 1024 * 1024

SC_CORES = 2
SC_SUBCORES = 16
SC_LANES = 16
SC_WORKERS = SC_CORES * SC_SUBCORES

FRONT_TL = 512
SSM_LC = 256
SSM_BLK = 4
COEF_LB_RE, COEF_LB_IM, COEF_LBLK_RE, COEF_LBLK_IM = 0, 1, 2, 3
TOK_TM = 512
FINAL_TM = 1024
EXP_UNIT = 128
EXP_CLASSES = 8
DISPATCH_CHUNK = 80
COMBINE_CHUNK = 24

R_E1, R_E2, R_W1, R_W2, R_RANK1, R_RANK2, R_CODE1, R_CODE2 = 0, 1, 2, 3, 4, 5, 6, 7
CODE_BITS = 16
CODE_SHIFT = float(1 << CODE_BITS)

_INV_SQRT2 = 1.0 / math.sqrt(2.0)
_BF16 = jnp.bfloat16
_F32 = jnp.float32
_U32 = jnp.uint32


def _gelu(x):
    return 0.5 * x * (1.0 + lax.erf(x * _INV_SQRT2))


def _rms(x, g):
    return x * lax.rsqrt(jnp.mean(x * x, axis=-1, keepdims=True) + EPS) * g


def _dot(a, b):
    return jnp.dot(a, b, preferred_element_type=_F32)


def _dot_f32(a, b):
    return jnp.dot(a, b, preferred_element_type=_F32, precision=lax.Precision.HIGHEST)


def _dot_hi(a, b, transpose_b=False):
    def split(x):
        hi = x.astype(_BF16)
        return hi, (x - hi.astype(_F32)).astype(_BF16)

    dims = (((1,), (1 if transpose_b else 0,)), ((), ()))
    dot = lambda u, v: lax.dot_general(u, v, dims, preferred_element_type=_F32)
    a_hi, a_lo = split(a)
    b_hi, b_lo = split(b)
    return dot(a_hi, b_hi) + dot(a_hi, b_lo) + dot(a_lo, b_hi)


def _pack_bf16_pair(x):
    w = x.shape[1] // 2
    hi = lax.bitcast_convert_type(x[:, :w].astype(_BF16).astype(_F32), _U32)
    lo = lax.bitcast_convert_type(x[:, w:].astype(_BF16).astype(_F32), _U32)
    return hi | (lo >> 16)


def _unpack_bf16_pair(p):
    hi = lax.bitcast_convert_type(p & jnp.uint32(0xFFFF0000), _F32)
    lo = lax.bitcast_convert_type(p << 16, _F32)
    return jnp.concatenate([hi, lo], axis=-1)


def _head_norm_gelu(vb, gn):
    v = _gelu(vb)
    parts = []
    for h in range(N_GMLP_HEADS):
        vh = v[:, h * GMLP_HEAD:(h + 1) * GMLP_HEAD]
        parts.append(vh * lax.rsqrt(jnp.mean(vh * vh, axis=-1, keepdims=True) + EPS))
    return jnp.concatenate(parts, axis=-1) * gn


def _front_prompt_kernel(x_ref, g1_ref, win_ref, gn_ref, ws_ref, bs_ref, gog_ref,
                         xa_ref, sg_ref, mixb_ref, win_bf, z_ref):
    @pl.when(pl.program_id(0) == 0)
    def _():
        win_bf[...] = win_ref[...].astype(_BF16)
        z_ref[...] = jnp.zeros_like(z_ref)

    z = z_ref[...]
    x = x_ref[0]
    hn = _rms(x, g1_ref[...]).astype(_BF16)
    z_ref[...] = _dot(hn, win_bf[...])
    xa_ref[0] = z[:, :D_SSM]
    sg_ref[0] = jax.nn.sigmoid(z[:, D_SSM:2 * D_SSM])
    ub = _gelu(z[:, 2 * D_SSM:2 * D_SSM + D_GMLP])
    vbn = _head_norm_gelu(z[:, 2 * D_SSM + D_GMLP:], gn_ref[...]).astype(_BF16)
    tl = x.shape[0]
    rows = []
    for c in range(tl // CHUNK):
        heads = []
        for h in range(N_GMLP_HEADS):
            vh = vbn[c * CHUNK:(c + 1) * CHUNK, h * GMLP_HEAD:(h + 1) * GMLP_HEAD]
            heads.append(_dot(ws_ref[h], vh) + bs_ref[:, h:h + 1])
        rows.append(jnp.concatenate(heads, axis=-1))
    s = jnp.concatenate(rows, axis=0)
    mixb_ref[0] = _rms(ub * s, gog_ref[...]).astype(_BF16)


def _front_prompt(x, g1, win, gn, ws_tril_bf, bs_t, gog):
    n, l, d = x.shape
    tl = FRONT_TL
    per_seq = l // tl
    n_tiles = n * per_seq
    cur = lambda i: jnp.minimum(i, n_tiles - 1)
    prev = lambda i: jnp.maximum(i - 1, 0)
    const = lambda *shape: pl.BlockSpec(shape, lambda i: (0,) * len(shape))
    seq = lambda w, which: pl.BlockSpec((1, tl, w), lambda i: (which(i) // per_seq, which(i) % per_seq, 0))
    return pl.pallas_call(
        _front_prompt_kernel,
        grid=(n_tiles + 1,),
        in_specs=[seq(d, cur), const(1, d), const(d, D_IN), const(1, D_GMLP),
                  const(N_GMLP_HEADS, CHUNK, CHUNK), const(CHUNK, N_GMLP_HEADS), const(1, D_GMLP)],
        out_specs=[seq(D_SSM, prev), seq(D_SSM, prev), seq(D_GMLP, prev)],
        out_shape=[jax.ShapeDtypeStruct((n, l, D_SSM), _F32),
                   jax.ShapeDtypeStruct((n, l, D_SSM), _F32),
                   jax.ShapeDtypeStruct((n, l, D_GMLP), _BF16)],
        scratch_shapes=[pltpu.VMEM((d, D_IN), _BF16), pltpu.VMEM((tl, D_IN), _F32)],
        compiler_params=pltpu.CompilerParams(
            dimension_semantics=("arbitrary",), vmem_limit_bytes=VMEM_LIMIT),
        name="front_prompt",
    )(x, g1, win, gn, ws_tril_bf, bs_t, gog)


def _ssm_prompt_kernel(xa_ref, sg_ref, v_ref, r_ref, coef_ref, dsk_ref, gos_ref,
                       mixa_ref, hfin_ref, s_ref, st_ref):
    lc = xa_ref.shape[1]
    nblk = lc // SSM_BLK
    rows = nblk * SUBLANES

    @pl.when(pl.program_id(0) == 0)
    def _():
        st_ref[...] = jnp.zeros_like(st_ref)

    def by_position(ref):
        t = pltpu.einshape("btc->tbc", ref[...]).reshape(nblk, SSM_BLK, SUBLANES, D_SSM)
        return [t[:, i].reshape(rows, D_SSM) for i in range(SSM_BLK)]

    xs = by_position(xa_ref)
    xs_bf = [x.astype(_BF16) for x in xs]
    xk = [jnp.concatenate([x[:, k * LANES:(k + 1) * LANES] for x in xs_bf], axis=-1)
          for k in range(N_LANE_TILES)]
    for k in range(N_LANE_TILES):
        s_ref[:, 2 * TILE_STATE * k:2 * TILE_STATE * (k + 1)] = _dot(xk[k], v_ref[k])

    for kk in range(0, N_LANE_TILES, 2):
        tiles = (kk, kk + 1)
        cols = [(2 * TILE_STATE * k, 2 * TILE_STATE * k + TILE_STATE) for k in tiles]
        lbs = [tuple(jnp.broadcast_to(coef_ref[row:row + 1, k * TILE_STATE:(k + 1) * TILE_STATE],
                                      (SUBLANES, TILE_STATE)) for row in (COEF_LBLK_RE, COEF_LBLK_IM))
               for k in tiles]

        def body(j, carry, cols=cols, lbs=lbs):
            r0 = pl.multiple_of(j * SUBLANES, SUBLANES)
            out = []
            for q, ((c_re, c_im), (lr, li)) in enumerate(zip(cols, lbs)):
                hr, hi = carry[2 * q], carry[2 * q + 1]
                sr = s_ref[pl.ds(r0, SUBLANES), c_re:c_re + TILE_STATE]
                si = s_ref[pl.ds(r0, SUBLANES), c_im:c_im + TILE_STATE]
                s_ref[pl.ds(r0, SUBLANES), c_re:c_re + TILE_STATE] = hr
                s_ref[pl.ds(r0, SUBLANES), c_im:c_im + TILE_STATE] = hi
                out += [lr * hr - li * hi + sr, lr * hi + li * hr + si]
            return tuple(out)

        init = tuple(st_ref[:, c:c + TILE_STATE] for c_pair in cols for c in c_pair)
        fin = lax.fori_loop(0, nblk, body, init, unroll=2)
        for q, (c_re, c_im) in enumerate(cols):
            st_ref[:, c_re:c_re + TILE_STATE] = fin[2 * q]
            st_ref[:, c_im:c_im + TILE_STATE] = fin[2 * q + 1]

    yk = []
    for k in range(N_LANE_TILES):
        h_in = s_ref[:, 2 * TILE_STATE * k:2 * TILE_STATE * (k + 1)].astype(_BF16)
        yk.append(_dot(jnp.concatenate([h_in, xk[k]], axis=-1), r_ref[k]))
    ys = []
    for i in range(SSM_BLK):
        y = jnp.concatenate([y_k[:, i * LANES:(i + 1) * LANES] for y_k in yk], axis=-1) + dsk_ref[...] * xs[i]
        ys.append(y.reshape(nblk, SUBLANES, D_SSM))
    y = pltpu.einshape("tbc->btc", jnp.stack(ys, axis=1).reshape(lc, SUBLANES, D_SSM))
    mixa_ref[...] = _rms(_gelu(y) * sg_ref[...], gos_ref[...]).astype(_BF16)
    hfin_ref[...] = st_ref[...]


def _ssm_prompt(xa, sg, v, r, coef, dsk, gos):
    n, l, _ = xa.shape
    lc = SSM_LC
    const = lambda *shape: pl.BlockSpec(shape, lambda i: (0,) * len(shape))
    seq_spec = pl.BlockSpec((n, lc, D_SSM), lambda i: (0, i, 0))
    return pl.pallas_call(
        _ssm_prompt_kernel,
        grid=(l // lc,),
        in_specs=[seq_spec, seq_spec, const(*v.shape), const(*r.shape),
                  const(*coef.shape), const(1, D_SSM), const(1, D_SSM)],
        out_specs=[seq_spec, const(n, 2 * STATE_COLS)],
        out_shape=[jax.ShapeDtypeStruct((n, l, D_SSM), _BF16),
                   jax.ShapeDtypeStruct((n, 2 * STATE_COLS), _F32)],
        scratch_shapes=[pltpu.VMEM((lc // SSM_BLK * n, 2 * STATE_COLS), _F32),
                        pltpu.VMEM((n, 2 * STATE_COLS), _F32)],
        compiler_params=pltpu.CompilerParams(
            dimension_semantics=("arbitrary",), vmem_limit_bytes=VMEM_LIMIT),
        name="ssm_prompt",
    )(xa, sg, v, r, coef, dsk, gos)


def _front_sample_kernel(x_ref, g1_ref, win_ref, gn_ref, w00_ref, b0_ref, gog_ref,
                         wb_ref, wc_ref, coef_ref, dsk_ref, gos_ref, h0r_ref, h0i_ref,
                         mix_ref, hr_ref, hi_ref, vrow_ref):
    x = x_ref[...]
    hn = _rms(x, g1_ref[...])
    z = _dot_hi(hn, win_ref[...])
    xa = z[:, :D_SSM]
    ys = []
    for k in range(N_LANE_TILES):
        bu = _dot_hi(xa[:, k * LANES:(k + 1) * LANES], wb_ref[k])
        sl = slice(k * TILE_STATE, (k + 1) * TILE_STATE)
        lr, li = coef_ref[COEF_LB_RE:COEF_LB_RE + 1, sl], coef_ref[COEF_LB_IM:COEF_LB_IM + 1, sl]
        h0r, h0i = h0r_ref[:, sl], h0i_ref[:, sl]
        nr = lr * h0r - li * h0i + bu[:, :TILE_STATE]
        ni = lr * h0i + li * h0r + bu[:, TILE_STATE:]
        hr_ref[:, sl] = nr
        hi_ref[:, sl] = ni
        ys.append(_dot_hi(jnp.concatenate([nr, ni], axis=-1), wc_ref[k]))
    y = jnp.concatenate(ys, axis=-1) + dsk_ref[...] * xa
    ya = _gelu(y) * jax.nn.sigmoid(z[:, D_SSM:2 * D_SSM])
    mix_ref[:, :D_SSM] = _rms(ya, gos_ref[...])
    ub = _gelu(z[:, 2 * D_SSM:2 * D_SSM + D_GMLP])
    vbn = _head_norm_gelu(z[:, 2 * D_SSM + D_GMLP:], gn_ref[...])
    vrow_ref[...] = vbn
    s = w00_ref[...] * vbn + b0_ref[...]
    mix_ref[:, D_SSM:] = _rms(ub * s, gog_ref[...])


def _front_sample(x, g1, win, gn, w00, b0, gog, wb, wc, coef, dsk, gos, h0r, h0i):
    n = x.shape[0]
    vmem = pl.BlockSpec(memory_space=pltpu.VMEM)
    return pl.pallas_call(
        _front_sample_kernel,
        in_specs=[vmem] * 14,
        out_specs=[vmem] * 4,
        out_shape=[jax.ShapeDtypeStruct((n, D_MODEL), _F32),
                   jax.ShapeDtypeStruct((n, STATE_COLS), _F32),
                   jax.ShapeDtypeStruct((n, STATE_COLS), _F32),
                   jax.ShapeDtypeStruct((n, D_GMLP), _F32)],
        compiler_params=pltpu.CompilerParams(vmem_limit_bytes=VMEM_LIMIT),
        name="front_sample",
    )(x, g1, win, gn, w00, b0, gog, wb, wc, coef, dsk, gos, h0r, h0i)


def _route(logits, base):
    tm = logits.shape[0]
    lt = logits.T
    ex = lt[:N_EXPERTS, :]
    gr = lt[N_EXPERTS:N_EXPERTS + SUBLANES, :]
    row_e = lax.broadcasted_iota(jnp.int32, ex.shape, 0).astype(_F32)
    row_g = lax.broadcasted_iota(jnp.int32, gr.shape, 0).astype(_F32)
    neg = jnp.float32(-jnp.inf)
    big = jnp.float32(LANES)
    is_g = row_g < N_EXPERT_GROUPS
    gl = jnp.where(is_g, gr, neg)
    gmax = jnp.max(gl, axis=0, keepdims=True)
    gi = jnp.min(jnp.where(gl == gmax, row_g, big), axis=0, keepdims=True)
    p_top = 1.0 / jnp.sum(jnp.where(is_g, jnp.exp(gl - gmax), 0.0), axis=0, keepdims=True)
    lo = gi * EXPERTS_PER_GROUP
    in_grp = (row_e >= lo) & (row_e < lo + EXPERTS_PER_GROUP)
    m1 = jnp.max(jnp.where(in_grp, ex, neg), axis=0, keepdims=True)
    i1 = jnp.min(jnp.where(in_grp & (ex == m1), row_e, big), axis=0, keepdims=True)
    rest = in_grp & (row_e != i1)
    m2 = jnp.max(jnp.where(rest, ex, neg), axis=0, keepdims=True)
    i2 = jnp.min(jnp.where(rest & (ex == m2), row_e, big), axis=0, keepdims=True)
    e2 = jnp.exp(m2 - m1)
    w1 = p_top / (1.0 + e2)
    w2 = p_top * e2 / (1.0 + e2)
    sel1 = row_e == i1
    sel2 = row_e == i2
    hits = jnp.where(sel1 | sel2, 1.0, 0.0)
    src = lax.broadcasted_iota(jnp.int32, (tm, tm), 0)
    dst = lax.broadcasted_iota(jnp.int32, (tm, tm), 1)
    before = _dot(hits.astype(_BF16), jnp.where(src < dst, 1.0, 0.0).astype(_BF16)) + base
    rank1 = jnp.sum(jnp.where(sel1, before, 0.0), axis=0, keepdims=True)
    rank2 = jnp.sum(jnp.where(sel2, before, 0.0), axis=0, keepdims=True)
    fields = {R_E1: i1, R_E2: i2, R_W1: w1, R_W2: w2, R_RANK1: rank1, R_RANK2: rank2,
              R_CODE1: i1 * CODE_SHIFT + rank1, R_CODE2: i2 * CODE_SHIFT + rank2}
    row8 = lax.broadcasted_iota(jnp.int32, (SUBLANES, tm), 0)
    route_t = jnp.zeros((SUBLANES, tm), _F32)
    for r, val in fields.items():
        route_t = jnp.where(row8 == r, val, route_t)
    return route_t, base + jnp.sum(hits, axis=1, keepdims=True)


def _mixer_out_prompt_kernel(x_ref, mixa_ref, mixb_ref, wo_ref, g2_ref, wr_ref, br_ref,
                             x1_ref, xn_ref, route_t_ref, cnt_ref, base_ref, logits_ref, wo_bf):
    i = pl.program_id(0)

    @pl.when(i == 0)
    def _():
        base_ref[...] = jnp.zeros_like(base_ref)
        logits_ref[...] = jnp.zeros_like(logits_ref)
        wo_bf[...] = wo_ref[...].astype(_BF16)

    prev_logits = logits_ref[...]
    x1 = x_ref[0] + _dot(mixa_ref[0], wo_bf[:D_SSM, :]) + _dot(mixb_ref[0], wo_bf[D_SSM:, :])
    xn = _rms(x1, g2_ref[...])
    x1_ref[...] = _pack_bf16_pair(x1)
    xn_ref[...] = _pack_bf16_pair(xn)
    logits_ref[...] = _dot(xn.astype(_BF16), wr_ref[...]) + br_ref[...]
    route_t, base = _route(prev_logits, base_ref[...])
    route_t_ref[...] = route_t
    base = jnp.where(i >= 1, base, base_ref[...])
    base_ref[...] = base
    cnt_ref[...] = base


def _mixer_out_sample_kernel(x_ref, mix_ref, wo_ref, g2_ref, wr_ref, br_ref, cnt_in_ref,
                             x1_in, xn_in, route_t_in,
                             x1_ref, xn_ref, route_t_ref, cnt_ref):
    del x1_in, xn_in, route_t_in
    x1 = (x_ref[...] + _dot_hi(mix_ref[:, :D_SSM], wo_ref[:D_SSM, :])
          + _dot_hi(mix_ref[:, D_SSM:], wo_ref[D_SSM:, :]))
    xn = _rms(x1, g2_ref[...])
    logits = _dot_hi(xn, wr_ref[...]) + br_ref[...]
    route_t, base = _route(logits, cnt_in_ref[...])
    x1_ref[...] = _pack_bf16_pair(x1)
    xn_ref[...] = _pack_bf16_pair(xn)
    route_t_ref[...] = route_t
    cnt_ref[...] = base


def _mixer_out(x_p, mixa, mixb, x_s, mix_s, wo, g2, wr, br):
    n, l, d = x_p.shape
    ns = x_s.shape[0]
    t_all = n * l + ns
    tm = TOK_TM
    per_seq = l // tm
    n_tiles = n * per_seq
    cur = lambda i: jnp.minimum(i, n_tiles - 1)
    prev = lambda i: jnp.maximum(i - 1, 0)
    const = lambda *shape: pl.BlockSpec(shape, lambda i: (0,) * len(shape))
    seq = lambda w: pl.BlockSpec((1, tm, w), lambda i: (cur(i) // per_seq, cur(i) % per_seq, 0))
    tok = lambda w, which: pl.BlockSpec((tm, w), lambda i: (which(i), 0))
    tok_shapes = [jax.ShapeDtypeStruct((t_all, d // 2), _U32),
                  jax.ShapeDtypeStruct((t_all, d // 2), _U32),
                  jax.ShapeDtypeStruct((SUBLANES, t_all), _F32)]
    cnt_shape = jax.ShapeDtypeStruct((N_EXPERTS, 1), _F32)
    x1, xn, route_t, cnt = pl.pallas_call(
        _mixer_out_prompt_kernel,
        grid=(n_tiles + 1,),
        in_specs=[seq(d), seq(D_SSM), seq(D_GMLP),
                  const(d, d), const(1, d), const(d, LANES), const(1, LANES)],
        out_specs=[tok(d // 2, cur), tok(d // 2, cur),
                   pl.BlockSpec((SUBLANES, tm), lambda i: (0, prev(i))), const(N_EXPERTS, 1)],
        out_shape=tok_shapes + [cnt_shape],
        scratch_shapes=[pltpu.VMEM((N_EXPERTS, 1), _F32), pltpu.VMEM((tm, LANES), _F32),
                        pltpu.VMEM((d, d), _BF16)],
        compiler_params=pltpu.CompilerParams(
            dimension_semantics=("arbitrary",), vmem_limit_bytes=VMEM_LIMIT),
        name="mixer_out_prompt",
    )(x_p, mixa, mixb, wo, g2, wr.astype(_BF16), br)
    tail = (n * l) // ns
    c1 = lambda *shape: pl.BlockSpec(shape, lambda i: (0,) * len(shape))
    anyspec = pl.BlockSpec(memory_space=pl.ANY)
    tail_spec = lambda w: pl.BlockSpec((ns, w), lambda i: (tail, 0))
    return pl.pallas_call(
        _mixer_out_sample_kernel,
        grid=(1,),
        in_specs=[c1(ns, d), c1(ns, d), c1(d, d), c1(1, d), c1(d, LANES), c1(1, LANES), c1(N_EXPERTS, 1),
                  anyspec, anyspec, anyspec],
        out_specs=[tail_spec(d // 2), tail_spec(d // 2),
                   pl.BlockSpec((SUBLANES, ns), lambda i: (0, tail)), c1(N_EXPERTS, 1)],
        out_shape=tok_shapes + [cnt_shape],
        input_output_aliases={7: 0, 8: 1, 9: 2},
        compiler_params=pltpu.CompilerParams(
            dimension_semantics=("arbitrary",), vmem_limit_bytes=VMEM_LIMIT),
        name="mixer_out_sample",
    )(x_s, mix_s, wo, g2, wr, br, cnt, x1, xn, route_t)


def _sc_stream(n_chunks, gather, write):
    gather(0).start()
    for j in range(n_chunks):
        if j + 1 < n_chunks:
            if j >= 1:
                write(j - 1).wait()
            gather(j + 1).start()
        gather(j).wait()
        write(j).start()
    if n_chunks >= 2:
        write(n_chunks - 2).wait()
    write(n_chunks - 1).wait()


def _sc_mesh():
    return plsc.VectorSubcoreMesh(core_axis_name="c", subcore_axis_name="s",
                                  num_cores=SC_CORES, num_subcores=SC_SUBCORES)


def _sc_buffers(chunk, w, dtype):
    return [pltpu.VMEM((chunk, w), dtype), pltpu.VMEM((chunk, w), dtype)] + [pltpu.SemaphoreType.DMA] * 4


def _sc_combine(table, idx, n_out, chunk):
    w = table.shape[1]
    rows_w = n_out // SC_WORKERS
    n_chunks = rows_w // chunk
    assert rows_w * SC_WORKERS == n_out and n_chunks * chunk == rows_w and rows_w % SUBLANES == 0

    def body(table_hbm, idx_hbm, out_hbm, idx_v, buf0, buf1, g0, g1, w0, w1):
        wid = lax.axis_index("s") * SC_CORES + lax.axis_index("c")
        base = pl.multiple_of(wid * rows_w, SUBLANES)
        pltpu.sync_copy(idx_hbm.at[pl.ds(base, rows_w)], idx_v)
        bufs, gsems, wsems = (buf0, buf1), (g0, g1), (w0, w1)

        def gather(j):
            return pltpu.make_async_copy(table_hbm.at[idx_v.at[pl.ds(j * chunk, chunk)]], bufs[j % 2], gsems[j % 2])

        def write(j):
            return pltpu.make_async_copy(bufs[j % 2], out_hbm.at[pl.ds(base + j * chunk, chunk)], wsems[j % 2])

        _sc_stream(n_chunks, gather, write)

    return pl.kernel(
        body,
        out_type=jax.ShapeDtypeStruct((n_out, w), table.dtype),
        mesh=_sc_mesh(),
        scratch_types=[pltpu.VMEM((rows_w,), jnp.int32)] + _sc_buffers(chunk, w, table.dtype),
        compiler_params=pltpu.CompilerParams(use_tc_tiling_on_sc=True),
        name="sc_combine",
    )(table, idx)


def _sc_dispatch(table, codes, start_row, n_out, chunk):
    t_all, w = table.shape
    n_pad = codes.shape[0]
    n_ent = 2 * t_all
    ent_w = n_pad // SC_WORKERS
    n_chunks = ent_w // chunk
    per_chunk = chunk // SC_LANES
    trash = n_out - (n_pad - n_ent)
    assert ent_w * SC_WORKERS == n_pad and n_chunks * chunk == ent_w
    assert per_chunk * SC_LANES == chunk and chunk <= LANES and n_pad - n_ent <= t_all

    def body(table_hbm, code_hbm, start_hbm, out_hbm, dest_hbm,
             code_v, dest_v, tok_v, dst_v, start_v, buf0, buf1, g0, g1, w0, w1):
        wid = lax.axis_index("s") * SC_CORES + lax.axis_index("c")
        ebase = pl.multiple_of(wid * ent_w, SUBLANES)
        pltpu.sync_copy(code_hbm.at[pl.ds(ebase, ent_w)], code_v)
        pltpu.sync_copy(start_hbm, start_v)
        lane = lax.iota(jnp.int32, SC_LANES)
        for j in range(n_chunks):
            for c in range(per_chunk):
                off = j * chunk + c * SC_LANES
                ent = ebase + off + lane
                code = code_v[pl.ds(off, SC_LANES)]
                d = plsc.load_gather(start_v, [code >> CODE_BITS]) + (code & ((1 << CODE_BITS) - 1))
                d = jnp.where(ent >= n_ent, trash + (ent - n_ent), d)
                tok = jnp.where(ent >= t_all, ent - t_all, ent)
                tok = jnp.where(tok >= t_all, tok - t_all, tok)
                dest_v[pl.ds(off, SC_LANES)] = d
                dst_v[j, pl.ds(c * SC_LANES, SC_LANES)] = d
                tok_v[j, pl.ds(c * SC_LANES, SC_LANES)] = tok
        pltpu.sync_copy(dest_v, dest_hbm.at[pl.ds(ebase, ent_w)])
        bufs, gsems, wsems = (buf0, buf1), (g0, g1), (w0, w1)

        def gather(j):
            return pltpu.make_async_copy(table_hbm.at[tok_v.at[j]], bufs[j % 2], gsems[j % 2])

        def scatter(j):
            return pltpu.make_async_copy(bufs[j % 2], out_hbm.at[dst_v.at[j]], wsems[j % 2])

        _sc_stream(n_chunks, gather, scatter)

    return pl.kernel(
        body,
        out_type=(jax.ShapeDtypeStruct((n_out, w), table.dtype), jax.ShapeDtypeStruct((n_pad,), jnp.int32)),
        mesh=_sc_mesh(),
        scratch_types=([pltpu.VMEM((ent_w,), jnp.int32), pltpu.VMEM((ent_w,), jnp.int32),
                        pltpu.VMEM((n_chunks, chunk), jnp.int32), pltpu.VMEM((n_chunks, chunk), jnp.int32),
                        pltpu.VMEM((LANES,), jnp.int32)] + _sc_buffers(chunk, w, table.dtype)),
        compiler_params=pltpu.CompilerParams(use_tc_tiling_on_sc=True, needs_layout_passes=False),
        name="sc_dispatch",
    )(table, codes, start_row)


def _experts_kernel(piece_start_ref, piece_row_ref, piece_cls_ref, wg_ref, wu_ref, wd_ref, xs_hbm, ys_hbm,
                    wg_bf, wu_bf, wd_bf, xbuf, ybuf, xsem, ysem):
    e = pl.program_id(0)
    g0 = piece_start_ref[e]
    n_here = piece_start_ref[e + 1] - g0
    n_total = piece_start_ref[N_EXPERTS]

    def per_class(g, fn):
        cls = piece_cls_ref[g]
        row = pl.multiple_of(piece_row_ref[g], EXP_UNIT)
        for c in range(1, EXP_CLASSES + 1):
            pl.when(cls == c)(lambda c=c: fn(c * EXP_UNIT, row))

    def x_copy(slot, rows, row):
        return pltpu.make_async_copy(xs_hbm.at[pl.ds(row, rows)], xbuf.at[slot, pl.ds(0, rows)], xsem.at[slot])

    def y_copy(slot, rows, row):
        return pltpu.make_async_copy(ybuf.at[slot, pl.ds(0, rows)], ys_hbm.at[pl.ds(row, rows)], ysem.at[slot])

    @pl.when((e == 0) & (n_total > 0))
    def _():
        per_class(0, lambda rows, row: x_copy(0, rows, row).start())

    wg_bf[...] = wg_ref[0].astype(_BF16)
    wu_bf[...] = wu_ref[0].astype(_BF16)
    wd_bf[...] = wd_ref[0].astype(_BF16)

    def piece(j, carry):
        g = g0 + j
        slot = lax.rem(g, 2)
        per_class(g, lambda rows, row: x_copy(slot, rows, row).wait())

        @pl.when(g + 1 < n_total)
        def _():
            per_class(g + 1, lambda rows, row: x_copy(1 - slot, rows, row).start())

        @pl.when(g >= 2)
        def _():
            per_class(g - 2, lambda rows, row: y_copy(slot, rows, row).wait())

        def compute(rows, row):
            x = _unpack_bf16_pair(xbuf[slot, pl.ds(0, rows)]).astype(_BF16)
            a = _dot(x, wg_bf[...])
            u = _dot(x, wu_bf[...])
            h = (a * jax.nn.sigmoid(a) * u).astype(_BF16)
            ybuf[slot, pl.ds(0, rows)] = _pack_bf16_pair(_dot(h, wd_bf[...]))
            y_copy(slot, rows, row).start()

        per_class(g, compute)
        return carry

    lax.fori_loop(0, n_here, piece, 0)

    @pl.when(e == N_EXPERTS - 1)
    def _():
        @pl.when(n_total >= 2)
        def _():
            per_class(n_total - 2, lambda rows, row: y_copy(lax.rem(n_total, 2), rows, row).wait())

        @pl.when(n_total >= 1)
        def _():
            per_class(n_total - 1, lambda rows, row: y_copy(lax.rem(n_total - 1, 2), rows, row).wait())


def _experts(piece_start, piece_row, piece_cls, n_rows, xs, w_gate, w_up, w_down):
    dh = xs.shape[1]
    d = 2 * dh
    tm = EXP_UNIT * EXP_CLASSES
    anyspec = pl.BlockSpec(memory_space=pl.ANY)
    wsel = lambda e, ps, pr, pc: (e, 0, 0)
    grid_spec = pltpu.PrefetchScalarGridSpec(
        num_scalar_prefetch=3,
        grid=(N_EXPERTS,),
        in_specs=[pl.BlockSpec((1, d, D_EXPERT), wsel), pl.BlockSpec((1, d, D_EXPERT), wsel),
                  pl.BlockSpec((1, D_EXPERT, d), wsel), anyspec],
        out_specs=anyspec,
        scratch_shapes=[pltpu.VMEM((d, D_EXPERT), _BF16), pltpu.VMEM((d, D_EXPERT), _BF16),
                        pltpu.VMEM((D_EXPERT, d), _BF16),
                        pltpu.VMEM((2, tm, dh), _U32), pltpu.VMEM((2, tm, dh), _U32),
                        pltpu.SemaphoreType.DMA((2,)), pltpu.SemaphoreType.DMA((2,))],
    )
    return pl.pallas_call(
        _experts_kernel,
        grid_spec=grid_spec,
        out_shape=jax.ShapeDtypeStruct((n_rows, dh), _U32),
        compiler_params=pltpu.CompilerParams(
            dimension_semantics=("arbitrary",), vmem_limit_bytes=VMEM_LIMIT),
        name="experts",
    )(piece_start, piece_row, piece_cls, w_gate, w_up, w_down, xs)


def _final_kernel(x1_ref, ya_ref, yb_ref, route_t_ref, gf_ref, y_ref):
    tm = x1_ref.shape[0]
    route = jnp.concatenate([route_t_ref[...], jnp.zeros((LANES - SUBLANES, tm), _F32)], axis=0).T
    x2 = (_unpack_bf16_pair(x1_ref[...]) + route[:, R_W1:R_W1 + 1] * _unpack_bf16_pair(ya_ref[...])
          + route[:, R_W2:R_W2 + 1] * _unpack_bf16_pair(yb_ref[...]))
    y_ref[...] = _rms(x2, gf_ref[...])


def _final(x1, yab, route_t, gf, n_prompt, n_sample):
    d = 2 * x1.shape[1]

    def call(tm, first_block, n_rows, name):
        tok = lambda w: pl.BlockSpec((tm, w), lambda i: (first_block + i, 0))
        sel = lambda k: pl.BlockSpec((None, tm, d // 2), lambda i: (k, first_block + i, 0))
        return pl.pallas_call(
            _final_kernel,
            grid=(n_rows // tm,),
            in_specs=[tok(d // 2), sel(0), sel(1),
                      pl.BlockSpec((SUBLANES, tm), lambda i: (0, first_block + i)),
                      pl.BlockSpec((1, d), lambda i: (0, 0))],
            out_specs=pl.BlockSpec((tm, d), lambda i: (i, 0)),
            out_shape=jax.ShapeDtypeStruct((n_rows, d), _F32),
            compiler_params=pltpu.CompilerParams(
                dimension_semantics=("arbitrary",), vmem_limit_bytes=VMEM_LIMIT),
            name=name,
        )(x1, yab, yab, route_t, gf)

    return (call(FINAL_TM, 0, n_prompt, "final_prompt"),
            call(n_sample, n_prompt // n_sample, n_sample, "final_sample"))


def _powers(lam_re, lam_im, dt):
    out = []
    for m in range(SSM_BLK + 1):
        mag = jnp.exp(m * lam_re * dt)
        ang = m * lam_im * dt
        out.append((mag * jnp.cos(ang), mag * jnp.sin(ang)))
    return out


def _spread(x, copies):
    w = x.shape[1]
    src = lax.broadcasted_iota(jnp.int32, (w, w * copies), 0)
    dst = lax.broadcasted_iota(jnp.int32, (w, w * copies), 1)
    return _dot_f32(x, jnp.where(dst % w == src, 1.0, 0.0))


def _ssm_prep_kernel(lam_ref, b_re, b_im, c_re, c_im, v_ref, r_ref, wb_ref, wc_ref, coef_ref):
    n_p, n_h = SSM_STATE, SSM_GROUP
    lr, li, dt = lam_ref[0:1, :], lam_ref[1:2, :], lam_ref[2:3, :]
    pw = _powers(lr, li, dt)
    den = lr * lr + li * li
    nr, ni = pw[1][0] - 1.0, pw[1][1]
    k_re = (nr * lr + ni * li) / den
    k_im = (ni * lr - nr * li) / den
    coef_ref[...] = jnp.concatenate(
        [pw[1][0], pw[1][1], pw[SSM_BLK][0], pw[SSM_BLK][1], jnp.zeros((SUBLANES - 4, TILE_STATE), _F32)], axis=0)

    on_diag_b = (lax.broadcasted_iota(jnp.int32, (TILE_STATE, LANES), 0) // n_p
                 == lax.broadcasted_iota(jnp.int32, (TILE_STATE, LANES), 1) // n_h)
    rows_gp = lambda ref: ref[...].reshape(TILE_STATE, n_h)
    bt_re = jnp.where(on_diag_b, _spread(rows_gp(b_re), SUBLANES), 0.0).T
    bt_im = jnp.where(on_diag_b, _spread(rows_gp(b_im), SUBLANES), 0.0).T
    bb_re = k_re * bt_re - k_im * bt_im
    bb_im = k_re * bt_im + k_im * bt_re
    wb_ref[0] = jnp.concatenate([bb_re, bb_im], axis=1)
    v_rows = []
    for s in range(SSM_BLK):
        pr, pi = pw[SSM_BLK - 1 - s]
        v_rows.append(jnp.concatenate([pr * bb_re - pi * bb_im, pr * bb_im + pi * bb_re], axis=1))
    v_ref[0] = jnp.concatenate(v_rows, axis=0).astype(v_ref.dtype)

    on_diag_c = (lax.broadcasted_iota(jnp.int32, (LANES, TILE_STATE), 0) // n_h
                 == lax.broadcasted_iota(jnp.int32, (LANES, TILE_STATE), 1) // n_p)
    rows_gh = lambda ref: ref[...].reshape(LANES, n_p)
    ct_re = jnp.where(on_diag_c, _spread(rows_gh(c_re), SUBLANES), 0.0)
    ct_im = jnp.where(on_diag_c, _spread(rows_gh(c_im), SUBLANES), 0.0)
    cl = [(ct_re * pr - ct_im * pi, ct_re * pi + ct_im * pr) for pr, pi in pw]
    wc_ref[0] = jnp.concatenate([cl[0][0], -cl[0][1]], axis=1).T
    direct = [_dot_hi(cl[m][0], bb_re, transpose_b=True) - _dot_hi(cl[m][1], bb_im, transpose_b=True)
              for m in range(SSM_BLK)]
    zero = jnp.zeros((LANES, LANES), _F32)
    rt = jnp.concatenate(
        [jnp.concatenate([cl[i + 1][0], -cl[i + 1][1]]
                         + [direct[i - s] if s <= i else zero for s in range(SSM_BLK)], axis=1)
         for i in range(SSM_BLK)], axis=0)
    r_ref[0] = rt.T.astype(r_ref.dtype)


def _ssm_params(lam_re, lam_im, log_dt, b_re, b_im, c_re, c_im, d_skip):
    n_g, n_p, n_h = N_SSM_GROUPS, SSM_STATE, SSM_GROUP
    dt = jnp.repeat(jnp.exp(log_dt), n_p)
    lam = jnp.zeros((SUBLANES, STATE_COLS), _F32).at[0].set(lam_re.reshape(-1)).at[1].set(
        lam_im.reshape(-1)).at[2].set(dt)
    groups = lambda r, c: pl.BlockSpec((SUBLANES, r, c), lambda k: (k, 0, 0))
    out3 = lambda rows, w: pl.BlockSpec((1, rows, w), lambda k: (k, 0, 0))
    cols = pl.BlockSpec((SUBLANES, TILE_STATE), lambda k: (0, k))
    k_blk = SSM_BLK * LANES
    v, r, wb, wc, coef = pl.pallas_call(
        _ssm_prep_kernel,
        grid=(N_LANE_TILES,),
        in_specs=[cols, groups(n_p, n_h), groups(n_p, n_h), groups(n_h, n_p), groups(n_h, n_p)],
        out_specs=[out3(k_blk, 2 * TILE_STATE), out3(2 * TILE_STATE + k_blk, k_blk),
                   out3(LANES, 2 * TILE_STATE), out3(2 * TILE_STATE, LANES), cols],
        out_shape=[jax.ShapeDtypeStruct((N_LANE_TILES, k_blk, 2 * TILE_STATE), _BF16),
                   jax.ShapeDtypeStruct((N_LANE_TILES, 2 * TILE_STATE + k_blk, k_blk), _BF16),
                   jax.ShapeDtypeStruct((N_LANE_TILES, LANES, 2 * TILE_STATE), _F32),
                   jax.ShapeDtypeStruct((N_LANE_TILES, 2 * TILE_STATE, LANES), _F32),
                   jax.ShapeDtypeStruct((SUBLANES, STATE_COLS), _F32)],
        compiler_params=pltpu.CompilerParams(
            dimension_semantics=("arbitrary",), vmem_limit_bytes=VMEM_LIMIT),
        name="ssm_prep",
    )(lam, b_re, b_im, c_re, c_im)
    return wb, wc, v, r, coef, d_skip.reshape(1, D_SSM)


def _dispatch_plan(route_t, cnt):
    t_all = route_t.shape[1]
    codes = route_t[R_CODE1:R_CODE2 + 1].astype(jnp.int32).reshape(-1)
    per_pass = SC_WORKERS * DISPATCH_CHUNK
    codes = jnp.pad(codes, (0, -(2 * t_all) % per_pass))
    counts = cnt[:, 0].astype(jnp.int32)
    zero = jnp.zeros((1,), jnp.int32)
    units = (counts + EXP_UNIT - 1) // EXP_UNIT
    unit_start = jnp.concatenate([zero, jnp.cumsum(units)])
    start_row = jnp.zeros((LANES,), jnp.int32).at[:N_EXPERTS].set(unit_start[:N_EXPERTS] * EXP_UNIT)
    pieces = (units + EXP_CLASSES - 1) // EXP_CLASSES
    piece_start = jnp.concatenate([zero, jnp.cumsum(pieces)])
    tm = EXP_UNIT * EXP_CLASSES
    max_units = (2 * t_all + N_EXPERTS * (EXP_UNIT - 1)) // EXP_UNIT
    max_pieces = (max_units + N_EXPERTS * (EXP_CLASSES - 1)) // EXP_CLASSES
    g = jnp.arange(max_pieces, dtype=jnp.int32)
    owner = ((g[:, None] >= piece_start[None, :-1]) & (g[:, None] < piece_start[None, 1:])).astype(jnp.int32)
    pick = lambda table: jnp.sum(owner * table[None, :], axis=1)
    first_unit = pick(unit_start[:-1]) + (g - pick(piece_start[:-1])) * EXP_CLASSES
    piece_row = first_unit * EXP_UNIT
    piece_cls = jnp.clip(pick(unit_start[1:]) - first_unit, 1, EXP_CLASSES)
    n_rows = (max_units * EXP_UNIT + tm - 1) // tm * tm + tm
    return codes, start_row, n_rows, piece_start, piece_row, piece_cls


def kernel(x_prompt, x_sample, state_ssm_re, state_ssm_im, norm1_g, w_in, lam_re, lam_im, log_dt, ssm_b_re, ssm_b_im, ssm_c_re, ssm_c_im, ssm_d, gmlp_norm_g, gmlp_w_s, gmlp_b_s, out_norm_ssm_g, out_norm_gmlp_g, w_out, norm2_g, w_router_group, b_router_group, w_router_expert, b_router_expert, w_gate, w_up, w_down, final_norm_g):
    n, l, d = x_prompt.shape
    ns = x_sample.shape[0]
    t_all = n * l + ns
    assert w_in.shape[0] == 1 and x_sample.shape[1] == 1 and n == SUBLANES and d == D_MODEL
    li = 0
    g1 = norm1_g[li].reshape(1, d)
    gn = gmlp_norm_g[li].reshape(1, D_GMLP)
    tril = jnp.tril(jnp.ones((CHUNK, CHUNK), dtype=bool))
    ws_tril = jnp.where(tril[None], gmlp_w_s[li], 0.0)
    bs = gmlp_b_s[li]
    gog = out_norm_gmlp_g[li].reshape(1, D_GMLP)
    gos = out_norm_ssm_g[li].reshape(1, D_SSM)
    wb, wc, v_blk, r_blk, coef, dsk = _ssm_params(
        lam_re[li], lam_im[li], log_dt[li], ssm_b_re[li], ssm_b_im[li], ssm_c_re[li], ssm_c_im[li], ssm_d[li])
    g2 = norm2_g[li].reshape(1, d)
    pad = LANES - N_EXPERTS - N_EXPERT_GROUPS
    wr = jnp.concatenate([w_router_expert[li], w_router_group[li], jnp.zeros((d, pad), _F32)], axis=1)
    br = jnp.concatenate([b_router_expert[li], b_router_group[li], jnp.zeros((pad,), _F32)]).reshape(1, LANES)

    xa, sg, mixb = _front_prompt(x_prompt, g1, w_in[li], gn, ws_tril.astype(_BF16), bs.T, gog)
    mixa, hfin = _ssm_prompt(xa, sg, v_blk, r_blk, coef, dsk, gos)
    w00 = jnp.repeat(ws_tril[:, 0, 0], GMLP_HEAD).reshape(1, D_GMLP)
    b0 = jnp.repeat(bs[:, 0], GMLP_HEAD).reshape(1, D_GMLP)
    mix_s, hr_s, hi_s, vrow = _front_sample(
        x_sample.reshape(ns, d), g1, w_in[li], gn, w00, b0, gog, wb, wc, coef, dsk, gos,
        state_ssm_re[li].reshape(ns, STATE_COLS), state_ssm_im[li].reshape(ns, STATE_COLS))

    x1, xn, route_t, cnt = _mixer_out(x_prompt, mixa, mixb, x_sample.reshape(ns, d), mix_s,
                                      w_out[li], g2, wr, br)
    codes, start_row, n_rows, piece_start, piece_row, piece_cls = _dispatch_plan(route_t, cnt)
    xs, dest = _sc_dispatch(xn, codes, start_row, n_rows, DISPATCH_CHUNK)
    ys = _experts(piece_start, piece_row, piece_cls, n_rows, xs, w_gate[li], w_up[li], w_down[li])
    yab = _sc_combine(ys, dest, 2 * t_all, COMBINE_CHUNK).reshape(2, t_all, d // 2)
    y_p, y_s = _final(x1, yab, route_t, final_norm_g.reshape(1, d), n * l, ns)

    hf = hfin.reshape(n, N_LANE_TILES, 2, 8, SSM_STATE)
    re_p = hf[:, :, 0].reshape(1, n, N_SSM_GROUPS, SSM_STATE)
    im_p = hf[:, :, 1].reshape(1, n, N_SSM_GROUPS, SSM_STATE)
    re_s = hr_s.reshape(1, ns, N_SSM_GROUPS, SSM_STATE)
    im_s = hi_s.reshape(1, ns, N_SSM_GROUPS, SSM_STATE)
    return (y_p.reshape(n, l, d), y_s.reshape(ns, 1, d), re_p, im_p, re_s, im_s,
            vrow.reshape(1, ns, 1, D_GMLP))
```

```python
import math

import jax
import jax.numpy as jnp
from jax import lax
from jax.experimental import pallas as pl
from jax.experimental.pallas import tpu as pltpu
from jax.experimental.pallas import tpu_sc as plsc

D_MODEL = 1024
D_SSM = 512
D_GMLP = 512
SSM_GROUP = 16
N_SSM_GROUPS = 32
SSM_STATE = 64
CHUNK = 128
N_GMLP_HEADS = 4
GMLP_HEAD = 128
N_EXPERT_GROUPS = 4
EXPERTS_PER_GROUP = 8
N_EXPERTS = 32
D_EXPERT = 512
D_IN = 2048
EPS = 1e-6

LANES = 128
SUBLANES = 8
N_LANE_TILES = D_SSM // LANES
STATE_COLS = N_SSM_GROUPS * SSM_STATE
TILE_STATE = STATE_COLS // N_LANE_TILES
VMEM_LIMIT = 56 * 1024 * 1024

SC_CORES = 2
SC_SUBCORES = 16
SC_LANES = 16
SC_WORKERS = SC_CORES * SC_SUBCORES

FRONT_TL = 512
SSM_LC = 256
SSM_BLK = 4
COEF_LB_RE, COEF_LB_IM, COEF_LBLK_RE, COEF_LBLK_IM = 0, 1, 2, 3
TOK_TM = 512
FINAL_TM = 1024
EXP_UNIT = 128
EXP_CLASSES = 8
DISPATCH_CHUNK = 80
COMBINE_CHUNK = 24

R_E1, R_E2, R_W1, R_W2, R_RANK1, R_RANK2, R_CODE1, R_CODE2 = 0, 1, 2, 3, 4, 5, 6, 7
CODE_BITS = 16
CODE_SHIFT = float(1 << CODE_BITS)

_INV_SQRT2 = 1.0 / math.sqrt(2.0)
_BF16 = jnp.bfloat16
_F32 = jnp.float32
_U32 = jnp.uint32


def _gelu(x):
    return 0.5 * x * (1.0 + lax.erf(x * _INV_SQRT2))


def _rms(x, g):
    return x * lax.rsqrt(jnp.mean(x * x, axis=-1, keepdims=True) + EPS) * g


def _dot(a, b):
    return jnp.dot(a, b, preferred_element_type=_F32)


def _dot_f32(a, b):
    return jnp.dot(a, b, preferred_element_type=_F32, precision=lax.Precision.HIGHEST)


def _dot_hi(a, b, transpose_b=False):
    def split(x):
        hi = x.astype(_BF16)
        return hi, (x - hi.astype(_F32)).astype(_BF16)

    dims = (((1,), (1 if transpose_b else 0,)), ((), ()))
    dot = lambda u, v: lax.dot_general(u, v, dims, preferred_element_type=_F32)
    a_hi, a_lo = split(a)
    b_hi, b_lo = split(b)
    return dot(a_hi, b_hi) + dot(a_hi, b_lo) + dot(a_lo, b_hi)


def _pack_bf16_pair(x):
    w = x.shape[1] // 2
    hi = lax.bitcast_convert_type(x[:, :w].astype(_BF16).astype(_F32), _U32)
    lo = lax.bitcast_convert_type(x[:, w:].astype(_BF16).astype(_F32), _U32)
    return hi | (lo >> 16)


def _unpack_bf16_pair(p):
    hi = lax.bitcast_convert_type(p & jnp.uint32(0xFFFF0000), _F32)
    lo = lax.bitcast_convert_type(p << 16, _F32)
    return jnp.concatenate([hi, lo], axis=-1)


def _head_norm_gelu(vb, gn):
    v = _gelu(vb)
    parts = []
    for h in range(N_GMLP_HEADS):
        vh = v[:, h * GMLP_HEAD:(h + 1) * GMLP_HEAD]
        parts.append(vh * lax.rsqrt(jnp.mean(vh * vh, axis=-1, keepdims=True) + EPS))
    return jnp.concatenate(parts, axis=-1) * gn


def _front_prompt_kernel(x_ref, g1_ref, win_ref, gn_ref, ws_ref, bs_ref, gog_ref,
                         xa_ref, sg_ref, mixb_ref, win_bf, z_ref):
    @pl.when(pl.program_id(0) == 0)
    def _():
        win_bf[...] = win_ref[...].astype(_BF16)
        z_ref[...] = jnp.zeros_like(z_ref)

    z = z_ref[...]
    x = x_ref[0]
    hn = _rms(x, g1_ref[...]).astype(_BF16)
    z_ref[...] = _dot(hn, win_bf[...])
    xa_ref[0] = z[:, :D_SSM]
    sg_ref[0] = jax.nn.sigmoid(z[:, D_SSM:2 * D_SSM])
    ub = _gelu(z[:, 2 * D_SSM:2 * D_SSM + D_GMLP])
    vbn = _head_norm_gelu(z[:, 2 * D_SSM + D_GMLP:], gn_ref[...]).astype(_BF16)
    tl = x.shape[0]
    rows = []
    for c in range(tl // CHUNK):
        heads = []
        for h in range(N_GMLP_HEADS):
            vh = vbn[c * CHUNK:(c + 1) * CHUNK, h * GMLP_HEAD:(h + 1) * GMLP_HEAD]
            heads.append(_dot(ws_ref[h], vh) + bs_ref[:, h:h + 1])
        rows.append(jnp.concatenate(heads, axis=-1))
    s = jnp.concatenate(rows, axis=0)
    mixb_ref[0] = _rms(ub * s, gog_ref[...]).astype(_BF16)


def _front_prompt(x, g1, win, gn, ws_tril_bf, bs_t, gog):
    n, l, d = x.shape
    tl = FRONT_TL
    per_seq = l // tl
    n_tiles = n * per_seq
    cur = lambda i: jnp.minimum(i, n_tiles - 1)
    prev = lambda i: jnp.maximum(i - 1, 0)
    const = lambda *shape: pl.BlockSpec(shape, lambda i: (0,) * len(shape))
    seq = lambda w, which: pl.BlockSpec((1, tl, w), lambda i: (which(i) // per_seq, which(i) % per_seq, 0))
    return pl.pallas_call(
        _front_prompt_kernel,
        grid=(n_tiles + 1,),
        in_specs=[seq(d, cur), const(1, d), const(d, D_IN), const(1, D_GMLP),
                  const(N_GMLP_HEADS, CHUNK, CHUNK), const(CHUNK, N_GMLP_HEADS), const(1, D_GMLP)],
        out_specs=[seq(D_SSM, prev), seq(D_SSM, prev), seq(D_GMLP, prev)],
        out_shape=[jax.ShapeDtypeStruct((n, l, D_SSM), _F32),
                   jax.ShapeDtypeStruct((n, l, D_SSM), _F32),
                   jax.ShapeDtypeStruct((n, l, D_GMLP), _BF16)],
        scratch_shapes=[pltpu.VMEM((d, D_IN), _BF16), pltpu.VMEM((tl, D_IN), _F32)],
        compiler_params=pltpu.CompilerParams(
            dimension_semantics=("arbitrary",), vmem_limit_bytes=VMEM_LIMIT),
        name="front_prompt",
    )(x, g1, win, gn, ws_tril_bf, bs_t, gog)


def _ssm_prompt_kernel(xa_ref, sg_ref, v_ref, r_ref, coef_ref, dsk_ref, gos_ref,
                       mixa_ref, hfin_ref, s_ref, st_ref):
    lc = xa_ref.shape[1]
    nblk = lc // SSM_BLK
    rows = nblk * SUBLANES

    @pl.when(pl.program_id(0) == 0)
    def _():
        st_ref[...] = jnp.zeros_like(st_ref)

    def by_position(ref):
        t = pltpu.einshape("btc->tbc", ref[...]).reshape(nblk, SSM_BLK, SUBLANES, D_SSM)
        return [t[:, i].reshape(rows, D_SSM) for i in range(SSM_BLK)]

    xs = by_position(xa_ref)
    xs_bf = [x.astype(_BF16) for x in xs]
    xk = [jnp.concatenate([x[:, k * LANES:(k + 1) * LANES] for x in xs_bf], axis=-1)
          for k in range(N_LANE_TILES)]
    for k in range(N_LANE_TILES):
        s_ref[:, 2 * TILE_STATE * k:2 * TILE_STATE * (k + 1)] = _dot(xk[k], v_ref[k])

    for kk in range(0, N_LANE_TILES, 2):
        tiles = (kk, kk + 1)
        cols = [(2 * TILE_STATE * k, 2 * TILE_STATE * k + TILE_STATE) for k in tiles]
        lbs = [tuple(jnp.broadcast_to(coef_ref[row:row + 1, k * TILE_STATE:(k + 1) * TILE_STATE],
                                      (SUBLANES, TILE_STATE)) for row in (COEF_LBLK_RE, COEF_LBLK_IM))
               for k in tiles]

        def body(j, carry, cols=cols, lbs=lbs):
            r0 = pl.multiple_of(j * SUBLANES, SUBLANES)
            out = []
            for q, ((c_re, c_im), (lr, li)) in enumerate(zip(cols, lbs)):
                hr, hi = carry[2 * q], carry[2 * q + 1]
                sr = s_ref[pl.ds(r0, SUBLANES), c_re:c_re + TILE_STATE]
                si = s_ref[pl.ds(r0, SUBLANES), c_im:c_im + TILE_STATE]
                s_ref[pl.ds(r0, SUBLANES), c_re:c_re + TILE_STATE] = hr
                s_ref[pl.ds(r0, SUBLANES), c_im:c_im + TILE_STATE] = hi
                out += [lr * hr - li * hi + sr, lr * hi + li * hr + si]
            return tuple(out)

        init = tuple(st_ref[:, c:c + TILE_STATE] for c_pair in cols for c in c_pair)
        fin = lax.fori_loop(0, nblk, body, init, unroll=2)
        for q, (c_re, c_im) in enumerate(cols):
            st_ref[:, c_re:c_re + TILE_STATE] = fin[2 * q]
            st_ref[:, c_im:c_im + TILE_STATE] = fin[2 * q + 1]

    yk = []
    for k in range(N_LANE_TILES):
        h_in = s_ref[:, 2 * TILE_STATE * k:2 * TILE_STATE * (k + 1)].astype(_BF16)
        yk.append(_dot(jnp.concatenate([h_in, xk[k]], axis=-1), r_ref[k]))
    ys = []
    for i in range(SSM_BLK):
        y = jnp.concatenate([y_k[:, i * LANES:(i + 1) * LANES] for y_k in yk], axis=-1) + dsk_ref[...] * xs[i]
        ys.append(y.reshape(nblk, SUBLANES, D_SSM))
    y = pltpu.einshape("tbc->btc", jnp.stack(ys, axis=1).reshape(lc, SUBLANES, D_SSM))
    mixa_ref[...] = _rms(_gelu(y) * sg_ref[...], gos_ref[...]).astype(_BF16)
    hfin_ref[...] = st_ref[...]


def _ssm_prompt(xa, sg, v, r, coef, dsk, gos):
    n, l, _ = xa.shape
    lc = SSM_LC
    const = lambda *shape: pl.BlockSpec(shape, lambda i: (0,) * len(shape))
    seq_spec = pl.BlockSpec((n, lc, D_SSM), lambda i: (0, i, 0))
    return pl.pallas_call(
        _ssm_prompt_kernel,
        grid=(l // lc,),
        in_specs=[seq_spec, seq_spec, const(*v.shape), const(*r.shape),
                  const(*coef.shape), const(1, D_SSM), const(1, D_SSM)],
        out_specs=[seq_spec, const(n, 2 * STATE_COLS)],
        out_shape=[jax.ShapeDtypeStruct((n, l, D_SSM), _BF16),
                   jax.ShapeDtypeStruct((n, 2 * STATE_COLS), _F32)],
        scratch_shapes=[pltpu.VMEM((lc // SSM_BLK * n, 2 * STATE_COLS), _F32),
                        pltpu.VMEM((n, 2 * STATE_COLS), _F32)],
        compiler_params=pltpu.CompilerParams(
            dimension_semantics=("arbitrary",), vmem_limit_bytes=VMEM_LIMIT),
        name="ssm_prompt",
    )(xa, sg, v, r, coef, dsk, gos)


def _front_sample_kernel(x_ref, g1_ref, win_ref, gn_ref, w00_ref, b0_ref, gog_ref,
                         wb_ref, wc_ref, coef_ref, dsk_ref, gos_ref, h0r_ref, h0i_ref,
                         mix_ref, hr_ref, hi_ref, vrow_ref):
    x = x_ref[...]
    hn = _rms(x, g1_ref[...])
    z = _dot_hi(hn, win_ref[...])
    xa = z[:, :D_SSM]
    ys = []
    for k in range(N_LANE_TILES):
        bu = _dot_hi(xa[:, k * LANES:(k + 1) * LANES], wb_ref[k])
        sl = slice(k * TILE_STATE, (k + 1) * TILE_STATE)
        lr, li = coef_ref[COEF_LB_RE:COEF_LB_RE + 1, sl], coef_ref[COEF_LB_IM:COEF_LB_IM + 1, sl]
        h0r, h0i = h0r_ref[:, sl], h0i_ref[:, sl]
        nr = lr * h0r - li * h0i + bu[:, :TILE_STATE]
        ni = lr * h0i + li * h0r + bu[:, TILE_STATE:]
        hr_ref[:, sl] = nr
        hi_ref[:, sl] = ni
        ys.append(_dot_hi(jnp.concatenate([nr, ni], axis=-1), wc_ref[k]))
    y = jnp.concatenate(ys, axis=-1) + dsk_ref[...] * xa
    ya = _gelu(y) * jax.nn.sigmoid(z[:, D_SSM:2 * D_SSM])
    mix_ref[:, :D_SSM] = _rms(ya, gos_ref[...])
    ub = _gelu(z[:, 2 * D_SSM:2 * D_SSM + D_GMLP])
    vbn = _head_norm_gelu(z[:, 2 * D_SSM + D_GMLP:], gn_ref[...])
    vrow_ref[...] = vbn
    s = w00_ref[...] * vbn + b0_ref[...]
    mix_ref[:, D_SSM:] = _rms(ub * s, gog_ref[...])


def _route(logits, base):
    tm = logits.shape[0]
    lt = logits.T
    ex = lt[:N_EXPERTS, :]
    gr = lt[N_EXPERTS:N_EXPERTS + SUBLANES, :]
    row_e = lax.broadcasted_iota(jnp.int32, ex.shape, 0).astype(_F32)
    row_g = lax.broadcasted_iota(jnp.int32, gr.shape, 0).astype(_F32)
    neg = jnp.float32(-jnp.inf)
    big = jnp.float32(LANES)
    is_g = row_g < N_EXPERT_GROUPS
    gl = jnp.where(is_g, gr, neg)
    gmax = jnp.max(gl, axis=0, keepdims=True)
    gi = jnp.min(jnp.where(gl == gmax, row_g, big), axis=0, keepdims=True)
    p_top = 1.0 / jnp.sum(jnp.where(is_g, jnp.exp(gl - gmax), 0.0), axis=0, keepdims=True)
    lo = gi * EXPERTS_PER_GROUP
    in_grp = (row_e >= lo) & (row_e < lo + EXPERTS_PER_GROUP)
    m1 = jnp.max(jnp.where(in_grp, ex, neg), axis=0, keepdims=True)
    i1 = jnp.min(jnp.where(in_grp & (ex == m1), row_e, big), axis=0, keepdims=True)
    rest = in_grp & (row_e != i1)
    m2 = jnp.max(jnp.where(rest, ex, neg), axis=0, keepdims=True)
    i2 = jnp.min(jnp.where(rest & (ex == m2), row_e, big), axis=0, keepdims=True)
    e2 = jnp.exp(m2 - m1)
    w1 = p_top / (1.0 + e2)
    w2 = p_top * e2 / (1.0 + e2)
    sel1 = row_e == i1
    sel2 = row_e == i2
    hits = jnp.where(sel1 | sel2, 1.0, 0.0)
    src = lax.broadcasted_iota(jnp.int32, (tm, tm), 0)
    dst = lax.broadcasted_iota(jnp.int32, (tm, tm), 1)
    before = _dot(hits.astype(_BF16), jnp.where(src < dst, 1.0, 0.0).astype(_BF16)) + base
    rank1 = jnp.sum(jnp.where(sel1, before, 0.0), axis=0, keepdims=True)
    rank2 = jnp.sum(jnp.where(sel2, before, 0.0), axis=0, keepdims=True)
    fields = {R_E1: i1, R_E2: i2, R_W1: w1, R_W2: w2, R_RANK1: rank1, R_RANK2: rank2,
              R_CODE1: i1 * CODE_SHIFT + rank1, R_CODE2: i2 * CODE_SHIFT + rank2}
    row8 = lax.broadcasted_iota(jnp.int32, (SUBLANES, tm), 0)
    route_t = jnp.zeros((SUBLANES, tm), _F32)
    for r, val in fields.items():
        route_t = jnp.where(row8 == r, val, route_t)
    return route_t, base + jnp.sum(hits, axis=1, keepdims=True)


def _mixer_out_prompt_kernel(x_ref, mixa_ref, mixb_ref, wo_ref, g2_ref, wr_ref, br_ref,
                             x1_ref, xn_ref, route_t_ref, cnt_ref, base_ref, logits_ref, wo_bf):
    i = pl.program_id(0)

    @pl.when(i == 0)
    def _():
        base_ref[...] = jnp.zeros_like(base_ref)
        logits_ref[...] = jnp.zeros_like(logits_ref)
        wo_bf[...] = wo_ref[...].astype(_BF16)

    prev_logits = logits_ref[...]
    x1 = x_ref[0] + _dot(mixa_ref[0], wo_bf[:D_SSM, :]) + _dot(mixb_ref[0], wo_bf[D_SSM:, :])
    xn = _rms(x1, g2_ref[...])
    x1_ref[...] = _pack_bf16_pair(x1)
    xn_ref[...] = _pack_bf16_pair(xn)
    logits_ref[...] = _dot(xn.astype(_BF16), wr_ref[...]) + br_ref[...]
    route_t, base = _route(prev_logits, base_ref[...])
    route_t_ref[...] = route_t
    base = jnp.where(i >= 1, base, base_ref[...])
    base_ref[...] = base
    cnt_ref[...] = base


def _mixer_out_sample_kernel(x_ref, mix_ref, wo_ref, g2_ref, wr_ref, br_ref, cnt_in_ref,
                             x1_in, xn_in, route_t_in,
                             x1_ref, xn_ref, route_t_ref, cnt_ref):
    del x1_in, xn_in, route_t_in
    x1 = (x_ref[...] + _dot_hi(mix_ref[:, :D_SSM], wo_ref[:D_SSM, :])
          + _dot_hi(mix_ref[:, D_SSM:], wo_ref[D_SSM:, :]))
    xn = _rms(x1, g2_ref[...])
    logits = _dot_hi(xn, wr_ref[...]) + br_ref[...]
    route_t, base = _route(logits, cnt_in_ref[...])
    x1_ref[...] = _pack_bf16_pair(x1)
    xn_ref[...] = _pack_bf16_pair(xn)
    route_t_ref[...] = route_t
    cnt_ref[...] = base


N_FRONT_SAMPLE_INPUTS = 14


def _sample_layer_kernel(*refs):
    front_in = refs[:N_FRONT_SAMPLE_INPUTS]
    wo_ref, g2_ref, wr_ref, br_ref, cnt_in_ref, x1_in, xn_in, route_t_in = refs[N_FRONT_SAMPLE_INPUTS:-8]
    x1_ref, xn_ref, route_t_ref, cnt_ref, hr_ref, hi_ref, vrow_ref, mix_ref = refs[-8:]
    _front_sample_kernel(*front_in, mix_ref, hr_ref, hi_ref, vrow_ref)
    _mixer_out_sample_kernel(front_in[0], mix_ref, wo_ref, g2_ref, wr_ref, br_ref, cnt_in_ref,
                             x1_in, xn_in, route_t_in, x1_ref, xn_ref, route_t_ref, cnt_ref)


def _mixer_out(x_p, mixa, mixb, sample_front, wo, g2, wr, br):
    n, l, d = x_p.shape
    ns = sample_front[0].shape[0]
    t_all = n * l + ns
    tm = TOK_TM
    per_seq = l // tm
    n_tiles = n * per_seq
    cur = lambda i: jnp.minimum(i, n_tiles - 1)
    prev = lambda i: jnp.maximum(i - 1, 0)
    const = lambda *shape: pl.BlockSpec(shape, lambda i: (0,) * len(shape))
    seq = lambda w: pl.BlockSpec((1, tm, w), lambda i: (cur(i) // per_seq, cur(i) % per_seq, 0))
    tok = lambda w, which: pl.BlockSpec((tm, w), lambda i: (which(i), 0))
    tok_shapes = [jax.ShapeDtypeStruct((t_all, d // 2), _U32),
                  jax.ShapeDtypeStruct((t_all, d // 2), _U32),
                  jax.ShapeDtypeStruct((SUBLANES, t_all), _F32)]
    cnt_shape = jax.ShapeDtypeStruct((N_EXPERTS, 1), _F32)
    x1, xn, route_t, cnt = pl.pallas_call(
        _mixer_out_prompt_kernel,
        grid=(n_tiles + 1,),
        in_specs=[seq(d), seq(D_SSM), seq(D_GMLP),
                  const(d, d), const(1, d), const(d, LANES), const(1, LANES)],
        out_specs=[tok(d // 2, cur), tok(d // 2, cur),
                   pl.BlockSpec((SUBLANES, tm), lambda i: (0, prev(i))), const(N_EXPERTS, 1)],
        out_shape=tok_shapes + [cnt_shape],
        scratch_shapes=[pltpu.VMEM((N_EXPERTS, 1), _F32), pltpu.VMEM((tm, LANES), _F32),
                        pltpu.VMEM((d, d), _BF16)],
        compiler_params=pltpu.CompilerParams(
            dimension_semantics=("arbitrary",), vmem_limit_bytes=VMEM_LIMIT),
        name="mixer_out_prompt",
    )(x_p, mixa, mixb, wo, g2, wr.astype(_BF16), br)
    tail = (n * l) // ns
    c1 = lambda *shape: pl.BlockSpec(shape, lambda i: (0,) * len(shape))
    anyspec = pl.BlockSpec(memory_space=pl.ANY)
    tail_spec = lambda w: pl.BlockSpec((ns, w), lambda i: (tail, 0))
    assert len(sample_front) == N_FRONT_SAMPLE_INPUTS
    whole = lambda a: c1(*a.shape)
    first_alias = N_FRONT_SAMPLE_INPUTS + 5
    state_shape = jax.ShapeDtypeStruct((ns, STATE_COLS), _F32)
    return pl.pallas_call(
        _sample_layer_kernel,
        grid=(1,),
        in_specs=([whole(a) for a in sample_front]
                  + [c1(d, d), c1(1, d), c1(d, LANES), c1(1, LANES), c1(N_EXPERTS, 1), anyspec, anyspec, anyspec]),
        out_specs=[tail_spec(d // 2), tail_spec(d // 2),
                   pl.BlockSpec((SUBLANES, ns), lambda i: (0, tail)), c1(N_EXPERTS, 1),
                   c1(ns, STATE_COLS), c1(ns, STATE_COLS), c1(ns, D_GMLP)],
        out_shape=tok_shapes + [cnt_shape, state_shape, state_shape, jax.ShapeDtypeStruct((ns, D_GMLP), _F32)],
        scratch_shapes=[pltpu.VMEM((ns, d), _F32)],
        input_output_aliases={first_alias: 0, first_alias + 1: 1, first_alias + 2: 2},
        compiler_params=pltpu.CompilerParams(
            dimension_semantics=("arbitrary",), vmem_limit_bytes=VMEM_LIMIT),
        name="sample_layer",
    )(*sample_front, wo, g2, wr, br, cnt, x1, xn, route_t)


def _sc_stream(n_chunks, gather, write):
    gather(0).start()
    for j in range(n_chunks):
        if j + 1 < n_chunks:
            if j >= 1:
                write(j - 1).wait()
            gather(j + 1).start()
        gather(j).wait()
        write(j).start()
    if n_chunks >= 2:
        write(n_chunks - 2).wait()
    write(n_chunks - 1).wait()


def _sc_mesh():
    return plsc.VectorSubcoreMesh(core_axis_name="c", subcore_axis_name="s",
                                  num_cores=SC_CORES, num_subcores=SC_SUBCORES)


def _sc_buffers(chunk, w, dtype):
    return [pltpu.VMEM((chunk, w), dtype), pltpu.VMEM((chunk, w), dtype)] + [pltpu.SemaphoreType.DMA] * 4


def _sc_combine(table, idx, n_out, chunk):
    w = table.shape[1]
    rows_w = n_out // SC_WORKERS
    n_chunks = rows_w // chunk
    assert rows_w * SC_WORKERS == n_out and n_chunks * chunk == rows_w and rows_w % SUBLANES == 0

    def body(table_hbm, idx_hbm, out_hbm, idx_v, buf0, buf1, g0, g1, w0, w1):
        wid = lax.axis_index("s") * SC_CORES + lax.axis_index("c")
        base = pl.multiple_of(wid * rows_w, SUBLANES)
        pltpu.sync_copy(idx_hbm.at[pl.ds(base, rows_w)], idx_v)
        bufs, gsems, wsems = (buf0, buf1), (g0, g1), (w0, w1)

        def gather(j):
            return pltpu.make_async_copy(table_hbm.at[idx_v.at[pl.ds(j * chunk, chunk)]], bufs[j % 2], gsems[j % 2])

        def write(j):
            return pltpu.make_async_copy(bufs[j % 2], out_hbm.at[pl.ds(base + j * chunk, chunk)], wsems[j % 2])

        _sc_stream(n_chunks, gather, write)

    return pl.kernel(
        body,
        out_type=jax.ShapeDtypeStruct((n_out, w), table.dtype),
        mesh=_sc_mesh(),
        scratch_types=[pltpu.VMEM((rows_w,), jnp.int32)] + _sc_buffers(chunk, w, table.dtype),
        compiler_params=pltpu.CompilerParams(use_tc_tiling_on_sc=True),
        name="sc_combine",
    )(table, idx)


def _sc_dispatch(table, codes, start_row, n_out, chunk):
    t_all, w = table.shape
    n_pad = codes.shape[0]
    n_ent = 2 * t_all
    ent_w = n_pad // SC_WORKERS
    n_chunks = ent_w // chunk
    per_chunk = chunk // SC_LANES
    trash = n_out - (n_pad - n_ent)
    assert ent_w * SC_WORKERS == n_pad and n_chunks * chunk == ent_w
    assert per_chunk * SC_LANES == chunk and chunk <= LANES and n_pad - n_ent <= t_all

    def body(table_hbm, code_hbm, start_hbm, out_hbm, dest_hbm,
             code_v, dest_v, tok_v, dst_v, start_v, buf0, buf1, g0, g1, w0, w1):
        wid = lax.axis_index("s") * SC_CORES + lax.axis_index("c")
        ebase = pl.multiple_of(wid * ent_w, SUBLANES)
        pltpu.sync_copy(code_hbm.at[pl.ds(ebase, ent_w)], code_v)
        pltpu.sync_copy(start_hbm, start_v)
        lane = lax.iota(jnp.int32, SC_LANES)
        for j in range(n_chunks):
            for c in range(per_chunk):
                off = j * chunk + c * SC_LANES
                ent = ebase + off + lane
                code = code_v[pl.ds(off, SC_LANES)]
                d = plsc.load_gather(start_v, [code >> CODE_BITS]) + (code & ((1 << CODE_BITS) - 1))
                d = jnp.where(ent >= n_ent, trash + (ent - n_ent), d)
                tok = jnp.where(ent >= t_all, ent - t_all, ent)
                tok = jnp.where(tok >= t_all, tok - t_all, tok)
                dest_v[pl.ds(off, SC_LANES)] = d
                dst_v[j, pl.ds(c * SC_LANES, SC_LANES)] = d
                tok_v[j, pl.ds(c * SC_LANES, SC_LANES)] = tok
        pltpu.sync_copy(dest_v, dest_hbm.at[pl.ds(ebase, ent_w)])
        bufs, gsems, wsems = (buf0, buf1), (g0, g1), (w0, w1)

        def gather(j):
            return pltpu.make_async_copy(table_hbm.at[tok_v.at[j]], bufs[j % 2], gsems[j % 2])

        def scatter(j):
            return pltpu.make_async_copy(bufs[j % 2], out_hbm.at[dst_v.at[j]], wsems[j % 2])

        _sc_stream(n_chunks, gather, scatter)

    return pl.kernel(
        body,
        out_type=(jax.ShapeDtypeStruct((n_out, w), table.dtype), jax.ShapeDtypeStruct((n_pad,), jnp.int32)),
        mesh=_sc_mesh(),
        scratch_types=([pltpu.VMEM((ent_w,), jnp.int32), pltpu.VMEM((ent_w,), jnp.int32),
                        pltpu.VMEM((n_chunks, chunk), jnp.int32), pltpu.VMEM((n_chunks, chunk), jnp.int32),
                        pltpu.VMEM((LANES,), jnp.int32)] + _sc_buffers(chunk, w, table.dtype)),
        compiler_params=pltpu.CompilerParams(use_tc_tiling_on_sc=True, needs_layout_passes=False),
        name="sc_dispatch",
    )(table, codes, start_row)


def _experts_kernel(piece_start_ref, piece_row_ref, piece_cls_ref, wg_ref, wu_ref, wd_ref, xs_hbm, ys_hbm,
                    wg_bf, wu_bf, wd_bf, xbuf, ybuf, xsem, ysem):
    e = pl.program_id(0)
    g0 = piece_start_ref[e]
    n_here = piece_start_ref[e + 1] - g0
    n_total = piece_start_ref[N_EXPERTS]

    def per_class(g, fn):
        cls = piece_cls_ref[g]
        row = pl.multiple_of(piece_row_ref[g], EXP_UNIT)
        for c in range(1, EXP_CLASSES + 1):
            pl.when(cls == c)(lambda c=c: fn(c * EXP_UNIT, row))

    def x_copy(slot, rows, row):
        return pltpu.make_async_copy(xs_hbm.at[pl.ds(row, rows)], xbuf.at[slot, pl.ds(0, rows)], xsem.at[slot])

    def y_copy(slot, rows, row):
        return pltpu.make_async_copy(ybuf.at[slot, pl.ds(0, rows)], ys_hbm.at[pl.ds(row, rows)], ysem.at[slot])

    @pl.when((e == 0) & (n_total > 0))
    def _():
        per_class(0, lambda rows, row: x_copy(0, rows, row).start())

    wg_bf[...] = wg_ref[0].astype(_BF16)
    wu_bf[...] = wu_ref[0].astype(_BF16)
    wd_bf[...] = wd_ref[0].astype(_BF16)

    def piece(j, carry):
        g = g0 + j
        slot = lax.rem(g, 2)
        per_class(g, lambda rows, row: x_copy(slot, rows, row).wait())

        @pl.when(g + 1 < n_total)
        def _():
            per_class(g + 1, lambda rows, row: x_copy(1 - slot, rows, row).start())

        @pl.when(g >= 2)
        def _():
            per_class(g - 2, lambda rows, row: y_copy(slot, rows, row).wait())

        def compute(rows, row):
            x = _unpack_bf16_pair(xbuf[slot, pl.ds(0, rows)]).astype(_BF16)
            a = _dot(x, wg_bf[...])
            u = _dot(x, wu_bf[...])
            h = (a * jax.nn.sigmoid(a) * u).astype(_BF16)
            ybuf[slot, pl.ds(0, rows)] = _pack_bf16_pair(_dot(h, wd_bf[...]))
            y_copy(slot, rows, row).start()

        per_class(g, compute)
        return carry

    lax.fori_loop(0, n_here, piece, 0)

    @pl.when(e == N_EXPERTS - 1)
    def _():
        @pl.when(n_total >= 2)
        def _():
            per_class(n_total - 2, lambda rows, row: y_copy(lax.rem(n_total, 2), rows, row).wait())

        @pl.when(n_total >= 1)
        def _():
            per_class(n_total - 1, lambda rows, row: y_copy(lax.rem(n_total - 1, 2), rows, row).wait())


def _experts(piece_start, piece_row, piece_cls, n_rows, xs, w_gate, w_up, w_down):
    dh = xs.shape[1]
    d = 2 * dh
    tm = EXP_UNIT * EXP_CLASSES
    anyspec = pl.BlockSpec(memory_space=pl.ANY)
    wsel = lambda e, ps, pr, pc: (e, 0, 0)
    grid_spec = pltpu.PrefetchScalarGridSpec(
        num_scalar_prefetch=3,
        grid=(N_EXPERTS,),
        in_specs=[pl.BlockSpec((1, d, D_EXPERT), wsel), pl.BlockSpec((1, d, D_EXPERT), wsel),
                  pl.BlockSpec((1, D_EXPERT, d), wsel), anyspec],
        out_specs=anyspec,
        scratch_shapes=[pltpu.VMEM((d, D_EXPERT), _BF16), pltpu.VMEM((d, D_EXPERT), _BF16),
                        pltpu.VMEM((D_EXPERT, d), _BF16),
                        pltpu.VMEM((2, tm, dh), _U32), pltpu.VMEM((2, tm, dh), _U32),
                        pltpu.SemaphoreType.DMA((2,)), pltpu.SemaphoreType.DMA((2,))],
    )
    return pl.pallas_call(
        _experts_kernel,
        grid_spec=grid_spec,
        out_shape=jax.ShapeDtypeStruct((n_rows, dh), _U32),
        compiler_params=pltpu.CompilerParams(
            dimension_semantics=("arbitrary",), vmem_limit_bytes=VMEM_LIMIT),
        name="experts",
    )(piece_start, piece_row, piece_cls, w_gate, w_up, w_down, xs)


def _final_kernel(x1_ref, ya_ref, yb_ref, route_t_ref, gf_ref, y_ref):
    tm = x1_ref.shape[0]
    route = jnp.concatenate([route_t_ref[...], jnp.zeros((LANES - SUBLANES, tm), _F32)], axis=0).T
    x2 = (_unpack_bf16_pair(x1_ref[...]) + route[:, R_W1:R_W1 + 1] * _unpack_bf16_pair(ya_ref[...])
          + route[:, R_W2:R_W2 + 1] * _unpack_bf16_pair(yb_ref[...]))
    y_ref[...] = _rms(x2, gf_ref[...])


def _final(x1, yab, route_t, gf, n_prompt, n_sample):
    d = 2 * x1.shape[1]

    def call(tm, first_block, n_rows, name):
        tok = lambda w: pl.BlockSpec((tm, w), lambda i: (first_block + i, 0))
        sel = lambda k: pl.BlockSpec((None, tm, d // 2), lambda i: (k, first_block + i, 0))
        return pl.pallas_call(
            _final_kernel,
            grid=(n_rows // tm,),
            in_specs=[tok(d // 2), sel(0), sel(1),
                      pl.BlockSpec((SUBLANES, tm), lambda i: (0, first_block + i)),
                      pl.BlockSpec((1, d), lambda i: (0, 0))],
            out_specs=pl.BlockSpec((tm, d), lambda i: (i, 0)),
            out_shape=jax.ShapeDtypeStruct((n_rows, d), _F32),
            compiler_params=pltpu.CompilerParams(
                dimension_semantics=("arbitrary",), vmem_limit_bytes=VMEM_LIMIT),
            name=name,
        )(x1, yab, yab, route_t, gf)

    return (call(FINAL_TM, 0, n_prompt, "final_prompt"),
            call(n_sample, n_prompt // n_sample, n_sample, "final_sample"))


def _powers(lam_re, lam_im, dt):
    out = []
    for m in range(SSM_BLK + 1):
        mag = jnp.exp(m * lam_re * dt)
        ang = m * lam_im * dt
        out.append((mag * jnp.cos(ang), mag * jnp.sin(ang)))
    return out


def _spread(x, copies):
    w = x.shape[1]
    src = lax.broadcasted_iota(jnp.int32, (w, w * copies), 0)
    dst = lax.broadcasted_iota(jnp.int32, (w, w * copies), 1)
    return _dot_f32(x, jnp.where(dst % w == src, 1.0, 0.0))


def _ssm_prep_kernel(lam_ref, b_re, b_im, c_re, c_im, v_ref, r_ref, wb_ref, wc_ref, coef_ref):
    n_p, n_h = SSM_STATE, SSM_GROUP
    lr, li, dt = lam_ref[0:1, :], lam_ref[1:2, :], lam_ref[2:3, :]
    pw = _powers(lr, li, dt)
    den = lr * lr + li * li
    nr, ni = pw[1][0] - 1.0, pw[1][1]
    k_re = (nr * lr + ni * li) / den
    k_im = (ni * lr - nr * li) / den
    coef_ref[...] = jnp.concatenate(
        [pw[1][0], pw[1][1], pw[SSM_BLK][0], pw[SSM_BLK][1], jnp.zeros((SUBLANES - 4, TILE_STATE), _F32)], axis=0)

    on_diag_b = (lax.broadcasted_iota(jnp.int32, (TILE_STATE, LANES), 0) // n_p
                 == lax.broadcasted_iota(jnp.int32, (TILE_STATE, LANES), 1) // n_h)
    rows_gp = lambda ref: ref[...].reshape(TILE_STATE, n_h)
    bt_re = jnp.where(on_diag_b, _spread(rows_gp(b_re), SUBLANES), 0.0).T
    bt_im = jnp.where(on_diag_b, _spread(rows_gp(b_im), SUBLANES), 0.0).T
    bb_re = k_re * bt_re - k_im * bt_im
    bb_im = k_re * bt_im + k_im * bt_re
    wb_ref[0] = jnp.concatenate([bb_re, bb_im], axis=1)
    v_rows = []
    for s in range(SSM_BLK):
        pr, pi = pw[SSM_BLK - 1 - s]
        v_rows.append(jnp.concatenate([pr * bb_re - pi * bb_im, pr * bb_im + pi * bb_re], axis=1))
    v_ref[0] = jnp.concatenate(v_rows, axis=0).astype(v_ref.dtype)

    on_diag_c = (lax.broadcasted_iota(jnp.int32, (LANES, TILE_STATE), 0) // n_h
                 == lax.broadcasted_iota(jnp.int32, (LANES, TILE_STATE), 1) // n_p)
    rows_gh = lambda ref: ref[...].reshape(LANES, n_p)
    ct_re = jnp.where(on_diag_c, _spread(rows_gh(c_re), SUBLANES), 0.0)
    ct_im = jnp.where(on_diag_c, _spread(rows_gh(c_im), SUBLANES), 0.0)
    cl = [(ct_re * pr - ct_im * pi, ct_re * pi + ct_im * pr) for pr, pi in pw]
    wc_ref[0] = jnp.concatenate([cl[0][0], -cl[0][1]], axis=1).T
    direct = [_dot_hi(cl[m][0], bb_re, transpose_b=True) - _dot_hi(cl[m][1], bb_im, transpose_b=True)
              for m in range(SSM_BLK)]
    zero = jnp.zeros((LANES, LANES), _F32)
    rt = jnp.concatenate(
        [jnp.concatenate([cl[i + 1][0], -cl[i + 1][1]]
                         + [direct[i - s] if s <= i else zero for s in range(SSM_BLK)], axis=1)
         for i in range(SSM_BLK)], axis=0)
    r_ref[0] = rt.T.astype(r_ref.dtype)


def _ssm_params(lam_re, lam_im, log_dt, b_re, b_im, c_re, c_im, d_skip):
    n_g, n_p, n_h = N_SSM_GROUPS, SSM_STATE, SSM_GROUP
    dt = jnp.repeat(jnp.exp(log_dt), n_p)
    lam = jnp.zeros((SUBLANES, STATE_COLS), _F32).at[0].set(lam_re.reshape(-1)).at[1].set(
        lam_im.reshape(-1)).at[2].set(dt)
    groups = lambda r, c: pl.BlockSpec((SUBLANES, r, c), lambda k: (k, 0, 0))
    out3 = lambda rows, w: pl.BlockSpec((1, rows, w), lambda k: (k, 0, 0))
    cols = pl.BlockSpec((SUBLANES, TILE_STATE), lambda k: (0, k))
    k_blk = SSM_BLK * LANES
    v, r, wb, wc, coef = pl.pallas_call(
        _ssm_prep_kernel,
        grid=(N_LANE_TILES,),
        in_specs=[cols, groups(n_p, n_h), groups(n_p, n_h), groups(n_h, n_p), groups(n_h, n_p)],
        out_specs=[out3(k_blk, 2 * TILE_STATE), out3(2 * TILE_STATE + k_blk, k_blk),
                   out3(LANES, 2 * TILE_STATE), out3(2 * TILE_STATE, LANES), cols],
        out_shape=[jax.ShapeDtypeStruct((N_LANE_TILES, k_blk, 2 * TILE_STATE), _BF16),
                   jax.ShapeDtypeStruct((N_LANE_TILES, 2 * TILE_STATE + k_blk, k_blk), _BF16),
                   jax.ShapeDtypeStruct((N_LANE_TILES, LANES, 2 * TILE_STATE), _F32),
                   jax.ShapeDtypeStruct((N_LANE_TILES, 2 * TILE_STATE, LANES), _F32),
                   jax.ShapeDtypeStruct((SUBLANES, STATE_COLS), _F32)],
        compiler_params=pltpu.CompilerParams(
            dimension_semantics=("arbitrary",), vmem_limit_bytes=VMEM_LIMIT),
        name="ssm_prep",
    )(lam, b_re, b_im, c_re, c_im)
    return wb, wc, v, r, coef, d_skip.reshape(1, D_SSM)


def _dispatch_plan(route_t, cnt):
    t_all = route_t.shape[1]
    codes = route_t[R_CODE1:R_CODE2 + 1].astype(jnp.int32).reshape(-1)
    per_pass = SC_WORKERS * DISPATCH_CHUNK
    codes = jnp.pad(codes, (0, -(2 * t_all) % per_pass))
    counts = cnt[:, 0].astype(jnp.int32)
    zero = jnp.zeros((1,), jnp.int32)
    units = (counts + EXP_UNIT - 1) // EXP_UNIT
    unit_start = jnp.concatenate([zero, jnp.cumsum(units)])
    start_row = jnp.zeros((LANES,), jnp.int32).at[:N_EXPERTS].set(unit_start[:N_EXPERTS] * EXP_UNIT)
    pieces = (units + EXP_CLASSES - 1) // EXP_CLASSES
    piece_start = jnp.concatenate([zero, jnp.cumsum(pieces)])
    tm = EXP_UNIT * EXP_CLASSES
    max_units = (2 * t_all + N_EXPERTS * (EXP_UNIT - 1)) // EXP_UNIT
    max_pieces = (max_units + N_EXPERTS * (EXP_CLASSES - 1)) // EXP_CLASSES
    g = jnp.arange(max_pieces, dtype=jnp.int32)
    owner = ((g[:, None] >= piece_start[None, :-1]) & (g[:, None] < piece_start[None, 1:])).astype(jnp.int32)
    pick = lambda table: jnp.sum(owner * table[None, :], axis=1)
    first_unit = pick(unit_start[:-1]) + (g - pick(piece_start[:-1])) * EXP_CLASSES
    piece_row = first_unit * EXP_UNIT
    piece_cls = jnp.clip(pick(unit_start[1:]) - first_unit, 1, EXP_CLASSES)
    n_rows = (max_units * EXP_UNIT + tm - 1) // tm * tm + tm
    return codes, start_row, n_rows, piece_start, piece_row, piece_cls


def kernel(x_prompt, x_sample, state_ssm_re, state_ssm_im, norm1_g, w_in, lam_re, lam_im, log_dt, ssm_b_re, ssm_b_im, ssm_c_re, ssm_c_im, ssm_d, gmlp_norm_g, gmlp_w_s, gmlp_b_s, out_norm_ssm_g, out_norm_gmlp_g, w_out, norm2_g, w_router_group, b_router_group, w_router_expert, b_router_expert, w_gate, w_up, w_down, final_norm_g):
    n, l, d = x_prompt.shape
    ns = x_sample.shape[0]
    t_all = n * l + ns
    assert w_in.shape[0] == 1 and x_sample.shape[1] == 1 and n == SUBLANES and d == D_MODEL
    li = 0
    g1 = norm1_g[li].reshape(1, d)
    gn = gmlp_norm_g[li].reshape(1, D_GMLP)
    tril = jnp.tril(jnp.ones((CHUNK, CHUNK), dtype=bool))
    ws_tril = jnp.where(tril[None], gmlp_w_s[li], 0.0)
    bs = gmlp_b_s[li]
    gog = out_norm_gmlp_g[li].reshape(1, D_GMLP)
    gos = out_norm_ssm_g[li].reshape(1, D_SSM)
    wb, wc, v_blk, r_blk, coef, dsk = _ssm_params(
        lam_re[li], lam_im[li], log_dt[li], ssm_b_re[li], ssm_b_im[li], ssm_c_re[li], ssm_c_im[li], ssm_d[li])
    g2 = norm2_g[li].reshape(1, d)
    pad = LANES - N_EXPERTS - N_EXPERT_GROUPS
    wr = jnp.concatenate([w_router_expert[li], w_router_group[li], jnp.zeros((d, pad), _F32)], axis=1)
    br = jnp.concatenate([b_router_expert[li], b_router_group[li], jnp.zeros((pad,), _F32)]).reshape(1, LANES)

    xa, sg, mixb = _front_prompt(x_prompt, g1, w_in[li], gn, ws_tril.astype(_BF16), bs.T, gog)
    mixa, hfin = _ssm_prompt(xa, sg, v_blk, r_blk, coef, dsk, gos)
    w00 = jnp.repeat(ws_tril[:, 0, 0], GMLP_HEAD).reshape(1, D_GMLP)
    b0 = jnp.repeat(bs[:, 0], GMLP_HEAD).reshape(1, D_GMLP)
    sample_front = (x_sample.reshape(ns, d), g1, w_in[li], gn, w00, b0, gog, wb, wc, coef, dsk, gos,
                    state_ssm_re[li].reshape(ns, STATE_COLS), state_ssm_im[li].reshape(ns, STATE_COLS))

    x1, xn, route_t, cnt, hr_s, hi_s, vrow = _mixer_out(x_prompt, mixa, mixb, sample_front,
                                                       w_out[li], g2, wr, br)
    codes, start_row, n_rows, piece_start, piece_row, piece_cls = _dispatch_plan(route_t, cnt)
    xs, dest = _sc_dispatch(xn, codes, start_row, n_rows, DISPATCH_CHUNK)
    ys = _experts(piece_start, piece_row, piece_cls, n_rows, xs, w_gate[li], w_up[li], w_down[li])
    yab = _sc_combine(ys, dest, 2 * t_all, COMBINE_CHUNK).reshape(2, t_all, d // 2)
    y_p, y_s = _final(x1, yab, route_t, final_norm_g.reshape(1, d), n * l, ns)

    hf = hfin.reshape(n, N_LANE_TILES, 2, 8, SSM_STATE)
    re_p = hf[:, :, 0].reshape(1, n, N_SSM_GROUPS, SSM_STATE)
    im_p = hf[:, :, 1].reshape(1, n, N_SSM_GROUPS, SSM_STATE)
    re_s = hr_s.reshape(1, ns, N_SSM_GROUPS, SSM_STATE)
    im_s = hi_s.reshape(1, ns, N_SSM_GROUPS, SSM_STATE)
    return (y_p.reshape(n, l, d), y_s.reshape(ns, 1, d), re_p, im_p, re_s, im_s,
            vrow.reshape(1, ns, 1, D_GMLP))
```
